```python
import jax, jax.numpy as jnp
from jax import lax
import numpy as np

D_MODEL = 1024
BATCH = 8
SEQ = 8192
DEPTH = 2

CHUNK = 64
Q_BLOCK = 128
CONV_WIDTH = 31
N_HEADS = 16
HEAD_DIM = D_MODEL // N_HEADS
ATTN_WIDTH = N_HEADS * HEAD_DIM
D_FF = -(-(8 * D_MODEL) // (3 * 256)) * 256
N_A_LAYERS = DEPTH // 2
N_B_LAYERS = DEPTH - N_A_LAYERS
EPS = 1e-6
FORGET_BIAS_MEAN = 2.0

kernel_name = "yoco_conformer_fox_adaln_trunk"


def _rmsnorm(x, g):
    x32 = x.astype(jnp.float32)
    y = x32 * lax.rsqrt(jnp.mean(x32 * x32, axis=-1, keepdims=True) + EPS)
    return (y * g.astype(jnp.float32)).astype(x.dtype)


def _layernorm(x, g, b):
    x32 = x.astype(jnp.float32)
    mu = jnp.mean(x32, axis=-1, keepdims=True)
    var = jnp.mean(jnp.square(x32 - mu), axis=-1, keepdims=True)
    y = (x32 - mu) * lax.rsqrt(var + EPS)
    return (y * g.astype(jnp.float32) + b.astype(jnp.float32)).astype(x.dtype)


def _modulate(h, shift, scale):
    return h * (1.0 + scale[:, None, :]) + shift[:, None, :]


def _conformer_conv(h, w_in, b_in, w_dw, b_dw, ln_g, ln_b, w_out, b_out):
    u = h @ w_in + b_in
    a, g = jnp.split(u, 2, axis=-1)
    u = a * jax.nn.sigmoid(g)
    d = u.shape[-1]
    u = lax.conv_general_dilated(
        u, w_dw[:, None, :].astype(u.dtype),
        window_strides=(1,), padding=[(CONV_WIDTH - 1, 0)],
        dimension_numbers=("NWC", "WIO", "NWC"),
        feature_group_count=d) + b_dw
    u = jax.nn.silu(_layernorm(u, ln_g, ln_b))
    return u @ w_out + b_out


def _forgetting_attention(q, k, v, cum):
    b, s, h, hd = q.shape
    n_blk = s // Q_BLOCK
    scale = hd ** -0.5
    qb = q.reshape(b, n_blk, Q_BLOCK, h, hd).transpose(1, 0, 2, 3, 4)
    cum_h = cum.transpose(0, 2, 1)
    cqb = cum_h.reshape(b, h, n_blk, Q_BLOCK).transpose(2, 0, 1, 3)
    key_pos = jnp.arange(s)

    def one_block(args):
        q_i, cq_i, i = args
        logits = jnp.einsum("bqhd,bkhd->bhqk", q_i, k).astype(jnp.float32) * scale
        logits = logits + (cq_i[..., :, None] - cum_h[:, :, None, :])
        q_pos = i * Q_BLOCK + jnp.arange(Q_BLOCK)
        mask = key_pos[None, :] <= q_pos[:, None]
        logits = jnp.where(mask[None, None], logits, -jnp.inf)
        p = jax.nn.softmax(logits, axis=-1)
        return jnp.einsum("bhqk,bkhd->bqhd", p.astype(v.dtype), v)

    out = lax.map(one_block, (qb, cqb, jnp.arange(n_blk)))
    return out.transpose(1, 0, 2, 3, 4).reshape(b, s, h, hd)


def _fwd_setup_inputs(seed: int = 0) -> dict:
    key = jax.random.key(seed)
    ks = iter(jax.random.split(key, 32))
    D, F, K, H = D_MODEL, D_FF, CONV_WIDTH, N_HEADS

    def nrm(shape, std):
        return jax.random.normal(next(ks), shape, jnp.float32) * std

    def gain(shape):
        return 1.0 + nrm(shape, 0.02)

    return {
        "x": nrm((BATCH, SEQ, D), 1.0),
        "c": nrm((BATCH, D), 1.0),
        "mix_norm_g": gain((DEPTH, D)),
        "mix_ada_w": nrm((DEPTH, D, 3 * D), 0.5 * D ** -0.5),
        "mix_ada_b": nrm((DEPTH, 3 * D), 0.02),
        "ffn_norm_g": gain((DEPTH, D)),
        "ffn_ada_w": nrm((DEPTH, D, 3 * D), 0.5 * D ** -0.5),
        "ffn_ada_b": nrm((DEPTH, 3 * D), 0.02),
        "ffn_w_in": nrm((DEPTH, D, 2 * F), D ** -0.5),
        "ffn_w_out": nrm((DEPTH, F, D), F ** -0.5),
        "conv_w_in": nrm((N_A_LAYERS, D, 2 * D), D ** -0.5),
        "conv_b_in": nrm((N_A_LAYERS, 2 * D), 0.02),
        "conv_w_dw": nrm((N_A_LAYERS, K, D), K ** -0.5),
        "conv_b_dw": nrm((N_A_LAYERS, D), 0.02),
        "conv_ln_g": gain((N_A_LAYERS, D)),
        "conv_ln_b": nrm((N_A_LAYERS, D), 0.02),
        "conv_w_out": nrm((N_A_LAYERS, D, D), D ** -0.5),
        "conv_b_out": nrm((N_A_LAYERS, D), 0.02),
        "kv_norm_g": gain((D,)),
        "kv_ada_w": nrm((D, 2 * D), 0.5 * D ** -0.5),
        "kv_ada_b": nrm((2 * D,), 0.02),
        "kv_w": nrm((D, 2 * ATTN_WIDTH + H), D ** -0.5),
        "forget_b": FORGET_BIAS_MEAN + nrm((H,), 0.1),
        "attn_w_q": nrm((N_B_LAYERS, D, ATTN_WIDTH), D ** -0.5),
        "attn_w_o": nrm((N_B_LAYERS, ATTN_WIDTH, D), ATTN_WIDTH ** -0.5),
        "final_norm_g": gain((D,)),
    }


def _fwd_reference(x, c, mix_norm_g, mix_ada_w, mix_ada_b, ffn_norm_g, ffn_ada_w, ffn_ada_b,
              ffn_w_in, ffn_w_out, conv_w_in, conv_b_in, conv_w_dw, conv_b_dw, conv_ln_g,
              conv_ln_b, conv_w_out, conv_b_out, kv_norm_g, kv_ada_w, kv_ada_b, kv_w,
              forget_b, attn_w_q, attn_w_o, final_norm_g):
    b, s, _ = x.shape
    c_act = jax.nn.silu(c)

    def ada(w, bias, n):
        return jnp.split(c_act @ w + bias, n, axis=-1)

    k_sh = v_sh = cum_sh = None
    for layer in range(DEPTH):
        shift, scale, gate = ada(mix_ada_w[layer], mix_ada_b[layer], 3)
        h = _modulate(_rmsnorm(x, mix_norm_g[layer]), shift, scale)
        if layer < N_A_LAYERS:
            i = layer
            y = _conformer_conv(h, conv_w_in[i], conv_b_in[i], conv_w_dw[i], conv_b_dw[i],
                                conv_ln_g[i], conv_ln_b[i], conv_w_out[i], conv_b_out[i])
        else:
            j = layer - N_A_LAYERS
            q = (h @ attn_w_q[j]).reshape(b, s, N_HEADS, HEAD_DIM)
            o = _forgetting_attention(q, k_sh, v_sh, cum_sh)
            y = o.reshape(b, s, ATTN_WIDTH) @ attn_w_o[j]
        x = x + gate[:, None, :] * y

        shift, scale, gate = ada(ffn_ada_w[layer], ffn_ada_b[layer], 3)
        h = _modulate(_rmsnorm(x, ffn_norm_g[layer]), shift, scale)
        u_gate, u_up = jnp.split(h @ ffn_w_in[layer], 2, axis=-1)
        x = x + gate[:, None, :] * ((jax.nn.silu(u_gate) * u_up) @ ffn_w_out[layer])

        if layer == N_A_LAYERS - 1:
            shift, scale = ada(kv_ada_w, kv_ada_b, 2)
            hk = _modulate(_rmsnorm(x, kv_norm_g), shift, scale)
            kvf = hk @ kv_w
            k_sh = kvf[..., :ATTN_WIDTH].reshape(b, s, N_HEADS, HEAD_DIM)
            v_sh = kvf[..., ATTN_WIDTH:2 * ATTN_WIDTH].reshape(b, s, N_HEADS, HEAD_DIM)
            f_logit = (kvf[..., 2 * ATTN_WIDTH:] + forget_b).astype(jnp.float32)
            cum_sh = jnp.cumsum(jax.nn.log_sigmoid(f_logit), axis=1)

    return _rmsnorm(x, final_norm_g)


import jax as _jax
import jax.numpy as _jnp

TWIN_FORMAT = 'train_step'
FWD_PARAMS = ['x', 'c', 'mix_norm_g', 'mix_ada_w', 'mix_ada_b', 'ffn_norm_g', 'ffn_ada_w', 'ffn_ada_b', 'ffn_w_in', 'ffn_w_out', 'conv_w_in', 'conv_b_in', 'conv_w_dw', 'conv_b_dw', 'conv_ln_g', 'conv_ln_b', 'conv_w_out', 'conv_b_out', 'kv_norm_g', 'kv_ada_w', 'kv_ada_b', 'kv_w', 'forget_b', 'attn_w_q', 'attn_w_o', 'final_norm_g']
TWIN_WEIGHTS = ['mix_norm_g', 'mix_ada_w', 'mix_ada_b', 'ffn_norm_g', 'ffn_ada_w', 'ffn_ada_b', 'ffn_w_in', 'ffn_w_out', 'conv_w_in', 'conv_b_in', 'conv_w_dw', 'conv_b_dw', 'conv_ln_g', 'conv_ln_b', 'conv_w_out', 'conv_b_out', 'kv_norm_g', 'kv_ada_w', 'kv_ada_b', 'kv_w', 'forget_b', 'attn_w_q', 'attn_w_o', 'final_norm_g']
TWIN_DIFF_INPUT = 'x'
TWIN_INPUTS = ['x', 'c', 'mix_norm_g', 'mix_ada_w', 'mix_ada_b', 'ffn_norm_g', 'ffn_ada_w', 'ffn_ada_b', 'ffn_w_in', 'ffn_w_out', 'conv_w_in', 'conv_b_in', 'conv_w_dw', 'conv_b_dw', 'conv_ln_g', 'conv_ln_b', 'conv_w_out', 'conv_b_out', 'kv_norm_g', 'kv_ada_w', 'kv_ada_b', 'kv_w', 'forget_b', 'attn_w_q', 'attn_w_o', 'final_norm_g', 'loss_target', 'm_mix_norm_g', 'm_mix_ada_w', 'm_mix_ada_b', 'm_ffn_norm_g', 'm_ffn_ada_w', 'm_ffn_ada_b', 'm_ffn_w_in', 'm_ffn_w_out', 'm_conv_w_in', 'm_conv_b_in', 'm_conv_w_dw', 'm_conv_b_dw', 'm_conv_ln_g', 'm_conv_ln_b', 'm_conv_w_out', 'm_conv_b_out', 'm_kv_norm_g', 'm_kv_ada_w', 'm_kv_ada_b', 'm_kv_w', 'm_forget_b', 'm_attn_w_q', 'm_attn_w_o', 'm_final_norm_g', 'v_mix_norm_g', 'v_mix_ada_w', 'v_mix_ada_b', 'v_ffn_norm_g', 'v_ffn_ada_w', 'v_ffn_ada_b', 'v_ffn_w_in', 'v_ffn_w_out', 'v_conv_w_in', 'v_conv_b_in', 'v_conv_w_dw', 'v_conv_b_dw', 'v_conv_ln_g', 'v_conv_ln_b', 'v_conv_w_out', 'v_conv_b_out', 'v_kv_norm_g', 'v_kv_ada_w', 'v_kv_ada_b', 'v_kv_w', 'v_forget_b', 'v_attn_w_q', 'v_attn_w_o', 'v_final_norm_g']
TWIN_OUTPUTS = ['loss', 'grad_x', 'grad_mix_norm_g', 'grad_mix_ada_w', 'grad_mix_ada_b', 'grad_ffn_norm_g', 'grad_ffn_ada_w', 'grad_ffn_ada_b', 'grad_ffn_w_in', 'grad_ffn_w_out', 'grad_conv_w_in', 'grad_conv_b_in', 'grad_conv_w_dw', 'grad_conv_b_dw', 'grad_conv_ln_g', 'grad_conv_ln_b', 'grad_conv_w_out', 'grad_conv_b_out', 'grad_kv_norm_g', 'grad_kv_ada_w', 'grad_kv_ada_b', 'grad_kv_w', 'grad_forget_b', 'grad_attn_w_q', 'grad_attn_w_o', 'grad_final_norm_g', 'delta_mix_norm_g', 'delta_mix_ada_w', 'delta_mix_ada_b', 'delta_ffn_norm_g', 'delta_ffn_ada_w', 'delta_ffn_ada_b', 'delta_ffn_w_in', 'delta_ffn_w_out', 'delta_conv_w_in', 'delta_conv_b_in', 'delta_conv_w_dw', 'delta_conv_b_dw', 'delta_conv_ln_g', 'delta_conv_ln_b', 'delta_conv_w_out', 'delta_conv_b_out', 'delta_kv_norm_g', 'delta_kv_ada_w', 'delta_kv_ada_b', 'delta_kv_w', 'delta_forget_b', 'delta_attn_w_q', 'delta_attn_w_o', 'delta_final_norm_g', 'new_m_mix_norm_g', 'new_m_mix_ada_w', 'new_m_mix_ada_b', 'new_m_ffn_norm_g', 'new_m_ffn_ada_w', 'new_m_ffn_ada_b', 'new_m_ffn_w_in', 'new_m_ffn_w_out', 'new_m_conv_w_in', 'new_m_conv_b_in', 'new_m_conv_w_dw', 'new_m_conv_b_dw', 'new_m_conv_ln_g', 'new_m_conv_ln_b', 'new_m_conv_w_out', 'new_m_conv_b_out', 'new_m_kv_norm_g', 'new_m_kv_ada_w', 'new_m_kv_ada_b', 'new_m_kv_w', 'new_m_forget_b', 'new_m_attn_w_q', 'new_m_attn_w_o', 'new_m_final_norm_g', 'new_v_mix_norm_g', 'new_v_mix_ada_w', 'new_v_mix_ada_b', 'new_v_ffn_norm_g', 'new_v_ffn_ada_w', 'new_v_ffn_ada_b', 'new_v_ffn_w_in', 'new_v_ffn_w_out', 'new_v_conv_w_in', 'new_v_conv_b_in', 'new_v_conv_w_dw', 'new_v_conv_b_dw', 'new_v_conv_ln_g', 'new_v_conv_ln_b', 'new_v_conv_w_out', 'new_v_conv_b_out', 'new_v_kv_norm_g', 'new_v_kv_ada_w', 'new_v_kv_ada_b', 'new_v_kv_w', 'new_v_forget_b', 'new_v_attn_w_q', 'new_v_attn_w_o', 'new_v_final_norm_g']
TWIN_LEAF_KINDS = {'loss': 'loss', 'grad_x': 'grad_x', 'grad_mix_norm_g': 'grad_w', 'grad_mix_ada_w': 'grad_w', 'grad_mix_ada_b': 'grad_w', 'grad_ffn_norm_g': 'grad_w', 'grad_ffn_ada_w': 'grad_w', 'grad_ffn_ada_b': 'grad_w', 'grad_ffn_w_in': 'grad_w', 'grad_ffn_w_out': 'grad_w', 'grad_conv_w_in': 'grad_w', 'grad_conv_b_in': 'grad_w', 'grad_conv_w_dw': 'grad_w', 'grad_conv_b_dw': 'grad_w', 'grad_conv_ln_g': 'grad_w', 'grad_conv_ln_b': 'grad_w', 'grad_conv_w_out': 'grad_w', 'grad_conv_b_out': 'grad_w', 'grad_kv_norm_g': 'grad_w', 'grad_kv_ada_w': 'grad_w', 'grad_kv_ada_b': 'grad_w', 'grad_kv_w': 'grad_w', 'grad_forget_b': 'grad_w', 'grad_attn_w_q': 'grad_w', 'grad_attn_w_o': 'grad_w', 'grad_final_norm_g': 'grad_w', 'delta_mix_norm_g': 'delta_w', 'delta_mix_ada_w': 'delta_w', 'delta_mix_ada_b': 'delta_w', 'delta_ffn_norm_g': 'delta_w', 'delta_ffn_ada_w': 'delta_w', 'delta_ffn_ada_b': 'delta_w', 'delta_ffn_w_in': 'delta_w', 'delta_ffn_w_out': 'delta_w', 'delta_conv_w_in': 'delta_w', 'delta_conv_b_in': 'delta_w', 'delta_conv_w_dw': 'delta_w', 'delta_conv_b_dw': 'delta_w', 'delta_conv_ln_g': 'delta_w', 'delta_conv_ln_b': 'delta_w', 'delta_conv_w_out': 'delta_w', 'delta_conv_b_out': 'delta_w', 'delta_kv_norm_g': 'delta_w', 'delta_kv_ada_w': 'delta_w', 'delta_kv_ada_b': 'delta_w', 'delta_kv_w': 'delta_w', 'delta_forget_b': 'delta_w', 'delta_attn_w_q': 'delta_w', 'delta_attn_w_o': 'delta_w', 'delta_final_norm_g': 'delta_w', 'new_m_mix_norm_g': 'new_m', 'new_m_mix_ada_w': 'new_m', 'new_m_mix_ada_b': 'new_m', 'new_m_ffn_norm_g': 'new_m', 'new_m_ffn_ada_w': 'new_m', 'new_m_ffn_ada_b': 'new_m', 'new_m_ffn_w_in': 'new_m', 'new_m_ffn_w_out': 'new_m', 'new_m_conv_w_in': 'new_m', 'new_m_conv_b_in': 'new_m', 'new_m_conv_w_dw': 'new_m', 'new_m_conv_b_dw': 'new_m', 'new_m_conv_ln_g': 'new_m', 'new_m_conv_ln_b': 'new_m', 'new_m_conv_w_out': 'new_m', 'new_m_conv_b_out': 'new_m', 'new_m_kv_norm_g': 'new_m', 'new_m_kv_ada_w': 'new_m', 'new_m_kv_ada_b': 'new_m', 'new_m_kv_w': 'new_m', 'new_m_forget_b': 'new_m', 'new_m_attn_w_q': 'new_m', 'new_m_attn_w_o': 'new_m', 'new_m_final_norm_g': 'new_m', 'new_v_mix_norm_g': 'new_v', 'new_v_mix_ada_w': 'new_v', 'new_v_mix_ada_b': 'new_v', 'new_v_ffn_norm_g': 'new_v', 'new_v_ffn_ada_w': 'new_v', 'new_v_ffn_ada_b': 'new_v', 'new_v_ffn_w_in': 'new_v', 'new_v_ffn_w_out': 'new_v', 'new_v_conv_w_in': 'new_v', 'new_v_conv_b_in': 'new_v', 'new_v_conv_w_dw': 'new_v', 'new_v_conv_b_dw': 'new_v', 'new_v_conv_ln_g': 'new_v', 'new_v_conv_ln_b': 'new_v', 'new_v_conv_w_out': 'new_v', 'new_v_conv_b_out': 'new_v', 'new_v_kv_norm_g': 'new_v', 'new_v_kv_ada_w': 'new_v', 'new_v_kv_ada_b': 'new_v', 'new_v_kv_w': 'new_v', 'new_v_forget_b': 'new_v', 'new_v_attn_w_q': 'new_v', 'new_v_attn_w_o': 'new_v', 'new_v_final_norm_g': 'new_v'}


def _forward(args):
    return _fwd_reference(*[args[k] for k in FWD_PARAMS])


def _output_shape():
    def fwd():
        inp = _fwd_setup_inputs(0)
        return _fwd_reference(*[inp[k] for k in FWD_PARAMS])
    out = _jax.eval_shape(fwd)
    return out.shape, out.dtype

N_MICROBATCH = 1
ADAM_LR = 0.001
ADAM_B1 = 0.9
ADAM_B2 = 0.999
ADAM_EPS = 1e-08
ADAM_WD = 0.01
ADAM_STEP = 10
PER_EXAMPLE_BATCH_AXIS = {'x': 0, 'c': 0, 'loss_target': 0}
SHARED_INPUTS = []
_WEIGHT_DTYPES = {'mix_norm_g': _jnp.float32, 'mix_ada_w': _jnp.float32, 'mix_ada_b': _jnp.float32, 'ffn_norm_g': _jnp.float32, 'ffn_ada_w': _jnp.float32, 'ffn_ada_b': _jnp.float32, 'ffn_w_in': _jnp.float32, 'ffn_w_out': _jnp.float32, 'conv_w_in': _jnp.float32, 'conv_b_in': _jnp.float32, 'conv_w_dw': _jnp.float32, 'conv_b_dw': _jnp.float32, 'conv_ln_g': _jnp.float32, 'conv_ln_b': _jnp.float32, 'conv_w_out': _jnp.float32, 'conv_b_out': _jnp.float32, 'kv_norm_g': _jnp.float32, 'kv_ada_w': _jnp.float32, 'kv_ada_b': _jnp.float32, 'kv_w': _jnp.float32, 'forget_b': _jnp.float32, 'attn_w_q': _jnp.float32, 'attn_w_o': _jnp.float32, 'final_norm_g': _jnp.float32}
MOMENT_SCALE = {'mix_norm_g': 4.211443e-02, 'mix_ada_w': 5.502733e-02, 'mix_ada_b': 9.632894e-02, 'ffn_norm_g': 7.270260e-02, 'ffn_ada_w': 7.350999e-02, 'ffn_ada_b': 1.245682e-01, 'ffn_w_in': 3.196636e-02, 'ffn_w_out': 5.216948e-02, 'conv_w_in': 3.750966e-02, 'conv_b_in': 3.990539e-02, 'conv_w_dw': 4.885489e-02, 'conv_b_dw': 1.051114e-01, 'conv_ln_g': 5.852083e-02, 'conv_ln_b': 5.428048e-02, 'conv_w_out': 4.791286e-02, 'conv_b_out': 8.530933e-02, 'kv_norm_g': 4.553477e-02, 'kv_ada_w': 3.883459e-02, 'kv_ada_b': 6.833176e-02, 'kv_w': 3.545621e-02, 'forget_b': 1.595700e-01, 'attn_w_q': 2.778082e-02, 'attn_w_o': 4.095840e-02, 'final_norm_g': 6.401812e+01}


def _to_microbatches(a, axis):
    t = _jnp.moveaxis(a, axis, 0)
    t = t.reshape((N_MICROBATCH, t.shape[0] // N_MICROBATCH) + t.shape[1:])
    return _jnp.moveaxis(t, 1, axis + 1)


def setup_inputs(seed: int = 0) -> dict:
    inp = _fwd_setup_inputs(seed)
    key = _jax.random.fold_in(_jax.random.key(seed), 7919)
    shape, _ = _output_shape()
    out = dict(inp)
    out["loss_target"] = _jax.random.normal(_jax.random.fold_in(key, 0), shape, _jnp.float32)
    for i, name in enumerate(TWIN_WEIGHTS):
        w = inp[name].astype(_jnp.float32)
        if MOMENT_SCALE is None:
            s = _jnp.sqrt(_jnp.mean(_jnp.square(w)) + 1e-30)
        else:
            s = MOMENT_SCALE[name]
        km, kv = _jax.random.split(_jax.random.fold_in(key, i + 1))
        out[name] = w
        out["m_" + name] = s * _jax.random.normal(km, w.shape, _jnp.float32)
        out["v_" + name] = (s * s) * _jax.random.uniform(kv, w.shape, _jnp.float32, 0.5, 1.5)
    if N_MICROBATCH > 1:
        for name, axis in PER_EXAMPLE_BATCH_AXIS.items():
            out[name] = _to_microbatches(out[name], axis)
    return {'x': out['x'], 'c': out['c'], 'mix_norm_g': out['mix_norm_g'], 'mix_ada_w': out['mix_ada_w'], 'mix_ada_b': out['mix_ada_b'], 'ffn_norm_g': out['ffn_norm_g'], 'ffn_ada_w': out['ffn_ada_w'], 'ffn_ada_b': out['ffn_ada_b'], 'ffn_w_in': out['ffn_w_in'], 'ffn_w_out': out['ffn_w_out'], 'conv_w_in': out['conv_w_in'], 'conv_b_in': out['conv_b_in'], 'conv_w_dw': out['conv_w_dw'], 'conv_b_dw': out['conv_b_dw'], 'conv_ln_g': out['conv_ln_g'], 'conv_ln_b': out['conv_ln_b'], 'conv_w_out': out['conv_w_out'], 'conv_b_out': out['conv_b_out'], 'kv_norm_g': out['kv_norm_g'], 'kv_ada_w': out['kv_ada_w'], 'kv_ada_b': out['kv_ada_b'], 'kv_w': out['kv_w'], 'forget_b': out['forget_b'], 'attn_w_q': out['attn_w_q'], 'attn_w_o': out['attn_w_o'], 'final_norm_g': out['final_norm_g'], 'loss_target': out['loss_target'], 'm_mix_norm_g': out['m_mix_norm_g'], 'm_mix_ada_w': out['m_mix_ada_w'], 'm_mix_ada_b': out['m_mix_ada_b'], 'm_ffn_norm_g': out['m_ffn_norm_g'], 'm_ffn_ada_w': out['m_ffn_ada_w'], 'm_ffn_ada_b': out['m_ffn_ada_b'], 'm_ffn_w_in': out['m_ffn_w_in'], 'm_ffn_w_out': out['m_ffn_w_out'], 'm_conv_w_in': out['m_conv_w_in'], 'm_conv_b_in': out['m_conv_b_in'], 'm_conv_w_dw': out['m_conv_w_dw'], 'm_conv_b_dw': out['m_conv_b_dw'], 'm_conv_ln_g': out['m_conv_ln_g'], 'm_conv_ln_b': out['m_conv_ln_b'], 'm_conv_w_out': out['m_conv_w_out'], 'm_conv_b_out': out['m_conv_b_out'], 'm_kv_norm_g': out['m_kv_norm_g'], 'm_kv_ada_w': out['m_kv_ada_w'], 'm_kv_ada_b': out['m_kv_ada_b'], 'm_kv_w': out['m_kv_w'], 'm_forget_b': out['m_forget_b'], 'm_attn_w_q': out['m_attn_w_q'], 'm_attn_w_o': out['m_attn_w_o'], 'm_final_norm_g': out['m_final_norm_g'], 'v_mix_norm_g': out['v_mix_norm_g'], 'v_mix_ada_w': out['v_mix_ada_w'], 'v_mix_ada_b': out['v_mix_ada_b'], 'v_ffn_norm_g': out['v_ffn_norm_g'], 'v_ffn_ada_w': out['v_ffn_ada_w'], 'v_ffn_ada_b': out['v_ffn_ada_b'], 'v_ffn_w_in': out['v_ffn_w_in'], 'v_ffn_w_out': out['v_ffn_w_out'], 'v_conv_w_in': out['v_conv_w_in'], 'v_conv_b_in': out['v_conv_b_in'], 'v_conv_w_dw': out['v_conv_w_dw'], 'v_conv_b_dw': out['v_conv_b_dw'], 'v_conv_ln_g': out['v_conv_ln_g'], 'v_conv_ln_b': out['v_conv_ln_b'], 'v_conv_w_out': out['v_conv_w_out'], 'v_conv_b_out': out['v_conv_b_out'], 'v_kv_norm_g': out['v_kv_norm_g'], 'v_kv_ada_w': out['v_kv_ada_w'], 'v_kv_ada_b': out['v_kv_ada_b'], 'v_kv_w': out['v_kv_w'], 'v_forget_b': out['v_forget_b'], 'v_attn_w_q': out['v_attn_w_q'], 'v_attn_w_o': out['v_attn_w_o'], 'v_final_norm_g': out['v_final_norm_g']}


def _loss(weights, diff, rest, loss_target):
    with _jax.named_scope("forward"):
        args = {**rest, TWIN_DIFF_INPUT: diff, **{k: w.astype(_WEIGHT_DTYPES[k]) for k, w in weights.items()}}
        y = _forward(args)
    with _jax.named_scope("loss_head"):
        err = _jnp.square(y.astype(_jnp.float32) - loss_target)
        return 0.5 * _jnp.sum(_jnp.mean(err, axis=-1)) if err.ndim else 0.5 * err


def _adamw(w, g, m, v):
    m = ADAM_B1 * m + (1.0 - ADAM_B1) * g
    v = ADAM_B2 * v + (1.0 - ADAM_B2) * _jnp.square(g)
    m_hat = m / (1.0 - ADAM_B1 ** ADAM_STEP)
    v_hat = v / (1.0 - ADAM_B2 ** ADAM_STEP)
    delta = -ADAM_LR * (m_hat / (_jnp.sqrt(v_hat) + ADAM_EPS) + ADAM_WD * w)
    return delta, m, v


def reference(x, c, mix_norm_g, mix_ada_w, mix_ada_b, ffn_norm_g, ffn_ada_w, ffn_ada_b, ffn_w_in, ffn_w_out, conv_w_in, conv_b_in, conv_w_dw, conv_b_dw, conv_ln_g, conv_ln_b, conv_w_out, conv_b_out, kv_norm_g, kv_ada_w, kv_ada_b, kv_w, forget_b, attn_w_q, attn_w_o, final_norm_g, loss_target, m_mix_norm_g, m_mix_ada_w, m_mix_ada_b, m_ffn_norm_g, m_ffn_ada_w, m_ffn_ada_b, m_ffn_w_in, m_ffn_w_out, m_conv_w_in, m_conv_b_in, m_conv_w_dw, m_conv_b_dw, m_conv_ln_g, m_conv_ln_b, m_conv_w_out, m_conv_b_out, m_kv_norm_g, m_kv_ada_w, m_kv_ada_b, m_kv_w, m_forget_b, m_attn_w_q, m_attn_w_o, m_final_norm_g, v_mix_norm_g, v_mix_ada_w, v_mix_ada_b, v_ffn_norm_g, v_ffn_ada_w, v_ffn_ada_b, v_ffn_w_in, v_ffn_w_out, v_conv_w_in, v_conv_b_in, v_conv_w_dw, v_conv_b_dw, v_conv_ln_g, v_conv_ln_b, v_conv_w_out, v_conv_b_out, v_kv_norm_g, v_kv_ada_w, v_kv_ada_b, v_kv_w, v_forget_b, v_attn_w_q, v_attn_w_o, v_final_norm_g):
    given = dict(x=x, c=c, mix_norm_g=mix_norm_g, mix_ada_w=mix_ada_w, mix_ada_b=mix_ada_b, ffn_norm_g=ffn_norm_g, ffn_ada_w=ffn_ada_w, ffn_ada_b=ffn_ada_b, ffn_w_in=ffn_w_in, ffn_w_out=ffn_w_out, conv_w_in=conv_w_in, conv_b_in=conv_b_in, conv_w_dw=conv_w_dw, conv_b_dw=conv_b_dw, conv_ln_g=conv_ln_g, conv_ln_b=conv_ln_b, conv_w_out=conv_w_out, conv_b_out=conv_b_out, kv_norm_g=kv_norm_g, kv_ada_w=kv_ada_w, kv_ada_b=kv_ada_b, kv_w=kv_w, forget_b=forget_b, attn_w_q=attn_w_q, attn_w_o=attn_w_o, final_norm_g=final_norm_g, loss_target=loss_target, m_mix_norm_g=m_mix_norm_g, m_mix_ada_w=m_mix_ada_w, m_mix_ada_b=m_mix_ada_b, m_ffn_norm_g=m_ffn_norm_g, m_ffn_ada_w=m_ffn_ada_w, m_ffn_ada_b=m_ffn_ada_b, m_ffn_w_in=m_ffn_w_in, m_ffn_w_out=m_ffn_w_out, m_conv_w_in=m_conv_w_in, m_conv_b_in=m_conv_b_in, m_conv_w_dw=m_conv_w_dw, m_conv_b_dw=m_conv_b_dw, m_conv_ln_g=m_conv_ln_g, m_conv_ln_b=m_conv_ln_b, m_conv_w_out=m_conv_w_out, m_conv_b_out=m_conv_b_out, m_kv_norm_g=m_kv_norm_g, m_kv_ada_w=m_kv_ada_w, m_kv_ada_b=m_kv_ada_b, m_kv_w=m_kv_w, m_forget_b=m_forget_b, m_attn_w_q=m_attn_w_q, m_attn_w_o=m_attn_w_o, m_final_norm_g=m_final_norm_g, v_mix_norm_g=v_mix_norm_g, v_mix_ada_w=v_mix_ada_w, v_mix_ada_b=v_mix_ada_b, v_ffn_norm_g=v_ffn_norm_g, v_ffn_ada_w=v_ffn_ada_w, v_ffn_ada_b=v_ffn_ada_b, v_ffn_w_in=v_ffn_w_in, v_ffn_w_out=v_ffn_w_out, v_conv_w_in=v_conv_w_in, v_conv_b_in=v_conv_b_in, v_conv_w_dw=v_conv_w_dw, v_conv_b_dw=v_conv_b_dw, v_conv_ln_g=v_conv_ln_g, v_conv_ln_b=v_conv_ln_b, v_conv_w_out=v_conv_w_out, v_conv_b_out=v_conv_b_out, v_kv_norm_g=v_kv_norm_g, v_kv_ada_w=v_kv_ada_w, v_kv_ada_b=v_kv_ada_b, v_kv_w=v_kv_w, v_forget_b=v_forget_b, v_attn_w_q=v_attn_w_q, v_attn_w_o=v_attn_w_o, v_final_norm_g=v_final_norm_g)
    weights = {n: given[n] for n in TWIN_WEIGHTS}
    shared = {n: given[n] for n in SHARED_INPUTS}
    per_example = {n: given[n] for n in ['x', 'c']}
    grad_fn = _jax.value_and_grad(_loss, argnums=(0, 1))

    def one_microbatch(ex, loss_target):
        ex = dict(ex)
        diff = ex.pop(TWIN_DIFF_INPUT)
        return grad_fn(weights, diff, {**shared, **ex}, loss_target)

    if N_MICROBATCH == 1:
        loss, (grad_w, grad_x) = one_microbatch(per_example, given["loss_target"])
    else:
        def body(carry, xs):
            loss_sum, grad_sum = carry
            l_k, (gw_k, gx_k) = one_microbatch(xs[0], xs[1])
            with _jax.named_scope("update"):
                return (loss_sum + l_k, _jax.tree.map(_jnp.add, grad_sum, gw_k)), gx_k

        init = (_jnp.zeros((), _jnp.float32), _jax.tree.map(_jnp.zeros_like, weights))
        (loss, grad_w), grad_x = _jax.lax.scan(body, init, (per_example, given["loss_target"]))
    with _jax.named_scope("update"):
        delta_w, new_m, new_v = {}, {}, {}
        for n in TWIN_WEIGHTS:
            delta_w[n], new_m[n], new_v[n] = _adamw(weights[n], grad_w[n], given["m_" + n], given["v_" + n])
    return (loss, grad_x, *[grad_w[n] for n in TWIN_WEIGHTS], *[delta_w[n] for n in TWIN_WEIGHTS],
            *[new_m[n] for n in TWIN_WEIGHTS], *[new_v[n] for n in TWIN_WEIGHTS])
```

```python
import functools

import jax
import jax.numpy as jnp
from jax import lax
from jax.experimental import pallas as pl
from jax.experimental.pallas import tpu as pltpu

F32 = jnp.float32
BF16 = jnp.bfloat16
MESH = pl.DeviceIdType.MESH

EPS = 1e-6
N_HEADS = 16
HEAD_DIM = 64
CONV_K = 31
LANE = 128
HALO = 32
ATT_T = 256
VMEM_MB = 48

ADAM_LR = 0.001
ADAM_B1 = 0.9
ADAM_B2 = 0.999
ADAM_EPS = 1e-08
ADAM_WD = 0.01
ADAM_STEP = 10


def _sds(shape, dtype):
    return jax.ShapeDtypeStruct(tuple(shape), dtype)


def _cp(sem=None, vmem_mb=VMEM_MB):
    return pltpu.CompilerParams(dimension_semantics=sem, vmem_limit_bytes=vmem_mb << 20)


def _tile(n, pref):
    return pref if n % pref == 0 else n


def _row_tile(r):
    for cand in range(512, 7, -8):
        if r % cand == 0:
            return cand
    return r


def _resident(shape):
    nd = len(shape)
    return pl.BlockSpec(tuple(shape), lambda *_: (0,) * nd, pipeline_mode=pl.Buffered(1))


def _dot(a, b):
    return jnp.dot(a, b, preferred_element_type=F32)


def _dot_nt(a, b):
    return lax.dot_general(a, b, (((1,), (1,)), ((), ())), preferred_element_type=F32)


def _dot_tn(a, b):
    return lax.dot_general(a, b, (((0,), (0,)), ((), ())), preferred_element_type=F32)


def _sigmoid(x):
    return 1.0 / (1.0 + jnp.exp(-x))


def _colsum(x):
    return jnp.sum(x, axis=0, keepdims=True)


def _rms_parts(x):
    rstd = lax.rsqrt(jnp.mean(x * x, axis=-1, keepdims=True) + EPS)
    return x * rstd, rstd


def _allgather8(x_shard, name, in_vmem):
    m_per, n = x_shard.shape

    def body(x_ref, out_ref, send_sems, recv_sems, local_sem):
        x, y, c = lax.axis_index("x"), lax.axis_index("y"), lax.axis_index("c")
        me, sibling = (x, y, c), (x, y, 1 - c)
        chips = [(1 - x, y), (x, 1 - y), (1 - x, 1 - y)]

        def rows(px, py, pc):
            return out_ref.at[pl.ds((4 * px + 2 * py + pc) * m_per, m_per), :]

        def copy(k, block, to, src=None):
            return pltpu.make_async_remote_copy(
                src_ref=rows(*block) if src is None else src, dst_ref=rows(*block),
                send_sem=send_sems.at[k], recv_sem=recv_sems.at[k], device_id=to, device_id_type=MESH)

        mine = pltpu.make_async_copy(x_ref, rows(*me), local_sem)
        mine.start()
        first = [copy(0, me, sibling, src=x_ref)]
        first += [copy(1 + j, me, (*chip, c), src=x_ref) for j, chip in enumerate(chips)]
        for cp in first:
            cp.start()
        passed = [copy(4 + j, (*chip, c), sibling) for j, chip in enumerate(chips)]
        for j, chip in enumerate(chips):
            copy(1 + j, (*chip, c), me).wait_recv()
            passed[j].start()
        copy(0, sibling, me).wait_recv()
        for j, chip in enumerate(chips):
            copy(4 + j, (*chip, 1 - c), me).wait_recv()
        for cp in first + passed:
            cp.wait_send()
        mine.wait()

    space = pltpu.VMEM if in_vmem else pl.ANY
    return pl.pallas_call(
        body, out_shape=_sds((8 * m_per, n), x_shard.dtype),
        in_specs=[pl.BlockSpec(memory_space=space)], out_specs=pl.BlockSpec(memory_space=space),
        scratch_shapes=[pltpu.SemaphoreType.DMA((7,)), pltpu.SemaphoreType.DMA((7,)), pltpu.SemaphoreType.DMA],
        name=name)(x_shard)


def _swap_halves(p):
    nb, _, r, w = p.shape

    def body(p_ref, land_ref, send_sems, recv_sems):
        x, y, c = lax.axis_index("x"), lax.axis_index("y"), lax.axis_index("c")
        copies = [pltpu.make_async_remote_copy(
            src_ref=p_ref.at[j, 1 - c], dst_ref=land_ref.at[j], send_sem=send_sems.at[j], recv_sem=recv_sems.at[j],
            device_id=(x, y, 1 - c), device_id_type=MESH) for j in range(nb)]
        for cp in copies:
            cp.start()
        for cp in copies:
            cp.wait_recv()
        for cp in copies:
            cp.wait_send()

    return pl.pallas_call(
        body, out_shape=_sds((nb, r, w), p.dtype),
        in_specs=[pl.BlockSpec(memory_space=pl.ANY)], out_specs=pl.BlockSpec(memory_space=pl.ANY),
        scratch_shapes=[pltpu.SemaphoreType.DMA((nb,)), pltpu.SemaphoreType.DMA((nb,))],
        name="rs_swap_halves")(p)


def _scatter_chips(q):
    _, r, w = q.shape

    def body(q_ref, land_ref, send_sems, recv_sems):
        x, y, c = lax.axis_index("x"), lax.axis_index("y"), lax.axis_index("c")
        chips = [(1 - x, y), (x, 1 - y), (1 - x, 1 - y)]
        copies = [pltpu.make_async_remote_copy(
            src_ref=q_ref.at[2 * cx + cy], dst_ref=land_ref.at[k], send_sem=send_sems.at[k], recv_sem=recv_sems.at[k],
            device_id=(cx, cy, c), device_id_type=MESH) for k, (cx, cy) in enumerate(chips)]
        for cp in copies:
            cp.start()
        for cp in copies:
            cp.wait_recv()
        for cp in copies:
            cp.wait_send()

    return pl.pallas_call(
        body, out_shape=_sds((3, r, w), q.dtype),
        in_specs=[pl.BlockSpec(memory_space=pl.ANY)], out_specs=pl.BlockSpec(memory_space=pl.ANY),
        scratch_shapes=[pltpu.SemaphoreType.DMA((3,)), pltpu.SemaphoreType.DMA((3,))],
        name="rs_scatter_chips")(q)


def _share_half(rsum):
    r, w = rsum.shape

    def body(r_ref, out_ref, send_sem, recv_sem, local_sem):
        x, y, c = lax.axis_index("x"), lax.axis_index("y"), lax.axis_index("c")
        mine = pltpu.make_async_copy(r_ref, out_ref.at[c], local_sem)
        mine.start()
        cp = pltpu.make_async_remote_copy(
            src_ref=r_ref, dst_ref=out_ref.at[c], send_sem=send_sem, recv_sem=recv_sem,
            device_id=(x, y, 1 - c), device_id_type=MESH)
        cp.start()
        cp.wait_recv()
        cp.wait_send()
        mine.wait()

    return pl.pallas_call(
        body, out_shape=_sds((2, r, w), rsum.dtype),
        in_specs=[pl.BlockSpec(memory_space=pl.ANY)], out_specs=pl.BlockSpec(memory_space=pl.ANY),
        scratch_shapes=[pltpu.SemaphoreType.DMA, pltpu.SemaphoreType.DMA, pltpu.SemaphoreType.DMA],
        name="rs_share_half")(rsum)


def _add_halves(p, land, core):
    nb, _, r, w = p.shape
    tr = _row_tile(r)

    def body(core_ref, p_ref, l_ref, o_ref):
        o_ref[0] = p_ref[0, 0] + l_ref[0]

    gs = pltpu.PrefetchScalarGridSpec(
        num_scalar_prefetch=1, grid=(nb, r // tr),
        in_specs=[pl.BlockSpec((1, 1, tr, w), lambda j, i, cr: (j, cr[0], i, 0)),
                  pl.BlockSpec((1, tr, w), lambda j, i, cr: (j, i, 0))],
        out_specs=pl.BlockSpec((1, tr, w), lambda j, i, cr: (j, i, 0)))
    return pl.pallas_call(body, grid_spec=gs, out_shape=_sds((nb, r, w), p.dtype), name="rs_add_halves",
                          compiler_params=_cp(("parallel", "parallel")))(core, p, land)


def _add_chips(q, land, chip):
    _, r, w = q.shape
    tr = _row_tile(r)

    def body(chip_ref, q_ref, l_ref, o_ref):
        o_ref[...] = ((q_ref[0] + l_ref[0]) + l_ref[1]) + l_ref[2]

    gs = pltpu.PrefetchScalarGridSpec(
        num_scalar_prefetch=1, grid=(r // tr,),
        in_specs=[pl.BlockSpec((1, tr, w), lambda i, ch: (ch[0], i, 0)),
                  pl.BlockSpec((3, tr, w), lambda i, ch: (0, i, 0))],
        out_specs=pl.BlockSpec((tr, w), lambda i, ch: (i, 0)))
    return pl.pallas_call(body, grid_spec=gs, out_shape=_sds((r, w), q.dtype), name="rs_add_chips",
                          compiler_params=_cp(("parallel",)))(chip, q, land)


def _sum8(g):
    _, m, n = g.shape

    def body(g_ref, o_ref):
        acc = g_ref[0]
        for k in range(1, 8):
            acc = acc + g_ref[k]
        o_ref[...] = acc

    return pl.pallas_call(body, out_shape=_sds((m, n), g.dtype), name="sum8")(g)


def _ada_fwd(c_all, w3, name):
    nl, d, n = w3.shape
    tn = 256

    def body(c_ref, w_ref, o_ref):
        cc = c_ref[...]
        ca = (cc * _sigmoid(cc)).astype(BF16)
        o_ref[0] = _dot(ca, w_ref[0].astype(BF16))

    return pl.pallas_call(
        body, grid=(nl, n // tn), out_shape=_sds((nl, 8, n), F32),
        in_specs=[pl.BlockSpec((8, d), lambda l, j: (0, 0)), pl.BlockSpec((1, d, tn), lambda l, j: (l, 0, j))],
        out_specs=pl.BlockSpec((1, 8, tn), lambda l, j: (l, 0, j)),
        name=name, compiler_params=_cp(("parallel", "parallel")))(c_all, w3)


def _in_pair(x, gain, shift, scale, wg, bias, conv, name):
    s, d = x.shape
    n = wg.shape[2]
    ts = _tile(s, 512)

    def body(*refs):
        if conv:
            x_ref, g_ref, sh_ref, sc_ref, wa_ref, wb_ref, ba_ref, bb_ref, h_ref, o_ref, sa_ref, sb_ref, hs = refs
        else:
            x_ref, g_ref, sh_ref, sc_ref, wa_ref, wb_ref, h_ref, o_ref, sa_ref, sb_ref, hs = refs

        @pl.when(pl.program_id(1) == 0)
        def _():
            xhat, _ = _rms_parts(x_ref[...])
            h = (xhat * g_ref[...]) * (1.0 + sc_ref[...]) + sh_ref[...]
            hs[...] = h.astype(BF16)
            h_ref[...] = hs[...]

        h = hs[...]
        a = _dot(h, wa_ref[0])
        b = _dot(h, wb_ref[0])
        if conv:
            a = a + ba_ref[0]
            b = b + bb_ref[0]
            o_ref[...] = a * _sigmoid(b)
        else:
            o_ref[...] = (a * _sigmoid(a) * b).astype(BF16)
        sa_ref[...] = a.astype(BF16)
        sb_ref[...] = b.astype(BF16)

    vec = pl.BlockSpec((1, d), lambda i, q: (0, 0))
    in_specs = [pl.BlockSpec((ts, d), lambda i, q: (i, 0)), vec, vec, vec,
                pl.BlockSpec((1, d, n), lambda i, q: (q, 0, 0)), pl.BlockSpec((1, d, n), lambda i, q: (q + 2, 0, 0))]
    args = [x, gain, shift, scale, wg, wg]
    if conv:
        in_specs += [pl.BlockSpec((1, 1, n), lambda i, q: (q, 0, 0)), pl.BlockSpec((1, 1, n), lambda i, q: (q + 2, 0, 0))]
        args += [bias, bias]
    tile = pl.BlockSpec((ts, n), lambda i, q: (i, q))
    return pl.pallas_call(
        body, grid=(s // ts, 2),
        out_shape=(_sds((s, d), BF16), _sds((s, 2 * n), F32 if conv else BF16), _sds((s, 2 * n), BF16), _sds((s, 2 * n), BF16)),
        in_specs=in_specs, out_specs=(pl.BlockSpec((ts, d), lambda i, q: (i, 0)), tile, tile, tile),
        scratch_shapes=[pltpu.VMEM((ts, d), BF16)],
        name=name, compiler_params=_cp(("parallel", "arbitrary")))(*args)


def _dwconv_fwd(glu, wdw, bdw, lng, lnb):
    s, d = glu.shape
    ts = _tile(s, 256)
    rb, cb = 32, 256

    def body(cur_ref, halo_ref, w_ref, b_ref, g_ref, be_ref, dwo_ref, sw_ref, buf):
        i = pl.program_id(0)

        @pl.when(i == 0)
        def _():
            buf[pl.ds(0, HALO), :] = jnp.zeros((HALO, d), F32)

        @pl.when(i > 0)
        def _():
            buf[pl.ds(0, HALO), :] = halo_ref[...]

        buf[pl.ds(HALO, ts), :] = cur_ref[...]
        for r in range(ts // rb):
            for cc in range(d // cb):
                cs = pl.ds(cc * cb, cb)
                acc = jnp.zeros((rb, cb), F32) + b_ref[:, cs]
                for k in range(CONV_K):
                    acc = acc + w_ref[pl.ds(k, 1), cs] * buf[pl.ds(HALO - (CONV_K - 1) + k + r * rb, rb), cs]
                dwo_ref[pl.ds(r * rb, rb), cs] = acc
            rows = pl.ds(r * rb, rb)
            yv = dwo_ref[rows, :]
            mu = jnp.mean(yv, axis=-1, keepdims=True)
            yc = yv - mu
            var = jnp.mean(yc * yc, axis=-1, keepdims=True)
            ln = yc * lax.rsqrt(var + EPS) * g_ref[...] + be_ref[...]
            sw_ref[rows, :] = (ln * _sigmoid(ln)).astype(BF16)

    vec = pl.BlockSpec((1, d), lambda i: (0, 0))
    return pl.pallas_call(
        body, grid=(s // ts,), out_shape=(_sds((s, d), F32), _sds((s, d), BF16)),
        in_specs=[pl.BlockSpec((ts, d), lambda i: (i, 0)),
                  pl.BlockSpec((HALO, d), lambda i: (jnp.maximum(i * (ts // HALO) - 1, 0), 0)),
                  pl.BlockSpec((HALO, d), lambda i: (0, 0)), vec, vec, vec],
        out_specs=(pl.BlockSpec((ts, d), lambda i: (i, 0)), pl.BlockSpec((ts, d), lambda i: (i, 0))),
        scratch_shapes=[pltpu.VMEM((HALO + ts, d), F32)],
        name="dwconv_fwd", compiler_params=_cp(("parallel",)))(glu, glu, wdw, bdw, lng, lnb)


def _mm_res(a, w, b, gate, x, name):
    s, k = a.shape
    d = w.shape[1]
    ts = _tile(s, 512)

    def body(a_ref, w_ref, b_ref, g_ref, x_ref, o_ref):
        yv = _dot(a_ref[...], w_ref[...]) + b_ref[...]
        o_ref[...] = x_ref[...] + g_ref[...] * yv

    vec = pl.BlockSpec((1, d), lambda i: (0, 0))
    return pl.pallas_call(
        body, grid=(s // ts,), out_shape=_sds((s, d), F32),
        in_specs=[pl.BlockSpec((ts, k), lambda i: (i, 0)), _resident((k, d)), vec, vec, pl.BlockSpec((ts, d), lambda i: (i, 0))],
        out_specs=pl.BlockSpec((ts, d), lambda i: (i, 0)),
        name=name, compiler_params=_cp(("parallel",)))(a, w, b, gate, x)


def _qkv(x, kvp, mxp, wk, wv, wf, wq):
    s, d = x.shape
    ts = _tile(s, 512)
    qscale = HEAD_DIM ** -0.5

    def body(x_ref, gk, shk, sck, gm, shm, scm, wk_ref, wv_ref, wf_ref, wq_ref, hk_ref, h1_ref, k_ref, v_ref, q_ref, f_ref):
        xhat, _ = _rms_parts(x_ref[...])
        hk = ((xhat * gk[...]) * (1.0 + sck[...]) + shk[...]).astype(BF16)
        h1 = ((xhat * gm[...]) * (1.0 + scm[...]) + shm[...]).astype(BF16)
        hk_ref[...] = hk
        h1_ref[...] = h1
        k_ref[...] = _dot(hk, wk_ref[...]).astype(BF16)
        v_ref[...] = _dot(hk, wv_ref[...]).astype(BF16)
        f_ref[...] = _dot(hk, wf_ref[...])
        q_ref[...] = (_dot(h1, wq_ref[...]) * qscale).astype(BF16)

    vec = pl.BlockSpec((1, d), lambda i: (0, 0))
    row = pl.BlockSpec((ts, d), lambda i: (i, 0))
    return pl.pallas_call(
        body, grid=(s // ts,),
        out_shape=tuple(_sds((s, d), BF16) for _ in range(5)) + (_sds((s, LANE), F32),),
        in_specs=[row, vec, vec, vec, vec, vec, vec, _resident((d, d)), _resident((d, d)), _resident((d, LANE)), _resident((d, d))],
        out_specs=(row, row, row, row, row, pl.BlockSpec((ts, LANE), lambda i: (i, 0))),
        name="qkv_proj", compiler_params=_cp(("parallel",)))(x, *kvp, *mxp, wk, wv, wf, wq)


def _log_sigmoid(z):
    return jnp.minimum(z, 0.0) - jnp.log(1.0 + jnp.exp(-jnp.abs(z)))


def _cumsum_fwd(flog, fb):
    s = flog.shape[0]
    ts = _tile(s, 256)

    def body(f_ref, b_ref, cum_ref, cumt_ref, carry):
        @pl.when(pl.program_id(0) == 0)
        def _():
            carry[...] = jnp.zeros_like(carry)

        ls = _log_sigmoid(f_ref[...] + b_ref[...])
        r = lax.broadcasted_iota(jnp.int32, (ts, ts), 0)
        cidx = lax.broadcasted_iota(jnp.int32, (ts, ts), 1)
        tri = (cidx <= r).astype(F32)
        cs = jnp.dot(tri, ls, preferred_element_type=F32, precision=lax.Precision.HIGHEST) + carry[...]
        cum_ref[...] = cs
        cumt_ref[...] = cs.T
        carry[...] = cs[ts - 1:ts, :]

    return pl.pallas_call(
        body, grid=(s // ts,), out_shape=(_sds((s, LANE), F32), _sds((LANE, s), F32)),
        in_specs=[pl.BlockSpec((ts, LANE), lambda i: (i, 0)), pl.BlockSpec((1, LANE), lambda i: (0, 0))],
        out_specs=(pl.BlockSpec((ts, LANE), lambda i: (i, 0)), pl.BlockSpec((LANE, ts), lambda i: (0, i))),
        scratch_shapes=[pltpu.VMEM((1, LANE), F32)],
        name="forget_cumsum", compiler_params=_cp(("arbitrary",)))(flog, fb)


def _pick_row(m, idx):
    r = lax.broadcasted_iota(jnp.int32, (m.shape[0], 1), 0)
    return jnp.sum(jnp.where(r == idx, m, 0.0), axis=0, keepdims=True)


def _pick_col(m, idx):
    cidx = lax.broadcasted_iota(jnp.int32, (1, m.shape[1]), 1)
    return jnp.sum(jnp.where(cidx == idx, m, 0.0), axis=1, keepdims=True)


def _attn_fwd(q, k, v, cum, cumt):
    s, d = q.shape
    t = _tile(s, ATT_T)
    npair = d // LANE

    def body(q_ref, k_ref, v_ref, cum_ref, cumt_ref, o_ref, lse_ref):
        p = pl.program_id(0)
        i = pl.program_id(1)
        lane = lax.broadcasted_iota(jnp.int32, (1, LANE), 1)
        lo = lane < HEAD_DIM
        qq = q_ref[...]
        zero = jnp.zeros_like(qq)
        qh = (jnp.where(lo, qq, zero), jnp.where(lo, zero, qq))
        q0 = pl.multiple_of(i * t, t)
        cqt = cumt_ref[:, pl.ds(q0, t)]
        cq = (_pick_row(cqt, 2 * p), _pick_row(cqt, 2 * p + 1))
        krow = lax.broadcasted_iota(jnp.int32, (t, t), 0)
        qcol = lax.broadcasted_iota(jnp.int32, (t, t), 1)
        causal = krow <= qcol
        one = jnp.ones((1, LANE), BF16)
        zl = jnp.zeros((1, LANE), BF16)

        def kv_step(j, carry, diag):
            ks = pl.multiple_of(j * t, t)
            kk = k_ref[pl.ds(ks, t), :]
            vv = v_ref[pl.ds(ks, t), :]
            ckt = cum_ref[pl.ds(ks, t), :]
            vh = (jnp.where(lo, vv, jnp.where(lane == HEAD_DIM, one, zl)),
                  jnp.where(lo, jnp.where(lane == 0, one, zl), vv))
            out = []
            for hh in range(2):
                m, acc = carry[2 * hh], carry[2 * hh + 1]
                ck = _pick_col(ckt, 2 * p + hh)
                sc = _dot_nt(kk, qh[hh]) - ck
                if diag:
                    sc = jnp.where(causal, sc, -jnp.inf)
                mx = jnp.max(sc, axis=0, keepdims=True) + cq[hh]
                mn = jnp.maximum(m, mx)
                alpha = jnp.exp(m - mn)
                pt = jnp.exp(sc + (cq[hh] - mn)).astype(BF16)
                acc = alpha * acc + _dot_tn(vh[hh], pt)
                out += [mn, acc]
            return tuple(out)

        minit = jnp.full((1, t), -jnp.inf, F32)
        ainit = jnp.zeros((LANE, t), F32)
        carry = kv_step(i, (minit, ainit, minit, ainit), True)
        carry = lax.fori_loop(0, i, lambda j, cr: kv_step(j, cr, False), carry)
        m0, a0, m1, a1 = carry
        l0 = a0[HEAD_DIM:HEAD_DIM + 1, :]
        l1 = a1[0:1, :]
        row = lax.broadcasted_iota(jnp.int32, (LANE, 1), 0)
        ot = jnp.where(row < HEAD_DIM, a0 / l0, a1 / l1)
        o_ref[...] = ot.T.astype(BF16)
        r8 = lax.broadcasted_iota(jnp.int32, (8, 1), 0)
        lse_ref[0] = jnp.where(r8 == 0, m0 + jnp.log(l0), jnp.where(r8 == 1, m1 + jnp.log(l1), 0.0))

    return pl.pallas_call(
        body, grid=(npair, s // t), out_shape=(_sds((s, d), BF16), _sds((npair, 8, s), F32)),
        in_specs=[pl.BlockSpec((t, LANE), lambda p, i: (i, p)),
                  pl.BlockSpec((s, LANE), lambda p, i: (0, p)), pl.BlockSpec((s, LANE), lambda p, i: (0, p)),
                  pl.BlockSpec((s, LANE), lambda p, i: (0, 0)), pl.BlockSpec((N_HEADS, s), lambda p, i: (0, 0))],
        out_specs=(pl.BlockSpec((t, LANE), lambda p, i: (i, p)), pl.BlockSpec((1, 8, t), lambda p, i: (p, 0, i))),
        name="fox_attn_fwd", compiler_params=_cp(("parallel", "parallel")))(q, k, v, cum, cumt)


def _final(x, gain, target):
    s, d = x.shape
    ts = _tile(s, 512)

    def body(x_ref, g_ref, t_ref, lsum_ref, dx_ref, dg_ref):
        @pl.when(pl.program_id(0) == 0)
        def _():
            lsum_ref[...] = jnp.zeros_like(lsum_ref)
            dg_ref[...] = jnp.zeros_like(dg_ref)

        xhat, rstd = _rms_parts(x_ref[...])
        e = xhat * g_ref[...] - t_ref[...]
        lsum_ref[...] += _colsum(e * e)
        dout = e * (1.0 / d)
        dg_ref[...] += _colsum(dout * xhat)
        dxhat = dout * g_ref[...]
        dx_ref[...] = rstd * (dxhat - xhat * jnp.mean(dxhat * xhat, axis=-1, keepdims=True))

    vec = pl.BlockSpec((1, d), lambda i: (0, 0))
    row = pl.BlockSpec((ts, d), lambda i: (i, 0))
    return pl.pallas_call(
        body, grid=(s // ts,), out_shape=(_sds((1, d), F32), _sds((s, d), F32), _sds((1, d), F32)),
        in_specs=[row, vec, row], out_specs=(vec, row, vec),
        name="final_norm_loss", compiler_params=_cp(("arbitrary",)))(x, gain, target)


def _ffn_bwd_act(dx, gate, w_out, ug, uu):
    s, d = dx.shape
    f = w_out.shape[0]
    n = f // 2
    ts = _tile(s, 512)

    def body(dx_ref, g_ref, w_ref, ug_ref, uu_ref, dug_ref, duu_ref, dys):
        @pl.when(pl.program_id(1) == 0)
        def _():
            dys[...] = (dx_ref[...] * g_ref[...]).astype(BF16)

        dact = _dot_nt(dys[...], w_ref[...])
        g = ug_ref[...].astype(F32)
        u = uu_ref[...].astype(F32)
        sg = _sigmoid(g)
        dug_ref[...] = (dact * u * sg * (1.0 + g * (1.0 - sg))).astype(BF16)
        duu_ref[...] = (dact * g * sg).astype(BF16)

    tile = pl.BlockSpec((ts, n), lambda i, q: (i, q))
    return pl.pallas_call(
        body, grid=(s // ts, 2), out_shape=(_sds((s, f), BF16), _sds((s, f), BF16)),
        in_specs=[pl.BlockSpec((ts, d), lambda i, q: (i, 0)), pl.BlockSpec((1, d), lambda i, q: (0, 0)),
                  pl.BlockSpec((n, d), lambda i, q: (q, 0)), tile, tile],
        out_specs=(tile, tile), scratch_shapes=[pltpu.VMEM((ts, d), BF16)],
        name="ffn_bwd_act", compiler_params=_cp(("parallel", "arbitrary")))(dx, gate, w_out, ug, uu)


def _dw_mm(a, b_list, tk, tn, name, gate=None, wfull=None, dgate_init=None):
    s, kdim = a.shape
    nb1 = b_list[0].shape[1] // tn
    nb = nb1 * len(b_list)
    ts = _tile(s, 512)
    nk = kdim // tk
    ns = s // ts
    gated = gate is not None

    def body(*refs):
        a_ref = refs[0]
        b_refs = refs[1:1 + len(b_list)]
        rest = refs[1 + len(b_list):]
        if gated:
            g_ref, w_ref, di_ref, o_ref, dg_ref, acc = rest
        else:
            o_ref, acc = rest
        jn, ik, st = pl.program_id(0), pl.program_id(1), pl.program_id(2)

        @pl.when(st == 0)
        def _():
            acc[...] = jnp.zeros_like(acc)

        for mi, b_ref in enumerate(b_refs):
            @pl.when(jn // nb1 == mi)
            def _(b_ref=b_ref):
                acc[...] += _dot_tn(a_ref[...], b_ref[...].astype(BF16))

        if gated:
            @pl.when(jnp.logical_and(ik == 0, st == 0))
            def _():
                dg_ref[...] = di_ref[...]

        @pl.when(st == ns - 1)
        def _():
            if gated:
                o_ref[0] = acc[...] * g_ref[...]
                dg_ref[...] += _colsum(acc[...] * w_ref[...].astype(F32))
            else:
                o_ref[0] = acc[...]

    in_specs = [pl.BlockSpec((ts, tk), lambda jn, ik, st: (st, ik))]
    for mi in range(len(b_list)):
        in_specs.append(pl.BlockSpec(
            (ts, tn), lambda jn, ik, st, mi=mi: (st, jnp.clip(jn - mi * nb1, 0, nb1 - 1))))
    args = [a] + list(b_list)
    out_shape = [_sds((nb, kdim, tn), F32)]
    out_specs = [pl.BlockSpec((1, tk, tn), lambda jn, ik, st: (jn, ik, 0))]
    if gated:
        vec = pl.BlockSpec((1, tn), lambda jn, ik, st: (0, jn))
        in_specs += [vec, pl.BlockSpec((tk, tn), lambda jn, ik, st: (ik, jn)), vec]
        args += [gate, wfull, dgate_init]
        out_shape.append(_sds((1, nb * tn), F32))
        out_specs.append(vec)
    res = pl.pallas_call(
        body, grid=(nb, nk, ns), out_shape=tuple(out_shape), in_specs=in_specs, out_specs=tuple(out_specs),
        scratch_shapes=[pltpu.VMEM((tk, tn), F32)],
        name=name, compiler_params=_cp(("parallel", "arbitrary", "arbitrary")))(*args)
    return res if gated else res[0]


def _mm_normbwd(terms, x, dxres, gain, scale, name, ts_pref=256):
    s, d = x.shape
    ts = _tile(s, ts_pref)
    arrs, warrs = [], []
    for a, _, w, _ in terms:
        if not any(a is z for z in arrs):
            arrs.append(a)
        if not any(w is z for z in warrs):
            warrs.append(w)
    ai = [next(i for i, z in enumerate(arrs) if z is a) for a, _, _, _ in terms]
    wi = [next(i for i, z in enumerate(warrs) if z is w) for _, _, w, _ in terms]

    def body(*refs):
        a_refs = refs[:len(arrs)]
        w_refs = refs[len(arrs):len(arrs) + len(warrs)]
        x_ref, dr_ref, g_ref, sc_ref, dx_ref, dsh_ref, dsc_ref, dg_ref = refs[len(arrs) + len(warrs):]

        @pl.when(pl.program_id(0) == 0)
        def _():
            dsh_ref[...] = jnp.zeros_like(dsh_ref)
            dsc_ref[...] = jnp.zeros_like(dsc_ref)
            dg_ref[...] = jnp.zeros_like(dg_ref)

        dh = None
        for ti, (_, c0, w, q) in enumerate(terms):
            n = w.shape[2]
            part = _dot_nt(a_refs[ai[ti]][:, pl.ds(c0, n)], w_refs[wi[ti]][q])
            dh = part if dh is None else dh + part
        xhat, rstd = _rms_parts(x_ref[...])
        nrm = xhat * g_ref[...]
        dsh_ref[...] += _colsum(dh)
        dsc_ref[...] += _colsum(dh * nrm)
        dn = dh * (1.0 + sc_ref[...])
        dg_ref[...] += _colsum(dn * xhat)
        dxhat = dn * g_ref[...]
        dx_ref[...] = dr_ref[...] + rstd * (dxhat - xhat * jnp.mean(dxhat * xhat, axis=-1, keepdims=True))

    vec = pl.BlockSpec((1, d), lambda i: (0, 0))
    row = pl.BlockSpec((ts, d), lambda i: (i, 0))
    in_specs = [pl.BlockSpec((ts, a.shape[1]), lambda i: (i, 0)) for a in arrs]
    in_specs += [_resident(w.shape) for w in warrs]
    in_specs += [row, row, vec, vec]
    return pl.pallas_call(
        body, grid=(s // ts,), out_shape=(_sds((s, d), F32), _sds((1, d), F32), _sds((1, d), F32), _sds((1, d), F32)),
        in_specs=in_specs, out_specs=(row, vec, vec, vec),
        name=name, compiler_params=_cp(("arbitrary",)))(*arrs, *warrs, x, dxres, gain, scale)


def _do_kernel(dx, gate, wo, o):
    s, d = dx.shape
    ts = _tile(s, 512)

    def body(dx_ref, g_ref, w_ref, o_ref, do_ref, dl_ref):
        dy = (dx_ref[...] * g_ref[...]).astype(BF16)
        do = _dot_nt(dy, w_ref[...])
        do_ref[...] = do.astype(BF16)
        prod = do * o_ref[...].astype(F32)
        hrow = lax.broadcasted_iota(jnp.int32, (N_HEADS, d), 0)
        hcol = lax.broadcasted_iota(jnp.int32, (N_HEADS, d), 1) // HEAD_DIM
        sel = (hrow == hcol).astype(F32)
        dl_ref[...] = lax.dot_general(sel, prod, (((1,), (1,)), ((), ())), preferred_element_type=F32,
                                      precision=lax.Precision.HIGHEST)

    row = pl.BlockSpec((ts, d), lambda i: (i, 0))
    return pl.pallas_call(
        body, grid=(s // ts,), out_shape=(_sds((s, d), BF16), _sds((N_HEADS, s), F32)),
        in_specs=[row, pl.BlockSpec((1, d), lambda i: (0, 0)), _resident(wo.shape), row],
        out_specs=(row, pl.BlockSpec((N_HEADS, ts), lambda i: (0, i))),
        name="attn_do", compiler_params=_cp(("parallel",)))(dx, gate, wo, o)


def _attn_bwd(q, k, v, do, cum, cumt, lse, deltat):
    s, d = q.shape
    t = _tile(s, ATT_T)
    npair = d // LANE
    nt = s // t
    qscale = HEAD_DIM ** -0.5

    def body(q_ref, do_ref, k_ref, v_ref, ck_ref, cumt_ref, lse_ref, dl_ref, dq_ref, dk_ref, dv_ref, dcum_ref, dqs):
        p = pl.program_id(0)
        j = pl.program_id(1)
        lane = lax.broadcasted_iota(jnp.int32, (1, LANE), 1)
        lo = lane < HEAD_DIM
        one = jnp.ones((1, LANE), BF16)
        zl = jnp.zeros((1, LANE), BF16)
        ones_lane = (jnp.where(lane == HEAD_DIM, one, zl), jnp.where(lane == 0, one, zl))

        @pl.when(jnp.logical_and(p == 0, j == 0))
        def _():
            dcum_ref[...] = jnp.zeros_like(dcum_ref)

        @pl.when(j == 0)
        def _():
            dqs[...] = jnp.zeros_like(dqs)

        kk = k_ref[...]
        vv = v_ref[...]
        zero = jnp.zeros_like(kk)
        kmask = (jnp.where(lo, kk, zero), jnp.where(lo, zero, kk))
        kones = (jnp.where(lo, kk, ones_lane[0]), jnp.where(lo, ones_lane[1], kk))
        ckt = ck_ref[...]
        ck = (_pick_col(ckt, 2 * p), _pick_col(ckt, 2 * p + 1))
        krow = lax.broadcasted_iota(jnp.int32, (t, t), 0)
        qcol = lax.broadcasted_iota(jnp.int32, (t, t), 1)
        causal = krow <= qcol

        def q_step(i, carry, diag):
            dv_acc, dk0, dk1 = carry
            qs = pl.multiple_of(i * t, t)
            qq = q_ref[pl.ds(qs, t), :]
            dd = do_ref[pl.ds(qs, t), :]
            qones = (jnp.where(lo, qq, ones_lane[0]), jnp.where(lo, ones_lane[1], qq))
            doh = (jnp.where(lo, dd, zero), jnp.where(lo, zero, dd))
            cqt = cumt_ref[:, pl.ds(qs, t)]
            dlt = dl_ref[:, pl.ds(qs, t)]
            lset = lse_ref[0, :, pl.ds(qs, t)]
            dks = [dk0, dk1]
            for hh in range(2):
                cq = _pick_row(cqt, 2 * p + hh)
                dl = _pick_row(dlt, 2 * p + hh)
                ls = lset[hh:hh + 1, :]
                sc = _dot_nt(kmask[hh], qq) - ck[hh]
                if diag:
                    sc = jnp.where(causal, sc, -jnp.inf)
                pt = jnp.exp(sc + (cq - ls))
                dpt = _dot_nt(vv, doh[hh])
                dst = (pt * (dpt - dl)).astype(BF16)
                dv_acc = dv_acc + _dot(pt.astype(BF16), doh[hh])
                dks[hh] = dks[hh] + _dot(dst, qones[hh])
                dqs[hh, pl.ds(qs, t), :] += _dot_tn(dst, kones[hh])
            return dv_acc, dks[0], dks[1]

        z = jnp.zeros((t, LANE), F32)
        carry = q_step(j, (z, z, z), True)
        dv_acc, dk0, dk1 = lax.fori_loop(j + 1, nt, lambda i, cr: q_step(i, cr, False), carry)
        dv_ref[...] = dv_acc.astype(BF16)
        dk_ref[...] = jnp.where(lo, dk0, dk1).astype(BF16)
        dck = (jnp.where(lane == 2 * p, dk0[:, HEAD_DIM:HEAD_DIM + 1], 0.0)
               + jnp.where(lane == 2 * p + 1, dk1[:, 0:1], 0.0))
        rows = pl.ds(pl.multiple_of(j * t, t), t)
        dcum_ref[rows, :] = dcum_ref[rows, :] - dck

        @pl.when(j == nt - 1)
        def _():
            a0 = dqs[0]
            a1 = dqs[1]
            dq_ref[...] = (jnp.where(lo, a0, a1) * qscale).astype(BF16)
            dcq = (jnp.where(lane == 2 * p, a0[:, HEAD_DIM:HEAD_DIM + 1], 0.0)
                   + jnp.where(lane == 2 * p + 1, a1[:, 0:1], 0.0))
            dcum_ref[...] = dcum_ref[...] + dcq

    col = pl.BlockSpec((s, LANE), lambda p, j: (0, p))
    blk = pl.BlockSpec((t, LANE), lambda p, j: (j, p))
    return pl.pallas_call(
        body, grid=(npair, nt),
        out_shape=(_sds((s, d), BF16), _sds((s, d), BF16), _sds((s, d), BF16), _sds((s, LANE), F32)),
        in_specs=[col, col, blk, blk, pl.BlockSpec((t, LANE), lambda p, j: (j, 0)),
                  pl.BlockSpec((N_HEADS, s), lambda p, j: (0, 0)), pl.BlockSpec((1, 8, s), lambda p, j: (p, 0, 0)),
                  pl.BlockSpec((N_HEADS, s), lambda p, j: (0, 0))],
        out_specs=(col, blk, blk, pl.BlockSpec((s, LANE), lambda p, j: (0, 0))),
        scratch_shapes=[pltpu.VMEM((2, s, LANE), F32)],
        name="fox_attn_bwd", compiler_params=_cp(("arbitrary", "arbitrary")))(q, do, k, v, cum, cumt, lse, deltat)


def _cumsum_bwd(dcum, flog, fb):
    s = dcum.shape[0]
    ts = _tile(s, 256)
    nt = s // ts

    def body(dc_ref, f_ref, b_ref, df_ref, db_ref, carry):
        @pl.when(pl.program_id(0) == 0)
        def _():
            carry[...] = jnp.zeros_like(carry)
            db_ref[...] = jnp.zeros_like(db_ref)

        r = lax.broadcasted_iota(jnp.int32, (ts, ts), 0)
        cidx = lax.broadcasted_iota(jnp.int32, (ts, ts), 1)
        tri = (cidx >= r).astype(F32)
        dls = jnp.dot(tri, dc_ref[...], preferred_element_type=F32, precision=lax.Precision.HIGHEST) + carry[...]
        carry[...] = dls[0:1, :]
        z = f_ref[...] + b_ref[...]
        df = dls * (1.0 / (1.0 + jnp.exp(z)))
        db_ref[...] += _colsum(df)
        df_ref[...] = df.astype(BF16)

    rev = pl.BlockSpec((ts, LANE), lambda i: (nt - 1 - i, 0))
    vec = pl.BlockSpec((1, LANE), lambda i: (0, 0))
    return pl.pallas_call(
        body, grid=(nt,), out_shape=(_sds((s, LANE), BF16), _sds((1, LANE), F32)),
        in_specs=[rev, rev, vec], out_specs=(rev, vec), scratch_shapes=[pltpu.VMEM((1, LANE), F32)],
        name="forget_cumsum_bwd", compiler_params=_cp(("arbitrary",)))(dcum, flog, fb)


def _conv_bwd1(dx, gate, w_out, b_out, dwo, lng, lnb):
    s, d = dx.shape
    ts = _tile(s, 512)
    ns = s // ts

    def body(dx_ref, g_ref, w_ref, bo_ref, y_ref, lg_ref, lb_ref, dd_ref, dlg_ref, dlb_ref, dbd_ref, dbo_ref, dge_ref, cs):
        i = pl.program_id(0)

        @pl.when(i == 0)
        def _():
            for r in (dlg_ref, dlb_ref, dbd_ref, cs):
                r[...] = jnp.zeros_like(r)

        dxv = dx_ref[...]
        cs[...] += _colsum(dxv)
        dsw = _dot_nt((dxv * g_ref[...]).astype(BF16), w_ref[...])
        yv = y_ref[...]
        mu = jnp.mean(yv, axis=-1, keepdims=True)
        yc = yv - mu
        rstd = lax.rsqrt(jnp.mean(yc * yc, axis=-1, keepdims=True) + EPS)
        xhat = yc * rstd
        ln = xhat * lg_ref[...] + lb_ref[...]
        sg = _sigmoid(ln)
        dln = dsw * (sg * (1.0 + ln * (1.0 - sg)))
        dlg_ref[...] += _colsum(dln * xhat)
        dlb_ref[...] += _colsum(dln)
        dxh = dln * lg_ref[...]
        dd = rstd * (dxh - jnp.mean(dxh, axis=-1, keepdims=True) - xhat * jnp.mean(dxh * xhat, axis=-1, keepdims=True))
        dbd_ref[...] += _colsum(dd)
        dd_ref[...] = dd

        @pl.when(i == ns - 1)
        def _():
            dbo_ref[...] = g_ref[...] * cs[...]
            dge_ref[...] = bo_ref[...] * cs[...]

    vec = pl.BlockSpec((1, d), lambda i: (0, 0))
    row = pl.BlockSpec((ts, d), lambda i: (i, 0))
    return pl.pallas_call(
        body, grid=(ns,), out_shape=(_sds((s, d), F32),) + tuple(_sds((1, d), F32) for _ in range(5)),
        in_specs=[row, vec, _resident(w_out.shape), vec, row, vec, vec], out_specs=(row, vec, vec, vec, vec, vec),
        scratch_shapes=[pltpu.VMEM((1, d), F32)],
        name="conv_bwd_ln", compiler_params=_cp(("arbitrary",)))(dx, gate, w_out, b_out, dwo, lng, lnb)


def _dwconv_bwd(ddwo, glu, a_s, g_s, wdw):
    s, d = ddwo.shape
    ts = _tile(s, 256)
    ns = s // ts
    rb, cb = 32, 256
    nrb = ts // rb

    def body(dd_ref, ddn_ref, gl_ref, glh_ref, a_ref, g_ref, w_ref, da_ref, dg_ref, dw_ref, sa_ref, sg_ref, bufd, bufg, dws):
        i = pl.program_id(0)

        @pl.when(i == 0)
        def _():
            dws[...] = jnp.zeros_like(dws)
            sa_ref[...] = jnp.zeros_like(sa_ref)
            sg_ref[...] = jnp.zeros_like(sg_ref)
            bufg[pl.ds(0, HALO), :] = jnp.zeros((HALO, d), F32)

        @pl.when(i > 0)
        def _():
            bufg[pl.ds(0, HALO), :] = glh_ref[...]

        bufg[pl.ds(HALO, ts), :] = gl_ref[...]
        bufd[pl.ds(0, ts), :] = dd_ref[...]

        @pl.when(i == ns - 1)
        def _():
            bufd[pl.ds(ts, HALO), :] = jnp.zeros((HALO, d), F32)

        @pl.when(i < ns - 1)
        def _():
            bufd[pl.ds(ts, HALO), :] = ddn_ref[...]

        for cc in range(d // cb):
            cs = pl.ds(cc * cb, cb)
            for r in range(nrb):
                acc = jnp.zeros((rb, cb), F32)
                for k in range(CONV_K):
                    acc = acc + w_ref[pl.ds(k, 1), cs] * bufd[pl.ds(r * rb + (CONV_K - 1) - k, rb), cs]
                rows = pl.ds(r * rb, rb)
                av = a_ref[rows, cs].astype(F32)
                sg = _sigmoid(g_ref[rows, cs].astype(F32))
                dav = acc * sg
                dgv = acc * av * sg * (1.0 - sg)
                da_ref[rows, cs] = dav.astype(BF16)
                dg_ref[rows, cs] = dgv.astype(BF16)
                sa_ref[:, cs] += _colsum(dav)
                sg_ref[:, cs] += _colsum(dgv)
            for k in range(CONV_K):
                acc8 = jnp.zeros((8, cb), F32)
                for r in range(nrb):
                    prod = bufd[pl.ds(r * rb, rb), cs] * bufg[pl.ds(HALO - (CONV_K - 1) + k + r * rb, rb), cs]
                    acc8 = acc8 + (prod[0:8] + prod[8:16]) + (prod[16:24] + prod[24:32])
                dws[pl.ds(8 * k, 8), cs] += acc8

        @pl.when(i == ns - 1)
        def _():
            dw_ref[...] = jnp.zeros_like(dw_ref)
            for k in range(CONV_K):
                dw_ref[pl.ds(k, 1), :] = _colsum(dws[pl.ds(8 * k, 8), :])

    row = pl.BlockSpec((ts, d), lambda i: (i, 0))
    vec = pl.BlockSpec((1, d), lambda i: (0, 0))
    hb = ts // HALO
    return pl.pallas_call(
        body, grid=(ns,),
        out_shape=(_sds((s, d), BF16), _sds((s, d), BF16), _sds((HALO, d), F32), _sds((1, d), F32), _sds((1, d), F32)),
        in_specs=[row, pl.BlockSpec((HALO, d), lambda i: (jnp.minimum((i + 1) * hb, ns * hb - 1), 0)),
                  row, pl.BlockSpec((HALO, d), lambda i: (jnp.maximum(i * hb - 1, 0), 0)),
                  row, row, pl.BlockSpec((HALO, d), lambda i: (0, 0))],
        out_specs=(row, row, pl.BlockSpec((HALO, d), lambda i: (0, 0)), vec, vec),
        scratch_shapes=[pltpu.VMEM((ts + HALO, d), F32), pltpu.VMEM((HALO + ts, d), F32), pltpu.VMEM((8 * HALO, d), F32)],
        name="dwconv_bwd", compiler_params=_cp(("arbitrary",)))(ddwo, ddwo, glu, glu, a_s, g_s, wdw)


def _ada_wgrad(cat, da, name):
    nl, _, n = da.shape
    d = cat.shape[0]
    tn = 256

    def body(c_ref, d_ref, o_ref):
        acc = c_ref[:, 0:1] * d_ref[0, 0:1, :]
        for r in range(1, 8):
            acc = acc + c_ref[:, r:r + 1] * d_ref[0, r:r + 1, :]
        o_ref[0] = acc

    return pl.pallas_call(
        body, grid=(nl, n // tn), out_shape=_sds((nl, d, n), F32),
        in_specs=[pl.BlockSpec((d, 8), lambda l, j: (0, 0)), pl.BlockSpec((1, 8, tn), lambda l, j: (l, 0, j))],
        out_specs=pl.BlockSpec((1, d, tn), lambda l, j: (l, 0, j)),
        name=name, compiler_params=_cp(("parallel", "parallel")))(cat, da)


def _silu_rows(c_all):
    def body(c_ref, o_ref):
        cc = c_ref[...]
        o_ref[...] = cc * _sigmoid(cc)

    return pl.pallas_call(body, out_shape=_sds(c_all.shape, F32), name="silu_c")(c_all)


def _adamw(w, g, m, v, name):
    r, c = w.shape
    tr = r
    for cand in (512, 256, 128, 64, 32, 16, 8):
        if r % cand == 0 and cand * c * 4 <= (1 << 20):
            tr = cand
            break
    bc1 = 1.0 - ADAM_B1 ** ADAM_STEP
    bc2 = 1.0 - ADAM_B2 ** ADAM_STEP

    def body(w_ref, g_ref, m_ref, v_ref, d_ref, nm_ref, nv_ref):
        gv = g_ref[...]
        mn = ADAM_B1 * m_ref[...] + (1.0 - ADAM_B1) * gv
        vn = ADAM_B2 * v_ref[...] + (1.0 - ADAM_B2) * (gv * gv)
        mh = mn / bc1
        vh = vn / bc2
        d_ref[...] = -ADAM_LR * (mh / (jnp.sqrt(vh) + ADAM_EPS) + ADAM_WD * w_ref[...])
        nm_ref[...] = mn
        nv_ref[...] = vn

    blk = pl.BlockSpec((tr, c), lambda i: (i, 0))
    return pl.pallas_call(
        body, grid=(r // tr,), out_shape=tuple(_sds((r, c), F32) for _ in range(3)),
        in_specs=[blk, blk, blk, blk], out_specs=(blk, blk, blk),
        name=name, compiler_params=_cp(("parallel",)))(w, g, m, v)


def _halves(w2):
    r, c = w2.shape
    return w2.reshape(2, (r // 2) * c // 1024, 1024)


def _pad_rows(a, rows, axis):
    pad = [(0, 0)] * a.ndim
    pad[axis] = (0, rows - a.shape[axis])
    return jnp.pad(a, pad)


def _vec(a):
    return a.reshape(1, -1)


def kernel(x, c, mix_norm_g, mix_ada_w, mix_ada_b, ffn_norm_g, ffn_ada_w, ffn_ada_b, ffn_w_in, ffn_w_out, conv_w_in, conv_b_in, conv_w_dw, conv_b_dw, conv_ln_g, conv_ln_b, conv_w_out, conv_b_out, kv_norm_g, kv_ada_w, kv_ada_b, kv_w, forget_b, attn_w_q, attn_w_o, final_norm_g, loss_target, m_mix_norm_g, m_mix_ada_w, m_mix_ada_b, m_ffn_norm_g, m_ffn_ada_w, m_ffn_ada_b, m_ffn_w_in, m_ffn_w_out, m_conv_w_in, m_conv_b_in, m_conv_w_dw, m_conv_b_dw, m_conv_ln_g, m_conv_ln_b, m_conv_w_out, m_conv_b_out, m_kv_norm_g, m_kv_ada_w, m_kv_ada_b, m_kv_w, m_forget_b, m_attn_w_q, m_attn_w_o, m_final_norm_g, v_mix_norm_g, v_mix_ada_w, v_mix_ada_b, v_ffn_norm_g, v_ffn_ada_w, v_ffn_ada_b, v_ffn_w_in, v_ffn_w_out, v_conv_w_in, v_conv_b_in, v_conv_w_dw, v_conv_b_dw, v_conv_ln_g, v_conv_ln_b, v_conv_w_out, v_conv_b_out, v_kv_norm_g, v_kv_ada_w, v_kv_ada_b, v_kv_w, v_forget_b, v_attn_w_q, v_attn_w_o, v_final_norm_g):
    xi, yi, ci = lax.axis_index("x"), lax.axis_index("y"), lax.axis_index("c")
    chip = 2 * xi + yi
    dev = 4 * xi + 2 * yi + ci
    s, d = x.shape[1], x.shape[2]
    f = ffn_w_out.shape[1] * 4
    x0 = x[0]
    nkv = kv_w.shape[1]
    nkv_all = 4 * nkv

    wdw_loc = _pad_rows(conv_w_dw[0], HALO, 0)
    small = jnp.concatenate([c.reshape(-1), conv_b_in.reshape(-1), wdw_loc.reshape(-1), conv_b_dw.reshape(-1),
                             conv_ln_g.reshape(-1), conv_ln_b.reshape(-1), conv_b_out.reshape(-1)])
    n_small = small.shape[0]
    w_small = -(-n_small // (8 * LANE)) * LANE
    small = jnp.pad(small, (0, 8 * w_small - n_small)).reshape(8, w_small)
    small_all = _allgather8(small, "ag_small_params", True).reshape(8, 8 * w_small)
    c_all = small_all[:, :d]
    per_chip = small_all[0::2]
    dq_ = d // 4
    o1 = d
    b_in_full = per_chip[:, o1:o1 + 2 * dq_].reshape(4, 1, 2 * dq_)
    o1 += 2 * dq_
    wdw_full = per_chip[:, o1:o1 + HALO * dq_].reshape(4, HALO, dq_).transpose(1, 0, 2).reshape(HALO, d)
    o1 += HALO * dq_
    bdw_full = per_chip[:, o1:o1 + dq_].reshape(1, d)
    lng_full = per_chip[:, o1 + dq_:o1 + 2 * dq_].reshape(1, d)
    lnb_full = per_chip[:, o1 + 2 * dq_:o1 + 3 * dq_].reshape(1, d)
    bout_full = per_chip[:, o1 + 3 * dq_:o1 + 4 * dq_].reshape(1, d)

    a_mix = _ada_fwd(c_all, mix_ada_w, "ada_mix")
    a_ffn = _ada_fwd(c_all, ffn_ada_w, "ada_ffn")
    a_kv = _ada_fwd(c_all, kv_ada_w[None], "ada_kv")
    n3 = mix_ada_w.shape[2]
    n2 = kv_ada_w.shape[1]
    ada_loc = jnp.concatenate([a_mix[0], a_mix[1], a_ffn[0], a_ffn[1], a_kv[0]], axis=1)
    w_ada = ada_loc.shape[1]
    ada_all = _allgather8(ada_loc, "ag_ada", True).reshape(8, 8, w_ada)
    ada_me = lax.dynamic_index_in_dim(ada_all, dev, axis=1, keepdims=False)[0::2]

    def ada_vec(off, n, bias):
        return ada_me[:, off:off + n].reshape(1, 4 * n) + bias.reshape(1, -1)

    ada_m0 = ada_vec(0, n3, mix_ada_b[0])
    ada_m1 = ada_vec(n3, n3, mix_ada_b[1])
    ada_f0 = ada_vec(2 * n3, n3, ffn_ada_b[0])
    ada_f1 = ada_vec(3 * n3, n3, ffn_ada_b[1])
    ada_k = ada_vec(4 * n3, n2, kv_ada_b)

    def split3(a):
        return a[:, :d], a[:, d:2 * d], a[:, 2 * d:3 * d]

    sh_m0, sc_m0, gt_m0 = split3(ada_m0)
    sh_m1, sc_m1, gt_m1 = split3(ada_m1)
    sh_f0, sc_f0, gt_f0 = split3(ada_f0)
    sh_f1, sc_f1, gt_f1 = split3(ada_f1)
    sh_k, sc_k = ada_k[:, :d], ada_k[:, d:2 * d]

    big = [ffn_w_in[0], ffn_w_in[1], ffn_w_out[0], ffn_w_out[1], conv_w_in[0], conv_w_out[0], kv_w, attn_w_q[0], attn_w_o[0]]
    hrows = [(w.shape[0] // 2) * w.shape[1] // 1024 for w in big]
    prows = [-(-r // 16) * 16 for r in hrows]
    offs = [sum(prows[:i]) for i in range(len(big))]
    rtot = sum(prows)
    pack = jnp.concatenate([_pad_rows(_halves(w.astype(BF16)), pr, 1) for w, pr in zip(big, prows)], axis=1)
    mine = lax.dynamic_index_in_dim(pack, ci, axis=0, keepdims=False)
    gathered = _allgather8(mine, "ag_weights", False).reshape(4, 2, rtot, 1024)

    def full_w(i):
        w = big[i]
        return gathered[:, :, offs[i]:offs[i] + hrows[i], :].reshape(4, w.shape[0], w.shape[1])

    w_in = [full_w(0), full_w(1)]
    w_out = [full_w(2).reshape(f, d), full_w(3).reshape(f, d)]
    cw_in = full_w(4)
    cw_out = full_w(5).reshape(d, d)
    kvw = full_w(6).transpose(1, 0, 2).reshape(d, nkv_all)
    wk, wv = kvw[:, :d], kvw[:, d:2 * d]
    wf = jnp.pad(kvw[:, 2 * d:], ((0, 0), (0, LANE - N_HEADS)))
    wq = full_w(7).reshape(d, d)
    wo = full_w(8).reshape(d, d)

    zero_b = jnp.zeros((1, d), F32)
    g_m0, g_m1 = _vec(mix_norm_g[0]), _vec(mix_norm_g[1])
    g_f0, g_f1 = _vec(ffn_norm_g[0]), _vec(ffn_norm_g[1])
    g_k, g_fin = _vec(kv_norm_g), _vec(final_norm_g)
    fb = jnp.pad(forget_b, (0, LANE - N_HEADS)).reshape(1, LANE)

    h0, glu, a_s, g_s = _in_pair(x0, g_m0, sh_m0, sc_m0, cw_in, b_in_full, True, "conv_in")
    dwo, sw = _dwconv_fwd(glu, wdw_full, bdw_full, lng_full, lnb_full)
    x1 = _mm_res(sw, cw_out, bout_full, gt_m0, x0, "conv_out")
    hf0, act0, ug0, uu0 = _in_pair(x1, g_f0, sh_f0, sc_f0, w_in[0], None, False, "ffn0_in")
    x2 = _mm_res(act0, w_out[0], zero_b, gt_f0, x1, "ffn0_out")
    hk, h1, kk, vv, qq, flog = _qkv(x2, (g_k, sh_k, sc_k), (g_m1, sh_m1, sc_m1), wk, wv, wf, wq)
    cum, cumt = _cumsum_fwd(flog, fb)
    o, lse = _attn_fwd(qq, kk, vv, cum, cumt)
    x3 = _mm_res(o, wo, zero_b, gt_m1, x2, "attn_out")
    hf1, act1, ug1, uu1 = _in_pair(x3, g_f1, sh_f1, sc_f1, w_in[1], None, False, "ffn1_in")
    x4 = _mm_res(act1, w_out[1], zero_b, gt_f1, x3, "ffn1_out")
    lsum, dx4, d_gfin = _final(x4, g_fin, loss_target[0])
    loss = lax.psum(0.5 / d * jnp.sum(lsum), ("x", "y", "c"))

    nf = f // 2

    def ffn_bwd(dx_out, x_in, hf, act, ug, uu, gain, scale, gate, w_in_l, w_out_l, tag):
        dug, duu = _ffn_bwd_act(dx_out, gate, w_out_l, ug, uu)
        dw_out, dgate = _dw_mm(act, [dx_out], nf, d, tag + "_dw_out", gate=gate, wfull=w_out_l, dgate_init=zero_b)
        terms = [(dug, 0, w_in_l, 0), (dug, nf, w_in_l, 1), (duu, 0, w_in_l, 2), (duu, nf, w_in_l, 3)]
        dx_in, dsh, dsc, dgn = _mm_normbwd(terms, x_in, dx_out, gain, scale, tag + "_bwd_in")
        dw_in = _dw_mm(hf, [dug, duu], d, nf, tag + "_dw_in")
        return dx_in, dw_in, dw_out[0], dsh, dsc, dgate, dgn

    dx3, dw_in1, dw_out1, dsh_f1, dsc_f1, dgt_f1, dgn_f1 = ffn_bwd(dx4, x3, hf1, act1, ug1, uu1, g_f1, sc_f1, gt_f1, w_in[1], w_out[1], "ffn1")

    do, deltat = _do_kernel(dx3, gt_m1, wo, o)
    dwo_att, dgt_m1 = _dw_mm(o, [dx3], d, d, "attn_dw_o", gate=gt_m1, wfull=wo, dgate_init=zero_b)
    dq, dk, dv, dcum = _attn_bwd(qq, kk, vv, do, cum, cumt, lse, deltat)
    wq3 = wq.reshape(1, d, d)
    dx2a, dsh_m1, dsc_m1, dgn_m1 = _mm_normbwd([(dq, 0, wq3, 0)], x2, dx3, g_m1, sc_m1, "attn_bwd_q")
    dwq = _dw_mm(h1, [dq], d, d, "attn_dw_q")[0]

    df, dfb = _cumsum_bwd(dcum, flog, fb)
    terms = [(dk, 0, wk.reshape(1, d, d), 0), (dv, 0, wv.reshape(1, d, d), 0), (df, 0, wf.reshape(1, d, LANE), 0)]
    dx2, dsh_k, dsc_k, dgn_k = _mm_normbwd(terms, x2, dx2a, g_k, sc_k, "kv_bwd")
    dwk = _dw_mm(hk, [dk], d, d, "kv_dw_k")[0]
    dwv = _dw_mm(hk, [dv], d, d, "kv_dw_v")[0]
    dwf = _dw_mm(hk, [df], d, LANE, "kv_dw_f")[0]
    dkvw = jnp.concatenate([dwk, dwv, dwf[:, :N_HEADS]], axis=1)
    dkvw = dkvw.reshape(d, 4, nkv).transpose(1, 0, 2)

    dx1, dw_in0, dw_out0, dsh_f0, dsc_f0, dgt_f0, dgn_f0 = ffn_bwd(dx2, x1, hf0, act0, ug0, uu0, g_f0, sc_f0, gt_f0, w_in[0], w_out[0], "ffn0")

    ddwo, d_lng, d_lnb, d_bdw, d_bout, dgt_extra = _conv_bwd1(dx1, gt_m0, cw_out, bout_full, dwo, lng_full, lnb_full)
    dcw_out, dgt_m0 = _dw_mm(sw, [dx1], d, d, "conv_dw_out", gate=gt_m0, wfull=cw_out, dgate_init=dgt_extra)
    da, dg, d_wdw, d_bin_a, d_bin_g = _dwconv_bwd(ddwo, glu, a_s, g_s, wdw_full)
    nc = cw_in.shape[2]
    terms = [(da, 0, cw_in, 0), (da, nc, cw_in, 1), (dg, 0, cw_in, 2), (dg, nc, cw_in, 3)]
    dx0, dsh_m0, dsc_m0, dgn_m0 = _mm_normbwd(terms, x0, dx1, g_m0, sc_m0, "conv_bwd_in")
    dcw_in = _dw_mm(h0, [da, dg], d, nc, "conv_dw_in")

    bigg = [dw_in0, dw_in1, dw_out0.reshape(4, f // 4, d), dw_out1.reshape(4, f // 4, d), dcw_in, dcw_out[0].reshape(4, d // 4, d),
            dkvw, dwq.reshape(4, d // 4, d), dwo_att[0].reshape(4, d // 4, d)]
    gpack = jnp.concatenate(
        [_pad_rows(g.reshape(4, 2, hr, 1024), pr, 2) for g, hr, pr in zip(bigg, hrows, prows)], axis=2)
    core_arr = jnp.reshape(ci, (1,)).astype(jnp.int32)
    chip_arr = jnp.reshape(chip, (1,)).astype(jnp.int32)
    land1 = _swap_halves(gpack)
    qsum = _add_halves(gpack, land1, core_arr)
    land2 = _scatter_chips(qsum)
    rsum = _add_chips(qsum, land2, chip_arr)
    gboth = _share_half(rsum)

    def shard_grad(i):
        w = big[i]
        return gboth[:, offs[i]:offs[i] + hrows[i], :].reshape(w.shape)

    d_ada = [jnp.concatenate([dsh_m0, dsc_m0, dgt_m0], axis=1), jnp.concatenate([dsh_m1, dsc_m1, dgt_m1], axis=1),
             jnp.concatenate([dsh_f0, dsc_f0, dgt_f0], axis=1), jnp.concatenate([dsh_f1, dsc_f1, dgt_f1], axis=1),
             jnp.concatenate([dsh_k, dsc_k], axis=1)]
    fields = d_ada + [dgn_m0, dgn_m1, dgn_f0, dgn_f1, dgn_k, d_gfin, d_bin_a, d_bin_g, d_bdw, d_lng, d_lnb, d_bout,
                      d_wdw.reshape(1, -1), dfb]
    foffs = [0]
    for fl in fields:
        foffs.append(foffs[-1] + fl.shape[1])
    n_row = foffs[-1]
    w_row = -(-n_row // (8 * LANE)) * LANE
    row = jnp.pad(jnp.concatenate(fields, axis=1), ((0, 0), (0, 8 * w_row - n_row))).reshape(8, w_row)
    rows_all = _allgather8(row, "ag_small_grads", True).reshape(8, 8, w_row)
    rsum_small = _sum8(rows_all).reshape(1, 8 * w_row)
    rows_flat = rows_all.reshape(8, 8 * w_row)

    def fsum(i):
        return rsum_small[:, foffs[i]:foffs[i + 1]]

    cat = _silu_rows(c_all).T

    def ada_cols(i, n):
        full = rows_flat[:, foffs[i]:foffs[i + 1]].reshape(8, 4, n)
        return lax.dynamic_index_in_dim(full, chip, axis=1, keepdims=False)

    g_mix_ada_w = _ada_wgrad(cat, jnp.stack([ada_cols(0, n3), ada_cols(1, n3)]), "ada_mix_wgrad")
    g_ffn_ada_w = _ada_wgrad(cat, jnp.stack([ada_cols(2, n3), ada_cols(3, n3)]), "ada_ffn_wgrad")
    g_kv_ada_w = _ada_wgrad(cat, ada_cols(4, n2)[None], "ada_kv_wgrad")[0]

    def my_cols(v, n):
        return lax.dynamic_index_in_dim(v.reshape(4, n), chip, axis=0, keepdims=False)

    grads = {
        "mix_norm_g": jnp.concatenate([fsum(5), fsum(6)], axis=0),
        "mix_ada_w": g_mix_ada_w,
        "mix_ada_b": jnp.concatenate([fsum(0), fsum(1)], axis=0),
        "ffn_norm_g": jnp.concatenate([fsum(7), fsum(8)], axis=0),
        "ffn_ada_w": g_ffn_ada_w,
        "ffn_ada_b": jnp.concatenate([fsum(2), fsum(3)], axis=0),
        "ffn_w_in": jnp.stack([shard_grad(0), shard_grad(1)]),
        "ffn_w_out": jnp.stack([shard_grad(2), shard_grad(3)]),
        "conv_w_in": shard_grad(4)[None],
        "conv_b_in": my_cols(jnp.concatenate([fsum(11), fsum(12)], axis=1), 2 * dq_)[None],
        "conv_w_dw": lax.dynamic_index_in_dim(fsum(17).reshape(HALO, 4, dq_), chip, axis=1, keepdims=False)[:CONV_K][None],
        "conv_b_dw": my_cols(fsum(13), dq_)[None],
        "conv_ln_g": my_cols(fsum(14), dq_)[None],
        "conv_ln_b": my_cols(fsum(15), dq_)[None],
        "conv_w_out": shard_grad(5)[None],
        "conv_b_out": my_cols(fsum(16), dq_)[None],
        "kv_norm_g": fsum(9).reshape(-1),
        "kv_ada_w": g_kv_ada_w,
        "kv_ada_b": fsum(4).reshape(-1),
        "kv_w": shard_grad(6),
        "forget_b": fsum(18).reshape(-1)[:N_HEADS],
        "attn_w_q": shard_grad(7)[None],
        "attn_w_o": shard_grad(8)[None],
        "final_norm_g": fsum(10).reshape(-1),
    }
    weights = dict(mix_norm_g=mix_norm_g, mix_ada_w=mix_ada_w, mix_ada_b=mix_ada_b, ffn_norm_g=ffn_norm_g, ffn_ada_w=ffn_ada_w, ffn_ada_b=ffn_ada_b, ffn_w_in=ffn_w_in, ffn_w_out=ffn_w_out, conv_w_in=conv_w_in, conv_b_in=conv_b_in, conv_w_dw=conv_w_dw, conv_b_dw=conv_b_dw, conv_ln_g=conv_ln_g, conv_ln_b=conv_ln_b, conv_w_out=conv_w_out, conv_b_out=conv_b_out, kv_norm_g=kv_norm_g, kv_ada_w=kv_ada_w, kv_ada_b=kv_ada_b, kv_w=kv_w, forget_b=forget_b, attn_w_q=attn_w_q, attn_w_o=attn_w_o, final_norm_g=final_norm_g)
    moms = dict(mix_norm_g=(m_mix_norm_g, v_mix_norm_g), mix_ada_w=(m_mix_ada_w, v_mix_ada_w), mix_ada_b=(m_mix_ada_b, v_mix_ada_b), ffn_norm_g=(m_ffn_norm_g, v_ffn_norm_g), ffn_ada_w=(m_ffn_ada_w, v_ffn_ada_w), ffn_ada_b=(m_ffn_ada_b, v_ffn_ada_b), ffn_w_in=(m_ffn_w_in, v_ffn_w_in), ffn_w_out=(m_ffn_w_out, v_ffn_w_out), conv_w_in=(m_conv_w_in, v_conv_w_in), conv_b_in=(m_conv_b_in, v_conv_b_in), conv_w_dw=(m_conv_w_dw, v_conv_w_dw), conv_b_dw=(m_conv_b_dw, v_conv_b_dw), conv_ln_g=(m_conv_ln_g, v_conv_ln_g), conv_ln_b=(m_conv_ln_b, v_conv_ln_b), conv_w_out=(m_conv_w_out, v_conv_w_out), conv_b_out=(m_conv_b_out, v_conv_b_out), kv_norm_g=(m_kv_norm_g, v_kv_norm_g), kv_ada_w=(m_kv_ada_w, v_kv_ada_w), kv_ada_b=(m_kv_ada_b, v_kv_ada_b), kv_w=(m_kv_w, v_kv_w), forget_b=(m_forget_b, v_forget_b), attn_w_q=(m_attn_w_q, v_attn_w_q), attn_w_o=(m_attn_w_o, v_attn_w_o), final_norm_g=(m_final_norm_g, v_final_norm_g))
    names = list(weights)

    deltas, new_m, new_v = {}, {}, {}
    small_names = [n for n in names if weights[n].size < (1 << 16)]
    for n in names:
        if n in small_names:
            continue
        w = weights[n]
        w2 = w.reshape(-1, w.shape[-1])
        dl, nm, nv = _adamw(w2, grads[n].reshape(w2.shape), moms[n][0].reshape(w2.shape), moms[n][1].reshape(w2.shape), "adamw_" + n)
        deltas[n], new_m[n], new_v[n] = dl.reshape(w.shape), nm.reshape(w.shape), nv.reshape(w.shape)

    def pack_small(get):
        flat = jnp.concatenate([get(n).reshape(-1) for n in small_names])
        rows_ = -(-flat.shape[0] // (8 * LANE)) * 8
        return jnp.pad(flat, (0, rows_ * LANE - flat.shape[0])).reshape(rows_, LANE)

    ws, gs = pack_small(lambda n: weights[n]), pack_small(lambda n: grads[n])
    ms_, vs_ = pack_small(lambda n: moms[n][0]), pack_small(lambda n: moms[n][1])
    vs_ = jnp.where(jnp.arange(vs_.size).reshape(vs_.shape) < sum(weights[n].size for n in small_names), vs_, 1.0)
    dl, nm, nv = _adamw(ws, gs, ms_, vs_, "adamw_small")
    off = 0
    for n in small_names:
        sz = weights[n].size
        shp = weights[n].shape
        deltas[n] = dl.reshape(-1)[off:off + sz].reshape(shp)
        new_m[n] = nm.reshape(-1)[off:off + sz].reshape(shp)
        new_v[n] = nv.reshape(-1)[off:off + sz].reshape(shp)
        off += sz

    grad_out = [grads[n].reshape(weights[n].shape) for n in names]
    return (loss, dx0[None], *grad_out, *[deltas[n] for n in names], *[new_m[n] for n in names], *[new_v[n] for n in names])
```

```python
import functools

import jax
import jax.numpy as jnp
from jax import lax
from jax.experimental import pallas as pl
from jax.experimental.pallas import tpu as pltpu

F32 = jnp.float32
BF16 = jnp.bfloat16
MESH = pl.DeviceIdType.MESH

EPS = 1e-6
N_HEADS = 16
HEAD_DIM = 64
CONV_K = 31
LANE = 128
HALO = 32
ATT_TQ = 1024
ATT_TK = 512
NPIECE = 3
SPARE = (HEAD_DIM, 0)
VMEM_MB = 48
ATT_BWD_VMEM_MB = 56

ADAM_LR = 0.001
ADAM_B1 = 0.9
ADAM_B2 = 0.999
ADAM_EPS = 1e-08
ADAM_WD = 0.01
ADAM_STEP = 10


def _sds(shape, dtype):
    return jax.ShapeDtypeStruct(tuple(shape), dtype)


def _cp(sem=None, vmem_mb=VMEM_MB):
    return pltpu.CompilerParams(dimension_semantics=sem, vmem_limit_bytes=vmem_mb << 20)


def _tile(n, pref):
    return pref if n % pref == 0 else n


def _row_tile(r):
    for cand in range(512, 7, -8):
        if r % cand == 0:
            return cand
    return r


def _resident(shape):
    nd = len(shape)
    return pl.BlockSpec(tuple(shape), lambda *_: (0,) * nd, pipeline_mode=pl.Buffered(1))


def _dot(a, b):
    return jnp.dot(a, b, preferred_element_type=F32)


def _dot_nt(a, b):
    return lax.dot_general(a, b, (((1,), (1,)), ((), ())), preferred_element_type=F32)


def _dot_tn(a, b):
    return lax.dot_general(a, b, (((0,), (0,)), ((), ())), preferred_element_type=F32)


def _sigmoid(x):
    return 1.0 / (1.0 + jnp.exp(-x))


def _colsum(x):
    return jnp.sum(x, axis=0, keepdims=True)


def _rms_parts(x):
    rstd = lax.rsqrt(jnp.mean(x * x, axis=-1, keepdims=True) + EPS)
    return x * rstd, rstd


def _allgather8(x_shard, name, in_vmem):
    m_per, n = x_shard.shape

    def body(x_ref, out_ref, send_sems, recv_sems, local_sem):
        x, y, c = lax.axis_index("x"), lax.axis_index("y"), lax.axis_index("c")
        me, sibling = (x, y, c), (x, y, 1 - c)
        chips = [(1 - x, y), (x, 1 - y), (1 - x, 1 - y)]

        def rows(px, py, pc):
            return out_ref.at[pl.ds((4 * px + 2 * py + pc) * m_per, m_per), :]

        def copy(k, block, to, src=None):
            return pltpu.make_async_remote_copy(
                src_ref=rows(*block) if src is None else src, dst_ref=rows(*block),
                send_sem=send_sems.at[k], recv_sem=recv_sems.at[k], device_id=to, device_id_type=MESH)

        mine = pltpu.make_async_copy(x_ref, rows(*me), local_sem)
        mine.start()
        first = [copy(0, me, sibling, src=x_ref)]
        first += [copy(1 + j, me, (*chip, c), src=x_ref) for j, chip in enumerate(chips)]
        for cp in first:
            cp.start()
        passed = [copy(4 + j, (*chip, c), sibling) for j, chip in enumerate(chips)]
        for j, chip in enumerate(chips):
            copy(1 + j, (*chip, c), me).wait_recv()
            passed[j].start()
        copy(0, sibling, me).wait_recv()
        for j, chip in enumerate(chips):
            copy(4 + j, (*chip, 1 - c), me).wait_recv()
        for cp in first + passed:
            cp.wait_send()
        mine.wait()

    space = pltpu.VMEM if in_vmem else pl.ANY
    return pl.pallas_call(
        body, out_shape=_sds((8 * m_per, n), x_shard.dtype),
        in_specs=[pl.BlockSpec(memory_space=space)], out_specs=pl.BlockSpec(memory_space=space),
        scratch_shapes=[pltpu.SemaphoreType.DMA((7,)), pltpu.SemaphoreType.DMA((7,)), pltpu.SemaphoreType.DMA],
        name=name)(x_shard)


def _swap_halves(p):
    nb, _, r, w = p.shape

    def body(p_ref, land_ref, send_sems, recv_sems):
        x, y, c = lax.axis_index("x"), lax.axis_index("y"), lax.axis_index("c")
        copies = [pltpu.make_async_remote_copy(
            src_ref=p_ref.at[j, 1 - c], dst_ref=land_ref.at[j], send_sem=send_sems.at[j], recv_sem=recv_sems.at[j],
            device_id=(x, y, 1 - c), device_id_type=MESH) for j in range(nb)]
        for cp in copies:
            cp.start()
        for cp in copies:
            cp.wait_recv()
        for cp in copies:
            cp.wait_send()

    return pl.pallas_call(
        body, out_shape=_sds((nb, r, w), p.dtype),
        in_specs=[pl.BlockSpec(memory_space=pl.ANY)], out_specs=pl.BlockSpec(memory_space=pl.ANY),
        scratch_shapes=[pltpu.SemaphoreType.DMA((nb,)), pltpu.SemaphoreType.DMA((nb,))],
        name="rs_swap_halves")(p)


def _scatter_chips(q):
    _, r, w = q.shape

    def body(q_ref, land_ref, send_sems, recv_sems):
        x, y, c = lax.axis_index("x"), lax.axis_index("y"), lax.axis_index("c")
        chips = [(1 - x, y), (x, 1 - y), (1 - x, 1 - y)]
        copies = [pltpu.make_async_remote_copy(
            src_ref=q_ref.at[2 * cx + cy], dst_ref=land_ref.at[k], send_sem=send_sems.at[k], recv_sem=recv_sems.at[k],
            device_id=(cx, cy, c), device_id_type=MESH) for k, (cx, cy) in enumerate(chips)]
        for cp in copies:
            cp.start()
        for cp in copies:
            cp.wait_recv()
        for cp in copies:
            cp.wait_send()

    return pl.pallas_call(
        body, out_shape=_sds((3, r, w), q.dtype),
        in_specs=[pl.BlockSpec(memory_space=pl.ANY)], out_specs=pl.BlockSpec(memory_space=pl.ANY),
        scratch_shapes=[pltpu.SemaphoreType.DMA((3,)), pltpu.SemaphoreType.DMA((3,))],
        name="rs_scatter_chips")(q)


def _share_half(rsum):
    r, w = rsum.shape

    def body(r_ref, out_ref, send_sem, recv_sem, local_sem):
        x, y, c = lax.axis_index("x"), lax.axis_index("y"), lax.axis_index("c")
        mine = pltpu.make_async_copy(r_ref, out_ref.at[c], local_sem)
        mine.start()
        cp = pltpu.make_async_remote_copy(
            src_ref=r_ref, dst_ref=out_ref.at[c], send_sem=send_sem, recv_sem=recv_sem,
            device_id=(x, y, 1 - c), device_id_type=MESH)
        cp.start()
        cp.wait_recv()
        cp.wait_send()
        mine.wait()

    return pl.pallas_call(
        body, out_shape=_sds((2, r, w), rsum.dtype),
        in_specs=[pl.BlockSpec(memory_space=pl.ANY)], out_specs=pl.BlockSpec(memory_space=pl.ANY),
        scratch_shapes=[pltpu.SemaphoreType.DMA, pltpu.SemaphoreType.DMA, pltpu.SemaphoreType.DMA],
        name="rs_share_half")(rsum)


def _add_halves(p, land, core):
    nb, _, r, w = p.shape
    tr = _row_tile(r)

    def body(core_ref, p_ref, l_ref, o_ref):
        o_ref[0] = p_ref[0, 0] + l_ref[0]

    gs = pltpu.PrefetchScalarGridSpec(
        num_scalar_prefetch=1, grid=(nb, r // tr),
        in_specs=[pl.BlockSpec((1, 1, tr, w), lambda j, i, cr: (j, cr[0], i, 0)),
                  pl.BlockSpec((1, tr, w), lambda j, i, cr: (j, i, 0))],
        out_specs=pl.BlockSpec((1, tr, w), lambda j, i, cr: (j, i, 0)))
    return pl.pallas_call(body, grid_spec=gs, out_shape=_sds((nb, r, w), p.dtype), name="rs_add_halves",
                          compiler_params=_cp(("parallel", "parallel")))(core, p, land)


def _add_chips(q, land, chip):
    _, r, w = q.shape
    tr = _row_tile(r)

    def body(chip_ref, q_ref, l_ref, o_ref):
        o_ref[...] = ((q_ref[0] + l_ref[0]) + l_ref[1]) + l_ref[2]

    gs = pltpu.PrefetchScalarGridSpec(
        num_scalar_prefetch=1, grid=(r // tr,),
        in_specs=[pl.BlockSpec((1, tr, w), lambda i, ch: (ch[0], i, 0)),
                  pl.BlockSpec((3, tr, w), lambda i, ch: (0, i, 0))],
        out_specs=pl.BlockSpec((tr, w), lambda i, ch: (i, 0)))
    return pl.pallas_call(body, grid_spec=gs, out_shape=_sds((r, w), q.dtype), name="rs_add_chips",
                          compiler_params=_cp(("parallel",)))(chip, q, land)


def _sum8(g):
    _, m, n = g.shape

    def body(g_ref, o_ref):
        acc = g_ref[0]
        for k in range(1, 8):
            acc = acc + g_ref[k]
        o_ref[...] = acc

    return pl.pallas_call(body, out_shape=_sds((m, n), g.dtype), name="sum8")(g)


def _ada_fwd(c_all, w3, name):
    nl, d, n = w3.shape
    tn = 256

    def body(c_ref, w_ref, o_ref):
        cc = c_ref[...]
        ca = (cc * _sigmoid(cc)).astype(BF16)
        o_ref[0] = _dot(ca, w_ref[0].astype(BF16))

    return pl.pallas_call(
        body, grid=(nl, n // tn), out_shape=_sds((nl, 8, n), F32),
        in_specs=[pl.BlockSpec((8, d), lambda l, j: (0, 0)), pl.BlockSpec((1, d, tn), lambda l, j: (l, 0, j))],
        out_specs=pl.BlockSpec((1, 8, tn), lambda l, j: (l, 0, j)),
        name=name, compiler_params=_cp(("parallel", "parallel")))(c_all, w3)


def _in_pair(x, gain, shift, scale, wg, bias, conv, name):
    s, d = x.shape
    n = wg.shape[2]
    ts = _tile(s, 512)

    def body(*refs):
        if conv:
            x_ref, g_ref, sh_ref, sc_ref, wa_ref, wb_ref, ba_ref, bb_ref, h_ref, o_ref, sa_ref, sb_ref, hs = refs
        else:
            x_ref, g_ref, sh_ref, sc_ref, wa_ref, wb_ref, h_ref, o_ref, sa_ref, sb_ref, hs = refs

        @pl.when(pl.program_id(1) == 0)
        def _():
            xhat, _ = _rms_parts(x_ref[...])
            h = (xhat * g_ref[...]) * (1.0 + sc_ref[...]) + sh_ref[...]
            hs[...] = h.astype(BF16)
            h_ref[...] = hs[...]

        h = hs[...]
        a = _dot(h, wa_ref[0])
        b = _dot(h, wb_ref[0])
        if conv:
            a = a + ba_ref[0]
            b = b + bb_ref[0]
            o_ref[...] = a * _sigmoid(b)
        else:
            o_ref[...] = (a * _sigmoid(a) * b).astype(BF16)
        sa_ref[...] = a.astype(BF16)
        sb_ref[...] = b.astype(BF16)

    vec = pl.BlockSpec((1, d), lambda i, q: (0, 0))
    in_specs = [pl.BlockSpec((ts, d), lambda i, q: (i, 0)), vec, vec, vec,
                pl.BlockSpec((1, d, n), lambda i, q: (q, 0, 0)), pl.BlockSpec((1, d, n), lambda i, q: (q + 2, 0, 0))]
    args = [x, gain, shift, scale, wg, wg]
    if conv:
        in_specs += [pl.BlockSpec((1, 1, n), lambda i, q: (q, 0, 0)), pl.BlockSpec((1, 1, n), lambda i, q: (q + 2, 0, 0))]
        args += [bias, bias]
    tile = pl.BlockSpec((ts, n), lambda i, q: (i, q))
    return pl.pallas_call(
        body, grid=(s // ts, 2),
        out_shape=(_sds((s, d), BF16), _sds((s, 2 * n), F32 if conv else BF16), _sds((s, 2 * n), BF16), _sds((s, 2 * n), BF16)),
        in_specs=in_specs, out_specs=(pl.BlockSpec((ts, d), lambda i, q: (i, 0)), tile, tile, tile),
        scratch_shapes=[pltpu.VMEM((ts, d), BF16)],
        name=name, compiler_params=_cp(("parallel", "arbitrary")))(*args)


def _dwconv_fwd(glu, wdw, bdw, lng, lnb):
    s, d = glu.shape
    ts = _tile(s, 256)
    rb, cb = 32, 256

    def body(cur_ref, halo_ref, w_ref, b_ref, g_ref, be_ref, dwo_ref, sw_ref, buf):
        i = pl.program_id(0)

        @pl.when(i == 0)
        def _():
            buf[pl.ds(0, HALO), :] = jnp.zeros((HALO, d), F32)

        @pl.when(i > 0)
        def _():
            buf[pl.ds(0, HALO), :] = halo_ref[...]

        buf[pl.ds(HALO, ts), :] = cur_ref[...]
        for r in range(ts // rb):
            for cc in range(d // cb):
                cs = pl.ds(cc * cb, cb)
                acc = jnp.zeros((rb, cb), F32) + b_ref[:, cs]
                for k in range(CONV_K):
                    acc = acc + w_ref[pl.ds(k, 1), cs] * buf[pl.ds(HALO - (CONV_K - 1) + k + r * rb, rb), cs]
                dwo_ref[pl.ds(r * rb, rb), cs] = acc
            rows = pl.ds(r * rb, rb)
            yv = dwo_ref[rows, :]
            mu = jnp.mean(yv, axis=-1, keepdims=True)
            yc = yv - mu
            var = jnp.mean(yc * yc, axis=-1, keepdims=True)
            ln = yc * lax.rsqrt(var + EPS) * g_ref[...] + be_ref[...]
            sw_ref[rows, :] = (ln * _sigmoid(ln)).astype(BF16)

    vec = pl.BlockSpec((1, d), lambda i: (0, 0))
    return pl.pallas_call(
        body, grid=(s // ts,), out_shape=(_sds((s, d), F32), _sds((s, d), BF16)),
        in_specs=[pl.BlockSpec((ts, d), lambda i: (i, 0)),
                  pl.BlockSpec((HALO, d), lambda i: (jnp.maximum(i * (ts // HALO) - 1, 0), 0)),
                  pl.BlockSpec((HALO, d), lambda i: (0, 0)), vec, vec, vec],
        out_specs=(pl.BlockSpec((ts, d), lambda i: (i, 0)), pl.BlockSpec((ts, d), lambda i: (i, 0))),
        scratch_shapes=[pltpu.VMEM((HALO + ts, d), F32)],
        name="dwconv_fwd", compiler_params=_cp(("parallel",)))(glu, glu, wdw, bdw, lng, lnb)


def _mm_res(a, w, b, gate, x, name):
    s, k = a.shape
    d = w.shape[1]
    ts = _tile(s, 512)

    def body(a_ref, w_ref, b_ref, g_ref, x_ref, o_ref):
        yv = _dot(a_ref[...], w_ref[...]) + b_ref[...]
        o_ref[...] = x_ref[...] + g_ref[...] * yv

    vec = pl.BlockSpec((1, d), lambda i: (0, 0))
    return pl.pallas_call(
        body, grid=(s // ts,), out_shape=_sds((s, d), F32),
        in_specs=[pl.BlockSpec((ts, k), lambda i: (i, 0)), _resident((k, d)), vec, vec, pl.BlockSpec((ts, d), lambda i: (i, 0))],
        out_specs=pl.BlockSpec((ts, d), lambda i: (i, 0)),
        name=name, compiler_params=_cp(("parallel",)))(a, w, b, gate, x)


def _qkv(x, kvp, mxp, wk, wv, wf, wq):
    s, d = x.shape
    ts = _tile(s, 512)
    qscale = HEAD_DIM ** -0.5

    def body(x_ref, gk, shk, sck, gm, shm, scm, wk_ref, wv_ref, wf_ref, wq_ref, hk_ref, h1_ref, k_ref, v_ref, q_ref, f_ref):
        xhat, _ = _rms_parts(x_ref[...])
        hk = ((xhat * gk[...]) * (1.0 + sck[...]) + shk[...]).astype(BF16)
        h1 = ((xhat * gm[...]) * (1.0 + scm[...]) + shm[...]).astype(BF16)
        hk_ref[...] = hk
        h1_ref[...] = h1
        k_ref[...] = _dot(hk, wk_ref[...]).astype(BF16)
        v_ref[...] = _dot(hk, wv_ref[...]).astype(BF16)
        f_ref[...] = _dot(hk, wf_ref[...])
        q_ref[...] = (_dot(h1, wq_ref[...]) * qscale).astype(BF16)

    vec = pl.BlockSpec((1, d), lambda i: (0, 0))
    row = pl.BlockSpec((ts, d), lambda i: (i, 0))
    return pl.pallas_call(
        body, grid=(s // ts,),
        out_shape=tuple(_sds((s, d), BF16) for _ in range(5)) + (_sds((s, LANE), F32),),
        in_specs=[row, vec, vec, vec, vec, vec, vec, _resident((d, d)), _resident((d, d)), _resident((d, LANE)), _resident((d, d))],
        out_specs=(row, row, row, row, row, pl.BlockSpec((ts, LANE), lambda i: (i, 0))),
        name="qkv_proj", compiler_params=_cp(("parallel",)))(x, *kvp, *mxp, wk, wv, wf, wq)


def _log_sigmoid(z):
    return jnp.minimum(z, 0.0) - jnp.log(1.0 + jnp.exp(-jnp.abs(z)))


def _cumsum_fwd(flog, fb):
    s = flog.shape[0]
    ts = _tile(s, 256)

    def body(f_ref, b_ref, cum_ref, cumt_ref, carry):
        @pl.when(pl.program_id(0) == 0)
        def _():
            carry[...] = jnp.zeros_like(carry)

        ls = _log_sigmoid(f_ref[...] + b_ref[...])
        r = lax.broadcasted_iota(jnp.int32, (ts, ts), 0)
        cidx = lax.broadcasted_iota(jnp.int32, (ts, ts), 1)
        tri = (cidx <= r).astype(F32)
        cs = jnp.dot(tri, ls, preferred_element_type=F32, precision=lax.Precision.HIGHEST) + carry[...]
        cum_ref[...] = cs
        cumt_ref[...] = cs.T
        carry[...] = cs[ts - 1:ts, :]

    return pl.pallas_call(
        body, grid=(s // ts,), out_shape=(_sds((s, LANE), F32), _sds((LANE, s), F32)),
        in_specs=[pl.BlockSpec((ts, LANE), lambda i: (i, 0)), pl.BlockSpec((1, LANE), lambda i: (0, 0))],
        out_specs=(pl.BlockSpec((ts, LANE), lambda i: (i, 0)), pl.BlockSpec((LANE, ts), lambda i: (0, i))),
        scratch_shapes=[pltpu.VMEM((1, LANE), F32)],
        name="forget_cumsum", compiler_params=_cp(("arbitrary",)))(flog, fb)


def _pick_row(m, idx):
    r = lax.broadcasted_iota(jnp.int32, (m.shape[0], 1), 0)
    return jnp.sum(jnp.where(r == idx, m, 0.0), axis=0, keepdims=True)


def _pick_col(m, idx):
    cidx = lax.broadcasted_iota(jnp.int32, (1, m.shape[1]), 1)
    return jnp.sum(jnp.where(cidx == idx, m, 0.0), axis=1, keepdims=True)


def _split3(x):
    hi = x.astype(BF16)
    r1 = x - hi.astype(F32)
    mid = r1.astype(BF16)
    lo = (r1 - mid.astype(F32)).astype(BF16)
    return hi, mid, lo


def _head_mask(lane, hh):
    lo = lane < HEAD_DIM
    return lo if hh == 0 else jnp.logical_not(lo)


def _attn_prep(k, v, cum):
    s, d = k.shape
    npair = d // LANE
    tc = _tile(s, 512)

    def body(k_ref, v_ref, c_ref, ka_ref, kt_ref, vt_ref):
        p = pl.program_id(0)
        lane = lax.broadcasted_iota(jnp.int32, (1, LANE), 1)
        kk = k_ref[...]
        vv = v_ref[...].astype(F32)
        ckt = c_ref[...]
        for hh in range(2):
            head = _head_mask(lane, hh)
            b = SPARE[hh]
            ck = _pick_col(ckt, 2 * p + hh)
            extra = jnp.where(lane == b + NPIECE, 1.0, 0.0).astype(BF16) + jnp.zeros((tc, LANE), BF16)
            for n_, pc in enumerate(_split3(ck)):
                extra = jnp.where(lane == b + n_, pc, extra)
            ka = jnp.where(head, kk, extra)
            ka_ref[0, hh] = ka
            kt_ref[0, hh] = ka.astype(F32).T.astype(BF16)
            vx = jnp.where(head, vv, jnp.where(lane == b, 1.0, 0.0))
            vt_ref[0, hh] = vx.T.astype(BF16)

    blk = pl.BlockSpec((tc, LANE), lambda p, c: (c, p))
    return pl.pallas_call(
        body, grid=(npair, s // tc),
        out_shape=(_sds((npair, 2, s, LANE), BF16), _sds((npair, 2, LANE, s), BF16), _sds((npair, 2, LANE, s), BF16)),
        in_specs=[blk, blk, pl.BlockSpec((tc, LANE), lambda p, c: (c, 0))],
        out_specs=(pl.BlockSpec((1, 2, tc, LANE), lambda p, c: (p, 0, c, 0)),
                   pl.BlockSpec((1, 2, LANE, tc), lambda p, c: (p, 0, 0, c)),
                   pl.BlockSpec((1, 2, LANE, tc), lambda p, c: (p, 0, 0, c))),
        name="fox_attn_prep", compiler_params=_cp(("parallel", "parallel")))(k, v, cum)


def _q_aug(qq, lane, hh):
    b = SPARE[hh]
    sel = jnp.logical_and(lane >= b, lane < b + NPIECE)
    neg = jnp.full((1, LANE), -1.0, BF16)
    zl = jnp.zeros((1, LANE), BF16)
    return jnp.where(_head_mask(lane, hh), qq, jnp.where(sel, neg, zl))


def _attn_fwd(q, kaug, vtr, cumt):
    s, d = q.shape
    tq = _tile(s, ATT_TQ)
    tk = _tile(s, ATT_TK)
    npair = d // LANE
    npart = max(1, tq // tk)

    def body(q_ref, ka_ref, vt_ref, cumt_ref, o_ref, lse_ref):
        p = pl.program_id(0)
        i = pl.program_id(1)
        lane = lax.broadcasted_iota(jnp.int32, (1, LANE), 1)
        qq = q_ref[...]
        qx = (_q_aug(qq, lane, 0), _q_aug(qq, lane, 1))
        cqt = cumt_ref[:, pl.ds(pl.multiple_of(i * tq, tq), tq)]
        cq = (_pick_row(cqt, 2 * p), _pick_row(cqt, 2 * p + 1))
        jd = (i * tq) // tk

        def kv_step(j, carry, diag):
            ks = pl.multiple_of(j * tk, tk)
            if diag:
                krow = lax.broadcasted_iota(jnp.int32, (tk, tq), 0) + j * tk
                qcol = lax.broadcasted_iota(jnp.int32, (tk, tq), 1) + i * tq
                causal = krow <= qcol
            out = []
            for hh in range(2):
                m, acc = carry[2 * hh], carry[2 * hh + 1]
                sc = _dot_nt(ka_ref[0, hh, pl.ds(ks, tk), :], qx[hh])
                if diag:
                    sc = jnp.where(causal, sc, -jnp.inf)
                mx = jnp.max(sc, axis=0, keepdims=True) + cq[hh]
                mn = jnp.maximum(m, mx)
                alpha = jnp.exp(m - mn)
                pt = jnp.exp(sc + (cq[hh] - mn)).astype(BF16)
                acc = alpha * acc + _dot(vt_ref[0, hh, :, pl.ds(ks, tk)], pt)
                out += [mn, acc]
            return tuple(out)

        minit = jnp.full((1, tq), -jnp.inf, F32)
        ainit = jnp.zeros((LANE, tq), F32)
        carry = (minit, ainit, minit, ainit)
        for pj in range(npart):
            carry = kv_step(jd + pj, carry, True)
        carry = lax.fori_loop(0, jd, lambda j, cr: kv_step(j, cr, False), carry)
        m0, a0, m1, a1 = carry
        l0 = a0[SPARE[0]:SPARE[0] + 1, :]
        l1 = a1[SPARE[1]:SPARE[1] + 1, :]
        row = lax.broadcasted_iota(jnp.int32, (LANE, 1), 0)
        ot = jnp.where(row < HEAD_DIM, a0 / l0, a1 / l1)
        o_ref[...] = ot.T.astype(BF16)
        r8 = lax.broadcasted_iota(jnp.int32, (8, 1), 0)
        lse_ref[0] = jnp.where(r8 == 0, m0 + jnp.log(l0), jnp.where(r8 == 1, m1 + jnp.log(l1), 0.0))

    return pl.pallas_call(
        body, grid=(npair, s // tq), out_shape=(_sds((s, d), BF16), _sds((npair, 8, s), F32)),
        in_specs=[pl.BlockSpec((tq, LANE), lambda p, i: (i, p)),
                  pl.BlockSpec((1, 2, s, LANE), lambda p, i: (p, 0, 0, 0)),
                  pl.BlockSpec((1, 2, LANE, s), lambda p, i: (p, 0, 0, 0)),
                  pl.BlockSpec((N_HEADS, s), lambda p, i: (0, 0))],
        out_specs=(pl.BlockSpec((tq, LANE), lambda p, i: (i, p)), pl.BlockSpec((1, 8, tq), lambda p, i: (p, 0, i))),
        name="fox_attn_fwd", compiler_params=_cp(("parallel", "parallel")))(q, kaug, vtr, cumt)


def _final(x, gain, target):
    s, d = x.shape
    ts = _tile(s, 512)

    def body(x_ref, g_ref, t_ref, lsum_ref, dx_ref, dg_ref):
        @pl.when(pl.program_id(0) == 0)
        def _():
            lsum_ref[...] = jnp.zeros_like(lsum_ref)
            dg_ref[...] = jnp.zeros_like(dg_ref)

        xhat, rstd = _rms_parts(x_ref[...])
        e = xhat * g_ref[...] - t_ref[...]
        lsum_ref[...] += _colsum(e * e)
        dout = e * (1.0 / d)
        dg_ref[...] += _colsum(dout * xhat)
        dxhat = dout * g_ref[...]
        dx_ref[...] = rstd * (dxhat - xhat * jnp.mean(dxhat * xhat, axis=-1, keepdims=True))

    vec = pl.BlockSpec((1, d), lambda i: (0, 0))
    row = pl.BlockSpec((ts, d), lambda i: (i, 0))
    return pl.pallas_call(
        body, grid=(s // ts,), out_shape=(_sds((1, d), F32), _sds((s, d), F32), _sds((1, d), F32)),
        in_specs=[row, vec, row], out_specs=(vec, row, vec),
        name="final_norm_loss", compiler_params=_cp(("arbitrary",)))(x, gain, target)


def _ffn_bwd_act(dx, gate, w_out, ug, uu):
    s, d = dx.shape
    f = w_out.shape[0]
    n = f // 2
    ts = _tile(s, 512)

    def body(dx_ref, g_ref, w_ref, ug_ref, uu_ref, dug_ref, duu_ref, dys):
        @pl.when(pl.program_id(1) == 0)
        def _():
            dys[...] = (dx_ref[...] * g_ref[...]).astype(BF16)

        dact = _dot_nt(dys[...], w_ref[...])
        g = ug_ref[...].astype(F32)
        u = uu_ref[...].astype(F32)
        sg = _sigmoid(g)
        dug_ref[...] = (dact * u * sg * (1.0 + g * (1.0 - sg))).astype(BF16)
        duu_ref[...] = (dact * g * sg).astype(BF16)

    tile = pl.BlockSpec((ts, n), lambda i, q: (i, q))
    return pl.pallas_call(
        body, grid=(s // ts, 2), out_shape=(_sds((s, f), BF16), _sds((s, f), BF16)),
        in_specs=[pl.BlockSpec((ts, d), lambda i, q: (i, 0)), pl.BlockSpec((1, d), lambda i, q: (0, 0)),
                  pl.BlockSpec((n, d), lambda i, q: (q, 0)), tile, tile],
        out_specs=(tile, tile), scratch_shapes=[pltpu.VMEM((ts, d), BF16)],
        name="ffn_bwd_act", compiler_params=_cp(("parallel", "arbitrary")))(dx, gate, w_out, ug, uu)


def _dw_mm(a, b_list, tk, tn, name, gate=None, wfull=None, dgate_init=None):
    s, kdim = a.shape
    nb1 = b_list[0].shape[1] // tn
    nb = nb1 * len(b_list)
    ts = _tile(s, 512)
    nk = kdim // tk
    ns = s // ts
    gated = gate is not None

    def body(*refs):
        a_ref = refs[0]
        b_refs = refs[1:1 + len(b_list)]
        rest = refs[1 + len(b_list):]
        if gated:
            g_ref, w_ref, di_ref, o_ref, dg_ref, acc = rest
        else:
            o_ref, acc = rest
        jn, ik, st = pl.program_id(0), pl.program_id(1), pl.program_id(2)

        @pl.when(st == 0)
        def _():
            acc[...] = jnp.zeros_like(acc)

        for mi, b_ref in enumerate(b_refs):
            @pl.when(jn // nb1 == mi)
            def _(b_ref=b_ref):
                acc[...] += _dot_tn(a_ref[...], b_ref[...].astype(BF16))

        if gated:
            @pl.when(jnp.logical_and(ik == 0, st == 0))
            def _():
                dg_ref[...] = di_ref[...]

        @pl.when(st == ns - 1)
        def _():
            if gated:
                o_ref[0] = acc[...] * g_ref[...]
                dg_ref[...] += _colsum(acc[...] * w_ref[...].astype(F32))
            else:
                o_ref[0] = acc[...]

    in_specs = [pl.BlockSpec((ts, tk), lambda jn, ik, st: (st, ik))]
    for mi in range(len(b_list)):
        in_specs.append(pl.BlockSpec(
            (ts, tn), lambda jn, ik, st, mi=mi: (st, jnp.clip(jn - mi * nb1, 0, nb1 - 1))))
    args = [a] + list(b_list)
    out_shape = [_sds((nb, kdim, tn), F32)]
    out_specs = [pl.BlockSpec((1, tk, tn), lambda jn, ik, st: (jn, ik, 0))]
    if gated:
        vec = pl.BlockSpec((1, tn), lambda jn, ik, st: (0, jn))
        in_specs += [vec, pl.BlockSpec((tk, tn), lambda jn, ik, st: (ik, jn)), vec]
        args += [gate, wfull, dgate_init]
        out_shape.append(_sds((1, nb * tn), F32))
        out_specs.append(vec)
    res = pl.pallas_call(
        body, grid=(nb, nk, ns), out_shape=tuple(out_shape), in_specs=in_specs, out_specs=tuple(out_specs),
        scratch_shapes=[pltpu.VMEM((tk, tn), F32)],
        name=name, compiler_params=_cp(("parallel", "arbitrary", "arbitrary")))(*args)
    return res if gated else res[0]


def _mm_normbwd(terms, x, dxres, gain, scale, name, ts_pref=256):
    s, d = x.shape
    ts = _tile(s, ts_pref)
    arrs, warrs = [], []
    for a, _, w, _ in terms:
        if not any(a is z for z in arrs):
            arrs.append(a)
        if not any(w is z for z in warrs):
            warrs.append(w)
    ai = [next(i for i, z in enumerate(arrs) if z is a) for a, _, _, _ in terms]
    wi = [next(i for i, z in enumerate(warrs) if z is w) for _, _, w, _ in terms]

    def body(*refs):
        a_refs = refs[:len(arrs)]
        w_refs = refs[len(arrs):len(arrs) + len(warrs)]
        x_ref, dr_ref, g_ref, sc_ref, dx_ref, dsh_ref, dsc_ref, dg_ref = refs[len(arrs) + len(warrs):]

        @pl.when(pl.program_id(0) == 0)
        def _():
            dsh_ref[...] = jnp.zeros_like(dsh_ref)
            dsc_ref[...] = jnp.zeros_like(dsc_ref)
            dg_ref[...] = jnp.zeros_like(dg_ref)

        dh = None
        for ti, (_, c0, w, q) in enumerate(terms):
            n = w.shape[2]
            part = _dot_nt(a_refs[ai[ti]][:, pl.ds(c0, n)], w_refs[wi[ti]][q])
            dh = part if dh is None else dh + part
        xhat, rstd = _rms_parts(x_ref[...])
        nrm = xhat * g_ref[...]
        dsh_ref[...] += _colsum(dh)
        dsc_ref[...] += _colsum(dh * nrm)
        dn = dh * (1.0 + sc_ref[...])
        dg_ref[...] += _colsum(dn * xhat)
        dxhat = dn * g_ref[...]
        dx_ref[...] = dr_ref[...] + rstd * (dxhat - xhat * jnp.mean(dxhat * xhat, axis=-1, keepdims=True))

    vec = pl.BlockSpec((1, d), lambda i: (0, 0))
    row = pl.BlockSpec((ts, d), lambda i: (i, 0))
    in_specs = [pl.BlockSpec((ts, a.shape[1]), lambda i: (i, 0)) for a in arrs]
    in_specs += [_resident(w.shape) for w in warrs]
    in_specs += [row, row, vec, vec]
    return pl.pallas_call(
        body, grid=(s // ts,), out_shape=(_sds((s, d), F32), _sds((1, d), F32), _sds((1, d), F32), _sds((1, d), F32)),
        in_specs=in_specs, out_specs=(row, vec, vec, vec),
        name=name, compiler_params=_cp(("arbitrary",)))(*arrs, *warrs, x, dxres, gain, scale)


def _do_kernel(dx, gate, wo, o):
    s, d = dx.shape
    ts = _tile(s, 512)

    def body(dx_ref, g_ref, w_ref, o_ref, do_ref, dl_ref):
        dy = (dx_ref[...] * g_ref[...]).astype(BF16)
        do = _dot_nt(dy, w_ref[...])
        do_ref[...] = do.astype(BF16)
        prod = do * o_ref[...].astype(F32)
        hrow = lax.broadcasted_iota(jnp.int32, (N_HEADS, d), 0)
        hcol = lax.broadcasted_iota(jnp.int32, (N_HEADS, d), 1) // HEAD_DIM
        sel = (hrow == hcol).astype(F32)
        dl_ref[...] = lax.dot_general(sel, prod, (((1,), (1,)), ((), ())), preferred_element_type=F32,
                                      precision=lax.Precision.HIGHEST)

    row = pl.BlockSpec((ts, d), lambda i: (i, 0))
    return pl.pallas_call(
        body, grid=(s // ts,), out_shape=(_sds((s, d), BF16), _sds((N_HEADS, s), F32)),
        in_specs=[row, pl.BlockSpec((1, d), lambda i: (0, 0)), _resident(wo.shape), row],
        out_specs=(row, pl.BlockSpec((N_HEADS, ts), lambda i: (0, i))),
        name="attn_do", compiler_params=_cp(("parallel",)))(dx, gate, wo, o)


def _attn_bwd(q, do, kaug, kaugt, v, cumt, lse, deltat):
    s, d = q.shape
    tq = _tile(s, ATT_TQ)
    tk = _tile(s, ATT_TK)
    assert tq % tk == 0
    npair = d // LANE
    nq = s // tq
    nkb = s // tk
    qscale = HEAD_DIM ** -0.5

    def body(q_ref, do_ref, ka_ref, kt_ref, v_ref, cumt_ref, lse_ref, dl_ref,
             dq_ref, dk_ref, dv_ref, dcq_ref, dck_ref, qaug, dom, rowv, dqt):
        p = pl.program_id(0)
        j = pl.program_id(1)
        lane = lax.broadcasted_iota(jnp.int32, (1, LANE), 1)
        lo = lane < HEAD_DIM
        r8 = lax.broadcasted_iota(jnp.int32, (8, 1), 0)

        @pl.when(j == 0)
        def _():
            dqt[...] = jnp.zeros_like(dqt)
            for c in range(nq):
                rows = pl.ds(c * tq, tq)
                qq = q_ref[rows, :]
                dd = do_ref[rows, :]
                cqt = cumt_ref[:, rows]
                dlt = dl_ref[:, rows]
                lst = lse_ref[0, :, rows]
                for hh in range(2):
                    qaug[hh, rows, :] = _q_aug(qq, lane, hh)
                    dom[hh, rows, :] = jnp.where(_head_mask(lane, hh), dd, jnp.zeros_like(dd))
                    rowv[hh, :, rows] = jnp.where(
                        r8 == 0, _pick_row(cqt, 2 * p + hh) - lst[hh:hh + 1, :],
                        jnp.where(r8 == 1, _pick_row(dlt, 2 * p + hh), 0.0))

        vv = v_ref[...]
        i0 = (j * tk) // tq

        def q_step(i, carry, diag):
            dv_acc, dk0, dk1 = carry
            qs = pl.multiple_of(i * tq, tq)
            if diag:
                krow = lax.broadcasted_iota(jnp.int32, (tk, tq), 0) + j * tk
                qcol = lax.broadcasted_iota(jnp.int32, (tk, tq), 1) + i * tq
                causal = krow <= qcol
            dks = [dk0, dk1]
            for hh in range(2):
                rv = rowv[hh, :, pl.ds(qs, tq)]
                qa = qaug[hh, pl.ds(qs, tq), :]
                dh = dom[hh, pl.ds(qs, tq), :]
                sc = _dot_nt(ka_ref[0, hh], qa)
                if diag:
                    sc = jnp.where(causal, sc, -jnp.inf)
                pt = jnp.exp(sc + rv[0:1, :])
                dpt = _dot_nt(vv, dh)
                dst = (pt * (dpt - rv[1:2, :])).astype(BF16)
                dv_acc = dv_acc + _dot(pt.astype(BF16), dh)
                dks[hh] = dks[hh] + _dot(dst, qa)
                dqt[hh, :, pl.ds(qs, tq)] += _dot(kt_ref[0, hh], dst)
            return dv_acc, dks[0], dks[1]

        z = jnp.zeros((tk, LANE), F32)
        carry = q_step(i0, (z, z, z), True)
        dv_acc, dk0, dk1 = lax.fori_loop(i0 + 1, nq, lambda i, cr: q_step(i, cr, False), carry)
        dv_ref[...] = dv_acc.astype(BF16)
        dk_ref[...] = jnp.where(lo, dk0, dk1).astype(BF16)
        dck_ref[0] = jnp.where(r8 == 0, dk0.T[SPARE[0]:SPARE[0] + 1, :],
                               jnp.where(r8 == 1, dk1.T[SPARE[1]:SPARE[1] + 1, :], 0.0))

        @pl.when(j == nkb - 1)
        def _():
            for c in range(nq):
                rows = pl.ds(c * tq, tq)
                a0 = dqt[0, :, rows].T
                a1 = dqt[1, :, rows].T
                dq_ref[rows, :] = (jnp.where(lo, a0, a1) * qscale).astype(BF16)
            r0, r1 = SPARE[0] + NPIECE, SPARE[1] + NPIECE
            dcq_ref[0] = jnp.where(r8 == 0, dqt[0, r0:r0 + 1, :], jnp.where(r8 == 1, dqt[1, r1:r1 + 1, :], 0.0))

    col = pl.BlockSpec((s, LANE), lambda p, j: (0, p), pipeline_mode=pl.Buffered(1))
    rows16 = pl.BlockSpec((N_HEADS, s), lambda p, j: (0, 0), pipeline_mode=pl.Buffered(1))
    blk = pl.BlockSpec((tk, LANE), lambda p, j: (j, p))
    return pl.pallas_call(
        body, grid=(npair, nkb),
        out_shape=(_sds((s, d), BF16), _sds((s, d), BF16), _sds((s, d), BF16), _sds((npair, 8, s), F32), _sds((npair, 8, s), F32)),
        in_specs=[col, col, pl.BlockSpec((1, 2, tk, LANE), lambda p, j: (p, 0, j, 0)),
                  pl.BlockSpec((1, 2, LANE, tk), lambda p, j: (p, 0, 0, j)), blk, rows16,
                  pl.BlockSpec((1, 8, s), lambda p, j: (p, 0, 0), pipeline_mode=pl.Buffered(1)), rows16],
        out_specs=(pl.BlockSpec((s, LANE), lambda p, j: (0, p)), blk, blk,
                   pl.BlockSpec((1, 8, s), lambda p, j: (p, 0, 0)), pl.BlockSpec((1, 8, tk), lambda p, j: (p, 0, j))),
        scratch_shapes=[pltpu.VMEM((2, s, LANE), BF16), pltpu.VMEM((2, s, LANE), BF16), pltpu.VMEM((2, 8, s), F32),
                        pltpu.VMEM((2, LANE, s), F32)],
        name="fox_attn_bwd", compiler_params=_cp(("arbitrary", "arbitrary"), ATT_BWD_VMEM_MB))(
            q, do, kaug, kaugt, v, cumt, lse, deltat)


def _cumsum_bwd(dcq, dck, flog, fb):
    s = flog.shape[0]
    ts = _tile(s, 256)
    nt = s // ts

    def body(dq_ref, dk_ref, f_ref, b_ref, df_ref, db_ref, carry):
        @pl.when(pl.program_id(0) == 0)
        def _():
            carry[...] = jnp.zeros_like(carry)
            db_ref[...] = jnp.zeros_like(db_ref)

        r = lax.broadcasted_iota(jnp.int32, (ts, ts), 0)
        cidx = lax.broadcasted_iota(jnp.int32, (ts, ts), 1)
        tri = (r >= cidx).astype(F32)
        dct = dq_ref[...] + dk_ref[...]
        dlst = jnp.dot(dct, tri, preferred_element_type=F32, precision=lax.Precision.HIGHEST) + carry[...]
        carry[...] = dlst[:, 0:1]
        dls = jnp.concatenate([dlst, jnp.zeros((LANE - N_HEADS, ts), F32)], axis=0).T
        z = f_ref[...] + b_ref[...]
        df = dls * (1.0 / (1.0 + jnp.exp(z)))
        db_ref[...] += _colsum(df)
        df_ref[...] = df.astype(BF16)

    rev = pl.BlockSpec((ts, LANE), lambda i: (nt - 1 - i, 0))
    revt = pl.BlockSpec((N_HEADS, ts), lambda i: (0, nt - 1 - i))
    vec = pl.BlockSpec((1, LANE), lambda i: (0, 0))
    return pl.pallas_call(
        body, grid=(nt,), out_shape=(_sds((s, LANE), BF16), _sds((1, LANE), F32)),
        in_specs=[revt, revt, rev, vec], out_specs=(rev, vec), scratch_shapes=[pltpu.VMEM((N_HEADS, 1), F32)],
        name="forget_cumsum_bwd", compiler_params=_cp(("arbitrary",)))(dcq, dck, flog, fb)


def _conv_bwd1(dx, gate, w_out, b_out, dwo, lng, lnb):
    s, d = dx.shape
    ts = _tile(s, 512)
    ns = s // ts

    def body(dx_ref, g_ref, w_ref, bo_ref, y_ref, lg_ref, lb_ref, dd_ref, dlg_ref, dlb_ref, dbd_ref, dbo_ref, dge_ref, cs):
        i = pl.program_id(0)

        @pl.when(i == 0)
        def _():
            for r in (dlg_ref, dlb_ref, dbd_ref, cs):
                r[...] = jnp.zeros_like(r)

        dxv = dx_ref[...]
        cs[...] += _colsum(dxv)
        dsw = _dot_nt((dxv * g_ref[...]).astype(BF16), w_ref[...])
        yv = y_ref[...]
        mu = jnp.mean(yv, axis=-1, keepdims=True)
        yc = yv - mu
        rstd = lax.rsqrt(jnp.mean(yc * yc, axis=-1, keepdims=True) + EPS)
        xhat = yc * rstd
        ln = xhat * lg_ref[...] + lb_ref[...]
        sg = _sigmoid(ln)
        dln = dsw * (sg * (1.0 + ln * (1.0 - sg)))
        dlg_ref[...] += _colsum(dln * xhat)
        dlb_ref[...] += _colsum(dln)
        dxh = dln * lg_ref[...]
        dd = rstd * (dxh - jnp.mean(dxh, axis=-1, keepdims=True) - xhat * jnp.mean(dxh * xhat, axis=-1, keepdims=True))
        dbd_ref[...] += _colsum(dd)
        dd_ref[...] = dd

        @pl.when(i == ns - 1)
        def _():
            dbo_ref[...] = g_ref[...] * cs[...]
            dge_ref[...] = bo_ref[...] * cs[...]

    vec = pl.BlockSpec((1, d), lambda i: (0, 0))
    row = pl.BlockSpec((ts, d), lambda i: (i, 0))
    return pl.pallas_call(
        body, grid=(ns,), out_shape=(_sds((s, d), F32),) + tuple(_sds((1, d), F32) for _ in range(5)),
        in_specs=[row, vec, _resident(w_out.shape), vec, row, vec, vec], out_specs=(row, vec, vec, vec, vec, vec),
        scratch_shapes=[pltpu.VMEM((1, d), F32)],
        name="conv_bwd_ln", compiler_params=_cp(("arbitrary",)))(dx, gate, w_out, b_out, dwo, lng, lnb)


def _dwconv_bwd(ddwo, glu, a_s, g_s, wdw):
    s, d = ddwo.shape
    ts = _tile(s, 256)
    ns = s // ts
    rb, cb = 32, 256
    nrb = ts // rb

    def body(dd_ref, ddn_ref, gl_ref, glh_ref, a_ref, g_ref, w_ref, da_ref, dg_ref, dw_ref, sa_ref, sg_ref, bufd, bufg, dws):
        i = pl.program_id(0)

        @pl.when(i == 0)
        def _():
            dws[...] = jnp.zeros_like(dws)
            sa_ref[...] = jnp.zeros_like(sa_ref)
            sg_ref[...] = jnp.zeros_like(sg_ref)
            bufg[pl.ds(0, HALO), :] = jnp.zeros((HALO, d), F32)

        @pl.when(i > 0)
        def _():
            bufg[pl.ds(0, HALO), :] = glh_ref[...]

        bufg[pl.ds(HALO, ts), :] = gl_ref[...]
        bufd[pl.ds(0, ts), :] = dd_ref[...]

        @pl.when(i == ns - 1)
        def _():
            bufd[pl.ds(ts, HALO), :] = jnp.zeros((HALO, d), F32)

        @pl.when(i < ns - 1)
        def _():
            bufd[pl.ds(ts, HALO), :] = ddn_ref[...]

        for cc in range(d // cb):
            cs = pl.ds(cc * cb, cb)
            for r in range(nrb):
                acc = jnp.zeros((rb, cb), F32)
                for k in range(CONV_K):
                    acc = acc + w_ref[pl.ds(k, 1), cs] * bufd[pl.ds(r * rb + (CONV_K - 1) - k, rb), cs]
                rows = pl.ds(r * rb, rb)
                av = a_ref[rows, cs].astype(F32)
                sg = _sigmoid(g_ref[rows, cs].astype(F32))
                dav = acc * sg
                dgv = acc * av * sg * (1.0 - sg)
                da_ref[rows, cs] = dav.astype(BF16)
                dg_ref[rows, cs] = dgv.astype(BF16)
                sa_ref[:, cs] += _colsum(dav)
                sg_ref[:, cs] += _colsum(dgv)
            for k in range(CONV_K):
                acc8 = jnp.zeros((8, cb), F32)
                for r in range(nrb):
                    prod = bufd[pl.ds(r * rb, rb), cs] * bufg[pl.ds(HALO - (CONV_K - 1) + k + r * rb, rb), cs]
                    acc8 = acc8 + (prod[0:8] + prod[8:16]) + (prod[16:24] + prod[24:32])
                dws[pl.ds(8 * k, 8), cs] += acc8

        @pl.when(i == ns - 1)
        def _():
            dw_ref[...] = jnp.zeros_like(dw_ref)
            for k in range(CONV_K):
                dw_ref[pl.ds(k, 1), :] = _colsum(dws[pl.ds(8 * k, 8), :])

    row = pl.BlockSpec((ts, d), lambda i: (i, 0))
    vec = pl.BlockSpec((1, d), lambda i: (0, 0))
    hb = ts // HALO
    return pl.pallas_call(
        body, grid=(ns,),
        out_shape=(_sds((s, d), BF16), _sds((s, d), BF16), _sds((HALO, d), F32), _sds((1, d), F32), _sds((1, d), F32)),
        in_specs=[row, pl.BlockSpec((HALO, d), lambda i: (jnp.minimum((i + 1) * hb, ns * hb - 1), 0)),
                  row, pl.BlockSpec((HALO, d), lambda i: (jnp.maximum(i * hb - 1, 0), 0)),
                  row, row, pl.BlockSpec((HALO, d), lambda i: (0, 0))],
        out_specs=(row, row, pl.BlockSpec((HALO, d), lambda i: (0, 0)), vec, vec),
        scratch_shapes=[pltpu.VMEM((ts + HALO, d), F32), pltpu.VMEM((HALO + ts, d), F32), pltpu.VMEM((8 * HALO, d), F32)],
        name="dwconv_bwd", compiler_params=_cp(("arbitrary",)))(ddwo, ddwo, glu, glu, a_s, g_s, wdw)


def _ada_wgrad(cat, da, name):
    nl, _, n = da.shape
    d = cat.shape[0]
    tn = 256

    def body(c_ref, d_ref, o_ref):
        acc = c_ref[:, 0:1] * d_ref[0, 0:1, :]
        for r in range(1, 8):
            acc = acc + c_ref[:, r:r + 1] * d_ref[0, r:r + 1, :]
        o_ref[0] = acc

    return pl.pallas_call(
        body, grid=(nl, n // tn), out_shape=_sds((nl, d, n), F32),
        in_specs=[pl.BlockSpec((d, 8), lambda l, j: (0, 0)), pl.BlockSpec((1, 8, tn), lambda l, j: (l, 0, j))],
        out_specs=pl.BlockSpec((1, d, tn), lambda l, j: (l, 0, j)),
        name=name, compiler_params=_cp(("parallel", "parallel")))(cat, da)


def _silu_rows(c_all):
    def body(c_ref, o_ref):
        cc = c_ref[...]
        o_ref[...] = cc * _sigmoid(cc)

    return pl.pallas_call(body, out_shape=_sds(c_all.shape, F32), name="silu_c")(c_all)


def _adamw(w, g, m, v, name):
    r, c = w.shape
    tr = r
    for cand in (512, 256, 128, 64, 32, 16, 8):
        if r % cand == 0 and cand * c * 4 <= (1 << 20):
            tr = cand
            break
    bc1 = 1.0 - ADAM_B1 ** ADAM_STEP
    bc2 = 1.0 - ADAM_B2 ** ADAM_STEP

    def body(w_ref, g_ref, m_ref, v_ref, d_ref, nm_ref, nv_ref):
        gv = g_ref[...]
        mn = ADAM_B1 * m_ref[...] + (1.0 - ADAM_B1) * gv
        vn = ADAM_B2 * v_ref[...] + (1.0 - ADAM_B2) * (gv * gv)
        mh = mn / bc1
        vh = vn / bc2
        d_ref[...] = -ADAM_LR * (mh / (jnp.sqrt(vh) + ADAM_EPS) + ADAM_WD * w_ref[...])
        nm_ref[...] = mn
        nv_ref[...] = vn

    blk = pl.BlockSpec((tr, c), lambda i: (i, 0))
    return pl.pallas_call(
        body, grid=(r // tr,), out_shape=tuple(_sds((r, c), F32) for _ in range(3)),
        in_specs=[blk, blk, blk, blk], out_specs=(blk, blk, blk),
        name=name, compiler_params=_cp(("parallel",)))(w, g, m, v)


def _halves(w2):
    r, c = w2.shape
    return w2.reshape(2, (r // 2) * c // 1024, 1024)


def _pad_rows(a, rows, axis):
    pad = [(0, 0)] * a.ndim
    pad[axis] = (0, rows - a.shape[axis])
    return jnp.pad(a, pad)


def _vec(a):
    return a.reshape(1, -1)


def kernel(x, c, mix_norm_g, mix_ada_w, mix_ada_b, ffn_norm_g, ffn_ada_w, ffn_ada_b, ffn_w_in, ffn_w_out, conv_w_in, conv_b_in, conv_w_dw, conv_b_dw, conv_ln_g, conv_ln_b, conv_w_out, conv_b_out, kv_norm_g, kv_ada_w, kv_ada_b, kv_w, forget_b, attn_w_q, attn_w_o, final_norm_g, loss_target, m_mix_norm_g, m_mix_ada_w, m_mix_ada_b, m_ffn_norm_g, m_ffn_ada_w, m_ffn_ada_b, m_ffn_w_in, m_ffn_w_out, m_conv_w_in, m_conv_b_in, m_conv_w_dw, m_conv_b_dw, m_conv_ln_g, m_conv_ln_b, m_conv_w_out, m_conv_b_out, m_kv_norm_g, m_kv_ada_w, m_kv_ada_b, m_kv_w, m_forget_b, m_attn_w_q, m_attn_w_o, m_final_norm_g, v_mix_norm_g, v_mix_ada_w, v_mix_ada_b, v_ffn_norm_g, v_ffn_ada_w, v_ffn_ada_b, v_ffn_w_in, v_ffn_w_out, v_conv_w_in, v_conv_b_in, v_conv_w_dw, v_conv_b_dw, v_conv_ln_g, v_conv_ln_b, v_conv_w_out, v_conv_b_out, v_kv_norm_g, v_kv_ada_w, v_kv_ada_b, v_kv_w, v_forget_b, v_attn_w_q, v_attn_w_o, v_final_norm_g):
    xi, yi, ci = lax.axis_index("x"), lax.axis_index("y"), lax.axis_index("c")
    chip = 2 * xi + yi
    dev = 4 * xi + 2 * yi + ci
    s, d = x.shape[1], x.shape[2]
    f = ffn_w_out.shape[1] * 4
    x0 = x[0]
    nkv = kv_w.shape[1]
    nkv_all = 4 * nkv

    wdw_loc = _pad_rows(conv_w_dw[0], HALO, 0)
    small = jnp.concatenate([c.reshape(-1), conv_b_in.reshape(-1), wdw_loc.reshape(-1), conv_b_dw.reshape(-1),
                             conv_ln_g.reshape(-1), conv_ln_b.reshape(-1), conv_b_out.reshape(-1)])
    n_small = small.shape[0]
    w_small = -(-n_small // (8 * LANE)) * LANE
    small = jnp.pad(small, (0, 8 * w_small - n_small)).reshape(8, w_small)
    small_all = _allgather8(small, "ag_small_params", True).reshape(8, 8 * w_small)
    c_all = small_all[:, :d]
    per_chip = small_all[0::2]
    dq_ = d // 4
    o1 = d
    b_in_full = per_chip[:, o1:o1 + 2 * dq_].reshape(4, 1, 2 * dq_)
    o1 += 2 * dq_
    wdw_full = per_chip[:, o1:o1 + HALO * dq_].reshape(4, HALO, dq_).transpose(1, 0, 2).reshape(HALO, d)
    o1 += HALO * dq_
    bdw_full = per_chip[:, o1:o1 + dq_].reshape(1, d)
    lng_full = per_chip[:, o1 + dq_:o1 + 2 * dq_].reshape(1, d)
    lnb_full = per_chip[:, o1 + 2 * dq_:o1 + 3 * dq_].reshape(1, d)
    bout_full = per_chip[:, o1 + 3 * dq_:o1 + 4 * dq_].reshape(1, d)

    a_mix = _ada_fwd(c_all, mix_ada_w, "ada_mix")
    a_ffn = _ada_fwd(c_all, ffn_ada_w, "ada_ffn")
    a_kv = _ada_fwd(c_all, kv_ada_w[None], "ada_kv")
    n3 = mix_ada_w.shape[2]
    n2 = kv_ada_w.shape[1]
    ada_loc = jnp.concatenate([a_mix[0], a_mix[1], a_ffn[0], a_ffn[1], a_kv[0]], axis=1)
    w_ada = ada_loc.shape[1]
    ada_all = _allgather8(ada_loc, "ag_ada", True).reshape(8, 8, w_ada)
    ada_me = lax.dynamic_index_in_dim(ada_all, dev, axis=1, keepdims=False)[0::2]

    def ada_vec(off, n, bias):
        return ada_me[:, off:off + n].reshape(1, 4 * n) + bias.reshape(1, -1)

    ada_m0 = ada_vec(0, n3, mix_ada_b[0])
    ada_m1 = ada_vec(n3, n3, mix_ada_b[1])
    ada_f0 = ada_vec(2 * n3, n3, ffn_ada_b[0])
    ada_f1 = ada_vec(3 * n3, n3, ffn_ada_b[1])
    ada_k = ada_vec(4 * n3, n2, kv_ada_b)

    def split3(a):
        return a[:, :d], a[:, d:2 * d], a[:, 2 * d:3 * d]

    sh_m0, sc_m0, gt_m0 = split3(ada_m0)
    sh_m1, sc_m1, gt_m1 = split3(ada_m1)
    sh_f0, sc_f0, gt_f0 = split3(ada_f0)
    sh_f1, sc_f1, gt_f1 = split3(ada_f1)
    sh_k, sc_k = ada_k[:, :d], ada_k[:, d:2 * d]

    big = [ffn_w_in[0], ffn_w_in[1], ffn_w_out[0], ffn_w_out[1], conv_w_in[0], conv_w_out[0], kv_w, attn_w_q[0], attn_w_o[0]]
    hrows = [(w.shape[0] // 2) * w.shape[1] // 1024 for w in big]
    prows = [-(-r // 16) * 16 for r in hrows]
    offs = [sum(prows[:i]) for i in range(len(big))]
    rtot = sum(prows)
    pack = jnp.concatenate([_pad_rows(_halves(w.astype(BF16)), pr, 1) for w, pr in zip(big, prows)], axis=1)
    mine = lax.dynamic_index_in_dim(pack, ci, axis=0, keepdims=False)
    gathered = _allgather8(mine, "ag_weights", False).reshape(4, 2, rtot, 1024)

    def full_w(i):
        w = big[i]
        return gathered[:, :, offs[i]:offs[i] + hrows[i], :].reshape(4, w.shape[0], w.shape[1])

    w_in = [full_w(0), full_w(1)]
    w_out = [full_w(2).reshape(f, d), full_w(3).reshape(f, d)]
    cw_in = full_w(4)
    cw_out = full_w(5).reshape(d, d)
    kvw = full_w(6).transpose(1, 0, 2).reshape(d, nkv_all)
    wk, wv = kvw[:, :d], kvw[:, d:2 * d]
    wf = jnp.pad(kvw[:, 2 * d:], ((0, 0), (0, LANE - N_HEADS)))
    wq = full_w(7).reshape(d, d)
    wo = full_w(8).reshape(d, d)

    zero_b = jnp.zeros((1, d), F32)
    g_m0, g_m1 = _vec(mix_norm_g[0]), _vec(mix_norm_g[1])
    g_f0, g_f1 = _vec(ffn_norm_g[0]), _vec(ffn_norm_g[1])
    g_k, g_fin = _vec(kv_norm_g), _vec(final_norm_g)
    fb = jnp.pad(forget_b, (0, LANE - N_HEADS)).reshape(1, LANE)

    h0, glu, a_s, g_s = _in_pair(x0, g_m0, sh_m0, sc_m0, cw_in, b_in_full, True, "conv_in")
    dwo, sw = _dwconv_fwd(glu, wdw_full, bdw_full, lng_full, lnb_full)
    x1 = _mm_res(sw, cw_out, bout_full, gt_m0, x0, "conv_out")
    hf0, act0, ug0, uu0 = _in_pair(x1, g_f0, sh_f0, sc_f0, w_in[0], None, False, "ffn0_in")
    x2 = _mm_res(act0, w_out[0], zero_b, gt_f0, x1, "ffn0_out")
    hk, h1, kk, vv, qq, flog = _qkv(x2, (g_k, sh_k, sc_k), (g_m1, sh_m1, sc_m1), wk, wv, wf, wq)
    cum, cumt = _cumsum_fwd(flog, fb)
    kaug, kaugt, vtr = _attn_prep(kk, vv, cum)
    o, lse = _attn_fwd(qq, kaug, vtr, cumt)
    x3 = _mm_res(o, wo, zero_b, gt_m1, x2, "attn_out")
    hf1, act1, ug1, uu1 = _in_pair(x3, g_f1, sh_f1, sc_f1, w_in[1], None, False, "ffn1_in")
    x4 = _mm_res(act1, w_out[1], zero_b, gt_f1, x3, "ffn1_out")
    lsum, dx4, d_gfin = _final(x4, g_fin, loss_target[0])
    loss = lax.psum(0.5 / d * jnp.sum(lsum), ("x", "y", "c"))

    nf = f // 2

    def ffn_bwd(dx_out, x_in, hf, act, ug, uu, gain, scale, gate, w_in_l, w_out_l, tag):
        dug, duu = _ffn_bwd_act(dx_out, gate, w_out_l, ug, uu)
        dw_out, dgate = _dw_mm(act, [dx_out], nf, d, tag + "_dw_out", gate=gate, wfull=w_out_l, dgate_init=zero_b)
        terms = [(dug, 0, w_in_l, 0), (dug, nf, w_in_l, 1), (duu, 0, w_in_l, 2), (duu, nf, w_in_l, 3)]
        dx_in, dsh, dsc, dgn = _mm_normbwd(terms, x_in, dx_out, gain, scale, tag + "_bwd_in")
        dw_in = _dw_mm(hf, [dug, duu], d, nf, tag + "_dw_in")
        return dx_in, dw_in, dw_out[0], dsh, dsc, dgate, dgn

    dx3, dw_in1, dw_out1, dsh_f1, dsc_f1, dgt_f1, dgn_f1 = ffn_bwd(dx4, x3, hf1, act1, ug1, uu1, g_f1, sc_f1, gt_f1, w_in[1], w_out[1], "ffn1")

    do, deltat = _do_kernel(dx3, gt_m1, wo, o)
    dwo_att, dgt_m1 = _dw_mm(o, [dx3], d, d, "attn_dw_o", gate=gt_m1, wfull=wo, dgate_init=zero_b)
    dq, dk, dv, dcq, dck = _attn_bwd(qq, do, kaug, kaugt, vv, cumt, lse, deltat)
    wq3 = wq.reshape(1, d, d)
    dx2a, dsh_m1, dsc_m1, dgn_m1 = _mm_normbwd([(dq, 0, wq3, 0)], x2, dx3, g_m1, sc_m1, "attn_bwd_q")
    dwq = _dw_mm(h1, [dq], d, d, "attn_dw_q")[0]

    df, dfb = _cumsum_bwd(dcq[:, :2].reshape(N_HEADS, s), dck[:, :2].reshape(N_HEADS, s), flog, fb)
    terms = [(dk, 0, wk.reshape(1, d, d), 0), (dv, 0, wv.reshape(1, d, d), 0), (df, 0, wf.reshape(1, d, LANE), 0)]
    dx2, dsh_k, dsc_k, dgn_k = _mm_normbwd(terms, x2, dx2a, g_k, sc_k, "kv_bwd")
    dwk = _dw_mm(hk, [dk], d, d, "kv_dw_k")[0]
    dwv = _dw_mm(hk, [dv], d, d, "kv_dw_v")[0]
    dwf = _dw_mm(hk, [df], d, LANE, "kv_dw_f")[0]
    dkvw = jnp.concatenate([dwk, dwv, dwf[:, :N_HEADS]], axis=1)
    dkvw = dkvw.reshape(d, 4, nkv).transpose(1, 0, 2)

    dx1, dw_in0, dw_out0, dsh_f0, dsc_f0, dgt_f0, dgn_f0 = ffn_bwd(dx2, x1, hf0, act0, ug0, uu0, g_f0, sc_f0, gt_f0, w_in[0], w_out[0], "ffn0")

    ddwo, d_lng, d_lnb, d_bdw, d_bout, dgt_extra = _conv_bwd1(dx1, gt_m0, cw_out, bout_full, dwo, lng_full, lnb_full)
    dcw_out, dgt_m0 = _dw_mm(sw, [dx1], d, d, "conv_dw_out", gate=gt_m0, wfull=cw_out, dgate_init=dgt_extra)
    da, dg, d_wdw, d_bin_a, d_bin_g = _dwconv_bwd(ddwo, glu, a_s, g_s, wdw_full)
    nc = cw_in.shape[2]
    terms = [(da, 0, cw_in, 0), (da, nc, cw_in, 1), (dg, 0, cw_in, 2), (dg, nc, cw_in, 3)]
    dx0, dsh_m0, dsc_m0, dgn_m0 = _mm_normbwd(terms, x0, dx1, g_m0, sc_m0, "conv_bwd_in")
    dcw_in = _dw_mm(h0, [da, dg], d, nc, "conv_dw_in")

    bigg = [dw_in0, dw_in1, dw_out0.reshape(4, f // 4, d), dw_out1.reshape(4, f // 4, d), dcw_in, dcw_out[0].reshape(4, d // 4, d),
            dkvw, dwq.reshape(4, d // 4, d), dwo_att[0].reshape(4, d // 4, d)]
    gpack = jnp.concatenate(
        [_pad_rows(g.reshape(4, 2, hr, 1024), pr, 2) for g, hr, pr in zip(bigg, hrows, prows)], axis=2)
    core_arr = jnp.reshape(ci, (1,)).astype(jnp.int32)
    chip_arr = jnp.reshape(chip, (1,)).astype(jnp.int32)
    land1 = _swap_halves(gpack)
    qsum = _add_halves(gpack, land1, core_arr)
    land2 = _scatter_chips(qsum)
    rsum = _add_chips(qsum, land2, chip_arr)
    gboth = _share_half(rsum)

    def shard_grad(i):
        w = big[i]
        return gboth[:, offs[i]:offs[i] + hrows[i], :].reshape(w.shape)

    d_ada = [jnp.concatenate([dsh_m0, dsc_m0, dgt_m0], axis=1), jnp.concatenate([dsh_m1, dsc_m1, dgt_m1], axis=1),
             jnp.concatenate([dsh_f0, dsc_f0, dgt_f0], axis=1), jnp.concatenate([dsh_f1, dsc_f1, dgt_f1], axis=1),
             jnp.concatenate([dsh_k, dsc_k], axis=1)]
    fields = d_ada + [dgn_m0, dgn_m1, dgn_f0, dgn_f1, dgn_k, d_gfin, d_bin_a, d_bin_g, d_bdw, d_lng, d_lnb, d_bout,
                      d_wdw.reshape(1, -1), dfb]
    foffs = [0]
    for fl in fields:
        foffs.append(foffs[-1] + fl.shape[1])
    n_row = foffs[-1]
    w_row = -(-n_row // (8 * LANE)) * LANE
    row = jnp.pad(jnp.concatenate(fields, axis=1), ((0, 0), (0, 8 * w_row - n_row))).reshape(8, w_row)
    rows_all = _allgather8(row, "ag_small_grads", True).reshape(8, 8, w_row)
    rsum_small = _sum8(rows_all).reshape(1, 8 * w_row)
    rows_flat = rows_all.reshape(8, 8 * w_row)

    def fsum(i):
        return rsum_small[:, foffs[i]:foffs[i + 1]]

    cat = _silu_rows(c_all).T

    def ada_cols(i, n):
        full = rows_flat[:, foffs[i]:foffs[i + 1]].reshape(8, 4, n)
        return lax.dynamic_index_in_dim(full, chip, axis=1, keepdims=False)

    g_mix_ada_w = _ada_wgrad(cat, jnp.stack([ada_cols(0, n3), ada_cols(1, n3)]), "ada_mix_wgrad")
    g_ffn_ada_w = _ada_wgrad(cat, jnp.stack([ada_cols(2, n3), ada_cols(3, n3)]), "ada_ffn_wgrad")
    g_kv_ada_w = _ada_wgrad(cat, ada_cols(4, n2)[None], "ada_kv_wgrad")[0]

    def my_cols(v, n):
        return lax.dynamic_index_in_dim(v.reshape(4, n), chip, axis=0, keepdims=False)

    grads = {
        "mix_norm_g": jnp.concatenate([fsum(5), fsum(6)], axis=0),
        "mix_ada_w": g_mix_ada_w,
        "mix_ada_b": jnp.concatenate([fsum(0), fsum(1)], axis=0),
        "ffn_norm_g": jnp.concatenate([fsum(7), fsum(8)], axis=0),
        "ffn_ada_w": g_ffn_ada_w,
        "ffn_ada_b": jnp.concatenate([fsum(2), fsum(3)], axis=0),
        "ffn_w_in": jnp.stack([shard_grad(0), shard_grad(1)]),
        "ffn_w_out": jnp.stack([shard_grad(2), shard_grad(3)]),
        "conv_w_in": shard_grad(4)[None],
        "conv_b_in": my_cols(jnp.concatenate([fsum(11), fsum(12)], axis=1), 2 * dq_)[None],
        "conv_w_dw": lax.dynamic_index_in_dim(fsum(17).reshape(HALO, 4, dq_), chip, axis=1, keepdims=False)[:CONV_K][None],
        "conv_b_dw": my_cols(fsum(13), dq_)[None],
        "conv_ln_g": my_cols(fsum(14), dq_)[None],
        "conv_ln_b": my_cols(fsum(15), dq_)[None],
        "conv_w_out": shard_grad(5)[None],
        "conv_b_out": my_cols(fsum(16), dq_)[None],
        "kv_norm_g": fsum(9).reshape(-1),
        "kv_ada_w": g_kv_ada_w,
        "kv_ada_b": fsum(4).reshape(-1),
        "kv_w": shard_grad(6),
        "forget_b": fsum(18).reshape(-1)[:N_HEADS],
        "attn_w_q": shard_grad(7)[None],
        "attn_w_o": shard_grad(8)[None],
        "final_norm_g": fsum(10).reshape(-1),
    }
    weights = dict(mix_norm_g=mix_norm_g, mix_ada_w=mix_ada_w, mix_ada_b=mix_ada_b, ffn_norm_g=ffn_norm_g, ffn_ada_w=ffn_ada_w, ffn_ada_b=ffn_ada_b, ffn_w_in=ffn_w_in, ffn_w_out=ffn_w_out, conv_w_in=conv_w_in, conv_b_in=conv_b_in, conv_w_dw=conv_w_dw, conv_b_dw=conv_b_dw, conv_ln_g=conv_ln_g, conv_ln_b=conv_ln_b, conv_w_out=conv_w_out, conv_b_out=conv_b_out, kv_norm_g=kv_norm_g, kv_ada_w=kv_ada_w, kv_ada_b=kv_ada_b, kv_w=kv_w, forget_b=forget_b, attn_w_q=attn_w_q, attn_w_o=attn_w_o, final_norm_g=final_norm_g)
    moms = dict(mix_norm_g=(m_mix_norm_g, v_mix_norm_g), mix_ada_w=(m_mix_ada_w, v_mix_ada_w), mix_ada_b=(m_mix_ada_b, v_mix_ada_b), ffn_norm_g=(m_ffn_norm_g, v_ffn_norm_g), ffn_ada_w=(m_ffn_ada_w, v_ffn_ada_w), ffn_ada_b=(m_ffn_ada_b, v_ffn_ada_b), ffn_w_in=(m_ffn_w_in, v_ffn_w_in), ffn_w_out=(m_ffn_w_out, v_ffn_w_out), conv_w_in=(m_conv_w_in, v_conv_w_in), conv_b_in=(m_conv_b_in, v_conv_b_in), conv_w_dw=(m_conv_w_dw, v_conv_w_dw), conv_b_dw=(m_conv_b_dw, v_conv_b_dw), conv_ln_g=(m_conv_ln_g, v_conv_ln_g), conv_ln_b=(m_conv_ln_b, v_conv_ln_b), conv_w_out=(m_conv_w_out, v_conv_w_out), conv_b_out=(m_conv_b_out, v_conv_b_out), kv_norm_g=(m_kv_norm_g, v_kv_norm_g), kv_ada_w=(m_kv_ada_w, v_kv_ada_w), kv_ada_b=(m_kv_ada_b, v_kv_ada_b), kv_w=(m_kv_w, v_kv_w), forget_b=(m_forget_b, v_forget_b), attn_w_q=(m_attn_w_q, v_attn_w_q), attn_w_o=(m_attn_w_o, v_attn_w_o), final_norm_g=(m_final_norm_g, v_final_norm_g))
    names = list(weights)

    deltas, new_m, new_v = {}, {}, {}
    small_names = [n for n in names if weights[n].size < (1 << 16)]
    for n in names:
        if n in small_names:
            continue
        w = weights[n]
        w2 = w.reshape(-1, w.shape[-1])
        dl, nm, nv = _adamw(w2, grads[n].reshape(w2.shape), moms[n][0].reshape(w2.shape), moms[n][1].reshape(w2.shape), "adamw_" + n)
        deltas[n], new_m[n], new_v[n] = dl.reshape(w.shape), nm.reshape(w.shape), nv.reshape(w.shape)

    def pack_small(get):
        flat = jnp.concatenate([get(n).reshape(-1) for n in small_names])
        rows_ = -(-flat.shape[0] // (8 * LANE)) * 8
        return jnp.pad(flat, (0, rows_ * LANE - flat.shape[0])).reshape(rows_, LANE)

    ws, gs = pack_small(lambda n: weights[n]), pack_small(lambda n: grads[n])
    ms_, vs_ = pack_small(lambda n: moms[n][0]), pack_small(lambda n: moms[n][1])
    vs_ = jnp.where(jnp.arange(vs_.size).reshape(vs_.shape) < sum(weights[n].size for n in small_names), vs_, 1.0)
    dl, nm, nv = _adamw(ws, gs, ms_, vs_, "adamw_small")
    off = 0
    for n in small_names:
        sz = weights[n].size
        shp = weights[n].shape
        deltas[n] = dl.reshape(-1)[off:off + sz].reshape(shp)
        new_m[n] = nm.reshape(-1)[off:off + sz].reshape(shp)
        new_v[n] = nv.reshape(-1)[off:off + sz].reshape(shp)
        off += sz

    grad_out = [grads[n].reshape(weights[n].shape) for n in names]
    return (loss, dx0[None], *grad_out, *[deltas[n] for n in names], *[new_m[n] for n in names], *[new_v[n] for n in names])
```

```python
import functools

import jax
import jax.numpy as jnp
from jax import lax
from jax.experimental import pallas as pl
from jax.experimental.pallas import tpu as pltpu

F32 = jnp.float32
BF16 = jnp.bfloat16
MESH = pl.DeviceIdType.MESH

EPS = 1e-6
N_HEADS = 16
HEAD_DIM = 64
CONV_K = 31
LANE = 128
HALO = 32
ATT_TQ = 1024
ATT_TK = 512
NPIECE = 3
SPARE = (HEAD_DIM, 0)
VMEM_MB = 48
ATT_BWD_VMEM_MB = 56

ADAM_LR = 0.001
ADAM_B1 = 0.9
ADAM_B2 = 0.999
ADAM_EPS = 1e-08
ADAM_WD = 0.01
ADAM_STEP = 10


def _sds(shape, dtype):
    return jax.ShapeDtypeStruct(tuple(shape), dtype)


def _cp(sem=None, vmem_mb=VMEM_MB):
    return pltpu.CompilerParams(dimension_semantics=sem, vmem_limit_bytes=vmem_mb << 20)


def _tile(n, pref):
    return pref if n % pref == 0 else n


def _row_tile(r, mult):
    for cand in range(512 // mult * mult, mult - 1, -mult):
        if r % cand == 0:
            return cand
    return r


def _resident(shape):
    nd = len(shape)
    return pl.BlockSpec(tuple(shape), lambda *_: (0,) * nd, pipeline_mode=pl.Buffered(1))


def _dot(a, b):
    return jnp.dot(a, b, preferred_element_type=F32)


def _dot_nt(a, b):
    return lax.dot_general(a, b, (((1,), (1,)), ((), ())), preferred_element_type=F32)


def _dot_tn(a, b):
    return lax.dot_general(a, b, (((0,), (0,)), ((), ())), preferred_element_type=F32)


def _sigmoid(x):
    return 1.0 / (1.0 + jnp.exp(-x))


def _colsum(x):
    return jnp.sum(x, axis=0, keepdims=True)


def _rms_parts(x):
    rstd = lax.rsqrt(jnp.mean(x * x, axis=-1, keepdims=True) + EPS)
    return x * rstd, rstd


def _allgather8(x_shard, name, in_vmem):
    m_per, n = x_shard.shape

    def body(x_ref, out_ref, send_sems, recv_sems, local_sem):
        x, y, c = lax.axis_index("x"), lax.axis_index("y"), lax.axis_index("c")
        me, sibling = (x, y, c), (x, y, 1 - c)
        chips = [(1 - x, y), (x, 1 - y), (1 - x, 1 - y)]

        def rows(px, py, pc):
            return out_ref.at[pl.ds((4 * px + 2 * py + pc) * m_per, m_per), :]

        def copy(k, block, to, src=None):
            return pltpu.make_async_remote_copy(
                src_ref=rows(*block) if src is None else src, dst_ref=rows(*block),
                send_sem=send_sems.at[k], recv_sem=recv_sems.at[k], device_id=to, device_id_type=MESH)

        mine = pltpu.make_async_copy(x_ref, rows(*me), local_sem)
        mine.start()
        first = [copy(0, me, sibling, src=x_ref)]
        first += [copy(1 + j, me, (*chip, c), src=x_ref) for j, chip in enumerate(chips)]
        for cp in first:
            cp.start()
        passed = [copy(4 + j, (*chip, c), sibling) for j, chip in enumerate(chips)]
        for j, chip in enumerate(chips):
            copy(1 + j, (*chip, c), me).wait_recv()
            passed[j].start()
        copy(0, sibling, me).wait_recv()
        for j, chip in enumerate(chips):
            copy(4 + j, (*chip, 1 - c), me).wait_recv()
        for cp in first + passed:
            cp.wait_send()
        mine.wait()

    space = pltpu.VMEM if in_vmem else pl.ANY
    return pl.pallas_call(
        body, out_shape=_sds((8 * m_per, n), x_shard.dtype),
        in_specs=[pl.BlockSpec(memory_space=space)], out_specs=pl.BlockSpec(memory_space=space),
        scratch_shapes=[pltpu.SemaphoreType.DMA((7,)), pltpu.SemaphoreType.DMA((7,)), pltpu.SemaphoreType.DMA],
        name=name)(x_shard)


def _swap_halves(p):
    nb, _, r, w = p.shape

    def body(p_ref, land_ref, send_sems, recv_sems):
        x, y, c = lax.axis_index("x"), lax.axis_index("y"), lax.axis_index("c")
        copies = [pltpu.make_async_remote_copy(
            src_ref=p_ref.at[j, 1 - c], dst_ref=land_ref.at[j], send_sem=send_sems.at[j], recv_sem=recv_sems.at[j],
            device_id=(x, y, 1 - c), device_id_type=MESH) for j in range(nb)]
        for cp in copies:
            cp.start()
        for cp in copies:
            cp.wait_recv()
        for cp in copies:
            cp.wait_send()

    return pl.pallas_call(
        body, out_shape=_sds((nb, r, w), p.dtype),
        in_specs=[pl.BlockSpec(memory_space=pl.ANY)], out_specs=pl.BlockSpec(memory_space=pl.ANY),
        scratch_shapes=[pltpu.SemaphoreType.DMA((nb,)), pltpu.SemaphoreType.DMA((nb,))],
        name="rs_swap_halves")(p)


def _scatter_chips(q):
    _, r, w = q.shape

    def body(q_ref, land_ref, send_sems, recv_sems):
        x, y, c = lax.axis_index("x"), lax.axis_index("y"), lax.axis_index("c")
        chips = [(1 - x, y), (x, 1 - y), (1 - x, 1 - y)]
        copies = [pltpu.make_async_remote_copy(
            src_ref=q_ref.at[2 * cx + cy], dst_ref=land_ref.at[k], send_sem=send_sems.at[k], recv_sem=recv_sems.at[k],
            device_id=(cx, cy, c), device_id_type=MESH) for k, (cx, cy) in enumerate(chips)]
        for cp in copies:
            cp.start()
        for cp in copies:
            cp.wait_recv()
        for cp in copies:
            cp.wait_send()

    return pl.pallas_call(
        body, out_shape=_sds((3, r, w), q.dtype),
        in_specs=[pl.BlockSpec(memory_space=pl.ANY)], out_specs=pl.BlockSpec(memory_space=pl.ANY),
        scratch_shapes=[pltpu.SemaphoreType.DMA((3,)), pltpu.SemaphoreType.DMA((3,))],
        name="rs_scatter_chips")(q)


def _share_half(buf):
    def body(b_ref, out_ref, send_sem, recv_sem):
        x, y, c = lax.axis_index("x"), lax.axis_index("y"), lax.axis_index("c")
        cp = pltpu.make_async_remote_copy(
            src_ref=b_ref.at[c], dst_ref=out_ref.at[c], send_sem=send_sem, recv_sem=recv_sem,
            device_id=(x, y, 1 - c), device_id_type=MESH)
        cp.start()
        cp.wait_recv()
        cp.wait_send()

    return pl.pallas_call(
        body, out_shape=_sds(buf.shape, buf.dtype),
        in_specs=[pl.BlockSpec(memory_space=pl.ANY)], out_specs=pl.BlockSpec(memory_space=pl.ANY),
        scratch_shapes=[pltpu.SemaphoreType.DMA, pltpu.SemaphoreType.DMA],
        input_output_aliases={0: 0}, name="rs_share_half")(buf)


def _add_halves(p, land, sel):
    nb, _, r, w = p.shape
    tr = _row_tile(r, 16)

    def body(sel_ref, p_ref, l_ref, q16_ref, own_ref):
        q = p_ref[0, 0] + l_ref[0]
        q16_ref[0] = q.astype(BF16)

        @pl.when(pl.program_id(1) == sel_ref[1])
        def _():
            own_ref[...] = q

    gs = pltpu.PrefetchScalarGridSpec(
        num_scalar_prefetch=1, grid=(r // tr, nb),
        in_specs=[pl.BlockSpec((1, 1, tr, w), lambda i, j, sl: (j, sl[0], i, 0)),
                  pl.BlockSpec((1, tr, w), lambda i, j, sl: (j, i, 0))],
        out_specs=(pl.BlockSpec((1, tr, w), lambda i, j, sl: (j, i, 0)), pl.BlockSpec((tr, w), lambda i, j, sl: (i, 0))))
    return pl.pallas_call(body, grid_spec=gs, out_shape=(_sds((nb, r, w), BF16), _sds((r, w), F32)), name="rs_add_halves",
                          compiler_params=_cp(("parallel", "arbitrary")))(sel, p, land)


def _add_chips(own, land, sel):
    r, w = own.shape
    tr = _row_tile(r, 16)

    def body(sel_ref, q_ref, l_ref, o_ref):
        o_ref[0] = ((q_ref[...] + l_ref[0].astype(F32)) + l_ref[1].astype(F32)) + l_ref[2].astype(F32)

    gs = pltpu.PrefetchScalarGridSpec(
        num_scalar_prefetch=1, grid=(r // tr,),
        in_specs=[pl.BlockSpec((tr, w), lambda i, sl: (i, 0)), pl.BlockSpec((3, tr, w), lambda i, sl: (0, i, 0))],
        out_specs=pl.BlockSpec((1, tr, w), lambda i, sl: (sl[0], i, 0)))
    return pl.pallas_call(body, grid_spec=gs, out_shape=_sds((2, r, w), F32), name="rs_add_chips",
                          compiler_params=_cp(("parallel",)))(sel, own, land)


def _sum8(g):
    _, m, n = g.shape

    def body(g_ref, o_ref):
        acc = g_ref[0]
        for k in range(1, 8):
            acc = acc + g_ref[k]
        o_ref[...] = acc

    return pl.pallas_call(body, out_shape=_sds((m, n), g.dtype), name="sum8")(g)


def _ada_fwd(c_all, w3, name):
    nl, d, n = w3.shape
    tn = 256

    def body(c_ref, w_ref, o_ref):
        cc = c_ref[...]
        ca = (cc * _sigmoid(cc)).astype(BF16)
        o_ref[0] = _dot(ca, w_ref[0].astype(BF16))

    return pl.pallas_call(
        body, grid=(nl, n // tn), out_shape=_sds((nl, 8, n), F32),
        in_specs=[pl.BlockSpec((8, d), lambda l, j: (0, 0)), pl.BlockSpec((1, d, tn), lambda l, j: (l, 0, j))],
        out_specs=pl.BlockSpec((1, 8, tn), lambda l, j: (l, 0, j)),
        name=name, compiler_params=_cp(("parallel", "parallel")))(c_all, w3)


def _in_pair(x, gain, shift, scale, wg, bias, conv, name):
    s, d = x.shape
    n = wg.shape[2]
    ts = _tile(s, 512)

    def body(*refs):
        if conv:
            x_ref, g_ref, sh_ref, sc_ref, wa_ref, wb_ref, ba_ref, bb_ref, h_ref, o_ref, sa_ref, sb_ref, hs = refs
        else:
            x_ref, g_ref, sh_ref, sc_ref, wa_ref, wb_ref, h_ref, o_ref, sa_ref, sb_ref, hs = refs

        @pl.when(pl.program_id(1) == 0)
        def _():
            xhat, _ = _rms_parts(x_ref[...])
            h = (xhat * g_ref[...]) * (1.0 + sc_ref[...]) + sh_ref[...]
            hs[...] = h.astype(BF16)
            h_ref[...] = hs[...]

        h = hs[...]
        a = _dot(h, wa_ref[0])
        b = _dot(h, wb_ref[0])
        if conv:
            a = a + ba_ref[0]
            b = b + bb_ref[0]
            o_ref[...] = a * _sigmoid(b)
        else:
            o_ref[...] = (a * _sigmoid(a) * b).astype(BF16)
        sa_ref[...] = a.astype(BF16)
        sb_ref[...] = b.astype(BF16)

    vec = pl.BlockSpec((1, d), lambda i, q: (0, 0))
    in_specs = [pl.BlockSpec((ts, d), lambda i, q: (i, 0)), vec, vec, vec,
                pl.BlockSpec((1, d, n), lambda i, q: (q, 0, 0)), pl.BlockSpec((1, d, n), lambda i, q: (q + 2, 0, 0))]
    args = [x, gain, shift, scale, wg, wg]
    if conv:
        in_specs += [pl.BlockSpec((1, 1, n), lambda i, q: (q, 0, 0)), pl.BlockSpec((1, 1, n), lambda i, q: (q + 2, 0, 0))]
        args += [bias, bias]
    tile = pl.BlockSpec((ts, n), lambda i, q: (i, q))
    return pl.pallas_call(
        body, grid=(s // ts, 2),
        out_shape=(_sds((s, d), BF16), _sds((s, 2 * n), F32 if conv else BF16), _sds((s, 2 * n), BF16), _sds((s, 2 * n), BF16)),
        in_specs=in_specs, out_specs=(pl.BlockSpec((ts, d), lambda i, q: (i, 0)), tile, tile, tile),
        scratch_shapes=[pltpu.VMEM((ts, d), BF16)],
        name=name, compiler_params=_cp(("parallel", "arbitrary")))(*args)


def _dwconv_fwd(glu, wdw, bdw, lng, lnb):
    s, d = glu.shape
    ts = _tile(s, 256)
    rb, cb = 32, 256

    def body(cur_ref, halo_ref, w_ref, b_ref, g_ref, be_ref, dwo_ref, sw_ref, buf):
        i = pl.program_id(0)

        @pl.when(i == 0)
        def _():
            buf[pl.ds(0, HALO), :] = jnp.zeros((HALO, d), F32)

        @pl.when(i > 0)
        def _():
            buf[pl.ds(0, HALO), :] = halo_ref[...]

        buf[pl.ds(HALO, ts), :] = cur_ref[...]
        for r in range(ts // rb):
            for cc in range(d // cb):
                cs = pl.ds(cc * cb, cb)
                acc = jnp.zeros((rb, cb), F32) + b_ref[:, cs]
                for k in range(CONV_K):
                    acc = acc + w_ref[pl.ds(k, 1), cs] * buf[pl.ds(HALO - (CONV_K - 1) + k + r * rb, rb), cs]
                dwo_ref[pl.ds(r * rb, rb), cs] = acc
            rows = pl.ds(r * rb, rb)
            yv = dwo_ref[rows, :]
            mu = jnp.mean(yv, axis=-1, keepdims=True)
            yc = yv - mu
            var = jnp.mean(yc * yc, axis=-1, keepdims=True)
            ln = yc * lax.rsqrt(var + EPS) * g_ref[...] + be_ref[...]
            sw_ref[rows, :] = (ln * _sigmoid(ln)).astype(BF16)

    vec = pl.BlockSpec((1, d), lambda i: (0, 0))
    return pl.pallas_call(
        body, grid=(s // ts,), out_shape=(_sds((s, d), F32), _sds((s, d), BF16)),
        in_specs=[pl.BlockSpec((ts, d), lambda i: (i, 0)),
                  pl.BlockSpec((HALO, d), lambda i: (jnp.maximum(i * (ts // HALO) - 1, 0), 0)),
                  pl.BlockSpec((HALO, d), lambda i: (0, 0)), vec, vec, vec],
        out_specs=(pl.BlockSpec((ts, d), lambda i: (i, 0)), pl.BlockSpec((ts, d), lambda i: (i, 0))),
        scratch_shapes=[pltpu.VMEM((HALO + ts, d), F32)],
        name="dwconv_fwd", compiler_params=_cp(("parallel",)))(glu, glu, wdw, bdw, lng, lnb)


def _mm_res(a, w, b, gate, x, name):
    s, k = a.shape
    d = w.shape[1]
    ts = _tile(s, 512)

    def body(a_ref, w_ref, b_ref, g_ref, x_ref, o_ref):
        yv = _dot(a_ref[...], w_ref[...]) + b_ref[...]
        o_ref[...] = x_ref[...] + g_ref[...] * yv

    vec = pl.BlockSpec((1, d), lambda i: (0, 0))
    return pl.pallas_call(
        body, grid=(s // ts,), out_shape=_sds((s, d), F32),
        in_specs=[pl.BlockSpec((ts, k), lambda i: (i, 0)), _resident((k, d)), vec, vec, pl.BlockSpec((ts, d), lambda i: (i, 0))],
        out_specs=pl.BlockSpec((ts, d), lambda i: (i, 0)),
        name=name, compiler_params=_cp(("parallel",)))(a, w, b, gate, x)


def _qkv(x, kvp, mxp, wk, wv, wf, wq):
    s, d = x.shape
    ts = _tile(s, 512)
    qscale = HEAD_DIM ** -0.5

    def body(x_ref, gk, shk, sck, gm, shm, scm, wk_ref, wv_ref, wf_ref, wq_ref, hk_ref, h1_ref, k_ref, v_ref, q_ref, f_ref):
        xhat, _ = _rms_parts(x_ref[...])
        hk = ((xhat * gk[...]) * (1.0 + sck[...]) + shk[...]).astype(BF16)
        h1 = ((xhat * gm[...]) * (1.0 + scm[...]) + shm[...]).astype(BF16)
        hk_ref[...] = hk
        h1_ref[...] = h1
        k_ref[...] = _dot(hk, wk_ref[...]).astype(BF16)
        v_ref[...] = _dot(hk, wv_ref[...]).astype(BF16)
        f_ref[...] = _dot(hk, wf_ref[...])
        q_ref[...] = (_dot(h1, wq_ref[...]) * qscale).astype(BF16)

    vec = pl.BlockSpec((1, d), lambda i: (0, 0))
    row = pl.BlockSpec((ts, d), lambda i: (i, 0))
    return pl.pallas_call(
        body, grid=(s // ts,),
        out_shape=tuple(_sds((s, d), BF16) for _ in range(5)) + (_sds((s, LANE), F32),),
        in_specs=[row, vec, vec, vec, vec, vec, vec, _resident((d, d)), _resident((d, d)), _resident((d, LANE)), _resident((d, d))],
        out_specs=(row, row, row, row, row, pl.BlockSpec((ts, LANE), lambda i: (i, 0))),
        name="qkv_proj", compiler_params=_cp(("parallel",)))(x, *kvp, *mxp, wk, wv, wf, wq)


def _log_sigmoid(z):
    return jnp.minimum(z, 0.0) - jnp.log(1.0 + jnp.exp(-jnp.abs(z)))


def _cumsum_fwd(flog, fb):
    s = flog.shape[0]
    ts = _tile(s, 256)

    def body(f_ref, b_ref, cum_ref, cumt_ref, carry):
        @pl.when(pl.program_id(0) == 0)
        def _():
            carry[...] = jnp.zeros_like(carry)

        ls = _log_sigmoid(f_ref[...] + b_ref[...])
        r = lax.broadcasted_iota(jnp.int32, (ts, ts), 0)
        cidx = lax.broadcasted_iota(jnp.int32, (ts, ts), 1)
        tri = (cidx <= r).astype(F32)
        cs = jnp.dot(tri, ls, preferred_element_type=F32, precision=lax.Precision.HIGHEST) + carry[...]
        cum_ref[...] = cs
        cumt_ref[...] = cs.T
        carry[...] = cs[ts - 1:ts, :]

    return pl.pallas_call(
        body, grid=(s // ts,), out_shape=(_sds((s, LANE), F32), _sds((LANE, s), F32)),
        in_specs=[pl.BlockSpec((ts, LANE), lambda i: (i, 0)), pl.BlockSpec((1, LANE), lambda i: (0, 0))],
        out_specs=(pl.BlockSpec((ts, LANE), lambda i: (i, 0)), pl.BlockSpec((LANE, ts), lambda i: (0, i))),
        scratch_shapes=[pltpu.VMEM((1, LANE), F32)],
        name="forget_cumsum", compiler_params=_cp(("arbitrary",)))(flog, fb)


def _pick_row(m, idx):
    r = lax.broadcasted_iota(jnp.int32, (m.shape[0], 1), 0)
    return jnp.sum(jnp.where(r == idx, m, 0.0), axis=0, keepdims=True)


def _pick_col(m, idx):
    cidx = lax.broadcasted_iota(jnp.int32, (1, m.shape[1]), 1)
    return jnp.sum(jnp.where(cidx == idx, m, 0.0), axis=1, keepdims=True)


def _split3(x):
    hi = x.astype(BF16)
    r1 = x - hi.astype(F32)
    mid = r1.astype(BF16)
    lo = (r1 - mid.astype(F32)).astype(BF16)
    return hi, mid, lo


def _head_mask(lane, hh):
    lo = lane < HEAD_DIM
    return lo if hh == 0 else jnp.logical_not(lo)


def _attn_prep(k, v, cum):
    s, d = k.shape
    npair = d // LANE
    tc = _tile(s, 512)

    def body(k_ref, v_ref, c_ref, ka_ref, kt_ref, vt_ref):
        p = pl.program_id(0)
        lane = lax.broadcasted_iota(jnp.int32, (1, LANE), 1)
        kk = k_ref[...]
        vv = v_ref[...].astype(F32)
        ckt = c_ref[...]
        for hh in range(2):
            head = _head_mask(lane, hh)
            b = SPARE[hh]
            ck = _pick_col(ckt, 2 * p + hh)
            extra = jnp.where(lane == b + NPIECE, 1.0, 0.0).astype(BF16) + jnp.zeros((tc, LANE), BF16)
            for n_, pc in enumerate(_split3(ck)):
                extra = jnp.where(lane == b + n_, pc, extra)
            ka = jnp.where(head, kk, extra)
            ka_ref[0, hh] = ka
            kt_ref[0, hh] = ka.astype(F32).T.astype(BF16)
            vx = jnp.where(head, vv, jnp.where(lane == b, 1.0, 0.0))
            vt_ref[0, hh] = vx.T.astype(BF16)

    blk = pl.BlockSpec((tc, LANE), lambda p, c: (c, p))
    return pl.pallas_call(
        body, grid=(npair, s // tc),
        out_shape=(_sds((npair, 2, s, LANE), BF16), _sds((npair, 2, LANE, s), BF16), _sds((npair, 2, LANE, s), BF16)),
        in_specs=[blk, blk, pl.BlockSpec((tc, LANE), lambda p, c: (c, 0))],
        out_specs=(pl.BlockSpec((1, 2, tc, LANE), lambda p, c: (p, 0, c, 0)),
                   pl.BlockSpec((1, 2, LANE, tc), lambda p, c: (p, 0, 0, c)),
                   pl.BlockSpec((1, 2, LANE, tc), lambda p, c: (p, 0, 0, c))),
        name="fox_attn_prep", compiler_params=_cp(("parallel", "parallel")))(k, v, cum)


def _q_aug(qq, lane, hh):
    b = SPARE[hh]
    sel = jnp.logical_and(lane >= b, lane < b + NPIECE)
    neg = jnp.full((1, LANE), -1.0, BF16)
    zl = jnp.zeros((1, LANE), BF16)
    return jnp.where(_head_mask(lane, hh), qq, jnp.where(sel, neg, zl))


def _attn_fwd(q, kaug, vtr, cumt):
    s, d = q.shape
    tq = _tile(s, ATT_TQ)
    tk = _tile(s, ATT_TK)
    npair = d // LANE
    npart = max(1, tq // tk)

    def body(q_ref, ka_ref, vt_ref, cumt_ref, o_ref, lse_ref):
        p = pl.program_id(0)
        i = pl.program_id(1)
        lane = lax.broadcasted_iota(jnp.int32, (1, LANE), 1)
        qq = q_ref[...]
        qx = (_q_aug(qq, lane, 0), _q_aug(qq, lane, 1))
        cqt = cumt_ref[:, pl.ds(pl.multiple_of(i * tq, tq), tq)]
        cq = (_pick_row(cqt, 2 * p), _pick_row(cqt, 2 * p + 1))
        jd = (i * tq) // tk

        def kv_step(j, carry, diag):
            ks = pl.multiple_of(j * tk, tk)
            if diag:
                krow = lax.broadcasted_iota(jnp.int32, (tk, tq), 0) + j * tk
                qcol = lax.broadcasted_iota(jnp.int32, (tk, tq), 1) + i * tq
                causal = krow <= qcol
            out = []
            for hh in range(2):
                m, acc = carry[2 * hh], carry[2 * hh + 1]
                sc = _dot_nt(ka_ref[0, hh, pl.ds(ks, tk), :], qx[hh])
                if diag:
                    sc = jnp.where(causal, sc, -jnp.inf)
                mx = jnp.max(sc, axis=0, keepdims=True) + cq[hh]
                mn = jnp.maximum(m, mx)
                alpha = jnp.exp(m - mn)
                pt = jnp.exp(sc + (cq[hh] - mn)).astype(BF16)
                acc = alpha * acc + _dot(vt_ref[0, hh, :, pl.ds(ks, tk)], pt)
                out += [mn, acc]
            return tuple(out)

        minit = jnp.full((1, tq), -jnp.inf, F32)
        ainit = jnp.zeros((LANE, tq), F32)
        carry = (minit, ainit, minit, ainit)
        for pj in range(npart):
            carry = kv_step(jd + pj, carry, True)
        carry = lax.fori_loop(0, jd, lambda j, cr: kv_step(j, cr, False), carry)
        m0, a0, m1, a1 = carry
        l0 = a0[SPARE[0]:SPARE[0] + 1, :]
        l1 = a1[SPARE[1]:SPARE[1] + 1, :]
        row = lax.broadcasted_iota(jnp.int32, (LANE, 1), 0)
        ot = jnp.where(row < HEAD_DIM, a0 / l0, a1 / l1)
        o_ref[...] = ot.T.astype(BF16)
        r8 = lax.broadcasted_iota(jnp.int32, (8, 1), 0)
        lse_ref[0] = jnp.where(r8 == 0, m0 + jnp.log(l0), jnp.where(r8 == 1, m1 + jnp.log(l1), 0.0))

    return pl.pallas_call(
        body, grid=(npair, s // tq), out_shape=(_sds((s, d), BF16), _sds((npair, 8, s), F32)),
        in_specs=[pl.BlockSpec((tq, LANE), lambda p, i: (i, p)),
                  pl.BlockSpec((1, 2, s, LANE), lambda p, i: (p, 0, 0, 0)),
                  pl.BlockSpec((1, 2, LANE, s), lambda p, i: (p, 0, 0, 0)),
                  pl.BlockSpec((N_HEADS, s), lambda p, i: (0, 0))],
        out_specs=(pl.BlockSpec((tq, LANE), lambda p, i: (i, p)), pl.BlockSpec((1, 8, tq), lambda p, i: (p, 0, i))),
        name="fox_attn_fwd", compiler_params=_cp(("parallel", "parallel")))(q, kaug, vtr, cumt)


def _final(x, gain, target):
    s, d = x.shape
    ts = _tile(s, 512)

    def body(x_ref, g_ref, t_ref, lsum_ref, dx_ref, dg_ref):
        @pl.when(pl.program_id(0) == 0)
        def _():
            lsum_ref[...] = jnp.zeros_like(lsum_ref)
            dg_ref[...] = jnp.zeros_like(dg_ref)

        xhat, rstd = _rms_parts(x_ref[...])
        e = xhat * g_ref[...] - t_ref[...]
        lsum_ref[...] += _colsum(e * e)
        dout = e * (1.0 / d)
        dg_ref[...] += _colsum(dout * xhat)
        dxhat = dout * g_ref[...]
        dx_ref[...] = rstd * (dxhat - xhat * jnp.mean(dxhat * xhat, axis=-1, keepdims=True))

    vec = pl.BlockSpec((1, d), lambda i: (0, 0))
    row = pl.BlockSpec((ts, d), lambda i: (i, 0))
    return pl.pallas_call(
        body, grid=(s // ts,), out_shape=(_sds((1, d), F32), _sds((s, d), F32), _sds((1, d), F32)),
        in_specs=[row, vec, row], out_specs=(vec, row, vec),
        name="final_norm_loss", compiler_params=_cp(("arbitrary",)))(x, gain, target)


def _ffn_bwd_act(dx, gate, w_out, ug, uu):
    s, d = dx.shape
    f = w_out.shape[0]
    n = f // 2
    ts = _tile(s, 512)

    def body(dx_ref, g_ref, w_ref, ug_ref, uu_ref, dug_ref, duu_ref, dys):
        @pl.when(pl.program_id(1) == 0)
        def _():
            dys[...] = (dx_ref[...] * g_ref[...]).astype(BF16)

        dact = _dot_nt(dys[...], w_ref[...])
        g = ug_ref[...].astype(F32)
        u = uu_ref[...].astype(F32)
        sg = _sigmoid(g)
        dug_ref[...] = (dact * u * sg * (1.0 + g * (1.0 - sg))).astype(BF16)
        duu_ref[...] = (dact * g * sg).astype(BF16)

    tile = pl.BlockSpec((ts, n), lambda i, q: (i, q))
    return pl.pallas_call(
        body, grid=(s // ts, 2), out_shape=(_sds((s, f), BF16), _sds((s, f), BF16)),
        in_specs=[pl.BlockSpec((ts, d), lambda i, q: (i, 0)), pl.BlockSpec((1, d), lambda i, q: (0, 0)),
                  pl.BlockSpec((n, d), lambda i, q: (q, 0)), tile, tile],
        out_specs=(tile, tile), scratch_shapes=[pltpu.VMEM((ts, d), BF16)],
        name="ffn_bwd_act", compiler_params=_cp(("parallel", "arbitrary")))(dx, gate, w_out, ug, uu)


def _dw_mm(a, b_list, tk, tn, name, gate=None, wfull=None, dgate_init=None):
    s, kdim = a.shape
    nb1 = b_list[0].shape[1] // tn
    nb = nb1 * len(b_list)
    ts = _tile(s, 512)
    nk = kdim // tk
    ns = s // ts
    gated = gate is not None

    def body(*refs):
        a_ref = refs[0]
        b_refs = refs[1:1 + len(b_list)]
        rest = refs[1 + len(b_list):]
        if gated:
            g_ref, w_ref, di_ref, o_ref, dg_ref, acc = rest
        else:
            o_ref, acc = rest
        jn, ik, st = pl.program_id(0), pl.program_id(1), pl.program_id(2)

        @pl.when(st == 0)
        def _():
            acc[...] = jnp.zeros_like(acc)

        for mi, b_ref in enumerate(b_refs):
            @pl.when(jn // nb1 == mi)
            def _(b_ref=b_ref):
                acc[...] += _dot_tn(a_ref[...], b_ref[...].astype(BF16))

        if gated:
            @pl.when(jnp.logical_and(ik == 0, st == 0))
            def _():
                dg_ref[...] = di_ref[...]

        @pl.when(st == ns - 1)
        def _():
            if gated:
                o_ref[0] = acc[...] * g_ref[...]
                dg_ref[...] += _colsum(acc[...] * w_ref[...].astype(F32))
            else:
                o_ref[0] = acc[...]

    in_specs = [pl.BlockSpec((ts, tk), lambda jn, ik, st: (st, ik))]
    for mi in range(len(b_list)):
        in_specs.append(pl.BlockSpec(
            (ts, tn), lambda jn, ik, st, mi=mi: (st, jnp.clip(jn - mi * nb1, 0, nb1 - 1))))
    args = [a] + list(b_list)
    out_shape = [_sds((nb, kdim, tn), F32)]
    out_specs = [pl.BlockSpec((1, tk, tn), lambda jn, ik, st: (jn, ik, 0))]
    if gated:
        vec = pl.BlockSpec((1, tn), lambda jn, ik, st: (0, jn))
        in_specs += [vec, pl.BlockSpec((tk, tn), lambda jn, ik, st: (ik, jn)), vec]
        args += [gate, wfull, dgate_init]
        out_shape.append(_sds((1, nb * tn), F32))
        out_specs.append(vec)
    res = pl.pallas_call(
        body, grid=(nb, nk, ns), out_shape=tuple(out_shape), in_specs=in_specs, out_specs=tuple(out_specs),
        scratch_shapes=[pltpu.VMEM((tk, tn), F32)],
        name=name, compiler_params=_cp(("parallel", "arbitrary", "arbitrary")))(*args)
    return res if gated else res[0]


def _mm_normbwd(terms, x, dxres, gain, scale, name, ts_pref=256):
    s, d = x.shape
    ts = _tile(s, ts_pref)
    arrs, warrs = [], []
    for a, _, w, _ in terms:
        if not any(a is z for z in arrs):
            arrs.append(a)
        if not any(w is z for z in warrs):
            warrs.append(w)
    ai = [next(i for i, z in enumerate(arrs) if z is a) for a, _, _, _ in terms]
    wi = [next(i for i, z in enumerate(warrs) if z is w) for _, _, w, _ in terms]

    def body(*refs):
        a_refs = refs[:len(arrs)]
        w_refs = refs[len(arrs):len(arrs) + len(warrs)]
        x_ref, dr_ref, g_ref, sc_ref, dx_ref, dsh_ref, dsc_ref, dg_ref = refs[len(arrs) + len(warrs):]

        @pl.when(pl.program_id(0) == 0)
        def _():
            dsh_ref[...] = jnp.zeros_like(dsh_ref)
            dsc_ref[...] = jnp.zeros_like(dsc_ref)
            dg_ref[...] = jnp.zeros_like(dg_ref)

        dh = None
        for ti, (_, c0, w, q) in enumerate(terms):
            n = w.shape[2]
            part = _dot_nt(a_refs[ai[ti]][:, pl.ds(c0, n)], w_refs[wi[ti]][q])
            dh = part if dh is None else dh + part
        xhat, rstd = _rms_parts(x_ref[...])
        nrm = xhat * g_ref[...]
        dsh_ref[...] += _colsum(dh)
        dsc_ref[...] += _colsum(dh * nrm)
        dn = dh * (1.0 + sc_ref[...])
        dg_ref[...] += _colsum(dn * xhat)
        dxhat = dn * g_ref[...]
        dx_ref[...] = dr_ref[...] + rstd * (dxhat - xhat * jnp.mean(dxhat * xhat, axis=-1, keepdims=True))

    vec = pl.BlockSpec((1, d), lambda i: (0, 0))
    row = pl.BlockSpec((ts, d), lambda i: (i, 0))
    in_specs = [pl.BlockSpec((ts, a.shape[1]), lambda i: (i, 0)) for a in arrs]
    in_specs += [_resident(w.shape) for w in warrs]
    in_specs += [row, row, vec, vec]
    return pl.pallas_call(
        body, grid=(s // ts,), out_shape=(_sds((s, d), F32), _sds((1, d), F32), _sds((1, d), F32), _sds((1, d), F32)),
        in_specs=in_specs, out_specs=(row, vec, vec, vec),
        name=name, compiler_params=_cp(("arbitrary",)))(*arrs, *warrs, x, dxres, gain, scale)


def _do_kernel(dx, gate, wo, o):
    s, d = dx.shape
    ts = _tile(s, 512)

    def body(dx_ref, g_ref, w_ref, o_ref, do_ref, dl_ref):
        dy = (dx_ref[...] * g_ref[...]).astype(BF16)
        do = _dot_nt(dy, w_ref[...])
        do_ref[...] = do.astype(BF16)
        prod = do * o_ref[...].astype(F32)
        hrow = lax.broadcasted_iota(jnp.int32, (N_HEADS, d), 0)
        hcol = lax.broadcasted_iota(jnp.int32, (N_HEADS, d), 1) // HEAD_DIM
        sel = (hrow == hcol).astype(F32)
        dl_ref[...] = lax.dot_general(sel, prod, (((1,), (1,)), ((), ())), preferred_element_type=F32,
                                      precision=lax.Precision.HIGHEST)

    row = pl.BlockSpec((ts, d), lambda i: (i, 0))
    return pl.pallas_call(
        body, grid=(s // ts,), out_shape=(_sds((s, d), BF16), _sds((N_HEADS, s), F32)),
        in_specs=[row, pl.BlockSpec((1, d), lambda i: (0, 0)), _resident(wo.shape), row],
        out_specs=(row, pl.BlockSpec((N_HEADS, ts), lambda i: (0, i))),
        name="attn_do", compiler_params=_cp(("parallel",)))(dx, gate, wo, o)


def _attn_bwd(q, do, kaug, kaugt, v, cumt, lse, deltat):
    s, d = q.shape
    tq = _tile(s, ATT_TQ)
    tk = _tile(s, ATT_TK)
    assert tq % tk == 0
    npair = d // LANE
    nq = s // tq
    nkb = s // tk
    qscale = HEAD_DIM ** -0.5

    def body(q_ref, do_ref, ka_ref, kt_ref, v_ref, cumt_ref, lse_ref, dl_ref,
             dq_ref, dk_ref, dv_ref, dcq_ref, dck_ref, qaug, dom, rowv, dqt):
        p = pl.program_id(0)
        j = pl.program_id(1)
        lane = lax.broadcasted_iota(jnp.int32, (1, LANE), 1)
        lo = lane < HEAD_DIM
        r8 = lax.broadcasted_iota(jnp.int32, (8, 1), 0)

        @pl.when(j == 0)
        def _():
            dqt[...] = jnp.zeros_like(dqt)
            for c in range(nq):
                rows = pl.ds(c * tq, tq)
                qq = q_ref[rows, :]
                dd = do_ref[rows, :]
                cqt = cumt_ref[:, rows]
                dlt = dl_ref[:, rows]
                lst = lse_ref[0, :, rows]
                for hh in range(2):
                    qaug[hh, rows, :] = _q_aug(qq, lane, hh)
                    dom[hh, rows, :] = jnp.where(_head_mask(lane, hh), dd, jnp.zeros_like(dd))
                    rowv[hh, :, rows] = jnp.where(
                        r8 == 0, _pick_row(cqt, 2 * p + hh) - lst[hh:hh + 1, :],
                        jnp.where(r8 == 1, _pick_row(dlt, 2 * p + hh), 0.0))

        vv = v_ref[...]
        i0 = (j * tk) // tq

        def q_step(i, carry, diag):
            dv_acc, dk0, dk1 = carry
            qs = pl.multiple_of(i * tq, tq)
            if diag:
                krow = lax.broadcasted_iota(jnp.int32, (tk, tq), 0) + j * tk
                qcol = lax.broadcasted_iota(jnp.int32, (tk, tq), 1) + i * tq
                causal = krow <= qcol
            dks = [dk0, dk1]
            for hh in range(2):
                rv = rowv[hh, :, pl.ds(qs, tq)]
                qa = qaug[hh, pl.ds(qs, tq), :]
                dh = dom[hh, pl.ds(qs, tq), :]
                sc = _dot_nt(ka_ref[0, hh], qa)
                if diag:
                    sc = jnp.where(causal, sc, -jnp.inf)
                pt = jnp.exp(sc + rv[0:1, :])
                dpt = _dot_nt(vv, dh)
                dst = (pt * (dpt - rv[1:2, :])).astype(BF16)
                dv_acc = dv_acc + _dot(pt.astype(BF16), dh)
                dks[hh] = dks[hh] + _dot(dst, qa)
                dqt[hh, :, pl.ds(qs, tq)] += _dot(kt_ref[0, hh], dst)
            return dv_acc, dks[0], dks[1]

        z = jnp.zeros((tk, LANE), F32)
        carry = q_step(i0, (z, z, z), True)
        dv_acc, dk0, dk1 = lax.fori_loop(i0 + 1, nq, lambda i, cr: q_step(i, cr, False), carry)
        dv_ref[...] = dv_acc.astype(BF16)
        dk_ref[...] = jnp.where(lo, dk0, dk1).astype(BF16)
        dck_ref[0] = jnp.where(r8 == 0, dk0.T[SPARE[0]:SPARE[0] + 1, :],
                               jnp.where(r8 == 1, dk1.T[SPARE[1]:SPARE[1] + 1, :], 0.0))

        @pl.when(j == nkb - 1)
        def _():
            for c in range(nq):
                rows = pl.ds(c * tq, tq)
                a0 = dqt[0, :, rows].T
                a1 = dqt[1, :, rows].T
                dq_ref[rows, :] = (jnp.where(lo, a0, a1) * qscale).astype(BF16)
            r0, r1 = SPARE[0] + NPIECE, SPARE[1] + NPIECE
            dcq_ref[0] = jnp.where(r8 == 0, dqt[0, r0:r0 + 1, :], jnp.where(r8 == 1, dqt[1, r1:r1 + 1, :], 0.0))

    col = pl.BlockSpec((s, LANE), lambda p, j: (0, p), pipeline_mode=pl.Buffered(1))
    rows16 = pl.BlockSpec((N_HEADS, s), lambda p, j: (0, 0), pipeline_mode=pl.Buffered(1))
    blk = pl.BlockSpec((tk, LANE), lambda p, j: (j, p))
    return pl.pallas_call(
        body, grid=(npair, nkb),
        out_shape=(_sds((s, d), BF16), _sds((s, d), BF16), _sds((s, d), BF16), _sds((npair, 8, s), F32), _sds((npair, 8, s), F32)),
        in_specs=[col, col, pl.BlockSpec((1, 2, tk, LANE), lambda p, j: (p, 0, j, 0)),
                  pl.BlockSpec((1, 2, LANE, tk), lambda p, j: (p, 0, 0, j)), blk, rows16,
                  pl.BlockSpec((1, 8, s), lambda p, j: (p, 0, 0), pipeline_mode=pl.Buffered(1)), rows16],
        out_specs=(pl.BlockSpec((s, LANE), lambda p, j: (0, p)), blk, blk,
                   pl.BlockSpec((1, 8, s), lambda p, j: (p, 0, 0)), pl.BlockSpec((1, 8, tk), lambda p, j: (p, 0, j))),
        scratch_shapes=[pltpu.VMEM((2, s, LANE), BF16), pltpu.VMEM((2, s, LANE), BF16), pltpu.VMEM((2, 8, s), F32),
                        pltpu.VMEM((2, LANE, s), F32)],
        name="fox_attn_bwd", compiler_params=_cp(("arbitrary", "arbitrary"), ATT_BWD_VMEM_MB))(
            q, do, kaug, kaugt, v, cumt, lse, deltat)


def _cumsum_bwd(dcq, dck, flog, fb):
    s = flog.shape[0]
    ts = _tile(s, 256)
    nt = s // ts

    def body(dq_ref, dk_ref, f_ref, b_ref, df_ref, db_ref, carry):
        @pl.when(pl.program_id(0) == 0)
        def _():
            carry[...] = jnp.zeros_like(carry)
            db_ref[...] = jnp.zeros_like(db_ref)

        r = lax.broadcasted_iota(jnp.int32, (ts, ts), 0)
        cidx = lax.broadcasted_iota(jnp.int32, (ts, ts), 1)
        tri = (r >= cidx).astype(F32)
        dct = dq_ref[...] + dk_ref[...]
        dlst = jnp.dot(dct, tri, preferred_element_type=F32, precision=lax.Precision.HIGHEST) + carry[...]
        carry[...] = dlst[:, 0:1]
        dls = jnp.concatenate([dlst, jnp.zeros((LANE - N_HEADS, ts), F32)], axis=0).T
        z = f_ref[...] + b_ref[...]
        df = dls * (1.0 / (1.0 + jnp.exp(z)))
        db_ref[...] += _colsum(df)
        df_ref[...] = df.astype(BF16)

    rev = pl.BlockSpec((ts, LANE), lambda i: (nt - 1 - i, 0))
    revt = pl.BlockSpec((N_HEADS, ts), lambda i: (0, nt - 1 - i))
    vec = pl.BlockSpec((1, LANE), lambda i: (0, 0))
    return pl.pallas_call(
        body, grid=(nt,), out_shape=(_sds((s, LANE), BF16), _sds((1, LANE), F32)),
        in_specs=[revt, revt, rev, vec], out_specs=(rev, vec), scratch_shapes=[pltpu.VMEM((N_HEADS, 1), F32)],
        name="forget_cumsum_bwd", compiler_params=_cp(("arbitrary",)))(dcq, dck, flog, fb)


def _conv_bwd1(dx, gate, w_out, b_out, dwo, lng, lnb):
    s, d = dx.shape
    ts = _tile(s, 512)
    ns = s // ts

    def body(dx_ref, g_ref, w_ref, bo_ref, y_ref, lg_ref, lb_ref, dd_ref, dlg_ref, dlb_ref, dbd_ref, dbo_ref, dge_ref, cs):
        i = pl.program_id(0)

        @pl.when(i == 0)
        def _():
            for r in (dlg_ref, dlb_ref, dbd_ref, cs):
                r[...] = jnp.zeros_like(r)

        dxv = dx_ref[...]
        cs[...] += _colsum(dxv)
        dsw = _dot_nt((dxv * g_ref[...]).astype(BF16), w_ref[...])
        yv = y_ref[...]
        mu = jnp.mean(yv, axis=-1, keepdims=True)
        yc = yv - mu
        rstd = lax.rsqrt(jnp.mean(yc * yc, axis=-1, keepdims=True) + EPS)
        xhat = yc * rstd
        ln = xhat * lg_ref[...] + lb_ref[...]
        sg = _sigmoid(ln)
        dln = dsw * (sg * (1.0 + ln * (1.0 - sg)))
        dlg_ref[...] += _colsum(dln * xhat)
        dlb_ref[...] += _colsum(dln)
        dxh = dln * lg_ref[...]
        dd = rstd * (dxh - jnp.mean(dxh, axis=-1, keepdims=True) - xhat * jnp.mean(dxh * xhat, axis=-1, keepdims=True))
        dbd_ref[...] += _colsum(dd)
        dd_ref[...] = dd

        @pl.when(i == ns - 1)
        def _():
            dbo_ref[...] = g_ref[...] * cs[...]
            dge_ref[...] = bo_ref[...] * cs[...]

    vec = pl.BlockSpec((1, d), lambda i: (0, 0))
    row = pl.BlockSpec((ts, d), lambda i: (i, 0))
    return pl.pallas_call(
        body, grid=(ns,), out_shape=(_sds((s, d), F32),) + tuple(_sds((1, d), F32) for _ in range(5)),
        in_specs=[row, vec, _resident(w_out.shape), vec, row, vec, vec], out_specs=(row, vec, vec, vec, vec, vec),
        scratch_shapes=[pltpu.VMEM((1, d), F32)],
        name="conv_bwd_ln", compiler_params=_cp(("arbitrary",)))(dx, gate, w_out, b_out, dwo, lng, lnb)


def _dwconv_bwd(ddwo, glu, a_s, g_s, wdw):
    s, d = ddwo.shape
    ts = _tile(s, 256)
    ns = s // ts
    rb, cb = 32, 256
    nrb = ts // rb

    def body(dd_ref, ddn_ref, gl_ref, glh_ref, a_ref, g_ref, w_ref, da_ref, dg_ref, dw_ref, sa_ref, sg_ref, bufd, bufg, dws):
        i = pl.program_id(0)

        @pl.when(i == 0)
        def _():
            dws[...] = jnp.zeros_like(dws)
            sa_ref[...] = jnp.zeros_like(sa_ref)
            sg_ref[...] = jnp.zeros_like(sg_ref)
            bufg[pl.ds(0, HALO), :] = jnp.zeros((HALO, d), F32)

        @pl.when(i > 0)
        def _():
            bufg[pl.ds(0, HALO), :] = glh_ref[...]

        bufg[pl.ds(HALO, ts), :] = gl_ref[...]
        bufd[pl.ds(0, ts), :] = dd_ref[...]

        @pl.when(i == ns - 1)
        def _():
            bufd[pl.ds(ts, HALO), :] = jnp.zeros((HALO, d), F32)

        @pl.when(i < ns - 1)
        def _():
            bufd[pl.ds(ts, HALO), :] = ddn_ref[...]

        for cc in range(d // cb):
            cs = pl.ds(cc * cb, cb)
            for r in range(nrb):
                acc = jnp.zeros((rb, cb), F32)
                for k in range(CONV_K):
                    acc = acc + w_ref[pl.ds(k, 1), cs] * bufd[pl.ds(r * rb + (CONV_K - 1) - k, rb), cs]
                rows = pl.ds(r * rb, rb)
                av = a_ref[rows, cs].astype(F32)
                sg = _sigmoid(g_ref[rows, cs].astype(F32))
                dav = acc * sg
                dgv = acc * av * sg * (1.0 - sg)
                da_ref[rows, cs] = dav.astype(BF16)
                dg_ref[rows, cs] = dgv.astype(BF16)
                sa_ref[:, cs] += _colsum(dav)
                sg_ref[:, cs] += _colsum(dgv)
            for k in range(CONV_K):
                acc8 = jnp.zeros((8, cb), F32)
                for r in range(nrb):
                    prod = bufd[pl.ds(r * rb, rb), cs] * bufg[pl.ds(HALO - (CONV_K - 1) + k + r * rb, rb), cs]
                    acc8 = acc8 + (prod[0:8] + prod[8:16]) + (prod[16:24] + prod[24:32])
                dws[pl.ds(8 * k, 8), cs] += acc8

        @pl.when(i == ns - 1)
        def _():
            dw_ref[...] = jnp.zeros_like(dw_ref)
            for k in range(CONV_K):
                dw_ref[pl.ds(k, 1), :] = _colsum(dws[pl.ds(8 * k, 8), :])

    row = pl.BlockSpec((ts, d), lambda i: (i, 0))
    vec = pl.BlockSpec((1, d), lambda i: (0, 0))
    hb = ts // HALO
    return pl.pallas_call(
        body, grid=(ns,),
        out_shape=(_sds((s, d), BF16), _sds((s, d), BF16), _sds((HALO, d), F32), _sds((1, d), F32), _sds((1, d), F32)),
        in_specs=[row, pl.BlockSpec((HALO, d), lambda i: (jnp.minimum((i + 1) * hb, ns * hb - 1), 0)),
                  row, pl.BlockSpec((HALO, d), lambda i: (jnp.maximum(i * hb - 1, 0), 0)),
                  row, row, pl.BlockSpec((HALO, d), lambda i: (0, 0))],
        out_specs=(row, row, pl.BlockSpec((HALO, d), lambda i: (0, 0)), vec, vec),
        scratch_shapes=[pltpu.VMEM((ts + HALO, d), F32), pltpu.VMEM((HALO + ts, d), F32), pltpu.VMEM((8 * HALO, d), F32)],
        name="dwconv_bwd", compiler_params=_cp(("arbitrary",)))(ddwo, ddwo, glu, glu, a_s, g_s, wdw)


def _ada_wgrad(cat, da, name):
    nl, _, n = da.shape
    d = cat.shape[0]
    tn = 256

    def body(c_ref, d_ref, o_ref):
        acc = c_ref[:, 0:1] * d_ref[0, 0:1, :]
        for r in range(1, 8):
            acc = acc + c_ref[:, r:r + 1] * d_ref[0, r:r + 1, :]
        o_ref[0] = acc

    return pl.pallas_call(
        body, grid=(nl, n // tn), out_shape=_sds((nl, d, n), F32),
        in_specs=[pl.BlockSpec((d, 8), lambda l, j: (0, 0)), pl.BlockSpec((1, 8, tn), lambda l, j: (l, 0, j))],
        out_specs=pl.BlockSpec((1, d, tn), lambda l, j: (l, 0, j)),
        name=name, compiler_params=_cp(("parallel", "parallel")))(cat, da)


def _silu_rows(c_all):
    def body(c_ref, o_ref):
        cc = c_ref[...]
        o_ref[...] = cc * _sigmoid(cc)

    return pl.pallas_call(body, out_shape=_sds(c_all.shape, F32), name="silu_c")(c_all)


def _adamw(w, g, m, v, name):
    r, c = w.shape
    tr = r
    for cand in (512, 256, 128, 64, 32, 16, 8):
        if r % cand == 0 and cand * c * 4 <= (1 << 20):
            tr = cand
            break
    bc1 = 1.0 - ADAM_B1 ** ADAM_STEP
    bc2 = 1.0 - ADAM_B2 ** ADAM_STEP

    def body(w_ref, g_ref, m_ref, v_ref, d_ref, nm_ref, nv_ref):
        gv = g_ref[...]
        mn = ADAM_B1 * m_ref[...] + (1.0 - ADAM_B1) * gv
        vn = ADAM_B2 * v_ref[...] + (1.0 - ADAM_B2) * (gv * gv)
        mh = mn / bc1
        vh = vn / bc2
        d_ref[...] = -ADAM_LR * (mh / (jnp.sqrt(vh) + ADAM_EPS) + ADAM_WD * w_ref[...])
        nm_ref[...] = mn
        nv_ref[...] = vn

    blk = pl.BlockSpec((tr, c), lambda i: (i, 0))
    return pl.pallas_call(
        body, grid=(r // tr,), out_shape=tuple(_sds((r, c), F32) for _ in range(3)),
        in_specs=[blk, blk, blk, blk], out_specs=(blk, blk, blk),
        name=name, compiler_params=_cp(("parallel",)))(w, g, m, v)


def _halves(w2):
    r, c = w2.shape
    return w2.reshape(2, (r // 2) * c // 1024, 1024)


def _pad_rows(a, rows, axis):
    pad = [(0, 0)] * a.ndim
    pad[axis] = (0, rows - a.shape[axis])
    return jnp.pad(a, pad)


def _vec(a):
    return a.reshape(1, -1)


def kernel(x, c, mix_norm_g, mix_ada_w, mix_ada_b, ffn_norm_g, ffn_ada_w, ffn_ada_b, ffn_w_in, ffn_w_out, conv_w_in, conv_b_in, conv_w_dw, conv_b_dw, conv_ln_g, conv_ln_b, conv_w_out, conv_b_out, kv_norm_g, kv_ada_w, kv_ada_b, kv_w, forget_b, attn_w_q, attn_w_o, final_norm_g, loss_target, m_mix_norm_g, m_mix_ada_w, m_mix_ada_b, m_ffn_norm_g, m_ffn_ada_w, m_ffn_ada_b, m_ffn_w_in, m_ffn_w_out, m_conv_w_in, m_conv_b_in, m_conv_w_dw, m_conv_b_dw, m_conv_ln_g, m_conv_ln_b, m_conv_w_out, m_conv_b_out, m_kv_norm_g, m_kv_ada_w, m_kv_ada_b, m_kv_w, m_forget_b, m_attn_w_q, m_attn_w_o, m_final_norm_g, v_mix_norm_g, v_mix_ada_w, v_mix_ada_b, v_ffn_norm_g, v_ffn_ada_w, v_ffn_ada_b, v_ffn_w_in, v_ffn_w_out, v_conv_w_in, v_conv_b_in, v_conv_w_dw, v_conv_b_dw, v_conv_ln_g, v_conv_ln_b, v_conv_w_out, v_conv_b_out, v_kv_norm_g, v_kv_ada_w, v_kv_ada_b, v_kv_w, v_forget_b, v_attn_w_q, v_attn_w_o, v_final_norm_g):
    xi, yi, ci = lax.axis_index("x"), lax.axis_index("y"), lax.axis_index("c")
    chip = 2 * xi + yi
    dev = 4 * xi + 2 * yi + ci
    s, d = x.shape[1], x.shape[2]
    f = ffn_w_out.shape[1] * 4
    x0 = x[0]
    nkv = kv_w.shape[1]
    nkv_all = 4 * nkv

    wdw_loc = _pad_rows(conv_w_dw[0], HALO, 0)
    small = jnp.concatenate([c.reshape(-1), conv_b_in.reshape(-1), wdw_loc.reshape(-1), conv_b_dw.reshape(-1),
                             conv_ln_g.reshape(-1), conv_ln_b.reshape(-1), conv_b_out.reshape(-1)])
    n_small = small.shape[0]
    w_small = -(-n_small // (8 * LANE)) * LANE
    small = jnp.pad(small, (0, 8 * w_small - n_small)).reshape(8, w_small)
    small_all = _allgather8(small, "ag_small_params", True).reshape(8, 8 * w_small)
    c_all = small_all[:, :d]
    per_chip = small_all[0::2]
    dq_ = d // 4
    o1 = d
    b_in_full = per_chip[:, o1:o1 + 2 * dq_].reshape(4, 1, 2 * dq_)
    o1 += 2 * dq_
    wdw_full = per_chip[:, o1:o1 + HALO * dq_].reshape(4, HALO, dq_).transpose(1, 0, 2).reshape(HALO, d)
    o1 += HALO * dq_
    bdw_full = per_chip[:, o1:o1 + dq_].reshape(1, d)
    lng_full = per_chip[:, o1 + dq_:o1 + 2 * dq_].reshape(1, d)
    lnb_full = per_chip[:, o1 + 2 * dq_:o1 + 3 * dq_].reshape(1, d)
    bout_full = per_chip[:, o1 + 3 * dq_:o1 + 4 * dq_].reshape(1, d)

    a_mix = _ada_fwd(c_all, mix_ada_w, "ada_mix")
    a_ffn = _ada_fwd(c_all, ffn_ada_w, "ada_ffn")
    a_kv = _ada_fwd(c_all, kv_ada_w[None], "ada_kv")
    n3 = mix_ada_w.shape[2]
    n2 = kv_ada_w.shape[1]
    ada_loc = jnp.concatenate([a_mix[0], a_mix[1], a_ffn[0], a_ffn[1], a_kv[0]], axis=1)
    w_ada = ada_loc.shape[1]
    ada_all = _allgather8(ada_loc, "ag_ada", True).reshape(8, 8, w_ada)
    ada_me = lax.dynamic_index_in_dim(ada_all, dev, axis=1, keepdims=False)[0::2]

    def ada_vec(off, n, bias):
        return ada_me[:, off:off + n].reshape(1, 4 * n) + bias.reshape(1, -1)

    ada_m0 = ada_vec(0, n3, mix_ada_b[0])
    ada_m1 = ada_vec(n3, n3, mix_ada_b[1])
    ada_f0 = ada_vec(2 * n3, n3, ffn_ada_b[0])
    ada_f1 = ada_vec(3 * n3, n3, ffn_ada_b[1])
    ada_k = ada_vec(4 * n3, n2, kv_ada_b)

    def split3(a):
        return a[:, :d], a[:, d:2 * d], a[:, 2 * d:3 * d]

    sh_m0, sc_m0, gt_m0 = split3(ada_m0)
    sh_m1, sc_m1, gt_m1 = split3(ada_m1)
    sh_f0, sc_f0, gt_f0 = split3(ada_f0)
    sh_f1, sc_f1, gt_f1 = split3(ada_f1)
    sh_k, sc_k = ada_k[:, :d], ada_k[:, d:2 * d]

    big = [ffn_w_in[0], ffn_w_in[1], ffn_w_out[0], ffn_w_out[1], conv_w_in[0], conv_w_out[0], kv_w, attn_w_q[0], attn_w_o[0]]
    hrows = [(w.shape[0] // 2) * w.shape[1] // 1024 for w in big]
    prows = [-(-r // 16) * 16 for r in hrows]
    offs = [sum(prows[:i]) for i in range(len(big))]
    rtot = sum(prows)
    pack = jnp.concatenate([_pad_rows(_halves(w.astype(BF16)), pr, 1) for w, pr in zip(big, prows)], axis=1)
    mine = lax.dynamic_index_in_dim(pack, ci, axis=0, keepdims=False)
    gathered = _allgather8(mine, "ag_weights", False).reshape(4, 2, rtot, 1024)

    def full_w(i):
        w = big[i]
        return gathered[:, :, offs[i]:offs[i] + hrows[i], :].reshape(4, w.shape[0], w.shape[1])

    w_in = [full_w(0), full_w(1)]
    w_out = [full_w(2).reshape(f, d), full_w(3).reshape(f, d)]
    cw_in = full_w(4)
    cw_out = full_w(5).reshape(d, d)
    kvw = full_w(6).transpose(1, 0, 2).reshape(d, nkv_all)
    wk, wv = kvw[:, :d], kvw[:, d:2 * d]
    wf = jnp.pad(kvw[:, 2 * d:], ((0, 0), (0, LANE - N_HEADS)))
    wq = full_w(7).reshape(d, d)
    wo = full_w(8).reshape(d, d)

    zero_b = jnp.zeros((1, d), F32)
    g_m0, g_m1 = _vec(mix_norm_g[0]), _vec(mix_norm_g[1])
    g_f0, g_f1 = _vec(ffn_norm_g[0]), _vec(ffn_norm_g[1])
    g_k, g_fin = _vec(kv_norm_g), _vec(final_norm_g)
    fb = jnp.pad(forget_b, (0, LANE - N_HEADS)).reshape(1, LANE)

    h0, glu, a_s, g_s = _in_pair(x0, g_m0, sh_m0, sc_m0, cw_in, b_in_full, True, "conv_in")
    dwo, sw = _dwconv_fwd(glu, wdw_full, bdw_full, lng_full, lnb_full)
    x1 = _mm_res(sw, cw_out, bout_full, gt_m0, x0, "conv_out")
    hf0, act0, ug0, uu0 = _in_pair(x1, g_f0, sh_f0, sc_f0, w_in[0], None, False, "ffn0_in")
    x2 = _mm_res(act0, w_out[0], zero_b, gt_f0, x1, "ffn0_out")
    hk, h1, kk, vv, qq, flog = _qkv(x2, (g_k, sh_k, sc_k), (g_m1, sh_m1, sc_m1), wk, wv, wf, wq)
    cum, cumt = _cumsum_fwd(flog, fb)
    kaug, kaugt, vtr = _attn_prep(kk, vv, cum)
    o, lse = _attn_fwd(qq, kaug, vtr, cumt)
    x3 = _mm_res(o, wo, zero_b, gt_m1, x2, "attn_out")
    hf1, act1, ug1, uu1 = _in_pair(x3, g_f1, sh_f1, sc_f1, w_in[1], None, False, "ffn1_in")
    x4 = _mm_res(act1, w_out[1], zero_b, gt_f1, x3, "ffn1_out")
    lsum, dx4, d_gfin = _final(x4, g_fin, loss_target[0])
    loss = lax.psum(0.5 / d * jnp.sum(lsum), ("x", "y", "c"))

    nf = f // 2

    def ffn_bwd(dx_out, x_in, hf, act, ug, uu, gain, scale, gate, w_in_l, w_out_l, tag):
        dug, duu = _ffn_bwd_act(dx_out, gate, w_out_l, ug, uu)
        dw_out, dgate = _dw_mm(act, [dx_out], nf, d, tag + "_dw_out", gate=gate, wfull=w_out_l, dgate_init=zero_b)
        terms = [(dug, 0, w_in_l, 0), (dug, nf, w_in_l, 1), (duu, 0, w_in_l, 2), (duu, nf, w_in_l, 3)]
        dx_in, dsh, dsc, dgn = _mm_normbwd(terms, x_in, dx_out, gain, scale, tag + "_bwd_in")
        dw_in = _dw_mm(hf, [dug, duu], d, nf, tag + "_dw_in")
        return dx_in, dw_in, dw_out[0], dsh, dsc, dgate, dgn

    dx3, dw_in1, dw_out1, dsh_f1, dsc_f1, dgt_f1, dgn_f1 = ffn_bwd(dx4, x3, hf1, act1, ug1, uu1, g_f1, sc_f1, gt_f1, w_in[1], w_out[1], "ffn1")

    do, deltat = _do_kernel(dx3, gt_m1, wo, o)
    dwo_att, dgt_m1 = _dw_mm(o, [dx3], d, d, "attn_dw_o", gate=gt_m1, wfull=wo, dgate_init=zero_b)
    dq, dk, dv, dcq, dck = _attn_bwd(qq, do, kaug, kaugt, vv, cumt, lse, deltat)
    wq3 = wq.reshape(1, d, d)
    dx2a, dsh_m1, dsc_m1, dgn_m1 = _mm_normbwd([(dq, 0, wq3, 0)], x2, dx3, g_m1, sc_m1, "attn_bwd_q")
    dwq = _dw_mm(h1, [dq], d, d, "attn_dw_q")[0]

    df, dfb = _cumsum_bwd(dcq[:, :2].reshape(N_HEADS, s), dck[:, :2].reshape(N_HEADS, s), flog, fb)
    terms = [(dk, 0, wk.reshape(1, d, d), 0), (dv, 0, wv.reshape(1, d, d), 0), (df, 0, wf.reshape(1, d, LANE), 0)]
    dx2, dsh_k, dsc_k, dgn_k = _mm_normbwd(terms, x2, dx2a, g_k, sc_k, "kv_bwd")
    dwk = _dw_mm(hk, [dk], d, d, "kv_dw_k")[0]
    dwv = _dw_mm(hk, [dv], d, d, "kv_dw_v")[0]
    dwf = _dw_mm(hk, [df], d, LANE, "kv_dw_f")[0]
    dkvw = jnp.concatenate([dwk, dwv, dwf[:, :N_HEADS]], axis=1)
    dkvw = dkvw.reshape(d, 4, nkv).transpose(1, 0, 2)

    dx1, dw_in0, dw_out0, dsh_f0, dsc_f0, dgt_f0, dgn_f0 = ffn_bwd(dx2, x1, hf0, act0, ug0, uu0, g_f0, sc_f0, gt_f0, w_in[0], w_out[0], "ffn0")

    ddwo, d_lng, d_lnb, d_bdw, d_bout, dgt_extra = _conv_bwd1(dx1, gt_m0, cw_out, bout_full, dwo, lng_full, lnb_full)
    dcw_out, dgt_m0 = _dw_mm(sw, [dx1], d, d, "conv_dw_out", gate=gt_m0, wfull=cw_out, dgate_init=dgt_extra)
    da, dg, d_wdw, d_bin_a, d_bin_g = _dwconv_bwd(ddwo, glu, a_s, g_s, wdw_full)
    nc = cw_in.shape[2]
    terms = [(da, 0, cw_in, 0), (da, nc, cw_in, 1), (dg, 0, cw_in, 2), (dg, nc, cw_in, 3)]
    dx0, dsh_m0, dsc_m0, dgn_m0 = _mm_normbwd(terms, x0, dx1, g_m0, sc_m0, "conv_bwd_in")
    dcw_in = _dw_mm(h0, [da, dg], d, nc, "conv_dw_in")

    bigg = [dw_in0, dw_in1, dw_out0.reshape(4, f // 4, d), dw_out1.reshape(4, f // 4, d), dcw_in, dcw_out[0].reshape(4, d // 4, d),
            dkvw, dwq.reshape(4, d // 4, d), dwo_att[0].reshape(4, d // 4, d)]
    gpack = jnp.concatenate(
        [_pad_rows(g.reshape(4, 2, hr, 1024), pr, 2) for g, hr, pr in zip(bigg, hrows, prows)], axis=2)
    sel = jnp.stack([ci, chip]).astype(jnp.int32)
    land1 = _swap_halves(gpack)
    q16, qown = _add_halves(gpack, land1, sel)
    land2 = _scatter_chips(q16)
    gboth = _share_half(_add_chips(qown, land2, sel))

    def shard_grad(i):
        w = big[i]
        return gboth[:, offs[i]:offs[i] + hrows[i], :].reshape(w.shape)

    d_ada = [jnp.concatenate([dsh_m0, dsc_m0, dgt_m0], axis=1), jnp.concatenate([dsh_m1, dsc_m1, dgt_m1], axis=1),
             jnp.concatenate([dsh_f0, dsc_f0, dgt_f0], axis=1), jnp.concatenate([dsh_f1, dsc_f1, dgt_f1], axis=1),
             jnp.concatenate([dsh_k, dsc_k], axis=1)]
    fields = d_ada + [dgn_m0, dgn_m1, dgn_f0, dgn_f1, dgn_k, d_gfin, d_bin_a, d_bin_g, d_bdw, d_lng, d_lnb, d_bout,
                      d_wdw.reshape(1, -1), dfb]
    foffs = [0]
    for fl in fields:
        foffs.append(foffs[-1] + fl.shape[1])
    n_row = foffs[-1]
    w_row = -(-n_row // (8 * LANE)) * LANE
    row = jnp.pad(jnp.concatenate(fields, axis=1), ((0, 0), (0, 8 * w_row - n_row))).reshape(8, w_row)
    rows_all = _allgather8(row, "ag_small_grads", True).reshape(8, 8, w_row)
    rsum_small = _sum8(rows_all).reshape(1, 8 * w_row)
    rows_flat = rows_all.reshape(8, 8 * w_row)

    def fsum(i):
        return rsum_small[:, foffs[i]:foffs[i + 1]]

    cat = _silu_rows(c_all).T

    def ada_cols(i, n):
        full = rows_flat[:, foffs[i]:foffs[i + 1]].reshape(8, 4, n)
        return lax.dynamic_index_in_dim(full, chip, axis=1, keepdims=False)

    g_mix_ada_w = _ada_wgrad(cat, jnp.stack([ada_cols(0, n3), ada_cols(1, n3)]), "ada_mix_wgrad")
    g_ffn_ada_w = _ada_wgrad(cat, jnp.stack([ada_cols(2, n3), ada_cols(3, n3)]), "ada_ffn_wgrad")
    g_kv_ada_w = _ada_wgrad(cat, ada_cols(4, n2)[None], "ada_kv_wgrad")[0]

    def my_cols(v, n):
        return lax.dynamic_index_in_dim(v.reshape(4, n), chip, axis=0, keepdims=False)

    grads = {
        "mix_norm_g": jnp.concatenate([fsum(5), fsum(6)], axis=0),
        "mix_ada_w": g_mix_ada_w,
        "mix_ada_b": jnp.concatenate([fsum(0), fsum(1)], axis=0),
        "ffn_norm_g": jnp.concatenate([fsum(7), fsum(8)], axis=0),
        "ffn_ada_w": g_ffn_ada_w,
        "ffn_ada_b": jnp.concatenate([fsum(2), fsum(3)], axis=0),
        "ffn_w_in": jnp.stack([shard_grad(0), shard_grad(1)]),
        "ffn_w_out": jnp.stack([shard_grad(2), shard_grad(3)]),
        "conv_w_in": shard_grad(4)[None],
        "conv_b_in": my_cols(jnp.concatenate([fsum(11), fsum(12)], axis=1), 2 * dq_)[None],
        "conv_w_dw": lax.dynamic_index_in_dim(fsum(17).reshape(HALO, 4, dq_), chip, axis=1, keepdims=False)[:CONV_K][None],
        "conv_b_dw": my_cols(fsum(13), dq_)[None],
        "conv_ln_g": my_cols(fsum(14), dq_)[None],
        "conv_ln_b": my_cols(fsum(15), dq_)[None],
        "conv_w_out": shard_grad(5)[None],
        "conv_b_out": my_cols(fsum(16), dq_)[None],
        "kv_norm_g": fsum(9).reshape(-1),
        "kv_ada_w": g_kv_ada_w,
        "kv_ada_b": fsum(4).reshape(-1),
        "kv_w": shard_grad(6),
        "forget_b": fsum(18).reshape(-1)[:N_HEADS],
        "attn_w_q": shard_grad(7)[None],
        "attn_w_o": shard_grad(8)[None],
        "final_norm_g": fsum(10).reshape(-1),
    }
    weights = dict(mix_norm_g=mix_norm_g, mix_ada_w=mix_ada_w, mix_ada_b=mix_ada_b, ffn_norm_g=ffn_norm_g, ffn_ada_w=ffn_ada_w, ffn_ada_b=ffn_ada_b, ffn_w_in=ffn_w_in, ffn_w_out=ffn_w_out, conv_w_in=conv_w_in, conv_b_in=conv_b_in, conv_w_dw=conv_w_dw, conv_b_dw=conv_b_dw, conv_ln_g=conv_ln_g, conv_ln_b=conv_ln_b, conv_w_out=conv_w_out, conv_b_out=conv_b_out, kv_norm_g=kv_norm_g, kv_ada_w=kv_ada_w, kv_ada_b=kv_ada_b, kv_w=kv_w, forget_b=forget_b, attn_w_q=attn_w_q, attn_w_o=attn_w_o, final_norm_g=final_norm_g)
    moms = dict(mix_norm_g=(m_mix_norm_g, v_mix_norm_g), mix_ada_w=(m_mix_ada_w, v_mix_ada_w), mix_ada_b=(m_mix_ada_b, v_mix_ada_b), ffn_norm_g=(m_ffn_norm_g, v_ffn_norm_g), ffn_ada_w=(m_ffn_ada_w, v_ffn_ada_w), ffn_ada_b=(m_ffn_ada_b, v_ffn_ada_b), ffn_w_in=(m_ffn_w_in, v_ffn_w_in), ffn_w_out=(m_ffn_w_out, v_ffn_w_out), conv_w_in=(m_conv_w_in, v_conv_w_in), conv_b_in=(m_conv_b_in, v_conv_b_in), conv_w_dw=(m_conv_w_dw, v_conv_w_dw), conv_b_dw=(m_conv_b_dw, v_conv_b_dw), conv_ln_g=(m_conv_ln_g, v_conv_ln_g), conv_ln_b=(m_conv_ln_b, v_conv_ln_b), conv_w_out=(m_conv_w_out, v_conv_w_out), conv_b_out=(m_conv_b_out, v_conv_b_out), kv_norm_g=(m_kv_norm_g, v_kv_norm_g), kv_ada_w=(m_kv_ada_w, v_kv_ada_w), kv_ada_b=(m_kv_ada_b, v_kv_ada_b), kv_w=(m_kv_w, v_kv_w), forget_b=(m_forget_b, v_forget_b), attn_w_q=(m_attn_w_q, v_attn_w_q), attn_w_o=(m_attn_w_o, v_attn_w_o), final_norm_g=(m_final_norm_g, v_final_norm_g))
    names = list(weights)

    deltas, new_m, new_v = {}, {}, {}
    small_names = [n for n in names if weights[n].size < (1 << 16)]
    for n in names:
        if n in small_names:
            continue
        w = weights[n]
        w2 = w.reshape(-1, w.shape[-1])
        dl, nm, nv = _adamw(w2, grads[n].reshape(w2.shape), moms[n][0].reshape(w2.shape), moms[n][1].reshape(w2.shape), "adamw_" + n)
        deltas[n], new_m[n], new_v[n] = dl.reshape(w.shape), nm.reshape(w.shape), nv.reshape(w.shape)

    def pack_small(get):
        flat = jnp.concatenate([get(n).reshape(-1) for n in small_names])
        rows_ = -(-flat.shape[0] // (8 * LANE)) * 8
        return jnp.pad(flat, (0, rows_ * LANE - flat.shape[0])).reshape(rows_, LANE)

    ws, gs = pack_small(lambda n: weights[n]), pack_small(lambda n: grads[n])
    ms_, vs_ = pack_small(lambda n: moms[n][0]), pack_small(lambda n: moms[n][1])
    vs_ = jnp.where(jnp.arange(vs_.size).reshape(vs_.shape) < sum(weights[n].size for n in small_names), vs_, 1.0)
    dl, nm, nv = _adamw(ws, gs, ms_, vs_, "adamw_small")
    off = 0
    for n in small_names:
        sz = weights[n].size
        shp = weights[n].shape
        deltas[n] = dl.reshape(-1)[off:off + sz].reshape(shp)
        new_m[n] = nm.reshape(-1)[off:off + sz].reshape(shp)
        new_v[n] = nv.reshape(-1)[off:off + sz].reshape(shp)
        off += sz

    grad_out = [grads[n].reshape(weights[n].shape) for n in names]
    return (loss, dx0[None], *grad_out, *[deltas[n] for n in names], *[new_m[n] for n in names], *[new_v[n] for n in names])
```

```python
import functools

import jax
import jax.numpy as jnp
from jax import lax
from jax.experimental import pallas as pl
from jax.experimental.pallas import tpu as pltpu

F32 = jnp.float32
BF16 = jnp.bfloat16
MESH = pl.DeviceIdType.MESH

EPS = 1e-6
N_HEADS = 16
HEAD_DIM = 64
CONV_K = 31
LANE = 128
HALO = 32
ATT_TQ = 1024
ATT_TK = 512
NPIECE = 3
SPARE = (HEAD_DIM, 0)
VMEM_MB = 48
ATT_BWD_VMEM_MB = 56

ADAM_LR = 0.001
ADAM_B1 = 0.9
ADAM_B2 = 0.999
ADAM_EPS = 1e-08
ADAM_WD = 0.01
ADAM_STEP = 10


def _sds(shape, dtype):
    return jax.ShapeDtypeStruct(tuple(shape), dtype)


def _cp(sem=None, vmem_mb=VMEM_MB):
    return pltpu.CompilerParams(dimension_semantics=sem, vmem_limit_bytes=vmem_mb << 20)


def _tile(n, pref):
    return pref if n % pref == 0 else n


def _row_tile(r, mult):
    for cand in range(512 // mult * mult, mult - 1, -mult):
        if r % cand == 0:
            return cand
    return r


def _resident(shape):
    nd = len(shape)
    return pl.BlockSpec(tuple(shape), lambda *_: (0,) * nd, pipeline_mode=pl.Buffered(1))


def _dot(a, b):
    return jnp.dot(a, b, preferred_element_type=F32)


def _dot_nt(a, b):
    return lax.dot_general(a, b, (((1,), (1,)), ((), ())), preferred_element_type=F32)


def _dot_tn(a, b):
    return lax.dot_general(a, b, (((0,), (0,)), ((), ())), preferred_element_type=F32)


def _sigmoid(x):
    return 1.0 / (1.0 + jnp.exp(-x))


def _colsum(x):
    return jnp.sum(x, axis=0, keepdims=True)


def _rms_parts(x):
    rstd = lax.rsqrt(jnp.mean(x * x, axis=-1, keepdims=True) + EPS)
    return x * rstd, rstd


class _Gather8:
    def __init__(self, x):
        self.m_per, n = x.shape
        self.land = _sds((8 * self.m_per, n), x.dtype)
        self.sems = [pltpu.SemaphoreType.DMA((7,)), pltpu.SemaphoreType.DMA((7,)), pltpu.SemaphoreType.DMA]

    def _parts(self, x_ref, out_ref, send_sems, recv_sems, local_sem):
        x, y, c = lax.axis_index("x"), lax.axis_index("y"), lax.axis_index("c")
        me, sibling = (x, y, c), (x, y, 1 - c)
        chips = [(1 - x, y), (x, 1 - y), (1 - x, 1 - y)]
        m_per = self.m_per

        def rows(px, py, pc):
            return out_ref.at[pl.ds((4 * px + 2 * py + pc) * m_per, m_per), :]

        def copy(k, block, to, src=None):
            return pltpu.make_async_remote_copy(
                src_ref=rows(*block) if src is None else src, dst_ref=rows(*block),
                send_sem=send_sems.at[k], recv_sem=recv_sems.at[k], device_id=to, device_id_type=MESH)

        mine = pltpu.make_async_copy(x_ref, rows(*me), local_sem)
        first = [copy(0, me, sibling, src=x_ref)]
        first += [copy(1 + j, me, (*chip, c), src=x_ref) for j, chip in enumerate(chips)]
        passed = [copy(4 + j, (*chip, c), sibling) for j, chip in enumerate(chips)]
        return c, me, sibling, chips, copy, mine, first, passed

    def start(self, x_ref, out_ref, send_sems, recv_sems, local_sem):
        _, _, _, _, _, mine, first, _ = self._parts(x_ref, out_ref, send_sems, recv_sems, local_sem)
        mine.start()
        for cp in first:
            cp.start()

    def finish(self, x_ref, out_ref, send_sems, recv_sems, local_sem):
        c, me, sibling, chips, copy, mine, first, passed = self._parts(x_ref, out_ref, send_sems, recv_sems, local_sem)
        for j, chip in enumerate(chips):
            copy(1 + j, (*chip, c), me).wait_recv()
            passed[j].start()
        copy(0, sibling, me).wait_recv()
        for j, chip in enumerate(chips):
            copy(4 + j, (*chip, 1 - c), me).wait_recv()
        for cp in first + passed:
            cp.wait_send()
        mine.wait()


class _ScatterChips:
    def __init__(self, q):
        _, r, w = q.shape
        self.land = _sds((3, r, w), q.dtype)
        self.sems = [pltpu.SemaphoreType.DMA((3,)), pltpu.SemaphoreType.DMA((3,))]

    def _copies(self, q_ref, land_ref, send_sems, recv_sems):
        x, y, c = lax.axis_index("x"), lax.axis_index("y"), lax.axis_index("c")
        chips = [(1 - x, y), (x, 1 - y), (1 - x, 1 - y)]
        return [pltpu.make_async_remote_copy(
            src_ref=q_ref.at[2 * cx + cy], dst_ref=land_ref.at[k], send_sem=send_sems.at[k], recv_sem=recv_sems.at[k],
            device_id=(cx, cy, c), device_id_type=MESH) for k, (cx, cy) in enumerate(chips)]

    def start(self, *refs):
        for cp in self._copies(*refs):
            cp.start()

    def finish(self, *refs):
        copies = self._copies(*refs)
        for cp in copies:
            cp.wait_recv()
        for cp in copies:
            cp.wait_send()


def _hosted_call(body, comm, *, grid, in_specs, out_specs, out_shape, scratch_shapes, name, sem, args, vmem_mb=VMEM_MB):
    def first():
        return functools.reduce(jnp.logical_and, [pl.program_id(a) == 0 for a in range(len(grid))])

    def last():
        return functools.reduce(jnp.logical_and, [pl.program_id(a) == g - 1 for a, g in enumerate(grid)])

    out_specs = tuple(out_specs) if isinstance(out_specs, (tuple, list)) else (out_specs,)
    out_shape = tuple(out_shape) if isinstance(out_shape, (tuple, list)) else (out_shape,)
    if comm is None:
        return pl.pallas_call(body, grid=grid, in_specs=list(in_specs), out_specs=out_specs, out_shape=out_shape,
                              scratch_shapes=list(scratch_shapes), name=name, compiler_params=_cp(sem, vmem_mb))(*args)
    ex, src = comm
    n_in, n_out, n_scr = len(in_specs), len(out_shape), len(scratch_shapes)

    def wrapped(*refs):
        ins, src_ref = refs[:n_in], refs[n_in]
        outs, land_ref = refs[n_in + 1:n_in + 1 + n_out], refs[n_in + 1 + n_out]
        scr = refs[n_in + 2 + n_out:n_in + 2 + n_out + n_scr]
        sems = refs[n_in + 2 + n_out + n_scr:]

        @pl.when(first())
        def _():
            ex.start(src_ref, land_ref, *sems)

        body(*ins, *outs, *scr)

        @pl.when(last())
        def _():
            ex.finish(src_ref, land_ref, *sems)

    hbm = pl.BlockSpec(memory_space=pl.ANY)
    return pl.pallas_call(
        wrapped, grid=grid, in_specs=[*in_specs, hbm], out_specs=(*out_specs, hbm), out_shape=(*out_shape, ex.land),
        scratch_shapes=[*scratch_shapes, *ex.sems], name=name,
        compiler_params=_cp(tuple("arbitrary" for _ in grid), vmem_mb))(*args, src)


def _allgather8(x_shard, name, in_vmem):
    ex = _Gather8(x_shard)

    def body(*refs):
        ex.start(*refs)
        ex.finish(*refs)

    space = pltpu.VMEM if in_vmem else pl.ANY
    return pl.pallas_call(
        body, out_shape=ex.land, in_specs=[pl.BlockSpec(memory_space=space)], out_specs=pl.BlockSpec(memory_space=space),
        scratch_shapes=ex.sems, name=name)(x_shard)


def _swap_halves(p, name):
    nb, _, r, w = p.shape

    def body(p_ref, land_ref, send_sems, recv_sems):
        x, y, c = lax.axis_index("x"), lax.axis_index("y"), lax.axis_index("c")
        copies = [pltpu.make_async_remote_copy(
            src_ref=p_ref.at[j, 1 - c], dst_ref=land_ref.at[j], send_sem=send_sems.at[j], recv_sem=recv_sems.at[j],
            device_id=(x, y, 1 - c), device_id_type=MESH) for j in range(nb)]
        for cp in copies:
            cp.start()
        for cp in copies:
            cp.wait_recv()
        for cp in copies:
            cp.wait_send()

    return pl.pallas_call(
        body, out_shape=_sds((nb, r, w), p.dtype),
        in_specs=[pl.BlockSpec(memory_space=pl.ANY)], out_specs=pl.BlockSpec(memory_space=pl.ANY),
        scratch_shapes=[pltpu.SemaphoreType.DMA((nb,)), pltpu.SemaphoreType.DMA((nb,))],
        name=name)(p)


def _scatter_chips(q, name):
    ex = _ScatterChips(q)

    def body(*refs):
        ex.start(*refs)
        ex.finish(*refs)

    return pl.pallas_call(
        body, out_shape=ex.land, in_specs=[pl.BlockSpec(memory_space=pl.ANY)], out_specs=pl.BlockSpec(memory_space=pl.ANY),
        scratch_shapes=ex.sems, name=name)(q)


def _share_halves(bufs):
    n = len(bufs)

    def body(*refs):
        b_refs, out_refs, send_sems, recv_sems = refs[:n], refs[n:2 * n], refs[2 * n], refs[2 * n + 1]
        x, y, c = lax.axis_index("x"), lax.axis_index("y"), lax.axis_index("c")
        copies = [pltpu.make_async_remote_copy(
            src_ref=b_refs[k].at[c], dst_ref=out_refs[k].at[c], send_sem=send_sems.at[k], recv_sem=recv_sems.at[k],
            device_id=(x, y, 1 - c), device_id_type=MESH) for k in range(n)]
        for cp in copies:
            cp.start()
        for cp in copies:
            cp.wait_recv()
        for cp in copies:
            cp.wait_send()

    hbm = pl.BlockSpec(memory_space=pl.ANY)
    return pl.pallas_call(
        body, out_shape=tuple(_sds(b.shape, b.dtype) for b in bufs), in_specs=[hbm] * n, out_specs=tuple([hbm] * n),
        scratch_shapes=[pltpu.SemaphoreType.DMA((n,)), pltpu.SemaphoreType.DMA((n,))],
        input_output_aliases={k: k for k in range(n)}, name="rs_share_halves")(*bufs)


def _add_halves(p, land, sel, name):
    nb, _, r, w = p.shape
    tr = _row_tile(r, 16)

    def body(sel_ref, p_ref, l_ref, q16_ref, own_ref):
        q = p_ref[0, 0] + l_ref[0]
        q16_ref[0] = q.astype(BF16)

        @pl.when(pl.program_id(1) == sel_ref[1])
        def _():
            own_ref[...] = q

    gs = pltpu.PrefetchScalarGridSpec(
        num_scalar_prefetch=1, grid=(r // tr, nb),
        in_specs=[pl.BlockSpec((1, 1, tr, w), lambda i, j, sl: (j, sl[0], i, 0)),
                  pl.BlockSpec((1, tr, w), lambda i, j, sl: (j, i, 0))],
        out_specs=(pl.BlockSpec((1, tr, w), lambda i, j, sl: (j, i, 0)), pl.BlockSpec((tr, w), lambda i, j, sl: (i, 0))))
    return pl.pallas_call(body, grid_spec=gs, out_shape=(_sds((nb, r, w), BF16), _sds((r, w), F32)), name=name,
                          compiler_params=_cp(("parallel", "arbitrary")))(sel, p, land)


def _add_chips(own, land, sel, name):
    r, w = own.shape
    tr = _row_tile(r, 16)

    def body(sel_ref, q_ref, l_ref, o_ref):
        o_ref[0] = ((q_ref[...] + l_ref[0].astype(F32)) + l_ref[1].astype(F32)) + l_ref[2].astype(F32)

    gs = pltpu.PrefetchScalarGridSpec(
        num_scalar_prefetch=1, grid=(r // tr,),
        in_specs=[pl.BlockSpec((tr, w), lambda i, sl: (i, 0)), pl.BlockSpec((3, tr, w), lambda i, sl: (0, i, 0))],
        out_specs=pl.BlockSpec((1, tr, w), lambda i, sl: (sl[0], i, 0)))
    return pl.pallas_call(body, grid_spec=gs, out_shape=_sds((2, r, w), F32), name=name,
                          compiler_params=_cp(("parallel",)))(sel, own, land)


def _sum8(g):
    _, m, n = g.shape

    def body(g_ref, o_ref):
        acc = g_ref[0]
        for k in range(1, 8):
            acc = acc + g_ref[k]
        o_ref[...] = acc

    return pl.pallas_call(body, out_shape=_sds((m, n), g.dtype), name="sum8")(g)


def _ada_fwd(c_all, w3, name):
    nl, d, n = w3.shape
    tn = 256

    def body(c_ref, w_ref, o_ref):
        cc = c_ref[...]
        ca = (cc * _sigmoid(cc)).astype(BF16)
        o_ref[0] = _dot(ca, w_ref[0].astype(BF16))

    return pl.pallas_call(
        body, grid=(nl, n // tn), out_shape=_sds((nl, 8, n), F32),
        in_specs=[pl.BlockSpec((8, d), lambda l, j: (0, 0)), pl.BlockSpec((1, d, tn), lambda l, j: (l, 0, j))],
        out_specs=pl.BlockSpec((1, 8, tn), lambda l, j: (l, 0, j)),
        name=name, compiler_params=_cp(("parallel", "parallel")))(c_all, w3)


def _in_pair(x, gain, shift, scale, wg, bias, conv, name, comm=None):
    s, d = x.shape
    n = wg.shape[2]
    ts = _tile(s, 512)

    def body(*refs):
        if conv:
            x_ref, g_ref, sh_ref, sc_ref, wa_ref, wb_ref, ba_ref, bb_ref, h_ref, o_ref, sa_ref, sb_ref, hs = refs
        else:
            x_ref, g_ref, sh_ref, sc_ref, wa_ref, wb_ref, h_ref, o_ref, sa_ref, sb_ref, hs = refs

        @pl.when(pl.program_id(1) == 0)
        def _():
            xhat, _ = _rms_parts(x_ref[...])
            h = (xhat * g_ref[...]) * (1.0 + sc_ref[...]) + sh_ref[...]
            hs[...] = h.astype(BF16)
            h_ref[...] = hs[...]

        h = hs[...]
        a = _dot(h, wa_ref[0])
        b = _dot(h, wb_ref[0])
        if conv:
            a = a + ba_ref[0]
            b = b + bb_ref[0]
            o_ref[...] = a * _sigmoid(b)
        else:
            o_ref[...] = (a * _sigmoid(a) * b).astype(BF16)
        sa_ref[...] = a.astype(BF16)
        sb_ref[...] = b.astype(BF16)

    vec = pl.BlockSpec((1, d), lambda i, q: (0, 0))
    in_specs = [pl.BlockSpec((ts, d), lambda i, q: (i, 0)), vec, vec, vec,
                pl.BlockSpec((1, d, n), lambda i, q: (q, 0, 0)), pl.BlockSpec((1, d, n), lambda i, q: (q + 2, 0, 0))]
    args = [x, gain, shift, scale, wg, wg]
    if conv:
        in_specs += [pl.BlockSpec((1, 1, n), lambda i, q: (q, 0, 0)), pl.BlockSpec((1, 1, n), lambda i, q: (q + 2, 0, 0))]
        args += [bias, bias]
    tile = pl.BlockSpec((ts, n), lambda i, q: (i, q))
    return _hosted_call(
        body, comm, grid=(s // ts, 2),
        out_shape=(_sds((s, d), BF16), _sds((s, 2 * n), F32 if conv else BF16), _sds((s, 2 * n), BF16), _sds((s, 2 * n), BF16)),
        in_specs=in_specs, out_specs=(pl.BlockSpec((ts, d), lambda i, q: (i, 0)), tile, tile, tile),
        scratch_shapes=[pltpu.VMEM((ts, d), BF16)], name=name, sem=("parallel", "arbitrary"), args=args)


def _dwconv_fwd(glu, wdw, bdw, lng, lnb, comm=None):
    s, d = glu.shape
    ts = _tile(s, 256)
    rb, cb = 32, 256

    def body(cur_ref, halo_ref, w_ref, b_ref, g_ref, be_ref, dwo_ref, sw_ref, buf):
        i = pl.program_id(0)

        @pl.when(i == 0)
        def _():
            buf[pl.ds(0, HALO), :] = jnp.zeros((HALO, d), F32)

        @pl.when(i > 0)
        def _():
            buf[pl.ds(0, HALO), :] = halo_ref[...]

        buf[pl.ds(HALO, ts), :] = cur_ref[...]
        for r in range(ts // rb):
            for cc in range(d // cb):
                cs = pl.ds(cc * cb, cb)
                acc = jnp.zeros((rb, cb), F32) + b_ref[:, cs]
                for k in range(CONV_K):
                    acc = acc + w_ref[pl.ds(k, 1), cs] * buf[pl.ds(HALO - (CONV_K - 1) + k + r * rb, rb), cs]
                dwo_ref[pl.ds(r * rb, rb), cs] = acc
            rows = pl.ds(r * rb, rb)
            yv = dwo_ref[rows, :]
            mu = jnp.mean(yv, axis=-1, keepdims=True)
            yc = yv - mu
            var = jnp.mean(yc * yc, axis=-1, keepdims=True)
            ln = yc * lax.rsqrt(var + EPS) * g_ref[...] + be_ref[...]
            sw_ref[rows, :] = (ln * _sigmoid(ln)).astype(BF16)

    vec = pl.BlockSpec((1, d), lambda i: (0, 0))
    return _hosted_call(
        body, comm, grid=(s // ts,), out_shape=(_sds((s, d), F32), _sds((s, d), BF16)),
        in_specs=[pl.BlockSpec((ts, d), lambda i: (i, 0)),
                  pl.BlockSpec((HALO, d), lambda i: (jnp.maximum(i * (ts // HALO) - 1, 0), 0)),
                  pl.BlockSpec((HALO, d), lambda i: (0, 0)), vec, vec, vec],
        out_specs=(pl.BlockSpec((ts, d), lambda i: (i, 0)), pl.BlockSpec((ts, d), lambda i: (i, 0))),
        scratch_shapes=[pltpu.VMEM((HALO + ts, d), F32)], name="dwconv_fwd", sem=("parallel",),
        args=(glu, glu, wdw, bdw, lng, lnb))


def _mm_res(a, w, b, gate, x, name):
    s, k = a.shape
    d = w.shape[1]
    ts = _tile(s, 512)

    def body(a_ref, w_ref, b_ref, g_ref, x_ref, o_ref):
        yv = _dot(a_ref[...], w_ref[...]) + b_ref[...]
        o_ref[...] = x_ref[...] + g_ref[...] * yv

    vec = pl.BlockSpec((1, d), lambda i: (0, 0))
    return pl.pallas_call(
        body, grid=(s // ts,), out_shape=_sds((s, d), F32),
        in_specs=[pl.BlockSpec((ts, k), lambda i: (i, 0)), _resident((k, d)), vec, vec, pl.BlockSpec((ts, d), lambda i: (i, 0))],
        out_specs=pl.BlockSpec((ts, d), lambda i: (i, 0)),
        name=name, compiler_params=_cp(("parallel",)))(a, w, b, gate, x)


def _qkv(x, kvp, mxp, wk, wv, wf, wq):
    s, d = x.shape
    ts = _tile(s, 512)
    qscale = HEAD_DIM ** -0.5

    def body(x_ref, gk, shk, sck, gm, shm, scm, wk_ref, wv_ref, wf_ref, wq_ref, hk_ref, h1_ref, k_ref, v_ref, q_ref, f_ref):
        xhat, _ = _rms_parts(x_ref[...])
        hk = ((xhat * gk[...]) * (1.0 + sck[...]) + shk[...]).astype(BF16)
        h1 = ((xhat * gm[...]) * (1.0 + scm[...]) + shm[...]).astype(BF16)
        hk_ref[...] = hk
        h1_ref[...] = h1
        k_ref[...] = _dot(hk, wk_ref[...]).astype(BF16)
        v_ref[...] = _dot(hk, wv_ref[...]).astype(BF16)
        f_ref[...] = _dot(hk, wf_ref[...])
        q_ref[...] = (_dot(h1, wq_ref[...]) * qscale).astype(BF16)

    vec = pl.BlockSpec((1, d), lambda i: (0, 0))
    row = pl.BlockSpec((ts, d), lambda i: (i, 0))
    return pl.pallas_call(
        body, grid=(s // ts,),
        out_shape=tuple(_sds((s, d), BF16) for _ in range(5)) + (_sds((s, LANE), F32),),
        in_specs=[row, vec, vec, vec, vec, vec, vec, _resident((d, d)), _resident((d, d)), _resident((d, LANE)), _resident((d, d))],
        out_specs=(row, row, row, row, row, pl.BlockSpec((ts, LANE), lambda i: (i, 0))),
        name="qkv_proj", compiler_params=_cp(("parallel",)))(x, *kvp, *mxp, wk, wv, wf, wq)


def _log_sigmoid(z):
    return jnp.minimum(z, 0.0) - jnp.log(1.0 + jnp.exp(-jnp.abs(z)))


def _cumsum_fwd(flog, fb):
    s = flog.shape[0]
    ts = _tile(s, 256)

    def body(f_ref, b_ref, cum_ref, cumt_ref, carry):
        @pl.when(pl.program_id(0) == 0)
        def _():
            carry[...] = jnp.zeros_like(carry)

        ls = _log_sigmoid(f_ref[...] + b_ref[...])
        r = lax.broadcasted_iota(jnp.int32, (ts, ts), 0)
        cidx = lax.broadcasted_iota(jnp.int32, (ts, ts), 1)
        tri = (cidx <= r).astype(F32)
        cs = jnp.dot(tri, ls, preferred_element_type=F32, precision=lax.Precision.HIGHEST) + carry[...]
        cum_ref[...] = cs
        cumt_ref[...] = cs.T
        carry[...] = cs[ts - 1:ts, :]

    return pl.pallas_call(
        body, grid=(s // ts,), out_shape=(_sds((s, LANE), F32), _sds((LANE, s), F32)),
        in_specs=[pl.BlockSpec((ts, LANE), lambda i: (i, 0)), pl.BlockSpec((1, LANE), lambda i: (0, 0))],
        out_specs=(pl.BlockSpec((ts, LANE), lambda i: (i, 0)), pl.BlockSpec((LANE, ts), lambda i: (0, i))),
        scratch_shapes=[pltpu.VMEM((1, LANE), F32)],
        name="forget_cumsum", compiler_params=_cp(("arbitrary",)))(flog, fb)


def _pick_row(m, idx):
    r = lax.broadcasted_iota(jnp.int32, (m.shape[0], 1), 0)
    return jnp.sum(jnp.where(r == idx, m, 0.0), axis=0, keepdims=True)


def _pick_col(m, idx):
    cidx = lax.broadcasted_iota(jnp.int32, (1, m.shape[1]), 1)
    return jnp.sum(jnp.where(cidx == idx, m, 0.0), axis=1, keepdims=True)


def _split3(x):
    hi = x.astype(BF16)
    r1 = x - hi.astype(F32)
    mid = r1.astype(BF16)
    lo = (r1 - mid.astype(F32)).astype(BF16)
    return hi, mid, lo


def _head_mask(lane, hh):
    lo = lane < HEAD_DIM
    return lo if hh == 0 else jnp.logical_not(lo)


def _attn_prep(k, v, cum):
    s, d = k.shape
    npair = d // LANE
    tc = _tile(s, 512)

    def body(k_ref, v_ref, c_ref, ka_ref, kt_ref, vt_ref):
        p = pl.program_id(0)
        lane = lax.broadcasted_iota(jnp.int32, (1, LANE), 1)
        kk = k_ref[...]
        vv = v_ref[...].astype(F32)
        ckt = c_ref[...]
        for hh in range(2):
            head = _head_mask(lane, hh)
            b = SPARE[hh]
            ck = _pick_col(ckt, 2 * p + hh)
            extra = jnp.where(lane == b + NPIECE, 1.0, 0.0).astype(BF16) + jnp.zeros((tc, LANE), BF16)
            for n_, pc in enumerate(_split3(ck)):
                extra = jnp.where(lane == b + n_, pc, extra)
            ka = jnp.where(head, kk, extra)
            ka_ref[0, hh] = ka
            kt_ref[0, hh] = ka.astype(F32).T.astype(BF16)
            vx = jnp.where(head, vv, jnp.where(lane == b, 1.0, 0.0))
            vt_ref[0, hh] = vx.T.astype(BF16)

    blk = pl.BlockSpec((tc, LANE), lambda p, c: (c, p))
    return pl.pallas_call(
        body, grid=(npair, s // tc),
        out_shape=(_sds((npair, 2, s, LANE), BF16), _sds((npair, 2, LANE, s), BF16), _sds((npair, 2, LANE, s), BF16)),
        in_specs=[blk, blk, pl.BlockSpec((tc, LANE), lambda p, c: (c, 0))],
        out_specs=(pl.BlockSpec((1, 2, tc, LANE), lambda p, c: (p, 0, c, 0)),
                   pl.BlockSpec((1, 2, LANE, tc), lambda p, c: (p, 0, 0, c)),
                   pl.BlockSpec((1, 2, LANE, tc), lambda p, c: (p, 0, 0, c))),
        name="fox_attn_prep", compiler_params=_cp(("parallel", "parallel")))(k, v, cum)


def _q_aug(qq, lane, hh):
    b = SPARE[hh]
    sel = jnp.logical_and(lane >= b, lane < b + NPIECE)
    neg = jnp.full((1, LANE), -1.0, BF16)
    zl = jnp.zeros((1, LANE), BF16)
    return jnp.where(_head_mask(lane, hh), qq, jnp.where(sel, neg, zl))


def _attn_fwd(q, kaug, vtr, cumt):
    s, d = q.shape
    tq = _tile(s, ATT_TQ)
    tk = _tile(s, ATT_TK)
    npair = d // LANE
    npart = max(1, tq // tk)

    def body(q_ref, ka_ref, vt_ref, cumt_ref, o_ref, lse_ref):
        p = pl.program_id(0)
        i = pl.program_id(1)
        lane = lax.broadcasted_iota(jnp.int32, (1, LANE), 1)
        qq = q_ref[...]
        qx = (_q_aug(qq, lane, 0), _q_aug(qq, lane, 1))
        cqt = cumt_ref[:, pl.ds(pl.multiple_of(i * tq, tq), tq)]
        cq = (_pick_row(cqt, 2 * p), _pick_row(cqt, 2 * p + 1))
        jd = (i * tq) // tk

        def kv_step(j, carry, diag):
            ks = pl.multiple_of(j * tk, tk)
            if diag:
                krow = lax.broadcasted_iota(jnp.int32, (tk, tq), 0) + j * tk
                qcol = lax.broadcasted_iota(jnp.int32, (tk, tq), 1) + i * tq
                causal = krow <= qcol
            out = []
            for hh in range(2):
                m, acc = carry[2 * hh], carry[2 * hh + 1]
                sc = _dot_nt(ka_ref[0, hh, pl.ds(ks, tk), :], qx[hh])
                if diag:
                    sc = jnp.where(causal, sc, -jnp.inf)
                mx = jnp.max(sc, axis=0, keepdims=True) + cq[hh]
                mn = jnp.maximum(m, mx)
                alpha = jnp.exp(m - mn)
                pt = jnp.exp(sc + (cq[hh] - mn)).astype(BF16)
                acc = alpha * acc + _dot(vt_ref[0, hh, :, pl.ds(ks, tk)], pt)
                out += [mn, acc]
            return tuple(out)

        minit = jnp.full((1, tq), -jnp.inf, F32)
        ainit = jnp.zeros((LANE, tq), F32)
        carry = (minit, ainit, minit, ainit)
        for pj in range(npart):
            carry = kv_step(jd + pj, carry, True)
        carry = lax.fori_loop(0, jd, lambda j, cr: kv_step(j, cr, False), carry)
        m0, a0, m1, a1 = carry
        l0 = a0[SPARE[0]:SPARE[0] + 1, :]
        l1 = a1[SPARE[1]:SPARE[1] + 1, :]
        row = lax.broadcasted_iota(jnp.int32, (LANE, 1), 0)
        ot = jnp.where(row < HEAD_DIM, a0 / l0, a1 / l1)
        o_ref[...] = ot.T.astype(BF16)
        r8 = lax.broadcasted_iota(jnp.int32, (8, 1), 0)
        lse_ref[0] = jnp.where(r8 == 0, m0 + jnp.log(l0), jnp.where(r8 == 1, m1 + jnp.log(l1), 0.0))

    return pl.pallas_call(
        body, grid=(npair, s // tq), out_shape=(_sds((s, d), BF16), _sds((npair, 8, s), F32)),
        in_specs=[pl.BlockSpec((tq, LANE), lambda p, i: (i, p)),
                  pl.BlockSpec((1, 2, s, LANE), lambda p, i: (p, 0, 0, 0)),
                  pl.BlockSpec((1, 2, LANE, s), lambda p, i: (p, 0, 0, 0)),
                  pl.BlockSpec((N_HEADS, s), lambda p, i: (0, 0))],
        out_specs=(pl.BlockSpec((tq, LANE), lambda p, i: (i, p)), pl.BlockSpec((1, 8, tq), lambda p, i: (p, 0, i))),
        name="fox_attn_fwd", compiler_params=_cp(("parallel", "parallel")))(q, kaug, vtr, cumt)


def _final(x, gain, target):
    s, d = x.shape
    ts = _tile(s, 512)

    def body(x_ref, g_ref, t_ref, lsum_ref, dx_ref, dg_ref):
        @pl.when(pl.program_id(0) == 0)
        def _():
            lsum_ref[...] = jnp.zeros_like(lsum_ref)
            dg_ref[...] = jnp.zeros_like(dg_ref)

        xhat, rstd = _rms_parts(x_ref[...])
        e = xhat * g_ref[...] - t_ref[...]
        lsum_ref[...] += _colsum(e * e)
        dout = e * (1.0 / d)
        dg_ref[...] += _colsum(dout * xhat)
        dxhat = dout * g_ref[...]
        dx_ref[...] = rstd * (dxhat - xhat * jnp.mean(dxhat * xhat, axis=-1, keepdims=True))

    vec = pl.BlockSpec((1, d), lambda i: (0, 0))
    row = pl.BlockSpec((ts, d), lambda i: (i, 0))
    return pl.pallas_call(
        body, grid=(s // ts,), out_shape=(_sds((1, d), F32), _sds((s, d), F32), _sds((1, d), F32)),
        in_specs=[row, vec, row], out_specs=(vec, row, vec),
        name="final_norm_loss", compiler_params=_cp(("arbitrary",)))(x, gain, target)


def _ffn_bwd_act(dx, gate, w_out, ug, uu, name, comm=None):
    s, d = dx.shape
    f = w_out.shape[0]
    n = f // 2
    ts = _tile(s, 512)

    def body(dx_ref, g_ref, w_ref, ug_ref, uu_ref, dug_ref, duu_ref, dys):
        @pl.when(pl.program_id(1) == 0)
        def _():
            dys[...] = (dx_ref[...] * g_ref[...]).astype(BF16)

        dact = _dot_nt(dys[...], w_ref[...])
        g = ug_ref[...].astype(F32)
        u = uu_ref[...].astype(F32)
        sg = _sigmoid(g)
        dug_ref[...] = (dact * u * sg * (1.0 + g * (1.0 - sg))).astype(BF16)
        duu_ref[...] = (dact * g * sg).astype(BF16)

    tile = pl.BlockSpec((ts, n), lambda i, q: (i, q))
    return _hosted_call(
        body, comm, grid=(s // ts, 2), out_shape=(_sds((s, f), BF16), _sds((s, f), BF16)),
        in_specs=[pl.BlockSpec((ts, d), lambda i, q: (i, 0)), pl.BlockSpec((1, d), lambda i, q: (0, 0)),
                  pl.BlockSpec((n, d), lambda i, q: (q, 0)), tile, tile],
        out_specs=(tile, tile), scratch_shapes=[pltpu.VMEM((ts, d), BF16)],
        name=name, sem=("parallel", "arbitrary"), args=(dx, gate, w_out, ug, uu))


def _dw_mm(a, b_list, tk, tn, name, gate=None, wfull=None, dgate_init=None):
    s, kdim = a.shape
    nb1 = b_list[0].shape[1] // tn
    nb = nb1 * len(b_list)
    ts = _tile(s, 512)
    nk = kdim // tk
    ns = s // ts
    gated = gate is not None

    def body(*refs):
        a_ref = refs[0]
        b_refs = refs[1:1 + len(b_list)]
        rest = refs[1 + len(b_list):]
        if gated:
            g_ref, w_ref, di_ref, o_ref, dg_ref, acc = rest
        else:
            o_ref, acc = rest
        jn, ik, st = pl.program_id(0), pl.program_id(1), pl.program_id(2)

        @pl.when(st == 0)
        def _():
            acc[...] = jnp.zeros_like(acc)

        for mi, b_ref in enumerate(b_refs):
            @pl.when(jn // nb1 == mi)
            def _(b_ref=b_ref):
                acc[...] += _dot_tn(a_ref[...], b_ref[...].astype(BF16))

        if gated:
            @pl.when(jnp.logical_and(ik == 0, st == 0))
            def _():
                dg_ref[...] = di_ref[...]

        @pl.when(st == ns - 1)
        def _():
            if gated:
                o_ref[0] = acc[...] * g_ref[...]
                dg_ref[...] += _colsum(acc[...] * w_ref[...].astype(F32))
            else:
                o_ref[0] = acc[...]

    in_specs = [pl.BlockSpec((ts, tk), lambda jn, ik, st: (st, ik))]
    for mi in range(len(b_list)):
        in_specs.append(pl.BlockSpec(
            (ts, tn), lambda jn, ik, st, mi=mi: (st, jnp.clip(jn - mi * nb1, 0, nb1 - 1))))
    args = [a] + list(b_list)
    out_shape = [_sds((nb, kdim, tn), F32)]
    out_specs = [pl.BlockSpec((1, tk, tn), lambda jn, ik, st: (jn, ik, 0))]
    if gated:
        vec = pl.BlockSpec((1, tn), lambda jn, ik, st: (0, jn))
        in_specs += [vec, pl.BlockSpec((tk, tn), lambda jn, ik, st: (ik, jn)), vec]
        args += [gate, wfull, dgate_init]
        out_shape.append(_sds((1, nb * tn), F32))
        out_specs.append(vec)
    res = pl.pallas_call(
        body, grid=(nb, nk, ns), out_shape=tuple(out_shape), in_specs=in_specs, out_specs=tuple(out_specs),
        scratch_shapes=[pltpu.VMEM((tk, tn), F32)],
        name=name, compiler_params=_cp(("parallel", "arbitrary", "arbitrary")))(*args)
    return res if gated else res[0]


def _mm_normbwd(terms, x, dxres, gain, scale, name, ts_pref=256, comm=None):
    s, d = x.shape
    ts = _tile(s, ts_pref)
    arrs, warrs = [], []
    for a, _, w, _ in terms:
        if not any(a is z for z in arrs):
            arrs.append(a)
        if not any(w is z for z in warrs):
            warrs.append(w)
    ai = [next(i for i, z in enumerate(arrs) if z is a) for a, _, _, _ in terms]
    wi = [next(i for i, z in enumerate(warrs) if z is w) for _, _, w, _ in terms]

    def body(*refs):
        a_refs = refs[:len(arrs)]
        w_refs = refs[len(arrs):len(arrs) + len(warrs)]
        x_ref, dr_ref, g_ref, sc_ref, dx_ref, dsh_ref, dsc_ref, dg_ref = refs[len(arrs) + len(warrs):]

        @pl.when(pl.program_id(0) == 0)
        def _():
            dsh_ref[...] = jnp.zeros_like(dsh_ref)
            dsc_ref[...] = jnp.zeros_like(dsc_ref)
            dg_ref[...] = jnp.zeros_like(dg_ref)

        dh = None
        for ti, (_, c0, w, q) in enumerate(terms):
            n = w.shape[2]
            part = _dot_nt(a_refs[ai[ti]][:, pl.ds(c0, n)], w_refs[wi[ti]][q])
            dh = part if dh is None else dh + part
        xhat, rstd = _rms_parts(x_ref[...])
        nrm = xhat * g_ref[...]
        dsh_ref[...] += _colsum(dh)
        dsc_ref[...] += _colsum(dh * nrm)
        dn = dh * (1.0 + sc_ref[...])
        dg_ref[...] += _colsum(dn * xhat)
        dxhat = dn * g_ref[...]
        dx_ref[...] = dr_ref[...] + rstd * (dxhat - xhat * jnp.mean(dxhat * xhat, axis=-1, keepdims=True))

    vec = pl.BlockSpec((1, d), lambda i: (0, 0))
    row = pl.BlockSpec((ts, d), lambda i: (i, 0))
    in_specs = [pl.BlockSpec((ts, a.shape[1]), lambda i: (i, 0)) for a in arrs]
    in_specs += [_resident(w.shape) for w in warrs]
    in_specs += [row, row, vec, vec]
    return _hosted_call(
        body, comm, grid=(s // ts,), out_shape=(_sds((s, d), F32), _sds((1, d), F32), _sds((1, d), F32), _sds((1, d), F32)),
        in_specs=in_specs, out_specs=(row, vec, vec, vec), scratch_shapes=[],
        name=name, sem=("arbitrary",), args=(*arrs, *warrs, x, dxres, gain, scale))


def _do_kernel(dx, gate, wo, o):
    s, d = dx.shape
    ts = _tile(s, 512)

    def body(dx_ref, g_ref, w_ref, o_ref, do_ref, dl_ref):
        dy = (dx_ref[...] * g_ref[...]).astype(BF16)
        do = _dot_nt(dy, w_ref[...])
        do_ref[...] = do.astype(BF16)
        prod = do * o_ref[...].astype(F32)
        hrow = lax.broadcasted_iota(jnp.int32, (N_HEADS, d), 0)
        hcol = lax.broadcasted_iota(jnp.int32, (N_HEADS, d), 1) // HEAD_DIM
        sel = (hrow == hcol).astype(F32)
        dl_ref[...] = lax.dot_general(sel, prod, (((1,), (1,)), ((), ())), preferred_element_type=F32,
                                      precision=lax.Precision.HIGHEST)

    row = pl.BlockSpec((ts, d), lambda i: (i, 0))
    return pl.pallas_call(
        body, grid=(s // ts,), out_shape=(_sds((s, d), BF16), _sds((N_HEADS, s), F32)),
        in_specs=[row, pl.BlockSpec((1, d), lambda i: (0, 0)), _resident(wo.shape), row],
        out_specs=(row, pl.BlockSpec((N_HEADS, ts), lambda i: (0, i))),
        name="attn_do", compiler_params=_cp(("parallel",)))(dx, gate, wo, o)


def _attn_bwd(q, do, kaug, kaugt, v, cumt, lse, deltat, comm=None):
    s, d = q.shape
    tq = _tile(s, ATT_TQ)
    tk = _tile(s, ATT_TK)
    assert tq % tk == 0
    npair = d // LANE
    nq = s // tq
    nkb = s // tk
    qscale = HEAD_DIM ** -0.5

    def body(q_ref, do_ref, ka_ref, kt_ref, v_ref, cumt_ref, lse_ref, dl_ref,
             dq_ref, dk_ref, dv_ref, dcq_ref, dck_ref, qaug, dom, rowv, dqt):
        p = pl.program_id(0)
        j = pl.program_id(1)
        lane = lax.broadcasted_iota(jnp.int32, (1, LANE), 1)
        lo = lane < HEAD_DIM
        r8 = lax.broadcasted_iota(jnp.int32, (8, 1), 0)

        @pl.when(j == 0)
        def _():
            dqt[...] = jnp.zeros_like(dqt)
            for c in range(nq):
                rows = pl.ds(c * tq, tq)
                qq = q_ref[rows, :]
                dd = do_ref[rows, :]
                cqt = cumt_ref[:, rows]
                dlt = dl_ref[:, rows]
                lst = lse_ref[0, :, rows]
                for hh in range(2):
                    qaug[hh, rows, :] = _q_aug(qq, lane, hh)
                    dom[hh, rows, :] = jnp.where(_head_mask(lane, hh), dd, jnp.zeros_like(dd))
                    rowv[hh, :, rows] = jnp.where(
                        r8 == 0, _pick_row(cqt, 2 * p + hh) - lst[hh:hh + 1, :],
                        jnp.where(r8 == 1, _pick_row(dlt, 2 * p + hh), 0.0))

        vv = v_ref[...]
        i0 = (j * tk) // tq

        def q_step(i, carry, diag):
            dv_acc, dk0, dk1 = carry
            qs = pl.multiple_of(i * tq, tq)
            if diag:
                krow = lax.broadcasted_iota(jnp.int32, (tk, tq), 0) + j * tk
                qcol = lax.broadcasted_iota(jnp.int32, (tk, tq), 1) + i * tq
                causal = krow <= qcol
            dks = [dk0, dk1]
            for hh in range(2):
                rv = rowv[hh, :, pl.ds(qs, tq)]
                qa = qaug[hh, pl.ds(qs, tq), :]
                dh = dom[hh, pl.ds(qs, tq), :]
                sc = _dot_nt(ka_ref[0, hh], qa)
                if diag:
                    sc = jnp.where(causal, sc, -jnp.inf)
                pt = jnp.exp(sc + rv[0:1, :])
                dpt = _dot_nt(vv, dh)
                dst = (pt * (dpt - rv[1:2, :])).astype(BF16)
                dv_acc = dv_acc + _dot(pt.astype(BF16), dh)
                dks[hh] = dks[hh] + _dot(dst, qa)
                dqt[hh, :, pl.ds(qs, tq)] += _dot(kt_ref[0, hh], dst)
            return dv_acc, dks[0], dks[1]

        z = jnp.zeros((tk, LANE), F32)
        carry = q_step(i0, (z, z, z), True)
        dv_acc, dk0, dk1 = lax.fori_loop(i0 + 1, nq, lambda i, cr: q_step(i, cr, False), carry)
        dv_ref[...] = dv_acc.astype(BF16)
        dk_ref[...] = jnp.where(lo, dk0, dk1).astype(BF16)
        dck_ref[0] = jnp.where(r8 == 0, dk0.T[SPARE[0]:SPARE[0] + 1, :],
                               jnp.where(r8 == 1, dk1.T[SPARE[1]:SPARE[1] + 1, :], 0.0))

        @pl.when(j == nkb - 1)
        def _():
            for c in range(nq):
                rows = pl.ds(c * tq, tq)
                a0 = dqt[0, :, rows].T
                a1 = dqt[1, :, rows].T
                dq_ref[rows, :] = (jnp.where(lo, a0, a1) * qscale).astype(BF16)
            r0, r1 = SPARE[0] + NPIECE, SPARE[1] + NPIECE
            dcq_ref[0] = jnp.where(r8 == 0, dqt[0, r0:r0 + 1, :], jnp.where(r8 == 1, dqt[1, r1:r1 + 1, :], 0.0))

    col = pl.BlockSpec((s, LANE), lambda p, j: (0, p), pipeline_mode=pl.Buffered(1))
    rows16 = pl.BlockSpec((N_HEADS, s), lambda p, j: (0, 0), pipeline_mode=pl.Buffered(1))
    blk = pl.BlockSpec((tk, LANE), lambda p, j: (j, p))
    return _hosted_call(
        body, comm, grid=(npair, nkb),
        out_shape=(_sds((s, d), BF16), _sds((s, d), BF16), _sds((s, d), BF16), _sds((npair, 8, s), F32), _sds((npair, 8, s), F32)),
        in_specs=[col, col, pl.BlockSpec((1, 2, tk, LANE), lambda p, j: (p, 0, j, 0)),
                  pl.BlockSpec((1, 2, LANE, tk), lambda p, j: (p, 0, 0, j)), blk, rows16,
                  pl.BlockSpec((1, 8, s), lambda p, j: (p, 0, 0), pipeline_mode=pl.Buffered(1)), rows16],
        out_specs=(pl.BlockSpec((s, LANE), lambda p, j: (0, p)), blk, blk,
                   pl.BlockSpec((1, 8, s), lambda p, j: (p, 0, 0)), pl.BlockSpec((1, 8, tk), lambda p, j: (p, 0, j))),
        scratch_shapes=[pltpu.VMEM((2, s, LANE), BF16), pltpu.VMEM((2, s, LANE), BF16), pltpu.VMEM((2, 8, s), F32),
                        pltpu.VMEM((2, LANE, s), F32)],
        name="fox_attn_bwd", sem=("arbitrary", "arbitrary"), vmem_mb=ATT_BWD_VMEM_MB,
        args=(q, do, kaug, kaugt, v, cumt, lse, deltat))


def _cumsum_bwd(dcq, dck, flog, fb):
    s = flog.shape[0]
    ts = _tile(s, 256)
    nt = s // ts

    def body(dq_ref, dk_ref, f_ref, b_ref, df_ref, db_ref, carry):
        @pl.when(pl.program_id(0) == 0)
        def _():
            carry[...] = jnp.zeros_like(carry)
            db_ref[...] = jnp.zeros_like(db_ref)

        r = lax.broadcasted_iota(jnp.int32, (ts, ts), 0)
        cidx = lax.broadcasted_iota(jnp.int32, (ts, ts), 1)
        tri = (r >= cidx).astype(F32)
        dct = dq_ref[...] + dk_ref[...]
        dlst = jnp.dot(dct, tri, preferred_element_type=F32, precision=lax.Precision.HIGHEST) + carry[...]
        carry[...] = dlst[:, 0:1]
        dls = jnp.concatenate([dlst, jnp.zeros((LANE - N_HEADS, ts), F32)], axis=0).T
        z = f_ref[...] + b_ref[...]
        df = dls * (1.0 / (1.0 + jnp.exp(z)))
        db_ref[...] += _colsum(df)
        df_ref[...] = df.astype(BF16)

    rev = pl.BlockSpec((ts, LANE), lambda i: (nt - 1 - i, 0))
    revt = pl.BlockSpec((N_HEADS, ts), lambda i: (0, nt - 1 - i))
    vec = pl.BlockSpec((1, LANE), lambda i: (0, 0))
    return pl.pallas_call(
        body, grid=(nt,), out_shape=(_sds((s, LANE), BF16), _sds((1, LANE), F32)),
        in_specs=[revt, revt, rev, vec], out_specs=(rev, vec), scratch_shapes=[pltpu.VMEM((N_HEADS, 1), F32)],
        name="forget_cumsum_bwd", compiler_params=_cp(("arbitrary",)))(dcq, dck, flog, fb)


def _conv_bwd1(dx, gate, w_out, b_out, dwo, lng, lnb):
    s, d = dx.shape
    ts = _tile(s, 512)
    ns = s // ts

    def body(dx_ref, g_ref, w_ref, bo_ref, y_ref, lg_ref, lb_ref, dd_ref, dlg_ref, dlb_ref, dbd_ref, dbo_ref, dge_ref, cs):
        i = pl.program_id(0)

        @pl.when(i == 0)
        def _():
            for r in (dlg_ref, dlb_ref, dbd_ref, cs):
                r[...] = jnp.zeros_like(r)

        dxv = dx_ref[...]
        cs[...] += _colsum(dxv)
        dsw = _dot_nt((dxv * g_ref[...]).astype(BF16), w_ref[...])
        yv = y_ref[...]
        mu = jnp.mean(yv, axis=-1, keepdims=True)
        yc = yv - mu
        rstd = lax.rsqrt(jnp.mean(yc * yc, axis=-1, keepdims=True) + EPS)
        xhat = yc * rstd
        ln = xhat * lg_ref[...] + lb_ref[...]
        sg = _sigmoid(ln)
        dln = dsw * (sg * (1.0 + ln * (1.0 - sg)))
        dlg_ref[...] += _colsum(dln * xhat)
        dlb_ref[...] += _colsum(dln)
        dxh = dln * lg_ref[...]
        dd = rstd * (dxh - jnp.mean(dxh, axis=-1, keepdims=True) - xhat * jnp.mean(dxh * xhat, axis=-1, keepdims=True))
        dbd_ref[...] += _colsum(dd)
        dd_ref[...] = dd

        @pl.when(i == ns - 1)
        def _():
            dbo_ref[...] = g_ref[...] * cs[...]
            dge_ref[...] = bo_ref[...] * cs[...]

    vec = pl.BlockSpec((1, d), lambda i: (0, 0))
    row = pl.BlockSpec((ts, d), lambda i: (i, 0))
    return pl.pallas_call(
        body, grid=(ns,), out_shape=(_sds((s, d), F32),) + tuple(_sds((1, d), F32) for _ in range(5)),
        in_specs=[row, vec, _resident(w_out.shape), vec, row, vec, vec], out_specs=(row, vec, vec, vec, vec, vec),
        scratch_shapes=[pltpu.VMEM((1, d), F32)],
        name="conv_bwd_ln", compiler_params=_cp(("arbitrary",)))(dx, gate, w_out, b_out, dwo, lng, lnb)


def _dwconv_bwd(ddwo, glu, a_s, g_s, wdw, comm=None):
    s, d = ddwo.shape
    ts = _tile(s, 256)
    ns = s // ts
    rb, cb = 32, 256
    nrb = ts // rb

    def body(dd_ref, ddn_ref, gl_ref, glh_ref, a_ref, g_ref, w_ref, da_ref, dg_ref, dw_ref, sa_ref, sg_ref, bufd, bufg, dws):
        i = pl.program_id(0)

        @pl.when(i == 0)
        def _():
            dws[...] = jnp.zeros_like(dws)
            sa_ref[...] = jnp.zeros_like(sa_ref)
            sg_ref[...] = jnp.zeros_like(sg_ref)
            bufg[pl.ds(0, HALO), :] = jnp.zeros((HALO, d), F32)

        @pl.when(i > 0)
        def _():
            bufg[pl.ds(0, HALO), :] = glh_ref[...]

        bufg[pl.ds(HALO, ts), :] = gl_ref[...]
        bufd[pl.ds(0, ts), :] = dd_ref[...]

        @pl.when(i == ns - 1)
        def _():
            bufd[pl.ds(ts, HALO), :] = jnp.zeros((HALO, d), F32)

        @pl.when(i < ns - 1)
        def _():
            bufd[pl.ds(ts, HALO), :] = ddn_ref[...]

        for cc in range(d // cb):
            cs = pl.ds(cc * cb, cb)
            for r in range(nrb):
                acc = jnp.zeros((rb, cb), F32)
                for k in range(CONV_K):
                    acc = acc + w_ref[pl.ds(k, 1), cs] * bufd[pl.ds(r * rb + (CONV_K - 1) - k, rb), cs]
                rows = pl.ds(r * rb, rb)
                av = a_ref[rows, cs].astype(F32)
                sg = _sigmoid(g_ref[rows, cs].astype(F32))
                dav = acc * sg
                dgv = acc * av * sg * (1.0 - sg)
                da_ref[rows, cs] = dav.astype(BF16)
                dg_ref[rows, cs] = dgv.astype(BF16)
                sa_ref[:, cs] += _colsum(dav)
                sg_ref[:, cs] += _colsum(dgv)
            for k in range(CONV_K):
                acc8 = jnp.zeros((8, cb), F32)
                for r in range(nrb):
                    prod = bufd[pl.ds(r * rb, rb), cs] * bufg[pl.ds(HALO - (CONV_K - 1) + k + r * rb, rb), cs]
                    acc8 = acc8 + (prod[0:8] + prod[8:16]) + (prod[16:24] + prod[24:32])
                dws[pl.ds(8 * k, 8), cs] += acc8

        @pl.when(i == ns - 1)
        def _():
            dw_ref[...] = jnp.zeros_like(dw_ref)
            for k in range(CONV_K):
                dw_ref[pl.ds(k, 1), :] = _colsum(dws[pl.ds(8 * k, 8), :])

    row = pl.BlockSpec((ts, d), lambda i: (i, 0))
    vec = pl.BlockSpec((1, d), lambda i: (0, 0))
    hb = ts // HALO
    return _hosted_call(
        body, comm, grid=(ns,),
        out_shape=(_sds((s, d), BF16), _sds((s, d), BF16), _sds((HALO, d), F32), _sds((1, d), F32), _sds((1, d), F32)),
        in_specs=[row, pl.BlockSpec((HALO, d), lambda i: (jnp.minimum((i + 1) * hb, ns * hb - 1), 0)),
                  row, pl.BlockSpec((HALO, d), lambda i: (jnp.maximum(i * hb - 1, 0), 0)),
                  row, row, pl.BlockSpec((HALO, d), lambda i: (0, 0))],
        out_specs=(row, row, pl.BlockSpec((HALO, d), lambda i: (0, 0)), vec, vec),
        scratch_shapes=[pltpu.VMEM((ts + HALO, d), F32), pltpu.VMEM((HALO + ts, d), F32), pltpu.VMEM((8 * HALO, d), F32)],
        name="dwconv_bwd", sem=("arbitrary",), args=(ddwo, ddwo, glu, glu, a_s, g_s, wdw))


def _ada_wgrad(cat, da, name):
    nl, _, n = da.shape
    d = cat.shape[0]
    tn = 256

    def body(c_ref, d_ref, o_ref):
        acc = c_ref[:, 0:1] * d_ref[0, 0:1, :]
        for r in range(1, 8):
            acc = acc + c_ref[:, r:r + 1] * d_ref[0, r:r + 1, :]
        o_ref[0] = acc

    return pl.pallas_call(
        body, grid=(nl, n // tn), out_shape=_sds((nl, d, n), F32),
        in_specs=[pl.BlockSpec((d, 8), lambda l, j: (0, 0)), pl.BlockSpec((1, 8, tn), lambda l, j: (l, 0, j))],
        out_specs=pl.BlockSpec((1, d, tn), lambda l, j: (l, 0, j)),
        name=name, compiler_params=_cp(("parallel", "parallel")))(cat, da)


def _silu_rows(c_all):
    def body(c_ref, o_ref):
        cc = c_ref[...]
        o_ref[...] = cc * _sigmoid(cc)

    return pl.pallas_call(body, out_shape=_sds(c_all.shape, F32), name="silu_c")(c_all)


def _adamw(w, g, m, v, name):
    r, c = w.shape
    tr = r
    for cand in (512, 256, 128, 64, 32, 16, 8):
        if r % cand == 0 and cand * c * 4 <= (1 << 20):
            tr = cand
            break
    bc1 = 1.0 - ADAM_B1 ** ADAM_STEP
    bc2 = 1.0 - ADAM_B2 ** ADAM_STEP

    def body(w_ref, g_ref, m_ref, v_ref, d_ref, nm_ref, nv_ref):
        gv = g_ref[...]
        mn = ADAM_B1 * m_ref[...] + (1.0 - ADAM_B1) * gv
        vn = ADAM_B2 * v_ref[...] + (1.0 - ADAM_B2) * (gv * gv)
        mh = mn / bc1
        vh = vn / bc2
        d_ref[...] = -ADAM_LR * (mh / (jnp.sqrt(vh) + ADAM_EPS) + ADAM_WD * w_ref[...])
        nm_ref[...] = mn
        nv_ref[...] = vn

    blk = pl.BlockSpec((tr, c), lambda i: (i, 0))
    return pl.pallas_call(
        body, grid=(r // tr,), out_shape=tuple(_sds((r, c), F32) for _ in range(3)),
        in_specs=[blk, blk, blk, blk], out_specs=(blk, blk, blk),
        name=name, compiler_params=_cp(("parallel",)))(w, g, m, v)


def _halves(w2):
    r, c = w2.shape
    return w2.reshape(2, (r // 2) * c // 1024, 1024)


def _pad_rows(a, rows, axis):
    pad = [(0, 0)] * a.ndim
    pad[axis] = (0, rows - a.shape[axis])
    return jnp.pad(a, pad)


def _vec(a):
    return a.reshape(1, -1)


def kernel(x, c, mix_norm_g, mix_ada_w, mix_ada_b, ffn_norm_g, ffn_ada_w, ffn_ada_b, ffn_w_in, ffn_w_out, conv_w_in, conv_b_in, conv_w_dw, conv_b_dw, conv_ln_g, conv_ln_b, conv_w_out, conv_b_out, kv_norm_g, kv_ada_w, kv_ada_b, kv_w, forget_b, attn_w_q, attn_w_o, final_norm_g, loss_target, m_mix_norm_g, m_mix_ada_w, m_mix_ada_b, m_ffn_norm_g, m_ffn_ada_w, m_ffn_ada_b, m_ffn_w_in, m_ffn_w_out, m_conv_w_in, m_conv_b_in, m_conv_w_dw, m_conv_b_dw, m_conv_ln_g, m_conv_ln_b, m_conv_w_out, m_conv_b_out, m_kv_norm_g, m_kv_ada_w, m_kv_ada_b, m_kv_w, m_forget_b, m_attn_w_q, m_attn_w_o, m_final_norm_g, v_mix_norm_g, v_mix_ada_w, v_mix_ada_b, v_ffn_norm_g, v_ffn_ada_w, v_ffn_ada_b, v_ffn_w_in, v_ffn_w_out, v_conv_w_in, v_conv_b_in, v_conv_w_dw, v_conv_b_dw, v_conv_ln_g, v_conv_ln_b, v_conv_w_out, v_conv_b_out, v_kv_norm_g, v_kv_ada_w, v_kv_ada_b, v_kv_w, v_forget_b, v_attn_w_q, v_attn_w_o, v_final_norm_g):
    xi, yi, ci = lax.axis_index("x"), lax.axis_index("y"), lax.axis_index("c")
    chip = 2 * xi + yi
    dev = 4 * xi + 2 * yi + ci
    s, d = x.shape[1], x.shape[2]
    f = ffn_w_out.shape[1] * 4
    x0 = x[0]
    nkv = kv_w.shape[1]
    nkv_all = 4 * nkv

    wdw_loc = _pad_rows(conv_w_dw[0], HALO, 0)
    small = jnp.concatenate([c.reshape(-1), conv_b_in.reshape(-1), wdw_loc.reshape(-1), conv_b_dw.reshape(-1),
                             conv_ln_g.reshape(-1), conv_ln_b.reshape(-1), conv_b_out.reshape(-1)])
    n_small = small.shape[0]
    w_small = -(-n_small // (8 * LANE)) * LANE
    small = jnp.pad(small, (0, 8 * w_small - n_small)).reshape(8, w_small)
    small_all = _allgather8(small, "ag_small_params", True).reshape(8, 8 * w_small)
    c_all = small_all[:, :d]
    per_chip = small_all[0::2]
    dq_ = d // 4
    o1 = d
    b_in_full = per_chip[:, o1:o1 + 2 * dq_].reshape(4, 1, 2 * dq_)
    o1 += 2 * dq_
    wdw_full = per_chip[:, o1:o1 + HALO * dq_].reshape(4, HALO, dq_).transpose(1, 0, 2).reshape(HALO, d)
    o1 += HALO * dq_
    bdw_full = per_chip[:, o1:o1 + dq_].reshape(1, d)
    lng_full = per_chip[:, o1 + dq_:o1 + 2 * dq_].reshape(1, d)
    lnb_full = per_chip[:, o1 + 2 * dq_:o1 + 3 * dq_].reshape(1, d)
    bout_full = per_chip[:, o1 + 3 * dq_:o1 + 4 * dq_].reshape(1, d)

    a_mix = _ada_fwd(c_all, mix_ada_w, "ada_mix")
    a_ffn = _ada_fwd(c_all, ffn_ada_w, "ada_ffn")
    a_kv = _ada_fwd(c_all, kv_ada_w[None], "ada_kv")
    n3 = mix_ada_w.shape[2]
    n2 = kv_ada_w.shape[1]
    ada_loc = jnp.concatenate([a_mix[0], a_mix[1], a_ffn[0], a_ffn[1], a_kv[0]], axis=1)
    w_ada = ada_loc.shape[1]
    ada_all = _allgather8(ada_loc, "ag_ada", True).reshape(8, 8, w_ada)
    ada_me = lax.dynamic_index_in_dim(ada_all, dev, axis=1, keepdims=False)[0::2]

    def ada_vec(off, n, bias):
        return ada_me[:, off:off + n].reshape(1, 4 * n) + bias.reshape(1, -1)

    ada_m0 = ada_vec(0, n3, mix_ada_b[0])
    ada_m1 = ada_vec(n3, n3, mix_ada_b[1])
    ada_f0 = ada_vec(2 * n3, n3, ffn_ada_b[0])
    ada_f1 = ada_vec(3 * n3, n3, ffn_ada_b[1])
    ada_k = ada_vec(4 * n3, n2, kv_ada_b)

    def split3(a):
        return a[:, :d], a[:, d:2 * d], a[:, 2 * d:3 * d]

    sh_m0, sc_m0, gt_m0 = split3(ada_m0)
    sh_m1, sc_m1, gt_m1 = split3(ada_m1)
    sh_f0, sc_f0, gt_f0 = split3(ada_f0)
    sh_f1, sc_f1, gt_f1 = split3(ada_f1)
    sh_k, sc_k = ada_k[:, :d], ada_k[:, d:2 * d]

    def layout(ws):
        hr = [(w.shape[0] // 2) * w.shape[1] // 1024 for w in ws]
        pr = [-(-r // 16) * 16 for r in hr]
        return hr, pr, [sum(pr[:i]) for i in range(len(ws))], sum(pr)

    def my_pack(ws):
        _, pr, _, _ = layout(ws)
        pack = jnp.concatenate([_pad_rows(_halves(w.astype(BF16)), p_, 1) for w, p_ in zip(ws, pr)], axis=1)
        return lax.dynamic_index_in_dim(pack, ci, axis=0, keepdims=False)

    def unpack(gath, ws):
        hr, _, offs, rt = layout(ws)
        g4 = gath.reshape(4, 2, rt, 1024)
        return [g4[:, :, o:o + h, :].reshape(4, w.shape[0], w.shape[1]) for w, o, h in zip(ws, offs, hr)]

    grp_conv = [conv_w_in[0], conv_w_out[0]]
    grp_ffn0 = [ffn_w_in[0], ffn_w_out[0]]
    grp_rest = [ffn_w_in[1], ffn_w_out[1], kv_w, attn_w_q[0], attn_w_o[0]]
    cw_in, cw_out = unpack(_allgather8(my_pack(grp_conv), "ag_w_conv", False), grp_conv)
    cw_out = cw_out.reshape(d, d)
    pack_ffn0, pack_rest = my_pack(grp_ffn0), my_pack(grp_rest)

    zero_b = jnp.zeros((1, d), F32)
    g_m0, g_m1 = _vec(mix_norm_g[0]), _vec(mix_norm_g[1])
    g_f0, g_f1 = _vec(ffn_norm_g[0]), _vec(ffn_norm_g[1])
    g_k, g_fin = _vec(kv_norm_g), _vec(final_norm_g)
    fb = jnp.pad(forget_b, (0, LANE - N_HEADS)).reshape(1, LANE)

    h0, glu, a_s, g_s, gath_ffn0 = _in_pair(x0, g_m0, sh_m0, sc_m0, cw_in, b_in_full, True, "conv_in",
                                            comm=(_Gather8(pack_ffn0), pack_ffn0))
    dwo, sw, gath_rest = _dwconv_fwd(glu, wdw_full, bdw_full, lng_full, lnb_full, comm=(_Gather8(pack_rest), pack_rest))
    w_in0, w_out0 = unpack(gath_ffn0, grp_ffn0)
    w_in1, w_out1, kvw, wq, wo = unpack(gath_rest, grp_rest)
    w_in = [w_in0, w_in1]
    w_out = [w_out0.reshape(f, d), w_out1.reshape(f, d)]
    kvw = kvw.transpose(1, 0, 2).reshape(d, nkv_all)
    wk, wv = kvw[:, :d], kvw[:, d:2 * d]
    wf = jnp.pad(kvw[:, 2 * d:], ((0, 0), (0, LANE - N_HEADS)))
    wq, wo = wq.reshape(d, d), wo.reshape(d, d)
    x1 = _mm_res(sw, cw_out, bout_full, gt_m0, x0, "conv_out")
    hf0, act0, ug0, uu0 = _in_pair(x1, g_f0, sh_f0, sc_f0, w_in[0], None, False, "ffn0_in")
    x2 = _mm_res(act0, w_out[0], zero_b, gt_f0, x1, "ffn0_out")
    hk, h1, kk, vv, qq, flog = _qkv(x2, (g_k, sh_k, sc_k), (g_m1, sh_m1, sc_m1), wk, wv, wf, wq)
    cum, cumt = _cumsum_fwd(flog, fb)
    kaug, kaugt, vtr = _attn_prep(kk, vv, cum)
    o, lse = _attn_fwd(qq, kaug, vtr, cumt)
    x3 = _mm_res(o, wo, zero_b, gt_m1, x2, "attn_out")
    hf1, act1, ug1, uu1 = _in_pair(x3, g_f1, sh_f1, sc_f1, w_in[1], None, False, "ffn1_in")
    x4 = _mm_res(act1, w_out[1], zero_b, gt_f1, x3, "ffn1_out")
    lsum, dx4, d_gfin = _final(x4, g_fin, loss_target[0])
    loss = lax.psum(0.5 / d * jnp.sum(lsum), ("x", "y", "c"))

    nf = f // 2

    sel = jnp.stack([ci, chip]).astype(jnp.int32)

    def reduce_begin(gs, ws, tag):
        hr, pr, _, _ = layout(ws)
        gp = jnp.concatenate([_pad_rows(g.reshape(4, 2, h, 1024), p_, 2) for g, h, p_ in zip(gs, hr, pr)], axis=2)
        return _add_halves(gp, _swap_halves(gp, tag + "_swap"), sel, tag + "_add")

    def ffn_bwd(dx_out, x_in, hf, act, ug, uu, gain, scale, gate, w_in_l, w_out_l, tag, comm=None):
        res = _ffn_bwd_act(dx_out, gate, w_out_l, ug, uu, tag + "_bwd_act", comm=comm)
        dug, duu = res[0], res[1]
        dw_out, dgate = _dw_mm(act, [dx_out], nf, d, tag + "_dw_out", gate=gate, wfull=w_out_l, dgate_init=zero_b)
        terms = [(dug, 0, w_in_l, 0), (dug, nf, w_in_l, 1), (duu, 0, w_in_l, 2), (duu, nf, w_in_l, 3)]
        dx_in, dsh, dsc, dgn = _mm_normbwd(terms, x_in, dx_out, gain, scale, tag + "_bwd_in")
        dw_in = _dw_mm(hf, [dug, duu], d, nf, tag + "_dw_in")
        return dx_in, dw_in, dw_out[0], dsh, dsc, dgate, dgn, (res[2] if comm is not None else None)

    dx3, dw_in1, dw_out1, dsh_f1, dsc_f1, dgt_f1, dgn_f1, _ = ffn_bwd(dx4, x3, hf1, act1, ug1, uu1, g_f1, sc_f1, gt_f1, w_in[1], w_out[1], "ffn1")
    grp_ffn1 = [ffn_w_in[1], ffn_w_out[1]]
    q16_1, own_1 = reduce_begin([dw_in1, dw_out1.reshape(4, f // 4, d)], grp_ffn1, "rs_ffn1")

    do, deltat = _do_kernel(dx3, gt_m1, wo, o)
    dwo_att, dgt_m1 = _dw_mm(o, [dx3], d, d, "attn_dw_o", gate=gt_m1, wfull=wo, dgate_init=zero_b)
    dq, dk, dv, dcq, dck, land_1 = _attn_bwd(qq, do, kaug, kaugt, vv, cumt, lse, deltat, comm=(_ScatterChips(q16_1), q16_1))
    wq3 = wq.reshape(1, d, d)
    dx2a, dsh_m1, dsc_m1, dgn_m1 = _mm_normbwd([(dq, 0, wq3, 0)], x2, dx3, g_m1, sc_m1, "attn_bwd_q")
    dwq = _dw_mm(h1, [dq], d, d, "attn_dw_q")[0]

    df, dfb = _cumsum_bwd(dcq[:, :2].reshape(N_HEADS, s), dck[:, :2].reshape(N_HEADS, s), flog, fb)
    terms = [(dk, 0, wk.reshape(1, d, d), 0), (dv, 0, wv.reshape(1, d, d), 0), (df, 0, wf.reshape(1, d, LANE), 0)]
    dx2, dsh_k, dsc_k, dgn_k = _mm_normbwd(terms, x2, dx2a, g_k, sc_k, "kv_bwd")
    dwk = _dw_mm(hk, [dk], d, d, "kv_dw_k")[0]
    dwv = _dw_mm(hk, [dv], d, d, "kv_dw_v")[0]
    dwf = _dw_mm(hk, [df], d, LANE, "kv_dw_f")[0]
    dkvw = jnp.concatenate([dwk, dwv, dwf[:, :N_HEADS]], axis=1)
    dkvw = dkvw.reshape(d, 4, nkv).transpose(1, 0, 2)

    grp_attn = [kv_w, attn_w_q[0], attn_w_o[0]]
    q16_2, own_2 = reduce_begin([dkvw, dwq.reshape(4, d // 4, d), dwo_att[0].reshape(4, d // 4, d)], grp_attn, "rs_attn")
    dx1, dw_in0, dw_out0, dsh_f0, dsc_f0, dgt_f0, dgn_f0, land_2 = ffn_bwd(
        dx2, x1, hf0, act0, ug0, uu0, g_f0, sc_f0, gt_f0, w_in[0], w_out[0], "ffn0", comm=(_ScatterChips(q16_2), q16_2))
    q16_3, own_3 = reduce_begin([dw_in0, dw_out0.reshape(4, f // 4, d)], grp_ffn0, "rs_ffn0")

    ddwo, d_lng, d_lnb, d_bdw, d_bout, dgt_extra = _conv_bwd1(dx1, gt_m0, cw_out, bout_full, dwo, lng_full, lnb_full)
    dcw_out, dgt_m0 = _dw_mm(sw, [dx1], d, d, "conv_dw_out", gate=gt_m0, wfull=cw_out, dgate_init=dgt_extra)
    da, dg, d_wdw, d_bin_a, d_bin_g, land_3 = _dwconv_bwd(ddwo, glu, a_s, g_s, wdw_full, comm=(_ScatterChips(q16_3), q16_3))
    nc = cw_in.shape[2]
    terms = [(da, 0, cw_in, 0), (da, nc, cw_in, 1), (dg, 0, cw_in, 2), (dg, nc, cw_in, 3)]
    dx0, dsh_m0, dsc_m0, dgn_m0 = _mm_normbwd(terms, x0, dx1, g_m0, sc_m0, "conv_bwd_in")
    dcw_in = _dw_mm(h0, [da, dg], d, nc, "conv_dw_in")
    q16_4, own_4 = reduce_begin([dcw_in, dcw_out[0].reshape(4, d // 4, d)], grp_conv, "rs_conv")
    land_4 = _scatter_chips(q16_4, "rs_conv_scatter")

    reduced = _share_halves([_add_chips(own_1, land_1, sel, "rs_ffn1_sum"), _add_chips(own_2, land_2, sel, "rs_attn_sum"),
                             _add_chips(own_3, land_3, sel, "rs_ffn0_sum"), _add_chips(own_4, land_4, sel, "rs_conv_sum")])

    def shard_grads(buf, ws):
        hr, _, offs, _ = layout(ws)
        return [buf[:, o:o + h, :].reshape(w.shape) for w, o, h in zip(ws, offs, hr)]

    g_w_in1, g_w_out1 = shard_grads(reduced[0], grp_ffn1)
    g_kvw, g_wq, g_wo = shard_grads(reduced[1], grp_attn)
    g_w_in0, g_w_out0 = shard_grads(reduced[2], grp_ffn0)
    g_cw_in, g_cw_out = shard_grads(reduced[3], grp_conv)

    d_ada = [jnp.concatenate([dsh_m0, dsc_m0, dgt_m0], axis=1), jnp.concatenate([dsh_m1, dsc_m1, dgt_m1], axis=1),
             jnp.concatenate([dsh_f0, dsc_f0, dgt_f0], axis=1), jnp.concatenate([dsh_f1, dsc_f1, dgt_f1], axis=1),
             jnp.concatenate([dsh_k, dsc_k], axis=1)]
    fields = d_ada + [dgn_m0, dgn_m1, dgn_f0, dgn_f1, dgn_k, d_gfin, d_bin_a, d_bin_g, d_bdw, d_lng, d_lnb, d_bout,
                      d_wdw.reshape(1, -1), dfb]
    foffs = [0]
    for fl in fields:
        foffs.append(foffs[-1] + fl.shape[1])
    n_row = foffs[-1]
    w_row = -(-n_row // (8 * LANE)) * LANE
    row = jnp.pad(jnp.concatenate(fields, axis=1), ((0, 0), (0, 8 * w_row - n_row))).reshape(8, w_row)
    rows_all = _allgather8(row, "ag_small_grads", True).reshape(8, 8, w_row)
    rsum_small = _sum8(rows_all).reshape(1, 8 * w_row)
    rows_flat = rows_all.reshape(8, 8 * w_row)

    def fsum(i):
        return rsum_small[:, foffs[i]:foffs[i + 1]]

    cat = _silu_rows(c_all).T

    def ada_cols(i, n):
        full = rows_flat[:, foffs[i]:foffs[i + 1]].reshape(8, 4, n)
        return lax.dynamic_index_in_dim(full, chip, axis=1, keepdims=False)

    g_mix_ada_w = _ada_wgrad(cat, jnp.stack([ada_cols(0, n3), ada_cols(1, n3)]), "ada_mix_wgrad")
    g_ffn_ada_w = _ada_wgrad(cat, jnp.stack([ada_cols(2, n3), ada_cols(3, n3)]), "ada_ffn_wgrad")
    g_kv_ada_w = _ada_wgrad(cat, ada_cols(4, n2)[None], "ada_kv_wgrad")[0]

    def my_cols(v, n):
        return lax.dynamic_index_in_dim(v.reshape(4, n), chip, axis=0, keepdims=False)

    grads = {
        "mix_norm_g": jnp.concatenate([fsum(5), fsum(6)], axis=0),
        "mix_ada_w": g_mix_ada_w,
        "mix_ada_b": jnp.concatenate([fsum(0), fsum(1)], axis=0),
        "ffn_norm_g": jnp.concatenate([fsum(7), fsum(8)], axis=0),
        "ffn_ada_w": g_ffn_ada_w,
        "ffn_ada_b": jnp.concatenate([fsum(2), fsum(3)], axis=0),
        "ffn_w_in": jnp.stack([g_w_in0, g_w_in1]),
        "ffn_w_out": jnp.stack([g_w_out0, g_w_out1]),
        "conv_w_in": g_cw_in[None],
        "conv_b_in": my_cols(jnp.concatenate([fsum(11), fsum(12)], axis=1), 2 * dq_)[None],
        "conv_w_dw": lax.dynamic_index_in_dim(fsum(17).reshape(HALO, 4, dq_), chip, axis=1, keepdims=False)[:CONV_K][None],
        "conv_b_dw": my_cols(fsum(13), dq_)[None],
        "conv_ln_g": my_cols(fsum(14), dq_)[None],
        "conv_ln_b": my_cols(fsum(15), dq_)[None],
        "conv_w_out": g_cw_out[None],
        "conv_b_out": my_cols(fsum(16), dq_)[None],
        "kv_norm_g": fsum(9).reshape(-1),
        "kv_ada_w": g_kv_ada_w,
        "kv_ada_b": fsum(4).reshape(-1),
        "kv_w": g_kvw,
        "forget_b": fsum(18).reshape(-1)[:N_HEADS],
        "attn_w_q": g_wq[None],
        "attn_w_o": g_wo[None],
        "final_norm_g": fsum(10).reshape(-1),
    }
    weights = dict(mix_norm_g=mix_norm_g, mix_ada_w=mix_ada_w, mix_ada_b=mix_ada_b, ffn_norm_g=ffn_norm_g, ffn_ada_w=ffn_ada_w, ffn_ada_b=ffn_ada_b, ffn_w_in=ffn_w_in, ffn_w_out=ffn_w_out, conv_w_in=conv_w_in, conv_b_in=conv_b_in, conv_w_dw=conv_w_dw, conv_b_dw=conv_b_dw, conv_ln_g=conv_ln_g, conv_ln_b=conv_ln_b, conv_w_out=conv_w_out, conv_b_out=conv_b_out, kv_norm_g=kv_norm_g, kv_ada_w=kv_ada_w, kv_ada_b=kv_ada_b, kv_w=kv_w, forget_b=forget_b, attn_w_q=attn_w_q, attn_w_o=attn_w_o, final_norm_g=final_norm_g)
    moms = dict(mix_norm_g=(m_mix_norm_g, v_mix_norm_g), mix_ada_w=(m_mix_ada_w, v_mix_ada_w), mix_ada_b=(m_mix_ada_b, v_mix_ada_b), ffn_norm_g=(m_ffn_norm_g, v_ffn_norm_g), ffn_ada_w=(m_ffn_ada_w, v_ffn_ada_w), ffn_ada_b=(m_ffn_ada_b, v_ffn_ada_b), ffn_w_in=(m_ffn_w_in, v_ffn_w_in), ffn_w_out=(m_ffn_w_out, v_ffn_w_out), conv_w_in=(m_conv_w_in, v_conv_w_in), conv_b_in=(m_conv_b_in, v_conv_b_in), conv_w_dw=(m_conv_w_dw, v_conv_w_dw), conv_b_dw=(m_conv_b_dw, v_conv_b_dw), conv_ln_g=(m_conv_ln_g, v_conv_ln_g), conv_ln_b=(m_conv_ln_b, v_conv_ln_b), conv_w_out=(m_conv_w_out, v_conv_w_out), conv_b_out=(m_conv_b_out, v_conv_b_out), kv_norm_g=(m_kv_norm_g, v_kv_norm_g), kv_ada_w=(m_kv_ada_w, v_kv_ada_w), kv_ada_b=(m_kv_ada_b, v_kv_ada_b), kv_w=(m_kv_w, v_kv_w), forget_b=(m_forget_b, v_forget_b), attn_w_q=(m_attn_w_q, v_attn_w_q), attn_w_o=(m_attn_w_o, v_attn_w_o), final_norm_g=(m_final_norm_g, v_final_norm_g))
    names = list(weights)

    deltas, new_m, new_v = {}, {}, {}
    small_names = [n for n in names if weights[n].size < (1 << 16)]
    for n in names:
        if n in small_names:
            continue
        w = weights[n]
        w2 = w.reshape(-1, w.shape[-1])
        dl, nm, nv = _adamw(w2, grads[n].reshape(w2.shape), moms[n][0].reshape(w2.shape), moms[n][1].reshape(w2.shape), "adamw_" + n)
        deltas[n], new_m[n], new_v[n] = dl.reshape(w.shape), nm.reshape(w.shape), nv.reshape(w.shape)

    def pack_small(get):
        flat = jnp.concatenate([get(n).reshape(-1) for n in small_names])
        rows_ = -(-flat.shape[0] // (8 * LANE)) * 8
        return jnp.pad(flat, (0, rows_ * LANE - flat.shape[0])).reshape(rows_, LANE)

    ws, gs = pack_small(lambda n: weights[n]), pack_small(lambda n: grads[n])
    ms_, vs_ = pack_small(lambda n: moms[n][0]), pack_small(lambda n: moms[n][1])
    vs_ = jnp.where(jnp.arange(vs_.size).reshape(vs_.shape) < sum(weights[n].size for n in small_names), vs_, 1.0)
    dl, nm, nv = _adamw(ws, gs, ms_, vs_, "adamw_small")
    off = 0
    for n in small_names:
        sz = weights[n].size
        shp = weights[n].shape
        deltas[n] = dl.reshape(-1)[off:off + sz].reshape(shp)
        new_m[n] = nm.reshape(-1)[off:off + sz].reshape(shp)
        new_v[n] = nv.reshape(-1)[off:off + sz].reshape(shp)
        off += sz

    grad_out = [grads[n].reshape(weights[n].shape) for n in names]
    return (loss, dx0[None], *grad_out, *[deltas[n] for n in names], *[new_m[n] for n in names], *[new_v[n] for n in names])
```

```python
import functools

import jax
import jax.numpy as jnp
from jax import lax
from jax.experimental import pallas as pl
from jax.experimental.pallas import tpu as pltpu

F32 = jnp.float32
BF16 = jnp.bfloat16
MESH = pl.DeviceIdType.MESH

EPS = 1e-6
N_HEADS = 16
HEAD_DIM = 64
CONV_K = 31
LANE = 128
SUBLANES = 8
HALO = 32
ATT_TQ = 1024
ATT_TK = 512
NPIECE = 3
SPARE = (HEAD_DIM, 0)
VMEM_MB = 48
ATT_BWD_VMEM_MB = 56

ADAM_LR = 0.001
ADAM_B1 = 0.9
ADAM_B2 = 0.999
ADAM_EPS = 1e-08
ADAM_WD = 0.01
ADAM_STEP = 10


def _sds(shape, dtype):
    return jax.ShapeDtypeStruct(tuple(shape), dtype)


def _cp(sem=None, vmem_mb=VMEM_MB):
    return pltpu.CompilerParams(dimension_semantics=sem, vmem_limit_bytes=vmem_mb << 20)


def _tile(n, pref):
    return pref if n % pref == 0 else n


def _row_tile(r, mult):
    for cand in range(512 // mult * mult, mult - 1, -mult):
        if r % cand == 0:
            return cand
    return r


def _resident(shape):
    nd = len(shape)
    return pl.BlockSpec(tuple(shape), lambda *_: (0,) * nd, pipeline_mode=pl.Buffered(1))


def _dot(a, b):
    return jnp.dot(a, b, preferred_element_type=F32)


def _dot_nt(a, b):
    return lax.dot_general(a, b, (((1,), (1,)), ((), ())), preferred_element_type=F32)


def _dot_tn(a, b):
    return lax.dot_general(a, b, (((0,), (0,)), ((), ())), preferred_element_type=F32)


def _sigmoid(x):
    return 1.0 / (1.0 + jnp.exp(-x))


def _colsum(x):
    return jnp.sum(x, axis=0, keepdims=True)


def _rms_parts(x):
    rstd = lax.rsqrt(jnp.mean(x * x, axis=-1, keepdims=True) + EPS)
    return x * rstd, rstd


class _Gather8:
    def __init__(self, x):
        self.m_per, n = x.shape
        self.land = _sds((8 * self.m_per, n), x.dtype)
        self.sems = [pltpu.SemaphoreType.DMA((7,)), pltpu.SemaphoreType.DMA((7,)), pltpu.SemaphoreType.DMA]

    def _parts(self, x_ref, out_ref, send_sems, recv_sems, local_sem):
        x, y, c = lax.axis_index("x"), lax.axis_index("y"), lax.axis_index("c")
        me, sibling = (x, y, c), (x, y, 1 - c)
        chips = [(1 - x, y), (x, 1 - y), (1 - x, 1 - y)]
        m_per = self.m_per

        def rows(px, py, pc):
            return out_ref.at[pl.ds((4 * px + 2 * py + pc) * m_per, m_per), :]

        def copy(k, block, to, src=None):
            return pltpu.make_async_remote_copy(
                src_ref=rows(*block) if src is None else src, dst_ref=rows(*block),
                send_sem=send_sems.at[k], recv_sem=recv_sems.at[k], device_id=to, device_id_type=MESH)

        mine = pltpu.make_async_copy(x_ref, rows(*me), local_sem)
        first = [copy(0, me, sibling, src=x_ref)]
        first += [copy(1 + j, me, (*chip, c), src=x_ref) for j, chip in enumerate(chips)]
        passed = [copy(4 + j, (*chip, c), sibling) for j, chip in enumerate(chips)]
        return c, me, sibling, chips, copy, mine, first, passed

    def start(self, x_ref, out_ref, send_sems, recv_sems, local_sem):
        _, _, _, _, _, mine, first, _ = self._parts(x_ref, out_ref, send_sems, recv_sems, local_sem)
        mine.start()
        for cp in first:
            cp.start()

    def finish(self, x_ref, out_ref, send_sems, recv_sems, local_sem):
        c, me, sibling, chips, copy, mine, first, passed = self._parts(x_ref, out_ref, send_sems, recv_sems, local_sem)
        for j, chip in enumerate(chips):
            copy(1 + j, (*chip, c), me).wait_recv()
            passed[j].start()
        copy(0, sibling, me).wait_recv()
        for j, chip in enumerate(chips):
            copy(4 + j, (*chip, 1 - c), me).wait_recv()
        for cp in first + passed:
            cp.wait_send()
        mine.wait()


class _ScatterChips:
    def __init__(self, q):
        _, r, w = q.shape
        self.land = _sds((3, r, w), q.dtype)
        self.sems = [pltpu.SemaphoreType.DMA((3,)), pltpu.SemaphoreType.DMA((3,))]

    def _copies(self, q_ref, land_ref, send_sems, recv_sems):
        x, y, c = lax.axis_index("x"), lax.axis_index("y"), lax.axis_index("c")
        chips = [(1 - x, y), (x, 1 - y), (1 - x, 1 - y)]
        return [pltpu.make_async_remote_copy(
            src_ref=q_ref.at[2 * cx + cy], dst_ref=land_ref.at[k], send_sem=send_sems.at[k], recv_sem=recv_sems.at[k],
            device_id=(cx, cy, c), device_id_type=MESH) for k, (cx, cy) in enumerate(chips)]

    def start(self, *refs):
        for cp in self._copies(*refs):
            cp.start()

    def finish(self, *refs):
        copies = self._copies(*refs)
        for cp in copies:
            cp.wait_recv()
        for cp in copies:
            cp.wait_send()


def _hosted_call(body, comm, *, grid, in_specs, out_specs, out_shape, scratch_shapes, name, sem, args, vmem_mb=VMEM_MB):
    def first():
        return functools.reduce(jnp.logical_and, [pl.program_id(a) == 0 for a in range(len(grid))])

    def last():
        return functools.reduce(jnp.logical_and, [pl.program_id(a) == g - 1 for a, g in enumerate(grid)])

    out_specs = tuple(out_specs) if isinstance(out_specs, (tuple, list)) else (out_specs,)
    out_shape = tuple(out_shape) if isinstance(out_shape, (tuple, list)) else (out_shape,)
    if comm is None:
        return pl.pallas_call(body, grid=grid, in_specs=list(in_specs), out_specs=out_specs, out_shape=out_shape,
                              scratch_shapes=list(scratch_shapes), name=name, compiler_params=_cp(sem, vmem_mb))(*args)
    ex, src = comm
    n_in, n_out, n_scr = len(in_specs), len(out_shape), len(scratch_shapes)

    def wrapped(*refs):
        ins, src_ref = refs[:n_in], refs[n_in]
        outs, land_ref = refs[n_in + 1:n_in + 1 + n_out], refs[n_in + 1 + n_out]
        scr = refs[n_in + 2 + n_out:n_in + 2 + n_out + n_scr]
        sems = refs[n_in + 2 + n_out + n_scr:]

        @pl.when(first())
        def _():
            ex.start(src_ref, land_ref, *sems)

        body(*ins, *outs, *scr)

        @pl.when(last())
        def _():
            ex.finish(src_ref, land_ref, *sems)

    hbm = pl.BlockSpec(memory_space=pl.ANY)
    return pl.pallas_call(
        wrapped, grid=grid, in_specs=[*in_specs, hbm], out_specs=(*out_specs, hbm), out_shape=(*out_shape, ex.land),
        scratch_shapes=[*scratch_shapes, *ex.sems], name=name,
        compiler_params=_cp(tuple("arbitrary" for _ in grid), vmem_mb))(*args, src)


def _allgather8(x_shard, name, in_vmem):
    ex = _Gather8(x_shard)

    def body(*refs):
        ex.start(*refs)
        ex.finish(*refs)

    space = pltpu.VMEM if in_vmem else pl.ANY
    return pl.pallas_call(
        body, out_shape=ex.land, in_specs=[pl.BlockSpec(memory_space=space)], out_specs=pl.BlockSpec(memory_space=space),
        scratch_shapes=ex.sems, name=name)(x_shard)


def _swap_halves(p, name):
    nb, _, r, w = p.shape

    def body(p_ref, land_ref, send_sems, recv_sems):
        x, y, c = lax.axis_index("x"), lax.axis_index("y"), lax.axis_index("c")
        copies = [pltpu.make_async_remote_copy(
            src_ref=p_ref.at[j, 1 - c], dst_ref=land_ref.at[j], send_sem=send_sems.at[j], recv_sem=recv_sems.at[j],
            device_id=(x, y, 1 - c), device_id_type=MESH) for j in range(nb)]
        for cp in copies:
            cp.start()
        for cp in copies:
            cp.wait_recv()
        for cp in copies:
            cp.wait_send()

    return pl.pallas_call(
        body, out_shape=_sds((nb, r, w), p.dtype),
        in_specs=[pl.BlockSpec(memory_space=pl.ANY)], out_specs=pl.BlockSpec(memory_space=pl.ANY),
        scratch_shapes=[pltpu.SemaphoreType.DMA((nb,)), pltpu.SemaphoreType.DMA((nb,))],
        name=name)(p)


def _scatter_chips(q, name):
    ex = _ScatterChips(q)

    def body(*refs):
        ex.start(*refs)
        ex.finish(*refs)

    return pl.pallas_call(
        body, out_shape=ex.land, in_specs=[pl.BlockSpec(memory_space=pl.ANY)], out_specs=pl.BlockSpec(memory_space=pl.ANY),
        scratch_shapes=ex.sems, name=name)(q)


def _share_halves(bufs):
    n = len(bufs)

    def body(*refs):
        b_refs, out_refs, send_sems, recv_sems = refs[:n], refs[n:2 * n], refs[2 * n], refs[2 * n + 1]
        x, y, c = lax.axis_index("x"), lax.axis_index("y"), lax.axis_index("c")
        copies = [pltpu.make_async_remote_copy(
            src_ref=b_refs[k].at[c], dst_ref=out_refs[k].at[c], send_sem=send_sems.at[k], recv_sem=recv_sems.at[k],
            device_id=(x, y, 1 - c), device_id_type=MESH) for k in range(n)]
        for cp in copies:
            cp.start()
        for cp in copies:
            cp.wait_recv()
        for cp in copies:
            cp.wait_send()

    hbm = pl.BlockSpec(memory_space=pl.ANY)
    return pl.pallas_call(
        body, out_shape=tuple(_sds(b.shape, b.dtype) for b in bufs), in_specs=[hbm] * n, out_specs=tuple([hbm] * n),
        scratch_shapes=[pltpu.SemaphoreType.DMA((n,)), pltpu.SemaphoreType.DMA((n,))],
        input_output_aliases={k: k for k in range(n)}, name="rs_share_halves")(*bufs)


def _add_halves(p, land, sel, name):
    nb, _, r, w = p.shape
    tr = _row_tile(r, 16)

    def body(sel_ref, p_ref, l_ref, q16_ref, own_ref):
        q = p_ref[0, 0] + l_ref[0]
        q16_ref[0] = q.astype(BF16)

        @pl.when(pl.program_id(1) == sel_ref[1])
        def _():
            own_ref[...] = q

    gs = pltpu.PrefetchScalarGridSpec(
        num_scalar_prefetch=1, grid=(r // tr, nb),
        in_specs=[pl.BlockSpec((1, 1, tr, w), lambda i, j, sl: (j, sl[0], i, 0)),
                  pl.BlockSpec((1, tr, w), lambda i, j, sl: (j, i, 0))],
        out_specs=(pl.BlockSpec((1, tr, w), lambda i, j, sl: (j, i, 0)), pl.BlockSpec((tr, w), lambda i, j, sl: (i, 0))))
    return pl.pallas_call(body, grid_spec=gs, out_shape=(_sds((nb, r, w), BF16), _sds((r, w), F32)), name=name,
                          compiler_params=_cp(("parallel", "arbitrary")))(sel, p, land)


def _add_chips(own, land, sel, name):
    r, w = own.shape
    tr = _row_tile(r, 16)

    def body(sel_ref, q_ref, l_ref, o_ref):
        o_ref[0] = ((q_ref[...] + l_ref[0].astype(F32)) + l_ref[1].astype(F32)) + l_ref[2].astype(F32)

    gs = pltpu.PrefetchScalarGridSpec(
        num_scalar_prefetch=1, grid=(r // tr,),
        in_specs=[pl.BlockSpec((tr, w), lambda i, sl: (i, 0)), pl.BlockSpec((3, tr, w), lambda i, sl: (0, i, 0))],
        out_specs=pl.BlockSpec((1, tr, w), lambda i, sl: (sl[0], i, 0)))
    return pl.pallas_call(body, grid_spec=gs, out_shape=_sds((2, r, w), F32), name=name,
                          compiler_params=_cp(("parallel",)))(sel, own, land)


def _sum8(g):
    _, m, n = g.shape

    def body(g_ref, o_ref):
        acc = g_ref[0]
        for k in range(1, 8):
            acc = acc + g_ref[k]
        o_ref[...] = acc

    return pl.pallas_call(body, out_shape=_sds((m, n), g.dtype), name="sum8")(g)


def _ada_fwd(c_all, w3, name):
    nl, d, n = w3.shape
    tn = 256

    def body(c_ref, w_ref, o_ref):
        cc = c_ref[...]
        ca = (cc * _sigmoid(cc)).astype(BF16)
        o_ref[0] = _dot(ca, w_ref[0].astype(BF16))

    return pl.pallas_call(
        body, grid=(nl, n // tn), out_shape=_sds((nl, 8, n), F32),
        in_specs=[pl.BlockSpec((8, d), lambda l, j: (0, 0)), pl.BlockSpec((1, d, tn), lambda l, j: (l, 0, j))],
        out_specs=pl.BlockSpec((1, 8, tn), lambda l, j: (l, 0, j)),
        name=name, compiler_params=_cp(("parallel", "parallel")))(c_all, w3)


def _in_pair(x, gain, shift, scale, wg, bias, conv, name, comm=None):
    s, d = x.shape
    n = wg.shape[2]
    ts = _tile(s, 512)

    def body(*refs):
        if conv:
            x_ref, g_ref, sh_ref, sc_ref, wa_ref, wb_ref, ba_ref, bb_ref, h_ref, o_ref, sa_ref, sb_ref, hs = refs
        else:
            x_ref, g_ref, sh_ref, sc_ref, wa_ref, wb_ref, h_ref, o_ref, sa_ref, sb_ref, hs = refs

        @pl.when(pl.program_id(1) == 0)
        def _():
            xhat, _ = _rms_parts(x_ref[...])
            h = (xhat * g_ref[...]) * (1.0 + sc_ref[...]) + sh_ref[...]
            hs[...] = h.astype(BF16)
            h_ref[...] = hs[...]

        h = hs[...]
        a = _dot(h, wa_ref[0])
        b = _dot(h, wb_ref[0])
        if conv:
            a = a + ba_ref[0]
            b = b + bb_ref[0]
            o_ref[...] = a * _sigmoid(b)
        else:
            o_ref[...] = (a * _sigmoid(a) * b).astype(BF16)
        sa_ref[...] = a.astype(BF16)
        sb_ref[...] = b.astype(BF16)

    vec = pl.BlockSpec((1, d), lambda i, q: (0, 0))
    in_specs = [pl.BlockSpec((ts, d), lambda i, q: (i, 0)), vec, vec, vec,
                pl.BlockSpec((1, d, n), lambda i, q: (q, 0, 0)), pl.BlockSpec((1, d, n), lambda i, q: (q + 2, 0, 0))]
    args = [x, gain, shift, scale, wg, wg]
    if conv:
        in_specs += [pl.BlockSpec((1, 1, n), lambda i, q: (q, 0, 0)), pl.BlockSpec((1, 1, n), lambda i, q: (q + 2, 0, 0))]
        args += [bias, bias]
    tile = pl.BlockSpec((ts, n), lambda i, q: (i, q))
    return _hosted_call(
        body, comm, grid=(s // ts, 2),
        out_shape=(_sds((s, d), BF16), _sds((s, 2 * n), F32 if conv else BF16), _sds((s, 2 * n), BF16), _sds((s, 2 * n), BF16)),
        in_specs=in_specs, out_specs=(pl.BlockSpec((ts, d), lambda i, q: (i, 0)), tile, tile, tile),
        scratch_shapes=[pltpu.VMEM((ts, d), BF16)], name=name, sem=("parallel", "arbitrary"), args=args)


def _shift_copies(buf, shf):
    n = shf.shape[1]
    for r in range(1, SUBLANES):
        shf[r - 1, :, :] = buf[pl.ds(r, n), :]


def _shifted(buf, shf, start, n, cs):
    a, r = divmod(start, SUBLANES)
    if r == 0:
        return buf[pl.ds(start, n), cs]
    return shf[r - 1, pl.ds(a * SUBLANES, n), cs]


def _dwconv_fwd(glu, wdw, bdw, lng, lnb, comm=None):
    s, d = glu.shape
    ts = _tile(s, 256)
    rb, cb = 32, 256

    def body(cur_ref, halo_ref, w_ref, b_ref, g_ref, be_ref, dwo_ref, sw_ref, buf, shf):
        i = pl.program_id(0)

        @pl.when(i == 0)
        def _():
            buf[pl.ds(0, HALO), :] = jnp.zeros((HALO, d), F32)

        @pl.when(i > 0)
        def _():
            buf[pl.ds(0, HALO), :] = halo_ref[...]

        buf[pl.ds(HALO, ts), :] = cur_ref[...]
        _shift_copies(buf, shf)
        for r in range(ts // rb):
            for cc in range(d // cb):
                cs = pl.ds(cc * cb, cb)
                acc = jnp.zeros((rb, cb), F32) + b_ref[:, cs]
                for k in range(CONV_K):
                    acc = acc + w_ref[pl.ds(k, 1), cs] * _shifted(buf, shf, HALO - (CONV_K - 1) + k + r * rb, rb, cs)
                dwo_ref[pl.ds(r * rb, rb), cs] = acc
            rows = pl.ds(r * rb, rb)
            yv = dwo_ref[rows, :]
            mu = jnp.mean(yv, axis=-1, keepdims=True)
            yc = yv - mu
            var = jnp.mean(yc * yc, axis=-1, keepdims=True)
            ln = yc * lax.rsqrt(var + EPS) * g_ref[...] + be_ref[...]
            sw_ref[rows, :] = (ln * _sigmoid(ln)).astype(BF16)

    vec = pl.BlockSpec((1, d), lambda i: (0, 0))
    return _hosted_call(
        body, comm, grid=(s // ts,), out_shape=(_sds((s, d), F32), _sds((s, d), BF16)),
        in_specs=[pl.BlockSpec((ts, d), lambda i: (i, 0)),
                  pl.BlockSpec((HALO, d), lambda i: (jnp.maximum(i * (ts // HALO) - 1, 0), 0)),
                  pl.BlockSpec((HALO, d), lambda i: (0, 0)), vec, vec, vec],
        out_specs=(pl.BlockSpec((ts, d), lambda i: (i, 0)), pl.BlockSpec((ts, d), lambda i: (i, 0))),
        scratch_shapes=[pltpu.VMEM((HALO + ts, d), F32), pltpu.VMEM((SUBLANES - 1, HALO + ts - SUBLANES, d), F32)],
        name="dwconv_fwd", sem=("parallel",),
        args=(glu, glu, wdw, bdw, lng, lnb))


def _mm_res(a, w, b, gate, x, name):
    s, k = a.shape
    d = w.shape[1]
    ts = _tile(s, 512)

    def body(a_ref, w_ref, b_ref, g_ref, x_ref, o_ref):
        yv = _dot(a_ref[...], w_ref[...]) + b_ref[...]
        o_ref[...] = x_ref[...] + g_ref[...] * yv

    vec = pl.BlockSpec((1, d), lambda i: (0, 0))
    return pl.pallas_call(
        body, grid=(s // ts,), out_shape=_sds((s, d), F32),
        in_specs=[pl.BlockSpec((ts, k), lambda i: (i, 0)), _resident((k, d)), vec, vec, pl.BlockSpec((ts, d), lambda i: (i, 0))],
        out_specs=pl.BlockSpec((ts, d), lambda i: (i, 0)),
        name=name, compiler_params=_cp(("parallel",)))(a, w, b, gate, x)


def _qkv(x, kvp, mxp, wk, wv, wf, wq):
    s, d = x.shape
    ts = _tile(s, 512)
    qscale = HEAD_DIM ** -0.5

    def body(x_ref, gk, shk, sck, gm, shm, scm, wk_ref, wv_ref, wf_ref, wq_ref, hk_ref, h1_ref, k_ref, v_ref, q_ref, f_ref):
        xhat, _ = _rms_parts(x_ref[...])
        hk = ((xhat * gk[...]) * (1.0 + sck[...]) + shk[...]).astype(BF16)
        h1 = ((xhat * gm[...]) * (1.0 + scm[...]) + shm[...]).astype(BF16)
        hk_ref[...] = hk
        h1_ref[...] = h1
        k_ref[...] = _dot(hk, wk_ref[...]).astype(BF16)
        v_ref[...] = _dot(hk, wv_ref[...]).astype(BF16)
        f_ref[...] = _dot(hk, wf_ref[...])
        q_ref[...] = (_dot(h1, wq_ref[...]) * qscale).astype(BF16)

    vec = pl.BlockSpec((1, d), lambda i: (0, 0))
    row = pl.BlockSpec((ts, d), lambda i: (i, 0))
    return pl.pallas_call(
        body, grid=(s // ts,),
        out_shape=tuple(_sds((s, d), BF16) for _ in range(5)) + (_sds((s, LANE), F32),),
        in_specs=[row, vec, vec, vec, vec, vec, vec, _resident((d, d)), _resident((d, d)), _resident((d, LANE)), _resident((d, d))],
        out_specs=(row, row, row, row, row, pl.BlockSpec((ts, LANE), lambda i: (i, 0))),
        name="qkv_proj", compiler_params=_cp(("parallel",)))(x, *kvp, *mxp, wk, wv, wf, wq)


def _log_sigmoid(z):
    return jnp.minimum(z, 0.0) - jnp.log(1.0 + jnp.exp(-jnp.abs(z)))


def _cumsum_fwd(flog, fb):
    s = flog.shape[0]
    ts = _tile(s, 256)

    def body(f_ref, b_ref, cum_ref, cumt_ref, carry):
        @pl.when(pl.program_id(0) == 0)
        def _():
            carry[...] = jnp.zeros_like(carry)

        ls = _log_sigmoid(f_ref[...] + b_ref[...])
        r = lax.broadcasted_iota(jnp.int32, (ts, ts), 0)
        cidx = lax.broadcasted_iota(jnp.int32, (ts, ts), 1)
        tri = (cidx <= r).astype(F32)
        cs = jnp.dot(tri, ls, preferred_element_type=F32, precision=lax.Precision.HIGHEST) + carry[...]
        cum_ref[...] = cs
        cumt_ref[...] = cs.T
        carry[...] = cs[ts - 1:ts, :]

    return pl.pallas_call(
        body, grid=(s // ts,), out_shape=(_sds((s, LANE), F32), _sds((LANE, s), F32)),
        in_specs=[pl.BlockSpec((ts, LANE), lambda i: (i, 0)), pl.BlockSpec((1, LANE), lambda i: (0, 0))],
        out_specs=(pl.BlockSpec((ts, LANE), lambda i: (i, 0)), pl.BlockSpec((LANE, ts), lambda i: (0, i))),
        scratch_shapes=[pltpu.VMEM((1, LANE), F32)],
        name="forget_cumsum", compiler_params=_cp(("arbitrary",)))(flog, fb)


def _pick_row(m, idx):
    r = lax.broadcasted_iota(jnp.int32, (m.shape[0], 1), 0)
    return jnp.sum(jnp.where(r == idx, m, 0.0), axis=0, keepdims=True)


def _pick_col(m, idx):
    cidx = lax.broadcasted_iota(jnp.int32, (1, m.shape[1]), 1)
    return jnp.sum(jnp.where(cidx == idx, m, 0.0), axis=1, keepdims=True)


def _split3(x):
    hi = x.astype(BF16)
    r1 = x - hi.astype(F32)
    mid = r1.astype(BF16)
    lo = (r1 - mid.astype(F32)).astype(BF16)
    return hi, mid, lo


def _head_mask(lane, hh):
    lo = lane < HEAD_DIM
    return lo if hh == 0 else jnp.logical_not(lo)


def _attn_prep(k, v, cum):
    s, d = k.shape
    npair = d // LANE
    tc = _tile(s, 512)

    def body(k_ref, v_ref, c_ref, ka_ref, kt_ref, vt_ref):
        p = pl.program_id(0)
        lane = lax.broadcasted_iota(jnp.int32, (1, LANE), 1)
        kk = k_ref[...]
        vv = v_ref[...].astype(F32)
        ckt = c_ref[...]
        for hh in range(2):
            head = _head_mask(lane, hh)
            b = SPARE[hh]
            ck = _pick_col(ckt, 2 * p + hh)
            extra = jnp.where(lane == b + NPIECE, 1.0, 0.0).astype(BF16) + jnp.zeros((tc, LANE), BF16)
            for n_, pc in enumerate(_split3(ck)):
                extra = jnp.where(lane == b + n_, pc, extra)
            ka = jnp.where(head, kk, extra)
            ka_ref[0, hh] = ka
            kt_ref[0, hh] = ka.astype(F32).T.astype(BF16)
            vx = jnp.where(head, vv, jnp.where(lane == b, 1.0, 0.0))
            vt_ref[0, hh] = vx.T.astype(BF16)

    blk = pl.BlockSpec((tc, LANE), lambda p, c: (c, p))
    return pl.pallas_call(
        body, grid=(npair, s // tc),
        out_shape=(_sds((npair, 2, s, LANE), BF16), _sds((npair, 2, LANE, s), BF16), _sds((npair, 2, LANE, s), BF16)),
        in_specs=[blk, blk, pl.BlockSpec((tc, LANE), lambda p, c: (c, 0))],
        out_specs=(pl.BlockSpec((1, 2, tc, LANE), lambda p, c: (p, 0, c, 0)),
                   pl.BlockSpec((1, 2, LANE, tc), lambda p, c: (p, 0, 0, c)),
                   pl.BlockSpec((1, 2, LANE, tc), lambda p, c: (p, 0, 0, c))),
        name="fox_attn_prep", compiler_params=_cp(("parallel", "parallel")))(k, v, cum)


def _q_aug(qq, lane, hh):
    b = SPARE[hh]
    sel = jnp.logical_and(lane >= b, lane < b + NPIECE)
    neg = jnp.full((1, LANE), -1.0, BF16)
    zl = jnp.zeros((1, LANE), BF16)
    return jnp.where(_head_mask(lane, hh), qq, jnp.where(sel, neg, zl))


def _attn_fwd(q, kaug, vtr, cumt):
    s, d = q.shape
    tq = _tile(s, ATT_TQ)
    tk = _tile(s, ATT_TK)
    npair = d // LANE
    npart = max(1, tq // tk)

    def body(q_ref, ka_ref, vt_ref, cumt_ref, o_ref, lse_ref):
        p = pl.program_id(0)
        i = pl.program_id(1)
        lane = lax.broadcasted_iota(jnp.int32, (1, LANE), 1)
        qq = q_ref[...]
        qx = (_q_aug(qq, lane, 0), _q_aug(qq, lane, 1))
        cqt = cumt_ref[:, pl.ds(pl.multiple_of(i * tq, tq), tq)]
        cq = (_pick_row(cqt, 2 * p), _pick_row(cqt, 2 * p + 1))
        jd = (i * tq) // tk

        def kv_step(j, carry, diag, q_lo=0):
            ks = pl.multiple_of(j * tk, tk)
            nq_ = tq - q_lo
            if diag:
                krow = lax.broadcasted_iota(jnp.int32, (tk, nq_), 0) + j * tk
                qcol = lax.broadcasted_iota(jnp.int32, (tk, nq_), 1) + (i * tq + q_lo)
                causal = krow <= qcol
            out = []
            for hh in range(2):
                m_all, acc_all = carry[2 * hh], carry[2 * hh + 1]
                m, acc, cqh = m_all[:, q_lo:], acc_all[:, q_lo:], cq[hh][:, q_lo:]
                sc = _dot_nt(ka_ref[0, hh, pl.ds(ks, tk), :], qx[hh][q_lo:, :])
                if diag:
                    sc = jnp.where(causal, sc, -jnp.inf)
                mx = jnp.max(sc, axis=0, keepdims=True) + cqh
                mn = jnp.maximum(m, mx)
                alpha = jnp.exp(m - mn)
                pt = jnp.exp(sc + (cqh - mn)).astype(BF16)
                acc = alpha * acc + _dot(vt_ref[0, hh, :, pl.ds(ks, tk)], pt)
                if q_lo:
                    mn = jnp.concatenate([m_all[:, :q_lo], mn], axis=1)
                    acc = jnp.concatenate([acc_all[:, :q_lo], acc], axis=1)
                out += [mn, acc]
            return tuple(out)

        minit = jnp.full((1, tq), -jnp.inf, F32)
        ainit = jnp.zeros((LANE, tq), F32)
        carry = (minit, ainit, minit, ainit)
        for pj in range(npart):
            carry = kv_step(jd + pj, carry, True, q_lo=pj * tk)
        carry = lax.fori_loop(0, jd, lambda j, cr: kv_step(j, cr, False), carry)
        m0, a0, m1, a1 = carry
        l0 = a0[SPARE[0]:SPARE[0] + 1, :]
        l1 = a1[SPARE[1]:SPARE[1] + 1, :]
        row = lax.broadcasted_iota(jnp.int32, (LANE, 1), 0)
        ot = jnp.where(row < HEAD_DIM, a0 / l0, a1 / l1)
        o_ref[...] = ot.T.astype(BF16)
        r8 = lax.broadcasted_iota(jnp.int32, (8, 1), 0)
        lse_ref[0] = jnp.where(r8 == 0, m0 + jnp.log(l0), jnp.where(r8 == 1, m1 + jnp.log(l1), 0.0))

    return pl.pallas_call(
        body, grid=(npair, s // tq), out_shape=(_sds((s, d), BF16), _sds((npair, 8, s), F32)),
        in_specs=[pl.BlockSpec((tq, LANE), lambda p, i: (i, p)),
                  pl.BlockSpec((1, 2, s, LANE), lambda p, i: (p, 0, 0, 0)),
                  pl.BlockSpec((1, 2, LANE, s), lambda p, i: (p, 0, 0, 0)),
                  pl.BlockSpec((N_HEADS, s), lambda p, i: (0, 0))],
        out_specs=(pl.BlockSpec((tq, LANE), lambda p, i: (i, p)), pl.BlockSpec((1, 8, tq), lambda p, i: (p, 0, i))),
        name="fox_attn_fwd", compiler_params=_cp(("parallel", "parallel")))(q, kaug, vtr, cumt)


def _final(x, gain, target):
    s, d = x.shape
    ts = _tile(s, 512)

    def body(x_ref, g_ref, t_ref, lsum_ref, dx_ref, dg_ref):
        @pl.when(pl.program_id(0) == 0)
        def _():
            lsum_ref[...] = jnp.zeros_like(lsum_ref)
            dg_ref[...] = jnp.zeros_like(dg_ref)

        xhat, rstd = _rms_parts(x_ref[...])
        e = xhat * g_ref[...] - t_ref[...]
        lsum_ref[...] += _colsum(e * e)
        dout = e * (1.0 / d)
        dg_ref[...] += _colsum(dout * xhat)
        dxhat = dout * g_ref[...]
        dx_ref[...] = rstd * (dxhat - xhat * jnp.mean(dxhat * xhat, axis=-1, keepdims=True))

    vec = pl.BlockSpec((1, d), lambda i: (0, 0))
    row = pl.BlockSpec((ts, d), lambda i: (i, 0))
    return pl.pallas_call(
        body, grid=(s // ts,), out_shape=(_sds((1, d), F32), _sds((s, d), F32), _sds((1, d), F32)),
        in_specs=[row, vec, row], out_specs=(vec, row, vec),
        name="final_norm_loss", compiler_params=_cp(("arbitrary",)))(x, gain, target)


def _ffn_bwd_act(dx, gate, w_out, ug, uu, name, comm=None):
    s, d = dx.shape
    f = w_out.shape[0]
    n = f // 2
    ts = _tile(s, 512)

    def body(dx_ref, g_ref, w_ref, ug_ref, uu_ref, dug_ref, duu_ref, dys):
        @pl.when(pl.program_id(1) == 0)
        def _():
            dys[...] = (dx_ref[...] * g_ref[...]).astype(BF16)

        dact = _dot_nt(dys[...], w_ref[...])
        g = ug_ref[...].astype(F32)
        u = uu_ref[...].astype(F32)
        sg = _sigmoid(g)
        dug_ref[...] = (dact * u * sg * (1.0 + g * (1.0 - sg))).astype(BF16)
        duu_ref[...] = (dact * g * sg).astype(BF16)

    tile = pl.BlockSpec((ts, n), lambda i, q: (i, q))
    return _hosted_call(
        body, comm, grid=(s // ts, 2), out_shape=(_sds((s, f), BF16), _sds((s, f), BF16)),
        in_specs=[pl.BlockSpec((ts, d), lambda i, q: (i, 0)), pl.BlockSpec((1, d), lambda i, q: (0, 0)),
                  pl.BlockSpec((n, d), lambda i, q: (q, 0)), tile, tile],
        out_specs=(tile, tile), scratch_shapes=[pltpu.VMEM((ts, d), BF16)],
        name=name, sem=("parallel", "arbitrary"), args=(dx, gate, w_out, ug, uu))


def _dw_mm(a, b_list, tk, tn, name, gate=None, wfull=None, dgate_init=None):
    s, kdim = a.shape
    nb1 = b_list[0].shape[1] // tn
    nb = nb1 * len(b_list)
    ts = _tile(s, 512)
    nk = kdim // tk
    ns = s // ts
    gated = gate is not None

    def body(*refs):
        a_ref = refs[0]
        b_refs = refs[1:1 + len(b_list)]
        rest = refs[1 + len(b_list):]
        if gated:
            g_ref, w_ref, di_ref, o_ref, dg_ref, acc = rest
        else:
            o_ref, acc = rest
        jn, ik, st = pl.program_id(0), pl.program_id(1), pl.program_id(2)

        @pl.when(st == 0)
        def _():
            acc[...] = jnp.zeros_like(acc)

        for mi, b_ref in enumerate(b_refs):
            @pl.when(jn // nb1 == mi)
            def _(b_ref=b_ref):
                acc[...] += _dot_tn(a_ref[...], b_ref[...].astype(BF16))

        if gated:
            @pl.when(jnp.logical_and(ik == 0, st == 0))
            def _():
                dg_ref[...] = di_ref[...]

        @pl.when(st == ns - 1)
        def _():
            if gated:
                o_ref[0] = acc[...] * g_ref[...]
                dg_ref[...] += _colsum(acc[...] * w_ref[...].astype(F32))
            else:
                o_ref[0] = acc[...]

    in_specs = [pl.BlockSpec((ts, tk), lambda jn, ik, st: (st, ik))]
    for mi in range(len(b_list)):
        in_specs.append(pl.BlockSpec(
            (ts, tn), lambda jn, ik, st, mi=mi: (st, jnp.clip(jn - mi * nb1, 0, nb1 - 1))))
    args = [a] + list(b_list)
    out_shape = [_sds((nb, kdim, tn), F32)]
    out_specs = [pl.BlockSpec((1, tk, tn), lambda jn, ik, st: (jn, ik, 0))]
    if gated:
        vec = pl.BlockSpec((1, tn), lambda jn, ik, st: (0, jn))
        in_specs += [vec, pl.BlockSpec((tk, tn), lambda jn, ik, st: (ik, jn)), vec]
        args += [gate, wfull, dgate_init]
        out_shape.append(_sds((1, nb * tn), F32))
        out_specs.append(vec)
    res = pl.pallas_call(
        body, grid=(nb, nk, ns), out_shape=tuple(out_shape), in_specs=in_specs, out_specs=tuple(out_specs),
        scratch_shapes=[pltpu.VMEM((tk, tn), F32)],
        name=name, compiler_params=_cp(("parallel", "arbitrary", "arbitrary")))(*args)
    return res if gated else res[0]


def _mm_normbwd(terms, x, dxres, gain, scale, name, ts_pref=256, comm=None):
    s, d = x.shape
    ts = _tile(s, ts_pref)
    arrs, warrs = [], []
    for a, _, w, _ in terms:
        if not any(a is z for z in arrs):
            arrs.append(a)
        if not any(w is z for z in warrs):
            warrs.append(w)
    ai = [next(i for i, z in enumerate(arrs) if z is a) for a, _, _, _ in terms]
    wi = [next(i for i, z in enumerate(warrs) if z is w) for _, _, w, _ in terms]

    def body(*refs):
        a_refs = refs[:len(arrs)]
        w_refs = refs[len(arrs):len(arrs) + len(warrs)]
        x_ref, dr_ref, g_ref, sc_ref, dx_ref, dsh_ref, dsc_ref, dg_ref = refs[len(arrs) + len(warrs):]

        @pl.when(pl.program_id(0) == 0)
        def _():
            dsh_ref[...] = jnp.zeros_like(dsh_ref)
            dsc_ref[...] = jnp.zeros_like(dsc_ref)
            dg_ref[...] = jnp.zeros_like(dg_ref)

        dh = None
        for ti, (_, c0, w, q) in enumerate(terms):
            n = w.shape[2]
            part = _dot_nt(a_refs[ai[ti]][:, pl.ds(c0, n)], w_refs[wi[ti]][q])
            dh = part if dh is None else dh + part
        xhat, rstd = _rms_parts(x_ref[...])
        nrm = xhat * g_ref[...]
        dsh_ref[...] += _colsum(dh)
        dsc_ref[...] += _colsum(dh * nrm)
        dn = dh * (1.0 + sc_ref[...])
        dg_ref[...] += _colsum(dn * xhat)
        dxhat = dn * g_ref[...]
        dx_ref[...] = dr_ref[...] + rstd * (dxhat - xhat * jnp.mean(dxhat * xhat, axis=-1, keepdims=True))

    vec = pl.BlockSpec((1, d), lambda i: (0, 0))
    row = pl.BlockSpec((ts, d), lambda i: (i, 0))
    in_specs = [pl.BlockSpec((ts, a.shape[1]), lambda i: (i, 0)) for a in arrs]
    in_specs += [_resident(w.shape) for w in warrs]
    in_specs += [row, row, vec, vec]
    return _hosted_call(
        body, comm, grid=(s // ts,), out_shape=(_sds((s, d), F32), _sds((1, d), F32), _sds((1, d), F32), _sds((1, d), F32)),
        in_specs=in_specs, out_specs=(row, vec, vec, vec), scratch_shapes=[],
        name=name, sem=("arbitrary",), args=(*arrs, *warrs, x, dxres, gain, scale))


def _do_kernel(dx, gate, wo, o):
    s, d = dx.shape
    ts = _tile(s, 512)

    def body(dx_ref, g_ref, w_ref, o_ref, do_ref, dl_ref):
        dy = (dx_ref[...] * g_ref[...]).astype(BF16)
        do = _dot_nt(dy, w_ref[...])
        do_ref[...] = do.astype(BF16)
        prod = do * o_ref[...].astype(F32)
        hrow = lax.broadcasted_iota(jnp.int32, (N_HEADS, d), 0)
        hcol = lax.broadcasted_iota(jnp.int32, (N_HEADS, d), 1) // HEAD_DIM
        sel = (hrow == hcol).astype(F32)
        dl_ref[...] = lax.dot_general(sel, prod, (((1,), (1,)), ((), ())), preferred_element_type=F32,
                                      precision=lax.Precision.HIGHEST)

    row = pl.BlockSpec((ts, d), lambda i: (i, 0))
    return pl.pallas_call(
        body, grid=(s // ts,), out_shape=(_sds((s, d), BF16), _sds((N_HEADS, s), F32)),
        in_specs=[row, pl.BlockSpec((1, d), lambda i: (0, 0)), _resident(wo.shape), row],
        out_specs=(row, pl.BlockSpec((N_HEADS, ts), lambda i: (0, i))),
        name="attn_do", compiler_params=_cp(("parallel",)))(dx, gate, wo, o)


def _attn_bwd(q, do, kaug, kaugt, v, cumt, lse, deltat, comm=None):
    s, d = q.shape
    tq = _tile(s, ATT_TQ)
    tk = _tile(s, ATT_TK)
    assert tq in (tk, 2 * tk)
    npair = d // LANE
    nq = s // tq
    nkb = s // tk
    qscale = HEAD_DIM ** -0.5

    def body(q_ref, do_ref, ka_ref, kt_ref, v_ref, cumt_ref, lse_ref, dl_ref,
             dq_ref, dk_ref, dv_ref, dcq_ref, dck_ref, qaug, dom, rowv, dqt):
        p = pl.program_id(0)
        j = pl.program_id(1)
        lane = lax.broadcasted_iota(jnp.int32, (1, LANE), 1)
        lo = lane < HEAD_DIM
        r8 = lax.broadcasted_iota(jnp.int32, (8, 1), 0)

        @pl.when(j == 0)
        def _():
            dqt[...] = jnp.zeros_like(dqt)
            for c in range(nq):
                rows = pl.ds(c * tq, tq)
                qq = q_ref[rows, :]
                dd = do_ref[rows, :]
                cqt = cumt_ref[:, rows]
                dlt = dl_ref[:, rows]
                lst = lse_ref[0, :, rows]
                for hh in range(2):
                    qaug[hh, rows, :] = _q_aug(qq, lane, hh)
                    dom[hh, rows, :] = jnp.where(_head_mask(lane, hh), dd, jnp.zeros_like(dd))
                    rowv[hh, :, rows] = jnp.where(
                        r8 == 0, _pick_row(cqt, 2 * p + hh) - lst[hh:hh + 1, :],
                        jnp.where(r8 == 1, _pick_row(dlt, 2 * p + hh), 0.0))

        vv = v_ref[...]
        i0 = (j * tk) // tq

        def q_step(qs, nq_, carry, diag):
            dv_acc, dk0, dk1 = carry
            qs = pl.multiple_of(qs, tk)
            if diag:
                krow = lax.broadcasted_iota(jnp.int32, (tk, nq_), 0) + j * tk
                qcol = lax.broadcasted_iota(jnp.int32, (tk, nq_), 1) + qs
                causal = krow <= qcol
            dks = [dk0, dk1]
            for hh in range(2):
                rv = rowv[hh, :, pl.ds(qs, nq_)]
                qa = qaug[hh, pl.ds(qs, nq_), :]
                dh = dom[hh, pl.ds(qs, nq_), :]
                sc = _dot_nt(ka_ref[0, hh], qa)
                if diag:
                    sc = jnp.where(causal, sc, -jnp.inf)
                pt = jnp.exp(sc + rv[0:1, :])
                dpt = _dot_nt(vv, dh)
                dst = (pt * (dpt - rv[1:2, :])).astype(BF16)
                dv_acc = dv_acc + _dot(pt.astype(BF16), dh)
                dks[hh] = dks[hh] + _dot(dst, qa)
                dqt[hh, :, pl.ds(qs, nq_)] += _dot(kt_ref[0, hh], dst)
            return dv_acc, dks[0], dks[1]

        z = jnp.zeros((tk, LANE), F32)
        first = ((j * tk) % tq == 0).astype(jnp.int32)
        carry = lax.fori_loop(0, first, lambda _, cr: q_step(i0 * tq, tq, cr, True), (z, z, z))
        if tq > tk:
            carry = lax.fori_loop(0, 1 - first, lambda _, cr: q_step(j * tk, tq - tk, cr, True), carry)
        dv_acc, dk0, dk1 = lax.fori_loop(i0 + 1, nq, lambda i, cr: q_step(i * tq, tq, cr, False), carry)
        dv_ref[...] = dv_acc.astype(BF16)
        dk_ref[...] = jnp.where(lo, dk0, dk1).astype(BF16)
        dck_ref[0] = jnp.where(r8 == 0, dk0.T[SPARE[0]:SPARE[0] + 1, :],
                               jnp.where(r8 == 1, dk1.T[SPARE[1]:SPARE[1] + 1, :], 0.0))

        @pl.when(j == nkb - 1)
        def _():
            for c in range(nq):
                rows = pl.ds(c * tq, tq)
                a0 = dqt[0, :, rows].T
                a1 = dqt[1, :, rows].T
                dq_ref[rows, :] = (jnp.where(lo, a0, a1) * qscale).astype(BF16)
            r0, r1 = SPARE[0] + NPIECE, SPARE[1] + NPIECE
            dcq_ref[0] = jnp.where(r8 == 0, dqt[0, r0:r0 + 1, :], jnp.where(r8 == 1, dqt[1, r1:r1 + 1, :], 0.0))

    col = pl.BlockSpec((s, LANE), lambda p, j: (0, p), pipeline_mode=pl.Buffered(1))
    rows16 = pl.BlockSpec((N_HEADS, s), lambda p, j: (0, 0), pipeline_mode=pl.Buffered(1))
    blk = pl.BlockSpec((tk, LANE), lambda p, j: (j, p))
    return _hosted_call(
        body, comm, grid=(npair, nkb),
        out_shape=(_sds((s, d), BF16), _sds((s, d), BF16), _sds((s, d), BF16), _sds((npair, 8, s), F32), _sds((npair, 8, s), F32)),
        in_specs=[col, col, pl.BlockSpec((1, 2, tk, LANE), lambda p, j: (p, 0, j, 0)),
                  pl.BlockSpec((1, 2, LANE, tk), lambda p, j: (p, 0, 0, j)), blk, rows16,
                  pl.BlockSpec((1, 8, s), lambda p, j: (p, 0, 0), pipeline_mode=pl.Buffered(1)), rows16],
        out_specs=(pl.BlockSpec((s, LANE), lambda p, j: (0, p)), blk, blk,
                   pl.BlockSpec((1, 8, s), lambda p, j: (p, 0, 0)), pl.BlockSpec((1, 8, tk), lambda p, j: (p, 0, j))),
        scratch_shapes=[pltpu.VMEM((2, s, LANE), BF16), pltpu.VMEM((2, s, LANE), BF16), pltpu.VMEM((2, 8, s), F32),
                        pltpu.VMEM((2, LANE, s), F32)],
        name="fox_attn_bwd", sem=("arbitrary", "arbitrary"), vmem_mb=ATT_BWD_VMEM_MB,
        args=(q, do, kaug, kaugt, v, cumt, lse, deltat))


def _cumsum_bwd(dcq, dck, flog, fb):
    s = flog.shape[0]
    ts = _tile(s, 256)
    nt = s // ts

    def body(dq_ref, dk_ref, f_ref, b_ref, df_ref, db_ref, carry):
        @pl.when(pl.program_id(0) == 0)
        def _():
            carry[...] = jnp.zeros_like(carry)
            db_ref[...] = jnp.zeros_like(db_ref)

        r = lax.broadcasted_iota(jnp.int32, (ts, ts), 0)
        cidx = lax.broadcasted_iota(jnp.int32, (ts, ts), 1)
        tri = (r >= cidx).astype(F32)
        dct = dq_ref[...] + dk_ref[...]
        dlst = jnp.dot(dct, tri, preferred_element_type=F32, precision=lax.Precision.HIGHEST) + carry[...]
        carry[...] = dlst[:, 0:1]
        dls = jnp.concatenate([dlst, jnp.zeros((LANE - N_HEADS, ts), F32)], axis=0).T
        z = f_ref[...] + b_ref[...]
        df = dls * (1.0 / (1.0 + jnp.exp(z)))
        db_ref[...] += _colsum(df)
        df_ref[...] = df.astype(BF16)

    rev = pl.BlockSpec((ts, LANE), lambda i: (nt - 1 - i, 0))
    revt = pl.BlockSpec((N_HEADS, ts), lambda i: (0, nt - 1 - i))
    vec = pl.BlockSpec((1, LANE), lambda i: (0, 0))
    return pl.pallas_call(
        body, grid=(nt,), out_shape=(_sds((s, LANE), BF16), _sds((1, LANE), F32)),
        in_specs=[revt, revt, rev, vec], out_specs=(rev, vec), scratch_shapes=[pltpu.VMEM((N_HEADS, 1), F32)],
        name="forget_cumsum_bwd", compiler_params=_cp(("arbitrary",)))(dcq, dck, flog, fb)


def _conv_bwd1(dx, gate, w_out, b_out, dwo, lng, lnb):
    s, d = dx.shape
    ts = _tile(s, 512)
    ns = s // ts

    def body(dx_ref, g_ref, w_ref, bo_ref, y_ref, lg_ref, lb_ref, dd_ref, dlg_ref, dlb_ref, dbd_ref, dbo_ref, dge_ref, cs):
        i = pl.program_id(0)

        @pl.when(i == 0)
        def _():
            for r in (dlg_ref, dlb_ref, dbd_ref, cs):
                r[...] = jnp.zeros_like(r)

        dxv = dx_ref[...]
        cs[...] += _colsum(dxv)
        dsw = _dot_nt((dxv * g_ref[...]).astype(BF16), w_ref[...])
        yv = y_ref[...]
        mu = jnp.mean(yv, axis=-1, keepdims=True)
        yc = yv - mu
        rstd = lax.rsqrt(jnp.mean(yc * yc, axis=-1, keepdims=True) + EPS)
        xhat = yc * rstd
        ln = xhat * lg_ref[...] + lb_ref[...]
        sg = _sigmoid(ln)
        dln = dsw * (sg * (1.0 + ln * (1.0 - sg)))
        dlg_ref[...] += _colsum(dln * xhat)
        dlb_ref[...] += _colsum(dln)
        dxh = dln * lg_ref[...]
        dd = rstd * (dxh - jnp.mean(dxh, axis=-1, keepdims=True) - xhat * jnp.mean(dxh * xhat, axis=-1, keepdims=True))
        dbd_ref[...] += _colsum(dd)
        dd_ref[...] = dd

        @pl.when(i == ns - 1)
        def _():
            dbo_ref[...] = g_ref[...] * cs[...]
            dge_ref[...] = bo_ref[...] * cs[...]

    vec = pl.BlockSpec((1, d), lambda i: (0, 0))
    row = pl.BlockSpec((ts, d), lambda i: (i, 0))
    return pl.pallas_call(
        body, grid=(ns,), out_shape=(_sds((s, d), F32),) + tuple(_sds((1, d), F32) for _ in range(5)),
        in_specs=[row, vec, _resident(w_out.shape), vec, row, vec, vec], out_specs=(row, vec, vec, vec, vec, vec),
        scratch_shapes=[pltpu.VMEM((1, d), F32)],
        name="conv_bwd_ln", compiler_params=_cp(("arbitrary",)))(dx, gate, w_out, b_out, dwo, lng, lnb)


def _dwconv_bwd(ddwo, glu, a_s, g_s, wdw, comm=None):
    s, d = ddwo.shape
    ts = _tile(s, 256)
    ns = s // ts
    rb, cb = 32, 256
    nrb = ts // rb

    def body(dd_ref, ddn_ref, gl_ref, glh_ref, a_ref, g_ref, w_ref, da_ref, dg_ref, dw_ref, sa_ref, sg_ref, bufd, bufg, dws,
             shd, shg):
        i = pl.program_id(0)

        @pl.when(i == 0)
        def _():
            dws[...] = jnp.zeros_like(dws)
            sa_ref[...] = jnp.zeros_like(sa_ref)
            sg_ref[...] = jnp.zeros_like(sg_ref)
            bufg[pl.ds(0, HALO), :] = jnp.zeros((HALO, d), F32)

        @pl.when(i > 0)
        def _():
            bufg[pl.ds(0, HALO), :] = glh_ref[...]

        bufg[pl.ds(HALO, ts), :] = gl_ref[...]
        bufd[pl.ds(0, ts), :] = dd_ref[...]

        @pl.when(i == ns - 1)
        def _():
            bufd[pl.ds(ts, HALO), :] = jnp.zeros((HALO, d), F32)

        @pl.when(i < ns - 1)
        def _():
            bufd[pl.ds(ts, HALO), :] = ddn_ref[...]

        _shift_copies(bufd, shd)
        _shift_copies(bufg, shg)
        for cc in range(d // cb):
            cs = pl.ds(cc * cb, cb)
            for r in range(nrb):
                acc = jnp.zeros((rb, cb), F32)
                for k in range(CONV_K):
                    acc = acc + w_ref[pl.ds(k, 1), cs] * _shifted(bufd, shd, r * rb + (CONV_K - 1) - k, rb, cs)
                rows = pl.ds(r * rb, rb)
                av = a_ref[rows, cs].astype(F32)
                sg = _sigmoid(g_ref[rows, cs].astype(F32))
                dav = acc * sg
                dgv = acc * av * sg * (1.0 - sg)
                da_ref[rows, cs] = dav.astype(BF16)
                dg_ref[rows, cs] = dgv.astype(BF16)
                sa_ref[:, cs] += _colsum(dav)
                sg_ref[:, cs] += _colsum(dgv)
            for k in range(CONV_K):
                acc8 = jnp.zeros((8, cb), F32)
                for r in range(nrb):
                    prod = bufd[pl.ds(r * rb, rb), cs] * _shifted(bufg, shg, HALO - (CONV_K - 1) + k + r * rb, rb, cs)
                    acc8 = acc8 + (prod[0:8] + prod[8:16]) + (prod[16:24] + prod[24:32])
                dws[pl.ds(8 * k, 8), cs] += acc8

        @pl.when(i == ns - 1)
        def _():
            dw_ref[...] = jnp.zeros_like(dw_ref)
            for k in range(CONV_K):
                dw_ref[pl.ds(k, 1), :] = _colsum(dws[pl.ds(8 * k, 8), :])

    row = pl.BlockSpec((ts, d), lambda i: (i, 0))
    vec = pl.BlockSpec((1, d), lambda i: (0, 0))
    hb = ts // HALO
    return _hosted_call(
        body, comm, grid=(ns,),
        out_shape=(_sds((s, d), BF16), _sds((s, d), BF16), _sds((HALO, d), F32), _sds((1, d), F32), _sds((1, d), F32)),
        in_specs=[row, pl.BlockSpec((HALO, d), lambda i: (jnp.minimum((i + 1) * hb, ns * hb - 1), 0)),
                  row, pl.BlockSpec((HALO, d), lambda i: (jnp.maximum(i * hb - 1, 0), 0)),
                  row, row, pl.BlockSpec((HALO, d), lambda i: (0, 0))],
        out_specs=(row, row, pl.BlockSpec((HALO, d), lambda i: (0, 0)), vec, vec),
        scratch_shapes=[pltpu.VMEM((ts + HALO, d), F32), pltpu.VMEM((HALO + ts, d), F32), pltpu.VMEM((8 * HALO, d), F32),
                        pltpu.VMEM((SUBLANES - 1, HALO + ts - SUBLANES, d), F32),
                        pltpu.VMEM((SUBLANES - 1, HALO + ts - SUBLANES, d), F32)],
        name="dwconv_bwd", sem=("arbitrary",), args=(ddwo, ddwo, glu, glu, a_s, g_s, wdw))


def _ada_wgrad(cat, da, name):
    nl, _, n = da.shape
    d = cat.shape[0]
    tn = 256

    def body(c_ref, d_ref, o_ref):
        acc = c_ref[:, 0:1] * d_ref[0, 0:1, :]
        for r in range(1, 8):
            acc = acc + c_ref[:, r:r + 1] * d_ref[0, r:r + 1, :]
        o_ref[0] = acc

    return pl.pallas_call(
        body, grid=(nl, n // tn), out_shape=_sds((nl, d, n), F32),
        in_specs=[pl.BlockSpec((d, 8), lambda l, j: (0, 0)), pl.BlockSpec((1, 8, tn), lambda l, j: (l, 0, j))],
        out_specs=pl.BlockSpec((1, d, tn), lambda l, j: (l, 0, j)),
        name=name, compiler_params=_cp(("parallel", "parallel")))(cat, da)


def _silu_rows(c_all):
    def body(c_ref, o_ref):
        cc = c_ref[...]
        o_ref[...] = cc * _sigmoid(cc)

    return pl.pallas_call(body, out_shape=_sds(c_all.shape, F32), name="silu_c")(c_all)


def _adamw(w, g, m, v, name):
    r, c = w.shape
    tr = r
    for cand in (512, 256, 128, 64, 32, 16, 8):
        if r % cand == 0 and cand * c * 4 <= (1 << 20):
            tr = cand
            break
    bc1 = 1.0 - ADAM_B1 ** ADAM_STEP
    bc2 = 1.0 - ADAM_B2 ** ADAM_STEP

    def body(w_ref, g_ref, m_ref, v_ref, d_ref, nm_ref, nv_ref):
        gv = g_ref[...]
        mn = ADAM_B1 * m_ref[...] + (1.0 - ADAM_B1) * gv
        vn = ADAM_B2 * v_ref[...] + (1.0 - ADAM_B2) * (gv * gv)
        mh = mn / bc1
        vh = vn / bc2
        d_ref[...] = -ADAM_LR * (mh / (jnp.sqrt(vh) + ADAM_EPS) + ADAM_WD * w_ref[...])
        nm_ref[...] = mn
        nv_ref[...] = vn

    blk = pl.BlockSpec((tr, c), lambda i: (i, 0))
    return pl.pallas_call(
        body, grid=(r // tr,), out_shape=tuple(_sds((r, c), F32) for _ in range(3)),
        in_specs=[blk, blk, blk, blk], out_specs=(blk, blk, blk),
        name=name, compiler_params=_cp(("parallel",)))(w, g, m, v)


def _halves(w2):
    r, c = w2.shape
    return w2.reshape(2, (r // 2) * c // 1024, 1024)


def _pad_rows(a, rows, axis):
    pad = [(0, 0)] * a.ndim
    pad[axis] = (0, rows - a.shape[axis])
    return jnp.pad(a, pad)


def _vec(a):
    return a.reshape(1, -1)


def kernel(x, c, mix_norm_g, mix_ada_w, mix_ada_b, ffn_norm_g, ffn_ada_w, ffn_ada_b, ffn_w_in, ffn_w_out, conv_w_in, conv_b_in, conv_w_dw, conv_b_dw, conv_ln_g, conv_ln_b, conv_w_out, conv_b_out, kv_norm_g, kv_ada_w, kv_ada_b, kv_w, forget_b, attn_w_q, attn_w_o, final_norm_g, loss_target, m_mix_norm_g, m_mix_ada_w, m_mix_ada_b, m_ffn_norm_g, m_ffn_ada_w, m_ffn_ada_b, m_ffn_w_in, m_ffn_w_out, m_conv_w_in, m_conv_b_in, m_conv_w_dw, m_conv_b_dw, m_conv_ln_g, m_conv_ln_b, m_conv_w_out, m_conv_b_out, m_kv_norm_g, m_kv_ada_w, m_kv_ada_b, m_kv_w, m_forget_b, m_attn_w_q, m_attn_w_o, m_final_norm_g, v_mix_norm_g, v_mix_ada_w, v_mix_ada_b, v_ffn_norm_g, v_ffn_ada_w, v_ffn_ada_b, v_ffn_w_in, v_ffn_w_out, v_conv_w_in, v_conv_b_in, v_conv_w_dw, v_conv_b_dw, v_conv_ln_g, v_conv_ln_b, v_conv_w_out, v_conv_b_out, v_kv_norm_g, v_kv_ada_w, v_kv_ada_b, v_kv_w, v_forget_b, v_attn_w_q, v_attn_w_o, v_final_norm_g):
    xi, yi, ci = lax.axis_index("x"), lax.axis_index("y"), lax.axis_index("c")
    chip = 2 * xi + yi
    dev = 4 * xi + 2 * yi + ci
    s, d = x.shape[1], x.shape[2]
    f = ffn_w_out.shape[1] * 4
    x0 = x[0]
    nkv = kv_w.shape[1]
    nkv_all = 4 * nkv

    wdw_loc = _pad_rows(conv_w_dw[0], HALO, 0)
    small = jnp.concatenate([c.reshape(-1), conv_b_in.reshape(-1), wdw_loc.reshape(-1), conv_b_dw.reshape(-1),
                             conv_ln_g.reshape(-1), conv_ln_b.reshape(-1), conv_b_out.reshape(-1)])
    n_small = small.shape[0]
    w_small = -(-n_small // (8 * LANE)) * LANE
    small = jnp.pad(small, (0, 8 * w_small - n_small)).reshape(8, w_small)
    small_all = _allgather8(small, "ag_small_params", True).reshape(8, 8 * w_small)
    c_all = small_all[:, :d]
    per_chip = small_all[0::2]
    dq_ = d // 4
    o1 = d
    b_in_full = per_chip[:, o1:o1 + 2 * dq_].reshape(4, 1, 2 * dq_)
    o1 += 2 * dq_
    wdw_full = per_chip[:, o1:o1 + HALO * dq_].reshape(4, HALO, dq_).transpose(1, 0, 2).reshape(HALO, d)
    o1 += HALO * dq_
    bdw_full = per_chip[:, o1:o1 + dq_].reshape(1, d)
    lng_full = per_chip[:, o1 + dq_:o1 + 2 * dq_].reshape(1, d)
    lnb_full = per_chip[:, o1 + 2 * dq_:o1 + 3 * dq_].reshape(1, d)
    bout_full = per_chip[:, o1 + 3 * dq_:o1 + 4 * dq_].reshape(1, d)

    a_mix = _ada_fwd(c_all, mix_ada_w, "ada_mix")
    a_ffn = _ada_fwd(c_all, ffn_ada_w, "ada_ffn")
    a_kv = _ada_fwd(c_all, kv_ada_w[None], "ada_kv")
    n3 = mix_ada_w.shape[2]
    n2 = kv_ada_w.shape[1]
    ada_loc = jnp.concatenate([a_mix[0], a_mix[1], a_ffn[0], a_ffn[1], a_kv[0]], axis=1)
    w_ada = ada_loc.shape[1]
    ada_all = _allgather8(ada_loc, "ag_ada", True).reshape(8, 8, w_ada)
    ada_me = lax.dynamic_index_in_dim(ada_all, dev, axis=1, keepdims=False)[0::2]

    def ada_vec(off, n, bias):
        return ada_me[:, off:off + n].reshape(1, 4 * n) + bias.reshape(1, -1)

    ada_m0 = ada_vec(0, n3, mix_ada_b[0])
    ada_m1 = ada_vec(n3, n3, mix_ada_b[1])
    ada_f0 = ada_vec(2 * n3, n3, ffn_ada_b[0])
    ada_f1 = ada_vec(3 * n3, n3, ffn_ada_b[1])
    ada_k = ada_vec(4 * n3, n2, kv_ada_b)

    def split3(a):
        return a[:, :d], a[:, d:2 * d], a[:, 2 * d:3 * d]

    sh_m0, sc_m0, gt_m0 = split3(ada_m0)
    sh_m1, sc_m1, gt_m1 = split3(ada_m1)
    sh_f0, sc_f0, gt_f0 = split3(ada_f0)
    sh_f1, sc_f1, gt_f1 = split3(ada_f1)
    sh_k, sc_k = ada_k[:, :d], ada_k[:, d:2 * d]

    def layout(ws):
        hr = [(w.shape[0] // 2) * w.shape[1] // 1024 for w in ws]
        pr = [-(-r // 16) * 16 for r in hr]
        return hr, pr, [sum(pr[:i]) for i in range(len(ws))], sum(pr)

    def my_pack(ws):
        _, pr, _, _ = layout(ws)
        pack = jnp.concatenate([_pad_rows(_halves(w.astype(BF16)), p_, 1) for w, p_ in zip(ws, pr)], axis=1)
        return lax.dynamic_index_in_dim(pack, ci, axis=0, keepdims=False)

    def unpack(gath, ws):
        hr, _, offs, rt = layout(ws)
        g4 = gath.reshape(4, 2, rt, 1024)
        return [g4[:, :, o:o + h, :].reshape(4, w.shape[0], w.shape[1]) for w, o, h in zip(ws, offs, hr)]

    grp_conv = [conv_w_in[0], conv_w_out[0]]
    grp_ffn0 = [ffn_w_in[0], ffn_w_out[0]]
    grp_rest = [ffn_w_in[1], ffn_w_out[1], kv_w, attn_w_q[0], attn_w_o[0]]
    cw_in, cw_out = unpack(_allgather8(my_pack(grp_conv), "ag_w_conv", False), grp_conv)
    cw_out = cw_out.reshape(d, d)
    pack_ffn0, pack_rest = my_pack(grp_ffn0), my_pack(grp_rest)

    zero_b = jnp.zeros((1, d), F32)
    g_m0, g_m1 = _vec(mix_norm_g[0]), _vec(mix_norm_g[1])
    g_f0, g_f1 = _vec(ffn_norm_g[0]), _vec(ffn_norm_g[1])
    g_k, g_fin = _vec(kv_norm_g), _vec(final_norm_g)
    fb = jnp.pad(forget_b, (0, LANE - N_HEADS)).reshape(1, LANE)

    h0, glu, a_s, g_s, gath_ffn0 = _in_pair(x0, g_m0, sh_m0, sc_m0, cw_in, b_in_full, True, "conv_in",
                                            comm=(_Gather8(pack_ffn0), pack_ffn0))
    dwo, sw, gath_rest = _dwconv_fwd(glu, wdw_full, bdw_full, lng_full, lnb_full, comm=(_Gather8(pack_rest), pack_rest))
    w_in0, w_out0 = unpack(gath_ffn0, grp_ffn0)
    w_in1, w_out1, kvw, wq, wo = unpack(gath_rest, grp_rest)
    w_in = [w_in0, w_in1]
    w_out = [w_out0.reshape(f, d), w_out1.reshape(f, d)]
    kvw = kvw.transpose(1, 0, 2).reshape(d, nkv_all)
    wk, wv = kvw[:, :d], kvw[:, d:2 * d]
    wf = jnp.pad(kvw[:, 2 * d:], ((0, 0), (0, LANE - N_HEADS)))
    wq, wo = wq.reshape(d, d), wo.reshape(d, d)
    x1 = _mm_res(sw, cw_out, bout_full, gt_m0, x0, "conv_out")
    hf0, act0, ug0, uu0 = _in_pair(x1, g_f0, sh_f0, sc_f0, w_in[0], None, False, "ffn0_in")
    x2 = _mm_res(act0, w_out[0], zero_b, gt_f0, x1, "ffn0_out")
    hk, h1, kk, vv, qq, flog = _qkv(x2, (g_k, sh_k, sc_k), (g_m1, sh_m1, sc_m1), wk, wv, wf, wq)
    cum, cumt = _cumsum_fwd(flog, fb)
    kaug, kaugt, vtr = _attn_prep(kk, vv, cum)
    o, lse = _attn_fwd(qq, kaug, vtr, cumt)
    x3 = _mm_res(o, wo, zero_b, gt_m1, x2, "attn_out")
    hf1, act1, ug1, uu1 = _in_pair(x3, g_f1, sh_f1, sc_f1, w_in[1], None, False, "ffn1_in")
    x4 = _mm_res(act1, w_out[1], zero_b, gt_f1, x3, "ffn1_out")
    lsum, dx4, d_gfin = _final(x4, g_fin, loss_target[0])
    loss = lax.psum(0.5 / d * jnp.sum(lsum), ("x", "y", "c"))

    nf = f // 2

    sel = jnp.stack([ci, chip]).astype(jnp.int32)

    def reduce_begin(gs, ws, tag):
        hr, pr, _, _ = layout(ws)
        gp = jnp.concatenate([_pad_rows(g.reshape(4, 2, h, 1024), p_, 2) for g, h, p_ in zip(gs, hr, pr)], axis=2)
        return _add_halves(gp, _swap_halves(gp, tag + "_swap"), sel, tag + "_add")

    def ffn_bwd(dx_out, x_in, hf, act, ug, uu, gain, scale, gate, w_in_l, w_out_l, tag, comm=None):
        res = _ffn_bwd_act(dx_out, gate, w_out_l, ug, uu, tag + "_bwd_act", comm=comm)
        dug, duu = res[0], res[1]
        dw_out, dgate = _dw_mm(act, [dx_out], nf, d, tag + "_dw_out", gate=gate, wfull=w_out_l, dgate_init=zero_b)
        terms = [(dug, 0, w_in_l, 0), (dug, nf, w_in_l, 1), (duu, 0, w_in_l, 2), (duu, nf, w_in_l, 3)]
        dx_in, dsh, dsc, dgn = _mm_normbwd(terms, x_in, dx_out, gain, scale, tag + "_bwd_in")
        dw_in = _dw_mm(hf, [dug, duu], d, nf, tag + "_dw_in")
        return dx_in, dw_in, dw_out[0], dsh, dsc, dgate, dgn, (res[2] if comm is not None else None)

    dx3, dw_in1, dw_out1, dsh_f1, dsc_f1, dgt_f1, dgn_f1, _ = ffn_bwd(dx4, x3, hf1, act1, ug1, uu1, g_f1, sc_f1, gt_f1, w_in[1], w_out[1], "ffn1")
    grp_ffn1 = [ffn_w_in[1], ffn_w_out[1]]
    q16_1, own_1 = reduce_begin([dw_in1, dw_out1.reshape(4, f // 4, d)], grp_ffn1, "rs_ffn1")

    do, deltat = _do_kernel(dx3, gt_m1, wo, o)
    dwo_att, dgt_m1 = _dw_mm(o, [dx3], d, d, "attn_dw_o", gate=gt_m1, wfull=wo, dgate_init=zero_b)
    dq, dk, dv, dcq, dck, land_1 = _attn_bwd(qq, do, kaug, kaugt, vv, cumt, lse, deltat, comm=(_ScatterChips(q16_1), q16_1))
    wq3 = wq.reshape(1, d, d)
    dx2a, dsh_m1, dsc_m1, dgn_m1 = _mm_normbwd([(dq, 0, wq3, 0)], x2, dx3, g_m1, sc_m1, "attn_bwd_q")
    dwq = _dw_mm(h1, [dq], d, d, "attn_dw_q")[0]

    df, dfb = _cumsum_bwd(dcq[:, :2].reshape(N_HEADS, s), dck[:, :2].reshape(N_HEADS, s), flog, fb)
    terms = [(dk, 0, wk.reshape(1, d, d), 0), (dv, 0, wv.reshape(1, d, d), 0), (df, 0, wf.reshape(1, d, LANE), 0)]
    dx2, dsh_k, dsc_k, dgn_k = _mm_normbwd(terms, x2, dx2a, g_k, sc_k, "kv_bwd")
    dwk = _dw_mm(hk, [dk], d, d, "kv_dw_k")[0]
    dwv = _dw_mm(hk, [dv], d, d, "kv_dw_v")[0]
    dwf = _dw_mm(hk, [df], d, LANE, "kv_dw_f")[0]
    dkvw = jnp.concatenate([dwk, dwv, dwf[:, :N_HEADS]], axis=1)
    dkvw = dkvw.reshape(d, 4, nkv).transpose(1, 0, 2)

    grp_attn = [kv_w, attn_w_q[0], attn_w_o[0]]
    q16_2, own_2 = reduce_begin([dkvw, dwq.reshape(4, d // 4, d), dwo_att[0].reshape(4, d // 4, d)], grp_attn, "rs_attn")
    dx1, dw_in0, dw_out0, dsh_f0, dsc_f0, dgt_f0, dgn_f0, land_2 = ffn_bwd(
        dx2, x1, hf0, act0, ug0, uu0, g_f0, sc_f0, gt_f0, w_in[0], w_out[0], "ffn0", comm=(_ScatterChips(q16_2), q16_2))
    q16_3, own_3 = reduce_begin([dw_in0, dw_out0.reshape(4, f // 4, d)], grp_ffn0, "rs_ffn0")

    ddwo, d_lng, d_lnb, d_bdw, d_bout, dgt_extra = _conv_bwd1(dx1, gt_m0, cw_out, bout_full, dwo, lng_full, lnb_full)
    dcw_out, dgt_m0 = _dw_mm(sw, [dx1], d, d, "conv_dw_out", gate=gt_m0, wfull=cw_out, dgate_init=dgt_extra)
    da, dg, d_wdw, d_bin_a, d_bin_g, land_3 = _dwconv_bwd(ddwo, glu, a_s, g_s, wdw_full, comm=(_ScatterChips(q16_3), q16_3))
    nc = cw_in.shape[2]
    terms = [(da, 0, cw_in, 0), (da, nc, cw_in, 1), (dg, 0, cw_in, 2), (dg, nc, cw_in, 3)]
    dx0, dsh_m0, dsc_m0, dgn_m0 = _mm_normbwd(terms, x0, dx1, g_m0, sc_m0, "conv_bwd_in")
    dcw_in = _dw_mm(h0, [da, dg], d, nc, "conv_dw_in")
    q16_4, own_4 = reduce_begin([dcw_in, dcw_out[0].reshape(4, d // 4, d)], grp_conv, "rs_conv")
    land_4 = _scatter_chips(q16_4, "rs_conv_scatter")

    reduced = _share_halves([_add_chips(own_1, land_1, sel, "rs_ffn1_sum"), _add_chips(own_2, land_2, sel, "rs_attn_sum"),
                             _add_chips(own_3, land_3, sel, "rs_ffn0_sum"), _add_chips(own_4, land_4, sel, "rs_conv_sum")])

    def shard_grads(buf, ws):
        hr, _, offs, _ = layout(ws)
        return [buf[:, o:o + h, :].reshape(w.shape) for w, o, h in zip(ws, offs, hr)]

    g_w_in1, g_w_out1 = shard_grads(reduced[0], grp_ffn1)
    g_kvw, g_wq, g_wo = shard_grads(reduced[1], grp_attn)
    g_w_in0, g_w_out0 = shard_grads(reduced[2], grp_ffn0)
    g_cw_in, g_cw_out = shard_grads(reduced[3], grp_conv)

    d_ada = [jnp.concatenate([dsh_m0, dsc_m0, dgt_m0], axis=1), jnp.concatenate([dsh_m1, dsc_m1, dgt_m1], axis=1),
             jnp.concatenate([dsh_f0, dsc_f0, dgt_f0], axis=1), jnp.concatenate([dsh_f1, dsc_f1, dgt_f1], axis=1),
             jnp.concatenate([dsh_k, dsc_k], axis=1)]
    fields = d_ada + [dgn_m0, dgn_m1, dgn_f0, dgn_f1, dgn_k, d_gfin, d_bin_a, d_bin_g, d_bdw, d_lng, d_lnb, d_bout,
                      d_wdw.reshape(1, -1), dfb]
    foffs = [0]
    for fl in fields:
        foffs.append(foffs[-1] + fl.shape[1])
    n_row = foffs[-1]
    w_row = -(-n_row // (8 * LANE)) * LANE
    row = jnp.pad(jnp.concatenate(fields, axis=1), ((0, 0), (0, 8 * w_row - n_row))).reshape(8, w_row)
    rows_all = _allgather8(row, "ag_small_grads", True).reshape(8, 8, w_row)
    rsum_small = _sum8(rows_all).reshape(1, 8 * w_row)
    rows_flat = rows_all.reshape(8, 8 * w_row)

    def fsum(i):
        return rsum_small[:, foffs[i]:foffs[i + 1]]

    cat = _silu_rows(c_all).T

    def ada_cols(i, n):
        full = rows_flat[:, foffs[i]:foffs[i + 1]].reshape(8, 4, n)
        return lax.dynamic_index_in_dim(full, chip, axis=1, keepdims=False)

    g_mix_ada_w = _ada_wgrad(cat, jnp.stack([ada_cols(0, n3), ada_cols(1, n3)]), "ada_mix_wgrad")
    g_ffn_ada_w = _ada_wgrad(cat, jnp.stack([ada_cols(2, n3), ada_cols(3, n3)]), "ada_ffn_wgrad")
    g_kv_ada_w = _ada_wgrad(cat, ada_cols(4, n2)[None], "ada_kv_wgrad")[0]

    def my_cols(v, n):
        return lax.dynamic_index_in_dim(v.reshape(4, n), chip, axis=0, keepdims=False)

    grads = {
        "mix_norm_g": jnp.concatenate([fsum(5), fsum(6)], axis=0),
        "mix_ada_w": g_mix_ada_w,
        "mix_ada_b": jnp.concatenate([fsum(0), fsum(1)], axis=0),
        "ffn_norm_g": jnp.concatenate([fsum(7), fsum(8)], axis=0),
        "ffn_ada_w": g_ffn_ada_w,
        "ffn_ada_b": jnp.concatenate([fsum(2), fsum(3)], axis=0),
        "ffn_w_in": jnp.stack([g_w_in0, g_w_in1]),
        "ffn_w_out": jnp.stack([g_w_out0, g_w_out1]),
        "conv_w_in": g_cw_in[None],
        "conv_b_in": my_cols(jnp.concatenate([fsum(11), fsum(12)], axis=1), 2 * dq_)[None],
        "conv_w_dw": lax.dynamic_index_in_dim(fsum(17).reshape(HALO, 4, dq_), chip, axis=1, keepdims=False)[:CONV_K][None],
        "conv_b_dw": my_cols(fsum(13), dq_)[None],
        "conv_ln_g": my_cols(fsum(14), dq_)[None],
        "conv_ln_b": my_cols(fsum(15), dq_)[None],
        "conv_w_out": g_cw_out[None],
        "conv_b_out": my_cols(fsum(16), dq_)[None],
        "kv_norm_g": fsum(9).reshape(-1),
        "kv_ada_w": g_kv_ada_w,
        "kv_ada_b": fsum(4).reshape(-1),
        "kv_w": g_kvw,
        "forget_b": fsum(18).reshape(-1)[:N_HEADS],
        "attn_w_q": g_wq[None],
        "attn_w_o": g_wo[None],
        "final_norm_g": fsum(10).reshape(-1),
    }
    weights = dict(mix_norm_g=mix_norm_g, mix_ada_w=mix_ada_w, mix_ada_b=mix_ada_b, ffn_norm_g=ffn_norm_g, ffn_ada_w=ffn_ada_w, ffn_ada_b=ffn_ada_b, ffn_w_in=ffn_w_in, ffn_w_out=ffn_w_out, conv_w_in=conv_w_in, conv_b_in=conv_b_in, conv_w_dw=conv_w_dw, conv_b_dw=conv_b_dw, conv_ln_g=conv_ln_g, conv_ln_b=conv_ln_b, conv_w_out=conv_w_out, conv_b_out=conv_b_out, kv_norm_g=kv_norm_g, kv_ada_w=kv_ada_w, kv_ada_b=kv_ada_b, kv_w=kv_w, forget_b=forget_b, attn_w_q=attn_w_q, attn_w_o=attn_w_o, final_norm_g=final_norm_g)
    moms = dict(mix_norm_g=(m_mix_norm_g, v_mix_norm_g), mix_ada_w=(m_mix_ada_w, v_mix_ada_w), mix_ada_b=(m_mix_ada_b, v_mix_ada_b), ffn_norm_g=(m_ffn_norm_g, v_ffn_norm_g), ffn_ada_w=(m_ffn_ada_w, v_ffn_ada_w), ffn_ada_b=(m_ffn_ada_b, v_ffn_ada_b), ffn_w_in=(m_ffn_w_in, v_ffn_w_in), ffn_w_out=(m_ffn_w_out, v_ffn_w_out), conv_w_in=(m_conv_w_in, v_conv_w_in), conv_b_in=(m_conv_b_in, v_conv_b_in), conv_w_dw=(m_conv_w_dw, v_conv_w_dw), conv_b_dw=(m_conv_b_dw, v_conv_b_dw), conv_ln_g=(m_conv_ln_g, v_conv_ln_g), conv_ln_b=(m_conv_ln_b, v_conv_ln_b), conv_w_out=(m_conv_w_out, v_conv_w_out), conv_b_out=(m_conv_b_out, v_conv_b_out), kv_norm_g=(m_kv_norm_g, v_kv_norm_g), kv_ada_w=(m_kv_ada_w, v_kv_ada_w), kv_ada_b=(m_kv_ada_b, v_kv_ada_b), kv_w=(m_kv_w, v_kv_w), forget_b=(m_forget_b, v_forget_b), attn_w_q=(m_attn_w_q, v_attn_w_q), attn_w_o=(m_attn_w_o, v_attn_w_o), final_norm_g=(m_final_norm_g, v_final_norm_g))
    names = list(weights)

    deltas, new_m, new_v = {}, {}, {}
    small_names = [n for n in names if weights[n].size < (1 << 16)]
    for n in names:
        if n in small_names:
            continue
        w = weights[n]
        w2 = w.reshape(-1, w.shape[-1])
        dl, nm, nv = _adamw(w2, grads[n].reshape(w2.shape), moms[n][0].reshape(w2.shape), moms[n][1].reshape(w2.shape), "adamw_" + n)
        deltas[n], new_m[n], new_v[n] = dl.reshape(w.shape), nm.reshape(w.shape), nv.reshape(w.shape)

    def pack_small(get):
        flat = jnp.concatenate([get(n).reshape(-1) for n in small_names])
        rows_ = -(-flat.shape[0] // (8 * LANE)) * 8
        return jnp.pad(flat, (0, rows_ * LANE - flat.shape[0])).reshape(rows_, LANE)

    ws, gs = pack_small(lambda n: weights[n]), pack_small(lambda n: grads[n])
    ms_, vs_ = pack_small(lambda n: moms[n][0]), pack_small(lambda n: moms[n][1])
    vs_ = jnp.where(jnp.arange(vs_.size).reshape(vs_.shape) < sum(weights[n].size for n in small_names), vs_, 1.0)
    dl, nm, nv = _adamw(ws, gs, ms_, vs_, "adamw_small")
    off = 0
    for n in small_names:
        sz = weights[n].size
        shp = weights[n].shape
        deltas[n] = dl.reshape(-1)[off:off + sz].reshape(shp)
        new_m[n] = nm.reshape(-1)[off:off + sz].reshape(shp)
        new_v[n] = nv.reshape(-1)[off:off + sz].reshape(shp)
        off += sz

    grad_out = [grads[n].reshape(weights[n].shape) for n in names]
    return (loss, dx0[None], *grad_out, *[deltas[n] for n in names], *[new_m[n] for n in names], *[new_v[n] for n in names])
```

```python
import functools

import jax
import jax.numpy as jnp
from jax import lax
from jax.experimental import pallas as pl
from jax.experimental.pallas import tpu as pltpu

F32 = jnp.float32
BF16 = jnp.bfloat16
MESH = pl.DeviceIdType.MESH

EPS = 1e-6
N_HEADS = 16
HEAD_DIM = 64
CONV_K = 31
LANE = 128
SUBLANES = 8
HALO = 32
ATT_TQ = 1024
ATT_TK = 512
NPIECE = 3
SPARE = (HEAD_DIM, 0)
VMEM_MB = 48
ATT_BWD_VMEM_MB = 56

ADAM_LR = 0.001
ADAM_B1 = 0.9
ADAM_B2 = 0.999
ADAM_EPS = 1e-08
ADAM_WD = 0.01
ADAM_STEP = 10


def _sds(shape, dtype):
    return jax.ShapeDtypeStruct(tuple(shape), dtype)


def _cp(sem=None, vmem_mb=VMEM_MB):
    return pltpu.CompilerParams(dimension_semantics=sem, vmem_limit_bytes=vmem_mb << 20)


def _tile(n, pref):
    return pref if n % pref == 0 else n


def _row_tile(r, mult, width=1024):
    cap = max(mult, (512 * 1024 // width) // mult * mult)
    for cand in range(cap, mult - 1, -mult):
        if r % cand == 0:
            return cand
    return r


def _resident(shape):
    nd = len(shape)
    return pl.BlockSpec(tuple(shape), lambda *_: (0,) * nd, pipeline_mode=pl.Buffered(1))


def _dot(a, b):
    return jnp.dot(a, b, preferred_element_type=F32)


def _dot_nt(a, b):
    return lax.dot_general(a, b, (((1,), (1,)), ((), ())), preferred_element_type=F32)


def _dot_tn(a, b):
    return lax.dot_general(a, b, (((0,), (0,)), ((), ())), preferred_element_type=F32)


def _sigmoid(x):
    return 1.0 / (1.0 + jnp.exp(-x))


def _colsum(x):
    return jnp.sum(x, axis=0, keepdims=True)


def _rms_parts(x):
    rstd = lax.rsqrt(jnp.mean(x * x, axis=-1, keepdims=True) + EPS)
    return x * rstd, rstd


class _Gather8:
    def __init__(self, xs):
        self.n = len(xs)
        self.m = [x.shape[0] for x in xs]
        self.land = [_sds((8 * x.shape[0],) + tuple(x.shape[1:]), x.dtype) for x in xs]
        self.sems = [pltpu.SemaphoreType.DMA((7 * self.n,)), pltpu.SemaphoreType.DMA((7 * self.n,)),
                     pltpu.SemaphoreType.DMA((self.n,))]

    def _parts(self, a, x_refs, out_refs, send_sems, recv_sems, local_sems):
        x, y, c = lax.axis_index("x"), lax.axis_index("y"), lax.axis_index("c")
        me, sibling = (x, y, c), (x, y, 1 - c)
        chips = [(1 - x, y), (x, 1 - y), (1 - x, 1 - y)]
        m_per, x_ref, out_ref = self.m[a], x_refs[a], out_refs[a]

        def rows(px, py, pc):
            return out_ref.at[pl.ds((4 * px + 2 * py + pc) * m_per, m_per)]

        def copy(k, block, to, src=None):
            return pltpu.make_async_remote_copy(
                src_ref=rows(*block) if src is None else src, dst_ref=rows(*block),
                send_sem=send_sems.at[7 * a + k], recv_sem=recv_sems.at[7 * a + k], device_id=to, device_id_type=MESH)

        mine = pltpu.make_async_copy(x_ref, rows(*me), local_sems.at[a])
        first = [copy(0, me, sibling, src=x_ref)]
        first += [copy(1 + j, me, (*chip, c), src=x_ref) for j, chip in enumerate(chips)]
        passed = [copy(4 + j, (*chip, c), sibling) for j, chip in enumerate(chips)]
        return c, me, sibling, chips, copy, mine, first, passed

    def start(self, *refs):
        for a in range(self.n):
            _, _, _, _, _, mine, first, _ = self._parts(a, *refs)
            mine.start()
            for cp in first:
                cp.start()

    def finish(self, *refs):
        parts = [self._parts(a, *refs) for a in range(self.n)]
        for j in range(3):
            for c, me, sibling, chips, copy, mine, first, passed in parts:
                copy(1 + j, (*chips[j], c), me).wait_recv()
                passed[j].start()
        for c, me, sibling, chips, copy, mine, first, passed in parts:
            copy(0, sibling, me).wait_recv()
            for j, chip in enumerate(chips):
                copy(4 + j, (*chip, 1 - c), me).wait_recv()
            for cp in first + passed:
                cp.wait_send()
            mine.wait()


class _ScatterChips:
    def __init__(self, qs):
        self.n = len(qs)
        self.land = [_sds((3,) + tuple(q.shape[1:]), q.dtype) for q in qs]
        self.sems = [pltpu.SemaphoreType.DMA((3 * self.n,)), pltpu.SemaphoreType.DMA((3 * self.n,))]

    def _copies(self, q_refs, land_refs, send_sems, recv_sems):
        x, y, c = lax.axis_index("x"), lax.axis_index("y"), lax.axis_index("c")
        chips = [(1 - x, y), (x, 1 - y), (1 - x, 1 - y)]
        return [pltpu.make_async_remote_copy(
            src_ref=q_refs[a].at[2 * cx + cy], dst_ref=land_refs[a].at[k],
            send_sem=send_sems.at[3 * a + k], recv_sem=recv_sems.at[3 * a + k],
            device_id=(cx, cy, c), device_id_type=MESH) for a in range(self.n) for k, (cx, cy) in enumerate(chips)]

    def start(self, *refs):
        for cp in self._copies(*refs):
            cp.start()

    def finish(self, *refs):
        copies = self._copies(*refs)
        for cp in copies:
            cp.wait_recv()
        for cp in copies:
            cp.wait_send()


class _SwapHalves:
    def __init__(self, ps):
        self.n = len(ps)
        self.land = [_sds((p.shape[0],) + tuple(p.shape[2:]), p.dtype) for p in ps]
        self.nb = [p.shape[0] for p in ps]
        tot = sum(self.nb)
        self.sems = [pltpu.SemaphoreType.DMA((tot,)), pltpu.SemaphoreType.DMA((tot,))]

    def _copies(self, p_refs, land_refs, send_sems, recv_sems):
        x, y, c = lax.axis_index("x"), lax.axis_index("y"), lax.axis_index("c")
        out, k = [], 0
        for a in range(self.n):
            for j in range(self.nb[a]):
                out.append(pltpu.make_async_remote_copy(
                    src_ref=p_refs[a].at[j, 1 - c], dst_ref=land_refs[a].at[j], send_sem=send_sems.at[k],
                    recv_sem=recv_sems.at[k], device_id=(x, y, 1 - c), device_id_type=MESH))
                k += 1
        return out

    start = _ScatterChips.start
    finish = _ScatterChips.finish


def _hosted_call(body, comm, *, grid, in_specs, out_specs, out_shape, scratch_shapes, name, sem, args, vmem_mb=VMEM_MB):
    def first():
        return functools.reduce(jnp.logical_and, [pl.program_id(a) == 0 for a in range(len(grid))])

    def last():
        return functools.reduce(jnp.logical_and, [pl.program_id(a) == g - 1 for a, g in enumerate(grid)])

    out_specs = tuple(out_specs) if isinstance(out_specs, (tuple, list)) else (out_specs,)
    out_shape = tuple(out_shape) if isinstance(out_shape, (tuple, list)) else (out_shape,)
    if comm is None:
        return pl.pallas_call(body, grid=grid, in_specs=list(in_specs), out_specs=out_specs, out_shape=out_shape,
                              scratch_shapes=list(scratch_shapes), name=name, compiler_params=_cp(sem, vmem_mb))(*args)
    ex, srcs = comm
    n_in, n_out, n_scr, n_ex = len(in_specs), len(out_shape), len(scratch_shapes), ex.n

    def wrapped(*refs):
        ins, src_refs = refs[:n_in], refs[n_in:n_in + n_ex]
        o0 = n_in + n_ex
        outs, land_refs = refs[o0:o0 + n_out], refs[o0 + n_out:o0 + n_out + n_ex]
        s0 = o0 + n_out + n_ex
        scr, sems = refs[s0:s0 + n_scr], refs[s0 + n_scr:]

        @pl.when(first())
        def _():
            ex.start(src_refs, land_refs, *sems)

        body(*ins, *outs, *scr)

        @pl.when(last())
        def _():
            ex.finish(src_refs, land_refs, *sems)

    hbm = pl.BlockSpec(memory_space=pl.ANY)
    res = pl.pallas_call(
        wrapped, grid=grid, in_specs=[*in_specs, *[hbm] * n_ex], out_specs=(*out_specs, *[hbm] * n_ex),
        out_shape=(*out_shape, *ex.land), scratch_shapes=[*scratch_shapes, *ex.sems], name=name,
        compiler_params=_cp(tuple("arbitrary" for _ in grid), vmem_mb))(*args, *srcs)
    return (*res[:n_out], list(res[n_out:]))


def _exchange(ex, srcs, name, in_vmem=False):
    n = ex.n

    def body(*refs):
        src_refs, land_refs, sems = refs[:n], refs[n:2 * n], refs[2 * n:]
        ex.start(src_refs, land_refs, *sems)
        ex.finish(src_refs, land_refs, *sems)

    spec = pl.BlockSpec(memory_space=pltpu.VMEM if in_vmem else pl.ANY)
    return list(pl.pallas_call(
        body, out_shape=tuple(ex.land), in_specs=[spec] * n, out_specs=tuple([spec] * n),
        scratch_shapes=ex.sems, name=name)(*srcs))


def _allgather8(x_shard, name, in_vmem):
    return _exchange(_Gather8([x_shard]), [x_shard], name, in_vmem)[0]


def _share_halves(bufs):
    n = len(bufs)

    def body(*refs):
        b_refs, out_refs, send_sems, recv_sems = refs[:n], refs[n:2 * n], refs[2 * n], refs[2 * n + 1]
        x, y, c = lax.axis_index("x"), lax.axis_index("y"), lax.axis_index("c")
        copies = [pltpu.make_async_remote_copy(
            src_ref=b_refs[k].at[c], dst_ref=out_refs[k].at[c], send_sem=send_sems.at[k], recv_sem=recv_sems.at[k],
            device_id=(x, y, 1 - c), device_id_type=MESH) for k in range(n)]
        for cp in copies:
            cp.start()
        for cp in copies:
            cp.wait_recv()
        for cp in copies:
            cp.wait_send()

    hbm = pl.BlockSpec(memory_space=pl.ANY)
    return pl.pallas_call(
        body, out_shape=tuple(_sds(b.shape, b.dtype) for b in bufs), in_specs=[hbm] * n, out_specs=tuple([hbm] * n),
        scratch_shapes=[pltpu.SemaphoreType.DMA((n,)), pltpu.SemaphoreType.DMA((n,))],
        input_output_aliases={k: k for k in range(n)}, name="rs_share_halves")(*bufs)


def _add_halves(p, land, sel, name):
    nb, _, r, w = p.shape
    tr = _row_tile(r, 16, w)

    def body(sel_ref, p_ref, l_ref, q16_ref, own_ref):
        q = p_ref[0, 0] + l_ref[0]
        q16_ref[0] = q.astype(BF16)

        @pl.when(pl.program_id(1) == sel_ref[1])
        def _():
            own_ref[...] = q

    gs = pltpu.PrefetchScalarGridSpec(
        num_scalar_prefetch=1, grid=(r // tr, nb),
        in_specs=[pl.BlockSpec((1, 1, tr, w), lambda i, j, sl: (j, sl[0], i, 0)),
                  pl.BlockSpec((1, tr, w), lambda i, j, sl: (j, i, 0))],
        out_specs=(pl.BlockSpec((1, tr, w), lambda i, j, sl: (j, i, 0)), pl.BlockSpec((tr, w), lambda i, j, sl: (i, 0))))
    return pl.pallas_call(body, grid_spec=gs, out_shape=(_sds((nb, r, w), BF16), _sds((r, w), F32)), name=name,
                          compiler_params=_cp(("parallel", "arbitrary")))(sel, p, land)


def _add_chips(own, land, sel, name):
    r, w = own.shape
    tr = _row_tile(r, 16, w)

    def body(sel_ref, q_ref, l_ref, o_ref):
        o_ref[0] = ((q_ref[...] + l_ref[0].astype(F32)) + l_ref[1].astype(F32)) + l_ref[2].astype(F32)

    gs = pltpu.PrefetchScalarGridSpec(
        num_scalar_prefetch=1, grid=(r // tr,),
        in_specs=[pl.BlockSpec((tr, w), lambda i, sl: (i, 0)), pl.BlockSpec((3, tr, w), lambda i, sl: (0, i, 0))],
        out_specs=pl.BlockSpec((1, tr, w), lambda i, sl: (sl[0], i, 0)))
    return pl.pallas_call(body, grid_spec=gs, out_shape=_sds((2, r, w), F32), name=name,
                          compiler_params=_cp(("parallel",)))(sel, own, land)


def _sum8(g):
    _, m, n = g.shape

    def body(g_ref, o_ref):
        acc = g_ref[0]
        for k in range(1, 8):
            acc = acc + g_ref[k]
        o_ref[...] = acc

    return pl.pallas_call(body, out_shape=_sds((m, n), g.dtype), name="sum8")(g)


def _ada_fwd(c_all, w3, name):
    nl, d, n = w3.shape
    tn = 256

    def body(c_ref, w_ref, o_ref):
        cc = c_ref[...]
        ca = (cc * _sigmoid(cc)).astype(BF16)
        o_ref[0] = _dot(ca, w_ref[0].astype(BF16))

    return pl.pallas_call(
        body, grid=(nl, n // tn), out_shape=_sds((nl, 8, n), F32),
        in_specs=[pl.BlockSpec((8, d), lambda l, j: (0, 0)), pl.BlockSpec((1, d, tn), lambda l, j: (l, 0, j))],
        out_specs=pl.BlockSpec((1, 8, tn), lambda l, j: (l, 0, j)),
        name=name, compiler_params=_cp(("parallel", "parallel")))(c_all, w3)


def _in_pair(x, gain, shift, scale, wg, bias, conv, name, comm=None):
    s, d = x.shape
    n = wg.shape[2]
    ts = _tile(s, 512)

    def body(*refs):
        if conv:
            x_ref, g_ref, sh_ref, sc_ref, wa_ref, wb_ref, ba_ref, bb_ref, h_ref, o_ref, sa_ref, sb_ref, hs = refs
        else:
            x_ref, g_ref, sh_ref, sc_ref, wa_ref, wb_ref, h_ref, o_ref, sa_ref, sb_ref, hs = refs

        @pl.when(pl.program_id(1) == 0)
        def _():
            xhat, _ = _rms_parts(x_ref[...])
            h = (xhat * g_ref[...]) * (1.0 + sc_ref[...]) + sh_ref[...]
            hs[...] = h.astype(BF16)
            h_ref[...] = hs[...]

        h = hs[...]
        a = _dot(h, wa_ref[0])
        b = _dot(h, wb_ref[0])
        if conv:
            a = a + ba_ref[0]
            b = b + bb_ref[0]
            o_ref[...] = a * _sigmoid(b)
        else:
            o_ref[...] = (a * _sigmoid(a) * b).astype(BF16)
        sa_ref[...] = a.astype(BF16)
        sb_ref[...] = b.astype(BF16)

    vec = pl.BlockSpec((1, d), lambda i, q: (0, 0))
    in_specs = [pl.BlockSpec((ts, d), lambda i, q: (i, 0)), vec, vec, vec,
                pl.BlockSpec((1, d, n), lambda i, q: (q, 0, 0)), pl.BlockSpec((1, d, n), lambda i, q: (q + 2, 0, 0))]
    args = [x, gain, shift, scale, wg, wg]
    if conv:
        in_specs += [pl.BlockSpec((1, 1, n), lambda i, q: (q, 0, 0)), pl.BlockSpec((1, 1, n), lambda i, q: (q + 2, 0, 0))]
        args += [bias, bias]
    tile = pl.BlockSpec((ts, n), lambda i, q: (i, q))
    return _hosted_call(
        body, comm, grid=(s // ts, 2),
        out_shape=(_sds((s, d), BF16), _sds((s, 2 * n), F32 if conv else BF16), _sds((s, 2 * n), BF16), _sds((s, 2 * n), BF16)),
        in_specs=in_specs, out_specs=(pl.BlockSpec((ts, d), lambda i, q: (i, 0)), tile, tile, tile),
        scratch_shapes=[pltpu.VMEM((ts, d), BF16)], name=name, sem=("parallel", "arbitrary"), args=args)


def _shift_copies(buf, shf):
    n = shf.shape[1]
    for r in range(1, SUBLANES):
        shf[r - 1, :, :] = buf[pl.ds(r, n), :]


def _shifted(buf, shf, start, n, cs):
    a, r = divmod(start, SUBLANES)
    if r == 0:
        return buf[pl.ds(start, n), cs]
    return shf[r - 1, pl.ds(a * SUBLANES, n), cs]


def _dwconv_fwd(glu, wdw, bdw, lng, lnb, comm=None):
    s, d = glu.shape
    ts = _tile(s, 256)
    rb, cb = 32, 256

    def body(cur_ref, halo_ref, w_ref, b_ref, g_ref, be_ref, dwo_ref, sw_ref, buf, shf):
        i = pl.program_id(0)

        @pl.when(i == 0)
        def _():
            buf[pl.ds(0, HALO), :] = jnp.zeros((HALO, d), F32)

        @pl.when(i > 0)
        def _():
            buf[pl.ds(0, HALO), :] = halo_ref[...]

        buf[pl.ds(HALO, ts), :] = cur_ref[...]
        _shift_copies(buf, shf)
        for r in range(ts // rb):
            for cc in range(d // cb):
                cs = pl.ds(cc * cb, cb)
                acc = jnp.zeros((rb, cb), F32) + b_ref[:, cs]
                for k in range(CONV_K):
                    acc = acc + w_ref[pl.ds(k, 1), cs] * _shifted(buf, shf, HALO - (CONV_K - 1) + k + r * rb, rb, cs)
                dwo_ref[pl.ds(r * rb, rb), cs] = acc
            rows = pl.ds(r * rb, rb)
            yv = dwo_ref[rows, :]
            mu = jnp.mean(yv, axis=-1, keepdims=True)
            yc = yv - mu
            var = jnp.mean(yc * yc, axis=-1, keepdims=True)
            ln = yc * lax.rsqrt(var + EPS) * g_ref[...] + be_ref[...]
            sw_ref[rows, :] = (ln * _sigmoid(ln)).astype(BF16)

    vec = pl.BlockSpec((1, d), lambda i: (0, 0))
    return _hosted_call(
        body, comm, grid=(s // ts,), out_shape=(_sds((s, d), F32), _sds((s, d), BF16)),
        in_specs=[pl.BlockSpec((ts, d), lambda i: (i, 0)),
                  pl.BlockSpec((HALO, d), lambda i: (jnp.maximum(i * (ts // HALO) - 1, 0), 0)),
                  pl.BlockSpec((HALO, d), lambda i: (0, 0)), vec, vec, vec],
        out_specs=(pl.BlockSpec((ts, d), lambda i: (i, 0)), pl.BlockSpec((ts, d), lambda i: (i, 0))),
        scratch_shapes=[pltpu.VMEM((HALO + ts, d), F32), pltpu.VMEM((SUBLANES - 1, HALO + ts - SUBLANES, d), F32)],
        name="dwconv_fwd", sem=("parallel",),
        args=(glu, glu, wdw, bdw, lng, lnb))


def _mm_res(a, w, b, gate, x, name):
    s, k = a.shape
    d = w.shape[1]
    ts = _tile(s, 512)

    def body(a_ref, w_ref, b_ref, g_ref, x_ref, o_ref):
        yv = _dot(a_ref[...], w_ref[...]) + b_ref[...]
        o_ref[...] = x_ref[...] + g_ref[...] * yv

    vec = pl.BlockSpec((1, d), lambda i: (0, 0))
    return pl.pallas_call(
        body, grid=(s // ts,), out_shape=_sds((s, d), F32),
        in_specs=[pl.BlockSpec((ts, k), lambda i: (i, 0)), _resident((k, d)), vec, vec, pl.BlockSpec((ts, d), lambda i: (i, 0))],
        out_specs=pl.BlockSpec((ts, d), lambda i: (i, 0)),
        name=name, compiler_params=_cp(("parallel",)))(a, w, b, gate, x)


def _qkv(x, kvp, mxp, wk, wv, wf, wq):
    s, d = x.shape
    ts = _tile(s, 512)
    qscale = HEAD_DIM ** -0.5

    def body(x_ref, gk, shk, sck, gm, shm, scm, wk_ref, wv_ref, wf_ref, wq_ref, hk_ref, h1_ref, k_ref, v_ref, q_ref, f_ref):
        xhat, _ = _rms_parts(x_ref[...])
        hk = ((xhat * gk[...]) * (1.0 + sck[...]) + shk[...]).astype(BF16)
        h1 = ((xhat * gm[...]) * (1.0 + scm[...]) + shm[...]).astype(BF16)
        hk_ref[...] = hk
        h1_ref[...] = h1
        k_ref[...] = _dot(hk, wk_ref[...]).astype(BF16)
        v_ref[...] = _dot(hk, wv_ref[...]).astype(BF16)
        f_ref[...] = _dot(hk, wf_ref[...])
        q_ref[...] = (_dot(h1, wq_ref[...]) * qscale).astype(BF16)

    vec = pl.BlockSpec((1, d), lambda i: (0, 0))
    row = pl.BlockSpec((ts, d), lambda i: (i, 0))
    return pl.pallas_call(
        body, grid=(s // ts,),
        out_shape=tuple(_sds((s, d), BF16) for _ in range(5)) + (_sds((s, LANE), F32),),
        in_specs=[row, vec, vec, vec, vec, vec, vec, _resident((d, d)), _resident((d, d)), _resident((d, LANE)), _resident((d, d))],
        out_specs=(row, row, row, row, row, pl.BlockSpec((ts, LANE), lambda i: (i, 0))),
        name="qkv_proj", compiler_params=_cp(("parallel",)))(x, *kvp, *mxp, wk, wv, wf, wq)


def _log_sigmoid(z):
    return jnp.minimum(z, 0.0) - jnp.log(1.0 + jnp.exp(-jnp.abs(z)))


def _cumsum_fwd(flog, fb):
    s = flog.shape[0]
    ts = _tile(s, 256)

    def body(f_ref, b_ref, cum_ref, cumt_ref, carry):
        @pl.when(pl.program_id(0) == 0)
        def _():
            carry[...] = jnp.zeros_like(carry)

        ls = _log_sigmoid(f_ref[...] + b_ref[...])
        r = lax.broadcasted_iota(jnp.int32, (ts, ts), 0)
        cidx = lax.broadcasted_iota(jnp.int32, (ts, ts), 1)
        tri = (cidx <= r).astype(F32)
        cs = jnp.dot(tri, ls, preferred_element_type=F32, precision=lax.Precision.HIGHEST) + carry[...]
        cum_ref[...] = cs
        cumt_ref[...] = cs.T
        carry[...] = cs[ts - 1:ts, :]

    return pl.pallas_call(
        body, grid=(s // ts,), out_shape=(_sds((s, LANE), F32), _sds((LANE, s), F32)),
        in_specs=[pl.BlockSpec((ts, LANE), lambda i: (i, 0)), pl.BlockSpec((1, LANE), lambda i: (0, 0))],
        out_specs=(pl.BlockSpec((ts, LANE), lambda i: (i, 0)), pl.BlockSpec((LANE, ts), lambda i: (0, i))),
        scratch_shapes=[pltpu.VMEM((1, LANE), F32)],
        name="forget_cumsum", compiler_params=_cp(("arbitrary",)))(flog, fb)


def _pick_row(m, idx):
    r = lax.broadcasted_iota(jnp.int32, (m.shape[0], 1), 0)
    return jnp.sum(jnp.where(r == idx, m, 0.0), axis=0, keepdims=True)


def _pick_col(m, idx):
    cidx = lax.broadcasted_iota(jnp.int32, (1, m.shape[1]), 1)
    return jnp.sum(jnp.where(cidx == idx, m, 0.0), axis=1, keepdims=True)


def _split3(x):
    hi = x.astype(BF16)
    r1 = x - hi.astype(F32)
    mid = r1.astype(BF16)
    lo = (r1 - mid.astype(F32)).astype(BF16)
    return hi, mid, lo


def _head_mask(lane, hh):
    lo = lane < HEAD_DIM
    return lo if hh == 0 else jnp.logical_not(lo)


def _attn_prep(k, v, cum):
    s, d = k.shape
    npair = d // LANE
    tc = _tile(s, 512)

    def body(k_ref, v_ref, c_ref, ka_ref, kt_ref, vt_ref):
        p = pl.program_id(0)
        lane = lax.broadcasted_iota(jnp.int32, (1, LANE), 1)
        kk = k_ref[...]
        vv = v_ref[...].astype(F32)
        ckt = c_ref[...]
        for hh in range(2):
            head = _head_mask(lane, hh)
            b = SPARE[hh]
            ck = _pick_col(ckt, 2 * p + hh)
            extra = jnp.where(lane == b + NPIECE, 1.0, 0.0).astype(BF16) + jnp.zeros((tc, LANE), BF16)
            for n_, pc in enumerate(_split3(ck)):
                extra = jnp.where(lane == b + n_, pc, extra)
            ka = jnp.where(head, kk, extra)
            ka_ref[0, hh] = ka
            kt_ref[0, hh] = ka.astype(F32).T.astype(BF16)
            vx = jnp.where(head, vv, jnp.where(lane == b, 1.0, 0.0))
            vt_ref[0, hh] = vx.T.astype(BF16)

    blk = pl.BlockSpec((tc, LANE), lambda p, c: (c, p))
    return pl.pallas_call(
        body, grid=(npair, s // tc),
        out_shape=(_sds((npair, 2, s, LANE), BF16), _sds((npair, 2, LANE, s), BF16), _sds((npair, 2, LANE, s), BF16)),
        in_specs=[blk, blk, pl.BlockSpec((tc, LANE), lambda p, c: (c, 0))],
        out_specs=(pl.BlockSpec((1, 2, tc, LANE), lambda p, c: (p, 0, c, 0)),
                   pl.BlockSpec((1, 2, LANE, tc), lambda p, c: (p, 0, 0, c)),
                   pl.BlockSpec((1, 2, LANE, tc), lambda p, c: (p, 0, 0, c))),
        name="fox_attn_prep", compiler_params=_cp(("parallel", "parallel")))(k, v, cum)


def _q_aug(qq, lane, hh):
    b = SPARE[hh]
    sel = jnp.logical_and(lane >= b, lane < b + NPIECE)
    neg = jnp.full((1, LANE), -1.0, BF16)
    zl = jnp.zeros((1, LANE), BF16)
    return jnp.where(_head_mask(lane, hh), qq, jnp.where(sel, neg, zl))


def _attn_fwd(q, kaug, vtr, cumt):
    s, d = q.shape
    tq = _tile(s, ATT_TQ)
    tk = _tile(s, ATT_TK)
    npair = d // LANE
    npart = max(1, tq // tk)

    def body(q_ref, ka_ref, vt_ref, cumt_ref, o_ref, lse_ref):
        p = pl.program_id(0)
        i = pl.program_id(1)
        lane = lax.broadcasted_iota(jnp.int32, (1, LANE), 1)
        qq = q_ref[...]
        qx = (_q_aug(qq, lane, 0), _q_aug(qq, lane, 1))
        cqt = cumt_ref[:, pl.ds(pl.multiple_of(i * tq, tq), tq)]
        cq = (_pick_row(cqt, 2 * p), _pick_row(cqt, 2 * p + 1))
        jd = (i * tq) // tk

        def kv_step(j, carry, diag, q_lo=0):
            ks = pl.multiple_of(j * tk, tk)
            nq_ = tq - q_lo
            if diag:
                krow = lax.broadcasted_iota(jnp.int32, (tk, nq_), 0) + j * tk
                qcol = lax.broadcasted_iota(jnp.int32, (tk, nq_), 1) + (i * tq + q_lo)
                causal = krow <= qcol
            out = []
            for hh in range(2):
                m_all, acc_all = carry[2 * hh], carry[2 * hh + 1]
                m, acc, cqh = m_all[:, q_lo:], acc_all[:, q_lo:], cq[hh][:, q_lo:]
                sc = _dot_nt(ka_ref[0, hh, pl.ds(ks, tk), :], qx[hh][q_lo:, :])
                if diag:
                    sc = jnp.where(causal, sc, -jnp.inf)
                mx = jnp.max(sc, axis=0, keepdims=True) + cqh
                mn = jnp.maximum(m, mx)
                alpha = jnp.exp(m - mn)
                pt = jnp.exp(sc + (cqh - mn)).astype(BF16)
                acc = alpha * acc + _dot(vt_ref[0, hh, :, pl.ds(ks, tk)], pt)
                if q_lo:
                    mn = jnp.concatenate([m_all[:, :q_lo], mn], axis=1)
                    acc = jnp.concatenate([acc_all[:, :q_lo], acc], axis=1)
                out += [mn, acc]
            return tuple(out)

        minit = jnp.full((1, tq), -jnp.inf, F32)
        ainit = jnp.zeros((LANE, tq), F32)
        carry = (minit, ainit, minit, ainit)
        for pj in range(npart):
            carry = kv_step(jd + pj, carry, True, q_lo=pj * tk)
        carry = lax.fori_loop(0, jd, lambda j, cr: kv_step(j, cr, False), carry)
        m0, a0, m1, a1 = carry
        l0 = a0[SPARE[0]:SPARE[0] + 1, :]
        l1 = a1[SPARE[1]:SPARE[1] + 1, :]
        row = lax.broadcasted_iota(jnp.int32, (LANE, 1), 0)
        ot = jnp.where(row < HEAD_DIM, a0 / l0, a1 / l1)
        o_ref[...] = ot.T.astype(BF16)
        r8 = lax.broadcasted_iota(jnp.int32, (8, 1), 0)
        lse_ref[0] = jnp.where(r8 == 0, m0 + jnp.log(l0), jnp.where(r8 == 1, m1 + jnp.log(l1), 0.0))

    return pl.pallas_call(
        body, grid=(npair, s // tq), out_shape=(_sds((s, d), BF16), _sds((npair, 8, s), F32)),
        in_specs=[pl.BlockSpec((tq, LANE), lambda p, i: (i, p)),
                  pl.BlockSpec((1, 2, s, LANE), lambda p, i: (p, 0, 0, 0)),
                  pl.BlockSpec((1, 2, LANE, s), lambda p, i: (p, 0, 0, 0)),
                  pl.BlockSpec((N_HEADS, s), lambda p, i: (0, 0))],
        out_specs=(pl.BlockSpec((tq, LANE), lambda p, i: (i, p)), pl.BlockSpec((1, 8, tq), lambda p, i: (p, 0, i))),
        name="fox_attn_fwd", compiler_params=_cp(("parallel", "parallel")))(q, kaug, vtr, cumt)


def _final(x, gain, target):
    s, d = x.shape
    ts = _tile(s, 512)

    def body(x_ref, g_ref, t_ref, lsum_ref, dx_ref, dg_ref):
        @pl.when(pl.program_id(0) == 0)
        def _():
            lsum_ref[...] = jnp.zeros_like(lsum_ref)
            dg_ref[...] = jnp.zeros_like(dg_ref)

        xhat, rstd = _rms_parts(x_ref[...])
        e = xhat * g_ref[...] - t_ref[...]
        lsum_ref[...] += _colsum(e * e)
        dout = e * (1.0 / d)
        dg_ref[...] += _colsum(dout * xhat)
        dxhat = dout * g_ref[...]
        dx_ref[...] = rstd * (dxhat - xhat * jnp.mean(dxhat * xhat, axis=-1, keepdims=True))

    vec = pl.BlockSpec((1, d), lambda i: (0, 0))
    row = pl.BlockSpec((ts, d), lambda i: (i, 0))
    return pl.pallas_call(
        body, grid=(s // ts,), out_shape=(_sds((1, d), F32), _sds((s, d), F32), _sds((1, d), F32)),
        in_specs=[row, vec, row], out_specs=(vec, row, vec),
        name="final_norm_loss", compiler_params=_cp(("arbitrary",)))(x, gain, target)


def _ffn_bwd_act(dx, gate, w_out, ug, uu, name, comm=None):
    s, d = dx.shape
    f = w_out.shape[0]
    n = f // 2
    ts = _tile(s, 512)

    def body(dx_ref, g_ref, w_ref, ug_ref, uu_ref, dug_ref, duu_ref, dys):
        @pl.when(pl.program_id(1) == 0)
        def _():
            dys[...] = (dx_ref[...] * g_ref[...]).astype(BF16)

        dact = _dot_nt(dys[...], w_ref[...])
        g = ug_ref[...].astype(F32)
        u = uu_ref[...].astype(F32)
        sg = _sigmoid(g)
        dug_ref[...] = (dact * u * sg * (1.0 + g * (1.0 - sg))).astype(BF16)
        duu_ref[...] = (dact * g * sg).astype(BF16)

    tile = pl.BlockSpec((ts, n), lambda i, q: (i, q))
    return _hosted_call(
        body, comm, grid=(s // ts, 2), out_shape=(_sds((s, f), BF16), _sds((s, f), BF16)),
        in_specs=[pl.BlockSpec((ts, d), lambda i, q: (i, 0)), pl.BlockSpec((1, d), lambda i, q: (0, 0)),
                  pl.BlockSpec((n, d), lambda i, q: (q, 0)), tile, tile],
        out_specs=(tile, tile), scratch_shapes=[pltpu.VMEM((ts, d), BF16)],
        name=name, sem=("parallel", "arbitrary"), args=(dx, gate, w_out, ug, uu))


def _dw_mm(a, b_list, tk, tn, name, gate=None, wfull=None, dgate_init=None):
    s, kdim = a.shape
    nb1 = b_list[0].shape[1] // tn
    nb = nb1 * len(b_list)
    ts = _tile(s, 512)
    nk = kdim // tk
    ns = s // ts
    gated = gate is not None

    def body(*refs):
        a_ref = refs[0]
        b_refs = refs[1:1 + len(b_list)]
        rest = refs[1 + len(b_list):]
        if gated:
            g_ref, w_ref, di_ref, o_ref, dg_ref, acc = rest
        else:
            o_ref, acc = rest
        jn, ik, st = pl.program_id(0), pl.program_id(1), pl.program_id(2)

        @pl.when(st == 0)
        def _():
            acc[...] = jnp.zeros_like(acc)

        for mi, b_ref in enumerate(b_refs):
            @pl.when(jn // nb1 == mi)
            def _(b_ref=b_ref):
                acc[...] += _dot_tn(a_ref[...], b_ref[...].astype(BF16))

        if gated:
            @pl.when(jnp.logical_and(ik == 0, st == 0))
            def _():
                dg_ref[...] = di_ref[...]

        @pl.when(st == ns - 1)
        def _():
            if gated:
                o_ref[0] = acc[...] * g_ref[...]
                dg_ref[...] += _colsum(acc[...] * w_ref[...].astype(F32))
            else:
                o_ref[0] = acc[...]

    in_specs = [pl.BlockSpec((ts, tk), lambda jn, ik, st: (st, ik))]
    for mi in range(len(b_list)):
        in_specs.append(pl.BlockSpec(
            (ts, tn), lambda jn, ik, st, mi=mi: (st, jnp.clip(jn - mi * nb1, 0, nb1 - 1))))
    args = [a] + list(b_list)
    out_shape = [_sds((nb, kdim, tn), F32)]
    out_specs = [pl.BlockSpec((1, tk, tn), lambda jn, ik, st: (jn, ik, 0))]
    if gated:
        vec = pl.BlockSpec((1, tn), lambda jn, ik, st: (0, jn))
        in_specs += [vec, pl.BlockSpec((tk, tn), lambda jn, ik, st: (ik, jn)), vec]
        args += [gate, wfull, dgate_init]
        out_shape.append(_sds((1, nb * tn), F32))
        out_specs.append(vec)
    res = pl.pallas_call(
        body, grid=(nb, nk, ns), out_shape=tuple(out_shape), in_specs=in_specs, out_specs=tuple(out_specs),
        scratch_shapes=[pltpu.VMEM((tk, tn), F32)],
        name=name, compiler_params=_cp(("parallel", "arbitrary", "arbitrary")))(*args)
    return res if gated else res[0]


def _mm_normbwd(terms, x, dxres, gain, scale, name, ts_pref=256, comm=None):
    s, d = x.shape
    ts = _tile(s, ts_pref)
    arrs, warrs = [], []
    for a, _, w, _ in terms:
        if not any(a is z for z in arrs):
            arrs.append(a)
        if not any(w is z for z in warrs):
            warrs.append(w)
    ai = [next(i for i, z in enumerate(arrs) if z is a) for a, _, _, _ in terms]
    wi = [next(i for i, z in enumerate(warrs) if z is w) for _, _, w, _ in terms]

    def body(*refs):
        a_refs = refs[:len(arrs)]
        w_refs = refs[len(arrs):len(arrs) + len(warrs)]
        x_ref, dr_ref, g_ref, sc_ref, dx_ref, dsh_ref, dsc_ref, dg_ref = refs[len(arrs) + len(warrs):]

        @pl.when(pl.program_id(0) == 0)
        def _():
            dsh_ref[...] = jnp.zeros_like(dsh_ref)
            dsc_ref[...] = jnp.zeros_like(dsc_ref)
            dg_ref[...] = jnp.zeros_like(dg_ref)

        dh = None
        for ti, (_, c0, w, q) in enumerate(terms):
            n = w.shape[2]
            part = _dot_nt(a_refs[ai[ti]][:, pl.ds(c0, n)], w_refs[wi[ti]][q])
            dh = part if dh is None else dh + part
        xhat, rstd = _rms_parts(x_ref[...])
        nrm = xhat * g_ref[...]
        dsh_ref[...] += _colsum(dh)
        dsc_ref[...] += _colsum(dh * nrm)
        dn = dh * (1.0 + sc_ref[...])
        dg_ref[...] += _colsum(dn * xhat)
        dxhat = dn * g_ref[...]
        dx_ref[...] = dr_ref[...] + rstd * (dxhat - xhat * jnp.mean(dxhat * xhat, axis=-1, keepdims=True))

    vec = pl.BlockSpec((1, d), lambda i: (0, 0))
    row = pl.BlockSpec((ts, d), lambda i: (i, 0))
    in_specs = [pl.BlockSpec((ts, a.shape[1]), lambda i: (i, 0)) for a in arrs]
    in_specs += [_resident(w.shape) for w in warrs]
    in_specs += [row, row, vec, vec]
    return _hosted_call(
        body, comm, grid=(s // ts,), out_shape=(_sds((s, d), F32), _sds((1, d), F32), _sds((1, d), F32), _sds((1, d), F32)),
        in_specs=in_specs, out_specs=(row, vec, vec, vec), scratch_shapes=[],
        name=name, sem=("arbitrary",), args=(*arrs, *warrs, x, dxres, gain, scale))


def _do_kernel(dx, gate, wo, o):
    s, d = dx.shape
    ts = _tile(s, 512)

    def body(dx_ref, g_ref, w_ref, o_ref, do_ref, dl_ref):
        dy = (dx_ref[...] * g_ref[...]).astype(BF16)
        do = _dot_nt(dy, w_ref[...])
        do_ref[...] = do.astype(BF16)
        prod = do * o_ref[...].astype(F32)
        hrow = lax.broadcasted_iota(jnp.int32, (N_HEADS, d), 0)
        hcol = lax.broadcasted_iota(jnp.int32, (N_HEADS, d), 1) // HEAD_DIM
        sel = (hrow == hcol).astype(F32)
        dl_ref[...] = lax.dot_general(sel, prod, (((1,), (1,)), ((), ())), preferred_element_type=F32,
                                      precision=lax.Precision.HIGHEST)

    row = pl.BlockSpec((ts, d), lambda i: (i, 0))
    return pl.pallas_call(
        body, grid=(s // ts,), out_shape=(_sds((s, d), BF16), _sds((N_HEADS, s), F32)),
        in_specs=[row, pl.BlockSpec((1, d), lambda i: (0, 0)), _resident(wo.shape), row],
        out_specs=(row, pl.BlockSpec((N_HEADS, ts), lambda i: (0, i))),
        name="attn_do", compiler_params=_cp(("parallel",)))(dx, gate, wo, o)


def _attn_bwd(q, do, kaug, kaugt, v, cumt, lse, deltat, comm=None):
    s, d = q.shape
    tq = _tile(s, ATT_TQ)
    tk = _tile(s, ATT_TK)
    assert tq in (tk, 2 * tk)
    npair = d // LANE
    nq = s // tq
    nkb = s // tk
    qscale = HEAD_DIM ** -0.5

    def body(q_ref, do_ref, ka_ref, kt_ref, v_ref, cumt_ref, lse_ref, dl_ref,
             dq_ref, dk_ref, dv_ref, dcq_ref, dck_ref, qaug, dom, rowv, dqt):
        p = pl.program_id(0)
        j = pl.program_id(1)
        lane = lax.broadcasted_iota(jnp.int32, (1, LANE), 1)
        lo = lane < HEAD_DIM
        r8 = lax.broadcasted_iota(jnp.int32, (8, 1), 0)

        @pl.when(j == 0)
        def _():
            dqt[...] = jnp.zeros_like(dqt)
            for c in range(nq):
                rows = pl.ds(c * tq, tq)
                qq = q_ref[rows, :]
                dd = do_ref[rows, :]
                cqt = cumt_ref[:, rows]
                dlt = dl_ref[:, rows]
                lst = lse_ref[0, :, rows]
                for hh in range(2):
                    qaug[hh, rows, :] = _q_aug(qq, lane, hh)
                    dom[hh, rows, :] = jnp.where(_head_mask(lane, hh), dd, jnp.zeros_like(dd))
                    rowv[hh, :, rows] = jnp.where(
                        r8 == 0, _pick_row(cqt, 2 * p + hh) - lst[hh:hh + 1, :],
                        jnp.where(r8 == 1, _pick_row(dlt, 2 * p + hh), 0.0))

        vv = v_ref[...]
        i0 = (j * tk) // tq

        def q_step(qs, nq_, carry, diag):
            dv_acc, dk0, dk1 = carry
            qs = pl.multiple_of(qs, tk)
            if diag:
                krow = lax.broadcasted_iota(jnp.int32, (tk, nq_), 0) + j * tk
                qcol = lax.broadcasted_iota(jnp.int32, (tk, nq_), 1) + qs
                causal = krow <= qcol
            dks = [dk0, dk1]
            for hh in range(2):
                rv = rowv[hh, :, pl.ds(qs, nq_)]
                qa = qaug[hh, pl.ds(qs, nq_), :]
                dh = dom[hh, pl.ds(qs, nq_), :]
                sc = _dot_nt(ka_ref[0, hh], qa)
                if diag:
                    sc = jnp.where(causal, sc, -jnp.inf)
                pt = jnp.exp(sc + rv[0:1, :])
                dpt = _dot_nt(vv, dh)
                dst = (pt * (dpt - rv[1:2, :])).astype(BF16)
                dv_acc = dv_acc + _dot(pt.astype(BF16), dh)
                dks[hh] = dks[hh] + _dot(dst, qa)
                dqt[hh, :, pl.ds(qs, nq_)] += _dot(kt_ref[0, hh], dst)
            return dv_acc, dks[0], dks[1]

        z = jnp.zeros((tk, LANE), F32)
        first = ((j * tk) % tq == 0).astype(jnp.int32)
        carry = lax.fori_loop(0, first, lambda _, cr: q_step(i0 * tq, tq, cr, True), (z, z, z))
        if tq > tk:
            carry = lax.fori_loop(0, 1 - first, lambda _, cr: q_step(j * tk, tq - tk, cr, True), carry)
        dv_acc, dk0, dk1 = lax.fori_loop(i0 + 1, nq, lambda i, cr: q_step(i * tq, tq, cr, False), carry)
        dv_ref[...] = dv_acc.astype(BF16)
        dk_ref[...] = jnp.where(lo, dk0, dk1).astype(BF16)
        dck_ref[0] = jnp.where(r8 == 0, dk0.T[SPARE[0]:SPARE[0] + 1, :],
                               jnp.where(r8 == 1, dk1.T[SPARE[1]:SPARE[1] + 1, :], 0.0))

        @pl.when(j == nkb - 1)
        def _():
            for c in range(nq):
                rows = pl.ds(c * tq, tq)
                a0 = dqt[0, :, rows].T
                a1 = dqt[1, :, rows].T
                dq_ref[rows, :] = (jnp.where(lo, a0, a1) * qscale).astype(BF16)
            r0, r1 = SPARE[0] + NPIECE, SPARE[1] + NPIECE
            dcq_ref[0] = jnp.where(r8 == 0, dqt[0, r0:r0 + 1, :], jnp.where(r8 == 1, dqt[1, r1:r1 + 1, :], 0.0))

    col = pl.BlockSpec((s, LANE), lambda p, j: (0, p), pipeline_mode=pl.Buffered(1))
    rows16 = pl.BlockSpec((N_HEADS, s), lambda p, j: (0, 0), pipeline_mode=pl.Buffered(1))
    blk = pl.BlockSpec((tk, LANE), lambda p, j: (j, p))
    return _hosted_call(
        body, comm, grid=(npair, nkb),
        out_shape=(_sds((s, d), BF16), _sds((s, d), BF16), _sds((s, d), BF16), _sds((npair, 8, s), F32), _sds((npair, 8, s), F32)),
        in_specs=[col, col, pl.BlockSpec((1, 2, tk, LANE), lambda p, j: (p, 0, j, 0)),
                  pl.BlockSpec((1, 2, LANE, tk), lambda p, j: (p, 0, 0, j)), blk, rows16,
                  pl.BlockSpec((1, 8, s), lambda p, j: (p, 0, 0), pipeline_mode=pl.Buffered(1)), rows16],
        out_specs=(pl.BlockSpec((s, LANE), lambda p, j: (0, p)), blk, blk,
                   pl.BlockSpec((1, 8, s), lambda p, j: (p, 0, 0)), pl.BlockSpec((1, 8, tk), lambda p, j: (p, 0, j))),
        scratch_shapes=[pltpu.VMEM((2, s, LANE), BF16), pltpu.VMEM((2, s, LANE), BF16), pltpu.VMEM((2, 8, s), F32),
                        pltpu.VMEM((2, LANE, s), F32)],
        name="fox_attn_bwd", sem=("arbitrary", "arbitrary"), vmem_mb=ATT_BWD_VMEM_MB,
        args=(q, do, kaug, kaugt, v, cumt, lse, deltat))


def _cumsum_bwd(dcq, dck, flog, fb):
    s = flog.shape[0]
    ts = _tile(s, 256)
    nt = s // ts

    def body(dq_ref, dk_ref, f_ref, b_ref, df_ref, db_ref, carry):
        @pl.when(pl.program_id(0) == 0)
        def _():
            carry[...] = jnp.zeros_like(carry)
            db_ref[...] = jnp.zeros_like(db_ref)

        r = lax.broadcasted_iota(jnp.int32, (ts, ts), 0)
        cidx = lax.broadcasted_iota(jnp.int32, (ts, ts), 1)
        tri = (r >= cidx).astype(F32)
        dct = dq_ref[...] + dk_ref[...]
        dlst = jnp.dot(dct, tri, preferred_element_type=F32, precision=lax.Precision.HIGHEST) + carry[...]
        carry[...] = dlst[:, 0:1]
        dls = jnp.concatenate([dlst, jnp.zeros((LANE - N_HEADS, ts), F32)], axis=0).T
        z = f_ref[...] + b_ref[...]
        df = dls * (1.0 / (1.0 + jnp.exp(z)))
        db_ref[...] += _colsum(df)
        df_ref[...] = df.astype(BF16)

    rev = pl.BlockSpec((ts, LANE), lambda i: (nt - 1 - i, 0))
    revt = pl.BlockSpec((N_HEADS, ts), lambda i: (0, nt - 1 - i))
    vec = pl.BlockSpec((1, LANE), lambda i: (0, 0))
    return pl.pallas_call(
        body, grid=(nt,), out_shape=(_sds((s, LANE), BF16), _sds((1, LANE), F32)),
        in_specs=[revt, revt, rev, vec], out_specs=(rev, vec), scratch_shapes=[pltpu.VMEM((N_HEADS, 1), F32)],
        name="forget_cumsum_bwd", compiler_params=_cp(("arbitrary",)))(dcq, dck, flog, fb)


def _conv_bwd1(dx, gate, w_out, b_out, dwo, lng, lnb):
    s, d = dx.shape
    ts = _tile(s, 512)
    ns = s // ts

    def body(dx_ref, g_ref, w_ref, bo_ref, y_ref, lg_ref, lb_ref, dd_ref, dlg_ref, dlb_ref, dbd_ref, dbo_ref, dge_ref, cs):
        i = pl.program_id(0)

        @pl.when(i == 0)
        def _():
            for r in (dlg_ref, dlb_ref, dbd_ref, cs):
                r[...] = jnp.zeros_like(r)

        dxv = dx_ref[...]
        cs[...] += _colsum(dxv)
        dsw = _dot_nt((dxv * g_ref[...]).astype(BF16), w_ref[...])
        yv = y_ref[...]
        mu = jnp.mean(yv, axis=-1, keepdims=True)
        yc = yv - mu
        rstd = lax.rsqrt(jnp.mean(yc * yc, axis=-1, keepdims=True) + EPS)
        xhat = yc * rstd
        ln = xhat * lg_ref[...] + lb_ref[...]
        sg = _sigmoid(ln)
        dln = dsw * (sg * (1.0 + ln * (1.0 - sg)))
        dlg_ref[...] += _colsum(dln * xhat)
        dlb_ref[...] += _colsum(dln)
        dxh = dln * lg_ref[...]
        dd = rstd * (dxh - jnp.mean(dxh, axis=-1, keepdims=True) - xhat * jnp.mean(dxh * xhat, axis=-1, keepdims=True))
        dbd_ref[...] += _colsum(dd)
        dd_ref[...] = dd

        @pl.when(i == ns - 1)
        def _():
            dbo_ref[...] = g_ref[...] * cs[...]
            dge_ref[...] = bo_ref[...] * cs[...]

    vec = pl.BlockSpec((1, d), lambda i: (0, 0))
    row = pl.BlockSpec((ts, d), lambda i: (i, 0))
    return pl.pallas_call(
        body, grid=(ns,), out_shape=(_sds((s, d), F32),) + tuple(_sds((1, d), F32) for _ in range(5)),
        in_specs=[row, vec, _resident(w_out.shape), vec, row, vec, vec], out_specs=(row, vec, vec, vec, vec, vec),
        scratch_shapes=[pltpu.VMEM((1, d), F32)],
        name="conv_bwd_ln", compiler_params=_cp(("arbitrary",)))(dx, gate, w_out, b_out, dwo, lng, lnb)


def _dwconv_bwd(ddwo, glu, a_s, g_s, wdw, comm=None):
    s, d = ddwo.shape
    ts = _tile(s, 256)
    ns = s // ts
    rb, cb = 32, 256
    nrb = ts // rb

    def body(dd_ref, ddn_ref, gl_ref, glh_ref, a_ref, g_ref, w_ref, da_ref, dg_ref, dw_ref, sa_ref, sg_ref, bufd, bufg, dws,
             shd, shg):
        i = pl.program_id(0)

        @pl.when(i == 0)
        def _():
            dws[...] = jnp.zeros_like(dws)
            sa_ref[...] = jnp.zeros_like(sa_ref)
            sg_ref[...] = jnp.zeros_like(sg_ref)
            bufg[pl.ds(0, HALO), :] = jnp.zeros((HALO, d), F32)

        @pl.when(i > 0)
        def _():
            bufg[pl.ds(0, HALO), :] = glh_ref[...]

        bufg[pl.ds(HALO, ts), :] = gl_ref[...]
        bufd[pl.ds(0, ts), :] = dd_ref[...]

        @pl.when(i == ns - 1)
        def _():
            bufd[pl.ds(ts, HALO), :] = jnp.zeros((HALO, d), F32)

        @pl.when(i < ns - 1)
        def _():
            bufd[pl.ds(ts, HALO), :] = ddn_ref[...]

        _shift_copies(bufd, shd)
        _shift_copies(bufg, shg)
        for cc in range(d // cb):
            cs = pl.ds(cc * cb, cb)
            for r in range(nrb):
                acc = jnp.zeros((rb, cb), F32)
                for k in range(CONV_K):
                    acc = acc + w_ref[pl.ds(k, 1), cs] * _shifted(bufd, shd, r * rb + (CONV_K - 1) - k, rb, cs)
                rows = pl.ds(r * rb, rb)
                av = a_ref[rows, cs].astype(F32)
                sg = _sigmoid(g_ref[rows, cs].astype(F32))
                dav = acc * sg
                dgv = acc * av * sg * (1.0 - sg)
                da_ref[rows, cs] = dav.astype(BF16)
                dg_ref[rows, cs] = dgv.astype(BF16)
                sa_ref[:, cs] += _colsum(dav)
                sg_ref[:, cs] += _colsum(dgv)
            for k in range(CONV_K):
                acc8 = jnp.zeros((8, cb), F32)
                for r in range(nrb):
                    prod = bufd[pl.ds(r * rb, rb), cs] * _shifted(bufg, shg, HALO - (CONV_K - 1) + k + r * rb, rb, cs)
                    acc8 = acc8 + (prod[0:8] + prod[8:16]) + (prod[16:24] + prod[24:32])
                dws[pl.ds(8 * k, 8), cs] += acc8

        @pl.when(i == ns - 1)
        def _():
            dw_ref[...] = jnp.zeros_like(dw_ref)
            for k in range(CONV_K):
                dw_ref[pl.ds(k, 1), :] = _colsum(dws[pl.ds(8 * k, 8), :])

    row = pl.BlockSpec((ts, d), lambda i: (i, 0))
    vec = pl.BlockSpec((1, d), lambda i: (0, 0))
    hb = ts // HALO
    return _hosted_call(
        body, comm, grid=(ns,),
        out_shape=(_sds((s, d), BF16), _sds((s, d), BF16), _sds((HALO, d), F32), _sds((1, d), F32), _sds((1, d), F32)),
        in_specs=[row, pl.BlockSpec((HALO, d), lambda i: (jnp.minimum((i + 1) * hb, ns * hb - 1), 0)),
                  row, pl.BlockSpec((HALO, d), lambda i: (jnp.maximum(i * hb - 1, 0), 0)),
                  row, row, pl.BlockSpec((HALO, d), lambda i: (0, 0))],
        out_specs=(row, row, pl.BlockSpec((HALO, d), lambda i: (0, 0)), vec, vec),
        scratch_shapes=[pltpu.VMEM((ts + HALO, d), F32), pltpu.VMEM((HALO + ts, d), F32), pltpu.VMEM((8 * HALO, d), F32),
                        pltpu.VMEM((SUBLANES - 1, HALO + ts - SUBLANES, d), F32),
                        pltpu.VMEM((SUBLANES - 1, HALO + ts - SUBLANES, d), F32)],
        name="dwconv_bwd", sem=("arbitrary",), args=(ddwo, ddwo, glu, glu, a_s, g_s, wdw))


def _ada_wgrad(cat, da, name):
    nl, _, n = da.shape
    d = cat.shape[0]
    tn = 256

    def body(c_ref, d_ref, o_ref):
        acc = c_ref[:, 0:1] * d_ref[0, 0:1, :]
        for r in range(1, 8):
            acc = acc + c_ref[:, r:r + 1] * d_ref[0, r:r + 1, :]
        o_ref[0] = acc

    return pl.pallas_call(
        body, grid=(nl, n // tn), out_shape=_sds((nl, d, n), F32),
        in_specs=[pl.BlockSpec((d, 8), lambda l, j: (0, 0)), pl.BlockSpec((1, 8, tn), lambda l, j: (l, 0, j))],
        out_specs=pl.BlockSpec((1, d, tn), lambda l, j: (l, 0, j)),
        name=name, compiler_params=_cp(("parallel", "parallel")))(cat, da)


def _silu_rows(c_all):
    def body(c_ref, o_ref):
        cc = c_ref[...]
        o_ref[...] = cc * _sigmoid(cc)

    return pl.pallas_call(body, out_shape=_sds(c_all.shape, F32), name="silu_c")(c_all)


def _adamw(w, g, m, v, name):
    r, c = w.shape
    tr = r
    for cand in (512, 256, 128, 64, 32, 16, 8):
        if r % cand == 0 and cand * c * 4 <= (1 << 20):
            tr = cand
            break
    bc1 = 1.0 - ADAM_B1 ** ADAM_STEP
    bc2 = 1.0 - ADAM_B2 ** ADAM_STEP

    def body(w_ref, g_ref, m_ref, v_ref, d_ref, nm_ref, nv_ref):
        gv = g_ref[...]
        mn = ADAM_B1 * m_ref[...] + (1.0 - ADAM_B1) * gv
        vn = ADAM_B2 * v_ref[...] + (1.0 - ADAM_B2) * (gv * gv)
        mh = mn / bc1
        vh = vn / bc2
        d_ref[...] = -ADAM_LR * (mh / (jnp.sqrt(vh) + ADAM_EPS) + ADAM_WD * w_ref[...])
        nm_ref[...] = mn
        nv_ref[...] = vn

    blk = pl.BlockSpec((tr, c), lambda i: (i, 0))
    return pl.pallas_call(
        body, grid=(r // tr,), out_shape=tuple(_sds((r, c), F32) for _ in range(3)),
        in_specs=[blk, blk, blk, blk], out_specs=(blk, blk, blk),
        name=name, compiler_params=_cp(("parallel",)))(w, g, m, v)


def _halves(w2):
    r, c = w2.shape
    return w2.reshape(2, (r // 2) * c // 1024, 1024)


def _pad_rows(a, rows, axis):
    pad = [(0, 0)] * a.ndim
    pad[axis] = (0, rows - a.shape[axis])
    return jnp.pad(a, pad)


def _vec(a):
    return a.reshape(1, -1)


def kernel(x, c, mix_norm_g, mix_ada_w, mix_ada_b, ffn_norm_g, ffn_ada_w, ffn_ada_b, ffn_w_in, ffn_w_out, conv_w_in, conv_b_in, conv_w_dw, conv_b_dw, conv_ln_g, conv_ln_b, conv_w_out, conv_b_out, kv_norm_g, kv_ada_w, kv_ada_b, kv_w, forget_b, attn_w_q, attn_w_o, final_norm_g, loss_target, m_mix_norm_g, m_mix_ada_w, m_mix_ada_b, m_ffn_norm_g, m_ffn_ada_w, m_ffn_ada_b, m_ffn_w_in, m_ffn_w_out, m_conv_w_in, m_conv_b_in, m_conv_w_dw, m_conv_b_dw, m_conv_ln_g, m_conv_ln_b, m_conv_w_out, m_conv_b_out, m_kv_norm_g, m_kv_ada_w, m_kv_ada_b, m_kv_w, m_forget_b, m_attn_w_q, m_attn_w_o, m_final_norm_g, v_mix_norm_g, v_mix_ada_w, v_mix_ada_b, v_ffn_norm_g, v_ffn_ada_w, v_ffn_ada_b, v_ffn_w_in, v_ffn_w_out, v_conv_w_in, v_conv_b_in, v_conv_w_dw, v_conv_b_dw, v_conv_ln_g, v_conv_ln_b, v_conv_w_out, v_conv_b_out, v_kv_norm_g, v_kv_ada_w, v_kv_ada_b, v_kv_w, v_forget_b, v_attn_w_q, v_attn_w_o, v_final_norm_g):
    xi, yi, ci = lax.axis_index("x"), lax.axis_index("y"), lax.axis_index("c")
    chip = 2 * xi + yi
    dev = 4 * xi + 2 * yi + ci
    s, d = x.shape[1], x.shape[2]
    f = ffn_w_out.shape[1] * 4
    x0 = x[0]
    nkv = kv_w.shape[1]
    nkv_all = 4 * nkv

    wdw_loc = _pad_rows(conv_w_dw[0], HALO, 0)
    small = jnp.concatenate([c.reshape(-1), conv_b_in.reshape(-1), wdw_loc.reshape(-1), conv_b_dw.reshape(-1),
                             conv_ln_g.reshape(-1), conv_ln_b.reshape(-1), conv_b_out.reshape(-1)])
    n_small = small.shape[0]
    w_small = -(-n_small // (8 * LANE)) * LANE
    small = jnp.pad(small, (0, 8 * w_small - n_small)).reshape(8, w_small)
    small_all = _allgather8(small, "ag_small_params", True).reshape(8, 8 * w_small)
    c_all = small_all[:, :d]
    per_chip = small_all[0::2]
    dq_ = d // 4
    o1 = d
    b_in_full = per_chip[:, o1:o1 + 2 * dq_].reshape(4, 1, 2 * dq_)
    o1 += 2 * dq_
    wdw_full = per_chip[:, o1:o1 + HALO * dq_].reshape(4, HALO, dq_).transpose(1, 0, 2).reshape(HALO, d)
    o1 += HALO * dq_
    bdw_full = per_chip[:, o1:o1 + dq_].reshape(1, d)
    lng_full = per_chip[:, o1 + dq_:o1 + 2 * dq_].reshape(1, d)
    lnb_full = per_chip[:, o1 + 2 * dq_:o1 + 3 * dq_].reshape(1, d)
    bout_full = per_chip[:, o1 + 3 * dq_:o1 + 4 * dq_].reshape(1, d)

    a_mix = _ada_fwd(c_all, mix_ada_w, "ada_mix")
    a_ffn = _ada_fwd(c_all, ffn_ada_w, "ada_ffn")
    a_kv = _ada_fwd(c_all, kv_ada_w[None], "ada_kv")
    n3 = mix_ada_w.shape[2]
    n2 = kv_ada_w.shape[1]
    ada_loc = jnp.concatenate([a_mix[0], a_mix[1], a_ffn[0], a_ffn[1], a_kv[0]], axis=1)
    w_ada = ada_loc.shape[1]
    ada_all = _allgather8(ada_loc, "ag_ada", True).reshape(8, 8, w_ada)
    ada_me = lax.dynamic_index_in_dim(ada_all, dev, axis=1, keepdims=False)[0::2]

    def ada_vec(off, n, bias):
        return ada_me[:, off:off + n].reshape(1, 4 * n) + bias.reshape(1, -1)

    ada_m0 = ada_vec(0, n3, mix_ada_b[0])
    ada_m1 = ada_vec(n3, n3, mix_ada_b[1])
    ada_f0 = ada_vec(2 * n3, n3, ffn_ada_b[0])
    ada_f1 = ada_vec(3 * n3, n3, ffn_ada_b[1])
    ada_k = ada_vec(4 * n3, n2, kv_ada_b)

    def split3(a):
        return a[:, :d], a[:, d:2 * d], a[:, 2 * d:3 * d]

    sh_m0, sc_m0, gt_m0 = split3(ada_m0)
    sh_m1, sc_m1, gt_m1 = split3(ada_m1)
    sh_f0, sc_f0, gt_f0 = split3(ada_f0)
    sh_f1, sc_f1, gt_f1 = split3(ada_f1)
    sh_k, sc_k = ada_k[:, :d], ada_k[:, d:2 * d]

    def my_halves(ws):
        return [lax.dynamic_index_in_dim(w.astype(BF16).reshape(2, w.shape[0] // 2, w.shape[1]), ci, axis=0, keepdims=False)
                for w in ws]

    def whole(gath, ws):
        return [g.reshape(4, w.shape[0], w.shape[1]) for g, w in zip(gath, ws)]

    grp_conv = [conv_w_in[0], conv_w_out[0]]
    grp_ffn0 = [ffn_w_in[0], ffn_w_out[0]]
    grp_rest = [ffn_w_in[1], ffn_w_out[1], kv_w, attn_w_q[0], attn_w_o[0]]
    mine_conv, mine_ffn0, mine_rest = my_halves(grp_conv), my_halves(grp_ffn0), my_halves(grp_rest)
    cw_in, cw_out = whole(_exchange(_Gather8(mine_conv), mine_conv, "ag_w_conv"), grp_conv)
    cw_out = cw_out.reshape(d, d)

    zero_b = jnp.zeros((1, d), F32)
    g_m0, g_m1 = _vec(mix_norm_g[0]), _vec(mix_norm_g[1])
    g_f0, g_f1 = _vec(ffn_norm_g[0]), _vec(ffn_norm_g[1])
    g_k, g_fin = _vec(kv_norm_g), _vec(final_norm_g)
    fb = jnp.pad(forget_b, (0, LANE - N_HEADS)).reshape(1, LANE)

    h0, glu, a_s, g_s, gath_ffn0 = _in_pair(x0, g_m0, sh_m0, sc_m0, cw_in, b_in_full, True, "conv_in",
                                            comm=(_Gather8(mine_ffn0), mine_ffn0))
    dwo, sw, gath_rest = _dwconv_fwd(glu, wdw_full, bdw_full, lng_full, lnb_full, comm=(_Gather8(mine_rest), mine_rest))
    w_in0, w_out0 = whole(gath_ffn0, grp_ffn0)
    w_in1, w_out1, kvw, wq, wo = whole(gath_rest, grp_rest)
    w_in = [w_in0, w_in1]
    w_out = [w_out0.reshape(f, d), w_out1.reshape(f, d)]
    kvw = kvw.transpose(1, 0, 2).reshape(d, nkv_all)
    wk, wv = kvw[:, :d], kvw[:, d:2 * d]
    wf = jnp.pad(kvw[:, 2 * d:], ((0, 0), (0, LANE - N_HEADS)))
    wq, wo = wq.reshape(d, d), wo.reshape(d, d)
    x1 = _mm_res(sw, cw_out, bout_full, gt_m0, x0, "conv_out")
    hf0, act0, ug0, uu0 = _in_pair(x1, g_f0, sh_f0, sc_f0, w_in[0], None, False, "ffn0_in")
    x2 = _mm_res(act0, w_out[0], zero_b, gt_f0, x1, "ffn0_out")
    hk, h1, kk, vv, qq, flog = _qkv(x2, (g_k, sh_k, sc_k), (g_m1, sh_m1, sc_m1), wk, wv, wf, wq)
    cum, cumt = _cumsum_fwd(flog, fb)
    kaug, kaugt, vtr = _attn_prep(kk, vv, cum)
    o, lse = _attn_fwd(qq, kaug, vtr, cumt)
    x3 = _mm_res(o, wo, zero_b, gt_m1, x2, "attn_out")
    hf1, act1, ug1, uu1 = _in_pair(x3, g_f1, sh_f1, sc_f1, w_in[1], None, False, "ffn1_in")
    x4 = _mm_res(act1, w_out[1], zero_b, gt_f1, x3, "ffn1_out")
    lsum, dx4, d_gfin = _final(x4, g_fin, loss_target[0])
    loss = lax.psum(0.5 / d * jnp.sum(lsum), ("x", "y", "c"))

    nf = f // 2

    sel = jnp.stack([ci, chip]).astype(jnp.int32)

    def reduce_begin(gs, tag):
        ps = [g.reshape(4, 2, g.shape[1] // 2, g.shape[2]) for g in gs]
        lands = _exchange(_SwapHalves(ps), ps, tag + "_swap")
        pairs = [_add_halves(p_, l_, sel, f"{tag}_add{k}") for k, (p_, l_) in enumerate(zip(ps, lands))]
        return [q for q, _ in pairs], [o_ for _, o_ in pairs]

    def reduce_sum(owns, lands, tag):
        return [_add_chips(o_, l_, sel, f"{tag}_sum{k}") for k, (o_, l_) in enumerate(zip(owns, lands))]

    def ffn_bwd(dx_out, x_in, hf, act, ug, uu, gain, scale, gate, w_in_l, w_out_l, tag, comm=None):
        res = _ffn_bwd_act(dx_out, gate, w_out_l, ug, uu, tag + "_bwd_act", comm=comm)
        dug, duu = res[0], res[1]
        dw_out, dgate = _dw_mm(act, [dx_out], nf, d, tag + "_dw_out", gate=gate, wfull=w_out_l, dgate_init=zero_b)
        terms = [(dug, 0, w_in_l, 0), (dug, nf, w_in_l, 1), (duu, 0, w_in_l, 2), (duu, nf, w_in_l, 3)]
        dx_in, dsh, dsc, dgn = _mm_normbwd(terms, x_in, dx_out, gain, scale, tag + "_bwd_in")
        dw_in = _dw_mm(hf, [dug, duu], d, nf, tag + "_dw_in")
        return dx_in, dw_in, dw_out[0], dsh, dsc, dgate, dgn, (res[2] if comm is not None else None)

    dx3, dw_in1, dw_out1, dsh_f1, dsc_f1, dgt_f1, dgn_f1, _ = ffn_bwd(dx4, x3, hf1, act1, ug1, uu1, g_f1, sc_f1, gt_f1, w_in[1], w_out[1], "ffn1")
    q16_1, own_1 = reduce_begin([dw_in1, dw_out1.reshape(4, f // 4, d)], "rs_ffn1")

    do, deltat = _do_kernel(dx3, gt_m1, wo, o)
    dwo_att, dgt_m1 = _dw_mm(o, [dx3], d, d, "attn_dw_o", gate=gt_m1, wfull=wo, dgate_init=zero_b)
    dq, dk, dv, dcq, dck, land_1 = _attn_bwd(qq, do, kaug, kaugt, vv, cumt, lse, deltat, comm=(_ScatterChips(q16_1), q16_1))
    wq3 = wq.reshape(1, d, d)
    dx2a, dsh_m1, dsc_m1, dgn_m1 = _mm_normbwd([(dq, 0, wq3, 0)], x2, dx3, g_m1, sc_m1, "attn_bwd_q")
    dwq = _dw_mm(h1, [dq], d, d, "attn_dw_q")[0]

    df, dfb = _cumsum_bwd(dcq[:, :2].reshape(N_HEADS, s), dck[:, :2].reshape(N_HEADS, s), flog, fb)
    terms = [(dk, 0, wk.reshape(1, d, d), 0), (dv, 0, wv.reshape(1, d, d), 0), (df, 0, wf.reshape(1, d, LANE), 0)]
    dx2, dsh_k, dsc_k, dgn_k = _mm_normbwd(terms, x2, dx2a, g_k, sc_k, "kv_bwd")
    dwk = _dw_mm(hk, [dk], d, d, "kv_dw_k")[0]
    dwv = _dw_mm(hk, [dv], d, d, "kv_dw_v")[0]
    dwf = _dw_mm(hk, [df], d, LANE, "kv_dw_f")[0]
    dkvw = jnp.concatenate([dwk, dwv, dwf[:, :N_HEADS]], axis=1)
    dkvw = dkvw.reshape(d, 4, nkv).transpose(1, 0, 2)

    q16_2, own_2 = reduce_begin([dkvw, dwq.reshape(4, d // 4, d), dwo_att[0].reshape(4, d // 4, d)], "rs_attn")
    dx1, dw_in0, dw_out0, dsh_f0, dsc_f0, dgt_f0, dgn_f0, land_2 = ffn_bwd(
        dx2, x1, hf0, act0, ug0, uu0, g_f0, sc_f0, gt_f0, w_in[0], w_out[0], "ffn0", comm=(_ScatterChips(q16_2), q16_2))
    q16_3, own_3 = reduce_begin([dw_in0, dw_out0.reshape(4, f // 4, d)], "rs_ffn0")

    ddwo, d_lng, d_lnb, d_bdw, d_bout, dgt_extra = _conv_bwd1(dx1, gt_m0, cw_out, bout_full, dwo, lng_full, lnb_full)
    dcw_out, dgt_m0 = _dw_mm(sw, [dx1], d, d, "conv_dw_out", gate=gt_m0, wfull=cw_out, dgate_init=dgt_extra)
    da, dg, d_wdw, d_bin_a, d_bin_g, land_3 = _dwconv_bwd(ddwo, glu, a_s, g_s, wdw_full, comm=(_ScatterChips(q16_3), q16_3))
    nc = cw_in.shape[2]
    terms = [(da, 0, cw_in, 0), (da, nc, cw_in, 1), (dg, 0, cw_in, 2), (dg, nc, cw_in, 3)]
    dx0, dsh_m0, dsc_m0, dgn_m0 = _mm_normbwd(terms, x0, dx1, g_m0, sc_m0, "conv_bwd_in")
    dcw_in = _dw_mm(h0, [da, dg], d, nc, "conv_dw_in")
    q16_4, own_4 = reduce_begin([dcw_in, dcw_out[0].reshape(4, d // 4, d)], "rs_conv")
    land_4 = _exchange(_ScatterChips(q16_4), q16_4, "rs_conv_scatter")

    sums = (reduce_sum(own_1, land_1, "rs_ffn1") + reduce_sum(own_2, land_2, "rs_attn")
            + reduce_sum(own_3, land_3, "rs_ffn0") + reduce_sum(own_4, land_4, "rs_conv"))
    reduced = [b.reshape(2 * b.shape[1], b.shape[2]) for b in _share_halves(sums)]
    g_w_in1, g_w_out1, g_kvw, g_wq, g_wo, g_w_in0, g_w_out0, g_cw_in, g_cw_out = reduced

    d_ada = [jnp.concatenate([dsh_m0, dsc_m0, dgt_m0], axis=1), jnp.concatenate([dsh_m1, dsc_m1, dgt_m1], axis=1),
             jnp.concatenate([dsh_f0, dsc_f0, dgt_f0], axis=1), jnp.concatenate([dsh_f1, dsc_f1, dgt_f1], axis=1),
             jnp.concatenate([dsh_k, dsc_k], axis=1)]
    fields = d_ada + [dgn_m0, dgn_m1, dgn_f0, dgn_f1, dgn_k, d_gfin, d_bin_a, d_bin_g, d_bdw, d_lng, d_lnb, d_bout,
                      d_wdw.reshape(1, -1), dfb]
    foffs = [0]
    for fl in fields:
        foffs.append(foffs[-1] + fl.shape[1])
    n_row = foffs[-1]
    w_row = -(-n_row // (8 * LANE)) * LANE
    row = jnp.pad(jnp.concatenate(fields, axis=1), ((0, 0), (0, 8 * w_row - n_row))).reshape(8, w_row)
    rows_all = _allgather8(row, "ag_small_grads", True).reshape(8, 8, w_row)
    rsum_small = _sum8(rows_all).reshape(1, 8 * w_row)
    rows_flat = rows_all.reshape(8, 8 * w_row)

    def fsum(i):
        return rsum_small[:, foffs[i]:foffs[i + 1]]

    cat = _silu_rows(c_all).T

    def ada_cols(i, n):
        full = rows_flat[:, foffs[i]:foffs[i + 1]].reshape(8, 4, n)
        return lax.dynamic_index_in_dim(full, chip, axis=1, keepdims=False)

    g_mix_ada_w = _ada_wgrad(cat, jnp.stack([ada_cols(0, n3), ada_cols(1, n3)]), "ada_mix_wgrad")
    g_ffn_ada_w = _ada_wgrad(cat, jnp.stack([ada_cols(2, n3), ada_cols(3, n3)]), "ada_ffn_wgrad")
    g_kv_ada_w = _ada_wgrad(cat, ada_cols(4, n2)[None], "ada_kv_wgrad")[0]

    def my_cols(v, n):
        return lax.dynamic_index_in_dim(v.reshape(4, n), chip, axis=0, keepdims=False)

    grads = {
        "mix_norm_g": jnp.concatenate([fsum(5), fsum(6)], axis=0),
        "mix_ada_w": g_mix_ada_w,
        "mix_ada_b": jnp.concatenate([fsum(0), fsum(1)], axis=0),
        "ffn_norm_g": jnp.concatenate([fsum(7), fsum(8)], axis=0),
        "ffn_ada_w": g_ffn_ada_w,
        "ffn_ada_b": jnp.concatenate([fsum(2), fsum(3)], axis=0),
        "ffn_w_in": jnp.stack([g_w_in0, g_w_in1]),
        "ffn_w_out": jnp.stack([g_w_out0, g_w_out1]),
        "conv_w_in": g_cw_in[None],
        "conv_b_in": my_cols(jnp.concatenate([fsum(11), fsum(12)], axis=1), 2 * dq_)[None],
        "conv_w_dw": lax.dynamic_index_in_dim(fsum(17).reshape(HALO, 4, dq_), chip, axis=1, keepdims=False)[:CONV_K][None],
        "conv_b_dw": my_cols(fsum(13), dq_)[None],
        "conv_ln_g": my_cols(fsum(14), dq_)[None],
        "conv_ln_b": my_cols(fsum(15), dq_)[None],
        "conv_w_out": g_cw_out[None],
        "conv_b_out": my_cols(fsum(16), dq_)[None],
        "kv_norm_g": fsum(9).reshape(-1),
        "kv_ada_w": g_kv_ada_w,
        "kv_ada_b": fsum(4).reshape(-1),
        "kv_w": g_kvw,
        "forget_b": fsum(18).reshape(-1)[:N_HEADS],
        "attn_w_q": g_wq[None],
        "attn_w_o": g_wo[None],
        "final_norm_g": fsum(10).reshape(-1),
    }
    weights = dict(mix_norm_g=mix_norm_g, mix_ada_w=mix_ada_w, mix_ada_b=mix_ada_b, ffn_norm_g=ffn_norm_g, ffn_ada_w=ffn_ada_w, ffn_ada_b=ffn_ada_b, ffn_w_in=ffn_w_in, ffn_w_out=ffn_w_out, conv_w_in=conv_w_in, conv_b_in=conv_b_in, conv_w_dw=conv_w_dw, conv_b_dw=conv_b_dw, conv_ln_g=conv_ln_g, conv_ln_b=conv_ln_b, conv_w_out=conv_w_out, conv_b_out=conv_b_out, kv_norm_g=kv_norm_g, kv_ada_w=kv_ada_w, kv_ada_b=kv_ada_b, kv_w=kv_w, forget_b=forget_b, attn_w_q=attn_w_q, attn_w_o=attn_w_o, final_norm_g=final_norm_g)
    moms = dict(mix_norm_g=(m_mix_norm_g, v_mix_norm_g), mix_ada_w=(m_mix_ada_w, v_mix_ada_w), mix_ada_b=(m_mix_ada_b, v_mix_ada_b), ffn_norm_g=(m_ffn_norm_g, v_ffn_norm_g), ffn_ada_w=(m_ffn_ada_w, v_ffn_ada_w), ffn_ada_b=(m_ffn_ada_b, v_ffn_ada_b), ffn_w_in=(m_ffn_w_in, v_ffn_w_in), ffn_w_out=(m_ffn_w_out, v_ffn_w_out), conv_w_in=(m_conv_w_in, v_conv_w_in), conv_b_in=(m_conv_b_in, v_conv_b_in), conv_w_dw=(m_conv_w_dw, v_conv_w_dw), conv_b_dw=(m_conv_b_dw, v_conv_b_dw), conv_ln_g=(m_conv_ln_g, v_conv_ln_g), conv_ln_b=(m_conv_ln_b, v_conv_ln_b), conv_w_out=(m_conv_w_out, v_conv_w_out), conv_b_out=(m_conv_b_out, v_conv_b_out), kv_norm_g=(m_kv_norm_g, v_kv_norm_g), kv_ada_w=(m_kv_ada_w, v_kv_ada_w), kv_ada_b=(m_kv_ada_b, v_kv_ada_b), kv_w=(m_kv_w, v_kv_w), forget_b=(m_forget_b, v_forget_b), attn_w_q=(m_attn_w_q, v_attn_w_q), attn_w_o=(m_attn_w_o, v_attn_w_o), final_norm_g=(m_final_norm_g, v_final_norm_g))
    names = list(weights)

    deltas, new_m, new_v = {}, {}, {}
    small_names = [n for n in names if weights[n].size < (1 << 16)]
    for n in names:
        if n in small_names:
            continue
        w = weights[n]
        w2 = w.reshape(-1, w.shape[-1])
        dl, nm, nv = _adamw(w2, grads[n].reshape(w2.shape), moms[n][0].reshape(w2.shape), moms[n][1].reshape(w2.shape), "adamw_" + n)
        deltas[n], new_m[n], new_v[n] = dl.reshape(w.shape), nm.reshape(w.shape), nv.reshape(w.shape)

    def pack_small(get):
        flat = jnp.concatenate([get(n).reshape(-1) for n in small_names])
        rows_ = -(-flat.shape[0] // (8 * LANE)) * 8
        return jnp.pad(flat, (0, rows_ * LANE - flat.shape[0])).reshape(rows_, LANE)

    ws, gs = pack_small(lambda n: weights[n]), pack_small(lambda n: grads[n])
    ms_, vs_ = pack_small(lambda n: moms[n][0]), pack_small(lambda n: moms[n][1])
    vs_ = jnp.where(jnp.arange(vs_.size).reshape(vs_.shape) < sum(weights[n].size for n in small_names), vs_, 1.0)
    dl, nm, nv = _adamw(ws, gs, ms_, vs_, "adamw_small")
    off = 0
    for n in small_names:
        sz = weights[n].size
        shp = weights[n].shape
        deltas[n] = dl.reshape(-1)[off:off + sz].reshape(shp)
        new_m[n] = nm.reshape(-1)[off:off + sz].reshape(shp)
        new_v[n] = nv.reshape(-1)[off:off + sz].reshape(shp)
        off += sz

    grad_out = [grads[n].reshape(weights[n].shape) for n in names]
    return (loss, dx0[None], *grad_out, *[deltas[n] for n in names], *[new_m[n] for n in names], *[new_v[n] for n in names])
```

```python
import functools

import jax
import jax.numpy as jnp
from jax import lax
from jax.experimental import pallas as pl
from jax.experimental.pallas import tpu as pltpu

F32 = jnp.float32
BF16 = jnp.bfloat16
MESH = pl.DeviceIdType.MESH

EPS = 1e-6
N_HEADS = 16
HEAD_DIM = 64
CONV_K = 31
LANE = 128
SUBLANES = 8
HALO = 32
ATT_FWD_TQ = 2048
ATT_TQ = 1024
ATT_TK = 512
NPIECE = 3
SPARE = (HEAD_DIM, 0)
VMEM_MB = 48
ATT_BWD_VMEM_MB = 56

ADAM_LR = 0.001
ADAM_B1 = 0.9
ADAM_B2 = 0.999
ADAM_EPS = 1e-08
ADAM_WD = 0.01
ADAM_STEP = 10


def _sds(shape, dtype):
    return jax.ShapeDtypeStruct(tuple(shape), dtype)


def _cp(sem=None, vmem_mb=VMEM_MB):
    return pltpu.CompilerParams(dimension_semantics=sem, vmem_limit_bytes=vmem_mb << 20)


def _tile(n, pref):
    return pref if n % pref == 0 else n


def _row_tile(r, mult, width=1024):
    cap = max(mult, (512 * 1024 // width) // mult * mult)
    for cand in range(cap, mult - 1, -mult):
        if r % cand == 0:
            return cand
    return r


def _resident(shape):
    nd = len(shape)
    return pl.BlockSpec(tuple(shape), lambda *_: (0,) * nd, pipeline_mode=pl.Buffered(1))


def _dot(a, b):
    return jnp.dot(a, b, preferred_element_type=F32)


def _dot_nt(a, b):
    return lax.dot_general(a, b, (((1,), (1,)), ((), ())), preferred_element_type=F32)


def _dot_tn(a, b):
    return lax.dot_general(a, b, (((0,), (0,)), ((), ())), preferred_element_type=F32)


def _sigmoid(x):
    return 1.0 / (1.0 + jnp.exp(-x))


def _colsum(x):
    return jnp.sum(x, axis=0, keepdims=True)


def _rms_parts(x):
    rstd = lax.rsqrt(jnp.mean(x * x, axis=-1, keepdims=True) + EPS)
    return x * rstd, rstd


class _Gather8:
    def __init__(self, xs):
        self.n = len(xs)
        self.m = [x.shape[0] for x in xs]
        self.land = [_sds((8 * x.shape[0],) + tuple(x.shape[1:]), x.dtype) for x in xs]
        self.sems = [pltpu.SemaphoreType.DMA((7 * self.n,)), pltpu.SemaphoreType.DMA((7 * self.n,)),
                     pltpu.SemaphoreType.DMA((self.n,))]

    def _parts(self, a, x_refs, out_refs, send_sems, recv_sems, local_sems):
        x, y, c = lax.axis_index("x"), lax.axis_index("y"), lax.axis_index("c")
        me, sibling = (x, y, c), (x, y, 1 - c)
        chips = [(1 - x, y), (x, 1 - y), (1 - x, 1 - y)]
        m_per, x_ref, out_ref = self.m[a], x_refs[a], out_refs[a]

        def rows(px, py, pc):
            return out_ref.at[pl.ds((4 * px + 2 * py + pc) * m_per, m_per)]

        def copy(k, block, to, src=None):
            return pltpu.make_async_remote_copy(
                src_ref=rows(*block) if src is None else src, dst_ref=rows(*block),
                send_sem=send_sems.at[7 * a + k], recv_sem=recv_sems.at[7 * a + k], device_id=to, device_id_type=MESH)

        mine = pltpu.make_async_copy(x_ref, rows(*me), local_sems.at[a])
        first = [copy(0, me, sibling, src=x_ref)]
        first += [copy(1 + j, me, (*chip, c), src=x_ref) for j, chip in enumerate(chips)]
        passed = [copy(4 + j, (*chip, c), sibling) for j, chip in enumerate(chips)]
        return c, me, sibling, chips, copy, mine, first, passed

    def start(self, *refs):
        for a in range(self.n):
            _, _, _, _, _, mine, first, _ = self._parts(a, *refs)
            mine.start()
            for cp in first:
                cp.start()

    def finish(self, *refs):
        parts = [self._parts(a, *refs) for a in range(self.n)]
        for j in range(3):
            for c, me, sibling, chips, copy, mine, first, passed in parts:
                copy(1 + j, (*chips[j], c), me).wait_recv()
                passed[j].start()
        for c, me, sibling, chips, copy, mine, first, passed in parts:
            copy(0, sibling, me).wait_recv()
            for j, chip in enumerate(chips):
                copy(4 + j, (*chip, 1 - c), me).wait_recv()
            for cp in first + passed:
                cp.wait_send()
            mine.wait()


class _ScatterChips:
    def __init__(self, qs):
        self.n = len(qs)
        self.land = [_sds((3,) + tuple(q.shape[1:]), q.dtype) for q in qs]
        self.sems = [pltpu.SemaphoreType.DMA((3 * self.n,)), pltpu.SemaphoreType.DMA((3 * self.n,))]

    def _copies(self, q_refs, land_refs, send_sems, recv_sems):
        x, y, c = lax.axis_index("x"), lax.axis_index("y"), lax.axis_index("c")
        chips = [(1 - x, y), (x, 1 - y), (1 - x, 1 - y)]
        return [pltpu.make_async_remote_copy(
            src_ref=q_refs[a].at[2 * cx + cy], dst_ref=land_refs[a].at[k],
            send_sem=send_sems.at[3 * a + k], recv_sem=recv_sems.at[3 * a + k],
            device_id=(cx, cy, c), device_id_type=MESH) for a in range(self.n) for k, (cx, cy) in enumerate(chips)]

    def start(self, *refs):
        for cp in self._copies(*refs):
            cp.start()

    def finish(self, *refs):
        copies = self._copies(*refs)
        for cp in copies:
            cp.wait_recv()
        for cp in copies:
            cp.wait_send()


class _SwapHalves:
    def __init__(self, ps):
        self.n = len(ps)
        self.land = [_sds((p.shape[0],) + tuple(p.shape[2:]), p.dtype) for p in ps]
        self.nb = [p.shape[0] for p in ps]
        tot = sum(self.nb)
        self.sems = [pltpu.SemaphoreType.DMA((tot,)), pltpu.SemaphoreType.DMA((tot,))]

    def _copies(self, p_refs, land_refs, send_sems, recv_sems):
        x, y, c = lax.axis_index("x"), lax.axis_index("y"), lax.axis_index("c")
        out, k = [], 0
        for a in range(self.n):
            for j in range(self.nb[a]):
                out.append(pltpu.make_async_remote_copy(
                    src_ref=p_refs[a].at[j, 1 - c], dst_ref=land_refs[a].at[j], send_sem=send_sems.at[k],
                    recv_sem=recv_sems.at[k], device_id=(x, y, 1 - c), device_id_type=MESH))
                k += 1
        return out

    start = _ScatterChips.start
    finish = _ScatterChips.finish


def _hosted_call(body, comm, *, grid, in_specs, out_specs, out_shape, scratch_shapes, name, sem, args, vmem_mb=VMEM_MB):
    def first():
        return functools.reduce(jnp.logical_and, [pl.program_id(a) == 0 for a in range(len(grid))])

    def last():
        return functools.reduce(jnp.logical_and, [pl.program_id(a) == g - 1 for a, g in enumerate(grid)])

    out_specs = tuple(out_specs) if isinstance(out_specs, (tuple, list)) else (out_specs,)
    out_shape = tuple(out_shape) if isinstance(out_shape, (tuple, list)) else (out_shape,)
    if comm is None:
        return pl.pallas_call(body, grid=grid, in_specs=list(in_specs), out_specs=out_specs, out_shape=out_shape,
                              scratch_shapes=list(scratch_shapes), name=name, compiler_params=_cp(sem, vmem_mb))(*args)
    ex, srcs = comm
    n_in, n_out, n_scr, n_ex = len(in_specs), len(out_shape), len(scratch_shapes), ex.n

    def wrapped(*refs):
        ins, src_refs = refs[:n_in], refs[n_in:n_in + n_ex]
        o0 = n_in + n_ex
        outs, land_refs = refs[o0:o0 + n_out], refs[o0 + n_out:o0 + n_out + n_ex]
        s0 = o0 + n_out + n_ex
        scr, sems = refs[s0:s0 + n_scr], refs[s0 + n_scr:]

        @pl.when(first())
        def _():
            ex.start(src_refs, land_refs, *sems)

        body(*ins, *outs, *scr)

        @pl.when(last())
        def _():
            ex.finish(src_refs, land_refs, *sems)

    hbm = pl.BlockSpec(memory_space=pl.ANY)
    res = pl.pallas_call(
        wrapped, grid=grid, in_specs=[*in_specs, *[hbm] * n_ex], out_specs=(*out_specs, *[hbm] * n_ex),
        out_shape=(*out_shape, *ex.land), scratch_shapes=[*scratch_shapes, *ex.sems], name=name,
        compiler_params=_cp(tuple("arbitrary" for _ in grid), vmem_mb))(*args, *srcs)
    return (*res[:n_out], list(res[n_out:]))


def _exchange(ex, srcs, name, in_vmem=False):
    n = ex.n

    def body(*refs):
        src_refs, land_refs, sems = refs[:n], refs[n:2 * n], refs[2 * n:]
        ex.start(src_refs, land_refs, *sems)
        ex.finish(src_refs, land_refs, *sems)

    spec = pl.BlockSpec(memory_space=pltpu.VMEM if in_vmem else pl.ANY)
    return list(pl.pallas_call(
        body, out_shape=tuple(ex.land), in_specs=[spec] * n, out_specs=tuple([spec] * n),
        scratch_shapes=ex.sems, name=name)(*srcs))


def _allgather8(x_shard, name, in_vmem):
    return _exchange(_Gather8([x_shard]), [x_shard], name, in_vmem)[0]


def _share_halves(bufs):
    n = len(bufs)

    def body(*refs):
        b_refs, out_refs, send_sems, recv_sems = refs[:n], refs[n:2 * n], refs[2 * n], refs[2 * n + 1]
        x, y, c = lax.axis_index("x"), lax.axis_index("y"), lax.axis_index("c")
        copies = [pltpu.make_async_remote_copy(
            src_ref=b_refs[k].at[c], dst_ref=out_refs[k].at[c], send_sem=send_sems.at[k], recv_sem=recv_sems.at[k],
            device_id=(x, y, 1 - c), device_id_type=MESH) for k in range(n)]
        for cp in copies:
            cp.start()
        for cp in copies:
            cp.wait_recv()
        for cp in copies:
            cp.wait_send()

    hbm = pl.BlockSpec(memory_space=pl.ANY)
    return pl.pallas_call(
        body, out_shape=tuple(_sds(b.shape, b.dtype) for b in bufs), in_specs=[hbm] * n, out_specs=tuple([hbm] * n),
        scratch_shapes=[pltpu.SemaphoreType.DMA((n,)), pltpu.SemaphoreType.DMA((n,))],
        input_output_aliases={k: k for k in range(n)}, name="rs_share_halves")(*bufs)


def _add_halves(p, land, sel, name):
    nb, _, r, w = p.shape
    tr = _row_tile(r, 16, w)

    def body(sel_ref, p_ref, l_ref, q16_ref, own_ref):
        q = p_ref[0, 0] + l_ref[0]
        q16_ref[0] = q.astype(BF16)

        @pl.when(pl.program_id(1) == sel_ref[1])
        def _():
            own_ref[...] = q

    gs = pltpu.PrefetchScalarGridSpec(
        num_scalar_prefetch=1, grid=(r // tr, nb),
        in_specs=[pl.BlockSpec((1, 1, tr, w), lambda i, j, sl: (j, sl[0], i, 0)),
                  pl.BlockSpec((1, tr, w), lambda i, j, sl: (j, i, 0))],
        out_specs=(pl.BlockSpec((1, tr, w), lambda i, j, sl: (j, i, 0)), pl.BlockSpec((tr, w), lambda i, j, sl: (i, 0))))
    return pl.pallas_call(body, grid_spec=gs, out_shape=(_sds((nb, r, w), BF16), _sds((r, w), F32)), name=name,
                          compiler_params=_cp(("parallel", "arbitrary")))(sel, p, land)


def _add_chips(own, land, sel, name):
    r, w = own.shape
    tr = _row_tile(r, 16, w)

    def body(sel_ref, q_ref, l_ref, o_ref):
        o_ref[0] = ((q_ref[...] + l_ref[0].astype(F32)) + l_ref[1].astype(F32)) + l_ref[2].astype(F32)

    gs = pltpu.PrefetchScalarGridSpec(
        num_scalar_prefetch=1, grid=(r // tr,),
        in_specs=[pl.BlockSpec((tr, w), lambda i, sl: (i, 0)), pl.BlockSpec((3, tr, w), lambda i, sl: (0, i, 0))],
        out_specs=pl.BlockSpec((1, tr, w), lambda i, sl: (sl[0], i, 0)))
    return pl.pallas_call(body, grid_spec=gs, out_shape=_sds((2, r, w), F32), name=name,
                          compiler_params=_cp(("parallel",)))(sel, own, land)


def _sum8(g):
    _, m, n = g.shape

    def body(g_ref, o_ref):
        acc = g_ref[0]
        for k in range(1, 8):
            acc = acc + g_ref[k]
        o_ref[...] = acc

    return pl.pallas_call(body, out_shape=_sds((m, n), g.dtype), name="sum8")(g)


def _ada_fwd(c_all, w3, name):
    nl, d, n = w3.shape
    tn = 256

    def body(c_ref, w_ref, o_ref):
        cc = c_ref[...]
        ca = (cc * _sigmoid(cc)).astype(BF16)
        o_ref[0] = _dot(ca, w_ref[0].astype(BF16))

    return pl.pallas_call(
        body, grid=(nl, n // tn), out_shape=_sds((nl, 8, n), F32),
        in_specs=[pl.BlockSpec((8, d), lambda l, j: (0, 0)), pl.BlockSpec((1, d, tn), lambda l, j: (l, 0, j))],
        out_specs=pl.BlockSpec((1, 8, tn), lambda l, j: (l, 0, j)),
        name=name, compiler_params=_cp(("parallel", "parallel")))(c_all, w3)


def _in_pair(x, gain, shift, scale, wg, bias, conv, name, comm=None):
    s, d = x.shape
    n = wg.shape[2]
    ts = _tile(s, 512)

    def body(*refs):
        if conv:
            x_ref, g_ref, sh_ref, sc_ref, wa_ref, wb_ref, ba_ref, bb_ref, h_ref, o_ref, sa_ref, sb_ref, hs = refs
        else:
            x_ref, g_ref, sh_ref, sc_ref, wa_ref, wb_ref, h_ref, o_ref, sa_ref, sb_ref, hs = refs

        @pl.when(pl.program_id(1) == 0)
        def _():
            xhat, _ = _rms_parts(x_ref[...])
            h = (xhat * g_ref[...]) * (1.0 + sc_ref[...]) + sh_ref[...]
            hs[...] = h.astype(BF16)
            h_ref[...] = hs[...]

        h = hs[...]
        a = _dot(h, wa_ref[0])
        b = _dot(h, wb_ref[0])
        if conv:
            a = a + ba_ref[0]
            b = b + bb_ref[0]
            o_ref[...] = a * _sigmoid(b)
        else:
            o_ref[...] = (a * _sigmoid(a) * b).astype(BF16)
        sa_ref[...] = a.astype(BF16)
        sb_ref[...] = b.astype(BF16)

    vec = pl.BlockSpec((1, d), lambda i, q: (0, 0))
    in_specs = [pl.BlockSpec((ts, d), lambda i, q: (i, 0)), vec, vec, vec,
                pl.BlockSpec((1, d, n), lambda i, q: (q, 0, 0)), pl.BlockSpec((1, d, n), lambda i, q: (q + 2, 0, 0))]
    args = [x, gain, shift, scale, wg, wg]
    if conv:
        in_specs += [pl.BlockSpec((1, 1, n), lambda i, q: (q, 0, 0)), pl.BlockSpec((1, 1, n), lambda i, q: (q + 2, 0, 0))]
        args += [bias, bias]
    tile = pl.BlockSpec((ts, n), lambda i, q: (i, q))
    return _hosted_call(
        body, comm, grid=(s // ts, 2),
        out_shape=(_sds((s, d), BF16), _sds((s, 2 * n), F32 if conv else BF16), _sds((s, 2 * n), BF16), _sds((s, 2 * n), BF16)),
        in_specs=in_specs, out_specs=(pl.BlockSpec((ts, d), lambda i, q: (i, 0)), tile, tile, tile),
        scratch_shapes=[pltpu.VMEM((ts, d), BF16)], name=name, sem=("parallel", "arbitrary"), args=args)


def _shift_copies(buf, shf):
    n = shf.shape[1]
    for r in range(1, SUBLANES):
        shf[r - 1, :, :] = buf[pl.ds(r, n), :]


def _shifted(buf, shf, start, n, cs):
    a, r = divmod(start, SUBLANES)
    if r == 0:
        return buf[pl.ds(start, n), cs]
    return shf[r - 1, pl.ds(a * SUBLANES, n), cs]


def _dwconv_fwd(glu, wdw, bdw, lng, lnb, comm=None):
    s, d = glu.shape
    ts = _tile(s, 256)
    rb, cb = 32, 256

    def body(cur_ref, halo_ref, w_ref, b_ref, g_ref, be_ref, dwo_ref, sw_ref, buf, shf):
        i = pl.program_id(0)

        @pl.when(i == 0)
        def _():
            buf[pl.ds(0, HALO), :] = jnp.zeros((HALO, d), F32)

        @pl.when(i > 0)
        def _():
            buf[pl.ds(0, HALO), :] = halo_ref[...]

        buf[pl.ds(HALO, ts), :] = cur_ref[...]
        _shift_copies(buf, shf)
        for r in range(ts // rb):
            for cc in range(d // cb):
                cs = pl.ds(cc * cb, cb)
                acc = jnp.zeros((rb, cb), F32) + b_ref[:, cs]
                for k in range(CONV_K):
                    acc = acc + w_ref[pl.ds(k, 1), cs] * _shifted(buf, shf, HALO - (CONV_K - 1) + k + r * rb, rb, cs)
                dwo_ref[pl.ds(r * rb, rb), cs] = acc
            rows = pl.ds(r * rb, rb)
            yv = dwo_ref[rows, :]
            mu = jnp.mean(yv, axis=-1, keepdims=True)
            yc = yv - mu
            var = jnp.mean(yc * yc, axis=-1, keepdims=True)
            ln = yc * lax.rsqrt(var + EPS) * g_ref[...] + be_ref[...]
            sw_ref[rows, :] = (ln * _sigmoid(ln)).astype(BF16)

    vec = pl.BlockSpec((1, d), lambda i: (0, 0))
    return _hosted_call(
        body, comm, grid=(s // ts,), out_shape=(_sds((s, d), F32), _sds((s, d), BF16)),
        in_specs=[pl.BlockSpec((ts, d), lambda i: (i, 0)),
                  pl.BlockSpec((HALO, d), lambda i: (jnp.maximum(i * (ts // HALO) - 1, 0), 0)),
                  pl.BlockSpec((HALO, d), lambda i: (0, 0)), vec, vec, vec],
        out_specs=(pl.BlockSpec((ts, d), lambda i: (i, 0)), pl.BlockSpec((ts, d), lambda i: (i, 0))),
        scratch_shapes=[pltpu.VMEM((HALO + ts, d), F32), pltpu.VMEM((SUBLANES - 1, HALO + ts - SUBLANES, d), F32)],
        name="dwconv_fwd", sem=("parallel",),
        args=(glu, glu, wdw, bdw, lng, lnb))


def _mm_res(a, w, b, gate, x, name):
    s, k = a.shape
    d = w.shape[1]
    ts = _tile(s, 512)

    def body(a_ref, w_ref, b_ref, g_ref, x_ref, o_ref):
        yv = _dot(a_ref[...], w_ref[...]) + b_ref[...]
        o_ref[...] = x_ref[...] + g_ref[...] * yv

    vec = pl.BlockSpec((1, d), lambda i: (0, 0))
    return pl.pallas_call(
        body, grid=(s // ts,), out_shape=_sds((s, d), F32),
        in_specs=[pl.BlockSpec((ts, k), lambda i: (i, 0)), _resident((k, d)), vec, vec, pl.BlockSpec((ts, d), lambda i: (i, 0))],
        out_specs=pl.BlockSpec((ts, d), lambda i: (i, 0)),
        name=name, compiler_params=_cp(("parallel",)))(a, w, b, gate, x)


def _qkv(x, kvp, mxp, wk, wv, wf, wq):
    s, d = x.shape
    ts = _tile(s, 512)
    qscale = HEAD_DIM ** -0.5

    def body(x_ref, gk, shk, sck, gm, shm, scm, wk_ref, wv_ref, wf_ref, wq_ref, hk_ref, h1_ref, k_ref, v_ref, q_ref, f_ref):
        xhat, _ = _rms_parts(x_ref[...])
        hk = ((xhat * gk[...]) * (1.0 + sck[...]) + shk[...]).astype(BF16)
        h1 = ((xhat * gm[...]) * (1.0 + scm[...]) + shm[...]).astype(BF16)
        hk_ref[...] = hk
        h1_ref[...] = h1
        k_ref[...] = _dot(hk, wk_ref[...]).astype(BF16)
        v_ref[...] = _dot(hk, wv_ref[...]).astype(BF16)
        f_ref[...] = _dot(hk, wf_ref[...])
        q_ref[...] = (_dot(h1, wq_ref[...]) * qscale).astype(BF16)

    vec = pl.BlockSpec((1, d), lambda i: (0, 0))
    row = pl.BlockSpec((ts, d), lambda i: (i, 0))
    return pl.pallas_call(
        body, grid=(s // ts,),
        out_shape=tuple(_sds((s, d), BF16) for _ in range(5)) + (_sds((s, LANE), F32),),
        in_specs=[row, vec, vec, vec, vec, vec, vec, _resident((d, d)), _resident((d, d)), _resident((d, LANE)), _resident((d, d))],
        out_specs=(row, row, row, row, row, pl.BlockSpec((ts, LANE), lambda i: (i, 0))),
        name="qkv_proj", compiler_params=_cp(("parallel",)))(x, *kvp, *mxp, wk, wv, wf, wq)


def _log_sigmoid(z):
    return jnp.minimum(z, 0.0) - jnp.log(1.0 + jnp.exp(-jnp.abs(z)))


def _cumsum_fwd(flog, fb):
    s = flog.shape[0]
    ts = _tile(s, 256)

    def body(f_ref, b_ref, cum_ref, cumt_ref, carry):
        @pl.when(pl.program_id(0) == 0)
        def _():
            carry[...] = jnp.zeros_like(carry)

        ls = _log_sigmoid(f_ref[...] + b_ref[...])
        r = lax.broadcasted_iota(jnp.int32, (ts, ts), 0)
        cidx = lax.broadcasted_iota(jnp.int32, (ts, ts), 1)
        tri = (cidx <= r).astype(F32)
        cs = jnp.dot(tri, ls, preferred_element_type=F32, precision=lax.Precision.HIGHEST) + carry[...]
        cum_ref[...] = cs
        cumt_ref[...] = cs.T
        carry[...] = cs[ts - 1:ts, :]

    return pl.pallas_call(
        body, grid=(s // ts,), out_shape=(_sds((s, LANE), F32), _sds((LANE, s), F32)),
        in_specs=[pl.BlockSpec((ts, LANE), lambda i: (i, 0)), pl.BlockSpec((1, LANE), lambda i: (0, 0))],
        out_specs=(pl.BlockSpec((ts, LANE), lambda i: (i, 0)), pl.BlockSpec((LANE, ts), lambda i: (0, i))),
        scratch_shapes=[pltpu.VMEM((1, LANE), F32)],
        name="forget_cumsum", compiler_params=_cp(("arbitrary",)))(flog, fb)


def _pick_row(m, idx):
    r = lax.broadcasted_iota(jnp.int32, (m.shape[0], 1), 0)
    return jnp.sum(jnp.where(r == idx, m, 0.0), axis=0, keepdims=True)


def _pick_col(m, idx):
    cidx = lax.broadcasted_iota(jnp.int32, (1, m.shape[1]), 1)
    return jnp.sum(jnp.where(cidx == idx, m, 0.0), axis=1, keepdims=True)


def _split3(x):
    hi = x.astype(BF16)
    r1 = x - hi.astype(F32)
    mid = r1.astype(BF16)
    lo = (r1 - mid.astype(F32)).astype(BF16)
    return hi, mid, lo


def _head_mask(lane, hh):
    lo = lane < HEAD_DIM
    return lo if hh == 0 else jnp.logical_not(lo)


def _attn_prep(k, v, cum):
    s, d = k.shape
    npair = d // LANE
    tc = _tile(s, 512)

    def body(k_ref, v_ref, c_ref, ka_ref, kt_ref, vt_ref):
        p = pl.program_id(0)
        lane = lax.broadcasted_iota(jnp.int32, (1, LANE), 1)
        kk = k_ref[...]
        vv = v_ref[...].astype(F32)
        ckt = c_ref[...]
        for hh in range(2):
            head = _head_mask(lane, hh)
            b = SPARE[hh]
            ck = _pick_col(ckt, 2 * p + hh)
            extra = jnp.where(lane == b + NPIECE, 1.0, 0.0).astype(BF16) + jnp.zeros((tc, LANE), BF16)
            for n_, pc in enumerate(_split3(ck)):
                extra = jnp.where(lane == b + n_, pc, extra)
            ka = jnp.where(head, kk, extra)
            ka_ref[0, hh] = ka
            kt_ref[0, hh] = ka.astype(F32).T.astype(BF16)
            vx = jnp.where(head, vv, jnp.where(lane == b, 1.0, 0.0))
            vt_ref[0, hh] = vx.T.astype(BF16)

    blk = pl.BlockSpec((tc, LANE), lambda p, c: (c, p))
    return pl.pallas_call(
        body, grid=(npair, s // tc),
        out_shape=(_sds((npair, 2, s, LANE), BF16), _sds((npair, 2, LANE, s), BF16), _sds((npair, 2, LANE, s), BF16)),
        in_specs=[blk, blk, pl.BlockSpec((tc, LANE), lambda p, c: (c, 0))],
        out_specs=(pl.BlockSpec((1, 2, tc, LANE), lambda p, c: (p, 0, c, 0)),
                   pl.BlockSpec((1, 2, LANE, tc), lambda p, c: (p, 0, 0, c)),
                   pl.BlockSpec((1, 2, LANE, tc), lambda p, c: (p, 0, 0, c))),
        name="fox_attn_prep", compiler_params=_cp(("parallel", "parallel")))(k, v, cum)


def _q_aug(qq, lane, hh):
    b = SPARE[hh]
    sel = jnp.logical_and(lane >= b, lane < b + NPIECE)
    neg = jnp.full((1, LANE), -1.0, BF16)
    zl = jnp.zeros((1, LANE), BF16)
    return jnp.where(_head_mask(lane, hh), qq, jnp.where(sel, neg, zl))


def _attn_fwd(q, kaug, vtr, cumt):
    s, d = q.shape
    tq = _tile(s, ATT_FWD_TQ)
    tk = _tile(s, ATT_TK)
    npair = d // LANE
    npart = max(1, tq // tk)

    def body(q_ref, ka_ref, vt_ref, cumt_ref, o_ref, lse_ref):
        p = pl.program_id(0)
        i = pl.program_id(1)
        lane = lax.broadcasted_iota(jnp.int32, (1, LANE), 1)
        qq = q_ref[...]
        qx = (_q_aug(qq, lane, 0), _q_aug(qq, lane, 1))
        cqt = cumt_ref[:, pl.ds(pl.multiple_of(i * tq, tq), tq)]
        cq = (_pick_row(cqt, 2 * p), _pick_row(cqt, 2 * p + 1))
        jd = (i * tq) // tk

        def kv_step(j, carry, diag, q_lo=0):
            ks = pl.multiple_of(j * tk, tk)
            nq_ = tq - q_lo
            if diag:
                krow = lax.broadcasted_iota(jnp.int32, (tk, nq_), 0) + j * tk
                qcol = lax.broadcasted_iota(jnp.int32, (tk, nq_), 1) + (i * tq + q_lo)
                causal = krow <= qcol
            out = []
            for hh in range(2):
                m_all, acc_all = carry[2 * hh], carry[2 * hh + 1]
                m, acc, cqh = m_all[:, q_lo:], acc_all[:, q_lo:], cq[hh][:, q_lo:]
                sc = _dot_nt(ka_ref[0, hh, pl.ds(ks, tk), :], qx[hh][q_lo:, :])
                if diag:
                    sc = jnp.where(causal, sc, -jnp.inf)
                mx = jnp.max(sc, axis=0, keepdims=True) + cqh
                mn = jnp.maximum(m, mx)
                alpha = jnp.exp(m - mn)
                pt = jnp.exp(sc + (cqh - mn)).astype(BF16)
                acc = alpha * acc + _dot(vt_ref[0, hh, :, pl.ds(ks, tk)], pt)
                if q_lo:
                    mn = jnp.concatenate([m_all[:, :q_lo], mn], axis=1)
                    acc = jnp.concatenate([acc_all[:, :q_lo], acc], axis=1)
                out += [mn, acc]
            return tuple(out)

        minit = jnp.full((1, tq), -jnp.inf, F32)
        ainit = jnp.zeros((LANE, tq), F32)
        carry = (minit, ainit, minit, ainit)
        for pj in range(npart):
            carry = kv_step(jd + pj, carry, True, q_lo=pj * tk)
        carry = lax.fori_loop(0, jd, lambda j, cr: kv_step(j, cr, False), carry)
        m0, a0, m1, a1 = carry
        l0 = a0[SPARE[0]:SPARE[0] + 1, :]
        l1 = a1[SPARE[1]:SPARE[1] + 1, :]
        row = lax.broadcasted_iota(jnp.int32, (LANE, 1), 0)
        ot = jnp.where(row < HEAD_DIM, a0 / l0, a1 / l1)
        o_ref[...] = ot.T.astype(BF16)
        r8 = lax.broadcasted_iota(jnp.int32, (8, 1), 0)
        lse_ref[0] = jnp.where(r8 == 0, m0 + jnp.log(l0), jnp.where(r8 == 1, m1 + jnp.log(l1), 0.0))

    return pl.pallas_call(
        body, grid=(npair, s // tq), out_shape=(_sds((s, d), BF16), _sds((npair, 8, s), F32)),
        in_specs=[pl.BlockSpec((tq, LANE), lambda p, i: (i, p)),
                  pl.BlockSpec((1, 2, s, LANE), lambda p, i: (p, 0, 0, 0)),
                  pl.BlockSpec((1, 2, LANE, s), lambda p, i: (p, 0, 0, 0)),
                  pl.BlockSpec((N_HEADS, s), lambda p, i: (0, 0))],
        out_specs=(pl.BlockSpec((tq, LANE), lambda p, i: (i, p)), pl.BlockSpec((1, 8, tq), lambda p, i: (p, 0, i))),
        name="fox_attn_fwd", compiler_params=_cp(("parallel", "parallel")))(q, kaug, vtr, cumt)


def _final(x, gain, target):
    s, d = x.shape
    ts = _tile(s, 512)

    def body(x_ref, g_ref, t_ref, lsum_ref, dx_ref, dg_ref):
        @pl.when(pl.program_id(0) == 0)
        def _():
            lsum_ref[...] = jnp.zeros_like(lsum_ref)
            dg_ref[...] = jnp.zeros_like(dg_ref)

        xhat, rstd = _rms_parts(x_ref[...])
        e = xhat * g_ref[...] - t_ref[...]
        lsum_ref[...] += _colsum(e * e)
        dout = e * (1.0 / d)
        dg_ref[...] += _colsum(dout * xhat)
        dxhat = dout * g_ref[...]
        dx_ref[...] = rstd * (dxhat - xhat * jnp.mean(dxhat * xhat, axis=-1, keepdims=True))

    vec = pl.BlockSpec((1, d), lambda i: (0, 0))
    row = pl.BlockSpec((ts, d), lambda i: (i, 0))
    return pl.pallas_call(
        body, grid=(s // ts,), out_shape=(_sds((1, d), F32), _sds((s, d), F32), _sds((1, d), F32)),
        in_specs=[row, vec, row], out_specs=(vec, row, vec),
        name="final_norm_loss", compiler_params=_cp(("arbitrary",)))(x, gain, target)


def _ffn_bwd_act(dx, gate, w_out, ug, uu, name, comm=None):
    s, d = dx.shape
    f = w_out.shape[0]
    n = f // 2
    ts = _tile(s, 512)

    def body(dx_ref, g_ref, w_ref, ug_ref, uu_ref, dug_ref, duu_ref, dys):
        @pl.when(pl.program_id(1) == 0)
        def _():
            dys[...] = (dx_ref[...] * g_ref[...]).astype(BF16)

        dact = _dot_nt(dys[...], w_ref[...])
        g = ug_ref[...].astype(F32)
        u = uu_ref[...].astype(F32)
        sg = _sigmoid(g)
        dug_ref[...] = (dact * u * sg * (1.0 + g * (1.0 - sg))).astype(BF16)
        duu_ref[...] = (dact * g * sg).astype(BF16)

    tile = pl.BlockSpec((ts, n), lambda i, q: (i, q))
    return _hosted_call(
        body, comm, grid=(s // ts, 2), out_shape=(_sds((s, f), BF16), _sds((s, f), BF16)),
        in_specs=[pl.BlockSpec((ts, d), lambda i, q: (i, 0)), pl.BlockSpec((1, d), lambda i, q: (0, 0)),
                  pl.BlockSpec((n, d), lambda i, q: (q, 0)), tile, tile],
        out_specs=(tile, tile), scratch_shapes=[pltpu.VMEM((ts, d), BF16)],
        name=name, sem=("parallel", "arbitrary"), args=(dx, gate, w_out, ug, uu))


def _dw_mm(a, b_list, tk, tn, name, gate=None, wfull=None, dgate_init=None):
    s, kdim = a.shape
    nb1 = b_list[0].shape[1] // tn
    nb = nb1 * len(b_list)
    ts = _tile(s, 1024)
    nk = kdim // tk
    ns = s // ts
    gated = gate is not None

    def body(*refs):
        a_ref = refs[0]
        b_refs = refs[1:1 + len(b_list)]
        rest = refs[1 + len(b_list):]
        if gated:
            g_ref, w_ref, di_ref, o_ref, dg_ref, acc = rest
        else:
            o_ref, acc = rest
        jn, ik, st = pl.program_id(0), pl.program_id(1), pl.program_id(2)

        @pl.when(st == 0)
        def _():
            acc[...] = jnp.zeros_like(acc)

        for mi, b_ref in enumerate(b_refs):
            @pl.when(jn // nb1 == mi)
            def _(b_ref=b_ref):
                acc[...] += _dot_tn(a_ref[...], b_ref[...].astype(BF16))

        if gated:
            @pl.when(jnp.logical_and(ik == 0, st == 0))
            def _():
                dg_ref[...] = di_ref[...]

        @pl.when(st == ns - 1)
        def _():
            if gated:
                o_ref[0] = acc[...] * g_ref[...]
                dg_ref[...] += _colsum(acc[...] * w_ref[...].astype(F32))
            else:
                o_ref[0] = acc[...]

    in_specs = [pl.BlockSpec((ts, tk), lambda jn, ik, st: (st, ik))]
    for mi in range(len(b_list)):
        in_specs.append(pl.BlockSpec(
            (ts, tn), lambda jn, ik, st, mi=mi: (st, jnp.clip(jn - mi * nb1, 0, nb1 - 1))))
    args = [a] + list(b_list)
    out_shape = [_sds((nb, kdim, tn), F32)]
    out_specs = [pl.BlockSpec((1, tk, tn), lambda jn, ik, st: (jn, ik, 0))]
    if gated:
        vec = pl.BlockSpec((1, tn), lambda jn, ik, st: (0, jn))
        in_specs += [vec, pl.BlockSpec((tk, tn), lambda jn, ik, st: (ik, jn)), vec]
        args += [gate, wfull, dgate_init]
        out_shape.append(_sds((1, nb * tn), F32))
        out_specs.append(vec)
    res = pl.pallas_call(
        body, grid=(nb, nk, ns), out_shape=tuple(out_shape), in_specs=in_specs, out_specs=tuple(out_specs),
        scratch_shapes=[pltpu.VMEM((tk, tn), F32)],
        name=name, compiler_params=_cp(("parallel", "arbitrary", "arbitrary")))(*args)
    return res if gated else res[0]


def _mm_normbwd(terms, x, dxres, gain, scale, name, ts_pref=256, comm=None):
    s, d = x.shape
    ts = _tile(s, ts_pref)
    arrs, warrs = [], []
    for a, _, w, _ in terms:
        if not any(a is z for z in arrs):
            arrs.append(a)
        if not any(w is z for z in warrs):
            warrs.append(w)
    ai = [next(i for i, z in enumerate(arrs) if z is a) for a, _, _, _ in terms]
    wi = [next(i for i, z in enumerate(warrs) if z is w) for _, _, w, _ in terms]

    def body(*refs):
        a_refs = refs[:len(arrs)]
        w_refs = refs[len(arrs):len(arrs) + len(warrs)]
        x_ref, dr_ref, g_ref, sc_ref, dx_ref, dsh_ref, dsc_ref, dg_ref = refs[len(arrs) + len(warrs):]

        @pl.when(pl.program_id(0) == 0)
        def _():
            dsh_ref[...] = jnp.zeros_like(dsh_ref)
            dsc_ref[...] = jnp.zeros_like(dsc_ref)
            dg_ref[...] = jnp.zeros_like(dg_ref)

        dh = None
        for ti, (_, c0, w, q) in enumerate(terms):
            n = w.shape[2]
            part = _dot_nt(a_refs[ai[ti]][:, pl.ds(c0, n)], w_refs[wi[ti]][q])
            dh = part if dh is None else dh + part
        xhat, rstd = _rms_parts(x_ref[...])
        nrm = xhat * g_ref[...]
        dsh_ref[...] += _colsum(dh)
        dsc_ref[...] += _colsum(dh * nrm)
        dn = dh * (1.0 + sc_ref[...])
        dg_ref[...] += _colsum(dn * xhat)
        dxhat = dn * g_ref[...]
        dx_ref[...] = dr_ref[...] + rstd * (dxhat - xhat * jnp.mean(dxhat * xhat, axis=-1, keepdims=True))

    vec = pl.BlockSpec((1, d), lambda i: (0, 0))
    row = pl.BlockSpec((ts, d), lambda i: (i, 0))
    in_specs = [pl.BlockSpec((ts, a.shape[1]), lambda i: (i, 0)) for a in arrs]
    in_specs += [_resident(w.shape) for w in warrs]
    in_specs += [row, row, vec, vec]
    return _hosted_call(
        body, comm, grid=(s // ts,), out_shape=(_sds((s, d), F32), _sds((1, d), F32), _sds((1, d), F32), _sds((1, d), F32)),
        in_specs=in_specs, out_specs=(row, vec, vec, vec), scratch_shapes=[],
        name=name, sem=("arbitrary",), args=(*arrs, *warrs, x, dxres, gain, scale))


def _do_kernel(dx, gate, wo, o):
    s, d = dx.shape
    ts = _tile(s, 512)

    def body(dx_ref, g_ref, w_ref, o_ref, do_ref, dl_ref):
        dy = (dx_ref[...] * g_ref[...]).astype(BF16)
        do = _dot_nt(dy, w_ref[...])
        do_ref[...] = do.astype(BF16)
        prod = do * o_ref[...].astype(F32)
        hrow = lax.broadcasted_iota(jnp.int32, (N_HEADS, d), 0)
        hcol = lax.broadcasted_iota(jnp.int32, (N_HEADS, d), 1) // HEAD_DIM
        sel = (hrow == hcol).astype(F32)
        dl_ref[...] = lax.dot_general(sel, prod, (((1,), (1,)), ((), ())), preferred_element_type=F32,
                                      precision=lax.Precision.HIGHEST)

    row = pl.BlockSpec((ts, d), lambda i: (i, 0))
    return pl.pallas_call(
        body, grid=(s // ts,), out_shape=(_sds((s, d), BF16), _sds((N_HEADS, s), F32)),
        in_specs=[row, pl.BlockSpec((1, d), lambda i: (0, 0)), _resident(wo.shape), row],
        out_specs=(row, pl.BlockSpec((N_HEADS, ts), lambda i: (0, i))),
        name="attn_do", compiler_params=_cp(("parallel",)))(dx, gate, wo, o)


def _attn_bwd(q, do, kaug, kaugt, v, cumt, lse, deltat, comm=None):
    s, d = q.shape
    tq = _tile(s, ATT_TQ)
    tk = _tile(s, ATT_TK)
    assert tq in (tk, 2 * tk)
    npair = d // LANE
    nq = s // tq
    nkb = s // tk
    qscale = HEAD_DIM ** -0.5

    def body(q_ref, do_ref, ka_ref, kt_ref, v_ref, cumt_ref, lse_ref, dl_ref,
             dq_ref, dk_ref, dv_ref, dcq_ref, dck_ref, qaug, dom, rowv, dqt):
        p = pl.program_id(0)
        j = pl.program_id(1)
        lane = lax.broadcasted_iota(jnp.int32, (1, LANE), 1)
        lo = lane < HEAD_DIM
        r8 = lax.broadcasted_iota(jnp.int32, (8, 1), 0)

        @pl.when(j == 0)
        def _():
            dqt[...] = jnp.zeros_like(dqt)
            for c in range(nq):
                rows = pl.ds(c * tq, tq)
                qq = q_ref[rows, :]
                dd = do_ref[rows, :]
                cqt = cumt_ref[:, rows]
                dlt = dl_ref[:, rows]
                lst = lse_ref[0, :, rows]
                for hh in range(2):
                    qaug[hh, rows, :] = _q_aug(qq, lane, hh)
                    dom[hh, rows, :] = jnp.where(_head_mask(lane, hh), dd, jnp.zeros_like(dd))
                    rowv[hh, :, rows] = jnp.where(
                        r8 == 0, _pick_row(cqt, 2 * p + hh) - lst[hh:hh + 1, :],
                        jnp.where(r8 == 1, _pick_row(dlt, 2 * p + hh), 0.0))

        vv = v_ref[...]
        i0 = (j * tk) // tq

        def q_step(qs, nq_, carry, diag):
            dv_acc, dk0, dk1 = carry
            qs = pl.multiple_of(qs, tk)
            if diag:
                krow = lax.broadcasted_iota(jnp.int32, (tk, nq_), 0) + j * tk
                qcol = lax.broadcasted_iota(jnp.int32, (tk, nq_), 1) + qs
                causal = krow <= qcol
            dks = [dk0, dk1]
            for hh in range(2):
                rv = rowv[hh, :, pl.ds(qs, nq_)]
                qa = qaug[hh, pl.ds(qs, nq_), :]
                dh = dom[hh, pl.ds(qs, nq_), :]
                sc = _dot_nt(ka_ref[0, hh], qa)
                if diag:
                    sc = jnp.where(causal, sc, -jnp.inf)
                pt = jnp.exp(sc + rv[0:1, :])
                dpt = _dot_nt(vv, dh)
                dst = (pt * (dpt - rv[1:2, :])).astype(BF16)
                dv_acc = dv_acc + _dot(pt.astype(BF16), dh)
                dks[hh] = dks[hh] + _dot(dst, qa)
                dqt[hh, :, pl.ds(qs, nq_)] += _dot(kt_ref[0, hh], dst)
            return dv_acc, dks[0], dks[1]

        z = jnp.zeros((tk, LANE), F32)
        first = ((j * tk) % tq == 0).astype(jnp.int32)
        carry = lax.fori_loop(0, first, lambda _, cr: q_step(i0 * tq, tq, cr, True), (z, z, z))
        if tq > tk:
            carry = lax.fori_loop(0, 1 - first, lambda _, cr: q_step(j * tk, tq - tk, cr, True), carry)
        dv_acc, dk0, dk1 = lax.fori_loop(i0 + 1, nq, lambda i, cr: q_step(i * tq, tq, cr, False), carry)
        dv_ref[...] = dv_acc.astype(BF16)
        dk_ref[...] = jnp.where(lo, dk0, dk1).astype(BF16)
        dck_ref[0] = jnp.where(r8 == 0, dk0.T[SPARE[0]:SPARE[0] + 1, :],
                               jnp.where(r8 == 1, dk1.T[SPARE[1]:SPARE[1] + 1, :], 0.0))

        @pl.when(j == nkb - 1)
        def _():
            for c in range(nq):
                rows = pl.ds(c * tq, tq)
                a0 = dqt[0, :, rows].T
                a1 = dqt[1, :, rows].T
                dq_ref[rows, :] = (jnp.where(lo, a0, a1) * qscale).astype(BF16)
            r0, r1 = SPARE[0] + NPIECE, SPARE[1] + NPIECE
            dcq_ref[0] = jnp.where(r8 == 0, dqt[0, r0:r0 + 1, :], jnp.where(r8 == 1, dqt[1, r1:r1 + 1, :], 0.0))

    col = pl.BlockSpec((s, LANE), lambda p, j: (0, p), pipeline_mode=pl.Buffered(1))
    rows16 = pl.BlockSpec((N_HEADS, s), lambda p, j: (0, 0), pipeline_mode=pl.Buffered(1))
    blk = pl.BlockSpec((tk, LANE), lambda p, j: (j, p))
    return _hosted_call(
        body, comm, grid=(npair, nkb),
        out_shape=(_sds((s, d), BF16), _sds((s, d), BF16), _sds((s, d), BF16), _sds((npair, 8, s), F32), _sds((npair, 8, s), F32)),
        in_specs=[col, col, pl.BlockSpec((1, 2, tk, LANE), lambda p, j: (p, 0, j, 0)),
                  pl.BlockSpec((1, 2, LANE, tk), lambda p, j: (p, 0, 0, j)), blk, rows16,
                  pl.BlockSpec((1, 8, s), lambda p, j: (p, 0, 0), pipeline_mode=pl.Buffered(1)), rows16],
        out_specs=(pl.BlockSpec((s, LANE), lambda p, j: (0, p)), blk, blk,
                   pl.BlockSpec((1, 8, s), lambda p, j: (p, 0, 0)), pl.BlockSpec((1, 8, tk), lambda p, j: (p, 0, j))),
        scratch_shapes=[pltpu.VMEM((2, s, LANE), BF16), pltpu.VMEM((2, s, LANE), BF16), pltpu.VMEM((2, 8, s), F32),
                        pltpu.VMEM((2, LANE, s), F32)],
        name="fox_attn_bwd", sem=("arbitrary", "arbitrary"), vmem_mb=ATT_BWD_VMEM_MB,
        args=(q, do, kaug, kaugt, v, cumt, lse, deltat))


def _cumsum_bwd(dcq, dck, flog, fb):
    s = flog.shape[0]
    ts = _tile(s, 256)
    nt = s // ts

    def body(dq_ref, dk_ref, f_ref, b_ref, df_ref, db_ref, carry):
        @pl.when(pl.program_id(0) == 0)
        def _():
            carry[...] = jnp.zeros_like(carry)
            db_ref[...] = jnp.zeros_like(db_ref)

        r = lax.broadcasted_iota(jnp.int32, (ts, ts), 0)
        cidx = lax.broadcasted_iota(jnp.int32, (ts, ts), 1)
        tri = (r >= cidx).astype(F32)
        dct = dq_ref[...] + dk_ref[...]
        dlst = jnp.dot(dct, tri, preferred_element_type=F32, precision=lax.Precision.HIGHEST) + carry[...]
        carry[...] = dlst[:, 0:1]
        dls = jnp.concatenate([dlst, jnp.zeros((LANE - N_HEADS, ts), F32)], axis=0).T
        z = f_ref[...] + b_ref[...]
        df = dls * (1.0 / (1.0 + jnp.exp(z)))
        db_ref[...] += _colsum(df)
        df_ref[...] = df.astype(BF16)

    rev = pl.BlockSpec((ts, LANE), lambda i: (nt - 1 - i, 0))
    revt = pl.BlockSpec((N_HEADS, ts), lambda i: (0, nt - 1 - i))
    vec = pl.BlockSpec((1, LANE), lambda i: (0, 0))
    return pl.pallas_call(
        body, grid=(nt,), out_shape=(_sds((s, LANE), BF16), _sds((1, LANE), F32)),
        in_specs=[revt, revt, rev, vec], out_specs=(rev, vec), scratch_shapes=[pltpu.VMEM((N_HEADS, 1), F32)],
        name="forget_cumsum_bwd", compiler_params=_cp(("arbitrary",)))(dcq, dck, flog, fb)


def _conv_bwd1(dx, gate, w_out, b_out, dwo, lng, lnb):
    s, d = dx.shape
    ts = _tile(s, 512)
    ns = s // ts

    def body(dx_ref, g_ref, w_ref, bo_ref, y_ref, lg_ref, lb_ref, dd_ref, dlg_ref, dlb_ref, dbd_ref, dbo_ref, dge_ref, cs):
        i = pl.program_id(0)

        @pl.when(i == 0)
        def _():
            for r in (dlg_ref, dlb_ref, dbd_ref, cs):
                r[...] = jnp.zeros_like(r)

        dxv = dx_ref[...]
        cs[...] += _colsum(dxv)
        dsw = _dot_nt((dxv * g_ref[...]).astype(BF16), w_ref[...])
        yv = y_ref[...]
        mu = jnp.mean(yv, axis=-1, keepdims=True)
        yc = yv - mu
        rstd = lax.rsqrt(jnp.mean(yc * yc, axis=-1, keepdims=True) + EPS)
        xhat = yc * rstd
        ln = xhat * lg_ref[...] + lb_ref[...]
        sg = _sigmoid(ln)
        dln = dsw * (sg * (1.0 + ln * (1.0 - sg)))
        dlg_ref[...] += _colsum(dln * xhat)
        dlb_ref[...] += _colsum(dln)
        dxh = dln * lg_ref[...]
        dd = rstd * (dxh - jnp.mean(dxh, axis=-1, keepdims=True) - xhat * jnp.mean(dxh * xhat, axis=-1, keepdims=True))
        dbd_ref[...] += _colsum(dd)
        dd_ref[...] = dd

        @pl.when(i == ns - 1)
        def _():
            dbo_ref[...] = g_ref[...] * cs[...]
            dge_ref[...] = bo_ref[...] * cs[...]

    vec = pl.BlockSpec((1, d), lambda i: (0, 0))
    row = pl.BlockSpec((ts, d), lambda i: (i, 0))
    return pl.pallas_call(
        body, grid=(ns,), out_shape=(_sds((s, d), F32),) + tuple(_sds((1, d), F32) for _ in range(5)),
        in_specs=[row, vec, _resident(w_out.shape), vec, row, vec, vec], out_specs=(row, vec, vec, vec, vec, vec),
        scratch_shapes=[pltpu.VMEM((1, d), F32)],
        name="conv_bwd_ln", compiler_params=_cp(("arbitrary",)))(dx, gate, w_out, b_out, dwo, lng, lnb)


def _dwconv_bwd(ddwo, glu, a_s, g_s, wdw, comm=None):
    s, d = ddwo.shape
    ts = _tile(s, 256)
    ns = s // ts
    rb, cb = 32, 256
    nrb = ts // rb

    def body(dd_ref, ddn_ref, gl_ref, glh_ref, a_ref, g_ref, w_ref, da_ref, dg_ref, dw_ref, sa_ref, sg_ref, bufd, bufg, dws,
             shd, shg):
        i = pl.program_id(0)

        @pl.when(i == 0)
        def _():
            dws[...] = jnp.zeros_like(dws)
            sa_ref[...] = jnp.zeros_like(sa_ref)
            sg_ref[...] = jnp.zeros_like(sg_ref)
            bufg[pl.ds(0, HALO), :] = jnp.zeros((HALO, d), F32)

        @pl.when(i > 0)
        def _():
            bufg[pl.ds(0, HALO), :] = glh_ref[...]

        bufg[pl.ds(HALO, ts), :] = gl_ref[...]
        bufd[pl.ds(0, ts), :] = dd_ref[...]

        @pl.when(i == ns - 1)
        def _():
            bufd[pl.ds(ts, HALO), :] = jnp.zeros((HALO, d), F32)

        @pl.when(i < ns - 1)
        def _():
            bufd[pl.ds(ts, HALO), :] = ddn_ref[...]

        _shift_copies(bufd, shd)
        _shift_copies(bufg, shg)
        for cc in range(d // cb):
            cs = pl.ds(cc * cb, cb)
            for r in range(nrb):
                acc = jnp.zeros((rb, cb), F32)
                for k in range(CONV_K):
                    acc = acc + w_ref[pl.ds(k, 1), cs] * _shifted(bufd, shd, r * rb + (CONV_K - 1) - k, rb, cs)
                rows = pl.ds(r * rb, rb)
                av = a_ref[rows, cs].astype(F32)
                sg = _sigmoid(g_ref[rows, cs].astype(F32))
                dav = acc * sg
                dgv = acc * av * sg * (1.0 - sg)
                da_ref[rows, cs] = dav.astype(BF16)
                dg_ref[rows, cs] = dgv.astype(BF16)
                sa_ref[:, cs] += _colsum(dav)
                sg_ref[:, cs] += _colsum(dgv)
            for k in range(CONV_K):
                acc8 = jnp.zeros((8, cb), F32)
                for r in range(nrb):
                    prod = bufd[pl.ds(r * rb, rb), cs] * _shifted(bufg, shg, HALO - (CONV_K - 1) + k + r * rb, rb, cs)
                    acc8 = acc8 + (prod[0:8] + prod[8:16]) + (prod[16:24] + prod[24:32])
                dws[pl.ds(8 * k, 8), cs] += acc8

        @pl.when(i == ns - 1)
        def _():
            dw_ref[...] = jnp.zeros_like(dw_ref)
            for k in range(CONV_K):
                dw_ref[pl.ds(k, 1), :] = _colsum(dws[pl.ds(8 * k, 8), :])

    row = pl.BlockSpec((ts, d), lambda i: (i, 0))
    vec = pl.BlockSpec((1, d), lambda i: (0, 0))
    hb = ts // HALO
    return _hosted_call(
        body, comm, grid=(ns,),
        out_shape=(_sds((s, d), BF16), _sds((s, d), BF16), _sds((HALO, d), F32), _sds((1, d), F32), _sds((1, d), F32)),
        in_specs=[row, pl.BlockSpec((HALO, d), lambda i: (jnp.minimum((i + 1) * hb, ns * hb - 1), 0)),
                  row, pl.BlockSpec((HALO, d), lambda i: (jnp.maximum(i * hb - 1, 0), 0)),
                  row, row, pl.BlockSpec((HALO, d), lambda i: (0, 0))],
        out_specs=(row, row, pl.BlockSpec((HALO, d), lambda i: (0, 0)), vec, vec),
        scratch_shapes=[pltpu.VMEM((ts + HALO, d), F32), pltpu.VMEM((HALO + ts, d), F32), pltpu.VMEM((8 * HALO, d), F32),
                        pltpu.VMEM((SUBLANES - 1, HALO + ts - SUBLANES, d), F32),
                        pltpu.VMEM((SUBLANES - 1, HALO + ts - SUBLANES, d), F32)],
        name="dwconv_bwd", sem=("arbitrary",), args=(ddwo, ddwo, glu, glu, a_s, g_s, wdw))


def _ada_wgrad(cat, da, name):
    nl, _, n = da.shape
    d = cat.shape[0]
    tn = 256

    def body(c_ref, d_ref, o_ref):
        acc = c_ref[:, 0:1] * d_ref[0, 0:1, :]
        for r in range(1, 8):
            acc = acc + c_ref[:, r:r + 1] * d_ref[0, r:r + 1, :]
        o_ref[0] = acc

    return pl.pallas_call(
        body, grid=(nl, n // tn), out_shape=_sds((nl, d, n), F32),
        in_specs=[pl.BlockSpec((d, 8), lambda l, j: (0, 0)), pl.BlockSpec((1, 8, tn), lambda l, j: (l, 0, j))],
        out_specs=pl.BlockSpec((1, d, tn), lambda l, j: (l, 0, j)),
        name=name, compiler_params=_cp(("parallel", "parallel")))(cat, da)


def _silu_rows(c_all):
    def body(c_ref, o_ref):
        cc = c_ref[...]
        o_ref[...] = cc * _sigmoid(cc)

    return pl.pallas_call(body, out_shape=_sds(c_all.shape, F32), name="silu_c")(c_all)


def _adamw(w, g, m, v, name):
    r, c = w.shape
    tr = r
    for cand in (512, 256, 128, 64, 32, 16, 8):
        if r % cand == 0 and cand * c * 4 <= (1 << 20):
            tr = cand
            break
    bc1 = 1.0 - ADAM_B1 ** ADAM_STEP
    bc2 = 1.0 - ADAM_B2 ** ADAM_STEP

    def body(w_ref, g_ref, m_ref, v_ref, d_ref, nm_ref, nv_ref):
        gv = g_ref[...]
        mn = ADAM_B1 * m_ref[...] + (1.0 - ADAM_B1) * gv
        vn = ADAM_B2 * v_ref[...] + (1.0 - ADAM_B2) * (gv * gv)
        mh = mn / bc1
        vh = vn / bc2
        d_ref[...] = -ADAM_LR * (mh / (jnp.sqrt(vh) + ADAM_EPS) + ADAM_WD * w_ref[...])
        nm_ref[...] = mn
        nv_ref[...] = vn

    blk = pl.BlockSpec((tr, c), lambda i: (i, 0))
    return pl.pallas_call(
        body, grid=(r // tr,), out_shape=tuple(_sds((r, c), F32) for _ in range(3)),
        in_specs=[blk, blk, blk, blk], out_specs=(blk, blk, blk),
        name=name, compiler_params=_cp(("parallel",)))(w, g, m, v)


def _pad_rows(a, rows, axis):
    pad = [(0, 0)] * a.ndim
    pad[axis] = (0, rows - a.shape[axis])
    return jnp.pad(a, pad)


def _vec(a):
    return a.reshape(1, -1)


def kernel(x, c, mix_norm_g, mix_ada_w, mix_ada_b, ffn_norm_g, ffn_ada_w, ffn_ada_b, ffn_w_in, ffn_w_out, conv_w_in, conv_b_in, conv_w_dw, conv_b_dw, conv_ln_g, conv_ln_b, conv_w_out, conv_b_out, kv_norm_g, kv_ada_w, kv_ada_b, kv_w, forget_b, attn_w_q, attn_w_o, final_norm_g, loss_target, m_mix_norm_g, m_mix_ada_w, m_mix_ada_b, m_ffn_norm_g, m_ffn_ada_w, m_ffn_ada_b, m_ffn_w_in, m_ffn_w_out, m_conv_w_in, m_conv_b_in, m_conv_w_dw, m_conv_b_dw, m_conv_ln_g, m_conv_ln_b, m_conv_w_out, m_conv_b_out, m_kv_norm_g, m_kv_ada_w, m_kv_ada_b, m_kv_w, m_forget_b, m_attn_w_q, m_attn_w_o, m_final_norm_g, v_mix_norm_g, v_mix_ada_w, v_mix_ada_b, v_ffn_norm_g, v_ffn_ada_w, v_ffn_ada_b, v_ffn_w_in, v_ffn_w_out, v_conv_w_in, v_conv_b_in, v_conv_w_dw, v_conv_b_dw, v_conv_ln_g, v_conv_ln_b, v_conv_w_out, v_conv_b_out, v_kv_norm_g, v_kv_ada_w, v_kv_ada_b, v_kv_w, v_forget_b, v_attn_w_q, v_attn_w_o, v_final_norm_g):
    xi, yi, ci = lax.axis_index("x"), lax.axis_index("y"), lax.axis_index("c")
    chip = 2 * xi + yi
    dev = 4 * xi + 2 * yi + ci
    s, d = x.shape[1], x.shape[2]
    f = ffn_w_out.shape[1] * 4
    x0 = x[0]
    nkv = kv_w.shape[1]
    nkv_all = 4 * nkv

    wdw_loc = _pad_rows(conv_w_dw[0], HALO, 0)
    small = jnp.concatenate([c.reshape(-1), conv_b_in.reshape(-1), wdw_loc.reshape(-1), conv_b_dw.reshape(-1),
                             conv_ln_g.reshape(-1), conv_ln_b.reshape(-1), conv_b_out.reshape(-1)])
    n_small = small.shape[0]
    w_small = -(-n_small // (8 * LANE)) * LANE
    small = jnp.pad(small, (0, 8 * w_small - n_small)).reshape(8, w_small)
    small_all = _allgather8(small, "ag_small_params", True).reshape(8, 8 * w_small)
    c_all = small_all[:, :d]
    per_chip = small_all[0::2]
    dq_ = d // 4
    o1 = d
    b_in_full = per_chip[:, o1:o1 + 2 * dq_].reshape(4, 1, 2 * dq_)
    o1 += 2 * dq_
    wdw_full = per_chip[:, o1:o1 + HALO * dq_].reshape(4, HALO, dq_).transpose(1, 0, 2).reshape(HALO, d)
    o1 += HALO * dq_
    bdw_full = per_chip[:, o1:o1 + dq_].reshape(1, d)
    lng_full = per_chip[:, o1 + dq_:o1 + 2 * dq_].reshape(1, d)
    lnb_full = per_chip[:, o1 + 2 * dq_:o1 + 3 * dq_].reshape(1, d)
    bout_full = per_chip[:, o1 + 3 * dq_:o1 + 4 * dq_].reshape(1, d)

    a_mix = _ada_fwd(c_all, mix_ada_w, "ada_mix")
    a_ffn = _ada_fwd(c_all, ffn_ada_w, "ada_ffn")
    a_kv = _ada_fwd(c_all, kv_ada_w[None], "ada_kv")
    n3 = mix_ada_w.shape[2]
    n2 = kv_ada_w.shape[1]
    ada_loc = jnp.concatenate([a_mix[0], a_mix[1], a_ffn[0], a_ffn[1], a_kv[0]], axis=1)
    w_ada = ada_loc.shape[1]
    ada_all = _allgather8(ada_loc, "ag_ada", True).reshape(8, 8, w_ada)
    ada_me = lax.dynamic_index_in_dim(ada_all, dev, axis=1, keepdims=False)[0::2]

    def ada_vec(off, n, bias):
        return ada_me[:, off:off + n].reshape(1, 4 * n) + bias.reshape(1, -1)

    ada_m0 = ada_vec(0, n3, mix_ada_b[0])
    ada_m1 = ada_vec(n3, n3, mix_ada_b[1])
    ada_f0 = ada_vec(2 * n3, n3, ffn_ada_b[0])
    ada_f1 = ada_vec(3 * n3, n3, ffn_ada_b[1])
    ada_k = ada_vec(4 * n3, n2, kv_ada_b)

    def split3(a):
        return a[:, :d], a[:, d:2 * d], a[:, 2 * d:3 * d]

    sh_m0, sc_m0, gt_m0 = split3(ada_m0)
    sh_m1, sc_m1, gt_m1 = split3(ada_m1)
    sh_f0, sc_f0, gt_f0 = split3(ada_f0)
    sh_f1, sc_f1, gt_f1 = split3(ada_f1)
    sh_k, sc_k = ada_k[:, :d], ada_k[:, d:2 * d]

    def my_halves(ws):
        return [lax.dynamic_index_in_dim(w.astype(BF16).reshape(2, w.shape[0] // 2, w.shape[1]), ci, axis=0, keepdims=False)
                for w in ws]

    def whole(gath, ws):
        return [g.reshape(4, w.shape[0], w.shape[1]) for g, w in zip(gath, ws)]

    grp_conv = [conv_w_in[0], conv_w_out[0]]
    grp_ffn0 = [ffn_w_in[0], ffn_w_out[0]]
    grp_rest = [ffn_w_in[1], ffn_w_out[1], kv_w, attn_w_q[0], attn_w_o[0]]
    mine_conv, mine_ffn0, mine_rest = my_halves(grp_conv), my_halves(grp_ffn0), my_halves(grp_rest)
    cw_in, cw_out = whole(_exchange(_Gather8(mine_conv), mine_conv, "ag_w_conv"), grp_conv)
    cw_out = cw_out.reshape(d, d)

    zero_b = jnp.zeros((1, d), F32)
    g_m0, g_m1 = _vec(mix_norm_g[0]), _vec(mix_norm_g[1])
    g_f0, g_f1 = _vec(ffn_norm_g[0]), _vec(ffn_norm_g[1])
    g_k, g_fin = _vec(kv_norm_g), _vec(final_norm_g)
    fb = jnp.pad(forget_b, (0, LANE - N_HEADS)).reshape(1, LANE)

    h0, glu, a_s, g_s, gath_ffn0 = _in_pair(x0, g_m0, sh_m0, sc_m0, cw_in, b_in_full, True, "conv_in",
                                            comm=(_Gather8(mine_ffn0), mine_ffn0))
    dwo, sw, gath_rest = _dwconv_fwd(glu, wdw_full, bdw_full, lng_full, lnb_full, comm=(_Gather8(mine_rest), mine_rest))
    w_in0, w_out0 = whole(gath_ffn0, grp_ffn0)
    w_in1, w_out1, kvw, wq, wo = whole(gath_rest, grp_rest)
    w_in = [w_in0, w_in1]
    w_out = [w_out0.reshape(f, d), w_out1.reshape(f, d)]
    kvw = kvw.transpose(1, 0, 2).reshape(d, nkv_all)
    wk, wv = kvw[:, :d], kvw[:, d:2 * d]
    wf = jnp.pad(kvw[:, 2 * d:], ((0, 0), (0, LANE - N_HEADS)))
    wq, wo = wq.reshape(d, d), wo.reshape(d, d)
    x1 = _mm_res(sw, cw_out, bout_full, gt_m0, x0, "conv_out")
    hf0, act0, ug0, uu0 = _in_pair(x1, g_f0, sh_f0, sc_f0, w_in[0], None, False, "ffn0_in")
    x2 = _mm_res(act0, w_out[0], zero_b, gt_f0, x1, "ffn0_out")
    hk, h1, kk, vv, qq, flog = _qkv(x2, (g_k, sh_k, sc_k), (g_m1, sh_m1, sc_m1), wk, wv, wf, wq)
    cum, cumt = _cumsum_fwd(flog, fb)
    kaug, kaugt, vtr = _attn_prep(kk, vv, cum)
    o, lse = _attn_fwd(qq, kaug, vtr, cumt)
    x3 = _mm_res(o, wo, zero_b, gt_m1, x2, "attn_out")
    hf1, act1, ug1, uu1 = _in_pair(x3, g_f1, sh_f1, sc_f1, w_in[1], None, False, "ffn1_in")
    x4 = _mm_res(act1, w_out[1], zero_b, gt_f1, x3, "ffn1_out")
    lsum, dx4, d_gfin = _final(x4, g_fin, loss_target[0])
    loss = lax.psum(0.5 / d * jnp.sum(lsum), ("x", "y", "c"))

    nf = f // 2

    sel = jnp.stack([ci, chip]).astype(jnp.int32)

    def reduce_begin(gs, tag):
        ps = [g.reshape(4, 2, g.shape[1] // 2, g.shape[2]) for g in gs]
        lands = _exchange(_SwapHalves(ps), ps, tag + "_swap")
        pairs = [_add_halves(p_, l_, sel, f"{tag}_add{k}") for k, (p_, l_) in enumerate(zip(ps, lands))]
        return [q for q, _ in pairs], [o_ for _, o_ in pairs]

    def reduce_sum(owns, lands, tag):
        return [_add_chips(o_, l_, sel, f"{tag}_sum{k}") for k, (o_, l_) in enumerate(zip(owns, lands))]

    def ffn_bwd(dx_out, x_in, hf, act, ug, uu, gain, scale, gate, w_in_l, w_out_l, tag, comm=None):
        res = _ffn_bwd_act(dx_out, gate, w_out_l, ug, uu, tag + "_bwd_act", comm=comm)
        dug, duu = res[0], res[1]
        dw_out, dgate = _dw_mm(act, [dx_out], nf, d, tag + "_dw_out", gate=gate, wfull=w_out_l, dgate_init=zero_b)
        terms = [(dug, 0, w_in_l, 0), (dug, nf, w_in_l, 1), (duu, 0, w_in_l, 2), (duu, nf, w_in_l, 3)]
        dx_in, dsh, dsc, dgn = _mm_normbwd(terms, x_in, dx_out, gain, scale, tag + "_bwd_in")
        dw_in = _dw_mm(hf, [dug, duu], d, nf, tag + "_dw_in")
        return dx_in, dw_in, dw_out[0], dsh, dsc, dgate, dgn, (res[2] if comm is not None else None)

    dx3, dw_in1, dw_out1, dsh_f1, dsc_f1, dgt_f1, dgn_f1, _ = ffn_bwd(dx4, x3, hf1, act1, ug1, uu1, g_f1, sc_f1, gt_f1, w_in[1], w_out[1], "ffn1")
    q16_1, own_1 = reduce_begin([dw_in1, dw_out1.reshape(4, f // 4, d)], "rs_ffn1")

    do, deltat = _do_kernel(dx3, gt_m1, wo, o)
    dwo_att, dgt_m1 = _dw_mm(o, [dx3], d, d, "attn_dw_o", gate=gt_m1, wfull=wo, dgate_init=zero_b)
    dq, dk, dv, dcq, dck, land_1 = _attn_bwd(qq, do, kaug, kaugt, vv, cumt, lse, deltat, comm=(_ScatterChips(q16_1), q16_1))
    wq3 = wq.reshape(1, d, d)
    dx2a, dsh_m1, dsc_m1, dgn_m1 = _mm_normbwd([(dq, 0, wq3, 0)], x2, dx3, g_m1, sc_m1, "attn_bwd_q")
    dwq = _dw_mm(h1, [dq], d, d, "attn_dw_q")[0]

    df, dfb = _cumsum_bwd(dcq[:, :2].reshape(N_HEADS, s), dck[:, :2].reshape(N_HEADS, s), flog, fb)
    terms = [(dk, 0, wk.reshape(1, d, d), 0), (dv, 0, wv.reshape(1, d, d), 0), (df, 0, wf.reshape(1, d, LANE), 0)]
    dx2, dsh_k, dsc_k, dgn_k = _mm_normbwd(terms, x2, dx2a, g_k, sc_k, "kv_bwd")
    dwk = _dw_mm(hk, [dk], d, d, "kv_dw_k")[0]
    dwv = _dw_mm(hk, [dv], d, d, "kv_dw_v")[0]
    dwf = _dw_mm(hk, [df], d, LANE, "kv_dw_f")[0]
    dkvw = jnp.concatenate([dwk, dwv, dwf[:, :N_HEADS]], axis=1)
    dkvw = dkvw.reshape(d, 4, nkv).transpose(1, 0, 2)

    q16_2, own_2 = reduce_begin([dkvw, dwq.reshape(4, d // 4, d), dwo_att[0].reshape(4, d // 4, d)], "rs_attn")
    dx1, dw_in0, dw_out0, dsh_f0, dsc_f0, dgt_f0, dgn_f0, land_2 = ffn_bwd(
        dx2, x1, hf0, act0, ug0, uu0, g_f0, sc_f0, gt_f0, w_in[0], w_out[0], "ffn0", comm=(_ScatterChips(q16_2), q16_2))
    q16_3, own_3 = reduce_begin([dw_in0, dw_out0.reshape(4, f // 4, d)], "rs_ffn0")

    ddwo, d_lng, d_lnb, d_bdw, d_bout, dgt_extra = _conv_bwd1(dx1, gt_m0, cw_out, bout_full, dwo, lng_full, lnb_full)
    dcw_out, dgt_m0 = _dw_mm(sw, [dx1], d, d, "conv_dw_out", gate=gt_m0, wfull=cw_out, dgate_init=dgt_extra)
    da, dg, d_wdw, d_bin_a, d_bin_g, land_3 = _dwconv_bwd(ddwo, glu, a_s, g_s, wdw_full, comm=(_ScatterChips(q16_3), q16_3))
    nc = cw_in.shape[2]
    terms = [(da, 0, cw_in, 0), (da, nc, cw_in, 1), (dg, 0, cw_in, 2), (dg, nc, cw_in, 3)]
    dx0, dsh_m0, dsc_m0, dgn_m0 = _mm_normbwd(terms, x0, dx1, g_m0, sc_m0, "conv_bwd_in")
    dcw_in = _dw_mm(h0, [da, dg], d, nc, "conv_dw_in")
    q16_4, own_4 = reduce_begin([dcw_in, dcw_out[0].reshape(4, d // 4, d)], "rs_conv")
    land_4 = _exchange(_ScatterChips(q16_4), q16_4, "rs_conv_scatter")

    sums = (reduce_sum(own_1, land_1, "rs_ffn1") + reduce_sum(own_2, land_2, "rs_attn")
            + reduce_sum(own_3, land_3, "rs_ffn0") + reduce_sum(own_4, land_4, "rs_conv"))
    reduced = [b.reshape(2 * b.shape[1], b.shape[2]) for b in _share_halves(sums)]
    g_w_in1, g_w_out1, g_kvw, g_wq, g_wo, g_w_in0, g_w_out0, g_cw_in, g_cw_out = reduced

    d_ada = [jnp.concatenate([dsh_m0, dsc_m0, dgt_m0], axis=1), jnp.concatenate([dsh_m1, dsc_m1, dgt_m1], axis=1),
             jnp.concatenate([dsh_f0, dsc_f0, dgt_f0], axis=1), jnp.concatenate([dsh_f1, dsc_f1, dgt_f1], axis=1),
             jnp.concatenate([dsh_k, dsc_k], axis=1)]
    fields = d_ada + [dgn_m0, dgn_m1, dgn_f0, dgn_f1, dgn_k, d_gfin, d_bin_a, d_bin_g, d_bdw, d_lng, d_lnb, d_bout,
                      d_wdw.reshape(1, -1), dfb]
    foffs = [0]
    for fl in fields:
        foffs.append(foffs[-1] + fl.shape[1])
    n_row = foffs[-1]
    w_row = -(-n_row // (8 * LANE)) * LANE
    row = jnp.pad(jnp.concatenate(fields, axis=1), ((0, 0), (0, 8 * w_row - n_row))).reshape(8, w_row)
    rows_all = _allgather8(row, "ag_small_grads", True).reshape(8, 8, w_row)
    rsum_small = _sum8(rows_all).reshape(1, 8 * w_row)
    rows_flat = rows_all.reshape(8, 8 * w_row)

    def fsum(i):
        return rsum_small[:, foffs[i]:foffs[i + 1]]

    cat = _silu_rows(c_all).T

    def ada_cols(i, n):
        full = rows_flat[:, foffs[i]:foffs[i + 1]].reshape(8, 4, n)
        return lax.dynamic_index_in_dim(full, chip, axis=1, keepdims=False)

    g_mix_ada_w = _ada_wgrad(cat, jnp.stack([ada_cols(0, n3), ada_cols(1, n3)]), "ada_mix_wgrad")
    g_ffn_ada_w = _ada_wgrad(cat, jnp.stack([ada_cols(2, n3), ada_cols(3, n3)]), "ada_ffn_wgrad")
    g_kv_ada_w = _ada_wgrad(cat, ada_cols(4, n2)[None], "ada_kv_wgrad")[0]

    def my_cols(v, n):
        return lax.dynamic_index_in_dim(v.reshape(4, n), chip, axis=0, keepdims=False)

    grads = {
        "mix_norm_g": jnp.concatenate([fsum(5), fsum(6)], axis=0),
        "mix_ada_w": g_mix_ada_w,
        "mix_ada_b": jnp.concatenate([fsum(0), fsum(1)], axis=0),
        "ffn_norm_g": jnp.concatenate([fsum(7), fsum(8)], axis=0),
        "ffn_ada_w": g_ffn_ada_w,
        "ffn_ada_b": jnp.concatenate([fsum(2), fsum(3)], axis=0),
        "ffn_w_in": jnp.stack([g_w_in0, g_w_in1]),
        "ffn_w_out": jnp.stack([g_w_out0, g_w_out1]),
        "conv_w_in": g_cw_in[None],
        "conv_b_in": my_cols(jnp.concatenate([fsum(11), fsum(12)], axis=1), 2 * dq_)[None],
        "conv_w_dw": lax.dynamic_index_in_dim(fsum(17).reshape(HALO, 4, dq_), chip, axis=1, keepdims=False)[:CONV_K][None],
        "conv_b_dw": my_cols(fsum(13), dq_)[None],
        "conv_ln_g": my_cols(fsum(14), dq_)[None],
        "conv_ln_b": my_cols(fsum(15), dq_)[None],
        "conv_w_out": g_cw_out[None],
        "conv_b_out": my_cols(fsum(16), dq_)[None],
        "kv_norm_g": fsum(9).reshape(-1),
        "kv_ada_w": g_kv_ada_w,
        "kv_ada_b": fsum(4).reshape(-1),
        "kv_w": g_kvw,
        "forget_b": fsum(18).reshape(-1)[:N_HEADS],
        "attn_w_q": g_wq[None],
        "attn_w_o": g_wo[None],
        "final_norm_g": fsum(10).reshape(-1),
    }
    weights = dict(mix_norm_g=mix_norm_g, mix_ada_w=mix_ada_w, mix_ada_b=mix_ada_b, ffn_norm_g=ffn_norm_g, ffn_ada_w=ffn_ada_w, ffn_ada_b=ffn_ada_b, ffn_w_in=ffn_w_in, ffn_w_out=ffn_w_out, conv_w_in=conv_w_in, conv_b_in=conv_b_in, conv_w_dw=conv_w_dw, conv_b_dw=conv_b_dw, conv_ln_g=conv_ln_g, conv_ln_b=conv_ln_b, conv_w_out=conv_w_out, conv_b_out=conv_b_out, kv_norm_g=kv_norm_g, kv_ada_w=kv_ada_w, kv_ada_b=kv_ada_b, kv_w=kv_w, forget_b=forget_b, attn_w_q=attn_w_q, attn_w_o=attn_w_o, final_norm_g=final_norm_g)
    moms = dict(mix_norm_g=(m_mix_norm_g, v_mix_norm_g), mix_ada_w=(m_mix_ada_w, v_mix_ada_w), mix_ada_b=(m_mix_ada_b, v_mix_ada_b), ffn_norm_g=(m_ffn_norm_g, v_ffn_norm_g), ffn_ada_w=(m_ffn_ada_w, v_ffn_ada_w), ffn_ada_b=(m_ffn_ada_b, v_ffn_ada_b), ffn_w_in=(m_ffn_w_in, v_ffn_w_in), ffn_w_out=(m_ffn_w_out, v_ffn_w_out), conv_w_in=(m_conv_w_in, v_conv_w_in), conv_b_in=(m_conv_b_in, v_conv_b_in), conv_w_dw=(m_conv_w_dw, v_conv_w_dw), conv_b_dw=(m_conv_b_dw, v_conv_b_dw), conv_ln_g=(m_conv_ln_g, v_conv_ln_g), conv_ln_b=(m_conv_ln_b, v_conv_ln_b), conv_w_out=(m_conv_w_out, v_conv_w_out), conv_b_out=(m_conv_b_out, v_conv_b_out), kv_norm_g=(m_kv_norm_g, v_kv_norm_g), kv_ada_w=(m_kv_ada_w, v_kv_ada_w), kv_ada_b=(m_kv_ada_b, v_kv_ada_b), kv_w=(m_kv_w, v_kv_w), forget_b=(m_forget_b, v_forget_b), attn_w_q=(m_attn_w_q, v_attn_w_q), attn_w_o=(m_attn_w_o, v_attn_w_o), final_norm_g=(m_final_norm_g, v_final_norm_g))
    names = list(weights)

    deltas, new_m, new_v = {}, {}, {}
    small_names = [n for n in names if weights[n].size < (1 << 16)]
    for n in names:
        if n in small_names:
            continue
        w = weights[n]
        w2 = w.reshape(-1, w.shape[-1])
        dl, nm, nv = _adamw(w2, grads[n].reshape(w2.shape), moms[n][0].reshape(w2.shape), moms[n][1].reshape(w2.shape), "adamw_" + n)
        deltas[n], new_m[n], new_v[n] = dl.reshape(w.shape), nm.reshape(w.shape), nv.reshape(w.shape)

    def pack_small(get):
        flat = jnp.concatenate([get(n).reshape(-1) for n in small_names])
        rows_ = -(-flat.shape[0] // (8 * LANE)) * 8
        return jnp.pad(flat, (0, rows_ * LANE - flat.shape[0])).reshape(rows_, LANE)

    ws, gs = pack_small(lambda n: weights[n]), pack_small(lambda n: grads[n])
    ms_, vs_ = pack_small(lambda n: moms[n][0]), pack_small(lambda n: moms[n][1])
    vs_ = jnp.where(jnp.arange(vs_.size).reshape(vs_.shape) < sum(weights[n].size for n in small_names), vs_, 1.0)
    dl, nm, nv = _adamw(ws, gs, ms_, vs_, "adamw_small")
    off = 0
    for n in small_names:
        sz = weights[n].size
        shp = weights[n].shape
        deltas[n] = dl.reshape(-1)[off:off + sz].reshape(shp)
        new_m[n] = nm.reshape(-1)[off:off + sz].reshape(shp)
        new_v[n] = nv.reshape(-1)[off:off + sz].reshape(shp)
        off += sz

    grad_out = [grads[n].reshape(weights[n].shape) for n in names]
    return (loss, dx0[None], *grad_out, *[deltas[n] for n in names], *[new_m[n] for n in names], *[new_v[n] for n in names])
```

```python
import functools

import jax
import jax.numpy as jnp
from jax import lax
from jax.experimental import pallas as pl
from jax.experimental.pallas import tpu as pltpu

F32 = jnp.float32
BF16 = jnp.bfloat16
MESH = pl.DeviceIdType.MESH

EPS = 1e-6
N_HEADS = 16
HEAD_DIM = 64
CONV_K = 31
LANE = 128
SUBLANES = 8
HALO = 32
ATT_FWD_TQ = 2048
ATT_TQ = 1024
ATT_TK = 512
NPIECE = 3
SPARE = (HEAD_DIM, 0)
VMEM_MB = 48
ATT_BWD_VMEM_MB = 56

ADAM_LR = 0.001
ADAM_B1 = 0.9
ADAM_B2 = 0.999
ADAM_EPS = 1e-08
ADAM_WD = 0.01
ADAM_STEP = 10


def _sds(shape, dtype):
    return jax.ShapeDtypeStruct(tuple(shape), dtype)


def _cp(sem=None, vmem_mb=VMEM_MB):
    return pltpu.CompilerParams(dimension_semantics=sem, vmem_limit_bytes=vmem_mb << 20)


def _tile(n, pref):
    return pref if n % pref == 0 else n


def _row_tile(r, mult, width=1024):
    cap = max(mult, (512 * 1024 // width) // mult * mult)
    for cand in range(cap, mult - 1, -mult):
        if r % cand == 0:
            return cand
    return r


def _resident(shape):
    nd = len(shape)
    return pl.BlockSpec(tuple(shape), lambda *_: (0,) * nd, pipeline_mode=pl.Buffered(1))


def _dot(a, b):
    return jnp.dot(a, b, preferred_element_type=F32)


def _dot_nt(a, b):
    return lax.dot_general(a, b, (((1,), (1,)), ((), ())), preferred_element_type=F32)


def _dot_tn(a, b):
    return lax.dot_general(a, b, (((0,), (0,)), ((), ())), preferred_element_type=F32)


def _sigmoid(x):
    return 1.0 / (1.0 + jnp.exp(-x))


def _colsum(x):
    return jnp.sum(x, axis=0, keepdims=True)


def _rms_parts(x):
    rstd = lax.rsqrt(jnp.mean(x * x, axis=-1, keepdims=True) + EPS)
    return x * rstd, rstd


class _Gather8:
    def __init__(self, xs):
        self.n = len(xs)
        self.m = [x.shape[0] for x in xs]
        self.land = [_sds((8 * x.shape[0],) + tuple(x.shape[1:]), x.dtype) for x in xs]
        self.sems = [pltpu.SemaphoreType.DMA((7 * self.n,)), pltpu.SemaphoreType.DMA((7 * self.n,)),
                     pltpu.SemaphoreType.DMA((self.n,))]

    def _parts(self, a, x_refs, out_refs, send_sems, recv_sems, local_sems):
        x, y, c = lax.axis_index("x"), lax.axis_index("y"), lax.axis_index("c")
        me, sibling = (x, y, c), (x, y, 1 - c)
        chips = [(1 - x, y), (x, 1 - y), (1 - x, 1 - y)]
        m_per, x_ref, out_ref = self.m[a], x_refs[a], out_refs[a]

        def rows(px, py, pc):
            return out_ref.at[pl.ds((4 * px + 2 * py + pc) * m_per, m_per)]

        def copy(k, block, to, src=None):
            return pltpu.make_async_remote_copy(
                src_ref=rows(*block) if src is None else src, dst_ref=rows(*block),
                send_sem=send_sems.at[7 * a + k], recv_sem=recv_sems.at[7 * a + k], device_id=to, device_id_type=MESH)

        mine = pltpu.make_async_copy(x_ref, rows(*me), local_sems.at[a])
        first = [copy(0, me, sibling, src=x_ref)]
        first += [copy(1 + j, me, (*chip, c), src=x_ref) for j, chip in enumerate(chips)]
        passed = [copy(4 + j, (*chip, c), sibling) for j, chip in enumerate(chips)]
        return c, me, sibling, chips, copy, mine, first, passed

    def start(self, *refs):
        for a in range(self.n):
            _, _, _, _, _, mine, first, _ = self._parts(a, *refs)
            mine.start()
            for cp in first:
                cp.start()

    def finish(self, *refs):
        parts = [self._parts(a, *refs) for a in range(self.n)]
        for j in range(3):
            for c, me, sibling, chips, copy, mine, first, passed in parts:
                copy(1 + j, (*chips[j], c), me).wait_recv()
                passed[j].start()
        for c, me, sibling, chips, copy, mine, first, passed in parts:
            copy(0, sibling, me).wait_recv()
            for j, chip in enumerate(chips):
                copy(4 + j, (*chip, 1 - c), me).wait_recv()
            for cp in first + passed:
                cp.wait_send()
            mine.wait()


class _ScatterChips:
    def __init__(self, qs):
        self.n = len(qs)
        self.land = [_sds((3,) + tuple(q.shape[1:]), q.dtype) for q in qs]
        self.sems = [pltpu.SemaphoreType.DMA((3 * self.n,)), pltpu.SemaphoreType.DMA((3 * self.n,))]

    def _copies(self, q_refs, land_refs, send_sems, recv_sems):
        x, y, c = lax.axis_index("x"), lax.axis_index("y"), lax.axis_index("c")
        chips = [(1 - x, y), (x, 1 - y), (1 - x, 1 - y)]
        return [pltpu.make_async_remote_copy(
            src_ref=q_refs[a].at[2 * cx + cy], dst_ref=land_refs[a].at[k],
            send_sem=send_sems.at[3 * a + k], recv_sem=recv_sems.at[3 * a + k],
            device_id=(cx, cy, c), device_id_type=MESH) for a in range(self.n) for k, (cx, cy) in enumerate(chips)]

    def start(self, *refs):
        for cp in self._copies(*refs):
            cp.start()

    def finish(self, *refs):
        copies = self._copies(*refs)
        for cp in copies:
            cp.wait_recv()
        for cp in copies:
            cp.wait_send()


class _SwapHalves:
    def __init__(self, ps):
        self.n = len(ps)
        self.land = [_sds((p.shape[0],) + tuple(p.shape[2:]), p.dtype) for p in ps]
        self.nb = [p.shape[0] for p in ps]
        tot = sum(self.nb)
        self.sems = [pltpu.SemaphoreType.DMA((tot,)), pltpu.SemaphoreType.DMA((tot,))]

    def _copies(self, p_refs, land_refs, send_sems, recv_sems):
        x, y, c = lax.axis_index("x"), lax.axis_index("y"), lax.axis_index("c")
        out, k = [], 0
        for a in range(self.n):
            for j in range(self.nb[a]):
                out.append(pltpu.make_async_remote_copy(
                    src_ref=p_refs[a].at[j, 1 - c], dst_ref=land_refs[a].at[j], send_sem=send_sems.at[k],
                    recv_sem=recv_sems.at[k], device_id=(x, y, 1 - c), device_id_type=MESH))
                k += 1
        return out

    start = _ScatterChips.start
    finish = _ScatterChips.finish


def _hosted_call(body, comm, *, grid, in_specs, out_specs, out_shape, scratch_shapes, name, sem, args, vmem_mb=VMEM_MB):
    def first():
        return functools.reduce(jnp.logical_and, [pl.program_id(a) == 0 for a in range(len(grid))])

    def last():
        return functools.reduce(jnp.logical_and, [pl.program_id(a) == g - 1 for a, g in enumerate(grid)])

    out_specs = tuple(out_specs) if isinstance(out_specs, (tuple, list)) else (out_specs,)
    out_shape = tuple(out_shape) if isinstance(out_shape, (tuple, list)) else (out_shape,)
    if comm is None:
        return pl.pallas_call(body, grid=grid, in_specs=list(in_specs), out_specs=out_specs, out_shape=out_shape,
                              scratch_shapes=list(scratch_shapes), name=name, compiler_params=_cp(sem, vmem_mb))(*args)
    ex, srcs = comm
    n_in, n_out, n_scr, n_ex = len(in_specs), len(out_shape), len(scratch_shapes), ex.n

    def wrapped(*refs):
        ins, src_refs = refs[:n_in], refs[n_in:n_in + n_ex]
        o0 = n_in + n_ex
        outs, land_refs = refs[o0:o0 + n_out], refs[o0 + n_out:o0 + n_out + n_ex]
        s0 = o0 + n_out + n_ex
        scr, sems = refs[s0:s0 + n_scr], refs[s0 + n_scr:]

        @pl.when(first())
        def _():
            ex.start(src_refs, land_refs, *sems)

        body(*ins, *outs, *scr)

        @pl.when(last())
        def _():
            ex.finish(src_refs, land_refs, *sems)

    hbm = pl.BlockSpec(memory_space=pl.ANY)
    res = pl.pallas_call(
        wrapped, grid=grid, in_specs=[*in_specs, *[hbm] * n_ex], out_specs=(*out_specs, *[hbm] * n_ex),
        out_shape=(*out_shape, *ex.land), scratch_shapes=[*scratch_shapes, *ex.sems], name=name,
        compiler_params=_cp(tuple("arbitrary" for _ in grid), vmem_mb))(*args, *srcs)
    return (*res[:n_out], list(res[n_out:]))


def _exchange(ex, srcs, name, in_vmem=False):
    n = ex.n

    def body(*refs):
        src_refs, land_refs, sems = refs[:n], refs[n:2 * n], refs[2 * n:]
        ex.start(src_refs, land_refs, *sems)
        ex.finish(src_refs, land_refs, *sems)

    spec = pl.BlockSpec(memory_space=pltpu.VMEM if in_vmem else pl.ANY)
    return list(pl.pallas_call(
        body, out_shape=tuple(ex.land), in_specs=[spec] * n, out_specs=tuple([spec] * n),
        scratch_shapes=ex.sems, name=name)(*srcs))


def _allgather8(x_shard, name, in_vmem):
    return _exchange(_Gather8([x_shard]), [x_shard], name, in_vmem)[0]


def _share_halves(bufs):
    n = len(bufs)

    def body(*refs):
        b_refs, out_refs, send_sems, recv_sems = refs[:n], refs[n:2 * n], refs[2 * n], refs[2 * n + 1]
        x, y, c = lax.axis_index("x"), lax.axis_index("y"), lax.axis_index("c")
        copies = [pltpu.make_async_remote_copy(
            src_ref=b_refs[k].at[c], dst_ref=out_refs[k].at[c], send_sem=send_sems.at[k], recv_sem=recv_sems.at[k],
            device_id=(x, y, 1 - c), device_id_type=MESH) for k in range(n)]
        for cp in copies:
            cp.start()
        for cp in copies:
            cp.wait_recv()
        for cp in copies:
            cp.wait_send()

    hbm = pl.BlockSpec(memory_space=pl.ANY)
    return pl.pallas_call(
        body, out_shape=tuple(_sds(b.shape, b.dtype) for b in bufs), in_specs=[hbm] * n, out_specs=tuple([hbm] * n),
        scratch_shapes=[pltpu.SemaphoreType.DMA((n,)), pltpu.SemaphoreType.DMA((n,))],
        input_output_aliases={k: k for k in range(n)}, name="rs_share_halves")(*bufs)


def _add_halves(p, land, sel, name):
    nb, _, r, w = p.shape
    tr = _row_tile(r, 16, w)

    def body(sel_ref, p_ref, l_ref, q16_ref, own_ref):
        q = p_ref[0, 0] + l_ref[0]
        q16_ref[0] = q.astype(BF16)

        @pl.when(pl.program_id(1) == sel_ref[1])
        def _():
            own_ref[...] = q

    gs = pltpu.PrefetchScalarGridSpec(
        num_scalar_prefetch=1, grid=(r // tr, nb),
        in_specs=[pl.BlockSpec((1, 1, tr, w), lambda i, j, sl: (j, sl[0], i, 0)),
                  pl.BlockSpec((1, tr, w), lambda i, j, sl: (j, i, 0))],
        out_specs=(pl.BlockSpec((1, tr, w), lambda i, j, sl: (j, i, 0)), pl.BlockSpec((tr, w), lambda i, j, sl: (i, 0))))
    return pl.pallas_call(body, grid_spec=gs, out_shape=(_sds((nb, r, w), BF16), _sds((r, w), F32)), name=name,
                          compiler_params=_cp(("parallel", "arbitrary")))(sel, p, land)


def _add_chips(own, land, sel, name):
    r, w = own.shape
    tr = _row_tile(r, 16, w)

    def body(sel_ref, q_ref, l_ref, o_ref):
        o_ref[0] = ((q_ref[...] + l_ref[0].astype(F32)) + l_ref[1].astype(F32)) + l_ref[2].astype(F32)

    gs = pltpu.PrefetchScalarGridSpec(
        num_scalar_prefetch=1, grid=(r // tr,),
        in_specs=[pl.BlockSpec((tr, w), lambda i, sl: (i, 0)), pl.BlockSpec((3, tr, w), lambda i, sl: (0, i, 0))],
        out_specs=pl.BlockSpec((1, tr, w), lambda i, sl: (sl[0], i, 0)))
    return pl.pallas_call(body, grid_spec=gs, out_shape=_sds((2, r, w), F32), name=name,
                          compiler_params=_cp(("parallel",)))(sel, own, land)


def _sum8(g):
    _, m, n = g.shape

    def body(g_ref, o_ref):
        acc = g_ref[0]
        for k in range(1, 8):
            acc = acc + g_ref[k]
        o_ref[...] = acc

    return pl.pallas_call(body, out_shape=_sds((m, n), g.dtype), name="sum8")(g)


def _ada_fwd(c_all, w3, name):
    nl, d, n = w3.shape
    tn = 256

    def body(c_ref, w_ref, o_ref):
        cc = c_ref[...]
        ca = (cc * _sigmoid(cc)).astype(BF16)
        o_ref[0] = _dot(ca, w_ref[0].astype(BF16))

    return pl.pallas_call(
        body, grid=(nl, n // tn), out_shape=_sds((nl, 8, n), F32),
        in_specs=[pl.BlockSpec((8, d), lambda l, j: (0, 0)), pl.BlockSpec((1, d, tn), lambda l, j: (l, 0, j))],
        out_specs=pl.BlockSpec((1, 8, tn), lambda l, j: (l, 0, j)),
        name=name, compiler_params=_cp(("parallel", "parallel")))(c_all, w3)


def _in_pair(x, gain, shift, scale, wg, bias, conv, name, comm=None):
    s, d = x.shape
    n = wg.shape[2]
    ts = _tile(s, 512)

    def body(*refs):
        if conv:
            x_ref, g_ref, sh_ref, sc_ref, w_ref, b_ref, h_ref, o_ref, sa_ref, sb_ref = refs
        else:
            x_ref, g_ref, sh_ref, sc_ref, w_ref, h_ref, o_ref, sa_ref, sb_ref = refs
        xhat, _ = _rms_parts(x_ref[...])
        h = ((xhat * g_ref[...]) * (1.0 + sc_ref[...]) + sh_ref[...]).astype(BF16)
        h_ref[...] = h
        for q in range(2):
            a = _dot(h, w_ref[q])
            b = _dot(h, w_ref[q + 2])
            cs = pl.ds(q * n, n)
            if conv:
                a = a + b_ref[q]
                b = b + b_ref[q + 2]
                o_ref[:, cs] = a * _sigmoid(b)
            else:
                o_ref[:, cs] = (a * _sigmoid(a) * b).astype(BF16)
            sa_ref[:, cs] = a.astype(BF16)
            sb_ref[:, cs] = b.astype(BF16)

    vec = pl.BlockSpec((1, d), lambda i: (0, 0))
    in_specs = [pl.BlockSpec((ts, d), lambda i: (i, 0)), vec, vec, vec, _resident(wg.shape)]
    args = [x, gain, shift, scale, wg]
    if conv:
        in_specs.append(_resident(bias.shape))
        args.append(bias)
    tile = pl.BlockSpec((ts, 2 * n), lambda i: (i, 0))
    return _hosted_call(
        body, comm, grid=(s // ts,),
        out_shape=(_sds((s, d), BF16), _sds((s, 2 * n), F32 if conv else BF16), _sds((s, 2 * n), BF16), _sds((s, 2 * n), BF16)),
        in_specs=in_specs, out_specs=(pl.BlockSpec((ts, d), lambda i: (i, 0)), tile, tile, tile),
        scratch_shapes=[], name=name, sem=("parallel",), args=args)


def _shift_copies(buf, shf):
    n = shf.shape[1]
    for r in range(1, SUBLANES):
        shf[r - 1, :, :] = buf[pl.ds(r, n), :]


def _shifted(buf, shf, start, n, cs):
    a, r = divmod(start, SUBLANES)
    if r == 0:
        return buf[pl.ds(start, n), cs]
    return shf[r - 1, pl.ds(a * SUBLANES, n), cs]


def _dwconv_fwd(glu, wdw, bdw, lng, lnb, comm=None):
    s, d = glu.shape
    ts = _tile(s, 256)
    rb, cb = 32, 256

    def body(cur_ref, halo_ref, w_ref, b_ref, g_ref, be_ref, dwo_ref, sw_ref, buf, shf):
        i = pl.program_id(0)

        @pl.when(i == 0)
        def _():
            buf[pl.ds(0, HALO), :] = jnp.zeros((HALO, d), F32)

        @pl.when(i > 0)
        def _():
            buf[pl.ds(0, HALO), :] = halo_ref[...]

        buf[pl.ds(HALO, ts), :] = cur_ref[...]
        _shift_copies(buf, shf)
        for r in range(ts // rb):
            for cc in range(d // cb):
                cs = pl.ds(cc * cb, cb)
                acc = jnp.zeros((rb, cb), F32) + b_ref[:, cs]
                for k in range(CONV_K):
                    acc = acc + w_ref[pl.ds(k, 1), cs] * _shifted(buf, shf, HALO - (CONV_K - 1) + k + r * rb, rb, cs)
                dwo_ref[pl.ds(r * rb, rb), cs] = acc
            rows = pl.ds(r * rb, rb)
            yv = dwo_ref[rows, :]
            mu = jnp.mean(yv, axis=-1, keepdims=True)
            yc = yv - mu
            var = jnp.mean(yc * yc, axis=-1, keepdims=True)
            ln = yc * lax.rsqrt(var + EPS) * g_ref[...] + be_ref[...]
            sw_ref[rows, :] = (ln * _sigmoid(ln)).astype(BF16)

    vec = pl.BlockSpec((1, d), lambda i: (0, 0))
    return _hosted_call(
        body, comm, grid=(s // ts,), out_shape=(_sds((s, d), F32), _sds((s, d), BF16)),
        in_specs=[pl.BlockSpec((ts, d), lambda i: (i, 0)),
                  pl.BlockSpec((HALO, d), lambda i: (jnp.maximum(i * (ts // HALO) - 1, 0), 0)),
                  pl.BlockSpec((HALO, d), lambda i: (0, 0)), vec, vec, vec],
        out_specs=(pl.BlockSpec((ts, d), lambda i: (i, 0)), pl.BlockSpec((ts, d), lambda i: (i, 0))),
        scratch_shapes=[pltpu.VMEM((HALO + ts, d), F32), pltpu.VMEM((SUBLANES - 1, HALO + ts - SUBLANES, d), F32)],
        name="dwconv_fwd", sem=("parallel",),
        args=(glu, glu, wdw, bdw, lng, lnb))


def _mm_res(a, w, b, gate, x, name):
    s, k = a.shape
    d = w.shape[1]
    ts = _tile(s, 512)

    def body(a_ref, w_ref, b_ref, g_ref, x_ref, o_ref):
        yv = _dot(a_ref[...], w_ref[...]) + b_ref[...]
        o_ref[...] = x_ref[...] + g_ref[...] * yv

    vec = pl.BlockSpec((1, d), lambda i: (0, 0))
    return pl.pallas_call(
        body, grid=(s // ts,), out_shape=_sds((s, d), F32),
        in_specs=[pl.BlockSpec((ts, k), lambda i: (i, 0)), _resident((k, d)), vec, vec, pl.BlockSpec((ts, d), lambda i: (i, 0))],
        out_specs=pl.BlockSpec((ts, d), lambda i: (i, 0)),
        name=name, compiler_params=_cp(("parallel",)))(a, w, b, gate, x)


def _qkv(x, kvp, mxp, wk, wv, wf, wq):
    s, d = x.shape
    ts = _tile(s, 512)
    qscale = HEAD_DIM ** -0.5

    def body(x_ref, gk, shk, sck, gm, shm, scm, wk_ref, wv_ref, wf_ref, wq_ref, hk_ref, h1_ref, k_ref, v_ref, q_ref, f_ref):
        xhat, _ = _rms_parts(x_ref[...])
        hk = ((xhat * gk[...]) * (1.0 + sck[...]) + shk[...]).astype(BF16)
        h1 = ((xhat * gm[...]) * (1.0 + scm[...]) + shm[...]).astype(BF16)
        hk_ref[...] = hk
        h1_ref[...] = h1
        k_ref[...] = _dot(hk, wk_ref[...]).astype(BF16)
        v_ref[...] = _dot(hk, wv_ref[...]).astype(BF16)
        f_ref[...] = _dot(hk, wf_ref[...])
        q_ref[...] = (_dot(h1, wq_ref[...]) * qscale).astype(BF16)

    vec = pl.BlockSpec((1, d), lambda i: (0, 0))
    row = pl.BlockSpec((ts, d), lambda i: (i, 0))
    return pl.pallas_call(
        body, grid=(s // ts,),
        out_shape=tuple(_sds((s, d), BF16) for _ in range(5)) + (_sds((s, LANE), F32),),
        in_specs=[row, vec, vec, vec, vec, vec, vec, _resident((d, d)), _resident((d, d)), _resident((d, LANE)), _resident((d, d))],
        out_specs=(row, row, row, row, row, pl.BlockSpec((ts, LANE), lambda i: (i, 0))),
        name="qkv_proj", compiler_params=_cp(("parallel",)))(x, *kvp, *mxp, wk, wv, wf, wq)


def _log_sigmoid(z):
    return jnp.minimum(z, 0.0) - jnp.log(1.0 + jnp.exp(-jnp.abs(z)))


def _cumsum_fwd(flog, fb):
    s = flog.shape[0]
    ts = _tile(s, 256)

    def body(f_ref, b_ref, cum_ref, cumt_ref, carry):
        @pl.when(pl.program_id(0) == 0)
        def _():
            carry[...] = jnp.zeros_like(carry)

        ls = _log_sigmoid(f_ref[...] + b_ref[...])
        r = lax.broadcasted_iota(jnp.int32, (ts, ts), 0)
        cidx = lax.broadcasted_iota(jnp.int32, (ts, ts), 1)
        tri = (cidx <= r).astype(F32)
        cs = jnp.dot(tri, ls, preferred_element_type=F32, precision=lax.Precision.HIGHEST) + carry[...]
        cum_ref[...] = cs
        cumt_ref[...] = cs.T
        carry[...] = cs[ts - 1:ts, :]

    return pl.pallas_call(
        body, grid=(s // ts,), out_shape=(_sds((s, LANE), F32), _sds((LANE, s), F32)),
        in_specs=[pl.BlockSpec((ts, LANE), lambda i: (i, 0)), pl.BlockSpec((1, LANE), lambda i: (0, 0))],
        out_specs=(pl.BlockSpec((ts, LANE), lambda i: (i, 0)), pl.BlockSpec((LANE, ts), lambda i: (0, i))),
        scratch_shapes=[pltpu.VMEM((1, LANE), F32)],
        name="forget_cumsum", compiler_params=_cp(("arbitrary",)))(flog, fb)


def _pick_row(m, idx):
    r = lax.broadcasted_iota(jnp.int32, (m.shape[0], 1), 0)
    return jnp.sum(jnp.where(r == idx, m, 0.0), axis=0, keepdims=True)


def _pick_col(m, idx):
    cidx = lax.broadcasted_iota(jnp.int32, (1, m.shape[1]), 1)
    return jnp.sum(jnp.where(cidx == idx, m, 0.0), axis=1, keepdims=True)


def _split3(x):
    hi = x.astype(BF16)
    r1 = x - hi.astype(F32)
    mid = r1.astype(BF16)
    lo = (r1 - mid.astype(F32)).astype(BF16)
    return hi, mid, lo


def _head_mask(lane, hh):
    lo = lane < HEAD_DIM
    return lo if hh == 0 else jnp.logical_not(lo)


def _attn_prep(k, v, cum):
    s, d = k.shape
    npair = d // LANE
    tc = _tile(s, 512)

    def body(k_ref, v_ref, c_ref, ka_ref, kt_ref, vt_ref):
        p = pl.program_id(0)
        lane = lax.broadcasted_iota(jnp.int32, (1, LANE), 1)
        kk = k_ref[...]
        vv = v_ref[...].astype(F32)
        ckt = c_ref[...]
        for hh in range(2):
            head = _head_mask(lane, hh)
            b = SPARE[hh]
            ck = _pick_col(ckt, 2 * p + hh)
            extra = jnp.where(lane == b + NPIECE, 1.0, 0.0).astype(BF16) + jnp.zeros((tc, LANE), BF16)
            for n_, pc in enumerate(_split3(ck)):
                extra = jnp.where(lane == b + n_, pc, extra)
            ka = jnp.where(head, kk, extra)
            ka_ref[0, hh] = ka
            kt_ref[0, hh] = ka.astype(F32).T.astype(BF16)
            vx = jnp.where(head, vv, jnp.where(lane == b, 1.0, 0.0))
            vt_ref[0, hh] = vx.T.astype(BF16)

    blk = pl.BlockSpec((tc, LANE), lambda p, c: (c, p))
    return pl.pallas_call(
        body, grid=(npair, s // tc),
        out_shape=(_sds((npair, 2, s, LANE), BF16), _sds((npair, 2, LANE, s), BF16), _sds((npair, 2, LANE, s), BF16)),
        in_specs=[blk, blk, pl.BlockSpec((tc, LANE), lambda p, c: (c, 0))],
        out_specs=(pl.BlockSpec((1, 2, tc, LANE), lambda p, c: (p, 0, c, 0)),
                   pl.BlockSpec((1, 2, LANE, tc), lambda p, c: (p, 0, 0, c)),
                   pl.BlockSpec((1, 2, LANE, tc), lambda p, c: (p, 0, 0, c))),
        name="fox_attn_prep", compiler_params=_cp(("parallel", "parallel")))(k, v, cum)


def _q_aug(qq, lane, hh):
    b = SPARE[hh]
    sel = jnp.logical_and(lane >= b, lane < b + NPIECE)
    neg = jnp.full((1, LANE), -1.0, BF16)
    zl = jnp.zeros((1, LANE), BF16)
    return jnp.where(_head_mask(lane, hh), qq, jnp.where(sel, neg, zl))


def _attn_fwd(q, kaug, vtr, cumt):
    s, d = q.shape
    tq = _tile(s, ATT_FWD_TQ)
    tk = _tile(s, ATT_TK)
    npair = d // LANE
    npart = max(1, tq // tk)

    def body(q_ref, ka_ref, vt_ref, cumt_ref, o_ref, lse_ref):
        p = pl.program_id(0)
        i = pl.program_id(1)
        lane = lax.broadcasted_iota(jnp.int32, (1, LANE), 1)
        qq = q_ref[...]
        qx = (_q_aug(qq, lane, 0), _q_aug(qq, lane, 1))
        cqt = cumt_ref[:, pl.ds(pl.multiple_of(i * tq, tq), tq)]
        cq = (_pick_row(cqt, 2 * p), _pick_row(cqt, 2 * p + 1))
        jd = (i * tq) // tk

        def kv_step(j, carry, diag, q_lo=0):
            ks = pl.multiple_of(j * tk, tk)
            nq_ = tq - q_lo
            if diag:
                krow = lax.broadcasted_iota(jnp.int32, (tk, nq_), 0) + j * tk
                qcol = lax.broadcasted_iota(jnp.int32, (tk, nq_), 1) + (i * tq + q_lo)
                causal = krow <= qcol
            out = []
            for hh in range(2):
                m_all, acc_all = carry[2 * hh], carry[2 * hh + 1]
                m, acc, cqh = m_all[:, q_lo:], acc_all[:, q_lo:], cq[hh][:, q_lo:]
                sc = _dot_nt(ka_ref[0, hh, pl.ds(ks, tk), :], qx[hh][q_lo:, :])
                if diag:
                    sc = jnp.where(causal, sc, -jnp.inf)
                mx = jnp.max(sc, axis=0, keepdims=True) + cqh
                mn = jnp.maximum(m, mx)
                alpha = jnp.exp(m - mn)
                pt = jnp.exp(sc + (cqh - mn)).astype(BF16)
                acc = alpha * acc + _dot(vt_ref[0, hh, :, pl.ds(ks, tk)], pt)
                if q_lo:
                    mn = jnp.concatenate([m_all[:, :q_lo], mn], axis=1)
                    acc = jnp.concatenate([acc_all[:, :q_lo], acc], axis=1)
                out += [mn, acc]
            return tuple(out)

        minit = jnp.full((1, tq), -jnp.inf, F32)
        ainit = jnp.zeros((LANE, tq), F32)
        carry = (minit, ainit, minit, ainit)
        for pj in range(npart):
            carry = kv_step(jd + pj, carry, True, q_lo=pj * tk)
        carry = lax.fori_loop(0, jd, lambda j, cr: kv_step(j, cr, False), carry)
        m0, a0, m1, a1 = carry
        l0 = a0[SPARE[0]:SPARE[0] + 1, :]
        l1 = a1[SPARE[1]:SPARE[1] + 1, :]
        row = lax.broadcasted_iota(jnp.int32, (LANE, 1), 0)
        ot = jnp.where(row < HEAD_DIM, a0 / l0, a1 / l1)
        o_ref[...] = ot.T.astype(BF16)
        r8 = lax.broadcasted_iota(jnp.int32, (8, 1), 0)
        lse_ref[0] = jnp.where(r8 == 0, m0 + jnp.log(l0), jnp.where(r8 == 1, m1 + jnp.log(l1), 0.0))

    return pl.pallas_call(
        body, grid=(npair, s // tq), out_shape=(_sds((s, d), BF16), _sds((npair, 8, s), F32)),
        in_specs=[pl.BlockSpec((tq, LANE), lambda p, i: (i, p)),
                  pl.BlockSpec((1, 2, s, LANE), lambda p, i: (p, 0, 0, 0)),
                  pl.BlockSpec((1, 2, LANE, s), lambda p, i: (p, 0, 0, 0)),
                  pl.BlockSpec((N_HEADS, s), lambda p, i: (0, 0))],
        out_specs=(pl.BlockSpec((tq, LANE), lambda p, i: (i, p)), pl.BlockSpec((1, 8, tq), lambda p, i: (p, 0, i))),
        name="fox_attn_fwd", compiler_params=_cp(("parallel", "parallel")))(q, kaug, vtr, cumt)


def _final(x, gain, target):
    s, d = x.shape
    ts = _tile(s, 512)

    def body(x_ref, g_ref, t_ref, lsum_ref, dx_ref, dg_ref):
        @pl.when(pl.program_id(0) == 0)
        def _():
            lsum_ref[...] = jnp.zeros_like(lsum_ref)
            dg_ref[...] = jnp.zeros_like(dg_ref)

        xhat, rstd = _rms_parts(x_ref[...])
        e = xhat * g_ref[...] - t_ref[...]
        lsum_ref[...] += _colsum(e * e)
        dout = e * (1.0 / d)
        dg_ref[...] += _colsum(dout * xhat)
        dxhat = dout * g_ref[...]
        dx_ref[...] = rstd * (dxhat - xhat * jnp.mean(dxhat * xhat, axis=-1, keepdims=True))

    vec = pl.BlockSpec((1, d), lambda i: (0, 0))
    row = pl.BlockSpec((ts, d), lambda i: (i, 0))
    return pl.pallas_call(
        body, grid=(s // ts,), out_shape=(_sds((1, d), F32), _sds((s, d), F32), _sds((1, d), F32)),
        in_specs=[row, vec, row], out_specs=(vec, row, vec),
        name="final_norm_loss", compiler_params=_cp(("arbitrary",)))(x, gain, target)


def _ffn_bwd_act(dx, gate, w_out, ug, uu, name, comm=None):
    s, d = dx.shape
    f = w_out.shape[0]
    n = f // 2
    ts = _tile(s, 512)

    def body(dx_ref, g_ref, w_ref, ug_ref, uu_ref, dug_ref, duu_ref):
        dy = (dx_ref[...] * g_ref[...]).astype(BF16)
        for q in range(2):
            cs = pl.ds(q * n, n)
            dact = _dot_nt(dy, w_ref[cs, :])
            g = ug_ref[:, cs].astype(F32)
            u = uu_ref[:, cs].astype(F32)
            sg = _sigmoid(g)
            dug_ref[:, cs] = (dact * u * sg * (1.0 + g * (1.0 - sg))).astype(BF16)
            duu_ref[:, cs] = (dact * g * sg).astype(BF16)

    tile = pl.BlockSpec((ts, f), lambda i: (i, 0))
    return _hosted_call(
        body, comm, grid=(s // ts,), out_shape=(_sds((s, f), BF16), _sds((s, f), BF16)),
        in_specs=[pl.BlockSpec((ts, d), lambda i: (i, 0)), pl.BlockSpec((1, d), lambda i: (0, 0)),
                  _resident(w_out.shape), tile, tile],
        out_specs=(tile, tile), scratch_shapes=[], name=name, sem=("parallel",), args=(dx, gate, w_out, ug, uu))


def _dw_mm(a, b_list, tk, tn, name, gate=None, wfull=None, dgate_init=None):
    s, kdim = a.shape
    nb1 = b_list[0].shape[1] // tn
    nb = nb1 * len(b_list)
    ts = _tile(s, 1024)
    nk = kdim // tk
    ns = s // ts
    gated = gate is not None

    def body(*refs):
        a_ref = refs[0]
        b_refs = refs[1:1 + len(b_list)]
        rest = refs[1 + len(b_list):]
        if gated:
            g_ref, w_ref, di_ref, o_ref, dg_ref, acc = rest
        else:
            o_ref, acc = rest
        jn, ik, st = pl.program_id(0), pl.program_id(1), pl.program_id(2)

        @pl.when(st == 0)
        def _():
            acc[...] = jnp.zeros_like(acc)

        for mi, b_ref in enumerate(b_refs):
            @pl.when(jn // nb1 == mi)
            def _(b_ref=b_ref):
                acc[...] += _dot_tn(a_ref[...], b_ref[...].astype(BF16))

        if gated:
            @pl.when(jnp.logical_and(ik == 0, st == 0))
            def _():
                dg_ref[...] = di_ref[...]

        @pl.when(st == ns - 1)
        def _():
            if gated:
                o_ref[0] = acc[...] * g_ref[...]
                dg_ref[...] += _colsum(acc[...] * w_ref[...].astype(F32))
            else:
                o_ref[0] = acc[...]

    in_specs = [pl.BlockSpec((ts, tk), lambda jn, ik, st: (st, ik))]
    for mi in range(len(b_list)):
        in_specs.append(pl.BlockSpec(
            (ts, tn), lambda jn, ik, st, mi=mi: (st, jnp.clip(jn - mi * nb1, 0, nb1 - 1))))
    args = [a] + list(b_list)
    out_shape = [_sds((nb, kdim, tn), F32)]
    out_specs = [pl.BlockSpec((1, tk, tn), lambda jn, ik, st: (jn, ik, 0))]
    if gated:
        vec = pl.BlockSpec((1, tn), lambda jn, ik, st: (0, jn))
        in_specs += [vec, pl.BlockSpec((tk, tn), lambda jn, ik, st: (ik, jn)), vec]
        args += [gate, wfull, dgate_init]
        out_shape.append(_sds((1, nb * tn), F32))
        out_specs.append(vec)
    res = pl.pallas_call(
        body, grid=(nb, nk, ns), out_shape=tuple(out_shape), in_specs=in_specs, out_specs=tuple(out_specs),
        scratch_shapes=[pltpu.VMEM((tk, tn), F32)],
        name=name, compiler_params=_cp(("parallel", "arbitrary", "arbitrary")))(*args)
    return res if gated else res[0]


def _mm_normbwd(terms, x, dxres, gain, scale, name, ts_pref=256, comm=None):
    s, d = x.shape
    ts = _tile(s, ts_pref)
    arrs, warrs = [], []
    for a, _, w, _ in terms:
        if not any(a is z for z in arrs):
            arrs.append(a)
        if not any(w is z for z in warrs):
            warrs.append(w)
    ai = [next(i for i, z in enumerate(arrs) if z is a) for a, _, _, _ in terms]
    wi = [next(i for i, z in enumerate(warrs) if z is w) for _, _, w, _ in terms]

    def body(*refs):
        a_refs = refs[:len(arrs)]
        w_refs = refs[len(arrs):len(arrs) + len(warrs)]
        x_ref, dr_ref, g_ref, sc_ref, dx_ref, dsh_ref, dsc_ref, dg_ref = refs[len(arrs) + len(warrs):]

        @pl.when(pl.program_id(0) == 0)
        def _():
            dsh_ref[...] = jnp.zeros_like(dsh_ref)
            dsc_ref[...] = jnp.zeros_like(dsc_ref)
            dg_ref[...] = jnp.zeros_like(dg_ref)

        dh = None
        for ti, (_, c0, w, q) in enumerate(terms):
            n = w.shape[2]
            part = _dot_nt(a_refs[ai[ti]][:, pl.ds(c0, n)], w_refs[wi[ti]][q])
            dh = part if dh is None else dh + part
        xhat, rstd = _rms_parts(x_ref[...])
        nrm = xhat * g_ref[...]
        dsh_ref[...] += _colsum(dh)
        dsc_ref[...] += _colsum(dh * nrm)
        dn = dh * (1.0 + sc_ref[...])
        dg_ref[...] += _colsum(dn * xhat)
        dxhat = dn * g_ref[...]
        dx_ref[...] = dr_ref[...] + rstd * (dxhat - xhat * jnp.mean(dxhat * xhat, axis=-1, keepdims=True))

    vec = pl.BlockSpec((1, d), lambda i: (0, 0))
    row = pl.BlockSpec((ts, d), lambda i: (i, 0))
    in_specs = [pl.BlockSpec((ts, a.shape[1]), lambda i: (i, 0)) for a in arrs]
    in_specs += [_resident(w.shape) for w in warrs]
    in_specs += [row, row, vec, vec]
    return _hosted_call(
        body, comm, grid=(s // ts,), out_shape=(_sds((s, d), F32), _sds((1, d), F32), _sds((1, d), F32), _sds((1, d), F32)),
        in_specs=in_specs, out_specs=(row, vec, vec, vec), scratch_shapes=[],
        name=name, sem=("arbitrary",), args=(*arrs, *warrs, x, dxres, gain, scale))


def _do_kernel(dx, gate, wo, o):
    s, d = dx.shape
    ts = _tile(s, 512)

    def body(dx_ref, g_ref, w_ref, o_ref, do_ref, dl_ref):
        dy = (dx_ref[...] * g_ref[...]).astype(BF16)
        do = _dot_nt(dy, w_ref[...])
        do_ref[...] = do.astype(BF16)
        prod = do * o_ref[...].astype(F32)
        hrow = lax.broadcasted_iota(jnp.int32, (N_HEADS, d), 0)
        hcol = lax.broadcasted_iota(jnp.int32, (N_HEADS, d), 1) // HEAD_DIM
        sel = (hrow == hcol).astype(F32)
        dl_ref[...] = lax.dot_general(sel, prod, (((1,), (1,)), ((), ())), preferred_element_type=F32,
                                      precision=lax.Precision.HIGHEST)

    row = pl.BlockSpec((ts, d), lambda i: (i, 0))
    return pl.pallas_call(
        body, grid=(s // ts,), out_shape=(_sds((s, d), BF16), _sds((N_HEADS, s), F32)),
        in_specs=[row, pl.BlockSpec((1, d), lambda i: (0, 0)), _resident(wo.shape), row],
        out_specs=(row, pl.BlockSpec((N_HEADS, ts), lambda i: (0, i))),
        name="attn_do", compiler_params=_cp(("parallel",)))(dx, gate, wo, o)


def _attn_bwd(q, do, kaug, kaugt, v, cumt, lse, deltat, comm=None):
    s, d = q.shape
    tq = _tile(s, ATT_TQ)
    tk = _tile(s, ATT_TK)
    assert tq in (tk, 2 * tk)
    npair = d // LANE
    nq = s // tq
    nkb = s // tk
    qscale = HEAD_DIM ** -0.5

    def body(q_ref, do_ref, ka_ref, kt_ref, v_ref, cumt_ref, lse_ref, dl_ref,
             dq_ref, dk_ref, dv_ref, dcq_ref, dck_ref, qaug, dom, rowv, dqt):
        p = pl.program_id(0)
        j = pl.program_id(1)
        lane = lax.broadcasted_iota(jnp.int32, (1, LANE), 1)
        lo = lane < HEAD_DIM
        r8 = lax.broadcasted_iota(jnp.int32, (8, 1), 0)

        @pl.when(j == 0)
        def _():
            dqt[...] = jnp.zeros_like(dqt)
            for c in range(nq):
                rows = pl.ds(c * tq, tq)
                qq = q_ref[rows, :]
                dd = do_ref[rows, :]
                cqt = cumt_ref[:, rows]
                dlt = dl_ref[:, rows]
                lst = lse_ref[0, :, rows]
                for hh in range(2):
                    qaug[hh, rows, :] = _q_aug(qq, lane, hh)
                    dom[hh, rows, :] = jnp.where(_head_mask(lane, hh), dd, jnp.zeros_like(dd))
                    rowv[hh, :, rows] = jnp.where(
                        r8 == 0, _pick_row(cqt, 2 * p + hh) - lst[hh:hh + 1, :],
                        jnp.where(r8 == 1, _pick_row(dlt, 2 * p + hh), 0.0))

        vv = v_ref[...]
        i0 = (j * tk) // tq

        def q_step(qs, nq_, carry, diag):
            dv_acc, dk0, dk1 = carry
            qs = pl.multiple_of(qs, tk)
            if diag:
                krow = lax.broadcasted_iota(jnp.int32, (tk, nq_), 0) + j * tk
                qcol = lax.broadcasted_iota(jnp.int32, (tk, nq_), 1) + qs
                causal = krow <= qcol
            dks = [dk0, dk1]
            for hh in range(2):
                rv = rowv[hh, :, pl.ds(qs, nq_)]
                qa = qaug[hh, pl.ds(qs, nq_), :]
                dh = dom[hh, pl.ds(qs, nq_), :]
                sc = _dot_nt(ka_ref[0, hh], qa)
                if diag:
                    sc = jnp.where(causal, sc, -jnp.inf)
                pt = jnp.exp(sc + rv[0:1, :])
                dpt = _dot_nt(vv, dh)
                dst = (pt * (dpt - rv[1:2, :])).astype(BF16)
                dv_acc = dv_acc + _dot(pt.astype(BF16), dh)
                dks[hh] = dks[hh] + _dot(dst, qa)
                dqt[hh, :, pl.ds(qs, nq_)] += _dot(kt_ref[0, hh], dst)
            return dv_acc, dks[0], dks[1]

        z = jnp.zeros((tk, LANE), F32)
        first = ((j * tk) % tq == 0).astype(jnp.int32)
        carry = lax.fori_loop(0, first, lambda _, cr: q_step(i0 * tq, tq, cr, True), (z, z, z))
        if tq > tk:
            carry = lax.fori_loop(0, 1 - first, lambda _, cr: q_step(j * tk, tq - tk, cr, True), carry)
        dv_acc, dk0, dk1 = lax.fori_loop(i0 + 1, nq, lambda i, cr: q_step(i * tq, tq, cr, False), carry)
        dv_ref[...] = dv_acc.astype(BF16)
        dk_ref[...] = jnp.where(lo, dk0, dk1).astype(BF16)
        dck_ref[0] = jnp.where(r8 == 0, dk0.T[SPARE[0]:SPARE[0] + 1, :],
                               jnp.where(r8 == 1, dk1.T[SPARE[1]:SPARE[1] + 1, :], 0.0))

        @pl.when(j == nkb - 1)
        def _():
            for c in range(nq):
                rows = pl.ds(c * tq, tq)
                a0 = dqt[0, :, rows].T
                a1 = dqt[1, :, rows].T
                dq_ref[rows, :] = (jnp.where(lo, a0, a1) * qscale).astype(BF16)
            r0, r1 = SPARE[0] + NPIECE, SPARE[1] + NPIECE
            dcq_ref[0] = jnp.where(r8 == 0, dqt[0, r0:r0 + 1, :], jnp.where(r8 == 1, dqt[1, r1:r1 + 1, :], 0.0))

    col = pl.BlockSpec((s, LANE), lambda p, j: (0, p), pipeline_mode=pl.Buffered(1))
    rows16 = pl.BlockSpec((N_HEADS, s), lambda p, j: (0, 0), pipeline_mode=pl.Buffered(1))
    blk = pl.BlockSpec((tk, LANE), lambda p, j: (j, p))
    return _hosted_call(
        body, comm, grid=(npair, nkb),
        out_shape=(_sds((s, d), BF16), _sds((s, d), BF16), _sds((s, d), BF16), _sds((npair, 8, s), F32), _sds((npair, 8, s), F32)),
        in_specs=[col, col, pl.BlockSpec((1, 2, tk, LANE), lambda p, j: (p, 0, j, 0)),
                  pl.BlockSpec((1, 2, LANE, tk), lambda p, j: (p, 0, 0, j)), blk, rows16,
                  pl.BlockSpec((1, 8, s), lambda p, j: (p, 0, 0), pipeline_mode=pl.Buffered(1)), rows16],
        out_specs=(pl.BlockSpec((s, LANE), lambda p, j: (0, p)), blk, blk,
                   pl.BlockSpec((1, 8, s), lambda p, j: (p, 0, 0)), pl.BlockSpec((1, 8, tk), lambda p, j: (p, 0, j))),
        scratch_shapes=[pltpu.VMEM((2, s, LANE), BF16), pltpu.VMEM((2, s, LANE), BF16), pltpu.VMEM((2, 8, s), F32),
                        pltpu.VMEM((2, LANE, s), F32)],
        name="fox_attn_bwd", sem=("arbitrary", "arbitrary"), vmem_mb=ATT_BWD_VMEM_MB,
        args=(q, do, kaug, kaugt, v, cumt, lse, deltat))


def _cumsum_bwd(dcq, dck, flog, fb):
    s = flog.shape[0]
    ts = _tile(s, 256)
    nt = s // ts

    def body(dq_ref, dk_ref, f_ref, b_ref, df_ref, db_ref, carry):
        @pl.when(pl.program_id(0) == 0)
        def _():
            carry[...] = jnp.zeros_like(carry)
            db_ref[...] = jnp.zeros_like(db_ref)

        r = lax.broadcasted_iota(jnp.int32, (ts, ts), 0)
        cidx = lax.broadcasted_iota(jnp.int32, (ts, ts), 1)
        tri = (r >= cidx).astype(F32)
        dct = dq_ref[...] + dk_ref[...]
        dlst = jnp.dot(dct, tri, preferred_element_type=F32, precision=lax.Precision.HIGHEST) + carry[...]
        carry[...] = dlst[:, 0:1]
        dls = jnp.concatenate([dlst, jnp.zeros((LANE - N_HEADS, ts), F32)], axis=0).T
        z = f_ref[...] + b_ref[...]
        df = dls * (1.0 / (1.0 + jnp.exp(z)))
        db_ref[...] += _colsum(df)
        df_ref[...] = df.astype(BF16)

    rev = pl.BlockSpec((ts, LANE), lambda i: (nt - 1 - i, 0))
    revt = pl.BlockSpec((N_HEADS, ts), lambda i: (0, nt - 1 - i))
    vec = pl.BlockSpec((1, LANE), lambda i: (0, 0))
    return pl.pallas_call(
        body, grid=(nt,), out_shape=(_sds((s, LANE), BF16), _sds((1, LANE), F32)),
        in_specs=[revt, revt, rev, vec], out_specs=(rev, vec), scratch_shapes=[pltpu.VMEM((N_HEADS, 1), F32)],
        name="forget_cumsum_bwd", compiler_params=_cp(("arbitrary",)))(dcq, dck, flog, fb)


def _conv_bwd1(dx, gate, w_out, b_out, dwo, lng, lnb):
    s, d = dx.shape
    ts = _tile(s, 512)
    ns = s // ts

    def body(dx_ref, g_ref, w_ref, bo_ref, y_ref, lg_ref, lb_ref, dd_ref, dlg_ref, dlb_ref, dbd_ref, dbo_ref, dge_ref, cs):
        i = pl.program_id(0)

        @pl.when(i == 0)
        def _():
            for r in (dlg_ref, dlb_ref, dbd_ref, cs):
                r[...] = jnp.zeros_like(r)

        dxv = dx_ref[...]
        cs[...] += _colsum(dxv)
        dsw = _dot_nt((dxv * g_ref[...]).astype(BF16), w_ref[...])
        yv = y_ref[...]
        mu = jnp.mean(yv, axis=-1, keepdims=True)
        yc = yv - mu
        rstd = lax.rsqrt(jnp.mean(yc * yc, axis=-1, keepdims=True) + EPS)
        xhat = yc * rstd
        ln = xhat * lg_ref[...] + lb_ref[...]
        sg = _sigmoid(ln)
        dln = dsw * (sg * (1.0 + ln * (1.0 - sg)))
        dlg_ref[...] += _colsum(dln * xhat)
        dlb_ref[...] += _colsum(dln)
        dxh = dln * lg_ref[...]
        dd = rstd * (dxh - jnp.mean(dxh, axis=-1, keepdims=True) - xhat * jnp.mean(dxh * xhat, axis=-1, keepdims=True))
        dbd_ref[...] += _colsum(dd)
        dd_ref[...] = dd

        @pl.when(i == ns - 1)
        def _():
            dbo_ref[...] = g_ref[...] * cs[...]
            dge_ref[...] = bo_ref[...] * cs[...]

    vec = pl.BlockSpec((1, d), lambda i: (0, 0))
    row = pl.BlockSpec((ts, d), lambda i: (i, 0))
    return pl.pallas_call(
        body, grid=(ns,), out_shape=(_sds((s, d), F32),) + tuple(_sds((1, d), F32) for _ in range(5)),
        in_specs=[row, vec, _resident(w_out.shape), vec, row, vec, vec], out_specs=(row, vec, vec, vec, vec, vec),
        scratch_shapes=[pltpu.VMEM((1, d), F32)],
        name="conv_bwd_ln", compiler_params=_cp(("arbitrary",)))(dx, gate, w_out, b_out, dwo, lng, lnb)


def _dwconv_bwd(ddwo, glu, a_s, g_s, wdw, comm=None):
    s, d = ddwo.shape
    ts = _tile(s, 256)
    ns = s // ts
    rb, cb = 32, 256
    nrb = ts // rb

    def body(dd_ref, ddn_ref, gl_ref, glh_ref, a_ref, g_ref, w_ref, da_ref, dg_ref, dw_ref, sa_ref, sg_ref, bufd, bufg, dws,
             shd, shg):
        i = pl.program_id(0)

        @pl.when(i == 0)
        def _():
            dws[...] = jnp.zeros_like(dws)
            sa_ref[...] = jnp.zeros_like(sa_ref)
            sg_ref[...] = jnp.zeros_like(sg_ref)
            bufg[pl.ds(0, HALO), :] = jnp.zeros((HALO, d), F32)

        @pl.when(i > 0)
        def _():
            bufg[pl.ds(0, HALO), :] = glh_ref[...]

        bufg[pl.ds(HALO, ts), :] = gl_ref[...]
        bufd[pl.ds(0, ts), :] = dd_ref[...]

        @pl.when(i == ns - 1)
        def _():
            bufd[pl.ds(ts, HALO), :] = jnp.zeros((HALO, d), F32)

        @pl.when(i < ns - 1)
        def _():
            bufd[pl.ds(ts, HALO), :] = ddn_ref[...]

        _shift_copies(bufd, shd)
        _shift_copies(bufg, shg)
        for cc in range(d // cb):
            cs = pl.ds(cc * cb, cb)
            for r in range(nrb):
                acc = jnp.zeros((rb, cb), F32)
                for k in range(CONV_K):
                    acc = acc + w_ref[pl.ds(k, 1), cs] * _shifted(bufd, shd, r * rb + (CONV_K - 1) - k, rb, cs)
                rows = pl.ds(r * rb, rb)
                av = a_ref[rows, cs].astype(F32)
                sg = _sigmoid(g_ref[rows, cs].astype(F32))
                dav = acc * sg
                dgv = acc * av * sg * (1.0 - sg)
                da_ref[rows, cs] = dav.astype(BF16)
                dg_ref[rows, cs] = dgv.astype(BF16)
                sa_ref[:, cs] += _colsum(dav)
                sg_ref[:, cs] += _colsum(dgv)
            for k in range(CONV_K):
                acc8 = jnp.zeros((8, cb), F32)
                for r in range(nrb):
                    prod = bufd[pl.ds(r * rb, rb), cs] * _shifted(bufg, shg, HALO - (CONV_K - 1) + k + r * rb, rb, cs)
                    acc8 = acc8 + (prod[0:8] + prod[8:16]) + (prod[16:24] + prod[24:32])
                dws[pl.ds(8 * k, 8), cs] += acc8

        @pl.when(i == ns - 1)
        def _():
            dw_ref[...] = jnp.zeros_like(dw_ref)
            for k in range(CONV_K):
                dw_ref[pl.ds(k, 1), :] = _colsum(dws[pl.ds(8 * k, 8), :])

    row = pl.BlockSpec((ts, d), lambda i: (i, 0))
    vec = pl.BlockSpec((1, d), lambda i: (0, 0))
    hb = ts // HALO
    return _hosted_call(
        body, comm, grid=(ns,),
        out_shape=(_sds((s, d), BF16), _sds((s, d), BF16), _sds((HALO, d), F32), _sds((1, d), F32), _sds((1, d), F32)),
        in_specs=[row, pl.BlockSpec((HALO, d), lambda i: (jnp.minimum((i + 1) * hb, ns * hb - 1), 0)),
                  row, pl.BlockSpec((HALO, d), lambda i: (jnp.maximum(i * hb - 1, 0), 0)),
                  row, row, pl.BlockSpec((HALO, d), lambda i: (0, 0))],
        out_specs=(row, row, pl.BlockSpec((HALO, d), lambda i: (0, 0)), vec, vec),
        scratch_shapes=[pltpu.VMEM((ts + HALO, d), F32), pltpu.VMEM((HALO + ts, d), F32), pltpu.VMEM((8 * HALO, d), F32),
                        pltpu.VMEM((SUBLANES - 1, HALO + ts - SUBLANES, d), F32),
                        pltpu.VMEM((SUBLANES - 1, HALO + ts - SUBLANES, d), F32)],
        name="dwconv_bwd", sem=("arbitrary",), args=(ddwo, ddwo, glu, glu, a_s, g_s, wdw))


def _ada_wgrad(cat, da, name):
    nl, _, n = da.shape
    d = cat.shape[0]
    tn = 256

    def body(c_ref, d_ref, o_ref):
        acc = c_ref[:, 0:1] * d_ref[0, 0:1, :]
        for r in range(1, 8):
            acc = acc + c_ref[:, r:r + 1] * d_ref[0, r:r + 1, :]
        o_ref[0] = acc

    return pl.pallas_call(
        body, grid=(nl, n // tn), out_shape=_sds((nl, d, n), F32),
        in_specs=[pl.BlockSpec((d, 8), lambda l, j: (0, 0)), pl.BlockSpec((1, 8, tn), lambda l, j: (l, 0, j))],
        out_specs=pl.BlockSpec((1, d, tn), lambda l, j: (l, 0, j)),
        name=name, compiler_params=_cp(("parallel", "parallel")))(cat, da)


def _silu_rows(c_all):
    def body(c_ref, o_ref):
        cc = c_ref[...]
        o_ref[...] = cc * _sigmoid(cc)

    return pl.pallas_call(body, out_shape=_sds(c_all.shape, F32), name="silu_c")(c_all)


def _adamw(w, g, m, v, name):
    r, c = w.shape
    tr = r
    for cand in (512, 256, 128, 64, 32, 16, 8):
        if r % cand == 0 and cand * c * 4 <= (1 << 20):
            tr = cand
            break
    bc1 = 1.0 - ADAM_B1 ** ADAM_STEP
    bc2 = 1.0 - ADAM_B2 ** ADAM_STEP

    def body(w_ref, g_ref, m_ref, v_ref, d_ref, nm_ref, nv_ref):
        gv = g_ref[...]
        mn = ADAM_B1 * m_ref[...] + (1.0 - ADAM_B1) * gv
        vn = ADAM_B2 * v_ref[...] + (1.0 - ADAM_B2) * (gv * gv)
        mh = mn / bc1
        vh = vn / bc2
        d_ref[...] = -ADAM_LR * (mh / (jnp.sqrt(vh) + ADAM_EPS) + ADAM_WD * w_ref[...])
        nm_ref[...] = mn
        nv_ref[...] = vn

    blk = pl.BlockSpec((tr, c), lambda i: (i, 0))
    return pl.pallas_call(
        body, grid=(r // tr,), out_shape=tuple(_sds((r, c), F32) for _ in range(3)),
        in_specs=[blk, blk, blk, blk], out_specs=(blk, blk, blk),
        name=name, compiler_params=_cp(("parallel",)))(w, g, m, v)


def _pad_rows(a, rows, axis):
    pad = [(0, 0)] * a.ndim
    pad[axis] = (0, rows - a.shape[axis])
    return jnp.pad(a, pad)


def _vec(a):
    return a.reshape(1, -1)


def kernel(x, c, mix_norm_g, mix_ada_w, mix_ada_b, ffn_norm_g, ffn_ada_w, ffn_ada_b, ffn_w_in, ffn_w_out, conv_w_in, conv_b_in, conv_w_dw, conv_b_dw, conv_ln_g, conv_ln_b, conv_w_out, conv_b_out, kv_norm_g, kv_ada_w, kv_ada_b, kv_w, forget_b, attn_w_q, attn_w_o, final_norm_g, loss_target, m_mix_norm_g, m_mix_ada_w, m_mix_ada_b, m_ffn_norm_g, m_ffn_ada_w, m_ffn_ada_b, m_ffn_w_in, m_ffn_w_out, m_conv_w_in, m_conv_b_in, m_conv_w_dw, m_conv_b_dw, m_conv_ln_g, m_conv_ln_b, m_conv_w_out, m_conv_b_out, m_kv_norm_g, m_kv_ada_w, m_kv_ada_b, m_kv_w, m_forget_b, m_attn_w_q, m_attn_w_o, m_final_norm_g, v_mix_norm_g, v_mix_ada_w, v_mix_ada_b, v_ffn_norm_g, v_ffn_ada_w, v_ffn_ada_b, v_ffn_w_in, v_ffn_w_out, v_conv_w_in, v_conv_b_in, v_conv_w_dw, v_conv_b_dw, v_conv_ln_g, v_conv_ln_b, v_conv_w_out, v_conv_b_out, v_kv_norm_g, v_kv_ada_w, v_kv_ada_b, v_kv_w, v_forget_b, v_attn_w_q, v_attn_w_o, v_final_norm_g):
    xi, yi, ci = lax.axis_index("x"), lax.axis_index("y"), lax.axis_index("c")
    chip = 2 * xi + yi
    dev = 4 * xi + 2 * yi + ci
    s, d = x.shape[1], x.shape[2]
    f = ffn_w_out.shape[1] * 4
    x0 = x[0]
    nkv = kv_w.shape[1]
    nkv_all = 4 * nkv

    wdw_loc = _pad_rows(conv_w_dw[0], HALO, 0)
    small = jnp.concatenate([c.reshape(-1), conv_b_in.reshape(-1), wdw_loc.reshape(-1), conv_b_dw.reshape(-1),
                             conv_ln_g.reshape(-1), conv_ln_b.reshape(-1), conv_b_out.reshape(-1)])
    n_small = small.shape[0]
    w_small = -(-n_small // (8 * LANE)) * LANE
    small = jnp.pad(small, (0, 8 * w_small - n_small)).reshape(8, w_small)
    small_all = _allgather8(small, "ag_small_params", True).reshape(8, 8 * w_small)
    c_all = small_all[:, :d]
    per_chip = small_all[0::2]
    dq_ = d // 4
    o1 = d
    b_in_full = per_chip[:, o1:o1 + 2 * dq_].reshape(4, 1, 2 * dq_)
    o1 += 2 * dq_
    wdw_full = per_chip[:, o1:o1 + HALO * dq_].reshape(4, HALO, dq_).transpose(1, 0, 2).reshape(HALO, d)
    o1 += HALO * dq_
    bdw_full = per_chip[:, o1:o1 + dq_].reshape(1, d)
    lng_full = per_chip[:, o1 + dq_:o1 + 2 * dq_].reshape(1, d)
    lnb_full = per_chip[:, o1 + 2 * dq_:o1 + 3 * dq_].reshape(1, d)
    bout_full = per_chip[:, o1 + 3 * dq_:o1 + 4 * dq_].reshape(1, d)

    a_mix = _ada_fwd(c_all, mix_ada_w, "ada_mix")
    a_ffn = _ada_fwd(c_all, ffn_ada_w, "ada_ffn")
    a_kv = _ada_fwd(c_all, kv_ada_w[None], "ada_kv")
    n3 = mix_ada_w.shape[2]
    n2 = kv_ada_w.shape[1]
    ada_loc = jnp.concatenate([a_mix[0], a_mix[1], a_ffn[0], a_ffn[1], a_kv[0]], axis=1)
    w_ada = ada_loc.shape[1]
    ada_all = _allgather8(ada_loc, "ag_ada", True).reshape(8, 8, w_ada)
    ada_me = lax.dynamic_index_in_dim(ada_all, dev, axis=1, keepdims=False)[0::2]

    def ada_vec(off, n, bias):
        return ada_me[:, off:off + n].reshape(1, 4 * n) + bias.reshape(1, -1)

    ada_m0 = ada_vec(0, n3, mix_ada_b[0])
    ada_m1 = ada_vec(n3, n3, mix_ada_b[1])
    ada_f0 = ada_vec(2 * n3, n3, ffn_ada_b[0])
    ada_f1 = ada_vec(3 * n3, n3, ffn_ada_b[1])
    ada_k = ada_vec(4 * n3, n2, kv_ada_b)

    def split3(a):
        return a[:, :d], a[:, d:2 * d], a[:, 2 * d:3 * d]

    sh_m0, sc_m0, gt_m0 = split3(ada_m0)
    sh_m1, sc_m1, gt_m1 = split3(ada_m1)
    sh_f0, sc_f0, gt_f0 = split3(ada_f0)
    sh_f1, sc_f1, gt_f1 = split3(ada_f1)
    sh_k, sc_k = ada_k[:, :d], ada_k[:, d:2 * d]

    def my_halves(ws):
        return [lax.dynamic_index_in_dim(w.astype(BF16).reshape(2, w.shape[0] // 2, w.shape[1]), ci, axis=0, keepdims=False)
                for w in ws]

    def whole(gath, ws):
        return [g.reshape(4, w.shape[0], w.shape[1]) for g, w in zip(gath, ws)]

    grp_conv = [conv_w_in[0]]
    grp_ffn0 = [ffn_w_in[0], ffn_w_out[0], conv_w_out[0]]
    grp_rest = [ffn_w_in[1], ffn_w_out[1], kv_w, attn_w_q[0], attn_w_o[0]]
    mine_conv, mine_ffn0, mine_rest = my_halves(grp_conv), my_halves(grp_ffn0), my_halves(grp_rest)
    cw_in, = whole(_exchange(_Gather8(mine_conv), mine_conv, "ag_w_conv"), grp_conv)

    zero_b = jnp.zeros((1, d), F32)
    g_m0, g_m1 = _vec(mix_norm_g[0]), _vec(mix_norm_g[1])
    g_f0, g_f1 = _vec(ffn_norm_g[0]), _vec(ffn_norm_g[1])
    g_k, g_fin = _vec(kv_norm_g), _vec(final_norm_g)
    fb = jnp.pad(forget_b, (0, LANE - N_HEADS)).reshape(1, LANE)

    h0, glu, a_s, g_s, gath_ffn0 = _in_pair(x0, g_m0, sh_m0, sc_m0, cw_in, b_in_full, True, "conv_in",
                                            comm=(_Gather8(mine_ffn0), mine_ffn0))
    dwo, sw, gath_rest = _dwconv_fwd(glu, wdw_full, bdw_full, lng_full, lnb_full, comm=(_Gather8(mine_rest), mine_rest))
    w_in0, w_out0, cw_out = whole(gath_ffn0, grp_ffn0)
    cw_out = cw_out.reshape(d, d)
    w_in1, w_out1, kvw, wq, wo = whole(gath_rest, grp_rest)
    w_in = [w_in0, w_in1]
    w_out = [w_out0.reshape(f, d), w_out1.reshape(f, d)]
    kvw = kvw.transpose(1, 0, 2).reshape(d, nkv_all)
    wk, wv = kvw[:, :d], kvw[:, d:2 * d]
    wf = jnp.pad(kvw[:, 2 * d:], ((0, 0), (0, LANE - N_HEADS)))
    wq, wo = wq.reshape(d, d), wo.reshape(d, d)
    x1 = _mm_res(sw, cw_out, bout_full, gt_m0, x0, "conv_out")
    hf0, act0, ug0, uu0 = _in_pair(x1, g_f0, sh_f0, sc_f0, w_in[0], None, False, "ffn0_in")
    x2 = _mm_res(act0, w_out[0], zero_b, gt_f0, x1, "ffn0_out")
    hk, h1, kk, vv, qq, flog = _qkv(x2, (g_k, sh_k, sc_k), (g_m1, sh_m1, sc_m1), wk, wv, wf, wq)
    cum, cumt = _cumsum_fwd(flog, fb)
    kaug, kaugt, vtr = _attn_prep(kk, vv, cum)
    o, lse = _attn_fwd(qq, kaug, vtr, cumt)
    x3 = _mm_res(o, wo, zero_b, gt_m1, x2, "attn_out")
    hf1, act1, ug1, uu1 = _in_pair(x3, g_f1, sh_f1, sc_f1, w_in[1], None, False, "ffn1_in")
    x4 = _mm_res(act1, w_out[1], zero_b, gt_f1, x3, "ffn1_out")
    lsum, dx4, d_gfin = _final(x4, g_fin, loss_target[0])
    loss = lax.psum(0.5 / d * jnp.sum(lsum), ("x", "y", "c"))

    nf = f // 2

    sel = jnp.stack([ci, chip]).astype(jnp.int32)

    def reduce_begin(gs, tag):
        ps = [g.reshape(4, 2, g.shape[1] // 2, g.shape[2]) for g in gs]
        lands = _exchange(_SwapHalves(ps), ps, tag + "_swap")
        pairs = [_add_halves(p_, l_, sel, f"{tag}_add{k}") for k, (p_, l_) in enumerate(zip(ps, lands))]
        return [q for q, _ in pairs], [o_ for _, o_ in pairs]

    def reduce_sum(owns, lands, tag):
        return [_add_chips(o_, l_, sel, f"{tag}_sum{k}") for k, (o_, l_) in enumerate(zip(owns, lands))]

    def ffn_bwd(dx_out, x_in, hf, act, ug, uu, gain, scale, gate, w_in_l, w_out_l, tag, comm=None):
        res = _ffn_bwd_act(dx_out, gate, w_out_l, ug, uu, tag + "_bwd_act", comm=comm)
        dug, duu = res[0], res[1]
        dw_out, dgate = _dw_mm(act, [dx_out], nf, d, tag + "_dw_out", gate=gate, wfull=w_out_l, dgate_init=zero_b)
        terms = [(dug, 0, w_in_l, 0), (dug, nf, w_in_l, 1), (duu, 0, w_in_l, 2), (duu, nf, w_in_l, 3)]
        dx_in, dsh, dsc, dgn = _mm_normbwd(terms, x_in, dx_out, gain, scale, tag + "_bwd_in")
        dw_in = _dw_mm(hf, [dug, duu], d, nf, tag + "_dw_in")
        return dx_in, dw_in, dw_out[0], dsh, dsc, dgate, dgn, (res[2] if comm is not None else None)

    dx3, dw_in1, dw_out1, dsh_f1, dsc_f1, dgt_f1, dgn_f1, _ = ffn_bwd(dx4, x3, hf1, act1, ug1, uu1, g_f1, sc_f1, gt_f1, w_in[1], w_out[1], "ffn1")
    q16_1, own_1 = reduce_begin([dw_in1, dw_out1.reshape(4, f // 4, d)], "rs_ffn1")

    do, deltat = _do_kernel(dx3, gt_m1, wo, o)
    dwo_att, dgt_m1 = _dw_mm(o, [dx3], d, d, "attn_dw_o", gate=gt_m1, wfull=wo, dgate_init=zero_b)
    dq, dk, dv, dcq, dck, land_1 = _attn_bwd(qq, do, kaug, kaugt, vv, cumt, lse, deltat, comm=(_ScatterChips(q16_1), q16_1))
    wq3 = wq.reshape(1, d, d)
    dx2a, dsh_m1, dsc_m1, dgn_m1 = _mm_normbwd([(dq, 0, wq3, 0)], x2, dx3, g_m1, sc_m1, "attn_bwd_q")
    dwq = _dw_mm(h1, [dq], d, d, "attn_dw_q")[0]

    df, dfb = _cumsum_bwd(dcq[:, :2].reshape(N_HEADS, s), dck[:, :2].reshape(N_HEADS, s), flog, fb)
    terms = [(dk, 0, wk.reshape(1, d, d), 0), (dv, 0, wv.reshape(1, d, d), 0), (df, 0, wf.reshape(1, d, LANE), 0)]
    dx2, dsh_k, dsc_k, dgn_k = _mm_normbwd(terms, x2, dx2a, g_k, sc_k, "kv_bwd")
    dwk, dwv = _dw_mm(hk, [dk, dv], d, d, "kv_dw_kv")
    dwf = _dw_mm(hk, [df], d, LANE, "kv_dw_f")[0]
    dkvw = jnp.concatenate([dwk, dwv, dwf[:, :N_HEADS]], axis=1)
    dkvw = dkvw.reshape(d, 4, nkv).transpose(1, 0, 2)

    q16_2, own_2 = reduce_begin([dkvw, dwq.reshape(4, d // 4, d), dwo_att[0].reshape(4, d // 4, d)], "rs_attn")
    dx1, dw_in0, dw_out0, dsh_f0, dsc_f0, dgt_f0, dgn_f0, land_2 = ffn_bwd(
        dx2, x1, hf0, act0, ug0, uu0, g_f0, sc_f0, gt_f0, w_in[0], w_out[0], "ffn0", comm=(_ScatterChips(q16_2), q16_2))

    ddwo, d_lng, d_lnb, d_bdw, d_bout, dgt_extra = _conv_bwd1(dx1, gt_m0, cw_out, bout_full, dwo, lng_full, lnb_full)
    dcw_out, dgt_m0 = _dw_mm(sw, [dx1], d, d, "conv_dw_out", gate=gt_m0, wfull=cw_out, dgate_init=dgt_extra)
    q16_3, own_3 = reduce_begin([dw_in0, dw_out0.reshape(4, f // 4, d), dcw_out[0].reshape(4, d // 4, d)], "rs_ffn0")
    da, dg, d_wdw, d_bin_a, d_bin_g, land_3 = _dwconv_bwd(ddwo, glu, a_s, g_s, wdw_full, comm=(_ScatterChips(q16_3), q16_3))
    nc = cw_in.shape[2]
    terms = [(da, 0, cw_in, 0), (da, nc, cw_in, 1), (dg, 0, cw_in, 2), (dg, nc, cw_in, 3)]
    dx0, dsh_m0, dsc_m0, dgn_m0 = _mm_normbwd(terms, x0, dx1, g_m0, sc_m0, "conv_bwd_in")
    dcw_in = _dw_mm(h0, [da, dg], d, nc, "conv_dw_in")
    q16_4, own_4 = reduce_begin([dcw_in], "rs_conv")
    land_4 = _exchange(_ScatterChips(q16_4), q16_4, "rs_conv_scatter")

    sums = (reduce_sum(own_1, land_1, "rs_ffn1") + reduce_sum(own_2, land_2, "rs_attn")
            + reduce_sum(own_3, land_3, "rs_ffn0") + reduce_sum(own_4, land_4, "rs_conv"))
    reduced = [b.reshape(2 * b.shape[1], b.shape[2]) for b in _share_halves(sums)]
    g_w_in1, g_w_out1, g_kvw, g_wq, g_wo, g_w_in0, g_w_out0, g_cw_out, g_cw_in = reduced

    d_ada = [jnp.concatenate([dsh_m0, dsc_m0, dgt_m0], axis=1), jnp.concatenate([dsh_m1, dsc_m1, dgt_m1], axis=1),
             jnp.concatenate([dsh_f0, dsc_f0, dgt_f0], axis=1), jnp.concatenate([dsh_f1, dsc_f1, dgt_f1], axis=1),
             jnp.concatenate([dsh_k, dsc_k], axis=1)]
    fields = d_ada + [dgn_m0, dgn_m1, dgn_f0, dgn_f1, dgn_k, d_gfin, d_bin_a, d_bin_g, d_bdw, d_lng, d_lnb, d_bout,
                      d_wdw.reshape(1, -1), dfb]
    foffs = [0]
    for fl in fields:
        foffs.append(foffs[-1] + fl.shape[1])
    n_row = foffs[-1]
    w_row = -(-n_row // (8 * LANE)) * LANE
    row = jnp.pad(jnp.concatenate(fields, axis=1), ((0, 0), (0, 8 * w_row - n_row))).reshape(8, w_row)
    rows_all = _allgather8(row, "ag_small_grads", True).reshape(8, 8, w_row)
    rsum_small = _sum8(rows_all).reshape(1, 8 * w_row)
    rows_flat = rows_all.reshape(8, 8 * w_row)

    def fsum(i):
        return rsum_small[:, foffs[i]:foffs[i + 1]]

    cat = _silu_rows(c_all).T

    def ada_cols(i, n):
        full = rows_flat[:, foffs[i]:foffs[i + 1]].reshape(8, 4, n)
        return lax.dynamic_index_in_dim(full, chip, axis=1, keepdims=False)

    g_mix_ada_w = _ada_wgrad(cat, jnp.stack([ada_cols(0, n3), ada_cols(1, n3)]), "ada_mix_wgrad")
    g_ffn_ada_w = _ada_wgrad(cat, jnp.stack([ada_cols(2, n3), ada_cols(3, n3)]), "ada_ffn_wgrad")
    g_kv_ada_w = _ada_wgrad(cat, ada_cols(4, n2)[None], "ada_kv_wgrad")[0]

    def my_cols(v, n):
        return lax.dynamic_index_in_dim(v.reshape(4, n), chip, axis=0, keepdims=False)

    grads = {
        "mix_norm_g": jnp.concatenate([fsum(5), fsum(6)], axis=0),
        "mix_ada_w": g_mix_ada_w,
        "mix_ada_b": jnp.concatenate([fsum(0), fsum(1)], axis=0),
        "ffn_norm_g": jnp.concatenate([fsum(7), fsum(8)], axis=0),
        "ffn_ada_w": g_ffn_ada_w,
        "ffn_ada_b": jnp.concatenate([fsum(2), fsum(3)], axis=0),
        "ffn_w_in": jnp.stack([g_w_in0, g_w_in1]),
        "ffn_w_out": jnp.stack([g_w_out0, g_w_out1]),
        "conv_w_in": g_cw_in[None],
        "conv_b_in": my_cols(jnp.concatenate([fsum(11), fsum(12)], axis=1), 2 * dq_)[None],
        "conv_w_dw": lax.dynamic_index_in_dim(fsum(17).reshape(HALO, 4, dq_), chip, axis=1, keepdims=False)[:CONV_K][None],
        "conv_b_dw": my_cols(fsum(13), dq_)[None],
        "conv_ln_g": my_cols(fsum(14), dq_)[None],
        "conv_ln_b": my_cols(fsum(15), dq_)[None],
        "conv_w_out": g_cw_out[None],
        "conv_b_out": my_cols(fsum(16), dq_)[None],
        "kv_norm_g": fsum(9).reshape(-1),
        "kv_ada_w": g_kv_ada_w,
        "kv_ada_b": fsum(4).reshape(-1),
        "kv_w": g_kvw,
        "forget_b": fsum(18).reshape(-1)[:N_HEADS],
        "attn_w_q": g_wq[None],
        "attn_w_o": g_wo[None],
        "final_norm_g": fsum(10).reshape(-1),
    }
    weights = dict(mix_norm_g=mix_norm_g, mix_ada_w=mix_ada_w, mix_ada_b=mix_ada_b, ffn_norm_g=ffn_norm_g, ffn_ada_w=ffn_ada_w, ffn_ada_b=ffn_ada_b, ffn_w_in=ffn_w_in, ffn_w_out=ffn_w_out, conv_w_in=conv_w_in, conv_b_in=conv_b_in, conv_w_dw=conv_w_dw, conv_b_dw=conv_b_dw, conv_ln_g=conv_ln_g, conv_ln_b=conv_ln_b, conv_w_out=conv_w_out, conv_b_out=conv_b_out, kv_norm_g=kv_norm_g, kv_ada_w=kv_ada_w, kv_ada_b=kv_ada_b, kv_w=kv_w, forget_b=forget_b, attn_w_q=attn_w_q, attn_w_o=attn_w_o, final_norm_g=final_norm_g)
    moms = dict(mix_norm_g=(m_mix_norm_g, v_mix_norm_g), mix_ada_w=(m_mix_ada_w, v_mix_ada_w), mix_ada_b=(m_mix_ada_b, v_mix_ada_b), ffn_norm_g=(m_ffn_norm_g, v_ffn_norm_g), ffn_ada_w=(m_ffn_ada_w, v_ffn_ada_w), ffn_ada_b=(m_ffn_ada_b, v_ffn_ada_b), ffn_w_in=(m_ffn_w_in, v_ffn_w_in), ffn_w_out=(m_ffn_w_out, v_ffn_w_out), conv_w_in=(m_conv_w_in, v_conv_w_in), conv_b_in=(m_conv_b_in, v_conv_b_in), conv_w_dw=(m_conv_w_dw, v_conv_w_dw), conv_b_dw=(m_conv_b_dw, v_conv_b_dw), conv_ln_g=(m_conv_ln_g, v_conv_ln_g), conv_ln_b=(m_conv_ln_b, v_conv_ln_b), conv_w_out=(m_conv_w_out, v_conv_w_out), conv_b_out=(m_conv_b_out, v_conv_b_out), kv_norm_g=(m_kv_norm_g, v_kv_norm_g), kv_ada_w=(m_kv_ada_w, v_kv_ada_w), kv_ada_b=(m_kv_ada_b, v_kv_ada_b), kv_w=(m_kv_w, v_kv_w), forget_b=(m_forget_b, v_forget_b), attn_w_q=(m_attn_w_q, v_attn_w_q), attn_w_o=(m_attn_w_o, v_attn_w_o), final_norm_g=(m_final_norm_g, v_final_norm_g))
    names = list(weights)

    deltas, new_m, new_v = {}, {}, {}
    small_names = [n for n in names if weights[n].size < (1 << 16)]
    for n in names:
        if n in small_names:
            continue
        w = weights[n]
        w2 = w.reshape(-1, w.shape[-1])
        dl, nm, nv = _adamw(w2, grads[n].reshape(w2.shape), moms[n][0].reshape(w2.shape), moms[n][1].reshape(w2.shape), "adamw_" + n)
        deltas[n], new_m[n], new_v[n] = dl.reshape(w.shape), nm.reshape(w.shape), nv.reshape(w.shape)

    def pack_small(get):
        flat = jnp.concatenate([get(n).reshape(-1) for n in small_names])
        rows_ = -(-flat.shape[0] // (8 * LANE)) * 8
        return jnp.pad(flat, (0, rows_ * LANE - flat.shape[0])).reshape(rows_, LANE)

    ws, gs = pack_small(lambda n: weights[n]), pack_small(lambda n: grads[n])
    ms_, vs_ = pack_small(lambda n: moms[n][0]), pack_small(lambda n: moms[n][1])
    vs_ = jnp.where(jnp.arange(vs_.size).reshape(vs_.shape) < sum(weights[n].size for n in small_names), vs_, 1.0)
    dl, nm, nv = _adamw(ws, gs, ms_, vs_, "adamw_small")
    off = 0
    for n in small_names:
        sz = weights[n].size
        shp = weights[n].shape
        deltas[n] = dl.reshape(-1)[off:off + sz].reshape(shp)
        new_m[n] = nm.reshape(-1)[off:off + sz].reshape(shp)
        new_v[n] = nv.reshape(-1)[off:off + sz].reshape(shp)
        off += sz

    grad_out = [grads[n].reshape(weights[n].shape) for n in names]
    return (loss, dx0[None], *grad_out, *[deltas[n] for n in names], *[new_m[n] for n in names], *[new_v[n] for n in names])
```

```python
import functools

import jax
import jax.numpy as jnp
from jax import lax
from jax.experimental import pallas as pl
from jax.experimental.pallas import tpu as pltpu

F32 = jnp.float32
BF16 = jnp.bfloat16
MESH = pl.DeviceIdType.MESH

EPS = 1e-6
N_HEADS = 16
HEAD_DIM = 64
CONV_K = 31
LANE = 128
SUBLANES = 8
HALO = 32
ATT_FWD_TQ = 2048
ATT_TQ = 1024
ATT_TK = 512
NPIECE = 3
SPARE = (HEAD_DIM, 0)
VMEM_MB = 48
ATT_BWD_VMEM_MB = 56

ADAM_LR = 0.001
ADAM_B1 = 0.9
ADAM_B2 = 0.999
ADAM_EPS = 1e-08
ADAM_WD = 0.01
ADAM_STEP = 10


def _sds(shape, dtype):
    return jax.ShapeDtypeStruct(tuple(shape), dtype)


def _cp(sem=None, vmem_mb=VMEM_MB):
    return pltpu.CompilerParams(dimension_semantics=sem, vmem_limit_bytes=vmem_mb << 20)


def _tile(n, pref):
    return pref if n % pref == 0 else n


def _row_tile(r, mult, width=1024):
    cap = max(mult, (512 * 1024 // width) // mult * mult)
    for cand in range(cap, mult - 1, -mult):
        if r % cand == 0:
            return cand
    return r


def _resident(shape):
    nd = len(shape)
    return pl.BlockSpec(tuple(shape), lambda *_: (0,) * nd, pipeline_mode=pl.Buffered(1))


def _dot(a, b):
    return jnp.dot(a, b, preferred_element_type=F32)


def _dot_nt(a, b):
    return lax.dot_general(a, b, (((1,), (1,)), ((), ())), preferred_element_type=F32)


def _dot_tn(a, b):
    return lax.dot_general(a, b, (((0,), (0,)), ((), ())), preferred_element_type=F32)


def _sigmoid(x):
    return 1.0 / (1.0 + jnp.exp(-x))


def _colsum(x):
    return jnp.sum(x, axis=0, keepdims=True)


def _rms_parts(x):
    rstd = lax.rsqrt(jnp.mean(x * x, axis=-1, keepdims=True) + EPS)
    return x * rstd, rstd


class _Gather8:
    def __init__(self, xs):
        self.n = len(xs)
        self.m = [x.shape[0] for x in xs]
        self.land = [_sds((8 * x.shape[0],) + tuple(x.shape[1:]), x.dtype) for x in xs]
        self.sems = [pltpu.SemaphoreType.DMA((7 * self.n,)), pltpu.SemaphoreType.DMA((7 * self.n,)),
                     pltpu.SemaphoreType.DMA((self.n,))]

    def _parts(self, a, x_refs, out_refs, send_sems, recv_sems, local_sems):
        x, y, c = lax.axis_index("x"), lax.axis_index("y"), lax.axis_index("c")
        me, sibling = (x, y, c), (x, y, 1 - c)
        chips = [(1 - x, y), (x, 1 - y), (1 - x, 1 - y)]
        m_per, x_ref, out_ref = self.m[a], x_refs[a], out_refs[a]

        def rows(px, py, pc):
            return out_ref.at[pl.ds((4 * px + 2 * py + pc) * m_per, m_per)]

        def copy(k, block, to, src=None):
            return pltpu.make_async_remote_copy(
                src_ref=rows(*block) if src is None else src, dst_ref=rows(*block),
                send_sem=send_sems.at[7 * a + k], recv_sem=recv_sems.at[7 * a + k], device_id=to, device_id_type=MESH)

        mine = pltpu.make_async_copy(x_ref, rows(*me), local_sems.at[a])
        first = [copy(0, me, sibling, src=x_ref)]
        first += [copy(1 + j, me, (*chip, c), src=x_ref) for j, chip in enumerate(chips)]
        passed = [copy(4 + j, (*chip, c), sibling) for j, chip in enumerate(chips)]
        return c, me, sibling, chips, copy, mine, first, passed

    def start(self, *refs):
        for a in range(self.n):
            _, _, _, _, _, mine, first, _ = self._parts(a, *refs)
            mine.start()
            for cp in first:
                cp.start()

    def finish(self, *refs):
        parts = [self._parts(a, *refs) for a in range(self.n)]
        for j in range(3):
            for c, me, sibling, chips, copy, mine, first, passed in parts:
                copy(1 + j, (*chips[j], c), me).wait_recv()
                passed[j].start()
        for c, me, sibling, chips, copy, mine, first, passed in parts:
            copy(0, sibling, me).wait_recv()
            for j, chip in enumerate(chips):
                copy(4 + j, (*chip, 1 - c), me).wait_recv()
            for cp in first + passed:
                cp.wait_send()
            mine.wait()


class _ScatterChips:
    def __init__(self, qs):
        self.n = len(qs)
        self.land = [_sds((3,) + tuple(q.shape[1:]), q.dtype) for q in qs]
        self.sems = [pltpu.SemaphoreType.DMA((3 * self.n,)), pltpu.SemaphoreType.DMA((3 * self.n,))]

    def _copies(self, q_refs, land_refs, send_sems, recv_sems):
        x, y, c = lax.axis_index("x"), lax.axis_index("y"), lax.axis_index("c")
        chips = [(1 - x, y), (x, 1 - y), (1 - x, 1 - y)]
        return [pltpu.make_async_remote_copy(
            src_ref=q_refs[a].at[2 * cx + cy], dst_ref=land_refs[a].at[k],
            send_sem=send_sems.at[3 * a + k], recv_sem=recv_sems.at[3 * a + k],
            device_id=(cx, cy, c), device_id_type=MESH) for a in range(self.n) for k, (cx, cy) in enumerate(chips)]

    def start(self, *refs):
        for cp in self._copies(*refs):
            cp.start()

    def finish(self, *refs):
        copies = self._copies(*refs)
        for cp in copies:
            cp.wait_recv()
        for cp in copies:
            cp.wait_send()


class _SwapHalves:
    def __init__(self, ps):
        self.n = len(ps)
        self.land = [_sds((p.shape[0],) + tuple(p.shape[2:]), p.dtype) for p in ps]
        self.nb = [p.shape[0] for p in ps]
        tot = sum(self.nb)
        self.sems = [pltpu.SemaphoreType.DMA((tot,)), pltpu.SemaphoreType.DMA((tot,))]

    def _copies(self, p_refs, land_refs, send_sems, recv_sems):
        x, y, c = lax.axis_index("x"), lax.axis_index("y"), lax.axis_index("c")
        out, k = [], 0
        for a in range(self.n):
            for j in range(self.nb[a]):
                out.append(pltpu.make_async_remote_copy(
                    src_ref=p_refs[a].at[j, 1 - c], dst_ref=land_refs[a].at[j], send_sem=send_sems.at[k],
                    recv_sem=recv_sems.at[k], device_id=(x, y, 1 - c), device_id_type=MESH))
                k += 1
        return out

    start = _ScatterChips.start
    finish = _ScatterChips.finish


def _hosted_call(body, comm, *, grid, in_specs, out_specs, out_shape, scratch_shapes, name, sem, args, vmem_mb=VMEM_MB):
    def first():
        return functools.reduce(jnp.logical_and, [pl.program_id(a) == 0 for a in range(len(grid))])

    def last():
        return functools.reduce(jnp.logical_and, [pl.program_id(a) == g - 1 for a, g in enumerate(grid)])

    out_specs = tuple(out_specs) if isinstance(out_specs, (tuple, list)) else (out_specs,)
    out_shape = tuple(out_shape) if isinstance(out_shape, (tuple, list)) else (out_shape,)
    if comm is None:
        return pl.pallas_call(body, grid=grid, in_specs=list(in_specs), out_specs=out_specs, out_shape=out_shape,
                              scratch_shapes=list(scratch_shapes), name=name, compiler_params=_cp(sem, vmem_mb))(*args)
    ex, srcs = comm
    n_in, n_out, n_scr, n_ex = len(in_specs), len(out_shape), len(scratch_shapes), ex.n

    def wrapped(*refs):
        ins, src_refs = refs[:n_in], refs[n_in:n_in + n_ex]
        o0 = n_in + n_ex
        outs, land_refs = refs[o0:o0 + n_out], refs[o0 + n_out:o0 + n_out + n_ex]
        s0 = o0 + n_out + n_ex
        scr, sems = refs[s0:s0 + n_scr], refs[s0 + n_scr:]

        @pl.when(first())
        def _():
            ex.start(src_refs, land_refs, *sems)

        body(*ins, *outs, *scr)

        @pl.when(last())
        def _():
            ex.finish(src_refs, land_refs, *sems)

    hbm = pl.BlockSpec(memory_space=pl.ANY)
    res = pl.pallas_call(
        wrapped, grid=grid, in_specs=[*in_specs, *[hbm] * n_ex], out_specs=(*out_specs, *[hbm] * n_ex),
        out_shape=(*out_shape, *ex.land), scratch_shapes=[*scratch_shapes, *ex.sems], name=name,
        compiler_params=_cp(tuple("arbitrary" for _ in grid), vmem_mb))(*args, *srcs)
    return (*res[:n_out], list(res[n_out:]))


def _exchange(ex, srcs, name, in_vmem=False):
    n = ex.n

    def body(*refs):
        src_refs, land_refs, sems = refs[:n], refs[n:2 * n], refs[2 * n:]
        ex.start(src_refs, land_refs, *sems)
        ex.finish(src_refs, land_refs, *sems)

    spec = pl.BlockSpec(memory_space=pltpu.VMEM if in_vmem else pl.ANY)
    return list(pl.pallas_call(
        body, out_shape=tuple(ex.land), in_specs=[spec] * n, out_specs=tuple([spec] * n),
        scratch_shapes=ex.sems, name=name)(*srcs))


def _allgather8(x_shard, name, in_vmem):
    return _exchange(_Gather8([x_shard]), [x_shard], name, in_vmem)[0]


def _share_halves(bufs):
    n = len(bufs)

    def body(*refs):
        b_refs, out_refs, send_sems, recv_sems = refs[:n], refs[n:2 * n], refs[2 * n], refs[2 * n + 1]
        x, y, c = lax.axis_index("x"), lax.axis_index("y"), lax.axis_index("c")
        copies = [pltpu.make_async_remote_copy(
            src_ref=b_refs[k].at[c], dst_ref=out_refs[k].at[c], send_sem=send_sems.at[k], recv_sem=recv_sems.at[k],
            device_id=(x, y, 1 - c), device_id_type=MESH) for k in range(n)]
        for cp in copies:
            cp.start()
        for cp in copies:
            cp.wait_recv()
        for cp in copies:
            cp.wait_send()

    hbm = pl.BlockSpec(memory_space=pl.ANY)
    return pl.pallas_call(
        body, out_shape=tuple(_sds(b.shape, b.dtype) for b in bufs), in_specs=[hbm] * n, out_specs=tuple([hbm] * n),
        scratch_shapes=[pltpu.SemaphoreType.DMA((n,)), pltpu.SemaphoreType.DMA((n,))],
        input_output_aliases={k: k for k in range(n)}, name="rs_share_halves")(*bufs)


def _add_halves(p, land, sel, name):
    nb, _, r, w = p.shape
    tr = _row_tile(r, 16, w)

    def body(sel_ref, p_ref, l_ref, q16_ref, own_ref):
        q = p_ref[0, 0] + l_ref[0]
        q16_ref[0] = q.astype(BF16)

        @pl.when(pl.program_id(1) == sel_ref[1])
        def _():
            own_ref[...] = q

    gs = pltpu.PrefetchScalarGridSpec(
        num_scalar_prefetch=1, grid=(r // tr, nb),
        in_specs=[pl.BlockSpec((1, 1, tr, w), lambda i, j, sl: (j, sl[0], i, 0)),
                  pl.BlockSpec((1, tr, w), lambda i, j, sl: (j, i, 0))],
        out_specs=(pl.BlockSpec((1, tr, w), lambda i, j, sl: (j, i, 0)), pl.BlockSpec((tr, w), lambda i, j, sl: (i, 0))))
    return pl.pallas_call(body, grid_spec=gs, out_shape=(_sds((nb, r, w), BF16), _sds((r, w), F32)), name=name,
                          compiler_params=_cp(("parallel", "arbitrary")))(sel, p, land)


def _add_chips(own, land, sel, name):
    r, w = own.shape
    tr = _row_tile(r, 16, w)

    def body(sel_ref, q_ref, l_ref, o_ref):
        o_ref[0] = ((q_ref[...] + l_ref[0].astype(F32)) + l_ref[1].astype(F32)) + l_ref[2].astype(F32)

    gs = pltpu.PrefetchScalarGridSpec(
        num_scalar_prefetch=1, grid=(r // tr,),
        in_specs=[pl.BlockSpec((tr, w), lambda i, sl: (i, 0)), pl.BlockSpec((3, tr, w), lambda i, sl: (0, i, 0))],
        out_specs=pl.BlockSpec((1, tr, w), lambda i, sl: (sl[0], i, 0)))
    return pl.pallas_call(body, grid_spec=gs, out_shape=_sds((2, r, w), F32), name=name,
                          compiler_params=_cp(("parallel",)))(sel, own, land)


def _sum8(g):
    _, m, n = g.shape

    def body(g_ref, o_ref):
        acc = g_ref[0]
        for k in range(1, 8):
            acc = acc + g_ref[k]
        o_ref[...] = acc

    return pl.pallas_call(body, out_shape=_sds((m, n), g.dtype), name="sum8")(g)


def _ada_fwd(c_all, w3, name):
    nl, d, n = w3.shape
    tn = 256

    def body(c_ref, w_ref, o_ref):
        cc = c_ref[...]
        ca = (cc * _sigmoid(cc)).astype(BF16)
        o_ref[0] = _dot(ca, w_ref[0].astype(BF16))

    return pl.pallas_call(
        body, grid=(nl, n // tn), out_shape=_sds((nl, 8, n), F32),
        in_specs=[pl.BlockSpec((8, d), lambda l, j: (0, 0)), pl.BlockSpec((1, d, tn), lambda l, j: (l, 0, j))],
        out_specs=pl.BlockSpec((1, 8, tn), lambda l, j: (l, 0, j)),
        name=name, compiler_params=_cp(("parallel", "parallel")))(c_all, w3)


def _in_pair(x, gain, shift, scale, wg, bias, conv, name, comm=None):
    s, d = x.shape
    n = wg.shape[2]
    ts = _tile(s, 512)

    def body(*refs):
        if conv:
            x_ref, g_ref, sh_ref, sc_ref, w_ref, b_ref, h_ref, o_ref, sa_ref, sb_ref = refs
        else:
            x_ref, g_ref, sh_ref, sc_ref, w_ref, h_ref, o_ref, sa_ref, sb_ref = refs
        xhat, _ = _rms_parts(x_ref[...])
        h = ((xhat * g_ref[...]) * (1.0 + sc_ref[...]) + sh_ref[...]).astype(BF16)
        h_ref[...] = h
        for q in range(2):
            a = _dot(h, w_ref[q])
            b = _dot(h, w_ref[q + 2])
            cs = pl.ds(q * n, n)
            if conv:
                a = a + b_ref[q]
                b = b + b_ref[q + 2]
                o_ref[:, cs] = a * _sigmoid(b)
            else:
                o_ref[:, cs] = (a * _sigmoid(a) * b).astype(BF16)
            sa_ref[:, cs] = a.astype(BF16)
            sb_ref[:, cs] = b.astype(BF16)

    vec = pl.BlockSpec((1, d), lambda i: (0, 0))
    in_specs = [pl.BlockSpec((ts, d), lambda i: (i, 0)), vec, vec, vec, _resident(wg.shape)]
    args = [x, gain, shift, scale, wg]
    if conv:
        in_specs.append(_resident(bias.shape))
        args.append(bias)
    tile = pl.BlockSpec((ts, 2 * n), lambda i: (i, 0))
    return _hosted_call(
        body, comm, grid=(s // ts,),
        out_shape=(_sds((s, d), BF16), _sds((s, 2 * n), F32 if conv else BF16), _sds((s, 2 * n), BF16), _sds((s, 2 * n), BF16)),
        in_specs=in_specs, out_specs=(pl.BlockSpec((ts, d), lambda i: (i, 0)), tile, tile, tile),
        scratch_shapes=[], name=name, sem=("parallel",), args=args)


def _shift_copies(buf, shf):
    n = shf.shape[1]
    for r in range(1, SUBLANES):
        shf[r - 1, :, :] = buf[pl.ds(r, n), :]


def _shifted(buf, shf, start, n, cs):
    a, r = divmod(start, SUBLANES)
    if r == 0:
        return buf[pl.ds(start, n), cs]
    return shf[r - 1, pl.ds(a * SUBLANES, n), cs]


def _dwconv_fwd(glu, wdw, bdw, lng, lnb, comm=None):
    s, d = glu.shape
    ts = _tile(s, 256)
    rb, cb = 32, 256

    def body(cur_ref, halo_ref, w_ref, b_ref, g_ref, be_ref, dwo_ref, sw_ref, buf, shf):
        i = pl.program_id(0)

        @pl.when(i == 0)
        def _():
            buf[pl.ds(0, HALO), :] = jnp.zeros((HALO, d), F32)

        @pl.when(i > 0)
        def _():
            buf[pl.ds(0, HALO), :] = halo_ref[...]

        buf[pl.ds(HALO, ts), :] = cur_ref[...]
        _shift_copies(buf, shf)
        for r in range(ts // rb):
            for cc in range(d // cb):
                cs = pl.ds(cc * cb, cb)
                acc = jnp.zeros((rb, cb), F32) + b_ref[:, cs]
                for k in range(CONV_K):
                    acc = acc + w_ref[pl.ds(k, 1), cs] * _shifted(buf, shf, HALO - (CONV_K - 1) + k + r * rb, rb, cs)
                dwo_ref[pl.ds(r * rb, rb), cs] = acc
            rows = pl.ds(r * rb, rb)
            yv = dwo_ref[rows, :]
            mu = jnp.mean(yv, axis=-1, keepdims=True)
            yc = yv - mu
            var = jnp.mean(yc * yc, axis=-1, keepdims=True)
            ln = yc * lax.rsqrt(var + EPS) * g_ref[...] + be_ref[...]
            sw_ref[rows, :] = (ln * _sigmoid(ln)).astype(BF16)

    vec = pl.BlockSpec((1, d), lambda i: (0, 0))
    return _hosted_call(
        body, comm, grid=(s // ts,), out_shape=(_sds((s, d), F32), _sds((s, d), BF16)),
        in_specs=[pl.BlockSpec((ts, d), lambda i: (i, 0)),
                  pl.BlockSpec((HALO, d), lambda i: (jnp.maximum(i * (ts // HALO) - 1, 0), 0)),
                  pl.BlockSpec((HALO, d), lambda i: (0, 0)), vec, vec, vec],
        out_specs=(pl.BlockSpec((ts, d), lambda i: (i, 0)), pl.BlockSpec((ts, d), lambda i: (i, 0))),
        scratch_shapes=[pltpu.VMEM((HALO + ts, d), F32), pltpu.VMEM((SUBLANES - 1, HALO + ts - SUBLANES, d), F32)],
        name="dwconv_fwd", sem=("parallel",),
        args=(glu, glu, wdw, bdw, lng, lnb))


def _mm_res(a, w, b, gate, x, name):
    s, k = a.shape
    d = w.shape[1]
    ts = _tile(s, 512)

    def body(a_ref, w_ref, b_ref, g_ref, x_ref, o_ref):
        yv = _dot(a_ref[...], w_ref[...]) + b_ref[...]
        o_ref[...] = x_ref[...] + g_ref[...] * yv

    vec = pl.BlockSpec((1, d), lambda i: (0, 0))
    return pl.pallas_call(
        body, grid=(s // ts,), out_shape=_sds((s, d), F32),
        in_specs=[pl.BlockSpec((ts, k), lambda i: (i, 0)), _resident((k, d)), vec, vec, pl.BlockSpec((ts, d), lambda i: (i, 0))],
        out_specs=pl.BlockSpec((ts, d), lambda i: (i, 0)),
        name=name, compiler_params=_cp(("parallel",)))(a, w, b, gate, x)


def _qkv(x, kvp, mxp, wk, wv, wf, wq):
    s, d = x.shape
    ts = _tile(s, 512)
    qscale = HEAD_DIM ** -0.5

    def body(x_ref, gk, shk, sck, gm, shm, scm, wk_ref, wv_ref, wf_ref, wq_ref, hk_ref, h1_ref, k_ref, v_ref, q_ref, f_ref):
        xhat, _ = _rms_parts(x_ref[...])
        hk = ((xhat * gk[...]) * (1.0 + sck[...]) + shk[...]).astype(BF16)
        h1 = ((xhat * gm[...]) * (1.0 + scm[...]) + shm[...]).astype(BF16)
        hk_ref[...] = hk
        h1_ref[...] = h1
        k_ref[...] = _dot(hk, wk_ref[...]).astype(BF16)
        v_ref[...] = _dot(hk, wv_ref[...]).astype(BF16)
        f_ref[...] = _dot(hk, wf_ref[...])
        q_ref[...] = (_dot(h1, wq_ref[...]) * qscale).astype(BF16)

    vec = pl.BlockSpec((1, d), lambda i: (0, 0))
    row = pl.BlockSpec((ts, d), lambda i: (i, 0))
    return pl.pallas_call(
        body, grid=(s // ts,),
        out_shape=tuple(_sds((s, d), BF16) for _ in range(5)) + (_sds((s, LANE), F32),),
        in_specs=[row, vec, vec, vec, vec, vec, vec, _resident((d, d)), _resident((d, d)), _resident((d, LANE)), _resident((d, d))],
        out_specs=(row, row, row, row, row, pl.BlockSpec((ts, LANE), lambda i: (i, 0))),
        name="qkv_proj", compiler_params=_cp(("parallel",)))(x, *kvp, *mxp, wk, wv, wf, wq)


def _log_sigmoid(z):
    return jnp.minimum(z, 0.0) - jnp.log(1.0 + jnp.exp(-jnp.abs(z)))


def _cumsum_fwd(flog, fb):
    s = flog.shape[0]
    ts = _tile(s, 256)

    def body(f_ref, b_ref, cum_ref, cumt_ref, carry):
        @pl.when(pl.program_id(0) == 0)
        def _():
            carry[...] = jnp.zeros_like(carry)

        ls = _log_sigmoid(f_ref[...] + b_ref[...])
        r = lax.broadcasted_iota(jnp.int32, (ts, ts), 0)
        cidx = lax.broadcasted_iota(jnp.int32, (ts, ts), 1)
        tri = (cidx <= r).astype(F32)
        cs = jnp.dot(tri, ls, preferred_element_type=F32, precision=lax.Precision.HIGHEST) + carry[...]
        cum_ref[...] = cs
        cumt_ref[...] = cs.T
        carry[...] = cs[ts - 1:ts, :]

    return pl.pallas_call(
        body, grid=(s // ts,), out_shape=(_sds((s, LANE), F32), _sds((LANE, s), F32)),
        in_specs=[pl.BlockSpec((ts, LANE), lambda i: (i, 0)), pl.BlockSpec((1, LANE), lambda i: (0, 0))],
        out_specs=(pl.BlockSpec((ts, LANE), lambda i: (i, 0)), pl.BlockSpec((LANE, ts), lambda i: (0, i))),
        scratch_shapes=[pltpu.VMEM((1, LANE), F32)],
        name="forget_cumsum", compiler_params=_cp(("arbitrary",)))(flog, fb)


def _pick_row(m, idx):
    r = lax.broadcasted_iota(jnp.int32, (m.shape[0], 1), 0)
    return jnp.sum(jnp.where(r == idx, m, 0.0), axis=0, keepdims=True)


def _pick_col(m, idx):
    cidx = lax.broadcasted_iota(jnp.int32, (1, m.shape[1]), 1)
    return jnp.sum(jnp.where(cidx == idx, m, 0.0), axis=1, keepdims=True)


def _split3(x):
    hi = x.astype(BF16)
    r1 = x - hi.astype(F32)
    mid = r1.astype(BF16)
    lo = (r1 - mid.astype(F32)).astype(BF16)
    return hi, mid, lo


def _head_mask(lane, hh):
    lo = lane < HEAD_DIM
    return lo if hh == 0 else jnp.logical_not(lo)


def _attn_prep(k, v, cum):
    s, d = k.shape
    npair = d // LANE
    tc = _tile(s, 1024)

    def body(k_ref, v_ref, c_ref, ka_ref, kt_ref, vt_ref):
        p = pl.program_id(0)
        lane = lax.broadcasted_iota(jnp.int32, (1, LANE), 1)
        kk = k_ref[...]
        vv = v_ref[...].astype(F32)
        ckt = c_ref[...]
        for hh in range(2):
            head = _head_mask(lane, hh)
            b = SPARE[hh]
            ck = _pick_col(ckt, 2 * p + hh)
            extra = jnp.where(lane == b + NPIECE, 1.0, 0.0).astype(BF16) + jnp.zeros((tc, LANE), BF16)
            for n_, pc in enumerate(_split3(ck)):
                extra = jnp.where(lane == b + n_, pc, extra)
            ka = jnp.where(head, kk, extra)
            ka_ref[0, hh] = ka
            kt_ref[0, hh] = ka.astype(F32).T.astype(BF16)
            vx = jnp.where(head, vv, jnp.where(lane == b, 1.0, 0.0))
            vt_ref[0, hh] = vx.T.astype(BF16)

    blk = pl.BlockSpec((tc, LANE), lambda p, c: (c, p))
    return pl.pallas_call(
        body, grid=(npair, s // tc),
        out_shape=(_sds((npair, 2, s, LANE), BF16), _sds((npair, 2, LANE, s), BF16), _sds((npair, 2, LANE, s), BF16)),
        in_specs=[blk, blk, pl.BlockSpec((tc, LANE), lambda p, c: (c, 0))],
        out_specs=(pl.BlockSpec((1, 2, tc, LANE), lambda p, c: (p, 0, c, 0)),
                   pl.BlockSpec((1, 2, LANE, tc), lambda p, c: (p, 0, 0, c)),
                   pl.BlockSpec((1, 2, LANE, tc), lambda p, c: (p, 0, 0, c))),
        name="fox_attn_prep", compiler_params=_cp(("parallel", "parallel")))(k, v, cum)


def _q_aug(qq, lane, hh):
    b = SPARE[hh]
    sel = jnp.logical_and(lane >= b, lane < b + NPIECE)
    neg = jnp.full((1, LANE), -1.0, BF16)
    zl = jnp.zeros((1, LANE), BF16)
    return jnp.where(_head_mask(lane, hh), qq, jnp.where(sel, neg, zl))


def _attn_fwd(q, kaug, vtr, cumt):
    s, d = q.shape
    tq = _tile(s, ATT_FWD_TQ)
    tk = _tile(s, ATT_TK)
    npair = d // LANE
    npart = max(1, tq // tk)

    def body(q_ref, ka_ref, vt_ref, cumt_ref, o_ref, lse_ref):
        p = pl.program_id(0)
        i = pl.program_id(1)
        lane = lax.broadcasted_iota(jnp.int32, (1, LANE), 1)
        qq = q_ref[...]
        qx = (_q_aug(qq, lane, 0), _q_aug(qq, lane, 1))
        cqt = cumt_ref[:, pl.ds(pl.multiple_of(i * tq, tq), tq)]
        cq = (_pick_row(cqt, 2 * p), _pick_row(cqt, 2 * p + 1))
        jd = (i * tq) // tk

        def kv_step(j, carry, diag, q_lo=0):
            ks = pl.multiple_of(j * tk, tk)
            nq_ = tq - q_lo
            if diag:
                krow = lax.broadcasted_iota(jnp.int32, (tk, nq_), 0) + j * tk
                qcol = lax.broadcasted_iota(jnp.int32, (tk, nq_), 1) + (i * tq + q_lo)
                causal = krow <= qcol
            out = []
            for hh in range(2):
                m_all, acc_all = carry[2 * hh], carry[2 * hh + 1]
                m, acc, cqh = m_all[:, q_lo:], acc_all[:, q_lo:], cq[hh][:, q_lo:]
                sc = _dot_nt(ka_ref[0, hh, pl.ds(ks, tk), :], qx[hh][q_lo:, :])
                if diag:
                    sc = jnp.where(causal, sc, -jnp.inf)
                mx = jnp.max(sc, axis=0, keepdims=True) + cqh
                mn = jnp.maximum(m, mx)
                alpha = jnp.exp(m - mn)
                pt = jnp.exp(sc + (cqh - mn)).astype(BF16)
                acc = alpha * acc + _dot(vt_ref[0, hh, :, pl.ds(ks, tk)], pt)
                if q_lo:
                    mn = jnp.concatenate([m_all[:, :q_lo], mn], axis=1)
                    acc = jnp.concatenate([acc_all[:, :q_lo], acc], axis=1)
                out += [mn, acc]
            return tuple(out)

        minit = jnp.full((1, tq), -jnp.inf, F32)
        ainit = jnp.zeros((LANE, tq), F32)
        carry = (minit, ainit, minit, ainit)
        for pj in range(npart):
            carry = kv_step(jd + pj, carry, True, q_lo=pj * tk)
        carry = lax.fori_loop(0, jd, lambda j, cr: kv_step(j, cr, False), carry)
        m0, a0, m1, a1 = carry
        l0 = a0[SPARE[0]:SPARE[0] + 1, :]
        l1 = a1[SPARE[1]:SPARE[1] + 1, :]
        row = lax.broadcasted_iota(jnp.int32, (LANE, 1), 0)
        ot = jnp.where(row < HEAD_DIM, a0 / l0, a1 / l1)
        o_ref[...] = ot.T.astype(BF16)
        r8 = lax.broadcasted_iota(jnp.int32, (8, 1), 0)
        lse_ref[0] = jnp.where(r8 == 0, m0 + jnp.log(l0), jnp.where(r8 == 1, m1 + jnp.log(l1), 0.0))

    return pl.pallas_call(
        body, grid=(npair, s // tq), out_shape=(_sds((s, d), BF16), _sds((npair, 8, s), F32)),
        in_specs=[pl.BlockSpec((tq, LANE), lambda p, i: (i, p)),
                  pl.BlockSpec((1, 2, s, LANE), lambda p, i: (p, 0, 0, 0)),
                  pl.BlockSpec((1, 2, LANE, s), lambda p, i: (p, 0, 0, 0)),
                  pl.BlockSpec((N_HEADS, s), lambda p, i: (0, 0))],
        out_specs=(pl.BlockSpec((tq, LANE), lambda p, i: (i, p)), pl.BlockSpec((1, 8, tq), lambda p, i: (p, 0, i))),
        name="fox_attn_fwd", compiler_params=_cp(("parallel", "parallel")))(q, kaug, vtr, cumt)


def _mm_res_final(a, w, gate, x, gain, target):
    s, k = a.shape
    d = w.shape[1]
    ts = _tile(s, 512)

    def body(a_ref, w_ref, gt_ref, x_ref, g_ref, t_ref, lsum_ref, dx_ref, dg_ref):
        @pl.when(pl.program_id(0) == 0)
        def _():
            lsum_ref[...] = jnp.zeros_like(lsum_ref)
            dg_ref[...] = jnp.zeros_like(dg_ref)

        xv = x_ref[...] + gt_ref[...] * _dot(a_ref[...], w_ref[...])
        xhat, rstd = _rms_parts(xv)
        e = xhat * g_ref[...] - t_ref[...]
        lsum_ref[...] += _colsum(e * e)
        dout = e * (1.0 / d)
        dg_ref[...] += _colsum(dout * xhat)
        dxhat = dout * g_ref[...]
        dx_ref[...] = rstd * (dxhat - xhat * jnp.mean(dxhat * xhat, axis=-1, keepdims=True))

    vec = pl.BlockSpec((1, d), lambda i: (0, 0))
    row = pl.BlockSpec((ts, d), lambda i: (i, 0))
    return pl.pallas_call(
        body, grid=(s // ts,), out_shape=(_sds((1, d), F32), _sds((s, d), F32), _sds((1, d), F32)),
        in_specs=[pl.BlockSpec((ts, k), lambda i: (i, 0)), _resident((k, d)), vec, row, vec, row], out_specs=(vec, row, vec),
        name="ffn1_out_final_loss", compiler_params=_cp(("arbitrary",)))(a, w, gate, x, gain, target)


def _ffn_bwd_act(dx, gate, w_out, ug, uu, name, comm=None):
    s, d = dx.shape
    f = w_out.shape[0]
    n = f // 2
    ts = _tile(s, 512)

    def body(dx_ref, g_ref, w_ref, ug_ref, uu_ref, dug_ref, duu_ref):
        dy = (dx_ref[...] * g_ref[...]).astype(BF16)
        for q in range(2):
            cs = pl.ds(q * n, n)
            dact = _dot_nt(dy, w_ref[cs, :])
            g = ug_ref[:, cs].astype(F32)
            u = uu_ref[:, cs].astype(F32)
            sg = _sigmoid(g)
            dug_ref[:, cs] = (dact * u * sg * (1.0 + g * (1.0 - sg))).astype(BF16)
            duu_ref[:, cs] = (dact * g * sg).astype(BF16)

    tile = pl.BlockSpec((ts, f), lambda i: (i, 0))
    return _hosted_call(
        body, comm, grid=(s // ts,), out_shape=(_sds((s, f), BF16), _sds((s, f), BF16)),
        in_specs=[pl.BlockSpec((ts, d), lambda i: (i, 0)), pl.BlockSpec((1, d), lambda i: (0, 0)),
                  _resident(w_out.shape), tile, tile],
        out_specs=(tile, tile), scratch_shapes=[], name=name, sem=("parallel",), args=(dx, gate, w_out, ug, uu))


def _dw_mm(a, b_list, tk, tn, name, gate=None, wfull=None, dgate_init=None):
    s, kdim = a.shape
    nb1 = b_list[0].shape[1] // tn
    nb = nb1 * len(b_list)
    ts = _tile(s, 1024)
    nk = kdim // tk
    ns = s // ts
    gated = gate is not None

    def body(*refs):
        a_ref = refs[0]
        b_refs = refs[1:1 + len(b_list)]
        rest = refs[1 + len(b_list):]
        if gated:
            g_ref, w_ref, di_ref, o_ref, dg_ref, acc = rest
        else:
            o_ref, acc = rest
        jn, ik, st = pl.program_id(0), pl.program_id(1), pl.program_id(2)

        @pl.when(st == 0)
        def _():
            acc[...] = jnp.zeros_like(acc)

        for mi, b_ref in enumerate(b_refs):
            @pl.when(jn // nb1 == mi)
            def _(b_ref=b_ref):
                acc[...] += _dot_tn(a_ref[...], b_ref[...].astype(BF16))

        if gated:
            @pl.when(jnp.logical_and(ik == 0, st == 0))
            def _():
                dg_ref[...] = di_ref[...]

        @pl.when(st == ns - 1)
        def _():
            if gated:
                o_ref[0] = acc[...] * g_ref[...]
                dg_ref[...] += _colsum(acc[...] * w_ref[...].astype(F32))
            else:
                o_ref[0] = acc[...]

    in_specs = [pl.BlockSpec((ts, tk), lambda jn, ik, st: (st, ik))]
    for mi in range(len(b_list)):
        in_specs.append(pl.BlockSpec(
            (ts, tn), lambda jn, ik, st, mi=mi: (st, jnp.clip(jn - mi * nb1, 0, nb1 - 1))))
    args = [a] + list(b_list)
    out_shape = [_sds((nb, kdim, tn), F32)]
    out_specs = [pl.BlockSpec((1, tk, tn), lambda jn, ik, st: (jn, ik, 0))]
    if gated:
        vec = pl.BlockSpec((1, tn), lambda jn, ik, st: (0, jn))
        in_specs += [vec, pl.BlockSpec((tk, tn), lambda jn, ik, st: (ik, jn)), vec]
        args += [gate, wfull, dgate_init]
        out_shape.append(_sds((1, nb * tn), F32))
        out_specs.append(vec)
    res = pl.pallas_call(
        body, grid=(nb, nk, ns), out_shape=tuple(out_shape), in_specs=in_specs, out_specs=tuple(out_specs),
        scratch_shapes=[pltpu.VMEM((tk, tn), F32)],
        name=name, compiler_params=_cp(("parallel", "arbitrary", "arbitrary")))(*args)
    return res if gated else res[0]


def _mm_normbwd(terms, x, dxres, gain, scale, name, ts_pref=256, comm=None, second=None):
    s, d = x.shape
    ts = _tile(s, ts_pref)
    sets = [(terms, gain, scale)] + ([second] if second is not None else [])
    arrs, warrs = [], []
    for tms, _, _ in sets:
        for a, _, w, _ in tms:
            if not any(a is z for z in arrs):
                arrs.append(a)
            if not any(w is z for z in warrs):
                warrs.append(w)
    na, nw, ns_ = len(arrs), len(warrs), len(sets)

    def body(*refs):
        a_refs, w_refs = refs[:na], refs[na:na + nw]
        x_ref, dr_ref = refs[na + nw], refs[na + nw + 1]
        par = refs[na + nw + 2:na + nw + 2 + 2 * ns_]
        dx_ref = refs[na + nw + 2 + 2 * ns_]
        sums = refs[na + nw + 3 + 2 * ns_:]

        @pl.when(pl.program_id(0) == 0)
        def _():
            for r in sums:
                r[...] = jnp.zeros_like(r)

        xhat, rstd = _rms_parts(x_ref[...])
        dxhat = None
        for k, (tms, _, _) in enumerate(sets):
            g_ref, sc_ref = par[2 * k], par[2 * k + 1]
            dsh_ref, dsc_ref, dg_ref = sums[3 * k:3 * k + 3]
            dh = None
            for a, c0, w, q in tms:
                ai = next(i for i, z in enumerate(arrs) if z is a)
                wi = next(i for i, z in enumerate(warrs) if z is w)
                part = _dot_nt(a_refs[ai][:, pl.ds(c0, w.shape[2])], w_refs[wi][q])
                dh = part if dh is None else dh + part
            dsh_ref[...] += _colsum(dh)
            dsc_ref[...] += _colsum(dh * (xhat * g_ref[...]))
            dn = dh * (1.0 + sc_ref[...])
            dg_ref[...] += _colsum(dn * xhat)
            dxh = dn * g_ref[...]
            dxhat = dxh if dxhat is None else dxhat + dxh
        dx_ref[...] = dr_ref[...] + rstd * (dxhat - xhat * jnp.mean(dxhat * xhat, axis=-1, keepdims=True))

    vec = pl.BlockSpec((1, d), lambda i: (0, 0))
    row = pl.BlockSpec((ts, d), lambda i: (i, 0))
    in_specs = [pl.BlockSpec((ts, a.shape[1]), lambda i: (i, 0)) for a in arrs]
    in_specs += [_resident(w.shape) for w in warrs]
    in_specs += [row, row] + [vec] * (2 * ns_)
    par_args = [p_ for _, g_, s_ in sets for p_ in (g_, s_)]
    return _hosted_call(
        body, comm, grid=(s // ts,), out_shape=(_sds((s, d), F32),) + tuple(_sds((1, d), F32) for _ in range(3 * ns_)),
        in_specs=in_specs, out_specs=(row,) + tuple(vec for _ in range(3 * ns_)), scratch_shapes=[],
        name=name, sem=("arbitrary",), args=(*arrs, *warrs, x, dxres, *par_args))


def _do_kernel(dx, gate, wo, o):
    s, d = dx.shape
    ts = _tile(s, 512)

    def body(dx_ref, g_ref, w_ref, o_ref, do_ref, dl_ref):
        dy = (dx_ref[...] * g_ref[...]).astype(BF16)
        do = _dot_nt(dy, w_ref[...])
        do_ref[...] = do.astype(BF16)
        prod = do * o_ref[...].astype(F32)
        hrow = lax.broadcasted_iota(jnp.int32, (N_HEADS, d), 0)
        hcol = lax.broadcasted_iota(jnp.int32, (N_HEADS, d), 1) // HEAD_DIM
        sel = (hrow == hcol).astype(F32)
        dl_ref[...] = lax.dot_general(sel, prod, (((1,), (1,)), ((), ())), preferred_element_type=F32,
                                      precision=lax.Precision.HIGHEST)

    row = pl.BlockSpec((ts, d), lambda i: (i, 0))
    return pl.pallas_call(
        body, grid=(s // ts,), out_shape=(_sds((s, d), BF16), _sds((N_HEADS, s), F32)),
        in_specs=[row, pl.BlockSpec((1, d), lambda i: (0, 0)), _resident(wo.shape), row],
        out_specs=(row, pl.BlockSpec((N_HEADS, ts), lambda i: (0, i))),
        name="attn_do", compiler_params=_cp(("parallel",)))(dx, gate, wo, o)


def _attn_bwd(q, do, kaug, kaugt, v, cumt, lse, deltat, comm=None):
    s, d = q.shape
    tq = _tile(s, ATT_TQ)
    tk = _tile(s, ATT_TK)
    assert tq in (tk, 2 * tk)
    npair = d // LANE
    nq = s // tq
    nkb = s // tk
    qscale = HEAD_DIM ** -0.5

    def body(q_ref, do_ref, ka_ref, kt_ref, v_ref, cumt_ref, lse_ref, dl_ref,
             dq_ref, dk_ref, dv_ref, dcq_ref, dck_ref, qaug, dom, rowv, dqt):
        p = pl.program_id(0)
        j = pl.program_id(1)
        lane = lax.broadcasted_iota(jnp.int32, (1, LANE), 1)
        lo = lane < HEAD_DIM
        r8 = lax.broadcasted_iota(jnp.int32, (8, 1), 0)

        @pl.when(j == 0)
        def _():
            dqt[...] = jnp.zeros_like(dqt)
            for c in range(nq):
                rows = pl.ds(c * tq, tq)
                qq = q_ref[rows, :]
                dd = do_ref[rows, :]
                cqt = cumt_ref[:, rows]
                dlt = dl_ref[:, rows]
                lst = lse_ref[0, :, rows]
                for hh in range(2):
                    qaug[hh, rows, :] = _q_aug(qq, lane, hh)
                    dom[hh, rows, :] = jnp.where(_head_mask(lane, hh), dd, jnp.zeros_like(dd))
                    rowv[hh, :, rows] = jnp.where(
                        r8 == 0, _pick_row(cqt, 2 * p + hh) - lst[hh:hh + 1, :],
                        jnp.where(r8 == 1, _pick_row(dlt, 2 * p + hh), 0.0))

        vv = v_ref[...]
        i0 = (j * tk) // tq

        def q_step(qs, nq_, carry, diag):
            dv_acc, dk0, dk1 = carry
            qs = pl.multiple_of(qs, tk)
            if diag:
                krow = lax.broadcasted_iota(jnp.int32, (tk, nq_), 0) + j * tk
                qcol = lax.broadcasted_iota(jnp.int32, (tk, nq_), 1) + qs
                causal = krow <= qcol
            dks = [dk0, dk1]
            for hh in range(2):
                rv = rowv[hh, :, pl.ds(qs, nq_)]
                qa = qaug[hh, pl.ds(qs, nq_), :]
                dh = dom[hh, pl.ds(qs, nq_), :]
                sc = _dot_nt(ka_ref[0, hh], qa)
                if diag:
                    sc = jnp.where(causal, sc, -jnp.inf)
                pt = jnp.exp(sc + rv[0:1, :])
                dpt = _dot_nt(vv, dh)
                dst = (pt * (dpt - rv[1:2, :])).astype(BF16)
                dv_acc = dv_acc + _dot(pt.astype(BF16), dh)
                dks[hh] = dks[hh] + _dot(dst, qa)
                dqt[hh, :, pl.ds(qs, nq_)] += _dot(kt_ref[0, hh], dst)
            return dv_acc, dks[0], dks[1]

        z = jnp.zeros((tk, LANE), F32)
        first = ((j * tk) % tq == 0).astype(jnp.int32)
        carry = lax.fori_loop(0, first, lambda _, cr: q_step(i0 * tq, tq, cr, True), (z, z, z))
        if tq > tk:
            carry = lax.fori_loop(0, 1 - first, lambda _, cr: q_step(j * tk, tq - tk, cr, True), carry)
        dv_acc, dk0, dk1 = lax.fori_loop(i0 + 1, nq, lambda i, cr: q_step(i * tq, tq, cr, False), carry)
        dv_ref[...] = dv_acc.astype(BF16)
        dk_ref[...] = jnp.where(lo, dk0, dk1).astype(BF16)
        dck_ref[0] = jnp.where(r8 == 0, dk0.T[SPARE[0]:SPARE[0] + 1, :],
                               jnp.where(r8 == 1, dk1.T[SPARE[1]:SPARE[1] + 1, :], 0.0))

        @pl.when(j == nkb - 1)
        def _():
            for c in range(nq):
                rows = pl.ds(c * tq, tq)
                a0 = dqt[0, :, rows].T
                a1 = dqt[1, :, rows].T
                dq_ref[rows, :] = (jnp.where(lo, a0, a1) * qscale).astype(BF16)
            r0, r1 = SPARE[0] + NPIECE, SPARE[1] + NPIECE
            dcq_ref[0] = jnp.where(r8 == 0, dqt[0, r0:r0 + 1, :], jnp.where(r8 == 1, dqt[1, r1:r1 + 1, :], 0.0))

    col = pl.BlockSpec((s, LANE), lambda p, j: (0, p), pipeline_mode=pl.Buffered(1))
    rows16 = pl.BlockSpec((N_HEADS, s), lambda p, j: (0, 0), pipeline_mode=pl.Buffered(1))
    blk = pl.BlockSpec((tk, LANE), lambda p, j: (j, p))
    return _hosted_call(
        body, comm, grid=(npair, nkb),
        out_shape=(_sds((s, d), BF16), _sds((s, d), BF16), _sds((s, d), BF16), _sds((npair, 8, s), F32), _sds((npair, 8, s), F32)),
        in_specs=[col, col, pl.BlockSpec((1, 2, tk, LANE), lambda p, j: (p, 0, j, 0)),
                  pl.BlockSpec((1, 2, LANE, tk), lambda p, j: (p, 0, 0, j)), blk, rows16,
                  pl.BlockSpec((1, 8, s), lambda p, j: (p, 0, 0), pipeline_mode=pl.Buffered(1)), rows16],
        out_specs=(pl.BlockSpec((s, LANE), lambda p, j: (0, p)), blk, blk,
                   pl.BlockSpec((1, 8, s), lambda p, j: (p, 0, 0)), pl.BlockSpec((1, 8, tk), lambda p, j: (p, 0, j))),
        scratch_shapes=[pltpu.VMEM((2, s, LANE), BF16), pltpu.VMEM((2, s, LANE), BF16), pltpu.VMEM((2, 8, s), F32),
                        pltpu.VMEM((2, LANE, s), F32)],
        name="fox_attn_bwd", sem=("arbitrary", "arbitrary"), vmem_mb=ATT_BWD_VMEM_MB,
        args=(q, do, kaug, kaugt, v, cumt, lse, deltat))


def _cumsum_bwd(dcq, dck, flog, fb):
    s = flog.shape[0]
    ts = _tile(s, 256)
    nt = s // ts

    def body(dq_ref, dk_ref, f_ref, b_ref, df_ref, db_ref, carry):
        @pl.when(pl.program_id(0) == 0)
        def _():
            carry[...] = jnp.zeros_like(carry)
            db_ref[...] = jnp.zeros_like(db_ref)

        r = lax.broadcasted_iota(jnp.int32, (ts, ts), 0)
        cidx = lax.broadcasted_iota(jnp.int32, (ts, ts), 1)
        tri = (r >= cidx).astype(F32)
        dct = dq_ref[...] + dk_ref[...]
        dlst = jnp.dot(dct, tri, preferred_element_type=F32, precision=lax.Precision.HIGHEST) + carry[...]
        carry[...] = dlst[:, 0:1]
        dls = jnp.concatenate([dlst, jnp.zeros((LANE - N_HEADS, ts), F32)], axis=0).T
        z = f_ref[...] + b_ref[...]
        df = dls * (1.0 / (1.0 + jnp.exp(z)))
        db_ref[...] += _colsum(df)
        df_ref[...] = df.astype(BF16)

    rev = pl.BlockSpec((ts, LANE), lambda i: (nt - 1 - i, 0))
    revt = pl.BlockSpec((N_HEADS, ts), lambda i: (0, nt - 1 - i))
    vec = pl.BlockSpec((1, LANE), lambda i: (0, 0))
    return pl.pallas_call(
        body, grid=(nt,), out_shape=(_sds((s, LANE), BF16), _sds((1, LANE), F32)),
        in_specs=[revt, revt, rev, vec], out_specs=(rev, vec), scratch_shapes=[pltpu.VMEM((N_HEADS, 1), F32)],
        name="forget_cumsum_bwd", compiler_params=_cp(("arbitrary",)))(dcq, dck, flog, fb)


def _conv_bwd1(dx, gate, w_out, b_out, dwo, lng, lnb):
    s, d = dx.shape
    ts = _tile(s, 512)
    ns = s // ts

    def body(dx_ref, g_ref, w_ref, bo_ref, y_ref, lg_ref, lb_ref, dd_ref, dlg_ref, dlb_ref, dbd_ref, dbo_ref, dge_ref, cs):
        i = pl.program_id(0)

        @pl.when(i == 0)
        def _():
            for r in (dlg_ref, dlb_ref, dbd_ref, cs):
                r[...] = jnp.zeros_like(r)

        dxv = dx_ref[...]
        cs[...] += _colsum(dxv)
        dsw = _dot_nt((dxv * g_ref[...]).astype(BF16), w_ref[...])
        yv = y_ref[...]
        mu = jnp.mean(yv, axis=-1, keepdims=True)
        yc = yv - mu
        rstd = lax.rsqrt(jnp.mean(yc * yc, axis=-1, keepdims=True) + EPS)
        xhat = yc * rstd
        ln = xhat * lg_ref[...] + lb_ref[...]
        sg = _sigmoid(ln)
        dln = dsw * (sg * (1.0 + ln * (1.0 - sg)))
        dlg_ref[...] += _colsum(dln * xhat)
        dlb_ref[...] += _colsum(dln)
        dxh = dln * lg_ref[...]
        dd = rstd * (dxh - jnp.mean(dxh, axis=-1, keepdims=True) - xhat * jnp.mean(dxh * xhat, axis=-1, keepdims=True))
        dbd_ref[...] += _colsum(dd)
        dd_ref[...] = dd

        @pl.when(i == ns - 1)
        def _():
            dbo_ref[...] = g_ref[...] * cs[...]
            dge_ref[...] = bo_ref[...] * cs[...]

    vec = pl.BlockSpec((1, d), lambda i: (0, 0))
    row = pl.BlockSpec((ts, d), lambda i: (i, 0))
    return pl.pallas_call(
        body, grid=(ns,), out_shape=(_sds((s, d), F32),) + tuple(_sds((1, d), F32) for _ in range(5)),
        in_specs=[row, vec, _resident(w_out.shape), vec, row, vec, vec], out_specs=(row, vec, vec, vec, vec, vec),
        scratch_shapes=[pltpu.VMEM((1, d), F32)],
        name="conv_bwd_ln", compiler_params=_cp(("arbitrary",)))(dx, gate, w_out, b_out, dwo, lng, lnb)


def _dwconv_bwd(ddwo, glu, a_s, g_s, wdw, comm=None):
    s, d = ddwo.shape
    ts = _tile(s, 256)
    ns = s // ts
    rb, cb = 32, 256
    nrb = ts // rb

    def body(dd_ref, ddn_ref, gl_ref, glh_ref, a_ref, g_ref, w_ref, da_ref, dg_ref, dw_ref, sa_ref, sg_ref, bufd, bufg, dws,
             shd, shg):
        i = pl.program_id(0)

        @pl.when(i == 0)
        def _():
            dws[...] = jnp.zeros_like(dws)
            sa_ref[...] = jnp.zeros_like(sa_ref)
            sg_ref[...] = jnp.zeros_like(sg_ref)
            bufg[pl.ds(0, HALO), :] = jnp.zeros((HALO, d), F32)

        @pl.when(i > 0)
        def _():
            bufg[pl.ds(0, HALO), :] = glh_ref[...]

        bufg[pl.ds(HALO, ts), :] = gl_ref[...]
        bufd[pl.ds(0, ts), :] = dd_ref[...]

        @pl.when(i == ns - 1)
        def _():
            bufd[pl.ds(ts, HALO), :] = jnp.zeros((HALO, d), F32)

        @pl.when(i < ns - 1)
        def _():
            bufd[pl.ds(ts, HALO), :] = ddn_ref[...]

        _shift_copies(bufd, shd)
        _shift_copies(bufg, shg)
        for cc in range(d // cb):
            cs = pl.ds(cc * cb, cb)
            for r in range(nrb):
                acc = jnp.zeros((rb, cb), F32)
                for k in range(CONV_K):
                    acc = acc + w_ref[pl.ds(k, 1), cs] * _shifted(bufd, shd, r * rb + (CONV_K - 1) - k, rb, cs)
                rows = pl.ds(r * rb, rb)
                av = a_ref[rows, cs].astype(F32)
                sg = _sigmoid(g_ref[rows, cs].astype(F32))
                dav = acc * sg
                dgv = acc * av * sg * (1.0 - sg)
                da_ref[rows, cs] = dav.astype(BF16)
                dg_ref[rows, cs] = dgv.astype(BF16)
                sa_ref[:, cs] += _colsum(dav)
                sg_ref[:, cs] += _colsum(dgv)
            for k in range(CONV_K):
                acc8 = jnp.zeros((8, cb), F32)
                for r in range(nrb):
                    prod = bufd[pl.ds(r * rb, rb), cs] * _shifted(bufg, shg, HALO - (CONV_K - 1) + k + r * rb, rb, cs)
                    acc8 = acc8 + (prod[0:8] + prod[8:16]) + (prod[16:24] + prod[24:32])
                dws[pl.ds(8 * k, 8), cs] += acc8

        @pl.when(i == ns - 1)
        def _():
            dw_ref[...] = jnp.zeros_like(dw_ref)
            for k in range(CONV_K):
                dw_ref[pl.ds(k, 1), :] = _colsum(dws[pl.ds(8 * k, 8), :])

    row = pl.BlockSpec((ts, d), lambda i: (i, 0))
    vec = pl.BlockSpec((1, d), lambda i: (0, 0))
    hb = ts // HALO
    return _hosted_call(
        body, comm, grid=(ns,),
        out_shape=(_sds((s, d), BF16), _sds((s, d), BF16), _sds((HALO, d), F32), _sds((1, d), F32), _sds((1, d), F32)),
        in_specs=[row, pl.BlockSpec((HALO, d), lambda i: (jnp.minimum((i + 1) * hb, ns * hb - 1), 0)),
                  row, pl.BlockSpec((HALO, d), lambda i: (jnp.maximum(i * hb - 1, 0), 0)),
                  row, row, pl.BlockSpec((HALO, d), lambda i: (0, 0))],
        out_specs=(row, row, pl.BlockSpec((HALO, d), lambda i: (0, 0)), vec, vec),
        scratch_shapes=[pltpu.VMEM((ts + HALO, d), F32), pltpu.VMEM((HALO + ts, d), F32), pltpu.VMEM((8 * HALO, d), F32),
                        pltpu.VMEM((SUBLANES - 1, HALO + ts - SUBLANES, d), F32),
                        pltpu.VMEM((SUBLANES - 1, HALO + ts - SUBLANES, d), F32)],
        name="dwconv_bwd", sem=("arbitrary",), args=(ddwo, ddwo, glu, glu, a_s, g_s, wdw))


def _ada_wgrad(cat, da, name):
    nl, _, n = da.shape
    d = cat.shape[0]
    tn = 256

    def body(c_ref, d_ref, o_ref):
        acc = c_ref[:, 0:1] * d_ref[0, 0:1, :]
        for r in range(1, 8):
            acc = acc + c_ref[:, r:r + 1] * d_ref[0, r:r + 1, :]
        o_ref[0] = acc

    return pl.pallas_call(
        body, grid=(nl, n // tn), out_shape=_sds((nl, d, n), F32),
        in_specs=[pl.BlockSpec((d, 8), lambda l, j: (0, 0)), pl.BlockSpec((1, 8, tn), lambda l, j: (l, 0, j))],
        out_specs=pl.BlockSpec((1, d, tn), lambda l, j: (l, 0, j)),
        name=name, compiler_params=_cp(("parallel", "parallel")))(cat, da)


def _silu_rows(c_all):
    def body(c_ref, o_ref):
        cc = c_ref[...]
        o_ref[...] = cc * _sigmoid(cc)

    return pl.pallas_call(body, out_shape=_sds(c_all.shape, F32), name="silu_c")(c_all)


def _adamw(w, g, m, v, name):
    r, c = w.shape
    tr = r
    for cand in (512, 256, 128, 64, 32, 16, 8):
        if r % cand == 0 and cand * c * 4 <= (1 << 20):
            tr = cand
            break
    bc1 = 1.0 - ADAM_B1 ** ADAM_STEP
    bc2 = 1.0 - ADAM_B2 ** ADAM_STEP

    def body(w_ref, g_ref, m_ref, v_ref, d_ref, nm_ref, nv_ref):
        gv = g_ref[...]
        mn = ADAM_B1 * m_ref[...] + (1.0 - ADAM_B1) * gv
        vn = ADAM_B2 * v_ref[...] + (1.0 - ADAM_B2) * (gv * gv)
        mh = mn / bc1
        vh = vn / bc2
        d_ref[...] = -ADAM_LR * (mh / (jnp.sqrt(vh) + ADAM_EPS) + ADAM_WD * w_ref[...])
        nm_ref[...] = mn
        nv_ref[...] = vn

    blk = pl.BlockSpec((tr, c), lambda i: (i, 0))
    return pl.pallas_call(
        body, grid=(r // tr,), out_shape=tuple(_sds((r, c), F32) for _ in range(3)),
        in_specs=[blk, blk, blk, blk], out_specs=(blk, blk, blk),
        name=name, compiler_params=_cp(("parallel",)))(w, g, m, v)


def _pad_rows(a, rows, axis):
    pad = [(0, 0)] * a.ndim
    pad[axis] = (0, rows - a.shape[axis])
    return jnp.pad(a, pad)


def _vec(a):
    return a.reshape(1, -1)


def kernel(x, c, mix_norm_g, mix_ada_w, mix_ada_b, ffn_norm_g, ffn_ada_w, ffn_ada_b, ffn_w_in, ffn_w_out, conv_w_in, conv_b_in, conv_w_dw, conv_b_dw, conv_ln_g, conv_ln_b, conv_w_out, conv_b_out, kv_norm_g, kv_ada_w, kv_ada_b, kv_w, forget_b, attn_w_q, attn_w_o, final_norm_g, loss_target, m_mix_norm_g, m_mix_ada_w, m_mix_ada_b, m_ffn_norm_g, m_ffn_ada_w, m_ffn_ada_b, m_ffn_w_in, m_ffn_w_out, m_conv_w_in, m_conv_b_in, m_conv_w_dw, m_conv_b_dw, m_conv_ln_g, m_conv_ln_b, m_conv_w_out, m_conv_b_out, m_kv_norm_g, m_kv_ada_w, m_kv_ada_b, m_kv_w, m_forget_b, m_attn_w_q, m_attn_w_o, m_final_norm_g, v_mix_norm_g, v_mix_ada_w, v_mix_ada_b, v_ffn_norm_g, v_ffn_ada_w, v_ffn_ada_b, v_ffn_w_in, v_ffn_w_out, v_conv_w_in, v_conv_b_in, v_conv_w_dw, v_conv_b_dw, v_conv_ln_g, v_conv_ln_b, v_conv_w_out, v_conv_b_out, v_kv_norm_g, v_kv_ada_w, v_kv_ada_b, v_kv_w, v_forget_b, v_attn_w_q, v_attn_w_o, v_final_norm_g):
    xi, yi, ci = lax.axis_index("x"), lax.axis_index("y"), lax.axis_index("c")
    chip = 2 * xi + yi
    dev = 4 * xi + 2 * yi + ci
    s, d = x.shape[1], x.shape[2]
    f = ffn_w_out.shape[1] * 4
    x0 = x[0]
    nkv = kv_w.shape[1]
    nkv_all = 4 * nkv

    wdw_loc = _pad_rows(conv_w_dw[0], HALO, 0)
    small = jnp.concatenate([c.reshape(-1), conv_b_in.reshape(-1), wdw_loc.reshape(-1), conv_b_dw.reshape(-1),
                             conv_ln_g.reshape(-1), conv_ln_b.reshape(-1), conv_b_out.reshape(-1)])
    n_small = small.shape[0]
    w_small = -(-n_small // (8 * LANE)) * LANE
    small = jnp.pad(small, (0, 8 * w_small - n_small)).reshape(8, w_small)
    small_all = _allgather8(small, "ag_small_params", True).reshape(8, 8 * w_small)
    c_all = small_all[:, :d]
    per_chip = small_all[0::2]
    dq_ = d // 4
    o1 = d
    b_in_full = per_chip[:, o1:o1 + 2 * dq_].reshape(4, 1, 2 * dq_)
    o1 += 2 * dq_
    wdw_full = per_chip[:, o1:o1 + HALO * dq_].reshape(4, HALO, dq_).transpose(1, 0, 2).reshape(HALO, d)
    o1 += HALO * dq_
    bdw_full = per_chip[:, o1:o1 + dq_].reshape(1, d)
    lng_full = per_chip[:, o1 + dq_:o1 + 2 * dq_].reshape(1, d)
    lnb_full = per_chip[:, o1 + 2 * dq_:o1 + 3 * dq_].reshape(1, d)
    bout_full = per_chip[:, o1 + 3 * dq_:o1 + 4 * dq_].reshape(1, d)

    a_mix = _ada_fwd(c_all, mix_ada_w, "ada_mix")
    a_ffn = _ada_fwd(c_all, ffn_ada_w, "ada_ffn")
    a_kv = _ada_fwd(c_all, kv_ada_w[None], "ada_kv")
    n3 = mix_ada_w.shape[2]
    n2 = kv_ada_w.shape[1]
    ada_loc = jnp.concatenate([a_mix[0], a_mix[1], a_ffn[0], a_ffn[1], a_kv[0]], axis=1)
    w_ada = ada_loc.shape[1]
    ada_all = _allgather8(ada_loc, "ag_ada", True).reshape(8, 8, w_ada)
    ada_me = lax.dynamic_index_in_dim(ada_all, dev, axis=1, keepdims=False)[0::2]

    def ada_vec(off, n, bias):
        return ada_me[:, off:off + n].reshape(1, 4 * n) + bias.reshape(1, -1)

    ada_m0 = ada_vec(0, n3, mix_ada_b[0])
    ada_m1 = ada_vec(n3, n3, mix_ada_b[1])
    ada_f0 = ada_vec(2 * n3, n3, ffn_ada_b[0])
    ada_f1 = ada_vec(3 * n3, n3, ffn_ada_b[1])
    ada_k = ada_vec(4 * n3, n2, kv_ada_b)

    def split3(a):
        return a[:, :d], a[:, d:2 * d], a[:, 2 * d:3 * d]

    sh_m0, sc_m0, gt_m0 = split3(ada_m0)
    sh_m1, sc_m1, gt_m1 = split3(ada_m1)
    sh_f0, sc_f0, gt_f0 = split3(ada_f0)
    sh_f1, sc_f1, gt_f1 = split3(ada_f1)
    sh_k, sc_k = ada_k[:, :d], ada_k[:, d:2 * d]

    def my_halves(ws):
        return [lax.dynamic_index_in_dim(w.astype(BF16).reshape(2, w.shape[0] // 2, w.shape[1]), ci, axis=0, keepdims=False)
                for w in ws]

    def whole(gath, ws):
        return [g.reshape(4, w.shape[0], w.shape[1]) for g, w in zip(gath, ws)]

    grp_conv = [conv_w_in[0]]
    grp_ffn0 = [ffn_w_in[0], ffn_w_out[0], conv_w_out[0]]
    grp_rest = [ffn_w_in[1], ffn_w_out[1], kv_w, attn_w_q[0], attn_w_o[0]]
    mine_conv, mine_ffn0, mine_rest = my_halves(grp_conv), my_halves(grp_ffn0), my_halves(grp_rest)
    cw_in, = whole(_exchange(_Gather8(mine_conv), mine_conv, "ag_w_conv"), grp_conv)

    zero_b = jnp.zeros((1, d), F32)
    g_m0, g_m1 = _vec(mix_norm_g[0]), _vec(mix_norm_g[1])
    g_f0, g_f1 = _vec(ffn_norm_g[0]), _vec(ffn_norm_g[1])
    g_k, g_fin = _vec(kv_norm_g), _vec(final_norm_g)
    fb = jnp.pad(forget_b, (0, LANE - N_HEADS)).reshape(1, LANE)

    h0, glu, a_s, g_s, gath_ffn0 = _in_pair(x0, g_m0, sh_m0, sc_m0, cw_in, b_in_full, True, "conv_in",
                                            comm=(_Gather8(mine_ffn0), mine_ffn0))
    dwo, sw, gath_rest = _dwconv_fwd(glu, wdw_full, bdw_full, lng_full, lnb_full, comm=(_Gather8(mine_rest), mine_rest))
    w_in0, w_out0, cw_out = whole(gath_ffn0, grp_ffn0)
    cw_out = cw_out.reshape(d, d)
    w_in1, w_out1, kvw, wq, wo = whole(gath_rest, grp_rest)
    w_in = [w_in0, w_in1]
    w_out = [w_out0.reshape(f, d), w_out1.reshape(f, d)]
    kvw = kvw.transpose(1, 0, 2).reshape(d, nkv_all)
    wk, wv = kvw[:, :d], kvw[:, d:2 * d]
    wf = jnp.pad(kvw[:, 2 * d:], ((0, 0), (0, LANE - N_HEADS)))
    wq, wo = wq.reshape(d, d), wo.reshape(d, d)
    x1 = _mm_res(sw, cw_out, bout_full, gt_m0, x0, "conv_out")
    hf0, act0, ug0, uu0 = _in_pair(x1, g_f0, sh_f0, sc_f0, w_in[0], None, False, "ffn0_in")
    x2 = _mm_res(act0, w_out[0], zero_b, gt_f0, x1, "ffn0_out")
    hk, h1, kk, vv, qq, flog = _qkv(x2, (g_k, sh_k, sc_k), (g_m1, sh_m1, sc_m1), wk, wv, wf, wq)
    cum, cumt = _cumsum_fwd(flog, fb)
    kaug, kaugt, vtr = _attn_prep(kk, vv, cum)
    o, lse = _attn_fwd(qq, kaug, vtr, cumt)
    x3 = _mm_res(o, wo, zero_b, gt_m1, x2, "attn_out")
    hf1, act1, ug1, uu1 = _in_pair(x3, g_f1, sh_f1, sc_f1, w_in[1], None, False, "ffn1_in")
    lsum, dx4, d_gfin = _mm_res_final(act1, w_out[1], gt_f1, x3, g_fin, loss_target[0])
    loss = lax.psum(0.5 / d * jnp.sum(lsum), ("x", "y", "c"))

    nf = f // 2

    sel = jnp.stack([ci, chip]).astype(jnp.int32)

    def reduce_begin(gs, tag):
        ps = [g.reshape(4, 2, g.shape[1] // 2, g.shape[2]) for g in gs]
        lands = _exchange(_SwapHalves(ps), ps, tag + "_swap")
        pairs = [_add_halves(p_, l_, sel, f"{tag}_add{k}") for k, (p_, l_) in enumerate(zip(ps, lands))]
        return [q for q, _ in pairs], [o_ for _, o_ in pairs]

    def reduce_sum(owns, lands, tag):
        return [_add_chips(o_, l_, sel, f"{tag}_sum{k}") for k, (o_, l_) in enumerate(zip(owns, lands))]

    def ffn_bwd(dx_out, x_in, hf, act, ug, uu, gain, scale, gate, w_in_l, w_out_l, tag, comm=None):
        res = _ffn_bwd_act(dx_out, gate, w_out_l, ug, uu, tag + "_bwd_act", comm=comm)
        dug, duu = res[0], res[1]
        dw_out, dgate = _dw_mm(act, [dx_out], nf, d, tag + "_dw_out", gate=gate, wfull=w_out_l, dgate_init=zero_b)
        terms = [(dug, 0, w_in_l, 0), (dug, nf, w_in_l, 1), (duu, 0, w_in_l, 2), (duu, nf, w_in_l, 3)]
        dx_in, dsh, dsc, dgn = _mm_normbwd(terms, x_in, dx_out, gain, scale, tag + "_bwd_in")
        dw_in = _dw_mm(hf, [dug, duu], d, nf, tag + "_dw_in")
        return dx_in, dw_in, dw_out[0], dsh, dsc, dgate, dgn, (res[2] if comm is not None else None)

    dx3, dw_in1, dw_out1, dsh_f1, dsc_f1, dgt_f1, dgn_f1, _ = ffn_bwd(dx4, x3, hf1, act1, ug1, uu1, g_f1, sc_f1, gt_f1, w_in[1], w_out[1], "ffn1")
    q16_1, own_1 = reduce_begin([dw_in1, dw_out1.reshape(4, f // 4, d)], "rs_ffn1")

    do, deltat = _do_kernel(dx3, gt_m1, wo, o)
    dwo_att, dgt_m1 = _dw_mm(o, [dx3], d, d, "attn_dw_o", gate=gt_m1, wfull=wo, dgate_init=zero_b)
    dq, dk, dv, dcq, dck, land_1 = _attn_bwd(qq, do, kaug, kaugt, vv, cumt, lse, deltat, comm=(_ScatterChips(q16_1), q16_1))
    dwq = _dw_mm(h1, [dq], d, d, "attn_dw_q")[0]

    df, dfb = _cumsum_bwd(dcq[:, :2].reshape(N_HEADS, s), dck[:, :2].reshape(N_HEADS, s), flog, fb)
    terms = [(dk, 0, wk.reshape(1, d, d), 0), (dv, 0, wv.reshape(1, d, d), 0), (df, 0, wf.reshape(1, d, LANE), 0)]
    dx2, dsh_m1, dsc_m1, dgn_m1, dsh_k, dsc_k, dgn_k = _mm_normbwd(
        [(dq, 0, wq.reshape(1, d, d), 0)], x2, dx3, g_m1, sc_m1, "attn_kv_bwd", second=(terms, g_k, sc_k))
    dwk, dwv = _dw_mm(hk, [dk, dv], d, d, "kv_dw_kv")
    dwf = _dw_mm(hk, [df], d, LANE, "kv_dw_f")[0]
    dkvw = jnp.concatenate([dwk, dwv, dwf[:, :N_HEADS]], axis=1)
    dkvw = dkvw.reshape(d, 4, nkv).transpose(1, 0, 2)

    q16_2, own_2 = reduce_begin([dkvw, dwq.reshape(4, d // 4, d), dwo_att[0].reshape(4, d // 4, d)], "rs_attn")
    dx1, dw_in0, dw_out0, dsh_f0, dsc_f0, dgt_f0, dgn_f0, land_2 = ffn_bwd(
        dx2, x1, hf0, act0, ug0, uu0, g_f0, sc_f0, gt_f0, w_in[0], w_out[0], "ffn0", comm=(_ScatterChips(q16_2), q16_2))

    ddwo, d_lng, d_lnb, d_bdw, d_bout, dgt_extra = _conv_bwd1(dx1, gt_m0, cw_out, bout_full, dwo, lng_full, lnb_full)
    dcw_out, dgt_m0 = _dw_mm(sw, [dx1], d, d, "conv_dw_out", gate=gt_m0, wfull=cw_out, dgate_init=dgt_extra)
    q16_3, own_3 = reduce_begin([dw_in0, dw_out0.reshape(4, f // 4, d), dcw_out[0].reshape(4, d // 4, d)], "rs_ffn0")
    da, dg, d_wdw, d_bin_a, d_bin_g, land_3 = _dwconv_bwd(ddwo, glu, a_s, g_s, wdw_full, comm=(_ScatterChips(q16_3), q16_3))
    nc = cw_in.shape[2]
    terms = [(da, 0, cw_in, 0), (da, nc, cw_in, 1), (dg, 0, cw_in, 2), (dg, nc, cw_in, 3)]
    dx0, dsh_m0, dsc_m0, dgn_m0 = _mm_normbwd(terms, x0, dx1, g_m0, sc_m0, "conv_bwd_in")
    dcw_in = _dw_mm(h0, [da, dg], d, nc, "conv_dw_in")
    q16_4, own_4 = reduce_begin([dcw_in], "rs_conv")
    land_4 = _exchange(_ScatterChips(q16_4), q16_4, "rs_conv_scatter")

    sums = (reduce_sum(own_1, land_1, "rs_ffn1") + reduce_sum(own_2, land_2, "rs_attn")
            + reduce_sum(own_3, land_3, "rs_ffn0") + reduce_sum(own_4, land_4, "rs_conv"))
    reduced = [b.reshape(2 * b.shape[1], b.shape[2]) for b in _share_halves(sums)]
    g_w_in1, g_w_out1, g_kvw, g_wq, g_wo, g_w_in0, g_w_out0, g_cw_out, g_cw_in = reduced

    d_ada = [jnp.concatenate([dsh_m0, dsc_m0, dgt_m0], axis=1), jnp.concatenate([dsh_m1, dsc_m1, dgt_m1], axis=1),
             jnp.concatenate([dsh_f0, dsc_f0, dgt_f0], axis=1), jnp.concatenate([dsh_f1, dsc_f1, dgt_f1], axis=1),
             jnp.concatenate([dsh_k, dsc_k], axis=1)]
    fields = d_ada + [dgn_m0, dgn_m1, dgn_f0, dgn_f1, dgn_k, d_gfin, d_bin_a, d_bin_g, d_bdw, d_lng, d_lnb, d_bout,
                      d_wdw.reshape(1, -1), dfb]
    foffs = [0]
    for fl in fields:
        foffs.append(foffs[-1] + fl.shape[1])
    n_row = foffs[-1]
    w_row = -(-n_row // (8 * LANE)) * LANE
    row = jnp.pad(jnp.concatenate(fields, axis=1), ((0, 0), (0, 8 * w_row - n_row))).reshape(8, w_row)
    rows_all = _allgather8(row, "ag_small_grads", True).reshape(8, 8, w_row)
    rsum_small = _sum8(rows_all).reshape(1, 8 * w_row)
    rows_flat = rows_all.reshape(8, 8 * w_row)

    def fsum(i):
        return rsum_small[:, foffs[i]:foffs[i + 1]]

    cat = _silu_rows(c_all).T

    def ada_cols(i, n):
        full = rows_flat[:, foffs[i]:foffs[i + 1]].reshape(8, 4, n)
        return lax.dynamic_index_in_dim(full, chip, axis=1, keepdims=False)

    g_mix_ada_w = _ada_wgrad(cat, jnp.stack([ada_cols(0, n3), ada_cols(1, n3)]), "ada_mix_wgrad")
    g_ffn_ada_w = _ada_wgrad(cat, jnp.stack([ada_cols(2, n3), ada_cols(3, n3)]), "ada_ffn_wgrad")
    g_kv_ada_w = _ada_wgrad(cat, ada_cols(4, n2)[None], "ada_kv_wgrad")[0]

    def my_cols(v, n):
        return lax.dynamic_index_in_dim(v.reshape(4, n), chip, axis=0, keepdims=False)

    grads = {
        "mix_norm_g": jnp.concatenate([fsum(5), fsum(6)], axis=0),
        "mix_ada_w": g_mix_ada_w,
        "mix_ada_b": jnp.concatenate([fsum(0), fsum(1)], axis=0),
        "ffn_norm_g": jnp.concatenate([fsum(7), fsum(8)], axis=0),
        "ffn_ada_w": g_ffn_ada_w,
        "ffn_ada_b": jnp.concatenate([fsum(2), fsum(3)], axis=0),
        "ffn_w_in": jnp.stack([g_w_in0, g_w_in1]),
        "ffn_w_out": jnp.stack([g_w_out0, g_w_out1]),
        "conv_w_in": g_cw_in[None],
        "conv_b_in": my_cols(jnp.concatenate([fsum(11), fsum(12)], axis=1), 2 * dq_)[None],
        "conv_w_dw": lax.dynamic_index_in_dim(fsum(17).reshape(HALO, 4, dq_), chip, axis=1, keepdims=False)[:CONV_K][None],
        "conv_b_dw": my_cols(fsum(13), dq_)[None],
        "conv_ln_g": my_cols(fsum(14), dq_)[None],
        "conv_ln_b": my_cols(fsum(15), dq_)[None],
        "conv_w_out": g_cw_out[None],
        "conv_b_out": my_cols(fsum(16), dq_)[None],
        "kv_norm_g": fsum(9).reshape(-1),
        "kv_ada_w": g_kv_ada_w,
        "kv_ada_b": fsum(4).reshape(-1),
        "kv_w": g_kvw,
        "forget_b": fsum(18).reshape(-1)[:N_HEADS],
        "attn_w_q": g_wq[None],
        "attn_w_o": g_wo[None],
        "final_norm_g": fsum(10).reshape(-1),
    }
    weights = dict(mix_norm_g=mix_norm_g, mix_ada_w=mix_ada_w, mix_ada_b=mix_ada_b, ffn_norm_g=ffn_norm_g, ffn_ada_w=ffn_ada_w, ffn_ada_b=ffn_ada_b, ffn_w_in=ffn_w_in, ffn_w_out=ffn_w_out, conv_w_in=conv_w_in, conv_b_in=conv_b_in, conv_w_dw=conv_w_dw, conv_b_dw=conv_b_dw, conv_ln_g=conv_ln_g, conv_ln_b=conv_ln_b, conv_w_out=conv_w_out, conv_b_out=conv_b_out, kv_norm_g=kv_norm_g, kv_ada_w=kv_ada_w, kv_ada_b=kv_ada_b, kv_w=kv_w, forget_b=forget_b, attn_w_q=attn_w_q, attn_w_o=attn_w_o, final_norm_g=final_norm_g)
    moms = dict(mix_norm_g=(m_mix_norm_g, v_mix_norm_g), mix_ada_w=(m_mix_ada_w, v_mix_ada_w), mix_ada_b=(m_mix_ada_b, v_mix_ada_b), ffn_norm_g=(m_ffn_norm_g, v_ffn_norm_g), ffn_ada_w=(m_ffn_ada_w, v_ffn_ada_w), ffn_ada_b=(m_ffn_ada_b, v_ffn_ada_b), ffn_w_in=(m_ffn_w_in, v_ffn_w_in), ffn_w_out=(m_ffn_w_out, v_ffn_w_out), conv_w_in=(m_conv_w_in, v_conv_w_in), conv_b_in=(m_conv_b_in, v_conv_b_in), conv_w_dw=(m_conv_w_dw, v_conv_w_dw), conv_b_dw=(m_conv_b_dw, v_conv_b_dw), conv_ln_g=(m_conv_ln_g, v_conv_ln_g), conv_ln_b=(m_conv_ln_b, v_conv_ln_b), conv_w_out=(m_conv_w_out, v_conv_w_out), conv_b_out=(m_conv_b_out, v_conv_b_out), kv_norm_g=(m_kv_norm_g, v_kv_norm_g), kv_ada_w=(m_kv_ada_w, v_kv_ada_w), kv_ada_b=(m_kv_ada_b, v_kv_ada_b), kv_w=(m_kv_w, v_kv_w), forget_b=(m_forget_b, v_forget_b), attn_w_q=(m_attn_w_q, v_attn_w_q), attn_w_o=(m_attn_w_o, v_attn_w_o), final_norm_g=(m_final_norm_g, v_final_norm_g))
    names = list(weights)

    deltas, new_m, new_v = {}, {}, {}
    small_names = [n for n in names if weights[n].size < (1 << 16)]
    for n in names:
        if n in small_names:
            continue
        w = weights[n]
        w2 = w.reshape(-1, w.shape[-1])
        dl, nm, nv = _adamw(w2, grads[n].reshape(w2.shape), moms[n][0].reshape(w2.shape), moms[n][1].reshape(w2.shape), "adamw_" + n)
        deltas[n], new_m[n], new_v[n] = dl.reshape(w.shape), nm.reshape(w.shape), nv.reshape(w.shape)

    def pack_small(get):
        flat = jnp.concatenate([get(n).reshape(-1) for n in small_names])
        rows_ = -(-flat.shape[0] // (8 * LANE)) * 8
        return jnp.pad(flat, (0, rows_ * LANE - flat.shape[0])).reshape(rows_, LANE)

    ws, gs = pack_small(lambda n: weights[n]), pack_small(lambda n: grads[n])
    ms_, vs_ = pack_small(lambda n: moms[n][0]), pack_small(lambda n: moms[n][1])
    vs_ = jnp.where(jnp.arange(vs_.size).reshape(vs_.shape) < sum(weights[n].size for n in small_names), vs_, 1.0)
    dl, nm, nv = _adamw(ws, gs, ms_, vs_, "adamw_small")
    off = 0
    for n in small_names:
        sz = weights[n].size
        shp = weights[n].shape
        deltas[n] = dl.reshape(-1)[off:off + sz].reshape(shp)
        new_m[n] = nm.reshape(-1)[off:off + sz].reshape(shp)
        new_v[n] = nv.reshape(-1)[off:off + sz].reshape(shp)
        off += sz

    grad_out = [grads[n].reshape(weights[n].shape) for n in names]
    return (loss, dx0[None], *grad_out, *[deltas[n] for n in names], *[new_m[n] for n in names], *[new_v[n] for n in names])
```

```python
import functools

import jax
import jax.numpy as jnp
from jax import lax
from jax.experimental import pallas as pl
from jax.experimental.pallas import tpu as pltpu

F32 = jnp.float32
BF16 = jnp.bfloat16
MESH = pl.DeviceIdType.MESH

EPS = 1e-6
N_HEADS = 16
HEAD_DIM = 64
CONV_K = 31
LANE = 128
SUBLANES = 8
HALO = 32
ATT_FWD_TQ = 2048
ATT_TQ = 1024
ATT_TK = 512
NPIECE = 3
SPARE = (HEAD_DIM, 0)
VMEM_MB = 48
ATT_BWD_VMEM_MB = 56

ADAM_LR = 0.001
ADAM_B1 = 0.9
ADAM_B2 = 0.999
ADAM_EPS = 1e-08
ADAM_WD = 0.01
ADAM_STEP = 10


def _sds(shape, dtype):
    return jax.ShapeDtypeStruct(tuple(shape), dtype)


def _cp(sem=None, vmem_mb=VMEM_MB):
    return pltpu.CompilerParams(dimension_semantics=sem, vmem_limit_bytes=vmem_mb << 20)


def _tile(n, pref):
    return pref if n % pref == 0 else n


def _row_tile(r, mult, width=1024):
    cap = max(mult, (512 * 1024 // width) // mult * mult)
    for cand in range(cap, mult - 1, -mult):
        if r % cand == 0:
            return cand
    return r


def _resident(shape):
    nd = len(shape)
    return pl.BlockSpec(tuple(shape), lambda *_: (0,) * nd, pipeline_mode=pl.Buffered(1))


def _dot(a, b):
    return jnp.dot(a, b, preferred_element_type=F32)


def _dot_nt(a, b):
    return lax.dot_general(a, b, (((1,), (1,)), ((), ())), preferred_element_type=F32)


def _dot_tn(a, b):
    return lax.dot_general(a, b, (((0,), (0,)), ((), ())), preferred_element_type=F32)


def _sigmoid(x):
    return 1.0 / (1.0 + jnp.exp(-x))


def _colsum(x):
    return jnp.sum(x, axis=0, keepdims=True)


def _rms_parts(x):
    rstd = lax.rsqrt(jnp.mean(x * x, axis=-1, keepdims=True) + EPS)
    return x * rstd, rstd


class _Gather8:
    def __init__(self, xs):
        self.n = len(xs)
        self.m = [x.shape[0] for x in xs]
        self.land = [_sds((8 * x.shape[0],) + tuple(x.shape[1:]), x.dtype) for x in xs]
        self.sems = [pltpu.SemaphoreType.DMA((7 * self.n,)), pltpu.SemaphoreType.DMA((7 * self.n,)),
                     pltpu.SemaphoreType.DMA((self.n,))]

    def _parts(self, a, x_refs, out_refs, send_sems, recv_sems, local_sems):
        x, y, c = lax.axis_index("x"), lax.axis_index("y"), lax.axis_index("c")
        me, sibling = (x, y, c), (x, y, 1 - c)
        chips = [(1 - x, y), (x, 1 - y), (1 - x, 1 - y)]
        m_per, x_ref, out_ref = self.m[a], x_refs[a], out_refs[a]

        def rows(px, py, pc):
            return out_ref.at[pl.ds((4 * px + 2 * py + pc) * m_per, m_per)]

        def copy(k, block, to, src=None):
            return pltpu.make_async_remote_copy(
                src_ref=rows(*block) if src is None else src, dst_ref=rows(*block),
                send_sem=send_sems.at[7 * a + k], recv_sem=recv_sems.at[7 * a + k], device_id=to, device_id_type=MESH)

        mine = pltpu.make_async_copy(x_ref, rows(*me), local_sems.at[a])
        first = [copy(0, me, sibling, src=x_ref)]
        first += [copy(1 + j, me, (*chip, c), src=x_ref) for j, chip in enumerate(chips)]
        passed = [copy(4 + j, (*chip, c), sibling) for j, chip in enumerate(chips)]
        return c, me, sibling, chips, copy, mine, first, passed

    def start(self, *refs):
        for a in range(self.n):
            _, _, _, _, _, mine, first, _ = self._parts(a, *refs)
            mine.start()
            for cp in first:
                cp.start()

    def finish(self, *refs):
        parts = [self._parts(a, *refs) for a in range(self.n)]
        for j in range(3):
            for c, me, sibling, chips, copy, mine, first, passed in parts:
                copy(1 + j, (*chips[j], c), me).wait_recv()
                passed[j].start()
        for c, me, sibling, chips, copy, mine, first, passed in parts:
            copy(0, sibling, me).wait_recv()
            for j, chip in enumerate(chips):
                copy(4 + j, (*chip, 1 - c), me).wait_recv()
            for cp in first + passed:
                cp.wait_send()
            mine.wait()


class _ScatterChips:
    def __init__(self, qs):
        self.n = len(qs)
        self.land = [_sds((3,) + tuple(q.shape[1:]), q.dtype) for q in qs]
        self.sems = [pltpu.SemaphoreType.DMA((3 * self.n,)), pltpu.SemaphoreType.DMA((3 * self.n,))]

    def _copies(self, q_refs, land_refs, send_sems, recv_sems):
        x, y, c = lax.axis_index("x"), lax.axis_index("y"), lax.axis_index("c")
        chips = [(1 - x, y), (x, 1 - y), (1 - x, 1 - y)]
        return [pltpu.make_async_remote_copy(
            src_ref=q_refs[a].at[2 * cx + cy], dst_ref=land_refs[a].at[k],
            send_sem=send_sems.at[3 * a + k], recv_sem=recv_sems.at[3 * a + k],
            device_id=(cx, cy, c), device_id_type=MESH) for a in range(self.n) for k, (cx, cy) in enumerate(chips)]

    def start(self, *refs):
        for cp in self._copies(*refs):
            cp.start()

    def finish(self, *refs):
        copies = self._copies(*refs)
        for cp in copies:
            cp.wait_recv()
        for cp in copies:
            cp.wait_send()


class _SwapHalves:
    def __init__(self, ps):
        self.n = len(ps)
        self.land = [_sds((p.shape[0],) + tuple(p.shape[2:]), p.dtype) for p in ps]
        self.nb = [p.shape[0] for p in ps]
        tot = sum(self.nb)
        self.sems = [pltpu.SemaphoreType.DMA((tot,)), pltpu.SemaphoreType.DMA((tot,))]

    def _copies(self, p_refs, land_refs, send_sems, recv_sems):
        x, y, c = lax.axis_index("x"), lax.axis_index("y"), lax.axis_index("c")
        out, k = [], 0
        for a in range(self.n):
            for j in range(self.nb[a]):
                out.append(pltpu.make_async_remote_copy(
                    src_ref=p_refs[a].at[j, 1 - c], dst_ref=land_refs[a].at[j], send_sem=send_sems.at[k],
                    recv_sem=recv_sems.at[k], device_id=(x, y, 1 - c), device_id_type=MESH))
                k += 1
        return out

    start = _ScatterChips.start
    finish = _ScatterChips.finish


def _hosted_call(body, comm, *, grid, in_specs, out_specs, out_shape, scratch_shapes, name, sem, args, vmem_mb=VMEM_MB):
    def first():
        return functools.reduce(jnp.logical_and, [pl.program_id(a) == 0 for a in range(len(grid))])

    def last():
        return functools.reduce(jnp.logical_and, [pl.program_id(a) == g - 1 for a, g in enumerate(grid)])

    out_specs = tuple(out_specs) if isinstance(out_specs, (tuple, list)) else (out_specs,)
    out_shape = tuple(out_shape) if isinstance(out_shape, (tuple, list)) else (out_shape,)
    if comm is None:
        return pl.pallas_call(body, grid=grid, in_specs=list(in_specs), out_specs=out_specs, out_shape=out_shape,
                              scratch_shapes=list(scratch_shapes), name=name, compiler_params=_cp(sem, vmem_mb))(*args)
    ex, srcs = comm
    n_in, n_out, n_scr, n_ex = len(in_specs), len(out_shape), len(scratch_shapes), ex.n

    def wrapped(*refs):
        ins, src_refs = refs[:n_in], refs[n_in:n_in + n_ex]
        o0 = n_in + n_ex
        outs, land_refs = refs[o0:o0 + n_out], refs[o0 + n_out:o0 + n_out + n_ex]
        s0 = o0 + n_out + n_ex
        scr, sems = refs[s0:s0 + n_scr], refs[s0 + n_scr:]

        @pl.when(first())
        def _():
            ex.start(src_refs, land_refs, *sems)

        body(*ins, *outs, *scr)

        @pl.when(last())
        def _():
            ex.finish(src_refs, land_refs, *sems)

    hbm = pl.BlockSpec(memory_space=pl.ANY)
    res = pl.pallas_call(
        wrapped, grid=grid, in_specs=[*in_specs, *[hbm] * n_ex], out_specs=(*out_specs, *[hbm] * n_ex),
        out_shape=(*out_shape, *ex.land), scratch_shapes=[*scratch_shapes, *ex.sems], name=name,
        compiler_params=_cp(tuple("arbitrary" for _ in grid), vmem_mb))(*args, *srcs)
    return (*res[:n_out], list(res[n_out:]))


def _exchange(ex, srcs, name, in_vmem=False):
    n = ex.n

    def body(*refs):
        src_refs, land_refs, sems = refs[:n], refs[n:2 * n], refs[2 * n:]
        ex.start(src_refs, land_refs, *sems)
        ex.finish(src_refs, land_refs, *sems)

    spec = pl.BlockSpec(memory_space=pltpu.VMEM if in_vmem else pl.ANY)
    return list(pl.pallas_call(
        body, out_shape=tuple(ex.land), in_specs=[spec] * n, out_specs=tuple([spec] * n),
        scratch_shapes=ex.sems, name=name)(*srcs))


def _allgather8(x_shard, name, in_vmem):
    return _exchange(_Gather8([x_shard]), [x_shard], name, in_vmem)[0]


def _share_halves(bufs):
    n = len(bufs)

    def body(*refs):
        b_refs, out_refs, send_sems, recv_sems = refs[:n], refs[n:2 * n], refs[2 * n], refs[2 * n + 1]
        x, y, c = lax.axis_index("x"), lax.axis_index("y"), lax.axis_index("c")
        copies = [pltpu.make_async_remote_copy(
            src_ref=b_refs[k].at[c], dst_ref=out_refs[k].at[c], send_sem=send_sems.at[k], recv_sem=recv_sems.at[k],
            device_id=(x, y, 1 - c), device_id_type=MESH) for k in range(n)]
        for cp in copies:
            cp.start()
        for cp in copies:
            cp.wait_recv()
        for cp in copies:
            cp.wait_send()

    hbm = pl.BlockSpec(memory_space=pl.ANY)
    return pl.pallas_call(
        body, out_shape=tuple(_sds(b.shape, b.dtype) for b in bufs), in_specs=[hbm] * n, out_specs=tuple([hbm] * n),
        scratch_shapes=[pltpu.SemaphoreType.DMA((n,)), pltpu.SemaphoreType.DMA((n,))],
        input_output_aliases={k: k for k in range(n)}, name="rs_share_halves")(*bufs)


def _add_halves(p, land, sel, name):
    nb, _, r, w = p.shape
    tr = _row_tile(r, 16, w)

    def body(sel_ref, p_ref, l_ref, q16_ref, own_ref):
        q = p_ref[0, 0] + l_ref[0]
        q16_ref[0] = q.astype(BF16)

        @pl.when(pl.program_id(1) == sel_ref[1])
        def _():
            own_ref[...] = q

    gs = pltpu.PrefetchScalarGridSpec(
        num_scalar_prefetch=1, grid=(r // tr, nb),
        in_specs=[pl.BlockSpec((1, 1, tr, w), lambda i, j, sl: (j, sl[0], i, 0)),
                  pl.BlockSpec((1, tr, w), lambda i, j, sl: (j, i, 0))],
        out_specs=(pl.BlockSpec((1, tr, w), lambda i, j, sl: (j, i, 0)), pl.BlockSpec((tr, w), lambda i, j, sl: (i, 0))))
    return pl.pallas_call(body, grid_spec=gs, out_shape=(_sds((nb, r, w), BF16), _sds((r, w), F32)), name=name,
                          compiler_params=_cp(("parallel", "arbitrary")))(sel, p, land)


def _add_chips(own, land, sel, name):
    r, w = own.shape
    tr = _row_tile(r, 16, w)

    def body(sel_ref, q_ref, l_ref, o_ref):
        o_ref[0] = ((q_ref[...] + l_ref[0].astype(F32)) + l_ref[1].astype(F32)) + l_ref[2].astype(F32)

    gs = pltpu.PrefetchScalarGridSpec(
        num_scalar_prefetch=1, grid=(r // tr,),
        in_specs=[pl.BlockSpec((tr, w), lambda i, sl: (i, 0)), pl.BlockSpec((3, tr, w), lambda i, sl: (0, i, 0))],
        out_specs=pl.BlockSpec((1, tr, w), lambda i, sl: (sl[0], i, 0)))
    return pl.pallas_call(body, grid_spec=gs, out_shape=_sds((2, r, w), F32), name=name,
                          compiler_params=_cp(("parallel",)))(sel, own, land)


def _sum8(g):
    _, m, n = g.shape

    def body(g_ref, o_ref):
        acc = g_ref[0]
        for k in range(1, 8):
            acc = acc + g_ref[k]
        o_ref[...] = acc

    return pl.pallas_call(body, out_shape=_sds((m, n), g.dtype), name="sum8")(g)


def _ada_fwd(c_all, w3, name):
    nl, d, n = w3.shape
    tn = 256

    def body(c_ref, w_ref, o_ref):
        cc = c_ref[...]
        ca = (cc * _sigmoid(cc)).astype(BF16)
        o_ref[0] = _dot(ca, w_ref[0].astype(BF16))

    return pl.pallas_call(
        body, grid=(nl, n // tn), out_shape=_sds((nl, 8, n), F32),
        in_specs=[pl.BlockSpec((8, d), lambda l, j: (0, 0)), pl.BlockSpec((1, d, tn), lambda l, j: (l, 0, j))],
        out_specs=pl.BlockSpec((1, 8, tn), lambda l, j: (l, 0, j)),
        name=name, compiler_params=_cp(("parallel", "parallel")))(c_all, w3)


def _in_pair(x, gain, shift, scale, wg, bias, conv, name, comm=None):
    s, d = x.shape
    n = wg.shape[2]
    ts = _tile(s, 512)

    def body(*refs):
        if conv:
            x_ref, g_ref, sh_ref, sc_ref, w_ref, b_ref, h_ref, o_ref, sa_ref, sb_ref = refs
        else:
            x_ref, g_ref, sh_ref, sc_ref, w_ref, h_ref, o_ref, sa_ref, sb_ref = refs
        xhat, _ = _rms_parts(x_ref[...])
        h = ((xhat * g_ref[...]) * (1.0 + sc_ref[...]) + sh_ref[...]).astype(BF16)
        h_ref[...] = h
        for q in range(2):
            a = _dot(h, w_ref[q])
            b = _dot(h, w_ref[q + 2])
            cs = pl.ds(q * n, n)
            if conv:
                a = a + b_ref[q]
                b = b + b_ref[q + 2]
                o_ref[:, cs] = a * _sigmoid(b)
            else:
                o_ref[:, cs] = (a * _sigmoid(a) * b).astype(BF16)
            sa_ref[:, cs] = a.astype(BF16)
            sb_ref[:, cs] = b.astype(BF16)

    vec = pl.BlockSpec((1, d), lambda i: (0, 0))
    in_specs = [pl.BlockSpec((ts, d), lambda i: (i, 0)), vec, vec, vec, _resident(wg.shape)]
    args = [x, gain, shift, scale, wg]
    if conv:
        in_specs.append(_resident(bias.shape))
        args.append(bias)
    tile = pl.BlockSpec((ts, 2 * n), lambda i: (i, 0))
    return _hosted_call(
        body, comm, grid=(s // ts,),
        out_shape=(_sds((s, d), BF16), _sds((s, 2 * n), F32 if conv else BF16), _sds((s, 2 * n), BF16), _sds((s, 2 * n), BF16)),
        in_specs=in_specs, out_specs=(pl.BlockSpec((ts, d), lambda i: (i, 0)), tile, tile, tile),
        scratch_shapes=[], name=name, sem=("parallel",), args=args)


def _shift_copies(buf, shf):
    n = shf.shape[1]
    for r in range(1, SUBLANES):
        shf[r - 1, :, :] = buf[pl.ds(r, n), :]


def _shifted(buf, shf, start, n, cs):
    a, r = divmod(start, SUBLANES)
    if r == 0:
        return buf[pl.ds(start, n), cs]
    return shf[r - 1, pl.ds(a * SUBLANES, n), cs]


def _dwconv_fwd(glu, wdw, bdw, lng, lnb, comm=None):
    s, d = glu.shape
    ts = _tile(s, 256)
    rb, cb = 32, 256

    def body(cur_ref, halo_ref, w_ref, b_ref, g_ref, be_ref, dwo_ref, sw_ref, buf, shf):
        i = pl.program_id(0)

        @pl.when(i == 0)
        def _():
            buf[pl.ds(0, HALO), :] = jnp.zeros((HALO, d), F32)

        @pl.when(i > 0)
        def _():
            buf[pl.ds(0, HALO), :] = halo_ref[...]

        buf[pl.ds(HALO, ts), :] = cur_ref[...]
        _shift_copies(buf, shf)
        for r in range(ts // rb):
            for cc in range(d // cb):
                cs = pl.ds(cc * cb, cb)
                acc = jnp.zeros((rb, cb), F32) + b_ref[:, cs]
                for k in range(CONV_K):
                    acc = acc + w_ref[pl.ds(k, 1), cs] * _shifted(buf, shf, HALO - (CONV_K - 1) + k + r * rb, rb, cs)
                dwo_ref[pl.ds(r * rb, rb), cs] = acc
            rows = pl.ds(r * rb, rb)
            yv = dwo_ref[rows, :]
            mu = jnp.mean(yv, axis=-1, keepdims=True)
            yc = yv - mu
            var = jnp.mean(yc * yc, axis=-1, keepdims=True)
            ln = yc * lax.rsqrt(var + EPS) * g_ref[...] + be_ref[...]
            sw_ref[rows, :] = (ln * _sigmoid(ln)).astype(BF16)

    vec = pl.BlockSpec((1, d), lambda i: (0, 0))
    return _hosted_call(
        body, comm, grid=(s // ts,), out_shape=(_sds((s, d), F32), _sds((s, d), BF16)),
        in_specs=[pl.BlockSpec((ts, d), lambda i: (i, 0)),
                  pl.BlockSpec((HALO, d), lambda i: (jnp.maximum(i * (ts // HALO) - 1, 0), 0)),
                  pl.BlockSpec((HALO, d), lambda i: (0, 0)), vec, vec, vec],
        out_specs=(pl.BlockSpec((ts, d), lambda i: (i, 0)), pl.BlockSpec((ts, d), lambda i: (i, 0))),
        scratch_shapes=[pltpu.VMEM((HALO + ts, d), F32), pltpu.VMEM((SUBLANES - 1, HALO + ts - SUBLANES, d), F32)],
        name="dwconv_fwd", sem=("parallel",),
        args=(glu, glu, wdw, bdw, lng, lnb))


def _mm_res(a, w, b, gate, x, name):
    s, k = a.shape
    d = w.shape[1]
    ts = _tile(s, 512)

    def body(a_ref, w_ref, b_ref, g_ref, x_ref, o_ref):
        yv = _dot(a_ref[...], w_ref[...]) + b_ref[...]
        o_ref[...] = x_ref[...] + g_ref[...] * yv

    vec = pl.BlockSpec((1, d), lambda i: (0, 0))
    return pl.pallas_call(
        body, grid=(s // ts,), out_shape=_sds((s, d), F32),
        in_specs=[pl.BlockSpec((ts, k), lambda i: (i, 0)), _resident((k, d)), vec, vec, pl.BlockSpec((ts, d), lambda i: (i, 0))],
        out_specs=pl.BlockSpec((ts, d), lambda i: (i, 0)),
        name=name, compiler_params=_cp(("parallel",)))(a, w, b, gate, x)


def _qkv(x, kvp, mxp, wk, wv, wf, wq):
    s, d = x.shape
    ts = _tile(s, 512)
    qscale = HEAD_DIM ** -0.5

    def body(x_ref, gk, shk, sck, gm, shm, scm, wk_ref, wv_ref, wf_ref, wq_ref, hk_ref, h1_ref, k_ref, v_ref, q_ref, f_ref):
        xhat, _ = _rms_parts(x_ref[...])
        hk = ((xhat * gk[...]) * (1.0 + sck[...]) + shk[...]).astype(BF16)
        h1 = ((xhat * gm[...]) * (1.0 + scm[...]) + shm[...]).astype(BF16)
        hk_ref[...] = hk
        h1_ref[...] = h1
        k_ref[...] = _dot(hk, wk_ref[...]).astype(BF16)
        v_ref[...] = _dot(hk, wv_ref[...]).astype(BF16)
        f_ref[...] = _dot(hk, wf_ref[...])
        q_ref[...] = (_dot(h1, wq_ref[...]) * qscale).astype(BF16)

    vec = pl.BlockSpec((1, d), lambda i: (0, 0))
    row = pl.BlockSpec((ts, d), lambda i: (i, 0))
    return pl.pallas_call(
        body, grid=(s // ts,),
        out_shape=tuple(_sds((s, d), BF16) for _ in range(5)) + (_sds((s, LANE), F32),),
        in_specs=[row, vec, vec, vec, vec, vec, vec, _resident((d, d)), _resident((d, d)), _resident((d, LANE)), _resident((d, d))],
        out_specs=(row, row, row, row, row, pl.BlockSpec((ts, LANE), lambda i: (i, 0))),
        name="qkv_proj", compiler_params=_cp(("parallel",)))(x, *kvp, *mxp, wk, wv, wf, wq)


def _log_sigmoid(z):
    return jnp.minimum(z, 0.0) - jnp.log(1.0 + jnp.exp(-jnp.abs(z)))


def _cumsum_fwd(flog, fb):
    s = flog.shape[0]
    ts = _tile(s, 256)

    def body(f_ref, b_ref, cum_ref, cumt_ref, carry):
        @pl.when(pl.program_id(0) == 0)
        def _():
            carry[...] = jnp.zeros_like(carry)

        ls = _log_sigmoid(f_ref[...] + b_ref[...])
        r = lax.broadcasted_iota(jnp.int32, (ts, ts), 0)
        cidx = lax.broadcasted_iota(jnp.int32, (ts, ts), 1)
        tri = (cidx <= r).astype(F32)
        cs = jnp.dot(tri, ls, preferred_element_type=F32, precision=lax.Precision.HIGHEST) + carry[...]
        cum_ref[...] = cs
        cumt_ref[...] = cs.T
        carry[...] = cs[ts - 1:ts, :]

    return pl.pallas_call(
        body, grid=(s // ts,), out_shape=(_sds((s, LANE), F32), _sds((LANE, s), F32)),
        in_specs=[pl.BlockSpec((ts, LANE), lambda i: (i, 0)), pl.BlockSpec((1, LANE), lambda i: (0, 0))],
        out_specs=(pl.BlockSpec((ts, LANE), lambda i: (i, 0)), pl.BlockSpec((LANE, ts), lambda i: (0, i))),
        scratch_shapes=[pltpu.VMEM((1, LANE), F32)],
        name="forget_cumsum", compiler_params=_cp(("arbitrary",)))(flog, fb)


def _pick_row(m, idx):
    r = lax.broadcasted_iota(jnp.int32, (m.shape[0], 1), 0)
    return jnp.sum(jnp.where(r == idx, m, 0.0), axis=0, keepdims=True)


def _pick_col(m, idx):
    cidx = lax.broadcasted_iota(jnp.int32, (1, m.shape[1]), 1)
    return jnp.sum(jnp.where(cidx == idx, m, 0.0), axis=1, keepdims=True)


def _split3(x):
    hi = x.astype(BF16)
    r1 = x - hi.astype(F32)
    mid = r1.astype(BF16)
    lo = (r1 - mid.astype(F32)).astype(BF16)
    return hi, mid, lo


def _head_mask(lane, hh):
    lo = lane < HEAD_DIM
    return lo if hh == 0 else jnp.logical_not(lo)


def _attn_prep(k, v, cum):
    s, d = k.shape
    npair = d // LANE
    tc = _tile(s, 1024)

    def body(k_ref, v_ref, c_ref, ka_ref, kt_ref, vt_ref):
        p = pl.program_id(0)
        lane = lax.broadcasted_iota(jnp.int32, (1, LANE), 1)
        kk = k_ref[...]
        vv = v_ref[...].astype(F32)
        ckt = c_ref[...]
        for hh in range(2):
            head = _head_mask(lane, hh)
            b = SPARE[hh]
            ck = _pick_col(ckt, 2 * p + hh)
            extra = jnp.where(lane == b + NPIECE, 1.0, 0.0).astype(BF16) + jnp.zeros((tc, LANE), BF16)
            for n_, pc in enumerate(_split3(ck)):
                extra = jnp.where(lane == b + n_, pc, extra)
            ka = jnp.where(head, kk, extra)
            ka_ref[0, hh] = ka
            kt_ref[0, hh] = ka.astype(F32).T.astype(BF16)
            vx = jnp.where(head, vv, jnp.where(lane == b, 1.0, 0.0))
            vt_ref[0, hh] = vx.T.astype(BF16)

    blk = pl.BlockSpec((tc, LANE), lambda p, c: (c, p))
    return pl.pallas_call(
        body, grid=(npair, s // tc),
        out_shape=(_sds((npair, 2, s, LANE), BF16), _sds((npair, 2, LANE, s), BF16), _sds((npair, 2, LANE, s), BF16)),
        in_specs=[blk, blk, pl.BlockSpec((tc, LANE), lambda p, c: (c, 0))],
        out_specs=(pl.BlockSpec((1, 2, tc, LANE), lambda p, c: (p, 0, c, 0)),
                   pl.BlockSpec((1, 2, LANE, tc), lambda p, c: (p, 0, 0, c)),
                   pl.BlockSpec((1, 2, LANE, tc), lambda p, c: (p, 0, 0, c))),
        name="fox_attn_prep", compiler_params=_cp(("parallel", "parallel")))(k, v, cum)


def _q_aug(qq, lane, hh):
    b = SPARE[hh]
    sel = jnp.logical_and(lane >= b, lane < b + NPIECE)
    neg = jnp.full((1, LANE), -1.0, BF16)
    zl = jnp.zeros((1, LANE), BF16)
    return jnp.where(_head_mask(lane, hh), qq, jnp.where(sel, neg, zl))


def _attn_fwd(q, kaug, vtr, cumt):
    s, d = q.shape
    tq = _tile(s, ATT_FWD_TQ)
    tk = _tile(s, ATT_TK)
    npair = d // LANE
    npart = max(1, tq // tk)

    def body(q_ref, ka_ref, vt_ref, cumt_ref, o_ref, lse_ref):
        p = pl.program_id(0)
        i = pl.program_id(1)
        lane = lax.broadcasted_iota(jnp.int32, (1, LANE), 1)
        qq = q_ref[...]
        qx = (_q_aug(qq, lane, 0), _q_aug(qq, lane, 1))
        cqt = cumt_ref[:, pl.ds(pl.multiple_of(i * tq, tq), tq)]
        cq = (_pick_row(cqt, 2 * p), _pick_row(cqt, 2 * p + 1))
        jd = (i * tq) // tk

        def kv_step(j, carry, diag, q_lo=0):
            ks = pl.multiple_of(j * tk, tk)
            nq_ = tq - q_lo
            if diag:
                krow = lax.broadcasted_iota(jnp.int32, (tk, nq_), 0) + j * tk
                qcol = lax.broadcasted_iota(jnp.int32, (tk, nq_), 1) + (i * tq + q_lo)
                causal = krow <= qcol
            out = []
            for hh in range(2):
                m_all, acc_all = carry[2 * hh], carry[2 * hh + 1]
                m, acc, cqh = m_all[:, q_lo:], acc_all[:, q_lo:], cq[hh][:, q_lo:]
                sc = _dot_nt(ka_ref[0, hh, pl.ds(ks, tk), :], qx[hh][q_lo:, :])
                if diag:
                    sc = jnp.where(causal, sc, -jnp.inf)
                mx = jnp.max(sc, axis=0, keepdims=True) + cqh
                mn = jnp.maximum(m, mx)
                alpha = jnp.exp(m - mn)
                pt = jnp.exp(sc + (cqh - mn)).astype(BF16)
                acc = alpha * acc + _dot(vt_ref[0, hh, :, pl.ds(ks, tk)], pt)
                if q_lo:
                    mn = jnp.concatenate([m_all[:, :q_lo], mn], axis=1)
                    acc = jnp.concatenate([acc_all[:, :q_lo], acc], axis=1)
                out += [mn, acc]
            return tuple(out)

        minit = jnp.full((1, tq), -jnp.inf, F32)
        ainit = jnp.zeros((LANE, tq), F32)
        carry = (minit, ainit, minit, ainit)
        for pj in range(npart):
            carry = kv_step(jd + pj, carry, True, q_lo=pj * tk)
        carry = lax.fori_loop(0, jd, lambda j, cr: kv_step(j, cr, False), carry)
        m0, a0, m1, a1 = carry
        l0 = a0[SPARE[0]:SPARE[0] + 1, :]
        l1 = a1[SPARE[1]:SPARE[1] + 1, :]
        row = lax.broadcasted_iota(jnp.int32, (LANE, 1), 0)
        ot = jnp.where(row < HEAD_DIM, a0 / l0, a1 / l1)
        o_ref[...] = ot.T.astype(BF16)
        r8 = lax.broadcasted_iota(jnp.int32, (8, 1), 0)
        lse_ref[0] = jnp.where(r8 == 0, m0 + jnp.log(l0), jnp.where(r8 == 1, m1 + jnp.log(l1), 0.0))

    return pl.pallas_call(
        body, grid=(npair, s // tq), out_shape=(_sds((s, d), BF16), _sds((npair, 8, s), F32)),
        in_specs=[pl.BlockSpec((tq, LANE), lambda p, i: (i, p)),
                  pl.BlockSpec((1, 2, s, LANE), lambda p, i: (p, 0, 0, 0)),
                  pl.BlockSpec((1, 2, LANE, s), lambda p, i: (p, 0, 0, 0)),
                  pl.BlockSpec((N_HEADS, s), lambda p, i: (0, 0))],
        out_specs=(pl.BlockSpec((tq, LANE), lambda p, i: (i, p)), pl.BlockSpec((1, 8, tq), lambda p, i: (p, 0, i))),
        name="fox_attn_fwd", compiler_params=_cp(("parallel", "parallel")))(q, kaug, vtr, cumt)


def _mm_res_final(a, w, gate, x, gain, target):
    s, k = a.shape
    d = w.shape[1]
    ts = _tile(s, 512)

    def body(a_ref, w_ref, gt_ref, x_ref, g_ref, t_ref, lsum_ref, dx_ref, dg_ref):
        @pl.when(pl.program_id(0) == 0)
        def _():
            lsum_ref[...] = jnp.zeros_like(lsum_ref)
            dg_ref[...] = jnp.zeros_like(dg_ref)

        xv = x_ref[...] + gt_ref[...] * _dot(a_ref[...], w_ref[...])
        xhat, rstd = _rms_parts(xv)
        e = xhat * g_ref[...] - t_ref[...]
        lsum_ref[...] += _colsum(e * e)
        dout = e * (1.0 / d)
        dg_ref[...] += _colsum(dout * xhat)
        dxhat = dout * g_ref[...]
        dx_ref[...] = rstd * (dxhat - xhat * jnp.mean(dxhat * xhat, axis=-1, keepdims=True))

    vec = pl.BlockSpec((1, d), lambda i: (0, 0))
    row = pl.BlockSpec((ts, d), lambda i: (i, 0))
    return pl.pallas_call(
        body, grid=(s // ts,), out_shape=(_sds((1, d), F32), _sds((s, d), F32), _sds((1, d), F32)),
        in_specs=[pl.BlockSpec((ts, k), lambda i: (i, 0)), _resident((k, d)), vec, row, vec, row], out_specs=(vec, row, vec),
        name="ffn1_out_final_loss", compiler_params=_cp(("arbitrary",)))(a, w, gate, x, gain, target)


def _ffn_bwd_act(dx, gate, w_out, ug, uu, name, comm=None):
    s, d = dx.shape
    f = w_out.shape[0]
    n = f // 2
    ts = _tile(s, 512)

    def body(dx_ref, g_ref, w_ref, ug_ref, uu_ref, dug_ref, duu_ref):
        dy = (dx_ref[...] * g_ref[...]).astype(BF16)
        for q in range(2):
            cs = pl.ds(q * n, n)
            dact = _dot_nt(dy, w_ref[cs, :])
            g = ug_ref[:, cs].astype(F32)
            u = uu_ref[:, cs].astype(F32)
            sg = _sigmoid(g)
            dug_ref[:, cs] = (dact * u * sg * (1.0 + g * (1.0 - sg))).astype(BF16)
            duu_ref[:, cs] = (dact * g * sg).astype(BF16)

    tile = pl.BlockSpec((ts, f), lambda i: (i, 0))
    return _hosted_call(
        body, comm, grid=(s // ts,), out_shape=(_sds((s, f), BF16), _sds((s, f), BF16)),
        in_specs=[pl.BlockSpec((ts, d), lambda i: (i, 0)), pl.BlockSpec((1, d), lambda i: (0, 0)),
                  _resident(w_out.shape), tile, tile],
        out_specs=(tile, tile), scratch_shapes=[], name=name, sem=("parallel",), args=(dx, gate, w_out, ug, uu))


def _dw_mm(a, b_list, tk, tn, name, gate=None, wfull=None, dgate_init=None):
    s, kdim = a.shape
    nb1 = b_list[0].shape[1] // tn
    nb = nb1 * len(b_list)
    ts = _tile(s, 1024)
    nk = kdim // tk
    ns = s // ts
    gated = gate is not None

    def body(*refs):
        a_ref = refs[0]
        b_refs = refs[1:1 + len(b_list)]
        rest = refs[1 + len(b_list):]
        if gated:
            g_ref, w_ref, di_ref, o_ref, dg_ref, acc = rest
        else:
            o_ref, acc = rest
        jn, ik, st = pl.program_id(0), pl.program_id(1), pl.program_id(2)

        @pl.when(st == 0)
        def _():
            acc[...] = jnp.zeros_like(acc)

        for mi, b_ref in enumerate(b_refs):
            @pl.when(jn // nb1 == mi)
            def _(b_ref=b_ref):
                acc[...] += _dot_tn(a_ref[...], b_ref[...].astype(BF16))

        if gated:
            @pl.when(jnp.logical_and(ik == 0, st == 0))
            def _():
                dg_ref[...] = di_ref[...]

        @pl.when(st == ns - 1)
        def _():
            if gated:
                o_ref[0] = acc[...] * g_ref[...]
                dg_ref[...] += _colsum(acc[...] * w_ref[...].astype(F32))
            else:
                o_ref[0] = acc[...]

    in_specs = [pl.BlockSpec((ts, tk), lambda jn, ik, st: (st, ik))]
    for mi in range(len(b_list)):
        in_specs.append(pl.BlockSpec(
            (ts, tn), lambda jn, ik, st, mi=mi: (st, jnp.clip(jn - mi * nb1, 0, nb1 - 1))))
    args = [a] + list(b_list)
    out_shape = [_sds((nb, kdim, tn), F32)]
    out_specs = [pl.BlockSpec((1, tk, tn), lambda jn, ik, st: (jn, ik, 0))]
    if gated:
        vec = pl.BlockSpec((1, tn), lambda jn, ik, st: (0, jn))
        in_specs += [vec, pl.BlockSpec((tk, tn), lambda jn, ik, st: (ik, jn)), vec]
        args += [gate, wfull, dgate_init]
        out_shape.append(_sds((1, nb * tn), F32))
        out_specs.append(vec)
    res = pl.pallas_call(
        body, grid=(nb, nk, ns), out_shape=tuple(out_shape), in_specs=in_specs, out_specs=tuple(out_specs),
        scratch_shapes=[pltpu.VMEM((tk, tn), F32)],
        name=name, compiler_params=_cp(("parallel", "arbitrary", "arbitrary")))(*args)
    return res if gated else res[0]


def _mm_normbwd(terms, x, dxres, gain, scale, name, ts_pref=256, comm=None, second=None):
    s, d = x.shape
    ts = _tile(s, ts_pref)
    sets = [(terms, gain, scale)] + ([second] if second is not None else [])
    arrs, warrs = [], []
    for tms, _, _ in sets:
        for a, _, w, _ in tms:
            if not any(a is z for z in arrs):
                arrs.append(a)
            if not any(w is z for z in warrs):
                warrs.append(w)
    na, nw, ns_ = len(arrs), len(warrs), len(sets)

    def body(*refs):
        a_refs, w_refs = refs[:na], refs[na:na + nw]
        x_ref, dr_ref = refs[na + nw], refs[na + nw + 1]
        par = refs[na + nw + 2:na + nw + 2 + 2 * ns_]
        dx_ref = refs[na + nw + 2 + 2 * ns_]
        sums = refs[na + nw + 3 + 2 * ns_:]

        @pl.when(pl.program_id(0) == 0)
        def _():
            for r in sums:
                r[...] = jnp.zeros_like(r)

        xhat, rstd = _rms_parts(x_ref[...])
        dxhat = None
        for k, (tms, _, _) in enumerate(sets):
            g_ref, sc_ref = par[2 * k], par[2 * k + 1]
            dsh_ref, dsc_ref, dg_ref = sums[3 * k:3 * k + 3]
            dh = None
            for a, c0, w, q in tms:
                ai = next(i for i, z in enumerate(arrs) if z is a)
                wi = next(i for i, z in enumerate(warrs) if z is w)
                part = _dot_nt(a_refs[ai][:, pl.ds(c0, w.shape[2])], w_refs[wi][q])
                dh = part if dh is None else dh + part
            dsh_ref[...] += _colsum(dh)
            dsc_ref[...] += _colsum(dh * (xhat * g_ref[...]))
            dn = dh * (1.0 + sc_ref[...])
            dg_ref[...] += _colsum(dn * xhat)
            dxh = dn * g_ref[...]
            dxhat = dxh if dxhat is None else dxhat + dxh
        dx_ref[...] = dr_ref[...] + rstd * (dxhat - xhat * jnp.mean(dxhat * xhat, axis=-1, keepdims=True))

    vec = pl.BlockSpec((1, d), lambda i: (0, 0))
    row = pl.BlockSpec((ts, d), lambda i: (i, 0))
    in_specs = [pl.BlockSpec((ts, a.shape[1]), lambda i: (i, 0)) for a in arrs]
    in_specs += [_resident(w.shape) for w in warrs]
    in_specs += [row, row] + [vec] * (2 * ns_)
    par_args = [p_ for _, g_, s_ in sets for p_ in (g_, s_)]
    return _hosted_call(
        body, comm, grid=(s // ts,), out_shape=(_sds((s, d), F32),) + tuple(_sds((1, d), F32) for _ in range(3 * ns_)),
        in_specs=in_specs, out_specs=(row,) + tuple(vec for _ in range(3 * ns_)), scratch_shapes=[],
        name=name, sem=("arbitrary",), args=(*arrs, *warrs, x, dxres, *par_args))


def _do_kernel(dx, gate, wo, o):
    s, d = dx.shape
    ts = _tile(s, 512)

    def body(dx_ref, g_ref, w_ref, o_ref, do_ref, dl_ref):
        dy = (dx_ref[...] * g_ref[...]).astype(BF16)
        do = _dot_nt(dy, w_ref[...])
        do_ref[...] = do.astype(BF16)
        prod = do * o_ref[...].astype(F32)
        hrow = lax.broadcasted_iota(jnp.int32, (N_HEADS, d), 0)
        hcol = lax.broadcasted_iota(jnp.int32, (N_HEADS, d), 1) // HEAD_DIM
        sel = (hrow == hcol).astype(F32)
        dl_ref[...] = lax.dot_general(sel, prod, (((1,), (1,)), ((), ())), preferred_element_type=F32,
                                      precision=lax.Precision.HIGHEST)

    row = pl.BlockSpec((ts, d), lambda i: (i, 0))
    return pl.pallas_call(
        body, grid=(s // ts,), out_shape=(_sds((s, d), BF16), _sds((N_HEADS, s), F32)),
        in_specs=[row, pl.BlockSpec((1, d), lambda i: (0, 0)), _resident(wo.shape), row],
        out_specs=(row, pl.BlockSpec((N_HEADS, ts), lambda i: (0, i))),
        name="attn_do", compiler_params=_cp(("parallel",)))(dx, gate, wo, o)


def _attn_bwd(q, do, kaug, kaugt, v, cumt, lse, deltat, comm=None):
    s, d = q.shape
    tq = _tile(s, ATT_TQ)
    tk = _tile(s, ATT_TK)
    assert tq in (tk, 2 * tk)
    npair = d // LANE
    nq = s // tq
    nkb = s // tk
    qscale = HEAD_DIM ** -0.5

    def body(q_ref, do_ref, ka_ref, kt_ref, v_ref, cumt_ref, lse_ref, dl_ref,
             dq_ref, dk_ref, dv_ref, dcq_ref, dck_ref, qaug, dom, rowv, dqt):
        p = pl.program_id(0)
        j = pl.program_id(1)
        lane = lax.broadcasted_iota(jnp.int32, (1, LANE), 1)
        lo = lane < HEAD_DIM
        r8 = lax.broadcasted_iota(jnp.int32, (8, 1), 0)

        @pl.when(j == 0)
        def _():
            dqt[...] = jnp.zeros_like(dqt)
            for c in range(nq):
                rows = pl.ds(c * tq, tq)
                qq = q_ref[rows, :]
                dd = do_ref[rows, :]
                cqt = cumt_ref[:, rows]
                dlt = dl_ref[:, rows]
                lst = lse_ref[0, :, rows]
                for hh in range(2):
                    qaug[hh, rows, :] = _q_aug(qq, lane, hh)
                    dom[hh, rows, :] = jnp.where(_head_mask(lane, hh), dd, jnp.zeros_like(dd))
                    rowv[hh, :, rows] = jnp.where(
                        r8 == 0, _pick_row(cqt, 2 * p + hh) - lst[hh:hh + 1, :],
                        jnp.where(r8 == 1, _pick_row(dlt, 2 * p + hh), 0.0))

        vv = v_ref[...]
        i0 = (j * tk) // tq

        def q_step(qs, nq_, carry, diag):
            dv_acc, dk0, dk1 = carry
            qs = pl.multiple_of(qs, tk)
            if diag:
                krow = lax.broadcasted_iota(jnp.int32, (tk, nq_), 0) + j * tk
                qcol = lax.broadcasted_iota(jnp.int32, (tk, nq_), 1) + qs
                causal = krow <= qcol
            dks = [dk0, dk1]
            for hh in range(2):
                rv = rowv[hh, :, pl.ds(qs, nq_)]
                qa = qaug[hh, pl.ds(qs, nq_), :]
                dh = dom[hh, pl.ds(qs, nq_), :]
                sc = _dot_nt(ka_ref[0, hh], qa)
                if diag:
                    sc = jnp.where(causal, sc, -jnp.inf)
                pt = jnp.exp(sc + rv[0:1, :])
                dpt = _dot_nt(vv, dh)
                dst = (pt * (dpt - rv[1:2, :])).astype(BF16)
                dv_acc = dv_acc + _dot(pt.astype(BF16), dh)
                dks[hh] = dks[hh] + _dot(dst, qa)
                dqt[hh, :, pl.ds(qs, nq_)] += _dot(kt_ref[0, hh], dst)
            return dv_acc, dks[0], dks[1]

        z = jnp.zeros((tk, LANE), F32)
        first = ((j * tk) % tq == 0).astype(jnp.int32)
        carry = lax.fori_loop(0, first, lambda _, cr: q_step(i0 * tq, tq, cr, True), (z, z, z))
        if tq > tk:
            carry = lax.fori_loop(0, 1 - first, lambda _, cr: q_step(j * tk, tq - tk, cr, True), carry)
        dv_acc, dk0, dk1 = lax.fori_loop(i0 + 1, nq, lambda i, cr: q_step(i * tq, tq, cr, False), carry)
        dv_ref[...] = dv_acc.astype(BF16)
        dk_ref[...] = jnp.where(lo, dk0, dk1).astype(BF16)
        dck_ref[0] = jnp.where(r8 == 0, dk0.T[SPARE[0]:SPARE[0] + 1, :],
                               jnp.where(r8 == 1, dk1.T[SPARE[1]:SPARE[1] + 1, :], 0.0))

        @pl.when(j == nkb - 1)
        def _():
            for c in range(nq):
                rows = pl.ds(c * tq, tq)
                a0 = dqt[0, :, rows].T
                a1 = dqt[1, :, rows].T
                dq_ref[rows, :] = (jnp.where(lo, a0, a1) * qscale).astype(BF16)
            r0, r1 = SPARE[0] + NPIECE, SPARE[1] + NPIECE
            dcq_ref[0] = jnp.where(r8 == 0, dqt[0, r0:r0 + 1, :], jnp.where(r8 == 1, dqt[1, r1:r1 + 1, :], 0.0))

    col = pl.BlockSpec((s, LANE), lambda p, j: (0, p), pipeline_mode=pl.Buffered(1))
    rows16 = pl.BlockSpec((N_HEADS, s), lambda p, j: (0, 0), pipeline_mode=pl.Buffered(1))
    blk = pl.BlockSpec((tk, LANE), lambda p, j: (j, p))
    return _hosted_call(
        body, comm, grid=(npair, nkb),
        out_shape=(_sds((s, d), BF16), _sds((s, d), BF16), _sds((s, d), BF16), _sds((npair, 8, s), F32), _sds((npair, 8, s), F32)),
        in_specs=[col, col, pl.BlockSpec((1, 2, tk, LANE), lambda p, j: (p, 0, j, 0)),
                  pl.BlockSpec((1, 2, LANE, tk), lambda p, j: (p, 0, 0, j)), blk, rows16,
                  pl.BlockSpec((1, 8, s), lambda p, j: (p, 0, 0), pipeline_mode=pl.Buffered(1)), rows16],
        out_specs=(pl.BlockSpec((s, LANE), lambda p, j: (0, p)), blk, blk,
                   pl.BlockSpec((1, 8, s), lambda p, j: (p, 0, 0)), pl.BlockSpec((1, 8, tk), lambda p, j: (p, 0, j))),
        scratch_shapes=[pltpu.VMEM((2, s, LANE), BF16), pltpu.VMEM((2, s, LANE), BF16), pltpu.VMEM((2, 8, s), F32),
                        pltpu.VMEM((2, LANE, s), F32)],
        name="fox_attn_bwd", sem=("arbitrary", "arbitrary"), vmem_mb=ATT_BWD_VMEM_MB,
        args=(q, do, kaug, kaugt, v, cumt, lse, deltat))


def _cumsum_bwd(dcq, dck, flog, fb):
    s = flog.shape[0]
    ts = _tile(s, 256)
    nt = s // ts

    def body(dq_ref, dk_ref, f_ref, b_ref, df_ref, db_ref, carry):
        @pl.when(pl.program_id(0) == 0)
        def _():
            carry[...] = jnp.zeros_like(carry)
            db_ref[...] = jnp.zeros_like(db_ref)

        r = lax.broadcasted_iota(jnp.int32, (ts, ts), 0)
        cidx = lax.broadcasted_iota(jnp.int32, (ts, ts), 1)
        tri = (r >= cidx).astype(F32)
        dct = dq_ref[...] + dk_ref[...]
        dlst = jnp.dot(dct, tri, preferred_element_type=F32, precision=lax.Precision.HIGHEST) + carry[...]
        carry[...] = dlst[:, 0:1]
        dls = jnp.concatenate([dlst, jnp.zeros((LANE - N_HEADS, ts), F32)], axis=0).T
        z = f_ref[...] + b_ref[...]
        df = dls * (1.0 / (1.0 + jnp.exp(z)))
        db_ref[...] += _colsum(df)
        df_ref[...] = df.astype(BF16)

    rev = pl.BlockSpec((ts, LANE), lambda i: (nt - 1 - i, 0))
    revt = pl.BlockSpec((N_HEADS, ts), lambda i: (0, nt - 1 - i))
    vec = pl.BlockSpec((1, LANE), lambda i: (0, 0))
    return pl.pallas_call(
        body, grid=(nt,), out_shape=(_sds((s, LANE), BF16), _sds((1, LANE), F32)),
        in_specs=[revt, revt, rev, vec], out_specs=(rev, vec), scratch_shapes=[pltpu.VMEM((N_HEADS, 1), F32)],
        name="forget_cumsum_bwd", compiler_params=_cp(("arbitrary",)))(dcq, dck, flog, fb)


def _conv_bwd1(dx, gate, w_out, b_out, dwo, lng, lnb):
    s, d = dx.shape
    ts = _tile(s, 512)
    ns = s // ts

    def body(dx_ref, g_ref, w_ref, bo_ref, y_ref, lg_ref, lb_ref, dd_ref, dlg_ref, dlb_ref, dbd_ref, dbo_ref, dge_ref, cs):
        i = pl.program_id(0)

        @pl.when(i == 0)
        def _():
            for r in (dlg_ref, dlb_ref, dbd_ref, cs):
                r[...] = jnp.zeros_like(r)

        dxv = dx_ref[...]
        cs[...] += _colsum(dxv)
        dsw = _dot_nt((dxv * g_ref[...]).astype(BF16), w_ref[...])
        yv = y_ref[...]
        mu = jnp.mean(yv, axis=-1, keepdims=True)
        yc = yv - mu
        rstd = lax.rsqrt(jnp.mean(yc * yc, axis=-1, keepdims=True) + EPS)
        xhat = yc * rstd
        ln = xhat * lg_ref[...] + lb_ref[...]
        sg = _sigmoid(ln)
        dln = dsw * (sg * (1.0 + ln * (1.0 - sg)))
        dlg_ref[...] += _colsum(dln * xhat)
        dlb_ref[...] += _colsum(dln)
        dxh = dln * lg_ref[...]
        dd = rstd * (dxh - jnp.mean(dxh, axis=-1, keepdims=True) - xhat * jnp.mean(dxh * xhat, axis=-1, keepdims=True))
        dbd_ref[...] += _colsum(dd)
        dd_ref[...] = dd

        @pl.when(i == ns - 1)
        def _():
            dbo_ref[...] = g_ref[...] * cs[...]
            dge_ref[...] = bo_ref[...] * cs[...]

    vec = pl.BlockSpec((1, d), lambda i: (0, 0))
    row = pl.BlockSpec((ts, d), lambda i: (i, 0))
    return pl.pallas_call(
        body, grid=(ns,), out_shape=(_sds((s, d), F32),) + tuple(_sds((1, d), F32) for _ in range(5)),
        in_specs=[row, vec, _resident(w_out.shape), vec, row, vec, vec], out_specs=(row, vec, vec, vec, vec, vec),
        scratch_shapes=[pltpu.VMEM((1, d), F32)],
        name="conv_bwd_ln", compiler_params=_cp(("arbitrary",)))(dx, gate, w_out, b_out, dwo, lng, lnb)


def _dwconv_bwd(ddwo, glu, a_s, g_s, wdw, comm=None):
    s, d = ddwo.shape
    ts = _tile(s, 256)
    ns = s // ts
    rb, cb = 32, 256
    nrb = ts // rb

    def body(dd_ref, ddn_ref, gl_ref, glh_ref, a_ref, g_ref, w_ref, da_ref, dg_ref, dw_ref, sa_ref, sg_ref, bufd, bufg, dws,
             shd, shg):
        i = pl.program_id(0)

        @pl.when(i == 0)
        def _():
            dws[...] = jnp.zeros_like(dws)
            sa_ref[...] = jnp.zeros_like(sa_ref)
            sg_ref[...] = jnp.zeros_like(sg_ref)
            bufg[pl.ds(0, HALO), :] = jnp.zeros((HALO, d), F32)

        @pl.when(i > 0)
        def _():
            bufg[pl.ds(0, HALO), :] = glh_ref[...]

        bufg[pl.ds(HALO, ts), :] = gl_ref[...]
        bufd[pl.ds(0, ts), :] = dd_ref[...]

        @pl.when(i == ns - 1)
        def _():
            bufd[pl.ds(ts, HALO), :] = jnp.zeros((HALO, d), F32)

        @pl.when(i < ns - 1)
        def _():
            bufd[pl.ds(ts, HALO), :] = ddn_ref[...]

        _shift_copies(bufd, shd)
        _shift_copies(bufg, shg)
        for cc in range(d // cb):
            cs = pl.ds(cc * cb, cb)
            for r in range(nrb):
                acc = jnp.zeros((rb, cb), F32)
                for k in range(CONV_K):
                    acc = acc + w_ref[pl.ds(k, 1), cs] * _shifted(bufd, shd, r * rb + (CONV_K - 1) - k, rb, cs)
                rows = pl.ds(r * rb, rb)
                av = a_ref[rows, cs].astype(F32)
                sg = _sigmoid(g_ref[rows, cs].astype(F32))
                dav = acc * sg
                dgv = acc * av * sg * (1.0 - sg)
                da_ref[rows, cs] = dav.astype(BF16)
                dg_ref[rows, cs] = dgv.astype(BF16)
                sa_ref[:, cs] += _colsum(dav)
                sg_ref[:, cs] += _colsum(dgv)
            for k in range(CONV_K):
                acc8 = jnp.zeros((8, cb), F32)
                for r in range(nrb):
                    prod = bufd[pl.ds(r * rb, rb), cs] * _shifted(bufg, shg, HALO - (CONV_K - 1) + k + r * rb, rb, cs)
                    acc8 = acc8 + (prod[0:8] + prod[8:16]) + (prod[16:24] + prod[24:32])
                dws[pl.ds(8 * k, 8), cs] += acc8

        @pl.when(i == ns - 1)
        def _():
            dw_ref[...] = jnp.zeros_like(dw_ref)
            for k in range(CONV_K):
                dw_ref[pl.ds(k, 1), :] = _colsum(dws[pl.ds(8 * k, 8), :])

    row = pl.BlockSpec((ts, d), lambda i: (i, 0))
    vec = pl.BlockSpec((1, d), lambda i: (0, 0))
    hb = ts // HALO
    return _hosted_call(
        body, comm, grid=(ns,),
        out_shape=(_sds((s, d), BF16), _sds((s, d), BF16), _sds((HALO, d), F32), _sds((1, d), F32), _sds((1, d), F32)),
        in_specs=[row, pl.BlockSpec((HALO, d), lambda i: (jnp.minimum((i + 1) * hb, ns * hb - 1), 0)),
                  row, pl.BlockSpec((HALO, d), lambda i: (jnp.maximum(i * hb - 1, 0), 0)),
                  row, row, pl.BlockSpec((HALO, d), lambda i: (0, 0))],
        out_specs=(row, row, pl.BlockSpec((HALO, d), lambda i: (0, 0)), vec, vec),
        scratch_shapes=[pltpu.VMEM((ts + HALO, d), F32), pltpu.VMEM((HALO + ts, d), F32), pltpu.VMEM((8 * HALO, d), F32),
                        pltpu.VMEM((SUBLANES - 1, HALO + ts - SUBLANES, d), F32),
                        pltpu.VMEM((SUBLANES - 1, HALO + ts - SUBLANES, d), F32)],
        name="dwconv_bwd", sem=("arbitrary",), args=(ddwo, ddwo, glu, glu, a_s, g_s, wdw))


def _ada_wgrad(cat, da, name):
    nl, _, n = da.shape
    d = cat.shape[0]
    tn = 256

    def body(c_ref, d_ref, o_ref):
        acc = c_ref[:, 0:1] * d_ref[0, 0:1, :]
        for r in range(1, 8):
            acc = acc + c_ref[:, r:r + 1] * d_ref[0, r:r + 1, :]
        o_ref[0] = acc

    return pl.pallas_call(
        body, grid=(nl, n // tn), out_shape=_sds((nl, d, n), F32),
        in_specs=[pl.BlockSpec((d, 8), lambda l, j: (0, 0)), pl.BlockSpec((1, 8, tn), lambda l, j: (l, 0, j))],
        out_specs=pl.BlockSpec((1, d, tn), lambda l, j: (l, 0, j)),
        name=name, compiler_params=_cp(("parallel", "parallel")))(cat, da)


def _silu_rows(c_all):
    def body(c_ref, o_ref):
        cc = c_ref[...]
        o_ref[...] = cc * _sigmoid(cc)

    return pl.pallas_call(body, out_shape=_sds(c_all.shape, F32), name="silu_c")(c_all)


def _adamw(w, g, m, v, name):
    r, c = w.shape
    tr = r
    for cand in (512, 256, 128, 64, 32, 16, 8):
        if r % cand == 0 and cand * c * 4 <= (1 << 20):
            tr = cand
            break
    bc1 = 1.0 - ADAM_B1 ** ADAM_STEP
    bc2 = 1.0 - ADAM_B2 ** ADAM_STEP

    def body(w_ref, g_ref, m_ref, v_ref, d_ref, nm_ref, nv_ref):
        gv = g_ref[...]
        mn = ADAM_B1 * m_ref[...] + (1.0 - ADAM_B1) * gv
        vn = ADAM_B2 * v_ref[...] + (1.0 - ADAM_B2) * (gv * gv)
        mh = mn / bc1
        vh = vn / bc2
        d_ref[...] = -ADAM_LR * (mh / (jnp.sqrt(vh) + ADAM_EPS) + ADAM_WD * w_ref[...])
        nm_ref[...] = mn
        nv_ref[...] = vn

    blk = pl.BlockSpec((tr, c), lambda i: (i, 0))
    return pl.pallas_call(
        body, grid=(r // tr,), out_shape=tuple(_sds((r, c), F32) for _ in range(3)),
        in_specs=[blk, blk, blk, blk], out_specs=(blk, blk, blk),
        name=name, compiler_params=_cp(("parallel",)))(w, g, m, v)


def _pad_rows(a, rows, axis):
    pad = [(0, 0)] * a.ndim
    pad[axis] = (0, rows - a.shape[axis])
    return jnp.pad(a, pad)


def _vec(a):
    return a.reshape(1, -1)


def kernel(x, c, mix_norm_g, mix_ada_w, mix_ada_b, ffn_norm_g, ffn_ada_w, ffn_ada_b, ffn_w_in, ffn_w_out, conv_w_in, conv_b_in, conv_w_dw, conv_b_dw, conv_ln_g, conv_ln_b, conv_w_out, conv_b_out, kv_norm_g, kv_ada_w, kv_ada_b, kv_w, forget_b, attn_w_q, attn_w_o, final_norm_g, loss_target, m_mix_norm_g, m_mix_ada_w, m_mix_ada_b, m_ffn_norm_g, m_ffn_ada_w, m_ffn_ada_b, m_ffn_w_in, m_ffn_w_out, m_conv_w_in, m_conv_b_in, m_conv_w_dw, m_conv_b_dw, m_conv_ln_g, m_conv_ln_b, m_conv_w_out, m_conv_b_out, m_kv_norm_g, m_kv_ada_w, m_kv_ada_b, m_kv_w, m_forget_b, m_attn_w_q, m_attn_w_o, m_final_norm_g, v_mix_norm_g, v_mix_ada_w, v_mix_ada_b, v_ffn_norm_g, v_ffn_ada_w, v_ffn_ada_b, v_ffn_w_in, v_ffn_w_out, v_conv_w_in, v_conv_b_in, v_conv_w_dw, v_conv_b_dw, v_conv_ln_g, v_conv_ln_b, v_conv_w_out, v_conv_b_out, v_kv_norm_g, v_kv_ada_w, v_kv_ada_b, v_kv_w, v_forget_b, v_attn_w_q, v_attn_w_o, v_final_norm_g):
    xi, yi, ci = lax.axis_index("x"), lax.axis_index("y"), lax.axis_index("c")
    chip = 2 * xi + yi
    dev = 4 * xi + 2 * yi + ci
    s, d = x.shape[1], x.shape[2]
    f = ffn_w_out.shape[1] * 4
    x0 = x[0]
    nkv = kv_w.shape[1]
    nkv_all = 4 * nkv

    def my_halves(ws):
        return [lax.dynamic_index_in_dim(w.astype(BF16).reshape(2, w.shape[0] // 2, w.shape[1]), ci, axis=0, keepdims=False)
                for w in ws]

    def whole(gath, ws):
        return [g.reshape(4, w.shape[0], w.shape[1]) for g, w in zip(gath, ws)]

    wdw_loc = _pad_rows(conv_w_dw[0], HALO, 0)
    small = jnp.concatenate([c.reshape(-1), conv_b_in.reshape(-1), wdw_loc.reshape(-1), conv_b_dw.reshape(-1),
                             conv_ln_g.reshape(-1), conv_ln_b.reshape(-1), conv_b_out.reshape(-1)])
    n_small = small.shape[0]
    w_small = -(-n_small // (8 * LANE)) * LANE
    small = jnp.pad(small, (0, 8 * w_small - n_small)).reshape(8, w_small)
    first = [small] + my_halves([conv_w_in[0]])
    small_all, cw_in = _exchange(_Gather8(first), first, "ag_small_params_w_conv")
    small_all = small_all.reshape(8, 8 * w_small)
    cw_in, = whole([cw_in], [conv_w_in[0]])
    c_all = small_all[:, :d]
    per_chip = small_all[0::2]
    dq_ = d // 4
    o1 = d
    b_in_full = per_chip[:, o1:o1 + 2 * dq_].reshape(4, 1, 2 * dq_)
    o1 += 2 * dq_
    wdw_full = per_chip[:, o1:o1 + HALO * dq_].reshape(4, HALO, dq_).transpose(1, 0, 2).reshape(HALO, d)
    o1 += HALO * dq_
    bdw_full = per_chip[:, o1:o1 + dq_].reshape(1, d)
    lng_full = per_chip[:, o1 + dq_:o1 + 2 * dq_].reshape(1, d)
    lnb_full = per_chip[:, o1 + 2 * dq_:o1 + 3 * dq_].reshape(1, d)
    bout_full = per_chip[:, o1 + 3 * dq_:o1 + 4 * dq_].reshape(1, d)

    a_mix = _ada_fwd(c_all, mix_ada_w, "ada_mix")
    a_ffn = _ada_fwd(c_all, ffn_ada_w, "ada_ffn")
    a_kv = _ada_fwd(c_all, kv_ada_w[None], "ada_kv")
    n3 = mix_ada_w.shape[2]
    n2 = kv_ada_w.shape[1]
    ada_loc = jnp.concatenate([a_mix[0], a_mix[1], a_ffn[0], a_ffn[1], a_kv[0]], axis=1)
    w_ada = ada_loc.shape[1]
    ada_all = _allgather8(ada_loc, "ag_ada", True).reshape(8, 8, w_ada)
    ada_me = lax.dynamic_index_in_dim(ada_all, dev, axis=1, keepdims=False)[0::2]

    def ada_vec(off, n, bias):
        return ada_me[:, off:off + n].reshape(1, 4 * n) + bias.reshape(1, -1)

    ada_m0 = ada_vec(0, n3, mix_ada_b[0])
    ada_m1 = ada_vec(n3, n3, mix_ada_b[1])
    ada_f0 = ada_vec(2 * n3, n3, ffn_ada_b[0])
    ada_f1 = ada_vec(3 * n3, n3, ffn_ada_b[1])
    ada_k = ada_vec(4 * n3, n2, kv_ada_b)

    def split3(a):
        return a[:, :d], a[:, d:2 * d], a[:, 2 * d:3 * d]

    sh_m0, sc_m0, gt_m0 = split3(ada_m0)
    sh_m1, sc_m1, gt_m1 = split3(ada_m1)
    sh_f0, sc_f0, gt_f0 = split3(ada_f0)
    sh_f1, sc_f1, gt_f1 = split3(ada_f1)
    sh_k, sc_k = ada_k[:, :d], ada_k[:, d:2 * d]

    grp_a = [ffn_w_in[0]]
    grp_b = [ffn_w_out[0], conv_w_out[0], ffn_w_in[1]]
    grp_c = [ffn_w_out[1], kv_w, attn_w_q[0], attn_w_o[0]]
    mine_a, mine_b, mine_c = my_halves(grp_a), my_halves(grp_b), my_halves(grp_c)

    zero_b = jnp.zeros((1, d), F32)
    g_m0, g_m1 = _vec(mix_norm_g[0]), _vec(mix_norm_g[1])
    g_f0, g_f1 = _vec(ffn_norm_g[0]), _vec(ffn_norm_g[1])
    g_k, g_fin = _vec(kv_norm_g), _vec(final_norm_g)
    fb = jnp.pad(forget_b, (0, LANE - N_HEADS)).reshape(1, LANE)

    h0, glu, a_s, g_s, gath_a = _in_pair(x0, g_m0, sh_m0, sc_m0, cw_in, b_in_full, True, "conv_in",
                                         comm=(_Gather8(mine_a), mine_a))
    dwo, sw, gath_b = _dwconv_fwd(glu, wdw_full, bdw_full, lng_full, lnb_full, comm=(_Gather8(mine_b), mine_b))
    w_in0, = whole(gath_a, grp_a)
    w_out0, cw_out, w_in1 = whole(gath_b, grp_b)
    cw_out = cw_out.reshape(d, d)
    w_in = [w_in0, w_in1]
    x1 = _mm_res(sw, cw_out, bout_full, gt_m0, x0, "conv_out")
    hf0, act0, ug0, uu0, gath_c = _in_pair(x1, g_f0, sh_f0, sc_f0, w_in[0], None, False, "ffn0_in",
                                           comm=(_Gather8(mine_c), mine_c))
    w_out1, kvw, wq, wo = whole(gath_c, grp_c)
    w_out = [w_out0.reshape(f, d), w_out1.reshape(f, d)]
    kvw = kvw.transpose(1, 0, 2).reshape(d, nkv_all)
    wk, wv = kvw[:, :d], kvw[:, d:2 * d]
    wf = jnp.pad(kvw[:, 2 * d:], ((0, 0), (0, LANE - N_HEADS)))
    wq, wo = wq.reshape(d, d), wo.reshape(d, d)
    x2 = _mm_res(act0, w_out[0], zero_b, gt_f0, x1, "ffn0_out")
    hk, h1, kk, vv, qq, flog = _qkv(x2, (g_k, sh_k, sc_k), (g_m1, sh_m1, sc_m1), wk, wv, wf, wq)
    cum, cumt = _cumsum_fwd(flog, fb)
    kaug, kaugt, vtr = _attn_prep(kk, vv, cum)
    o, lse = _attn_fwd(qq, kaug, vtr, cumt)
    x3 = _mm_res(o, wo, zero_b, gt_m1, x2, "attn_out")
    hf1, act1, ug1, uu1 = _in_pair(x3, g_f1, sh_f1, sc_f1, w_in[1], None, False, "ffn1_in")
    lsum, dx4, d_gfin = _mm_res_final(act1, w_out[1], gt_f1, x3, g_fin, loss_target[0])
    loss = lax.psum(0.5 / d * jnp.sum(lsum), ("x", "y", "c"))

    nf = f // 2

    sel = jnp.stack([ci, chip]).astype(jnp.int32)

    def reduce_begin(gs, tag):
        ps = [g.reshape(4, 2, g.shape[1] // 2, g.shape[2]) for g in gs]
        lands = _exchange(_SwapHalves(ps), ps, tag + "_swap")
        pairs = [_add_halves(p_, l_, sel, f"{tag}_add{k}") for k, (p_, l_) in enumerate(zip(ps, lands))]
        return [q for q, _ in pairs], [o_ for _, o_ in pairs]

    def reduce_sum(owns, lands, tag):
        return [_add_chips(o_, l_, sel, f"{tag}_sum{k}") for k, (o_, l_) in enumerate(zip(owns, lands))]

    def ffn_bwd(dx_out, x_in, hf, act, ug, uu, gain, scale, gate, w_in_l, w_out_l, tag, comm=None):
        res = _ffn_bwd_act(dx_out, gate, w_out_l, ug, uu, tag + "_bwd_act", comm=comm)
        dug, duu = res[0], res[1]
        dw_out, dgate = _dw_mm(act, [dx_out], nf, d, tag + "_dw_out", gate=gate, wfull=w_out_l, dgate_init=zero_b)
        terms = [(dug, 0, w_in_l, 0), (dug, nf, w_in_l, 1), (duu, 0, w_in_l, 2), (duu, nf, w_in_l, 3)]
        dx_in, dsh, dsc, dgn = _mm_normbwd(terms, x_in, dx_out, gain, scale, tag + "_bwd_in")
        dw_in = _dw_mm(hf, [dug, duu], d, nf, tag + "_dw_in")
        return dx_in, dw_in, dw_out[0], dsh, dsc, dgate, dgn, (res[2] if comm is not None else None)

    dx3, dw_in1, dw_out1, dsh_f1, dsc_f1, dgt_f1, dgn_f1, _ = ffn_bwd(dx4, x3, hf1, act1, ug1, uu1, g_f1, sc_f1, gt_f1, w_in[1], w_out[1], "ffn1")
    q16_1, own_1 = reduce_begin([dw_in1, dw_out1.reshape(4, f // 4, d)], "rs_ffn1")

    do, deltat = _do_kernel(dx3, gt_m1, wo, o)
    dwo_att, dgt_m1 = _dw_mm(o, [dx3], d, d, "attn_dw_o", gate=gt_m1, wfull=wo, dgate_init=zero_b)
    dq, dk, dv, dcq, dck, land_1 = _attn_bwd(qq, do, kaug, kaugt, vv, cumt, lse, deltat, comm=(_ScatterChips(q16_1), q16_1))
    dwq = _dw_mm(h1, [dq], d, d, "attn_dw_q")[0]

    df, dfb = _cumsum_bwd(dcq[:, :2].reshape(N_HEADS, s), dck[:, :2].reshape(N_HEADS, s), flog, fb)
    terms = [(dk, 0, wk.reshape(1, d, d), 0), (dv, 0, wv.reshape(1, d, d), 0), (df, 0, wf.reshape(1, d, LANE), 0)]
    dx2, dsh_m1, dsc_m1, dgn_m1, dsh_k, dsc_k, dgn_k = _mm_normbwd(
        [(dq, 0, wq.reshape(1, d, d), 0)], x2, dx3, g_m1, sc_m1, "attn_kv_bwd", second=(terms, g_k, sc_k))
    dwk, dwv = _dw_mm(hk, [dk, dv], d, d, "kv_dw_kv")
    dwf = _dw_mm(hk, [df], d, LANE, "kv_dw_f")[0]
    dkvw = jnp.concatenate([dwk, dwv, dwf[:, :N_HEADS]], axis=1)
    dkvw = dkvw.reshape(d, 4, nkv).transpose(1, 0, 2)

    q16_2, own_2 = reduce_begin([dkvw, dwq.reshape(4, d // 4, d), dwo_att[0].reshape(4, d // 4, d)], "rs_attn")
    dx1, dw_in0, dw_out0, dsh_f0, dsc_f0, dgt_f0, dgn_f0, land_2 = ffn_bwd(
        dx2, x1, hf0, act0, ug0, uu0, g_f0, sc_f0, gt_f0, w_in[0], w_out[0], "ffn0", comm=(_ScatterChips(q16_2), q16_2))

    ddwo, d_lng, d_lnb, d_bdw, d_bout, dgt_extra = _conv_bwd1(dx1, gt_m0, cw_out, bout_full, dwo, lng_full, lnb_full)
    dcw_out, dgt_m0 = _dw_mm(sw, [dx1], d, d, "conv_dw_out", gate=gt_m0, wfull=cw_out, dgate_init=dgt_extra)
    q16_3, own_3 = reduce_begin([dw_in0, dw_out0.reshape(4, f // 4, d), dcw_out[0].reshape(4, d // 4, d)], "rs_ffn0")
    da, dg, d_wdw, d_bin_a, d_bin_g, land_3 = _dwconv_bwd(ddwo, glu, a_s, g_s, wdw_full, comm=(_ScatterChips(q16_3), q16_3))
    nc = cw_in.shape[2]
    terms = [(da, 0, cw_in, 0), (da, nc, cw_in, 1), (dg, 0, cw_in, 2), (dg, nc, cw_in, 3)]
    dx0, dsh_m0, dsc_m0, dgn_m0 = _mm_normbwd(terms, x0, dx1, g_m0, sc_m0, "conv_bwd_in")
    dcw_in = _dw_mm(h0, [da, dg], d, nc, "conv_dw_in")
    q16_4, own_4 = reduce_begin([dcw_in], "rs_conv")
    land_4 = _exchange(_ScatterChips(q16_4), q16_4, "rs_conv_scatter")

    sums = (reduce_sum(own_1, land_1, "rs_ffn1") + reduce_sum(own_2, land_2, "rs_attn")
            + reduce_sum(own_3, land_3, "rs_ffn0") + reduce_sum(own_4, land_4, "rs_conv"))
    reduced = [b.reshape(2 * b.shape[1], b.shape[2]) for b in _share_halves(sums)]
    g_w_in1, g_w_out1, g_kvw, g_wq, g_wo, g_w_in0, g_w_out0, g_cw_out, g_cw_in = reduced

    d_ada = [jnp.concatenate([dsh_m0, dsc_m0, dgt_m0], axis=1), jnp.concatenate([dsh_m1, dsc_m1, dgt_m1], axis=1),
             jnp.concatenate([dsh_f0, dsc_f0, dgt_f0], axis=1), jnp.concatenate([dsh_f1, dsc_f1, dgt_f1], axis=1),
             jnp.concatenate([dsh_k, dsc_k], axis=1)]
    fields = d_ada + [dgn_m0, dgn_m1, dgn_f0, dgn_f1, dgn_k, d_gfin, d_bin_a, d_bin_g, d_bdw, d_lng, d_lnb, d_bout,
                      d_wdw.reshape(1, -1), dfb]
    foffs = [0]
    for fl in fields:
        foffs.append(foffs[-1] + fl.shape[1])
    n_row = foffs[-1]
    w_row = -(-n_row // (8 * LANE)) * LANE
    row = jnp.pad(jnp.concatenate(fields, axis=1), ((0, 0), (0, 8 * w_row - n_row))).reshape(8, w_row)
    rows_all = _allgather8(row, "ag_small_grads", True).reshape(8, 8, w_row)
    rsum_small = _sum8(rows_all).reshape(1, 8 * w_row)
    rows_flat = rows_all.reshape(8, 8 * w_row)

    def fsum(i):
        return rsum_small[:, foffs[i]:foffs[i + 1]]

    cat = _silu_rows(c_all).T

    def ada_cols(i, n):
        full = rows_flat[:, foffs[i]:foffs[i + 1]].reshape(8, 4, n)
        return lax.dynamic_index_in_dim(full, chip, axis=1, keepdims=False)

    g_mix_ada_w = _ada_wgrad(cat, jnp.stack([ada_cols(0, n3), ada_cols(1, n3)]), "ada_mix_wgrad")
    g_ffn_ada_w = _ada_wgrad(cat, jnp.stack([ada_cols(2, n3), ada_cols(3, n3)]), "ada_ffn_wgrad")
    g_kv_ada_w = _ada_wgrad(cat, ada_cols(4, n2)[None], "ada_kv_wgrad")[0]

    def my_cols(v, n):
        return lax.dynamic_index_in_dim(v.reshape(4, n), chip, axis=0, keepdims=False)

    grads = {
        "mix_norm_g": jnp.concatenate([fsum(5), fsum(6)], axis=0),
        "mix_ada_w": g_mix_ada_w,
        "mix_ada_b": jnp.concatenate([fsum(0), fsum(1)], axis=0),
        "ffn_norm_g": jnp.concatenate([fsum(7), fsum(8)], axis=0),
        "ffn_ada_w": g_ffn_ada_w,
        "ffn_ada_b": jnp.concatenate([fsum(2), fsum(3)], axis=0),
        "ffn_w_in": jnp.stack([g_w_in0, g_w_in1]),
        "ffn_w_out": jnp.stack([g_w_out0, g_w_out1]),
        "conv_w_in": g_cw_in[None],
        "conv_b_in": my_cols(jnp.concatenate([fsum(11), fsum(12)], axis=1), 2 * dq_)[None],
        "conv_w_dw": lax.dynamic_index_in_dim(fsum(17).reshape(HALO, 4, dq_), chip, axis=1, keepdims=False)[:CONV_K][None],
        "conv_b_dw": my_cols(fsum(13), dq_)[None],
        "conv_ln_g": my_cols(fsum(14), dq_)[None],
        "conv_ln_b": my_cols(fsum(15), dq_)[None],
        "conv_w_out": g_cw_out[None],
        "conv_b_out": my_cols(fsum(16), dq_)[None],
        "kv_norm_g": fsum(9).reshape(-1),
        "kv_ada_w": g_kv_ada_w,
        "kv_ada_b": fsum(4).reshape(-1),
        "kv_w": g_kvw,
        "forget_b": fsum(18).reshape(-1)[:N_HEADS],
        "attn_w_q": g_wq[None],
        "attn_w_o": g_wo[None],
        "final_norm_g": fsum(10).reshape(-1),
    }
    weights = dict(mix_norm_g=mix_norm_g, mix_ada_w=mix_ada_w, mix_ada_b=mix_ada_b, ffn_norm_g=ffn_norm_g, ffn_ada_w=ffn_ada_w, ffn_ada_b=ffn_ada_b, ffn_w_in=ffn_w_in, ffn_w_out=ffn_w_out, conv_w_in=conv_w_in, conv_b_in=conv_b_in, conv_w_dw=conv_w_dw, conv_b_dw=conv_b_dw, conv_ln_g=conv_ln_g, conv_ln_b=conv_ln_b, conv_w_out=conv_w_out, conv_b_out=conv_b_out, kv_norm_g=kv_norm_g, kv_ada_w=kv_ada_w, kv_ada_b=kv_ada_b, kv_w=kv_w, forget_b=forget_b, attn_w_q=attn_w_q, attn_w_o=attn_w_o, final_norm_g=final_norm_g)
    moms = dict(mix_norm_g=(m_mix_norm_g, v_mix_norm_g), mix_ada_w=(m_mix_ada_w, v_mix_ada_w), mix_ada_b=(m_mix_ada_b, v_mix_ada_b), ffn_norm_g=(m_ffn_norm_g, v_ffn_norm_g), ffn_ada_w=(m_ffn_ada_w, v_ffn_ada_w), ffn_ada_b=(m_ffn_ada_b, v_ffn_ada_b), ffn_w_in=(m_ffn_w_in, v_ffn_w_in), ffn_w_out=(m_ffn_w_out, v_ffn_w_out), conv_w_in=(m_conv_w_in, v_conv_w_in), conv_b_in=(m_conv_b_in, v_conv_b_in), conv_w_dw=(m_conv_w_dw, v_conv_w_dw), conv_b_dw=(m_conv_b_dw, v_conv_b_dw), conv_ln_g=(m_conv_ln_g, v_conv_ln_g), conv_ln_b=(m_conv_ln_b, v_conv_ln_b), conv_w_out=(m_conv_w_out, v_conv_w_out), conv_b_out=(m_conv_b_out, v_conv_b_out), kv_norm_g=(m_kv_norm_g, v_kv_norm_g), kv_ada_w=(m_kv_ada_w, v_kv_ada_w), kv_ada_b=(m_kv_ada_b, v_kv_ada_b), kv_w=(m_kv_w, v_kv_w), forget_b=(m_forget_b, v_forget_b), attn_w_q=(m_attn_w_q, v_attn_w_q), attn_w_o=(m_attn_w_o, v_attn_w_o), final_norm_g=(m_final_norm_g, v_final_norm_g))
    names = list(weights)

    deltas, new_m, new_v = {}, {}, {}
    small_names = [n for n in names if weights[n].size < (1 << 16)]
    for n in names:
        if n in small_names:
            continue
        w = weights[n]
        w2 = w.reshape(-1, w.shape[-1])
        dl, nm, nv = _adamw(w2, grads[n].reshape(w2.shape), moms[n][0].reshape(w2.shape), moms[n][1].reshape(w2.shape), "adamw_" + n)
        deltas[n], new_m[n], new_v[n] = dl.reshape(w.shape), nm.reshape(w.shape), nv.reshape(w.shape)

    def pack_small(get):
        flat = jnp.concatenate([get(n).reshape(-1) for n in small_names])
        rows_ = -(-flat.shape[0] // (8 * LANE)) * 8
        return jnp.pad(flat, (0, rows_ * LANE - flat.shape[0])).reshape(rows_, LANE)

    ws, gs = pack_small(lambda n: weights[n]), pack_small(lambda n: grads[n])
    ms_, vs_ = pack_small(lambda n: moms[n][0]), pack_small(lambda n: moms[n][1])
    vs_ = jnp.where(jnp.arange(vs_.size).reshape(vs_.shape) < sum(weights[n].size for n in small_names), vs_, 1.0)
    dl, nm, nv = _adamw(ws, gs, ms_, vs_, "adamw_small")
    off = 0
    for n in small_names:
        sz = weights[n].size
        shp = weights[n].shape
        deltas[n] = dl.reshape(-1)[off:off + sz].reshape(shp)
        new_m[n] = nm.reshape(-1)[off:off + sz].reshape(shp)
        new_v[n] = nv.reshape(-1)[off:off + sz].reshape(shp)
        off += sz

    grad_out = [grads[n].reshape(weights[n].shape) for n in names]
    return (loss, dx0[None], *grad_out, *[deltas[n] for n in names], *[new_m[n] for n in names], *[new_v[n] for n in names])
```

```python
import functools

import jax
import jax.numpy as jnp
from jax import lax
from jax.experimental import pallas as pl
from jax.experimental.pallas import tpu as pltpu

F32 = jnp.float32
BF16 = jnp.bfloat16
MESH = pl.DeviceIdType.MESH

EPS = 1e-6
N_HEADS = 16
HEAD_DIM = 64
CONV_K = 31
LANE = 128
SUBLANES = 8
HALO = 32
ATT_FWD_TQ = 2048
ATT_TQ = 1024
ATT_TK = 512
NPIECE = 3
SPARE = (HEAD_DIM, 0)
VMEM_MB = 48
ATT_BWD_VMEM_MB = 56

ADAM_LR = 0.001
ADAM_B1 = 0.9
ADAM_B2 = 0.999
ADAM_EPS = 1e-08
ADAM_WD = 0.01
ADAM_STEP = 10


def _sds(shape, dtype):
    return jax.ShapeDtypeStruct(tuple(shape), dtype)


def _cp(sem=None, vmem_mb=VMEM_MB):
    return pltpu.CompilerParams(dimension_semantics=sem, vmem_limit_bytes=vmem_mb << 20)


def _tile(n, pref):
    return pref if n % pref == 0 else n


def _row_tile(r, mult, width=1024):
    cap = max(mult, (512 * 1024 // width) // mult * mult)
    for cand in range(cap, mult - 1, -mult):
        if r % cand == 0:
            return cand
    return r


def _resident(shape):
    nd = len(shape)
    return pl.BlockSpec(tuple(shape), lambda *_: (0,) * nd, pipeline_mode=pl.Buffered(1))


def _dot(a, b):
    return jnp.dot(a, b, preferred_element_type=F32)


def _dot_nt(a, b):
    return lax.dot_general(a, b, (((1,), (1,)), ((), ())), preferred_element_type=F32)


def _dot_tn(a, b):
    return lax.dot_general(a, b, (((0,), (0,)), ((), ())), preferred_element_type=F32)


def _sigmoid(x):
    return 1.0 / (1.0 + jnp.exp(-x))


def _colsum(x):
    return jnp.sum(x, axis=0, keepdims=True)


def _rms_parts(x):
    rstd = lax.rsqrt(jnp.mean(x * x, axis=-1, keepdims=True) + EPS)
    return x * rstd, rstd


class _Gather8:
    def __init__(self, xs):
        self.n = len(xs)
        self.m = [x.shape[0] for x in xs]
        self.land = [_sds((8 * x.shape[0],) + tuple(x.shape[1:]), x.dtype) for x in xs]
        self.sems = [pltpu.SemaphoreType.DMA((7 * self.n,)), pltpu.SemaphoreType.DMA((7 * self.n,)),
                     pltpu.SemaphoreType.DMA((self.n,))]

    def _parts(self, a, x_refs, out_refs, send_sems, recv_sems, local_sems):
        x, y, c = lax.axis_index("x"), lax.axis_index("y"), lax.axis_index("c")
        me, sibling = (x, y, c), (x, y, 1 - c)
        chips = [(1 - x, y), (x, 1 - y), (1 - x, 1 - y)]
        m_per, x_ref, out_ref = self.m[a], x_refs[a], out_refs[a]

        def rows(px, py, pc):
            return out_ref.at[pl.ds((4 * px + 2 * py + pc) * m_per, m_per)]

        def copy(k, block, to, src=None):
            return pltpu.make_async_remote_copy(
                src_ref=rows(*block) if src is None else src, dst_ref=rows(*block),
                send_sem=send_sems.at[7 * a + k], recv_sem=recv_sems.at[7 * a + k], device_id=to, device_id_type=MESH)

        mine = pltpu.make_async_copy(x_ref, rows(*me), local_sems.at[a])
        first = [copy(0, me, sibling, src=x_ref)]
        first += [copy(1 + j, me, (*chip, c), src=x_ref) for j, chip in enumerate(chips)]
        passed = [copy(4 + j, (*chip, c), sibling) for j, chip in enumerate(chips)]
        return c, me, sibling, chips, copy, mine, first, passed

    def start(self, *refs):
        for a in range(self.n):
            _, _, _, _, _, mine, first, _ = self._parts(a, *refs)
            mine.start()
            for cp in first:
                cp.start()

    def finish(self, *refs):
        parts = [self._parts(a, *refs) for a in range(self.n)]
        for j in range(3):
            for c, me, sibling, chips, copy, mine, first, passed in parts:
                copy(1 + j, (*chips[j], c), me).wait_recv()
                passed[j].start()
        for c, me, sibling, chips, copy, mine, first, passed in parts:
            copy(0, sibling, me).wait_recv()
            for j, chip in enumerate(chips):
                copy(4 + j, (*chip, 1 - c), me).wait_recv()
            for cp in first + passed:
                cp.wait_send()
            mine.wait()


class _ScatterChips:
    def __init__(self, qs):
        self.n = len(qs)
        self.land = [_sds((3,) + tuple(q.shape[1:]), q.dtype) for q in qs]
        self.sems = [pltpu.SemaphoreType.DMA((3 * self.n,)), pltpu.SemaphoreType.DMA((3 * self.n,))]

    def _copies(self, q_refs, land_refs, send_sems, recv_sems):
        x, y, c = lax.axis_index("x"), lax.axis_index("y"), lax.axis_index("c")
        chips = [(1 - x, y), (x, 1 - y), (1 - x, 1 - y)]
        return [pltpu.make_async_remote_copy(
            src_ref=q_refs[a].at[2 * cx + cy], dst_ref=land_refs[a].at[k],
            send_sem=send_sems.at[3 * a + k], recv_sem=recv_sems.at[3 * a + k],
            device_id=(cx, cy, c), device_id_type=MESH) for a in range(self.n) for k, (cx, cy) in enumerate(chips)]

    def start(self, *refs):
        for cp in self._copies(*refs):
            cp.start()

    def finish(self, *refs):
        copies = self._copies(*refs)
        for cp in copies:
            cp.wait_recv()
        for cp in copies:
            cp.wait_send()


class _SwapHalves:
    def __init__(self, ps):
        self.n = len(ps)
        self.land = [_sds((p.shape[0],) + tuple(p.shape[2:]), p.dtype) for p in ps]
        self.nb = [p.shape[0] for p in ps]
        tot = sum(self.nb)
        self.sems = [pltpu.SemaphoreType.DMA((tot,)), pltpu.SemaphoreType.DMA((tot,))]

    def _copies(self, p_refs, land_refs, send_sems, recv_sems):
        x, y, c = lax.axis_index("x"), lax.axis_index("y"), lax.axis_index("c")
        out, k = [], 0
        for a in range(self.n):
            for j in range(self.nb[a]):
                out.append(pltpu.make_async_remote_copy(
                    src_ref=p_refs[a].at[j, 1 - c], dst_ref=land_refs[a].at[j], send_sem=send_sems.at[k],
                    recv_sem=recv_sems.at[k], device_id=(x, y, 1 - c), device_id_type=MESH))
                k += 1
        return out

    start = _ScatterChips.start
    finish = _ScatterChips.finish


def _hosted_call(body, comm, *, grid, in_specs, out_specs, out_shape, scratch_shapes, name, sem, args, vmem_mb=VMEM_MB):
    def first():
        return functools.reduce(jnp.logical_and, [pl.program_id(a) == 0 for a in range(len(grid))])

    def last():
        return functools.reduce(jnp.logical_and, [pl.program_id(a) == g - 1 for a, g in enumerate(grid)])

    out_specs = tuple(out_specs) if isinstance(out_specs, (tuple, list)) else (out_specs,)
    out_shape = tuple(out_shape) if isinstance(out_shape, (tuple, list)) else (out_shape,)
    if comm is None:
        return pl.pallas_call(body, grid=grid, in_specs=list(in_specs), out_specs=out_specs, out_shape=out_shape,
                              scratch_shapes=list(scratch_shapes), name=name, compiler_params=_cp(sem, vmem_mb))(*args)
    ex, srcs = comm
    n_in, n_out, n_scr, n_ex = len(in_specs), len(out_shape), len(scratch_shapes), ex.n

    def wrapped(*refs):
        ins, src_refs = refs[:n_in], refs[n_in:n_in + n_ex]
        o0 = n_in + n_ex
        outs, land_refs = refs[o0:o0 + n_out], refs[o0 + n_out:o0 + n_out + n_ex]
        s0 = o0 + n_out + n_ex
        scr, sems = refs[s0:s0 + n_scr], refs[s0 + n_scr:]

        @pl.when(first())
        def _():
            ex.start(src_refs, land_refs, *sems)

        body(*ins, *outs, *scr)

        @pl.when(last())
        def _():
            ex.finish(src_refs, land_refs, *sems)

    hbm = pl.BlockSpec(memory_space=pl.ANY)
    res = pl.pallas_call(
        wrapped, grid=grid, in_specs=[*in_specs, *[hbm] * n_ex], out_specs=(*out_specs, *[hbm] * n_ex),
        out_shape=(*out_shape, *ex.land), scratch_shapes=[*scratch_shapes, *ex.sems], name=name,
        compiler_params=_cp(tuple("arbitrary" for _ in grid), vmem_mb))(*args, *srcs)
    return (*res[:n_out], list(res[n_out:]))


def _exchange(ex, srcs, name, in_vmem=False):
    n = ex.n

    def body(*refs):
        src_refs, land_refs, sems = refs[:n], refs[n:2 * n], refs[2 * n:]
        ex.start(src_refs, land_refs, *sems)
        ex.finish(src_refs, land_refs, *sems)

    spec = pl.BlockSpec(memory_space=pltpu.VMEM if in_vmem else pl.ANY)
    return list(pl.pallas_call(
        body, out_shape=tuple(ex.land), in_specs=[spec] * n, out_specs=tuple([spec] * n),
        scratch_shapes=ex.sems, name=name)(*srcs))


def _allgather8(x_shard, name, in_vmem):
    return _exchange(_Gather8([x_shard]), [x_shard], name, in_vmem)[0]


def _share_halves(bufs):
    n = len(bufs)

    def body(*refs):
        b_refs, out_refs, send_sems, recv_sems = refs[:n], refs[n:2 * n], refs[2 * n], refs[2 * n + 1]
        x, y, c = lax.axis_index("x"), lax.axis_index("y"), lax.axis_index("c")
        copies = [pltpu.make_async_remote_copy(
            src_ref=b_refs[k].at[c], dst_ref=out_refs[k].at[c], send_sem=send_sems.at[k], recv_sem=recv_sems.at[k],
            device_id=(x, y, 1 - c), device_id_type=MESH) for k in range(n)]
        for cp in copies:
            cp.start()
        for cp in copies:
            cp.wait_recv()
        for cp in copies:
            cp.wait_send()

    hbm = pl.BlockSpec(memory_space=pl.ANY)
    return pl.pallas_call(
        body, out_shape=tuple(_sds(b.shape, b.dtype) for b in bufs), in_specs=[hbm] * n, out_specs=tuple([hbm] * n),
        scratch_shapes=[pltpu.SemaphoreType.DMA((n,)), pltpu.SemaphoreType.DMA((n,))],
        input_output_aliases={k: k for k in range(n)}, name="rs_share_halves")(*bufs)


def _add_halves(p, land, sel, name):
    nb, _, r, w = p.shape
    tr = _row_tile(r, 16, w)

    def body(sel_ref, p_ref, l_ref, q16_ref, own_ref):
        q = p_ref[0, 0] + l_ref[0]
        q16_ref[0] = q.astype(BF16)

        @pl.when(pl.program_id(1) == sel_ref[1])
        def _():
            own_ref[...] = q

    gs = pltpu.PrefetchScalarGridSpec(
        num_scalar_prefetch=1, grid=(r // tr, nb),
        in_specs=[pl.BlockSpec((1, 1, tr, w), lambda i, j, sl: (j, sl[0], i, 0)),
                  pl.BlockSpec((1, tr, w), lambda i, j, sl: (j, i, 0))],
        out_specs=(pl.BlockSpec((1, tr, w), lambda i, j, sl: (j, i, 0)), pl.BlockSpec((tr, w), lambda i, j, sl: (i, 0))))
    return pl.pallas_call(body, grid_spec=gs, out_shape=(_sds((nb, r, w), BF16), _sds((r, w), F32)), name=name,
                          compiler_params=_cp(("parallel", "arbitrary")))(sel, p, land)


def _add_chips(own, land, sel, name):
    r, w = own.shape
    tr = _row_tile(r, 16, w)

    def body(sel_ref, q_ref, l_ref, o_ref):
        o_ref[0] = ((q_ref[...] + l_ref[0].astype(F32)) + l_ref[1].astype(F32)) + l_ref[2].astype(F32)

    gs = pltpu.PrefetchScalarGridSpec(
        num_scalar_prefetch=1, grid=(r // tr,),
        in_specs=[pl.BlockSpec((tr, w), lambda i, sl: (i, 0)), pl.BlockSpec((3, tr, w), lambda i, sl: (0, i, 0))],
        out_specs=pl.BlockSpec((1, tr, w), lambda i, sl: (sl[0], i, 0)))
    return pl.pallas_call(body, grid_spec=gs, out_shape=_sds((2, r, w), F32), name=name,
                          compiler_params=_cp(("parallel",)))(sel, own, land)


def _sum8(g):
    _, m, n = g.shape

    def body(g_ref, o_ref):
        acc = g_ref[0]
        for k in range(1, 8):
            acc = acc + g_ref[k]
        o_ref[...] = acc

    return pl.pallas_call(body, out_shape=_sds((m, n), g.dtype), name="sum8")(g)


def _ada_fwd(c_all, w3, name):
    nl, d, n = w3.shape
    tn = 256

    def body(c_ref, w_ref, o_ref):
        cc = c_ref[...]
        ca = (cc * _sigmoid(cc)).astype(BF16)
        o_ref[0] = _dot(ca, w_ref[0].astype(BF16))

    return pl.pallas_call(
        body, grid=(nl, n // tn), out_shape=_sds((nl, 8, n), F32),
        in_specs=[pl.BlockSpec((8, d), lambda l, j: (0, 0)), pl.BlockSpec((1, d, tn), lambda l, j: (l, 0, j))],
        out_specs=pl.BlockSpec((1, 8, tn), lambda l, j: (l, 0, j)),
        name=name, compiler_params=_cp(("parallel", "parallel")))(c_all, w3)


def _in_pair(x, gain, shift, scale, wg, bias, conv, name, comm=None):
    s, d = x.shape
    n = wg.shape[2]
    ts = _tile(s, 512)

    def body(*refs):
        if conv:
            x_ref, g_ref, sh_ref, sc_ref, w_ref, b_ref, h_ref, o_ref, sa_ref, sb_ref = refs
        else:
            x_ref, g_ref, sh_ref, sc_ref, w_ref, h_ref, o_ref, sa_ref, sb_ref = refs
        xhat, _ = _rms_parts(x_ref[...])
        h = ((xhat * g_ref[...]) * (1.0 + sc_ref[...]) + sh_ref[...]).astype(BF16)
        h_ref[...] = h
        for q in range(2):
            a = _dot(h, w_ref[q])
            b = _dot(h, w_ref[q + 2])
            cs = pl.ds(q * n, n)
            if conv:
                a = a + b_ref[q]
                b = b + b_ref[q + 2]
                o_ref[:, cs] = a * _sigmoid(b)
            else:
                o_ref[:, cs] = (a * _sigmoid(a) * b).astype(BF16)
            sa_ref[:, cs] = a.astype(BF16)
            sb_ref[:, cs] = b.astype(BF16)

    vec = pl.BlockSpec((1, d), lambda i: (0, 0))
    in_specs = [pl.BlockSpec((ts, d), lambda i: (i, 0)), vec, vec, vec, _resident(wg.shape)]
    args = [x, gain, shift, scale, wg]
    if conv:
        in_specs.append(_resident(bias.shape))
        args.append(bias)
    tile = pl.BlockSpec((ts, 2 * n), lambda i: (i, 0))
    return _hosted_call(
        body, comm, grid=(s // ts,),
        out_shape=(_sds((s, d), BF16), _sds((s, 2 * n), F32 if conv else BF16), _sds((s, 2 * n), BF16), _sds((s, 2 * n), BF16)),
        in_specs=in_specs, out_specs=(pl.BlockSpec((ts, d), lambda i: (i, 0)), tile, tile, tile),
        scratch_shapes=[], name=name, sem=("parallel",), args=args)


def _shift_copies(buf, shf):
    n = shf.shape[1]
    for r in range(1, SUBLANES):
        shf[r - 1, :, :] = buf[pl.ds(r, n), :]


def _shifted(buf, shf, start, n, cs):
    a, r = divmod(start, SUBLANES)
    if r == 0:
        return buf[pl.ds(start, n), cs]
    return shf[r - 1, pl.ds(a * SUBLANES, n), cs]


def _dwconv_fwd(glu, wdw, bdw, lng, lnb, comm=None):
    s, d = glu.shape
    ts = _tile(s, 256)
    rb, cb = 32, 256

    def body(cur_ref, halo_ref, w_ref, b_ref, g_ref, be_ref, dwo_ref, sw_ref, buf, shf):
        i = pl.program_id(0)

        @pl.when(i == 0)
        def _():
            buf[pl.ds(0, HALO), :] = jnp.zeros((HALO, d), F32)

        @pl.when(i > 0)
        def _():
            buf[pl.ds(0, HALO), :] = halo_ref[...]

        buf[pl.ds(HALO, ts), :] = cur_ref[...]
        _shift_copies(buf, shf)
        for r in range(ts // rb):
            for cc in range(d // cb):
                cs = pl.ds(cc * cb, cb)
                acc = jnp.zeros((rb, cb), F32) + b_ref[:, cs]
                for k in range(CONV_K):
                    acc = acc + w_ref[pl.ds(k, 1), cs] * _shifted(buf, shf, HALO - (CONV_K - 1) + k + r * rb, rb, cs)
                dwo_ref[pl.ds(r * rb, rb), cs] = acc
            rows = pl.ds(r * rb, rb)
            yv = dwo_ref[rows, :]
            mu = jnp.mean(yv, axis=-1, keepdims=True)
            yc = yv - mu
            var = jnp.mean(yc * yc, axis=-1, keepdims=True)
            ln = yc * lax.rsqrt(var + EPS) * g_ref[...] + be_ref[...]
            sw_ref[rows, :] = (ln * _sigmoid(ln)).astype(BF16)

    vec = pl.BlockSpec((1, d), lambda i: (0, 0))
    return _hosted_call(
        body, comm, grid=(s // ts,), out_shape=(_sds((s, d), F32), _sds((s, d), BF16)),
        in_specs=[pl.BlockSpec((ts, d), lambda i: (i, 0)),
                  pl.BlockSpec((HALO, d), lambda i: (jnp.maximum(i * (ts // HALO) - 1, 0), 0)),
                  pl.BlockSpec((HALO, d), lambda i: (0, 0)), vec, vec, vec],
        out_specs=(pl.BlockSpec((ts, d), lambda i: (i, 0)), pl.BlockSpec((ts, d), lambda i: (i, 0))),
        scratch_shapes=[pltpu.VMEM((HALO + ts, d), F32), pltpu.VMEM((SUBLANES - 1, HALO + ts - SUBLANES, d), F32)],
        name="dwconv_fwd", sem=("parallel",),
        args=(glu, glu, wdw, bdw, lng, lnb))


def _mm_res(a, w, b, gate, x, name):
    s, k = a.shape
    d = w.shape[1]
    ts = _tile(s, 512)

    def body(a_ref, w_ref, b_ref, g_ref, x_ref, o_ref):
        yv = _dot(a_ref[...], w_ref[...]) + b_ref[...]
        o_ref[...] = x_ref[...] + g_ref[...] * yv

    vec = pl.BlockSpec((1, d), lambda i: (0, 0))
    return pl.pallas_call(
        body, grid=(s // ts,), out_shape=_sds((s, d), F32),
        in_specs=[pl.BlockSpec((ts, k), lambda i: (i, 0)), _resident((k, d)), vec, vec, pl.BlockSpec((ts, d), lambda i: (i, 0))],
        out_specs=pl.BlockSpec((ts, d), lambda i: (i, 0)),
        name=name, compiler_params=_cp(("parallel",)))(a, w, b, gate, x)


def _qkv(x, kvp, mxp, wk, wv, wf, wq):
    s, d = x.shape
    ts = _tile(s, 512)
    qscale = HEAD_DIM ** -0.5

    def body(x_ref, gk, shk, sck, gm, shm, scm, wk_ref, wv_ref, wf_ref, wq_ref, hk_ref, h1_ref, k_ref, v_ref, q_ref, f_ref):
        xhat, _ = _rms_parts(x_ref[...])
        hk = ((xhat * gk[...]) * (1.0 + sck[...]) + shk[...]).astype(BF16)
        h1 = ((xhat * gm[...]) * (1.0 + scm[...]) + shm[...]).astype(BF16)
        hk_ref[...] = hk
        h1_ref[...] = h1
        k_ref[...] = _dot(hk, wk_ref[...]).astype(BF16)
        v_ref[...] = _dot(hk, wv_ref[...]).astype(BF16)
        f_ref[...] = _dot(hk, wf_ref[...])
        q_ref[...] = (_dot(h1, wq_ref[...]) * qscale).astype(BF16)

    vec = pl.BlockSpec((1, d), lambda i: (0, 0))
    row = pl.BlockSpec((ts, d), lambda i: (i, 0))
    return pl.pallas_call(
        body, grid=(s // ts,),
        out_shape=tuple(_sds((s, d), BF16) for _ in range(5)) + (_sds((s, LANE), F32),),
        in_specs=[row, vec, vec, vec, vec, vec, vec, _resident((d, d)), _resident((d, d)), _resident((d, LANE)), _resident((d, d))],
        out_specs=(row, row, row, row, row, pl.BlockSpec((ts, LANE), lambda i: (i, 0))),
        name="qkv_proj", compiler_params=_cp(("parallel",)))(x, *kvp, *mxp, wk, wv, wf, wq)


def _log_sigmoid(z):
    return jnp.minimum(z, 0.0) - jnp.log(1.0 + jnp.exp(-jnp.abs(z)))


def _cumsum_fwd(flog, fb):
    s = flog.shape[0]
    ts = _tile(s, 256)

    def body(f_ref, b_ref, cum_ref, cumt_ref, carry):
        @pl.when(pl.program_id(0) == 0)
        def _():
            carry[...] = jnp.zeros_like(carry)

        ls = _log_sigmoid(f_ref[...] + b_ref[...])
        r = lax.broadcasted_iota(jnp.int32, (ts, ts), 0)
        cidx = lax.broadcasted_iota(jnp.int32, (ts, ts), 1)
        tri = (cidx <= r).astype(F32)
        cs = jnp.dot(tri, ls, preferred_element_type=F32, precision=lax.Precision.HIGHEST) + carry[...]
        cum_ref[...] = cs
        cumt_ref[...] = cs.T
        carry[...] = cs[ts - 1:ts, :]

    return pl.pallas_call(
        body, grid=(s // ts,), out_shape=(_sds((s, LANE), F32), _sds((LANE, s), F32)),
        in_specs=[pl.BlockSpec((ts, LANE), lambda i: (i, 0)), pl.BlockSpec((1, LANE), lambda i: (0, 0))],
        out_specs=(pl.BlockSpec((ts, LANE), lambda i: (i, 0)), pl.BlockSpec((LANE, ts), lambda i: (0, i))),
        scratch_shapes=[pltpu.VMEM((1, LANE), F32)],
        name="forget_cumsum", compiler_params=_cp(("arbitrary",)))(flog, fb)


def _pick_row(m, idx):
    r = lax.broadcasted_iota(jnp.int32, (m.shape[0], 1), 0)
    return jnp.sum(jnp.where(r == idx, m, 0.0), axis=0, keepdims=True)


def _pick_col(m, idx):
    cidx = lax.broadcasted_iota(jnp.int32, (1, m.shape[1]), 1)
    return jnp.sum(jnp.where(cidx == idx, m, 0.0), axis=1, keepdims=True)


def _split3(x):
    hi = x.astype(BF16)
    r1 = x - hi.astype(F32)
    mid = r1.astype(BF16)
    lo = (r1 - mid.astype(F32)).astype(BF16)
    return hi, mid, lo


def _head_mask(lane, hh):
    lo = lane < HEAD_DIM
    return lo if hh == 0 else jnp.logical_not(lo)


def _attn_prep(k, v, cum):
    s, d = k.shape
    npair = d // LANE
    tc = _tile(s, 1024)

    def body(k_ref, v_ref, c_ref, ka_ref, kt_ref, vt_ref):
        p = pl.program_id(0)
        lane = lax.broadcasted_iota(jnp.int32, (1, LANE), 1)
        kk = k_ref[...]
        vv = v_ref[...].astype(F32)
        ckt = c_ref[...]
        for hh in range(2):
            head = _head_mask(lane, hh)
            b = SPARE[hh]
            ck = _pick_col(ckt, 2 * p + hh)
            extra = jnp.where(lane == b + NPIECE, 1.0, 0.0).astype(BF16) + jnp.zeros((tc, LANE), BF16)
            for n_, pc in enumerate(_split3(ck)):
                extra = jnp.where(lane == b + n_, pc, extra)
            ka = jnp.where(head, kk, extra)
            ka_ref[0, hh] = ka
            kt_ref[0, hh] = ka.astype(F32).T.astype(BF16)
            vx = jnp.where(head, vv, jnp.where(lane == b, 1.0, 0.0))
            vt_ref[0, hh] = vx.T.astype(BF16)

    blk = pl.BlockSpec((tc, LANE), lambda p, c: (c, p))
    return pl.pallas_call(
        body, grid=(npair, s // tc),
        out_shape=(_sds((npair, 2, s, LANE), BF16), _sds((npair, 2, LANE, s), BF16), _sds((npair, 2, LANE, s), BF16)),
        in_specs=[blk, blk, pl.BlockSpec((tc, LANE), lambda p, c: (c, 0))],
        out_specs=(pl.BlockSpec((1, 2, tc, LANE), lambda p, c: (p, 0, c, 0)),
                   pl.BlockSpec((1, 2, LANE, tc), lambda p, c: (p, 0, 0, c)),
                   pl.BlockSpec((1, 2, LANE, tc), lambda p, c: (p, 0, 0, c))),
        name="fox_attn_prep", compiler_params=_cp(("parallel", "parallel")))(k, v, cum)


def _q_aug(qq, lane, hh):
    b = SPARE[hh]
    sel = jnp.logical_and(lane >= b, lane < b + NPIECE)
    neg = jnp.full((1, LANE), -1.0, BF16)
    zl = jnp.zeros((1, LANE), BF16)
    return jnp.where(_head_mask(lane, hh), qq, jnp.where(sel, neg, zl))


def _attn_fwd(q, kaug, vtr, cumt):
    s, d = q.shape
    tq = _tile(s, ATT_FWD_TQ)
    tk = _tile(s, ATT_TK)
    npair = d // LANE
    npart = max(1, tq // tk)

    def body(q_ref, ka_ref, vt_ref, cumt_ref, o_ref, lse_ref):
        p = pl.program_id(0)
        i = pl.program_id(1)
        lane = lax.broadcasted_iota(jnp.int32, (1, LANE), 1)
        qq = q_ref[...]
        qx = (_q_aug(qq, lane, 0), _q_aug(qq, lane, 1))
        cqt = cumt_ref[:, pl.ds(pl.multiple_of(i * tq, tq), tq)]
        cq = (_pick_row(cqt, 2 * p), _pick_row(cqt, 2 * p + 1))
        jd = (i * tq) // tk

        def kv_step(j, carry, diag, q_lo=0):
            ks = pl.multiple_of(j * tk, tk)
            nq_ = tq - q_lo
            if diag:
                krow = lax.broadcasted_iota(jnp.int32, (tk, nq_), 0) + j * tk
                qcol = lax.broadcasted_iota(jnp.int32, (tk, nq_), 1) + (i * tq + q_lo)
                causal = krow <= qcol
            out = []
            for hh in range(2):
                m_all, acc_all = carry[2 * hh], carry[2 * hh + 1]
                m, acc, cqh = m_all[:, q_lo:], acc_all[:, q_lo:], cq[hh][:, q_lo:]
                sc = _dot_nt(ka_ref[0, hh, pl.ds(ks, tk), :], qx[hh][q_lo:, :])
                if diag:
                    sc = jnp.where(causal, sc, -jnp.inf)
                mx = jnp.max(sc, axis=0, keepdims=True) + cqh
                mn = jnp.maximum(m, mx)
                alpha = jnp.exp(m - mn)
                pt = jnp.exp(sc + (cqh - mn)).astype(BF16)
                acc = alpha * acc + _dot(vt_ref[0, hh, :, pl.ds(ks, tk)], pt)
                if q_lo:
                    mn = jnp.concatenate([m_all[:, :q_lo], mn], axis=1)
                    acc = jnp.concatenate([acc_all[:, :q_lo], acc], axis=1)
                out += [mn, acc]
            return tuple(out)

        minit = jnp.full((1, tq), -jnp.inf, F32)
        ainit = jnp.zeros((LANE, tq), F32)
        carry = (minit, ainit, minit, ainit)
        for pj in range(npart):
            carry = kv_step(jd + pj, carry, True, q_lo=pj * tk)
        carry = lax.fori_loop(0, jd, lambda j, cr: kv_step(j, cr, False), carry)
        m0, a0, m1, a1 = carry
        l0 = a0[SPARE[0]:SPARE[0] + 1, :]
        l1 = a1[SPARE[1]:SPARE[1] + 1, :]
        row = lax.broadcasted_iota(jnp.int32, (LANE, 1), 0)
        ot = jnp.where(row < HEAD_DIM, a0 / l0, a1 / l1)
        o_ref[...] = ot.T.astype(BF16)
        r8 = lax.broadcasted_iota(jnp.int32, (8, 1), 0)
        lse_ref[0] = jnp.where(r8 == 0, m0 + jnp.log(l0), jnp.where(r8 == 1, m1 + jnp.log(l1), 0.0))

    return pl.pallas_call(
        body, grid=(npair, s // tq), out_shape=(_sds((s, d), BF16), _sds((npair, 8, s), F32)),
        in_specs=[pl.BlockSpec((tq, LANE), lambda p, i: (i, p)),
                  pl.BlockSpec((1, 2, s, LANE), lambda p, i: (p, 0, 0, 0)),
                  pl.BlockSpec((1, 2, LANE, s), lambda p, i: (p, 0, 0, 0)),
                  pl.BlockSpec((N_HEADS, s), lambda p, i: (0, 0))],
        out_specs=(pl.BlockSpec((tq, LANE), lambda p, i: (i, p)), pl.BlockSpec((1, 8, tq), lambda p, i: (p, 0, i))),
        name="fox_attn_fwd", compiler_params=_cp(("parallel", "parallel")))(q, kaug, vtr, cumt)


def _mm_res_final(a, w, gate, x, gain, target):
    s, k = a.shape
    d = w.shape[1]
    ts = _tile(s, 512)

    def body(a_ref, w_ref, gt_ref, x_ref, g_ref, t_ref, lsum_ref, dx_ref, dg_ref):
        @pl.when(pl.program_id(0) == 0)
        def _():
            lsum_ref[...] = jnp.zeros_like(lsum_ref)
            dg_ref[...] = jnp.zeros_like(dg_ref)

        xv = x_ref[...] + gt_ref[...] * _dot(a_ref[...], w_ref[...])
        xhat, rstd = _rms_parts(xv)
        e = xhat * g_ref[...] - t_ref[...]
        lsum_ref[...] += _colsum(e * e)
        dout = e * (1.0 / d)
        dg_ref[...] += _colsum(dout * xhat)
        dxhat = dout * g_ref[...]
        dx_ref[...] = rstd * (dxhat - xhat * jnp.mean(dxhat * xhat, axis=-1, keepdims=True))

    vec = pl.BlockSpec((1, d), lambda i: (0, 0))
    row = pl.BlockSpec((ts, d), lambda i: (i, 0))
    return pl.pallas_call(
        body, grid=(s // ts,), out_shape=(_sds((1, d), F32), _sds((s, d), F32), _sds((1, d), F32)),
        in_specs=[pl.BlockSpec((ts, k), lambda i: (i, 0)), _resident((k, d)), vec, row, vec, row], out_specs=(vec, row, vec),
        name="ffn1_out_final_loss", compiler_params=_cp(("arbitrary",)))(a, w, gate, x, gain, target)


def _ffn_bwd_act(dx, gate, w_out, ug, uu, name, comm=None):
    s, d = dx.shape
    f = w_out.shape[0]
    n = f // 2
    ts = _tile(s, 512)

    def body(dx_ref, g_ref, w_ref, ug_ref, uu_ref, dug_ref, duu_ref):
        dy = (dx_ref[...] * g_ref[...]).astype(BF16)
        for q in range(2):
            cs = pl.ds(q * n, n)
            dact = _dot_nt(dy, w_ref[cs, :])
            g = ug_ref[:, cs].astype(F32)
            u = uu_ref[:, cs].astype(F32)
            sg = _sigmoid(g)
            dug_ref[:, cs] = (dact * u * sg * (1.0 + g * (1.0 - sg))).astype(BF16)
            duu_ref[:, cs] = (dact * g * sg).astype(BF16)

    tile = pl.BlockSpec((ts, f), lambda i: (i, 0))
    return _hosted_call(
        body, comm, grid=(s // ts,), out_shape=(_sds((s, f), BF16), _sds((s, f), BF16)),
        in_specs=[pl.BlockSpec((ts, d), lambda i: (i, 0)), pl.BlockSpec((1, d), lambda i: (0, 0)),
                  _resident(w_out.shape), tile, tile],
        out_specs=(tile, tile), scratch_shapes=[], name=name, sem=("parallel",), args=(dx, gate, w_out, ug, uu))


def _dw_mm(a, b_list, tk, tn, name, gate=None, wfull=None, dgate_init=None):
    s, kdim = a.shape
    nb1 = b_list[0].shape[1] // tn
    nb = nb1 * len(b_list)
    ts = _tile(s, 1024)
    nk = kdim // tk
    ns = s // ts
    gated = gate is not None

    def body(*refs):
        a_ref = refs[0]
        b_refs = refs[1:1 + len(b_list)]
        rest = refs[1 + len(b_list):]
        if gated:
            g_ref, w_ref, di_ref, o_ref, dg_ref, acc = rest
        else:
            o_ref, acc = rest
        jn, ik, st = pl.program_id(0), pl.program_id(1), pl.program_id(2)

        @pl.when(st == 0)
        def _():
            acc[...] = jnp.zeros_like(acc)

        for mi, b_ref in enumerate(b_refs):
            @pl.when(jn // nb1 == mi)
            def _(b_ref=b_ref):
                acc[...] += _dot_tn(a_ref[...], b_ref[...].astype(BF16))

        if gated:
            @pl.when(jnp.logical_and(ik == 0, st == 0))
            def _():
                dg_ref[...] = di_ref[...]

        @pl.when(st == ns - 1)
        def _():
            if gated:
                o_ref[0] = acc[...] * g_ref[...]
                dg_ref[...] += _colsum(acc[...] * w_ref[...].astype(F32))
            else:
                o_ref[0] = acc[...]

    in_specs = [pl.BlockSpec((ts, tk), lambda jn, ik, st: (st, ik))]
    for mi in range(len(b_list)):
        in_specs.append(pl.BlockSpec(
            (ts, tn), lambda jn, ik, st, mi=mi: (st, jnp.clip(jn - mi * nb1, 0, nb1 - 1))))
    args = [a] + list(b_list)
    out_shape = [_sds((nb, kdim, tn), F32)]
    out_specs = [pl.BlockSpec((1, tk, tn), lambda jn, ik, st: (jn, ik, 0))]
    if gated:
        vec = pl.BlockSpec((1, tn), lambda jn, ik, st: (0, jn))
        in_specs += [vec, pl.BlockSpec((tk, tn), lambda jn, ik, st: (ik, jn)), vec]
        args += [gate, wfull, dgate_init]
        out_shape.append(_sds((1, nb * tn), F32))
        out_specs.append(vec)
    res = pl.pallas_call(
        body, grid=(nb, nk, ns), out_shape=tuple(out_shape), in_specs=in_specs, out_specs=tuple(out_specs),
        scratch_shapes=[pltpu.VMEM((tk, tn), F32)],
        name=name, compiler_params=_cp(("parallel", "arbitrary", "arbitrary")))(*args)
    return res if gated else res[0]


def _mm_normbwd(terms, x, dxres, gain, scale, name, ts_pref=256, comm=None, second=None):
    s, d = x.shape
    ts = _tile(s, ts_pref)
    sets = [(terms, gain, scale)] + ([second] if second is not None else [])
    arrs, warrs = [], []
    for tms, _, _ in sets:
        for a, _, w, _ in tms:
            if not any(a is z for z in arrs):
                arrs.append(a)
            if not any(w is z for z in warrs):
                warrs.append(w)
    na, nw, ns_ = len(arrs), len(warrs), len(sets)

    def body(*refs):
        a_refs, w_refs = refs[:na], refs[na:na + nw]
        x_ref, dr_ref = refs[na + nw], refs[na + nw + 1]
        par = refs[na + nw + 2:na + nw + 2 + 2 * ns_]
        dx_ref = refs[na + nw + 2 + 2 * ns_]
        sums = refs[na + nw + 3 + 2 * ns_:]

        @pl.when(pl.program_id(0) == 0)
        def _():
            for r in sums:
                r[...] = jnp.zeros_like(r)

        xhat, rstd = _rms_parts(x_ref[...])
        dxhat = None
        for k, (tms, _, _) in enumerate(sets):
            g_ref, sc_ref = par[2 * k], par[2 * k + 1]
            dsh_ref, dsc_ref, dg_ref = sums[3 * k:3 * k + 3]
            dh = None
            for a, c0, w, q in tms:
                ai = next(i for i, z in enumerate(arrs) if z is a)
                wi = next(i for i, z in enumerate(warrs) if z is w)
                part = _dot_nt(a_refs[ai][:, pl.ds(c0, w.shape[2])], w_refs[wi][q])
                dh = part if dh is None else dh + part
            dsh_ref[...] += _colsum(dh)
            dsc_ref[...] += _colsum(dh * (xhat * g_ref[...]))
            dn = dh * (1.0 + sc_ref[...])
            dg_ref[...] += _colsum(dn * xhat)
            dxh = dn * g_ref[...]
            dxhat = dxh if dxhat is None else dxhat + dxh
        dx_ref[...] = dr_ref[...] + rstd * (dxhat - xhat * jnp.mean(dxhat * xhat, axis=-1, keepdims=True))

    vec = pl.BlockSpec((1, d), lambda i: (0, 0))
    row = pl.BlockSpec((ts, d), lambda i: (i, 0))
    in_specs = [pl.BlockSpec((ts, a.shape[1]), lambda i: (i, 0)) for a in arrs]
    in_specs += [_resident(w.shape) for w in warrs]
    in_specs += [row, row] + [vec] * (2 * ns_)
    par_args = [p_ for _, g_, s_ in sets for p_ in (g_, s_)]
    return _hosted_call(
        body, comm, grid=(s // ts,), out_shape=(_sds((s, d), F32),) + tuple(_sds((1, d), F32) for _ in range(3 * ns_)),
        in_specs=in_specs, out_specs=(row,) + tuple(vec for _ in range(3 * ns_)), scratch_shapes=[],
        name=name, sem=("arbitrary",), args=(*arrs, *warrs, x, dxres, *par_args))


def _do_kernel(dx, gate, wo, o, comm=None):
    s, d = dx.shape
    ts = _tile(s, 512)

    def body(dx_ref, g_ref, w_ref, o_ref, do_ref, dl_ref):
        dy = (dx_ref[...] * g_ref[...]).astype(BF16)
        do = _dot_nt(dy, w_ref[...])
        do_ref[...] = do.astype(BF16)
        prod = do * o_ref[...].astype(F32)
        hrow = lax.broadcasted_iota(jnp.int32, (N_HEADS, d), 0)
        hcol = lax.broadcasted_iota(jnp.int32, (N_HEADS, d), 1) // HEAD_DIM
        sel = (hrow == hcol).astype(F32)
        dl_ref[...] = lax.dot_general(sel, prod, (((1,), (1,)), ((), ())), preferred_element_type=F32,
                                      precision=lax.Precision.HIGHEST)

    row = pl.BlockSpec((ts, d), lambda i: (i, 0))
    return _hosted_call(
        body, comm, grid=(s // ts,), out_shape=(_sds((s, d), BF16), _sds((N_HEADS, s), F32)),
        in_specs=[row, pl.BlockSpec((1, d), lambda i: (0, 0)), _resident(wo.shape), row],
        out_specs=(row, pl.BlockSpec((N_HEADS, ts), lambda i: (0, i))), scratch_shapes=[],
        name="attn_do", sem=("parallel",), args=(dx, gate, wo, o))


def _attn_bwd(q, do, kaug, kaugt, v, cumt, lse, deltat, comm=None):
    s, d = q.shape
    tq = _tile(s, ATT_TQ)
    tk = _tile(s, ATT_TK)
    assert tq in (tk, 2 * tk)
    npair = d // LANE
    nq = s // tq
    nkb = s // tk
    qscale = HEAD_DIM ** -0.5

    def body(q_ref, do_ref, ka_ref, kt_ref, v_ref, cumt_ref, lse_ref, dl_ref,
             dq_ref, dk_ref, dv_ref, dcq_ref, dck_ref, qaug, dom, rowv, dqt):
        p = pl.program_id(0)
        j = pl.program_id(1)
        lane = lax.broadcasted_iota(jnp.int32, (1, LANE), 1)
        lo = lane < HEAD_DIM
        r8 = lax.broadcasted_iota(jnp.int32, (8, 1), 0)

        @pl.when(j == 0)
        def _():
            dqt[...] = jnp.zeros_like(dqt)
            for c in range(nq):
                rows = pl.ds(c * tq, tq)
                qq = q_ref[rows, :]
                dd = do_ref[rows, :]
                cqt = cumt_ref[:, rows]
                dlt = dl_ref[:, rows]
                lst = lse_ref[0, :, rows]
                for hh in range(2):
                    qaug[hh, rows, :] = _q_aug(qq, lane, hh)
                    dom[hh, rows, :] = jnp.where(_head_mask(lane, hh), dd, jnp.zeros_like(dd))
                    rowv[hh, :, rows] = jnp.where(
                        r8 == 0, _pick_row(cqt, 2 * p + hh) - lst[hh:hh + 1, :],
                        jnp.where(r8 == 1, _pick_row(dlt, 2 * p + hh), 0.0))

        vv = v_ref[...]
        i0 = (j * tk) // tq

        def q_step(qs, nq_, carry, diag):
            dv_acc, dk0, dk1 = carry
            qs = pl.multiple_of(qs, tk)
            if diag:
                krow = lax.broadcasted_iota(jnp.int32, (tk, nq_), 0) + j * tk
                qcol = lax.broadcasted_iota(jnp.int32, (tk, nq_), 1) + qs
                causal = krow <= qcol
            dks = [dk0, dk1]
            for hh in range(2):
                rv = rowv[hh, :, pl.ds(qs, nq_)]
                qa = qaug[hh, pl.ds(qs, nq_), :]
                dh = dom[hh, pl.ds(qs, nq_), :]
                sc = _dot_nt(ka_ref[0, hh], qa)
                if diag:
                    sc = jnp.where(causal, sc, -jnp.inf)
                pt = jnp.exp(sc + rv[0:1, :])
                dpt = _dot_nt(vv, dh)
                dst = (pt * (dpt - rv[1:2, :])).astype(BF16)
                dv_acc = dv_acc + _dot(pt.astype(BF16), dh)
                dks[hh] = dks[hh] + _dot(dst, qa)
                dqt[hh, :, pl.ds(qs, nq_)] += _dot(kt_ref[0, hh], dst)
            return dv_acc, dks[0], dks[1]

        z = jnp.zeros((tk, LANE), F32)
        first = ((j * tk) % tq == 0).astype(jnp.int32)
        carry = lax.fori_loop(0, first, lambda _, cr: q_step(i0 * tq, tq, cr, True), (z, z, z))
        if tq > tk:
            carry = lax.fori_loop(0, 1 - first, lambda _, cr: q_step(j * tk, tq - tk, cr, True), carry)
        dv_acc, dk0, dk1 = lax.fori_loop(i0 + 1, nq, lambda i, cr: q_step(i * tq, tq, cr, False), carry)
        dv_ref[...] = dv_acc.astype(BF16)
        dk_ref[...] = jnp.where(lo, dk0, dk1).astype(BF16)
        dck_ref[0] = jnp.where(r8 == 0, dk0.T[SPARE[0]:SPARE[0] + 1, :],
                               jnp.where(r8 == 1, dk1.T[SPARE[1]:SPARE[1] + 1, :], 0.0))

        @pl.when(j == nkb - 1)
        def _():
            for c in range(nq):
                rows = pl.ds(c * tq, tq)
                a0 = dqt[0, :, rows].T
                a1 = dqt[1, :, rows].T
                dq_ref[rows, :] = (jnp.where(lo, a0, a1) * qscale).astype(BF16)
            r0, r1 = SPARE[0] + NPIECE, SPARE[1] + NPIECE
            dcq_ref[0] = jnp.where(r8 == 0, dqt[0, r0:r0 + 1, :], jnp.where(r8 == 1, dqt[1, r1:r1 + 1, :], 0.0))

    col = pl.BlockSpec((s, LANE), lambda p, j: (0, p), pipeline_mode=pl.Buffered(1))
    rows16 = pl.BlockSpec((N_HEADS, s), lambda p, j: (0, 0), pipeline_mode=pl.Buffered(1))
    blk = pl.BlockSpec((tk, LANE), lambda p, j: (j, p))
    return _hosted_call(
        body, comm, grid=(npair, nkb),
        out_shape=(_sds((s, d), BF16), _sds((s, d), BF16), _sds((s, d), BF16), _sds((npair, 8, s), F32), _sds((npair, 8, s), F32)),
        in_specs=[col, col, pl.BlockSpec((1, 2, tk, LANE), lambda p, j: (p, 0, j, 0)),
                  pl.BlockSpec((1, 2, LANE, tk), lambda p, j: (p, 0, 0, j)), blk, rows16,
                  pl.BlockSpec((1, 8, s), lambda p, j: (p, 0, 0), pipeline_mode=pl.Buffered(1)), rows16],
        out_specs=(pl.BlockSpec((s, LANE), lambda p, j: (0, p)), blk, blk,
                   pl.BlockSpec((1, 8, s), lambda p, j: (p, 0, 0)), pl.BlockSpec((1, 8, tk), lambda p, j: (p, 0, j))),
        scratch_shapes=[pltpu.VMEM((2, s, LANE), BF16), pltpu.VMEM((2, s, LANE), BF16), pltpu.VMEM((2, 8, s), F32),
                        pltpu.VMEM((2, LANE, s), F32)],
        name="fox_attn_bwd", sem=("arbitrary", "arbitrary"), vmem_mb=ATT_BWD_VMEM_MB,
        args=(q, do, kaug, kaugt, v, cumt, lse, deltat))


def _cumsum_bwd(dcq, dck, flog, fb):
    s = flog.shape[0]
    ts = _tile(s, 256)
    nt = s // ts

    def body(dq_ref, dk_ref, f_ref, b_ref, df_ref, db_ref, carry):
        @pl.when(pl.program_id(0) == 0)
        def _():
            carry[...] = jnp.zeros_like(carry)
            db_ref[...] = jnp.zeros_like(db_ref)

        r = lax.broadcasted_iota(jnp.int32, (ts, ts), 0)
        cidx = lax.broadcasted_iota(jnp.int32, (ts, ts), 1)
        tri = (r >= cidx).astype(F32)
        dct = dq_ref[...] + dk_ref[...]
        dlst = jnp.dot(dct, tri, preferred_element_type=F32, precision=lax.Precision.HIGHEST) + carry[...]
        carry[...] = dlst[:, 0:1]
        dls = jnp.concatenate([dlst, jnp.zeros((LANE - N_HEADS, ts), F32)], axis=0).T
        z = f_ref[...] + b_ref[...]
        df = dls * (1.0 / (1.0 + jnp.exp(z)))
        db_ref[...] += _colsum(df)
        df_ref[...] = df.astype(BF16)

    rev = pl.BlockSpec((ts, LANE), lambda i: (nt - 1 - i, 0))
    revt = pl.BlockSpec((N_HEADS, ts), lambda i: (0, nt - 1 - i))
    vec = pl.BlockSpec((1, LANE), lambda i: (0, 0))
    return pl.pallas_call(
        body, grid=(nt,), out_shape=(_sds((s, LANE), BF16), _sds((1, LANE), F32)),
        in_specs=[revt, revt, rev, vec], out_specs=(rev, vec), scratch_shapes=[pltpu.VMEM((N_HEADS, 1), F32)],
        name="forget_cumsum_bwd", compiler_params=_cp(("arbitrary",)))(dcq, dck, flog, fb)


def _conv_bwd1(dx, gate, w_out, b_out, dwo, lng, lnb):
    s, d = dx.shape
    ts = _tile(s, 512)
    ns = s // ts

    def body(dx_ref, g_ref, w_ref, bo_ref, y_ref, lg_ref, lb_ref, dd_ref, dlg_ref, dlb_ref, dbd_ref, dbo_ref, dge_ref, cs):
        i = pl.program_id(0)

        @pl.when(i == 0)
        def _():
            for r in (dlg_ref, dlb_ref, dbd_ref, cs):
                r[...] = jnp.zeros_like(r)

        dxv = dx_ref[...]
        cs[...] += _colsum(dxv)
        dsw = _dot_nt((dxv * g_ref[...]).astype(BF16), w_ref[...])
        yv = y_ref[...]
        mu = jnp.mean(yv, axis=-1, keepdims=True)
        yc = yv - mu
        rstd = lax.rsqrt(jnp.mean(yc * yc, axis=-1, keepdims=True) + EPS)
        xhat = yc * rstd
        ln = xhat * lg_ref[...] + lb_ref[...]
        sg = _sigmoid(ln)
        dln = dsw * (sg * (1.0 + ln * (1.0 - sg)))
        dlg_ref[...] += _colsum(dln * xhat)
        dlb_ref[...] += _colsum(dln)
        dxh = dln * lg_ref[...]
        dd = rstd * (dxh - jnp.mean(dxh, axis=-1, keepdims=True) - xhat * jnp.mean(dxh * xhat, axis=-1, keepdims=True))
        dbd_ref[...] += _colsum(dd)
        dd_ref[...] = dd

        @pl.when(i == ns - 1)
        def _():
            dbo_ref[...] = g_ref[...] * cs[...]
            dge_ref[...] = bo_ref[...] * cs[...]

    vec = pl.BlockSpec((1, d), lambda i: (0, 0))
    row = pl.BlockSpec((ts, d), lambda i: (i, 0))
    return pl.pallas_call(
        body, grid=(ns,), out_shape=(_sds((s, d), F32),) + tuple(_sds((1, d), F32) for _ in range(5)),
        in_specs=[row, vec, _resident(w_out.shape), vec, row, vec, vec], out_specs=(row, vec, vec, vec, vec, vec),
        scratch_shapes=[pltpu.VMEM((1, d), F32)],
        name="conv_bwd_ln", compiler_params=_cp(("arbitrary",)))(dx, gate, w_out, b_out, dwo, lng, lnb)


def _dwconv_bwd(ddwo, glu, a_s, g_s, wdw, comm=None):
    s, d = ddwo.shape
    ts = _tile(s, 256)
    ns = s // ts
    rb, cb = 32, 256
    nrb = ts // rb

    def body(dd_ref, ddn_ref, gl_ref, glh_ref, a_ref, g_ref, w_ref, da_ref, dg_ref, dw_ref, sa_ref, sg_ref, bufd, bufg, dws,
             shd, shg):
        i = pl.program_id(0)

        @pl.when(i == 0)
        def _():
            dws[...] = jnp.zeros_like(dws)
            sa_ref[...] = jnp.zeros_like(sa_ref)
            sg_ref[...] = jnp.zeros_like(sg_ref)
            bufg[pl.ds(0, HALO), :] = jnp.zeros((HALO, d), F32)

        @pl.when(i > 0)
        def _():
            bufg[pl.ds(0, HALO), :] = glh_ref[...]

        bufg[pl.ds(HALO, ts), :] = gl_ref[...]
        bufd[pl.ds(0, ts), :] = dd_ref[...]

        @pl.when(i == ns - 1)
        def _():
            bufd[pl.ds(ts, HALO), :] = jnp.zeros((HALO, d), F32)

        @pl.when(i < ns - 1)
        def _():
            bufd[pl.ds(ts, HALO), :] = ddn_ref[...]

        _shift_copies(bufd, shd)
        _shift_copies(bufg, shg)
        for cc in range(d // cb):
            cs = pl.ds(cc * cb, cb)
            for r in range(nrb):
                acc = jnp.zeros((rb, cb), F32)
                for k in range(CONV_K):
                    acc = acc + w_ref[pl.ds(k, 1), cs] * _shifted(bufd, shd, r * rb + (CONV_K - 1) - k, rb, cs)
                rows = pl.ds(r * rb, rb)
                av = a_ref[rows, cs].astype(F32)
                sg = _sigmoid(g_ref[rows, cs].astype(F32))
                dav = acc * sg
                dgv = acc * av * sg * (1.0 - sg)
                da_ref[rows, cs] = dav.astype(BF16)
                dg_ref[rows, cs] = dgv.astype(BF16)
                sa_ref[:, cs] += _colsum(dav)
                sg_ref[:, cs] += _colsum(dgv)
            for k in range(CONV_K):
                acc8 = jnp.zeros((8, cb), F32)
                for r in range(nrb):
                    prod = bufd[pl.ds(r * rb, rb), cs] * _shifted(bufg, shg, HALO - (CONV_K - 1) + k + r * rb, rb, cs)
                    acc8 = acc8 + (prod[0:8] + prod[8:16]) + (prod[16:24] + prod[24:32])
                dws[pl.ds(8 * k, 8), cs] += acc8

        @pl.when(i == ns - 1)
        def _():
            dw_ref[...] = jnp.zeros_like(dw_ref)
            for k in range(CONV_K):
                dw_ref[pl.ds(k, 1), :] = _colsum(dws[pl.ds(8 * k, 8), :])

    row = pl.BlockSpec((ts, d), lambda i: (i, 0))
    vec = pl.BlockSpec((1, d), lambda i: (0, 0))
    hb = ts // HALO
    return _hosted_call(
        body, comm, grid=(ns,),
        out_shape=(_sds((s, d), BF16), _sds((s, d), BF16), _sds((HALO, d), F32), _sds((1, d), F32), _sds((1, d), F32)),
        in_specs=[row, pl.BlockSpec((HALO, d), lambda i: (jnp.minimum((i + 1) * hb, ns * hb - 1), 0)),
                  row, pl.BlockSpec((HALO, d), lambda i: (jnp.maximum(i * hb - 1, 0), 0)),
                  row, row, pl.BlockSpec((HALO, d), lambda i: (0, 0))],
        out_specs=(row, row, pl.BlockSpec((HALO, d), lambda i: (0, 0)), vec, vec),
        scratch_shapes=[pltpu.VMEM((ts + HALO, d), F32), pltpu.VMEM((HALO + ts, d), F32), pltpu.VMEM((8 * HALO, d), F32),
                        pltpu.VMEM((SUBLANES - 1, HALO + ts - SUBLANES, d), F32),
                        pltpu.VMEM((SUBLANES - 1, HALO + ts - SUBLANES, d), F32)],
        name="dwconv_bwd", sem=("arbitrary",), args=(ddwo, ddwo, glu, glu, a_s, g_s, wdw))


def _ada_wgrad(cat, da, name):
    nl, _, n = da.shape
    d = cat.shape[0]
    tn = 256

    def body(c_ref, d_ref, o_ref):
        acc = c_ref[:, 0:1] * d_ref[0, 0:1, :]
        for r in range(1, 8):
            acc = acc + c_ref[:, r:r + 1] * d_ref[0, r:r + 1, :]
        o_ref[0] = acc

    return pl.pallas_call(
        body, grid=(nl, n // tn), out_shape=_sds((nl, d, n), F32),
        in_specs=[pl.BlockSpec((d, 8), lambda l, j: (0, 0)), pl.BlockSpec((1, 8, tn), lambda l, j: (l, 0, j))],
        out_specs=pl.BlockSpec((1, d, tn), lambda l, j: (l, 0, j)),
        name=name, compiler_params=_cp(("parallel", "parallel")))(cat, da)


def _silu_rows(c_all):
    def body(c_ref, o_ref):
        cc = c_ref[...]
        o_ref[...] = cc * _sigmoid(cc)

    return pl.pallas_call(body, out_shape=_sds(c_all.shape, F32), name="silu_c")(c_all)


def _adamw(w, g, m, v, name):
    r, c = w.shape
    tr = r
    for cand in (512, 256, 128, 64, 32, 16, 8):
        if r % cand == 0 and cand * c * 4 <= (1 << 20):
            tr = cand
            break
    bc1 = 1.0 - ADAM_B1 ** ADAM_STEP
    bc2 = 1.0 - ADAM_B2 ** ADAM_STEP

    def body(w_ref, g_ref, m_ref, v_ref, d_ref, nm_ref, nv_ref):
        gv = g_ref[...]
        mn = ADAM_B1 * m_ref[...] + (1.0 - ADAM_B1) * gv
        vn = ADAM_B2 * v_ref[...] + (1.0 - ADAM_B2) * (gv * gv)
        mh = mn / bc1
        vh = vn / bc2
        d_ref[...] = -ADAM_LR * (mh / (jnp.sqrt(vh) + ADAM_EPS) + ADAM_WD * w_ref[...])
        nm_ref[...] = mn
        nv_ref[...] = vn

    blk = pl.BlockSpec((tr, c), lambda i: (i, 0))
    return pl.pallas_call(
        body, grid=(r // tr,), out_shape=tuple(_sds((r, c), F32) for _ in range(3)),
        in_specs=[blk, blk, blk, blk], out_specs=(blk, blk, blk),
        name=name, compiler_params=_cp(("parallel",)))(w, g, m, v)


def _pad_rows(a, rows, axis):
    pad = [(0, 0)] * a.ndim
    pad[axis] = (0, rows - a.shape[axis])
    return jnp.pad(a, pad)


def _vec(a):
    return a.reshape(1, -1)


def kernel(x, c, mix_norm_g, mix_ada_w, mix_ada_b, ffn_norm_g, ffn_ada_w, ffn_ada_b, ffn_w_in, ffn_w_out, conv_w_in, conv_b_in, conv_w_dw, conv_b_dw, conv_ln_g, conv_ln_b, conv_w_out, conv_b_out, kv_norm_g, kv_ada_w, kv_ada_b, kv_w, forget_b, attn_w_q, attn_w_o, final_norm_g, loss_target, m_mix_norm_g, m_mix_ada_w, m_mix_ada_b, m_ffn_norm_g, m_ffn_ada_w, m_ffn_ada_b, m_ffn_w_in, m_ffn_w_out, m_conv_w_in, m_conv_b_in, m_conv_w_dw, m_conv_b_dw, m_conv_ln_g, m_conv_ln_b, m_conv_w_out, m_conv_b_out, m_kv_norm_g, m_kv_ada_w, m_kv_ada_b, m_kv_w, m_forget_b, m_attn_w_q, m_attn_w_o, m_final_norm_g, v_mix_norm_g, v_mix_ada_w, v_mix_ada_b, v_ffn_norm_g, v_ffn_ada_w, v_ffn_ada_b, v_ffn_w_in, v_ffn_w_out, v_conv_w_in, v_conv_b_in, v_conv_w_dw, v_conv_b_dw, v_conv_ln_g, v_conv_ln_b, v_conv_w_out, v_conv_b_out, v_kv_norm_g, v_kv_ada_w, v_kv_ada_b, v_kv_w, v_forget_b, v_attn_w_q, v_attn_w_o, v_final_norm_g):
    xi, yi, ci = lax.axis_index("x"), lax.axis_index("y"), lax.axis_index("c")
    chip = 2 * xi + yi
    dev = 4 * xi + 2 * yi + ci
    s, d = x.shape[1], x.shape[2]
    f = ffn_w_out.shape[1] * 4
    x0 = x[0]
    nkv = kv_w.shape[1]
    nkv_all = 4 * nkv

    def my_halves(ws):
        return [lax.dynamic_index_in_dim(w.astype(BF16).reshape(2, w.shape[0] // 2, w.shape[1]), ci, axis=0, keepdims=False)
                for w in ws]

    def whole(gath, ws):
        return [g.reshape(4, w.shape[0], w.shape[1]) for g, w in zip(gath, ws)]

    wdw_loc = _pad_rows(conv_w_dw[0], HALO, 0)
    small = jnp.concatenate([c.reshape(-1), conv_b_in.reshape(-1), wdw_loc.reshape(-1), conv_b_dw.reshape(-1),
                             conv_ln_g.reshape(-1), conv_ln_b.reshape(-1), conv_b_out.reshape(-1)])
    n_small = small.shape[0]
    w_small = -(-n_small // (8 * LANE)) * LANE
    small = jnp.pad(small, (0, 8 * w_small - n_small)).reshape(8, w_small)
    first = [small] + my_halves([conv_w_in[0]])
    small_all, cw_in = _exchange(_Gather8(first), first, "ag_small_params_w_conv")
    small_all = small_all.reshape(8, 8 * w_small)
    cw_in, = whole([cw_in], [conv_w_in[0]])
    c_all = small_all[:, :d]
    per_chip = small_all[0::2]
    dq_ = d // 4
    o1 = d
    b_in_full = per_chip[:, o1:o1 + 2 * dq_].reshape(4, 1, 2 * dq_)
    o1 += 2 * dq_
    wdw_full = per_chip[:, o1:o1 + HALO * dq_].reshape(4, HALO, dq_).transpose(1, 0, 2).reshape(HALO, d)
    o1 += HALO * dq_
    bdw_full = per_chip[:, o1:o1 + dq_].reshape(1, d)
    lng_full = per_chip[:, o1 + dq_:o1 + 2 * dq_].reshape(1, d)
    lnb_full = per_chip[:, o1 + 2 * dq_:o1 + 3 * dq_].reshape(1, d)
    bout_full = per_chip[:, o1 + 3 * dq_:o1 + 4 * dq_].reshape(1, d)

    a_mix = _ada_fwd(c_all, mix_ada_w, "ada_mix")
    a_ffn = _ada_fwd(c_all, ffn_ada_w, "ada_ffn")
    a_kv = _ada_fwd(c_all, kv_ada_w[None], "ada_kv")
    n3 = mix_ada_w.shape[2]
    n2 = kv_ada_w.shape[1]
    ada_loc = jnp.concatenate([a_mix[0], a_mix[1], a_ffn[0], a_ffn[1], a_kv[0]], axis=1)
    w_ada = ada_loc.shape[1]
    ada_all = _allgather8(ada_loc, "ag_ada", True).reshape(8, 8, w_ada)
    ada_me = lax.dynamic_index_in_dim(ada_all, dev, axis=1, keepdims=False)[0::2]

    def ada_vec(off, n, bias):
        return ada_me[:, off:off + n].reshape(1, 4 * n) + bias.reshape(1, -1)

    ada_m0 = ada_vec(0, n3, mix_ada_b[0])
    ada_m1 = ada_vec(n3, n3, mix_ada_b[1])
    ada_f0 = ada_vec(2 * n3, n3, ffn_ada_b[0])
    ada_f1 = ada_vec(3 * n3, n3, ffn_ada_b[1])
    ada_k = ada_vec(4 * n3, n2, kv_ada_b)

    def split3(a):
        return a[:, :d], a[:, d:2 * d], a[:, 2 * d:3 * d]

    sh_m0, sc_m0, gt_m0 = split3(ada_m0)
    sh_m1, sc_m1, gt_m1 = split3(ada_m1)
    sh_f0, sc_f0, gt_f0 = split3(ada_f0)
    sh_f1, sc_f1, gt_f1 = split3(ada_f1)
    sh_k, sc_k = ada_k[:, :d], ada_k[:, d:2 * d]

    grp_a = [ffn_w_in[0]]
    grp_b = [ffn_w_out[0], conv_w_out[0], ffn_w_in[1]]
    grp_c = [ffn_w_out[1], kv_w, attn_w_q[0], attn_w_o[0]]
    mine_a, mine_b, mine_c = my_halves(grp_a), my_halves(grp_b), my_halves(grp_c)

    zero_b = jnp.zeros((1, d), F32)
    g_m0, g_m1 = _vec(mix_norm_g[0]), _vec(mix_norm_g[1])
    g_f0, g_f1 = _vec(ffn_norm_g[0]), _vec(ffn_norm_g[1])
    g_k, g_fin = _vec(kv_norm_g), _vec(final_norm_g)
    fb = jnp.pad(forget_b, (0, LANE - N_HEADS)).reshape(1, LANE)

    h0, glu, a_s, g_s, gath_a = _in_pair(x0, g_m0, sh_m0, sc_m0, cw_in, b_in_full, True, "conv_in",
                                         comm=(_Gather8(mine_a), mine_a))
    dwo, sw, gath_b = _dwconv_fwd(glu, wdw_full, bdw_full, lng_full, lnb_full, comm=(_Gather8(mine_b), mine_b))
    w_in0, = whole(gath_a, grp_a)
    w_out0, cw_out, w_in1 = whole(gath_b, grp_b)
    cw_out = cw_out.reshape(d, d)
    w_in = [w_in0, w_in1]
    x1 = _mm_res(sw, cw_out, bout_full, gt_m0, x0, "conv_out")
    hf0, act0, ug0, uu0, gath_c = _in_pair(x1, g_f0, sh_f0, sc_f0, w_in[0], None, False, "ffn0_in",
                                           comm=(_Gather8(mine_c), mine_c))
    w_out1, kvw, wq, wo = whole(gath_c, grp_c)
    w_out = [w_out0.reshape(f, d), w_out1.reshape(f, d)]
    kvw = kvw.transpose(1, 0, 2).reshape(d, nkv_all)
    wk, wv = kvw[:, :d], kvw[:, d:2 * d]
    wf = jnp.pad(kvw[:, 2 * d:], ((0, 0), (0, LANE - N_HEADS)))
    wq, wo = wq.reshape(d, d), wo.reshape(d, d)
    x2 = _mm_res(act0, w_out[0], zero_b, gt_f0, x1, "ffn0_out")
    hk, h1, kk, vv, qq, flog = _qkv(x2, (g_k, sh_k, sc_k), (g_m1, sh_m1, sc_m1), wk, wv, wf, wq)
    cum, cumt = _cumsum_fwd(flog, fb)
    kaug, kaugt, vtr = _attn_prep(kk, vv, cum)
    o, lse = _attn_fwd(qq, kaug, vtr, cumt)
    x3 = _mm_res(o, wo, zero_b, gt_m1, x2, "attn_out")
    hf1, act1, ug1, uu1 = _in_pair(x3, g_f1, sh_f1, sc_f1, w_in[1], None, False, "ffn1_in")
    lsum, dx4, d_gfin = _mm_res_final(act1, w_out[1], gt_f1, x3, g_fin, loss_target[0])
    loss = lax.psum(0.5 / d * jnp.sum(lsum), ("x", "y", "c"))

    nf = f // 2

    sel = jnp.stack([ci, chip]).astype(jnp.int32)

    def halves_of(gs):
        return [g.reshape(4, 2, g.shape[1] // 2, g.shape[2]) for g in gs]

    def reduce_adds(ps, lands, tag):
        pairs = [_add_halves(p_, l_, sel, f"{tag}_add{k}") for k, (p_, l_) in enumerate(zip(ps, lands))]
        return [q for q, _ in pairs], [o_ for _, o_ in pairs]

    def reduce_begin(gs, tag):
        ps = halves_of(gs)
        return reduce_adds(ps, _exchange(_SwapHalves(ps), ps, tag + "_swap"), tag)

    def reduce_sum(owns, lands, tag):
        return [_add_chips(o_, l_, sel, f"{tag}_sum{k}") for k, (o_, l_) in enumerate(zip(owns, lands))]

    def ffn_bwd(dx_out, x_in, hf, act, ug, uu, gain, scale, gate, w_in_l, w_out_l, tag, comm=None):
        res = _ffn_bwd_act(dx_out, gate, w_out_l, ug, uu, tag + "_bwd_act", comm=comm)
        dug, duu = res[0], res[1]
        dw_out, dgate = _dw_mm(act, [dx_out], nf, d, tag + "_dw_out", gate=gate, wfull=w_out_l, dgate_init=zero_b)
        terms = [(dug, 0, w_in_l, 0), (dug, nf, w_in_l, 1), (duu, 0, w_in_l, 2), (duu, nf, w_in_l, 3)]
        dx_in, dsh, dsc, dgn = _mm_normbwd(terms, x_in, dx_out, gain, scale, tag + "_bwd_in")
        dw_in = _dw_mm(hf, [dug, duu], d, nf, tag + "_dw_in")
        return dx_in, dw_in, dw_out[0], dsh, dsc, dgate, dgn, (res[2] if comm is not None else None)

    dx3, dw_in1, dw_out1, dsh_f1, dsc_f1, dgt_f1, dgn_f1, _ = ffn_bwd(dx4, x3, hf1, act1, ug1, uu1, g_f1, sc_f1, gt_f1, w_in[1], w_out[1], "ffn1")
    ps_1 = halves_of([dw_in1, dw_out1.reshape(4, f // 4, d)])
    do, deltat, swapped_1 = _do_kernel(dx3, gt_m1, wo, o, comm=(_SwapHalves(ps_1), ps_1))
    q16_1, own_1 = reduce_adds(ps_1, swapped_1, "rs_ffn1")
    dwo_att, dgt_m1 = _dw_mm(o, [dx3], d, d, "attn_dw_o", gate=gt_m1, wfull=wo, dgate_init=zero_b)
    dq, dk, dv, dcq, dck, land_1 = _attn_bwd(qq, do, kaug, kaugt, vv, cumt, lse, deltat, comm=(_ScatterChips(q16_1), q16_1))
    dwq = _dw_mm(h1, [dq], d, d, "attn_dw_q")[0]

    df, dfb = _cumsum_bwd(dcq[:, :2].reshape(N_HEADS, s), dck[:, :2].reshape(N_HEADS, s), flog, fb)
    terms = [(dk, 0, wk.reshape(1, d, d), 0), (dv, 0, wv.reshape(1, d, d), 0), (df, 0, wf.reshape(1, d, LANE), 0)]
    dx2, dsh_m1, dsc_m1, dgn_m1, dsh_k, dsc_k, dgn_k = _mm_normbwd(
        [(dq, 0, wq.reshape(1, d, d), 0)], x2, dx3, g_m1, sc_m1, "attn_kv_bwd", second=(terms, g_k, sc_k))
    dwk, dwv = _dw_mm(hk, [dk, dv], d, d, "kv_dw_kv")
    dwf = _dw_mm(hk, [df], d, LANE, "kv_dw_f")[0]
    dkvw = jnp.concatenate([dwk, dwv, dwf[:, :N_HEADS]], axis=1)
    dkvw = dkvw.reshape(d, 4, nkv).transpose(1, 0, 2)

    q16_2, own_2 = reduce_begin([dkvw, dwq.reshape(4, d // 4, d), dwo_att[0].reshape(4, d // 4, d)], "rs_attn")
    dx1, dw_in0, dw_out0, dsh_f0, dsc_f0, dgt_f0, dgn_f0, land_2 = ffn_bwd(
        dx2, x1, hf0, act0, ug0, uu0, g_f0, sc_f0, gt_f0, w_in[0], w_out[0], "ffn0", comm=(_ScatterChips(q16_2), q16_2))

    ddwo, d_lng, d_lnb, d_bdw, d_bout, dgt_extra = _conv_bwd1(dx1, gt_m0, cw_out, bout_full, dwo, lng_full, lnb_full)
    dcw_out, dgt_m0 = _dw_mm(sw, [dx1], d, d, "conv_dw_out", gate=gt_m0, wfull=cw_out, dgate_init=dgt_extra)
    q16_3, own_3 = reduce_begin([dw_in0, dw_out0.reshape(4, f // 4, d), dcw_out[0].reshape(4, d // 4, d)], "rs_ffn0")
    da, dg, d_wdw, d_bin_a, d_bin_g, land_3 = _dwconv_bwd(ddwo, glu, a_s, g_s, wdw_full, comm=(_ScatterChips(q16_3), q16_3))
    nc = cw_in.shape[2]
    terms = [(da, 0, cw_in, 0), (da, nc, cw_in, 1), (dg, 0, cw_in, 2), (dg, nc, cw_in, 3)]
    dx0, dsh_m0, dsc_m0, dgn_m0 = _mm_normbwd(terms, x0, dx1, g_m0, sc_m0, "conv_bwd_in")
    dcw_in = _dw_mm(h0, [da, dg], d, nc, "conv_dw_in")
    q16_4, own_4 = reduce_begin([dcw_in], "rs_conv")
    land_4 = _exchange(_ScatterChips(q16_4), q16_4, "rs_conv_scatter")

    sums = (reduce_sum(own_1, land_1, "rs_ffn1") + reduce_sum(own_2, land_2, "rs_attn")
            + reduce_sum(own_3, land_3, "rs_ffn0") + reduce_sum(own_4, land_4, "rs_conv"))
    reduced = [b.reshape(2 * b.shape[1], b.shape[2]) for b in _share_halves(sums)]
    g_w_in1, g_w_out1, g_kvw, g_wq, g_wo, g_w_in0, g_w_out0, g_cw_out, g_cw_in = reduced

    d_ada = [jnp.concatenate([dsh_m0, dsc_m0, dgt_m0], axis=1), jnp.concatenate([dsh_m1, dsc_m1, dgt_m1], axis=1),
             jnp.concatenate([dsh_f0, dsc_f0, dgt_f0], axis=1), jnp.concatenate([dsh_f1, dsc_f1, dgt_f1], axis=1),
             jnp.concatenate([dsh_k, dsc_k], axis=1)]
    fields = d_ada + [dgn_m0, dgn_m1, dgn_f0, dgn_f1, dgn_k, d_gfin, d_bin_a, d_bin_g, d_bdw, d_lng, d_lnb, d_bout,
                      d_wdw.reshape(1, -1), dfb]
    foffs = [0]
    for fl in fields:
        foffs.append(foffs[-1] + fl.shape[1])
    n_row = foffs[-1]
    w_row = -(-n_row // (8 * LANE)) * LANE
    row = jnp.pad(jnp.concatenate(fields, axis=1), ((0, 0), (0, 8 * w_row - n_row))).reshape(8, w_row)
    rows_all = _allgather8(row, "ag_small_grads", True).reshape(8, 8, w_row)
    rsum_small = _sum8(rows_all).reshape(1, 8 * w_row)
    rows_flat = rows_all.reshape(8, 8 * w_row)

    def fsum(i):
        return rsum_small[:, foffs[i]:foffs[i + 1]]

    cat = _silu_rows(c_all).T

    def ada_cols(i, n):
        full = rows_flat[:, foffs[i]:foffs[i + 1]].reshape(8, 4, n)
        return lax.dynamic_index_in_dim(full, chip, axis=1, keepdims=False)

    g_mix_ada_w = _ada_wgrad(cat, jnp.stack([ada_cols(0, n3), ada_cols(1, n3)]), "ada_mix_wgrad")
    g_ffn_ada_w = _ada_wgrad(cat, jnp.stack([ada_cols(2, n3), ada_cols(3, n3)]), "ada_ffn_wgrad")
    g_kv_ada_w = _ada_wgrad(cat, ada_cols(4, n2)[None], "ada_kv_wgrad")[0]

    def my_cols(v, n):
        return lax.dynamic_index_in_dim(v.reshape(4, n), chip, axis=0, keepdims=False)

    grads = {
        "mix_norm_g": jnp.concatenate([fsum(5), fsum(6)], axis=0),
        "mix_ada_w": g_mix_ada_w,
        "mix_ada_b": jnp.concatenate([fsum(0), fsum(1)], axis=0),
        "ffn_norm_g": jnp.concatenate([fsum(7), fsum(8)], axis=0),
        "ffn_ada_w": g_ffn_ada_w,
        "ffn_ada_b": jnp.concatenate([fsum(2), fsum(3)], axis=0),
        "ffn_w_in": jnp.stack([g_w_in0, g_w_in1]),
        "ffn_w_out": jnp.stack([g_w_out0, g_w_out1]),
        "conv_w_in": g_cw_in[None],
        "conv_b_in": my_cols(jnp.concatenate([fsum(11), fsum(12)], axis=1), 2 * dq_)[None],
        "conv_w_dw": lax.dynamic_index_in_dim(fsum(17).reshape(HALO, 4, dq_), chip, axis=1, keepdims=False)[:CONV_K][None],
        "conv_b_dw": my_cols(fsum(13), dq_)[None],
        "conv_ln_g": my_cols(fsum(14), dq_)[None],
        "conv_ln_b": my_cols(fsum(15), dq_)[None],
        "conv_w_out": g_cw_out[None],
        "conv_b_out": my_cols(fsum(16), dq_)[None],
        "kv_norm_g": fsum(9).reshape(-1),
        "kv_ada_w": g_kv_ada_w,
        "kv_ada_b": fsum(4).reshape(-1),
        "kv_w": g_kvw,
        "forget_b": fsum(18).reshape(-1)[:N_HEADS],
        "attn_w_q": g_wq[None],
        "attn_w_o": g_wo[None],
        "final_norm_g": fsum(10).reshape(-1),
    }
    weights = dict(mix_norm_g=mix_norm_g, mix_ada_w=mix_ada_w, mix_ada_b=mix_ada_b, ffn_norm_g=ffn_norm_g, ffn_ada_w=ffn_ada_w, ffn_ada_b=ffn_ada_b, ffn_w_in=ffn_w_in, ffn_w_out=ffn_w_out, conv_w_in=conv_w_in, conv_b_in=conv_b_in, conv_w_dw=conv_w_dw, conv_b_dw=conv_b_dw, conv_ln_g=conv_ln_g, conv_ln_b=conv_ln_b, conv_w_out=conv_w_out, conv_b_out=conv_b_out, kv_norm_g=kv_norm_g, kv_ada_w=kv_ada_w, kv_ada_b=kv_ada_b, kv_w=kv_w, forget_b=forget_b, attn_w_q=attn_w_q, attn_w_o=attn_w_o, final_norm_g=final_norm_g)
    moms = dict(mix_norm_g=(m_mix_norm_g, v_mix_norm_g), mix_ada_w=(m_mix_ada_w, v_mix_ada_w), mix_ada_b=(m_mix_ada_b, v_mix_ada_b), ffn_norm_g=(m_ffn_norm_g, v_ffn_norm_g), ffn_ada_w=(m_ffn_ada_w, v_ffn_ada_w), ffn_ada_b=(m_ffn_ada_b, v_ffn_ada_b), ffn_w_in=(m_ffn_w_in, v_ffn_w_in), ffn_w_out=(m_ffn_w_out, v_ffn_w_out), conv_w_in=(m_conv_w_in, v_conv_w_in), conv_b_in=(m_conv_b_in, v_conv_b_in), conv_w_dw=(m_conv_w_dw, v_conv_w_dw), conv_b_dw=(m_conv_b_dw, v_conv_b_dw), conv_ln_g=(m_conv_ln_g, v_conv_ln_g), conv_ln_b=(m_conv_ln_b, v_conv_ln_b), conv_w_out=(m_conv_w_out, v_conv_w_out), conv_b_out=(m_conv_b_out, v_conv_b_out), kv_norm_g=(m_kv_norm_g, v_kv_norm_g), kv_ada_w=(m_kv_ada_w, v_kv_ada_w), kv_ada_b=(m_kv_ada_b, v_kv_ada_b), kv_w=(m_kv_w, v_kv_w), forget_b=(m_forget_b, v_forget_b), attn_w_q=(m_attn_w_q, v_attn_w_q), attn_w_o=(m_attn_w_o, v_attn_w_o), final_norm_g=(m_final_norm_g, v_final_norm_g))
    names = list(weights)

    deltas, new_m, new_v = {}, {}, {}
    small_names = [n for n in names if weights[n].size < (1 << 16)]
    for n in names:
        if n in small_names:
            continue
        w = weights[n]
        w2 = w.reshape(-1, w.shape[-1])
        dl, nm, nv = _adamw(w2, grads[n].reshape(w2.shape), moms[n][0].reshape(w2.shape), moms[n][1].reshape(w2.shape), "adamw_" + n)
        deltas[n], new_m[n], new_v[n] = dl.reshape(w.shape), nm.reshape(w.shape), nv.reshape(w.shape)

    def pack_small(get):
        flat = jnp.concatenate([get(n).reshape(-1) for n in small_names])
        rows_ = -(-flat.shape[0] // (8 * LANE)) * 8
        return jnp.pad(flat, (0, rows_ * LANE - flat.shape[0])).reshape(rows_, LANE)

    ws, gs = pack_small(lambda n: weights[n]), pack_small(lambda n: grads[n])
    ms_, vs_ = pack_small(lambda n: moms[n][0]), pack_small(lambda n: moms[n][1])
    vs_ = jnp.where(jnp.arange(vs_.size).reshape(vs_.shape) < sum(weights[n].size for n in small_names), vs_, 1.0)
    dl, nm, nv = _adamw(ws, gs, ms_, vs_, "adamw_small")
    off = 0
    for n in small_names:
        sz = weights[n].size
        shp = weights[n].shape
        deltas[n] = dl.reshape(-1)[off:off + sz].reshape(shp)
        new_m[n] = nm.reshape(-1)[off:off + sz].reshape(shp)
        new_v[n] = nv.reshape(-1)[off:off + sz].reshape(shp)
        off += sz

    grad_out = [grads[n].reshape(weights[n].shape) for n in names]
    return (loss, dx0[None], *grad_out, *[deltas[n] for n in names], *[new_m[n] for n in names], *[new_v[n] for n in names])
```

```python
import functools

import jax
import jax.numpy as jnp
from jax import lax
from jax.experimental import pallas as pl
from jax.experimental.pallas import tpu as pltpu

F32 = jnp.float32
BF16 = jnp.bfloat16
MESH = pl.DeviceIdType.MESH

EPS = 1e-6
N_HEADS = 16
HEAD_DIM = 64
CONV_K = 31
LANE = 128
SUBLANES = 8
HALO = 32
ATT_FWD_TQ = 2048
ATT_TQ = 1024
ATT_TK = 512
NPIECE = 3
SPARE = (HEAD_DIM, 0)
VMEM_MB = 48
ATT_BWD_VMEM_MB = 56

ADAM_LR = 0.001
ADAM_B1 = 0.9
ADAM_B2 = 0.999
ADAM_EPS = 1e-08
ADAM_WD = 0.01
ADAM_STEP = 10


def _sds(shape, dtype):
    return jax.ShapeDtypeStruct(tuple(shape), dtype)


def _cp(sem=None, vmem_mb=VMEM_MB):
    return pltpu.CompilerParams(dimension_semantics=sem, vmem_limit_bytes=vmem_mb << 20)


def _tile(n, pref):
    return pref if n % pref == 0 else n


def _row_tile(r, mult, width=1024):
    cap = max(mult, (512 * 1024 // width) // mult * mult)
    for cand in range(cap, mult - 1, -mult):
        if r % cand == 0:
            return cand
    return r


def _resident(shape):
    nd = len(shape)
    return pl.BlockSpec(tuple(shape), lambda *_: (0,) * nd, pipeline_mode=pl.Buffered(1))


def _dot(a, b):
    return jnp.dot(a, b, preferred_element_type=F32)


def _dot_nt(a, b):
    return lax.dot_general(a, b, (((1,), (1,)), ((), ())), preferred_element_type=F32)


def _dot_tn(a, b):
    return lax.dot_general(a, b, (((0,), (0,)), ((), ())), preferred_element_type=F32)


def _sigmoid(x):
    return 1.0 / (1.0 + jnp.exp(-x))


def _colsum(x):
    return jnp.sum(x, axis=0, keepdims=True)


def _rms_parts(x):
    rstd = lax.rsqrt(jnp.mean(x * x, axis=-1, keepdims=True) + EPS)
    return x * rstd, rstd


class _Gather8:
    def __init__(self, xs):
        self.n = len(xs)
        self.m = [x.shape[0] for x in xs]
        self.land = [_sds((8 * x.shape[0],) + tuple(x.shape[1:]), x.dtype) for x in xs]
        self.sems = [pltpu.SemaphoreType.DMA((7 * self.n,)), pltpu.SemaphoreType.DMA((7 * self.n,)),
                     pltpu.SemaphoreType.DMA((self.n,))]

    def _parts(self, a, x_refs, out_refs, send_sems, recv_sems, local_sems):
        x, y, c = lax.axis_index("x"), lax.axis_index("y"), lax.axis_index("c")
        me, sibling = (x, y, c), (x, y, 1 - c)
        chips = [(1 - x, y), (x, 1 - y), (1 - x, 1 - y)]
        m_per, x_ref, out_ref = self.m[a], x_refs[a], out_refs[a]

        def rows(px, py, pc):
            return out_ref.at[pl.ds((4 * px + 2 * py + pc) * m_per, m_per)]

        def copy(k, block, to, src=None):
            return pltpu.make_async_remote_copy(
                src_ref=rows(*block) if src is None else src, dst_ref=rows(*block),
                send_sem=send_sems.at[7 * a + k], recv_sem=recv_sems.at[7 * a + k], device_id=to, device_id_type=MESH)

        mine = pltpu.make_async_copy(x_ref, rows(*me), local_sems.at[a])
        first = [copy(0, me, sibling, src=x_ref)]
        first += [copy(1 + j, me, (*chip, c), src=x_ref) for j, chip in enumerate(chips)]
        passed = [copy(4 + j, (*chip, c), sibling) for j, chip in enumerate(chips)]
        return c, me, sibling, chips, copy, mine, first, passed

    def start(self, *refs):
        for a in range(self.n):
            _, _, _, _, _, mine, first, _ = self._parts(a, *refs)
            mine.start()
            for cp in first:
                cp.start()

    def finish(self, *refs):
        parts = [self._parts(a, *refs) for a in range(self.n)]
        for j in range(3):
            for c, me, sibling, chips, copy, mine, first, passed in parts:
                copy(1 + j, (*chips[j], c), me).wait_recv()
                passed[j].start()
        for c, me, sibling, chips, copy, mine, first, passed in parts:
            copy(0, sibling, me).wait_recv()
            for j, chip in enumerate(chips):
                copy(4 + j, (*chip, 1 - c), me).wait_recv()
            for cp in first + passed:
                cp.wait_send()
            mine.wait()


class _ScatterChips:
    def __init__(self, qs):
        self.n = len(qs)
        self.land = [_sds((3,) + tuple(q.shape[1:]), q.dtype) for q in qs]
        self.sems = [pltpu.SemaphoreType.DMA((3 * self.n,)), pltpu.SemaphoreType.DMA((3 * self.n,))]

    def _copies(self, q_refs, land_refs, send_sems, recv_sems):
        x, y, c = lax.axis_index("x"), lax.axis_index("y"), lax.axis_index("c")
        chips = [(1 - x, y), (x, 1 - y), (1 - x, 1 - y)]
        return [pltpu.make_async_remote_copy(
            src_ref=q_refs[a].at[2 * cx + cy], dst_ref=land_refs[a].at[k],
            send_sem=send_sems.at[3 * a + k], recv_sem=recv_sems.at[3 * a + k],
            device_id=(cx, cy, c), device_id_type=MESH) for a in range(self.n) for k, (cx, cy) in enumerate(chips)]

    def start(self, *refs):
        for cp in self._copies(*refs):
            cp.start()

    def finish(self, *refs):
        copies = self._copies(*refs)
        for cp in copies:
            cp.wait_recv()
        for cp in copies:
            cp.wait_send()


class _SwapHalves:
    def __init__(self, ps):
        self.n = len(ps)
        self.land = [_sds((p.shape[0],) + tuple(p.shape[2:]), p.dtype) for p in ps]
        self.nb = [p.shape[0] for p in ps]
        tot = sum(self.nb)
        self.sems = [pltpu.SemaphoreType.DMA((tot,)), pltpu.SemaphoreType.DMA((tot,))]

    def _copies(self, p_refs, land_refs, send_sems, recv_sems):
        x, y, c = lax.axis_index("x"), lax.axis_index("y"), lax.axis_index("c")
        out, k = [], 0
        for a in range(self.n):
            for j in range(self.nb[a]):
                out.append(pltpu.make_async_remote_copy(
                    src_ref=p_refs[a].at[j, 1 - c], dst_ref=land_refs[a].at[j], send_sem=send_sems.at[k],
                    recv_sem=recv_sems.at[k], device_id=(x, y, 1 - c), device_id_type=MESH))
                k += 1
        return out

    start = _ScatterChips.start
    finish = _ScatterChips.finish


def _hosted_call(body, comm, *, grid, in_specs, out_specs, out_shape, scratch_shapes, name, sem, args, vmem_mb=VMEM_MB):
    def first():
        return functools.reduce(jnp.logical_and, [pl.program_id(a) == 0 for a in range(len(grid))])

    def last():
        return functools.reduce(jnp.logical_and, [pl.program_id(a) == g - 1 for a, g in enumerate(grid)])

    out_specs = tuple(out_specs) if isinstance(out_specs, (tuple, list)) else (out_specs,)
    out_shape = tuple(out_shape) if isinstance(out_shape, (tuple, list)) else (out_shape,)
    if comm is None:
        return pl.pallas_call(body, grid=grid, in_specs=list(in_specs), out_specs=out_specs, out_shape=out_shape,
                              scratch_shapes=list(scratch_shapes), name=name, compiler_params=_cp(sem, vmem_mb))(*args)
    ex, srcs = comm
    n_in, n_out, n_scr, n_ex = len(in_specs), len(out_shape), len(scratch_shapes), ex.n

    def wrapped(*refs):
        ins, src_refs = refs[:n_in], refs[n_in:n_in + n_ex]
        o0 = n_in + n_ex
        outs, land_refs = refs[o0:o0 + n_out], refs[o0 + n_out:o0 + n_out + n_ex]
        s0 = o0 + n_out + n_ex
        scr, sems = refs[s0:s0 + n_scr], refs[s0 + n_scr:]

        @pl.when(first())
        def _():
            ex.start(src_refs, land_refs, *sems)

        body(*ins, *outs, *scr)

        @pl.when(last())
        def _():
            ex.finish(src_refs, land_refs, *sems)

    hbm = pl.BlockSpec(memory_space=pl.ANY)
    res = pl.pallas_call(
        wrapped, grid=grid, in_specs=[*in_specs, *[hbm] * n_ex], out_specs=(*out_specs, *[hbm] * n_ex),
        out_shape=(*out_shape, *ex.land), scratch_shapes=[*scratch_shapes, *ex.sems], name=name,
        compiler_params=_cp(tuple("arbitrary" for _ in grid), vmem_mb))(*args, *srcs)
    return (*res[:n_out], list(res[n_out:]))


def _exchange(ex, srcs, name, in_vmem=False):
    n = ex.n

    def body(*refs):
        src_refs, land_refs, sems = refs[:n], refs[n:2 * n], refs[2 * n:]
        ex.start(src_refs, land_refs, *sems)
        ex.finish(src_refs, land_refs, *sems)

    spec = pl.BlockSpec(memory_space=pltpu.VMEM if in_vmem else pl.ANY)
    return list(pl.pallas_call(
        body, out_shape=tuple(ex.land), in_specs=[spec] * n, out_specs=tuple([spec] * n),
        scratch_shapes=ex.sems, name=name)(*srcs))


def _allgather8(x_shard, name, in_vmem):
    return _exchange(_Gather8([x_shard]), [x_shard], name, in_vmem)[0]


def _share_halves(bufs):
    n = len(bufs)

    def body(*refs):
        b_refs, out_refs, send_sems, recv_sems = refs[:n], refs[n:2 * n], refs[2 * n], refs[2 * n + 1]
        x, y, c = lax.axis_index("x"), lax.axis_index("y"), lax.axis_index("c")
        copies = [pltpu.make_async_remote_copy(
            src_ref=b_refs[k].at[c], dst_ref=out_refs[k].at[c], send_sem=send_sems.at[k], recv_sem=recv_sems.at[k],
            device_id=(x, y, 1 - c), device_id_type=MESH) for k in range(n)]
        for cp in copies:
            cp.start()
        for cp in copies:
            cp.wait_recv()
        for cp in copies:
            cp.wait_send()

    hbm = pl.BlockSpec(memory_space=pl.ANY)
    return pl.pallas_call(
        body, out_shape=tuple(_sds(b.shape, b.dtype) for b in bufs), in_specs=[hbm] * n, out_specs=tuple([hbm] * n),
        scratch_shapes=[pltpu.SemaphoreType.DMA((n,)), pltpu.SemaphoreType.DMA((n,))],
        input_output_aliases={k: k for k in range(n)}, name="rs_share_halves")(*bufs)


def _add_halves(p, land, sel, name):
    nb, _, r, w = p.shape
    tr = _row_tile(r, 16, w)

    def body(sel_ref, p_ref, l_ref, q16_ref, own_ref):
        q = p_ref[0, 0] + l_ref[0]
        q16_ref[0] = q.astype(BF16)

        @pl.when(pl.program_id(1) == sel_ref[1])
        def _():
            own_ref[...] = q

    gs = pltpu.PrefetchScalarGridSpec(
        num_scalar_prefetch=1, grid=(r // tr, nb),
        in_specs=[pl.BlockSpec((1, 1, tr, w), lambda i, j, sl: (j, sl[0], i, 0)),
                  pl.BlockSpec((1, tr, w), lambda i, j, sl: (j, i, 0))],
        out_specs=(pl.BlockSpec((1, tr, w), lambda i, j, sl: (j, i, 0)), pl.BlockSpec((tr, w), lambda i, j, sl: (i, 0))))
    return pl.pallas_call(body, grid_spec=gs, out_shape=(_sds((nb, r, w), BF16), _sds((r, w), F32)), name=name,
                          compiler_params=_cp(("parallel", "arbitrary")))(sel, p, land)


def _add_chips(own, land, sel, name):
    r, w = own.shape
    tr = _row_tile(r, 16, w)

    def body(sel_ref, q_ref, l_ref, o_ref):
        o_ref[0] = ((q_ref[...] + l_ref[0].astype(F32)) + l_ref[1].astype(F32)) + l_ref[2].astype(F32)

    gs = pltpu.PrefetchScalarGridSpec(
        num_scalar_prefetch=1, grid=(r // tr,),
        in_specs=[pl.BlockSpec((tr, w), lambda i, sl: (i, 0)), pl.BlockSpec((3, tr, w), lambda i, sl: (0, i, 0))],
        out_specs=pl.BlockSpec((1, tr, w), lambda i, sl: (sl[0], i, 0)))
    return pl.pallas_call(body, grid_spec=gs, out_shape=_sds((2, r, w), F32), name=name,
                          compiler_params=_cp(("parallel",)))(sel, own, land)


def _sum8(g):
    _, m, n = g.shape

    def body(g_ref, o_ref):
        acc = g_ref[0]
        for k in range(1, 8):
            acc = acc + g_ref[k]
        o_ref[...] = acc

    return pl.pallas_call(body, out_shape=_sds((m, n), g.dtype), name="sum8")(g)


def _ada_fwd(c_all, w3, name):
    nl, d, n = w3.shape
    tn = 256

    def body(c_ref, w_ref, o_ref):
        cc = c_ref[...]
        ca = (cc * _sigmoid(cc)).astype(BF16)
        o_ref[0] = _dot(ca, w_ref[0].astype(BF16))

    return pl.pallas_call(
        body, grid=(nl, n // tn), out_shape=_sds((nl, 8, n), F32),
        in_specs=[pl.BlockSpec((8, d), lambda l, j: (0, 0)), pl.BlockSpec((1, d, tn), lambda l, j: (l, 0, j))],
        out_specs=pl.BlockSpec((1, 8, tn), lambda l, j: (l, 0, j)),
        name=name, compiler_params=_cp(("parallel", "parallel")))(c_all, w3)


def _in_pair(x, gain, shift, scale, wg, bias, conv, name, comm=None):
    s, d = x.shape
    n = wg.shape[2]
    ts = _tile(s, 512)

    def body(*refs):
        if conv:
            x_ref, g_ref, sh_ref, sc_ref, w_ref, b_ref, h_ref, o_ref, sa_ref, sb_ref = refs
        else:
            x_ref, g_ref, sh_ref, sc_ref, w_ref, h_ref, o_ref, sa_ref, sb_ref = refs
        xhat, _ = _rms_parts(x_ref[...])
        h = ((xhat * g_ref[...]) * (1.0 + sc_ref[...]) + sh_ref[...]).astype(BF16)
        h_ref[...] = h
        for q in range(2):
            a = _dot(h, w_ref[q])
            b = _dot(h, w_ref[q + 2])
            cs = pl.ds(q * n, n)
            if conv:
                a = a + b_ref[q]
                b = b + b_ref[q + 2]
                o_ref[:, cs] = a * _sigmoid(b)
            else:
                o_ref[:, cs] = (a * _sigmoid(a) * b).astype(BF16)
            sa_ref[:, cs] = a.astype(BF16)
            sb_ref[:, cs] = b.astype(BF16)

    vec = pl.BlockSpec((1, d), lambda i: (0, 0))
    in_specs = [pl.BlockSpec((ts, d), lambda i: (i, 0)), vec, vec, vec, _resident(wg.shape)]
    args = [x, gain, shift, scale, wg]
    if conv:
        in_specs.append(_resident(bias.shape))
        args.append(bias)
    tile = pl.BlockSpec((ts, 2 * n), lambda i: (i, 0))
    return _hosted_call(
        body, comm, grid=(s // ts,),
        out_shape=(_sds((s, d), BF16), _sds((s, 2 * n), F32 if conv else BF16), _sds((s, 2 * n), BF16), _sds((s, 2 * n), BF16)),
        in_specs=in_specs, out_specs=(pl.BlockSpec((ts, d), lambda i: (i, 0)), tile, tile, tile),
        scratch_shapes=[], name=name, sem=("parallel",), args=args)


def _shift_copies(buf, shf):
    n = shf.shape[1]
    for r in range(1, SUBLANES):
        shf[r - 1, :, :] = buf[pl.ds(r, n), :]


def _shifted(buf, shf, start, n, cs):
    a, r = divmod(start, SUBLANES)
    if r == 0:
        return buf[pl.ds(start, n), cs]
    return shf[r - 1, pl.ds(a * SUBLANES, n), cs]


def _dwconv_fwd(glu, wdw, bdw, lng, lnb, comm=None):
    s, d = glu.shape
    ts = _tile(s, 256)
    rb, cb = 32, 256

    def body(cur_ref, halo_ref, w_ref, b_ref, g_ref, be_ref, dwo_ref, sw_ref, buf, shf):
        i = pl.program_id(0)

        @pl.when(i == 0)
        def _():
            buf[pl.ds(0, HALO), :] = jnp.zeros((HALO, d), F32)

        @pl.when(i > 0)
        def _():
            buf[pl.ds(0, HALO), :] = halo_ref[...]

        buf[pl.ds(HALO, ts), :] = cur_ref[...]
        _shift_copies(buf, shf)
        for r in range(ts // rb):
            for cc in range(d // cb):
                cs = pl.ds(cc * cb, cb)
                acc = jnp.zeros((rb, cb), F32) + b_ref[:, cs]
                for k in range(CONV_K):
                    acc = acc + w_ref[pl.ds(k, 1), cs] * _shifted(buf, shf, HALO - (CONV_K - 1) + k + r * rb, rb, cs)
                dwo_ref[pl.ds(r * rb, rb), cs] = acc
            rows = pl.ds(r * rb, rb)
            yv = dwo_ref[rows, :]
            mu = jnp.mean(yv, axis=-1, keepdims=True)
            yc = yv - mu
            var = jnp.mean(yc * yc, axis=-1, keepdims=True)
            ln = yc * lax.rsqrt(var + EPS) * g_ref[...] + be_ref[...]
            sw_ref[rows, :] = (ln * _sigmoid(ln)).astype(BF16)

    vec = pl.BlockSpec((1, d), lambda i: (0, 0))
    return _hosted_call(
        body, comm, grid=(s // ts,), out_shape=(_sds((s, d), F32), _sds((s, d), BF16)),
        in_specs=[pl.BlockSpec((ts, d), lambda i: (i, 0)),
                  pl.BlockSpec((HALO, d), lambda i: (jnp.maximum(i * (ts // HALO) - 1, 0), 0)),
                  pl.BlockSpec((HALO, d), lambda i: (0, 0)), vec, vec, vec],
        out_specs=(pl.BlockSpec((ts, d), lambda i: (i, 0)), pl.BlockSpec((ts, d), lambda i: (i, 0))),
        scratch_shapes=[pltpu.VMEM((HALO + ts, d), F32), pltpu.VMEM((SUBLANES - 1, HALO + ts - SUBLANES, d), F32)],
        name="dwconv_fwd", sem=("parallel",),
        args=(glu, glu, wdw, bdw, lng, lnb))


def _mm_res(a, w, b, gate, x, name):
    s, k = a.shape
    d = w.shape[1]
    ts = _tile(s, 512)

    def body(a_ref, w_ref, b_ref, g_ref, x_ref, o_ref):
        yv = _dot(a_ref[...], w_ref[...]) + b_ref[...]
        o_ref[...] = x_ref[...] + g_ref[...] * yv

    vec = pl.BlockSpec((1, d), lambda i: (0, 0))
    return pl.pallas_call(
        body, grid=(s // ts,), out_shape=_sds((s, d), F32),
        in_specs=[pl.BlockSpec((ts, k), lambda i: (i, 0)), _resident((k, d)), vec, vec, pl.BlockSpec((ts, d), lambda i: (i, 0))],
        out_specs=pl.BlockSpec((ts, d), lambda i: (i, 0)),
        name=name, compiler_params=_cp(("parallel",)))(a, w, b, gate, x)


def _qkv(x, kvp, mxp, wk, wv, wf, wq):
    s, d = x.shape
    ts = _tile(s, 512)
    qscale = HEAD_DIM ** -0.5

    def body(x_ref, gk, shk, sck, gm, shm, scm, wk_ref, wv_ref, wf_ref, wq_ref, hk_ref, h1_ref, k_ref, v_ref, q_ref, f_ref):
        xhat, _ = _rms_parts(x_ref[...])
        hk = ((xhat * gk[...]) * (1.0 + sck[...]) + shk[...]).astype(BF16)
        h1 = ((xhat * gm[...]) * (1.0 + scm[...]) + shm[...]).astype(BF16)
        hk_ref[...] = hk
        h1_ref[...] = h1
        k_ref[...] = _dot(hk, wk_ref[...]).astype(BF16)
        v_ref[...] = _dot(hk, wv_ref[...]).astype(BF16)
        f_ref[...] = _dot(hk, wf_ref[...])
        q_ref[...] = (_dot(h1, wq_ref[...]) * qscale).astype(BF16)

    vec = pl.BlockSpec((1, d), lambda i: (0, 0))
    row = pl.BlockSpec((ts, d), lambda i: (i, 0))
    return pl.pallas_call(
        body, grid=(s // ts,),
        out_shape=tuple(_sds((s, d), BF16) for _ in range(5)) + (_sds((s, LANE), F32),),
        in_specs=[row, vec, vec, vec, vec, vec, vec, _resident((d, d)), _resident((d, d)), _resident((d, LANE)), _resident((d, d))],
        out_specs=(row, row, row, row, row, pl.BlockSpec((ts, LANE), lambda i: (i, 0))),
        name="qkv_proj", compiler_params=_cp(("parallel",)))(x, *kvp, *mxp, wk, wv, wf, wq)


def _log_sigmoid(z):
    return jnp.minimum(z, 0.0) - jnp.log(1.0 + jnp.exp(-jnp.abs(z)))


def _cumsum_fwd(flog, fb):
    s = flog.shape[0]
    ts = _tile(s, 256)

    def body(f_ref, b_ref, cum_ref, cumt_ref, carry):
        @pl.when(pl.program_id(0) == 0)
        def _():
            carry[...] = jnp.zeros_like(carry)

        ls = _log_sigmoid(f_ref[...] + b_ref[...])
        r = lax.broadcasted_iota(jnp.int32, (ts, ts), 0)
        cidx = lax.broadcasted_iota(jnp.int32, (ts, ts), 1)
        tri = (cidx <= r).astype(F32)
        cs = jnp.dot(tri, ls, preferred_element_type=F32, precision=lax.Precision.HIGHEST) + carry[...]
        cum_ref[...] = cs
        cumt_ref[...] = cs.T
        carry[...] = cs[ts - 1:ts, :]

    return pl.pallas_call(
        body, grid=(s // ts,), out_shape=(_sds((s, LANE), F32), _sds((LANE, s), F32)),
        in_specs=[pl.BlockSpec((ts, LANE), lambda i: (i, 0)), pl.BlockSpec((1, LANE), lambda i: (0, 0))],
        out_specs=(pl.BlockSpec((ts, LANE), lambda i: (i, 0)), pl.BlockSpec((LANE, ts), lambda i: (0, i))),
        scratch_shapes=[pltpu.VMEM((1, LANE), F32)],
        name="forget_cumsum", compiler_params=_cp(("arbitrary",)))(flog, fb)


def _pick_row(m, idx):
    r = lax.broadcasted_iota(jnp.int32, (m.shape[0], 1), 0)
    return jnp.sum(jnp.where(r == idx, m, 0.0), axis=0, keepdims=True)


def _pick_col(m, idx):
    cidx = lax.broadcasted_iota(jnp.int32, (1, m.shape[1]), 1)
    return jnp.sum(jnp.where(cidx == idx, m, 0.0), axis=1, keepdims=True)


def _split3(x):
    hi = x.astype(BF16)
    r1 = x - hi.astype(F32)
    mid = r1.astype(BF16)
    lo = (r1 - mid.astype(F32)).astype(BF16)
    return hi, mid, lo


def _head_mask(lane, hh):
    lo = lane < HEAD_DIM
    return lo if hh == 0 else jnp.logical_not(lo)


def _attn_prep(k, v, cum):
    s, d = k.shape
    npair = d // LANE
    tc = _tile(s, 1024)

    def body(k_ref, v_ref, c_ref, ka_ref, kt_ref, vt_ref):
        p = pl.program_id(0)
        lane = lax.broadcasted_iota(jnp.int32, (1, LANE), 1)
        kk = k_ref[...]
        vv = v_ref[...].astype(F32)
        ckt = c_ref[...]
        for hh in range(2):
            head = _head_mask(lane, hh)
            b = SPARE[hh]
            ck = _pick_col(ckt, 2 * p + hh)
            extra = jnp.where(lane == b + NPIECE, 1.0, 0.0).astype(BF16) + jnp.zeros((tc, LANE), BF16)
            for n_, pc in enumerate(_split3(ck)):
                extra = jnp.where(lane == b + n_, pc, extra)
            ka = jnp.where(head, kk, extra)
            ka_ref[0, hh] = ka
            kt_ref[0, hh] = ka.astype(F32).T.astype(BF16)
            vx = jnp.where(head, vv, jnp.where(lane == b, 1.0, 0.0))
            vt_ref[0, hh] = vx.T.astype(BF16)

    blk = pl.BlockSpec((tc, LANE), lambda p, c: (c, p))
    return pl.pallas_call(
        body, grid=(npair, s // tc),
        out_shape=(_sds((npair, 2, s, LANE), BF16), _sds((npair, 2, LANE, s), BF16), _sds((npair, 2, LANE, s), BF16)),
        in_specs=[blk, blk, pl.BlockSpec((tc, LANE), lambda p, c: (c, 0))],
        out_specs=(pl.BlockSpec((1, 2, tc, LANE), lambda p, c: (p, 0, c, 0)),
                   pl.BlockSpec((1, 2, LANE, tc), lambda p, c: (p, 0, 0, c)),
                   pl.BlockSpec((1, 2, LANE, tc), lambda p, c: (p, 0, 0, c))),
        name="fox_attn_prep", compiler_params=_cp(("parallel", "parallel")))(k, v, cum)


def _q_aug(qq, lane, hh):
    b = SPARE[hh]
    sel = jnp.logical_and(lane >= b, lane < b + NPIECE)
    neg = jnp.full((1, LANE), -1.0, BF16)
    zl = jnp.zeros((1, LANE), BF16)
    return jnp.where(_head_mask(lane, hh), qq, jnp.where(sel, neg, zl))


def _attn_fwd(q, kaug, vtr, cumt):
    s, d = q.shape
    tq = _tile(s, ATT_FWD_TQ)
    tk = _tile(s, ATT_TK)
    npair = d // LANE
    npart = max(1, tq // tk)

    def body(q_ref, ka_ref, vt_ref, cumt_ref, o_ref, lse_ref):
        p = pl.program_id(0)
        i = pl.program_id(1)
        lane = lax.broadcasted_iota(jnp.int32, (1, LANE), 1)
        qq = q_ref[...]
        qx = (_q_aug(qq, lane, 0), _q_aug(qq, lane, 1))
        cqt = cumt_ref[:, pl.ds(pl.multiple_of(i * tq, tq), tq)]
        cq = (_pick_row(cqt, 2 * p), _pick_row(cqt, 2 * p + 1))
        jd = (i * tq) // tk

        def kv_step(j, carry, diag, q_lo=0):
            ks = pl.multiple_of(j * tk, tk)
            nq_ = tq - q_lo
            if diag:
                krow = lax.broadcasted_iota(jnp.int32, (tk, nq_), 0) + j * tk
                qcol = lax.broadcasted_iota(jnp.int32, (tk, nq_), 1) + (i * tq + q_lo)
                causal = krow <= qcol
            out = []
            for hh in range(2):
                m_all, acc_all = carry[2 * hh], carry[2 * hh + 1]
                m, acc, cqh = m_all[:, q_lo:], acc_all[:, q_lo:], cq[hh][:, q_lo:]
                sc = _dot_nt(ka_ref[0, hh, pl.ds(ks, tk), :], qx[hh][q_lo:, :])
                if diag:
                    sc = jnp.where(causal, sc, -jnp.inf)
                mx = jnp.max(sc, axis=0, keepdims=True) + cqh
                mn = jnp.maximum(m, mx)
                alpha = jnp.exp(m - mn)
                pt = jnp.exp(sc + (cqh - mn)).astype(BF16)
                acc = alpha * acc + _dot(vt_ref[0, hh, :, pl.ds(ks, tk)], pt)
                if q_lo:
                    mn = jnp.concatenate([m_all[:, :q_lo], mn], axis=1)
                    acc = jnp.concatenate([acc_all[:, :q_lo], acc], axis=1)
                out += [mn, acc]
            return tuple(out)

        minit = jnp.full((1, tq), -jnp.inf, F32)
        ainit = jnp.zeros((LANE, tq), F32)
        carry = (minit, ainit, minit, ainit)
        for pj in range(npart):
            carry = kv_step(jd + pj, carry, True, q_lo=pj * tk)
        carry = lax.fori_loop(0, jd, lambda j, cr: kv_step(j, cr, False), carry)
        m0, a0, m1, a1 = carry
        l0 = a0[SPARE[0]:SPARE[0] + 1, :]
        l1 = a1[SPARE[1]:SPARE[1] + 1, :]
        row = lax.broadcasted_iota(jnp.int32, (LANE, 1), 0)
        ot = jnp.where(row < HEAD_DIM, a0 / l0, a1 / l1)
        o_ref[...] = ot.T.astype(BF16)
        r8 = lax.broadcasted_iota(jnp.int32, (8, 1), 0)
        lse_ref[0] = jnp.where(r8 == 0, m0 + jnp.log(l0), jnp.where(r8 == 1, m1 + jnp.log(l1), 0.0))

    return pl.pallas_call(
        body, grid=(npair, s // tq), out_shape=(_sds((s, d), BF16), _sds((npair, 8, s), F32)),
        in_specs=[pl.BlockSpec((tq, LANE), lambda p, i: (i, p)),
                  pl.BlockSpec((1, 2, s, LANE), lambda p, i: (p, 0, 0, 0)),
                  pl.BlockSpec((1, 2, LANE, s), lambda p, i: (p, 0, 0, 0)),
                  pl.BlockSpec((N_HEADS, s), lambda p, i: (0, 0))],
        out_specs=(pl.BlockSpec((tq, LANE), lambda p, i: (i, p)), pl.BlockSpec((1, 8, tq), lambda p, i: (p, 0, i))),
        name="fox_attn_fwd", compiler_params=_cp(("parallel", "parallel")))(q, kaug, vtr, cumt)


def _mm_res_final(a, w, gate, x, gain, target):
    s, k = a.shape
    d = w.shape[1]
    ts = _tile(s, 512)

    def body(a_ref, w_ref, gt_ref, x_ref, g_ref, t_ref, lsum_ref, dx_ref, dg_ref):
        @pl.when(pl.program_id(0) == 0)
        def _():
            lsum_ref[...] = jnp.zeros_like(lsum_ref)
            dg_ref[...] = jnp.zeros_like(dg_ref)

        xv = x_ref[...] + gt_ref[...] * _dot(a_ref[...], w_ref[...])
        xhat, rstd = _rms_parts(xv)
        e = xhat * g_ref[...] - t_ref[...]
        lsum_ref[...] += _colsum(e * e)
        dout = e * (1.0 / d)
        dg_ref[...] += _colsum(dout * xhat)
        dxhat = dout * g_ref[...]
        dx_ref[...] = rstd * (dxhat - xhat * jnp.mean(dxhat * xhat, axis=-1, keepdims=True))

    vec = pl.BlockSpec((1, d), lambda i: (0, 0))
    row = pl.BlockSpec((ts, d), lambda i: (i, 0))
    return pl.pallas_call(
        body, grid=(s // ts,), out_shape=(_sds((1, d), F32), _sds((s, d), F32), _sds((1, d), F32)),
        in_specs=[pl.BlockSpec((ts, k), lambda i: (i, 0)), _resident((k, d)), vec, row, vec, row], out_specs=(vec, row, vec),
        name="ffn1_out_final_loss", compiler_params=_cp(("arbitrary",)))(a, w, gate, x, gain, target)


def _ffn_bwd_act(dx, gate, w_out, ug, uu, name, comm=None):
    s, d = dx.shape
    f = w_out.shape[0]
    n = f // 2
    ts = _tile(s, 512)

    def body(dx_ref, g_ref, w_ref, ug_ref, uu_ref, dug_ref, duu_ref):
        dy = (dx_ref[...] * g_ref[...]).astype(BF16)
        for q in range(2):
            cs = pl.ds(q * n, n)
            dact = _dot_nt(dy, w_ref[cs, :])
            g = ug_ref[:, cs].astype(F32)
            u = uu_ref[:, cs].astype(F32)
            sg = _sigmoid(g)
            dug_ref[:, cs] = (dact * u * sg * (1.0 + g * (1.0 - sg))).astype(BF16)
            duu_ref[:, cs] = (dact * g * sg).astype(BF16)

    tile = pl.BlockSpec((ts, f), lambda i: (i, 0))
    return _hosted_call(
        body, comm, grid=(s // ts,), out_shape=(_sds((s, f), BF16), _sds((s, f), BF16)),
        in_specs=[pl.BlockSpec((ts, d), lambda i: (i, 0)), pl.BlockSpec((1, d), lambda i: (0, 0)),
                  _resident(w_out.shape), tile, tile],
        out_specs=(tile, tile), scratch_shapes=[], name=name, sem=("parallel",), args=(dx, gate, w_out, ug, uu))


def _dw_mm(a, b_list, tk, tn, name, gate=None, wfull=None, dgate_init=None):
    s, kdim = a.shape
    nb1 = b_list[0].shape[1] // tn
    nb = nb1 * len(b_list)
    ts = _tile(s, 1024)
    nk = kdim // tk
    ns = s // ts
    gated = gate is not None

    def body(*refs):
        a_ref = refs[0]
        b_refs = refs[1:1 + len(b_list)]
        rest = refs[1 + len(b_list):]
        if gated:
            g_ref, w_ref, di_ref, o_ref, dg_ref, acc = rest
        else:
            o_ref, acc = rest
        jn, ik, st = pl.program_id(0), pl.program_id(1), pl.program_id(2)

        @pl.when(st == 0)
        def _():
            acc[...] = jnp.zeros_like(acc)

        for mi, b_ref in enumerate(b_refs):
            @pl.when(jn // nb1 == mi)
            def _(b_ref=b_ref):
                acc[...] += _dot_tn(a_ref[...], b_ref[...].astype(BF16))

        if gated:
            @pl.when(jnp.logical_and(ik == 0, st == 0))
            def _():
                dg_ref[...] = di_ref[...]

        @pl.when(st == ns - 1)
        def _():
            if gated:
                o_ref[0] = acc[...] * g_ref[...]
                dg_ref[...] += _colsum(acc[...] * w_ref[...].astype(F32))
            else:
                o_ref[0] = acc[...]

    in_specs = [pl.BlockSpec((ts, tk), lambda jn, ik, st: (st, ik))]
    for mi in range(len(b_list)):
        in_specs.append(pl.BlockSpec(
            (ts, tn), lambda jn, ik, st, mi=mi: (st, jnp.clip(jn - mi * nb1, 0, nb1 - 1))))
    args = [a] + list(b_list)
    out_shape = [_sds((nb, kdim, tn), F32)]
    out_specs = [pl.BlockSpec((1, tk, tn), lambda jn, ik, st: (jn, ik, 0))]
    if gated:
        vec = pl.BlockSpec((1, tn), lambda jn, ik, st: (0, jn))
        in_specs += [vec, pl.BlockSpec((tk, tn), lambda jn, ik, st: (ik, jn)), vec]
        args += [gate, wfull, dgate_init]
        out_shape.append(_sds((1, nb * tn), F32))
        out_specs.append(vec)
    res = pl.pallas_call(
        body, grid=(nb, nk, ns), out_shape=tuple(out_shape), in_specs=in_specs, out_specs=tuple(out_specs),
        scratch_shapes=[pltpu.VMEM((tk, tn), F32)],
        name=name, compiler_params=_cp(("parallel", "arbitrary", "arbitrary")))(*args)
    return res if gated else res[0]


def _mm_normbwd(terms, x, dxres, gain, scale, name, ts_pref=256, comm=None, second=None):
    s, d = x.shape
    ts = _tile(s, ts_pref)
    sets = [(terms, gain, scale)] + ([second] if second is not None else [])
    arrs, warrs = [], []
    for tms, _, _ in sets:
        for a, _, w, _ in tms:
            if not any(a is z for z in arrs):
                arrs.append(a)
            if not any(w is z for z in warrs):
                warrs.append(w)
    na, nw, ns_ = len(arrs), len(warrs), len(sets)

    def body(*refs):
        a_refs, w_refs = refs[:na], refs[na:na + nw]
        x_ref, dr_ref = refs[na + nw], refs[na + nw + 1]
        par = refs[na + nw + 2:na + nw + 2 + 2 * ns_]
        dx_ref = refs[na + nw + 2 + 2 * ns_]
        sums = refs[na + nw + 3 + 2 * ns_:]

        @pl.when(pl.program_id(0) == 0)
        def _():
            for r in sums:
                r[...] = jnp.zeros_like(r)

        xhat, rstd = _rms_parts(x_ref[...])
        dxhat = None
        for k, (tms, _, _) in enumerate(sets):
            g_ref, sc_ref = par[2 * k], par[2 * k + 1]
            dsh_ref, dsc_ref, dg_ref = sums[3 * k:3 * k + 3]
            dh = None
            for a, c0, w, q in tms:
                ai = next(i for i, z in enumerate(arrs) if z is a)
                wi = next(i for i, z in enumerate(warrs) if z is w)
                part = _dot_nt(a_refs[ai][:, pl.ds(c0, w.shape[2])], w_refs[wi][q])
                dh = part if dh is None else dh + part
            dsh_ref[...] += _colsum(dh)
            dsc_ref[...] += _colsum(dh * (xhat * g_ref[...]))
            dn = dh * (1.0 + sc_ref[...])
            dg_ref[...] += _colsum(dn * xhat)
            dxh = dn * g_ref[...]
            dxhat = dxh if dxhat is None else dxhat + dxh
        dx_ref[...] = dr_ref[...] + rstd * (dxhat - xhat * jnp.mean(dxhat * xhat, axis=-1, keepdims=True))

    vec = pl.BlockSpec((1, d), lambda i: (0, 0))
    row = pl.BlockSpec((ts, d), lambda i: (i, 0))
    in_specs = [pl.BlockSpec((ts, a.shape[1]), lambda i: (i, 0)) for a in arrs]
    in_specs += [_resident(w.shape) for w in warrs]
    in_specs += [row, row] + [vec] * (2 * ns_)
    par_args = [p_ for _, g_, s_ in sets for p_ in (g_, s_)]
    return _hosted_call(
        body, comm, grid=(s // ts,), out_shape=(_sds((s, d), F32),) + tuple(_sds((1, d), F32) for _ in range(3 * ns_)),
        in_specs=in_specs, out_specs=(row,) + tuple(vec for _ in range(3 * ns_)), scratch_shapes=[],
        name=name, sem=("arbitrary",), args=(*arrs, *warrs, x, dxres, *par_args))


def _do_kernel(dx, gate, wo, o, comm=None):
    s, d = dx.shape
    ts = _tile(s, 512)

    def body(dx_ref, g_ref, w_ref, o_ref, do_ref, dl_ref):
        dy = (dx_ref[...] * g_ref[...]).astype(BF16)
        do = _dot_nt(dy, w_ref[...])
        do_ref[...] = do.astype(BF16)
        prod = do * o_ref[...].astype(F32)
        hrow = lax.broadcasted_iota(jnp.int32, (N_HEADS, d), 0)
        hcol = lax.broadcasted_iota(jnp.int32, (N_HEADS, d), 1) // HEAD_DIM
        sel = (hrow == hcol).astype(F32)
        dl_ref[...] = lax.dot_general(sel, prod, (((1,), (1,)), ((), ())), preferred_element_type=F32,
                                      precision=lax.Precision.HIGHEST)

    row = pl.BlockSpec((ts, d), lambda i: (i, 0))
    return _hosted_call(
        body, comm, grid=(s // ts,), out_shape=(_sds((s, d), BF16), _sds((N_HEADS, s), F32)),
        in_specs=[row, pl.BlockSpec((1, d), lambda i: (0, 0)), _resident(wo.shape), row],
        out_specs=(row, pl.BlockSpec((N_HEADS, ts), lambda i: (0, i))), scratch_shapes=[],
        name="attn_do", sem=("parallel",), args=(dx, gate, wo, o))


def _attn_bwd(q, do, kaug, kaugt, v, cumt, lse, deltat, comm=None):
    s, d = q.shape
    tq = _tile(s, ATT_TQ)
    tk = _tile(s, ATT_TK)
    assert tq in (tk, 2 * tk)
    npair = d // LANE
    nq = s // tq
    nkb = s // tk
    qscale = HEAD_DIM ** -0.5

    def body(q_ref, do_ref, ka_ref, kt_ref, v_ref, cumt_ref, lse_ref, dl_ref,
             dq_ref, dk_ref, dv_ref, dcq_ref, dck_ref, qaug, dom, rowv, dqt):
        p = pl.program_id(0)
        j = pl.program_id(1)
        lane = lax.broadcasted_iota(jnp.int32, (1, LANE), 1)
        lo = lane < HEAD_DIM
        r8 = lax.broadcasted_iota(jnp.int32, (8, 1), 0)

        @pl.when(j == 0)
        def _():
            dqt[...] = jnp.zeros_like(dqt)
            for c in range(nq):
                rows = pl.ds(c * tq, tq)
                qq = q_ref[rows, :]
                dd = do_ref[rows, :]
                cqt = cumt_ref[:, rows]
                dlt = dl_ref[:, rows]
                lst = lse_ref[0, :, rows]
                for hh in range(2):
                    qaug[hh, rows, :] = _q_aug(qq, lane, hh)
                    dom[hh, rows, :] = jnp.where(_head_mask(lane, hh), dd, jnp.zeros_like(dd))
                    rowv[hh, :, rows] = jnp.where(
                        r8 == 0, _pick_row(cqt, 2 * p + hh) - lst[hh:hh + 1, :],
                        jnp.where(r8 == 1, _pick_row(dlt, 2 * p + hh), 0.0))

        vv = v_ref[...]
        i0 = (j * tk) // tq

        def q_step(qs, nq_, carry, diag):
            dv_acc, dk0, dk1 = carry
            qs = pl.multiple_of(qs, tk)
            if diag:
                krow = lax.broadcasted_iota(jnp.int32, (tk, nq_), 0) + j * tk
                qcol = lax.broadcasted_iota(jnp.int32, (tk, nq_), 1) + qs
                causal = krow <= qcol
            dks = [dk0, dk1]
            for hh in range(2):
                rv = rowv[hh, :, pl.ds(qs, nq_)]
                qa = qaug[hh, pl.ds(qs, nq_), :]
                dh = dom[hh, pl.ds(qs, nq_), :]
                sc = _dot_nt(ka_ref[0, hh], qa)
                if diag:
                    sc = jnp.where(causal, sc, -jnp.inf)
                pt = jnp.exp(sc + rv[0:1, :])
                dpt = _dot_nt(vv, dh)
                dst = (pt * (dpt - rv[1:2, :])).astype(BF16)
                dv_acc = dv_acc + _dot(pt.astype(BF16), dh)
                dks[hh] = dks[hh] + _dot(dst, qa)
                dqt[hh, :, pl.ds(qs, nq_)] += _dot(kt_ref[0, hh], dst)
            return dv_acc, dks[0], dks[1]

        z = jnp.zeros((tk, LANE), F32)
        first = ((j * tk) % tq == 0).astype(jnp.int32)
        carry = lax.fori_loop(0, first, lambda _, cr: q_step(i0 * tq, tq, cr, True), (z, z, z))
        if tq > tk:
            carry = lax.fori_loop(0, 1 - first, lambda _, cr: q_step(j * tk, tq - tk, cr, True), carry)
        dv_acc, dk0, dk1 = lax.fori_loop(i0 + 1, nq, lambda i, cr: q_step(i * tq, tq, cr, False), carry)
        dv_ref[...] = dv_acc.astype(BF16)
        dk_ref[...] = jnp.where(lo, dk0, dk1).astype(BF16)
        dck_ref[0] = jnp.where(r8 == 0, dk0.T[SPARE[0]:SPARE[0] + 1, :],
                               jnp.where(r8 == 1, dk1.T[SPARE[1]:SPARE[1] + 1, :], 0.0))

        @pl.when(j == nkb - 1)
        def _():
            for c in range(nq):
                rows = pl.ds(c * tq, tq)
                a0 = dqt[0, :, rows].T
                a1 = dqt[1, :, rows].T
                dq_ref[rows, :] = (jnp.where(lo, a0, a1) * qscale).astype(BF16)
            r0, r1 = SPARE[0] + NPIECE, SPARE[1] + NPIECE
            dcq_ref[0] = jnp.where(r8 == 0, dqt[0, r0:r0 + 1, :], jnp.where(r8 == 1, dqt[1, r1:r1 + 1, :], 0.0))

    col = pl.BlockSpec((s, LANE), lambda p, j: (0, p), pipeline_mode=pl.Buffered(1))
    rows16 = pl.BlockSpec((N_HEADS, s), lambda p, j: (0, 0), pipeline_mode=pl.Buffered(1))
    blk = pl.BlockSpec((tk, LANE), lambda p, j: (j, p))
    return _hosted_call(
        body, comm, grid=(npair, nkb),
        out_shape=(_sds((s, d), BF16), _sds((s, d), BF16), _sds((s, d), BF16), _sds((npair, 8, s), F32), _sds((npair, 8, s), F32)),
        in_specs=[col, col, pl.BlockSpec((1, 2, tk, LANE), lambda p, j: (p, 0, j, 0)),
                  pl.BlockSpec((1, 2, LANE, tk), lambda p, j: (p, 0, 0, j)), blk, rows16,
                  pl.BlockSpec((1, 8, s), lambda p, j: (p, 0, 0), pipeline_mode=pl.Buffered(1)), rows16],
        out_specs=(pl.BlockSpec((s, LANE), lambda p, j: (0, p)), blk, blk,
                   pl.BlockSpec((1, 8, s), lambda p, j: (p, 0, 0)), pl.BlockSpec((1, 8, tk), lambda p, j: (p, 0, j))),
        scratch_shapes=[pltpu.VMEM((2, s, LANE), BF16), pltpu.VMEM((2, s, LANE), BF16), pltpu.VMEM((2, 8, s), F32),
                        pltpu.VMEM((2, LANE, s), F32)],
        name="fox_attn_bwd", sem=("arbitrary", "arbitrary"), vmem_mb=ATT_BWD_VMEM_MB,
        args=(q, do, kaug, kaugt, v, cumt, lse, deltat))


def _cumsum_bwd(dcq, dck, flog, fb):
    s = flog.shape[0]
    ts = _tile(s, 256)
    nt = s // ts

    def body(dq_ref, dk_ref, f_ref, b_ref, df_ref, db_ref, carry):
        @pl.when(pl.program_id(0) == 0)
        def _():
            carry[...] = jnp.zeros_like(carry)
            db_ref[...] = jnp.zeros_like(db_ref)

        r = lax.broadcasted_iota(jnp.int32, (ts, ts), 0)
        cidx = lax.broadcasted_iota(jnp.int32, (ts, ts), 1)
        tri = (r >= cidx).astype(F32)
        dct = dq_ref[...] + dk_ref[...]
        dlst = jnp.dot(dct, tri, preferred_element_type=F32, precision=lax.Precision.HIGHEST) + carry[...]
        carry[...] = dlst[:, 0:1]
        dls = jnp.concatenate([dlst, jnp.zeros((LANE - N_HEADS, ts), F32)], axis=0).T
        z = f_ref[...] + b_ref[...]
        df = dls * (1.0 / (1.0 + jnp.exp(z)))
        db_ref[...] += _colsum(df)
        df_ref[...] = df.astype(BF16)

    rev = pl.BlockSpec((ts, LANE), lambda i: (nt - 1 - i, 0))
    revt = pl.BlockSpec((N_HEADS, ts), lambda i: (0, nt - 1 - i))
    vec = pl.BlockSpec((1, LANE), lambda i: (0, 0))
    return pl.pallas_call(
        body, grid=(nt,), out_shape=(_sds((s, LANE), BF16), _sds((1, LANE), F32)),
        in_specs=[revt, revt, rev, vec], out_specs=(rev, vec), scratch_shapes=[pltpu.VMEM((N_HEADS, 1), F32)],
        name="forget_cumsum_bwd", compiler_params=_cp(("arbitrary",)))(dcq, dck, flog, fb)


def _conv_bwd1(dx, gate, w_out, b_out, dwo, lng, lnb, comm=None):
    s, d = dx.shape
    ts = _tile(s, 512)
    ns = s // ts

    def body(dx_ref, g_ref, w_ref, bo_ref, y_ref, lg_ref, lb_ref, dd_ref, dlg_ref, dlb_ref, dbd_ref, dbo_ref, dge_ref, cs):
        i = pl.program_id(0)

        @pl.when(i == 0)
        def _():
            for r in (dlg_ref, dlb_ref, dbd_ref, cs):
                r[...] = jnp.zeros_like(r)

        dxv = dx_ref[...]
        cs[...] += _colsum(dxv)
        dsw = _dot_nt((dxv * g_ref[...]).astype(BF16), w_ref[...])
        yv = y_ref[...]
        mu = jnp.mean(yv, axis=-1, keepdims=True)
        yc = yv - mu
        rstd = lax.rsqrt(jnp.mean(yc * yc, axis=-1, keepdims=True) + EPS)
        xhat = yc * rstd
        ln = xhat * lg_ref[...] + lb_ref[...]
        sg = _sigmoid(ln)
        dln = dsw * (sg * (1.0 + ln * (1.0 - sg)))
        dlg_ref[...] += _colsum(dln * xhat)
        dlb_ref[...] += _colsum(dln)
        dxh = dln * lg_ref[...]
        dd = rstd * (dxh - jnp.mean(dxh, axis=-1, keepdims=True) - xhat * jnp.mean(dxh * xhat, axis=-1, keepdims=True))
        dbd_ref[...] += _colsum(dd)
        dd_ref[...] = dd

        @pl.when(i == ns - 1)
        def _():
            dbo_ref[...] = g_ref[...] * cs[...]
            dge_ref[...] = bo_ref[...] * cs[...]

    vec = pl.BlockSpec((1, d), lambda i: (0, 0))
    row = pl.BlockSpec((ts, d), lambda i: (i, 0))
    return _hosted_call(
        body, comm, grid=(ns,), out_shape=(_sds((s, d), F32),) + tuple(_sds((1, d), F32) for _ in range(5)),
        in_specs=[row, vec, _resident(w_out.shape), vec, row, vec, vec], out_specs=(row, vec, vec, vec, vec, vec),
        scratch_shapes=[pltpu.VMEM((1, d), F32)],
        name="conv_bwd_ln", sem=("arbitrary",), args=(dx, gate, w_out, b_out, dwo, lng, lnb))


def _dwconv_bwd(ddwo, glu, a_s, g_s, wdw, comm=None):
    s, d = ddwo.shape
    ts = _tile(s, 256)
    ns = s // ts
    rb, cb = 32, 256
    nrb = ts // rb

    def body(dd_ref, ddn_ref, gl_ref, glh_ref, a_ref, g_ref, w_ref, da_ref, dg_ref, dw_ref, sa_ref, sg_ref, bufd, bufg, dws,
             shd, shg):
        i = pl.program_id(0)

        @pl.when(i == 0)
        def _():
            dws[...] = jnp.zeros_like(dws)
            sa_ref[...] = jnp.zeros_like(sa_ref)
            sg_ref[...] = jnp.zeros_like(sg_ref)
            bufg[pl.ds(0, HALO), :] = jnp.zeros((HALO, d), F32)

        @pl.when(i > 0)
        def _():
            bufg[pl.ds(0, HALO), :] = glh_ref[...]

        bufg[pl.ds(HALO, ts), :] = gl_ref[...]
        bufd[pl.ds(0, ts), :] = dd_ref[...]

        @pl.when(i == ns - 1)
        def _():
            bufd[pl.ds(ts, HALO), :] = jnp.zeros((HALO, d), F32)

        @pl.when(i < ns - 1)
        def _():
            bufd[pl.ds(ts, HALO), :] = ddn_ref[...]

        _shift_copies(bufd, shd)
        _shift_copies(bufg, shg)
        for cc in range(d // cb):
            cs = pl.ds(cc * cb, cb)
            for r in range(nrb):
                acc = jnp.zeros((rb, cb), F32)
                for k in range(CONV_K):
                    acc = acc + w_ref[pl.ds(k, 1), cs] * _shifted(bufd, shd, r * rb + (CONV_K - 1) - k, rb, cs)
                rows = pl.ds(r * rb, rb)
                av = a_ref[rows, cs].astype(F32)
                sg = _sigmoid(g_ref[rows, cs].astype(F32))
                dav = acc * sg
                dgv = acc * av * sg * (1.0 - sg)
                da_ref[rows, cs] = dav.astype(BF16)
                dg_ref[rows, cs] = dgv.astype(BF16)
                sa_ref[:, cs] += _colsum(dav)
                sg_ref[:, cs] += _colsum(dgv)
            for k in range(CONV_K):
                acc8 = jnp.zeros((8, cb), F32)
                for r in range(nrb):
                    prod = bufd[pl.ds(r * rb, rb), cs] * _shifted(bufg, shg, HALO - (CONV_K - 1) + k + r * rb, rb, cs)
                    acc8 = acc8 + (prod[0:8] + prod[8:16]) + (prod[16:24] + prod[24:32])
                dws[pl.ds(8 * k, 8), cs] += acc8

        @pl.when(i == ns - 1)
        def _():
            dw_ref[...] = jnp.zeros_like(dw_ref)
            for k in range(CONV_K):
                dw_ref[pl.ds(k, 1), :] = _colsum(dws[pl.ds(8 * k, 8), :])

    row = pl.BlockSpec((ts, d), lambda i: (i, 0))
    vec = pl.BlockSpec((1, d), lambda i: (0, 0))
    hb = ts // HALO
    return _hosted_call(
        body, comm, grid=(ns,),
        out_shape=(_sds((s, d), BF16), _sds((s, d), BF16), _sds((HALO, d), F32), _sds((1, d), F32), _sds((1, d), F32)),
        in_specs=[row, pl.BlockSpec((HALO, d), lambda i: (jnp.minimum((i + 1) * hb, ns * hb - 1), 0)),
                  row, pl.BlockSpec((HALO, d), lambda i: (jnp.maximum(i * hb - 1, 0), 0)),
                  row, row, pl.BlockSpec((HALO, d), lambda i: (0, 0))],
        out_specs=(row, row, pl.BlockSpec((HALO, d), lambda i: (0, 0)), vec, vec),
        scratch_shapes=[pltpu.VMEM((ts + HALO, d), F32), pltpu.VMEM((HALO + ts, d), F32), pltpu.VMEM((8 * HALO, d), F32),
                        pltpu.VMEM((SUBLANES - 1, HALO + ts - SUBLANES, d), F32),
                        pltpu.VMEM((SUBLANES - 1, HALO + ts - SUBLANES, d), F32)],
        name="dwconv_bwd", sem=("arbitrary",), args=(ddwo, ddwo, glu, glu, a_s, g_s, wdw))


def _ada_wgrad(cat, da, name):
    nl, _, n = da.shape
    d = cat.shape[0]
    tn = 256

    def body(c_ref, d_ref, o_ref):
        acc = c_ref[:, 0:1] * d_ref[0, 0:1, :]
        for r in range(1, 8):
            acc = acc + c_ref[:, r:r + 1] * d_ref[0, r:r + 1, :]
        o_ref[0] = acc

    return pl.pallas_call(
        body, grid=(nl, n // tn), out_shape=_sds((nl, d, n), F32),
        in_specs=[pl.BlockSpec((d, 8), lambda l, j: (0, 0)), pl.BlockSpec((1, 8, tn), lambda l, j: (l, 0, j))],
        out_specs=pl.BlockSpec((1, d, tn), lambda l, j: (l, 0, j)),
        name=name, compiler_params=_cp(("parallel", "parallel")))(cat, da)


def _silu_rows(c_all):
    def body(c_ref, o_ref):
        cc = c_ref[...]
        o_ref[...] = cc * _sigmoid(cc)

    return pl.pallas_call(body, out_shape=_sds(c_all.shape, F32), name="silu_c")(c_all)


def _adamw(w, g, m, v, name):
    r, c = w.shape
    tr = r
    for cand in (512, 256, 128, 64, 32, 16, 8):
        if r % cand == 0 and cand * c * 4 <= (1 << 20):
            tr = cand
            break
    bc1 = 1.0 - ADAM_B1 ** ADAM_STEP
    bc2 = 1.0 - ADAM_B2 ** ADAM_STEP

    def body(w_ref, g_ref, m_ref, v_ref, d_ref, nm_ref, nv_ref):
        gv = g_ref[...]
        mn = ADAM_B1 * m_ref[...] + (1.0 - ADAM_B1) * gv
        vn = ADAM_B2 * v_ref[...] + (1.0 - ADAM_B2) * (gv * gv)
        mh = mn / bc1
        vh = vn / bc2
        d_ref[...] = -ADAM_LR * (mh / (jnp.sqrt(vh) + ADAM_EPS) + ADAM_WD * w_ref[...])
        nm_ref[...] = mn
        nv_ref[...] = vn

    blk = pl.BlockSpec((tr, c), lambda i: (i, 0))
    return pl.pallas_call(
        body, grid=(r // tr,), out_shape=tuple(_sds((r, c), F32) for _ in range(3)),
        in_specs=[blk, blk, blk, blk], out_specs=(blk, blk, blk),
        name=name, compiler_params=_cp(("parallel",)))(w, g, m, v)


def _pad_rows(a, rows, axis):
    pad = [(0, 0)] * a.ndim
    pad[axis] = (0, rows - a.shape[axis])
    return jnp.pad(a, pad)


def _vec(a):
    return a.reshape(1, -1)


def kernel(x, c, mix_norm_g, mix_ada_w, mix_ada_b, ffn_norm_g, ffn_ada_w, ffn_ada_b, ffn_w_in, ffn_w_out, conv_w_in, conv_b_in, conv_w_dw, conv_b_dw, conv_ln_g, conv_ln_b, conv_w_out, conv_b_out, kv_norm_g, kv_ada_w, kv_ada_b, kv_w, forget_b, attn_w_q, attn_w_o, final_norm_g, loss_target, m_mix_norm_g, m_mix_ada_w, m_mix_ada_b, m_ffn_norm_g, m_ffn_ada_w, m_ffn_ada_b, m_ffn_w_in, m_ffn_w_out, m_conv_w_in, m_conv_b_in, m_conv_w_dw, m_conv_b_dw, m_conv_ln_g, m_conv_ln_b, m_conv_w_out, m_conv_b_out, m_kv_norm_g, m_kv_ada_w, m_kv_ada_b, m_kv_w, m_forget_b, m_attn_w_q, m_attn_w_o, m_final_norm_g, v_mix_norm_g, v_mix_ada_w, v_mix_ada_b, v_ffn_norm_g, v_ffn_ada_w, v_ffn_ada_b, v_ffn_w_in, v_ffn_w_out, v_conv_w_in, v_conv_b_in, v_conv_w_dw, v_conv_b_dw, v_conv_ln_g, v_conv_ln_b, v_conv_w_out, v_conv_b_out, v_kv_norm_g, v_kv_ada_w, v_kv_ada_b, v_kv_w, v_forget_b, v_attn_w_q, v_attn_w_o, v_final_norm_g):
    xi, yi, ci = lax.axis_index("x"), lax.axis_index("y"), lax.axis_index("c")
    chip = 2 * xi + yi
    dev = 4 * xi + 2 * yi + ci
    s, d = x.shape[1], x.shape[2]
    f = ffn_w_out.shape[1] * 4
    x0 = x[0]
    nkv = kv_w.shape[1]
    nkv_all = 4 * nkv

    def my_halves(ws):
        return [lax.dynamic_index_in_dim(w.astype(BF16).reshape(2, w.shape[0] // 2, w.shape[1]), ci, axis=0, keepdims=False)
                for w in ws]

    def whole(gath, ws):
        return [g.reshape(4, w.shape[0], w.shape[1]) for g, w in zip(gath, ws)]

    wdw_loc = _pad_rows(conv_w_dw[0], HALO, 0)
    small = jnp.concatenate([c.reshape(-1), conv_b_in.reshape(-1), wdw_loc.reshape(-1), conv_b_dw.reshape(-1),
                             conv_ln_g.reshape(-1), conv_ln_b.reshape(-1), conv_b_out.reshape(-1)])
    n_small = small.shape[0]
    w_small = -(-n_small // (8 * LANE)) * LANE
    small = jnp.pad(small, (0, 8 * w_small - n_small)).reshape(8, w_small)
    first = [small] + my_halves([conv_w_in[0]])
    small_all, cw_in = _exchange(_Gather8(first), first, "ag_small_params_w_conv")
    small_all = small_all.reshape(8, 8 * w_small)
    cw_in, = whole([cw_in], [conv_w_in[0]])
    c_all = small_all[:, :d]
    per_chip = small_all[0::2]
    dq_ = d // 4
    o1 = d
    b_in_full = per_chip[:, o1:o1 + 2 * dq_].reshape(4, 1, 2 * dq_)
    o1 += 2 * dq_
    wdw_full = per_chip[:, o1:o1 + HALO * dq_].reshape(4, HALO, dq_).transpose(1, 0, 2).reshape(HALO, d)
    o1 += HALO * dq_
    bdw_full = per_chip[:, o1:o1 + dq_].reshape(1, d)
    lng_full = per_chip[:, o1 + dq_:o1 + 2 * dq_].reshape(1, d)
    lnb_full = per_chip[:, o1 + 2 * dq_:o1 + 3 * dq_].reshape(1, d)
    bout_full = per_chip[:, o1 + 3 * dq_:o1 + 4 * dq_].reshape(1, d)

    a_mix = _ada_fwd(c_all, mix_ada_w, "ada_mix")
    a_ffn = _ada_fwd(c_all, ffn_ada_w, "ada_ffn")
    a_kv = _ada_fwd(c_all, kv_ada_w[None], "ada_kv")
    n3 = mix_ada_w.shape[2]
    n2 = kv_ada_w.shape[1]
    ada_loc = jnp.concatenate([a_mix[0], a_mix[1], a_ffn[0], a_ffn[1], a_kv[0]], axis=1)
    w_ada = ada_loc.shape[1]
    ada_all = _allgather8(ada_loc, "ag_ada", True).reshape(8, 8, w_ada)
    ada_me = lax.dynamic_index_in_dim(ada_all, dev, axis=1, keepdims=False)[0::2]

    def ada_vec(off, n, bias):
        return ada_me[:, off:off + n].reshape(1, 4 * n) + bias.reshape(1, -1)

    ada_m0 = ada_vec(0, n3, mix_ada_b[0])
    ada_m1 = ada_vec(n3, n3, mix_ada_b[1])
    ada_f0 = ada_vec(2 * n3, n3, ffn_ada_b[0])
    ada_f1 = ada_vec(3 * n3, n3, ffn_ada_b[1])
    ada_k = ada_vec(4 * n3, n2, kv_ada_b)

    def split3(a):
        return a[:, :d], a[:, d:2 * d], a[:, 2 * d:3 * d]

    sh_m0, sc_m0, gt_m0 = split3(ada_m0)
    sh_m1, sc_m1, gt_m1 = split3(ada_m1)
    sh_f0, sc_f0, gt_f0 = split3(ada_f0)
    sh_f1, sc_f1, gt_f1 = split3(ada_f1)
    sh_k, sc_k = ada_k[:, :d], ada_k[:, d:2 * d]

    grp_a = [ffn_w_in[0]]
    grp_b = [ffn_w_out[0], conv_w_out[0], ffn_w_in[1]]
    grp_c = [ffn_w_out[1], kv_w, attn_w_q[0], attn_w_o[0]]
    mine_a, mine_b, mine_c = my_halves(grp_a), my_halves(grp_b), my_halves(grp_c)

    zero_b = jnp.zeros((1, d), F32)
    g_m0, g_m1 = _vec(mix_norm_g[0]), _vec(mix_norm_g[1])
    g_f0, g_f1 = _vec(ffn_norm_g[0]), _vec(ffn_norm_g[1])
    g_k, g_fin = _vec(kv_norm_g), _vec(final_norm_g)
    fb = jnp.pad(forget_b, (0, LANE - N_HEADS)).reshape(1, LANE)

    h0, glu, a_s, g_s, gath_a = _in_pair(x0, g_m0, sh_m0, sc_m0, cw_in, b_in_full, True, "conv_in",
                                         comm=(_Gather8(mine_a), mine_a))
    dwo, sw, gath_b = _dwconv_fwd(glu, wdw_full, bdw_full, lng_full, lnb_full, comm=(_Gather8(mine_b), mine_b))
    w_in0, = whole(gath_a, grp_a)
    w_out0, cw_out, w_in1 = whole(gath_b, grp_b)
    cw_out = cw_out.reshape(d, d)
    w_in = [w_in0, w_in1]
    x1 = _mm_res(sw, cw_out, bout_full, gt_m0, x0, "conv_out")
    hf0, act0, ug0, uu0, gath_c = _in_pair(x1, g_f0, sh_f0, sc_f0, w_in[0], None, False, "ffn0_in",
                                           comm=(_Gather8(mine_c), mine_c))
    w_out1, kvw, wq, wo = whole(gath_c, grp_c)
    w_out = [w_out0.reshape(f, d), w_out1.reshape(f, d)]
    kvw = kvw.transpose(1, 0, 2).reshape(d, nkv_all)
    wk, wv = kvw[:, :d], kvw[:, d:2 * d]
    wf = jnp.pad(kvw[:, 2 * d:], ((0, 0), (0, LANE - N_HEADS)))
    wq, wo = wq.reshape(d, d), wo.reshape(d, d)
    x2 = _mm_res(act0, w_out[0], zero_b, gt_f0, x1, "ffn0_out")
    hk, h1, kk, vv, qq, flog = _qkv(x2, (g_k, sh_k, sc_k), (g_m1, sh_m1, sc_m1), wk, wv, wf, wq)
    cum, cumt = _cumsum_fwd(flog, fb)
    kaug, kaugt, vtr = _attn_prep(kk, vv, cum)
    o, lse = _attn_fwd(qq, kaug, vtr, cumt)
    x3 = _mm_res(o, wo, zero_b, gt_m1, x2, "attn_out")
    hf1, act1, ug1, uu1 = _in_pair(x3, g_f1, sh_f1, sc_f1, w_in[1], None, False, "ffn1_in")
    lsum, dx4, d_gfin = _mm_res_final(act1, w_out[1], gt_f1, x3, g_fin, loss_target[0])
    loss = lax.psum(0.5 / d * jnp.sum(lsum), ("x", "y", "c"))

    nf = f // 2

    sel = jnp.stack([ci, chip]).astype(jnp.int32)

    def halves_of(gs):
        return [g.reshape(4, 2, g.shape[1] // 2, g.shape[2]) for g in gs]

    def reduce_adds(ps, lands, tag):
        pairs = [_add_halves(p_, l_, sel, f"{tag}_add{k}") for k, (p_, l_) in enumerate(zip(ps, lands))]
        return [q for q, _ in pairs], [o_ for _, o_ in pairs]

    def reduce_begin(gs, tag):
        ps = halves_of(gs)
        return reduce_adds(ps, _exchange(_SwapHalves(ps), ps, tag + "_swap"), tag)

    def reduce_sum(owns, lands, tag):
        return [_add_chips(o_, l_, sel, f"{tag}_sum{k}") for k, (o_, l_) in enumerate(zip(owns, lands))]

    def ffn_bwd(dx_out, x_in, hf, act, ug, uu, gain, scale, gate, w_in_l, w_out_l, tag, comm=None):
        res = _ffn_bwd_act(dx_out, gate, w_out_l, ug, uu, tag + "_bwd_act", comm=comm)
        dug, duu = res[0], res[1]
        dw_out, dgate = _dw_mm(act, [dx_out], nf, d, tag + "_dw_out", gate=gate, wfull=w_out_l, dgate_init=zero_b)
        terms = [(dug, 0, w_in_l, 0), (dug, nf, w_in_l, 1), (duu, 0, w_in_l, 2), (duu, nf, w_in_l, 3)]
        dx_in, dsh, dsc, dgn = _mm_normbwd(terms, x_in, dx_out, gain, scale, tag + "_bwd_in")
        dw_in = _dw_mm(hf, [dug, duu], d, nf, tag + "_dw_in")
        return dx_in, dw_in, dw_out[0], dsh, dsc, dgate, dgn, (res[2] if comm is not None else None)

    dx3, dw_in1, dw_out1, dsh_f1, dsc_f1, dgt_f1, dgn_f1, _ = ffn_bwd(dx4, x3, hf1, act1, ug1, uu1, g_f1, sc_f1, gt_f1, w_in[1], w_out[1], "ffn1")
    ps_1 = halves_of([dw_in1, dw_out1.reshape(4, f // 4, d)])
    do, deltat, swapped_1 = _do_kernel(dx3, gt_m1, wo, o, comm=(_SwapHalves(ps_1), ps_1))
    q16_1, own_1 = reduce_adds(ps_1, swapped_1, "rs_ffn1")
    dwo_att, dgt_m1 = _dw_mm(o, [dx3], d, d, "attn_dw_o", gate=gt_m1, wfull=wo, dgate_init=zero_b)
    dq, dk, dv, dcq, dck, land_1 = _attn_bwd(qq, do, kaug, kaugt, vv, cumt, lse, deltat, comm=(_ScatterChips(q16_1), q16_1))
    dwq = _dw_mm(h1, [dq], d, d, "attn_dw_q")[0]

    df, dfb = _cumsum_bwd(dcq[:, :2].reshape(N_HEADS, s), dck[:, :2].reshape(N_HEADS, s), flog, fb)
    dwk, dwv = _dw_mm(hk, [dk, dv], d, d, "kv_dw_kv")
    dwf = _dw_mm(hk, [df], d, LANE, "kv_dw_f")[0]
    dkvw = jnp.concatenate([dwk, dwv, dwf[:, :N_HEADS]], axis=1)
    dkvw = dkvw.reshape(d, 4, nkv).transpose(1, 0, 2)
    ps_2 = halves_of([dkvw, dwq.reshape(4, d // 4, d), dwo_att[0].reshape(4, d // 4, d)])
    terms = [(dk, 0, wk.reshape(1, d, d), 0), (dv, 0, wv.reshape(1, d, d), 0), (df, 0, wf.reshape(1, d, LANE), 0)]
    dx2, dsh_m1, dsc_m1, dgn_m1, dsh_k, dsc_k, dgn_k, swapped_2 = _mm_normbwd(
        [(dq, 0, wq.reshape(1, d, d), 0)], x2, dx3, g_m1, sc_m1, "attn_kv_bwd", second=(terms, g_k, sc_k),
        comm=(_SwapHalves(ps_2), ps_2))
    q16_2, own_2 = reduce_adds(ps_2, swapped_2, "rs_attn")
    dx1, dw_in0, dw_out0, dsh_f0, dsc_f0, dgt_f0, dgn_f0, land_2 = ffn_bwd(
        dx2, x1, hf0, act0, ug0, uu0, g_f0, sc_f0, gt_f0, w_in[0], w_out[0], "ffn0", comm=(_ScatterChips(q16_2), q16_2))

    ps_3 = halves_of([dw_in0, dw_out0.reshape(4, f // 4, d)])
    ddwo, d_lng, d_lnb, d_bdw, d_bout, dgt_extra, swapped_3 = _conv_bwd1(
        dx1, gt_m0, cw_out, bout_full, dwo, lng_full, lnb_full, comm=(_SwapHalves(ps_3), ps_3))
    q16_3, own_3 = reduce_adds(ps_3, swapped_3, "rs_ffn0")
    dcw_out, dgt_m0 = _dw_mm(sw, [dx1], d, d, "conv_dw_out", gate=gt_m0, wfull=cw_out, dgate_init=dgt_extra)
    da, dg, d_wdw, d_bin_a, d_bin_g, land_3 = _dwconv_bwd(ddwo, glu, a_s, g_s, wdw_full, comm=(_ScatterChips(q16_3), q16_3))
    nc = cw_in.shape[2]
    terms = [(da, 0, cw_in, 0), (da, nc, cw_in, 1), (dg, 0, cw_in, 2), (dg, nc, cw_in, 3)]
    dx0, dsh_m0, dsc_m0, dgn_m0 = _mm_normbwd(terms, x0, dx1, g_m0, sc_m0, "conv_bwd_in")
    dcw_in = _dw_mm(h0, [da, dg], d, nc, "conv_dw_in")
    q16_4, own_4 = reduce_begin([dcw_out[0].reshape(4, d // 4, d), dcw_in], "rs_conv")
    land_4 = _exchange(_ScatterChips(q16_4), q16_4, "rs_conv_scatter")

    sums = (reduce_sum(own_1, land_1, "rs_ffn1") + reduce_sum(own_2, land_2, "rs_attn")
            + reduce_sum(own_3, land_3, "rs_ffn0") + reduce_sum(own_4, land_4, "rs_conv"))
    reduced = [b.reshape(2 * b.shape[1], b.shape[2]) for b in _share_halves(sums)]
    g_w_in1, g_w_out1, g_kvw, g_wq, g_wo, g_w_in0, g_w_out0, g_cw_out, g_cw_in = reduced

    d_ada = [jnp.concatenate([dsh_m0, dsc_m0, dgt_m0], axis=1), jnp.concatenate([dsh_m1, dsc_m1, dgt_m1], axis=1),
             jnp.concatenate([dsh_f0, dsc_f0, dgt_f0], axis=1), jnp.concatenate([dsh_f1, dsc_f1, dgt_f1], axis=1),
             jnp.concatenate([dsh_k, dsc_k], axis=1)]
    fields = d_ada + [dgn_m0, dgn_m1, dgn_f0, dgn_f1, dgn_k, d_gfin, d_bin_a, d_bin_g, d_bdw, d_lng, d_lnb, d_bout,
                      d_wdw.reshape(1, -1), dfb]
    foffs = [0]
    for fl in fields:
        foffs.append(foffs[-1] + fl.shape[1])
    n_row = foffs[-1]
    w_row = -(-n_row // (8 * LANE)) * LANE
    row = jnp.pad(jnp.concatenate(fields, axis=1), ((0, 0), (0, 8 * w_row - n_row))).reshape(8, w_row)
    rows_all = _allgather8(row, "ag_small_grads", True).reshape(8, 8, w_row)
    rsum_small = _sum8(rows_all).reshape(1, 8 * w_row)
    rows_flat = rows_all.reshape(8, 8 * w_row)

    def fsum(i):
        return rsum_small[:, foffs[i]:foffs[i + 1]]

    cat = _silu_rows(c_all).T

    def ada_cols(i, n):
        full = rows_flat[:, foffs[i]:foffs[i + 1]].reshape(8, 4, n)
        return lax.dynamic_index_in_dim(full, chip, axis=1, keepdims=False)

    g_mix_ada_w = _ada_wgrad(cat, jnp.stack([ada_cols(0, n3), ada_cols(1, n3)]), "ada_mix_wgrad")
    g_ffn_ada_w = _ada_wgrad(cat, jnp.stack([ada_cols(2, n3), ada_cols(3, n3)]), "ada_ffn_wgrad")
    g_kv_ada_w = _ada_wgrad(cat, ada_cols(4, n2)[None], "ada_kv_wgrad")[0]

    def my_cols(v, n):
        return lax.dynamic_index_in_dim(v.reshape(4, n), chip, axis=0, keepdims=False)

    grads = {
        "mix_norm_g": jnp.concatenate([fsum(5), fsum(6)], axis=0),
        "mix_ada_w": g_mix_ada_w,
        "mix_ada_b": jnp.concatenate([fsum(0), fsum(1)], axis=0),
        "ffn_norm_g": jnp.concatenate([fsum(7), fsum(8)], axis=0),
        "ffn_ada_w": g_ffn_ada_w,
        "ffn_ada_b": jnp.concatenate([fsum(2), fsum(3)], axis=0),
        "ffn_w_in": jnp.stack([g_w_in0, g_w_in1]),
        "ffn_w_out": jnp.stack([g_w_out0, g_w_out1]),
        "conv_w_in": g_cw_in[None],
        "conv_b_in": my_cols(jnp.concatenate([fsum(11), fsum(12)], axis=1), 2 * dq_)[None],
        "conv_w_dw": lax.dynamic_index_in_dim(fsum(17).reshape(HALO, 4, dq_), chip, axis=1, keepdims=False)[:CONV_K][None],
        "conv_b_dw": my_cols(fsum(13), dq_)[None],
        "conv_ln_g": my_cols(fsum(14), dq_)[None],
        "conv_ln_b": my_cols(fsum(15), dq_)[None],
        "conv_w_out": g_cw_out[None],
        "conv_b_out": my_cols(fsum(16), dq_)[None],
        "kv_norm_g": fsum(9).reshape(-1),
        "kv_ada_w": g_kv_ada_w,
        "kv_ada_b": fsum(4).reshape(-1),
        "kv_w": g_kvw,
        "forget_b": fsum(18).reshape(-1)[:N_HEADS],
        "attn_w_q": g_wq[None],
        "attn_w_o": g_wo[None],
        "final_norm_g": fsum(10).reshape(-1),
    }
    weights = dict(mix_norm_g=mix_norm_g, mix_ada_w=mix_ada_w, mix_ada_b=mix_ada_b, ffn_norm_g=ffn_norm_g, ffn_ada_w=ffn_ada_w, ffn_ada_b=ffn_ada_b, ffn_w_in=ffn_w_in, ffn_w_out=ffn_w_out, conv_w_in=conv_w_in, conv_b_in=conv_b_in, conv_w_dw=conv_w_dw, conv_b_dw=conv_b_dw, conv_ln_g=conv_ln_g, conv_ln_b=conv_ln_b, conv_w_out=conv_w_out, conv_b_out=conv_b_out, kv_norm_g=kv_norm_g, kv_ada_w=kv_ada_w, kv_ada_b=kv_ada_b, kv_w=kv_w, forget_b=forget_b, attn_w_q=attn_w_q, attn_w_o=attn_w_o, final_norm_g=final_norm_g)
    moms = dict(mix_norm_g=(m_mix_norm_g, v_mix_norm_g), mix_ada_w=(m_mix_ada_w, v_mix_ada_w), mix_ada_b=(m_mix_ada_b, v_mix_ada_b), ffn_norm_g=(m_ffn_norm_g, v_ffn_norm_g), ffn_ada_w=(m_ffn_ada_w, v_ffn_ada_w), ffn_ada_b=(m_ffn_ada_b, v_ffn_ada_b), ffn_w_in=(m_ffn_w_in, v_ffn_w_in), ffn_w_out=(m_ffn_w_out, v_ffn_w_out), conv_w_in=(m_conv_w_in, v_conv_w_in), conv_b_in=(m_conv_b_in, v_conv_b_in), conv_w_dw=(m_conv_w_dw, v_conv_w_dw), conv_b_dw=(m_conv_b_dw, v_conv_b_dw), conv_ln_g=(m_conv_ln_g, v_conv_ln_g), conv_ln_b=(m_conv_ln_b, v_conv_ln_b), conv_w_out=(m_conv_w_out, v_conv_w_out), conv_b_out=(m_conv_b_out, v_conv_b_out), kv_norm_g=(m_kv_norm_g, v_kv_norm_g), kv_ada_w=(m_kv_ada_w, v_kv_ada_w), kv_ada_b=(m_kv_ada_b, v_kv_ada_b), kv_w=(m_kv_w, v_kv_w), forget_b=(m_forget_b, v_forget_b), attn_w_q=(m_attn_w_q, v_attn_w_q), attn_w_o=(m_attn_w_o, v_attn_w_o), final_norm_g=(m_final_norm_g, v_final_norm_g))
    names = list(weights)

    deltas, new_m, new_v = {}, {}, {}
    small_names = [n for n in names if weights[n].size < (1 << 16)]
    for n in names:
        if n in small_names:
            continue
        w = weights[n]
        w2 = w.reshape(-1, w.shape[-1])
        dl, nm, nv = _adamw(w2, grads[n].reshape(w2.shape), moms[n][0].reshape(w2.shape), moms[n][1].reshape(w2.shape), "adamw_" + n)
        deltas[n], new_m[n], new_v[n] = dl.reshape(w.shape), nm.reshape(w.shape), nv.reshape(w.shape)

    def pack_small(get):
        flat = jnp.concatenate([get(n).reshape(-1) for n in small_names])
        rows_ = -(-flat.shape[0] // (8 * LANE)) * 8
        return jnp.pad(flat, (0, rows_ * LANE - flat.shape[0])).reshape(rows_, LANE)

    ws, gs = pack_small(lambda n: weights[n]), pack_small(lambda n: grads[n])
    ms_, vs_ = pack_small(lambda n: moms[n][0]), pack_small(lambda n: moms[n][1])
    vs_ = jnp.where(jnp.arange(vs_.size).reshape(vs_.shape) < sum(weights[n].size for n in small_names), vs_, 1.0)
    dl, nm, nv = _adamw(ws, gs, ms_, vs_, "adamw_small")
    off = 0
    for n in small_names:
        sz = weights[n].size
        shp = weights[n].shape
        deltas[n] = dl.reshape(-1)[off:off + sz].reshape(shp)
        new_m[n] = nm.reshape(-1)[off:off + sz].reshape(shp)
        new_v[n] = nv.reshape(-1)[off:off + sz].reshape(shp)
        off += sz

    grad_out = [grads[n].reshape(weights[n].shape) for n in names]
    return (loss, dx0[None], *grad_out, *[deltas[n] for n in names], *[new_m[n] for n in names], *[new_v[n] for n in names])
```

```python
import functools

import jax
import jax.numpy as jnp
from jax import lax
from jax.experimental import pallas as pl
from jax.experimental.pallas import tpu as pltpu

F32 = jnp.float32
BF16 = jnp.bfloat16
MESH = pl.DeviceIdType.MESH

EPS = 1e-6
N_HEADS = 16
HEAD_DIM = 64
CONV_K = 31
LANE = 128
SUBLANES = 8
HALO = 32
ATT_FWD_TQ = 2048
ATT_TQ = 1024
ATT_TK = 512
NPIECE = 3
SPARE = (HEAD_DIM, 0)
VMEM_MB = 48
ATT_BWD_VMEM_MB = 56

ADAM_LR = 0.001
ADAM_B1 = 0.9
ADAM_B2 = 0.999
ADAM_EPS = 1e-08
ADAM_WD = 0.01
ADAM_STEP = 10


def _sds(shape, dtype):
    return jax.ShapeDtypeStruct(tuple(shape), dtype)


def _cp(sem=None, vmem_mb=VMEM_MB):
    return pltpu.CompilerParams(dimension_semantics=sem, vmem_limit_bytes=vmem_mb << 20)


def _tile(n, pref):
    return pref if n % pref == 0 else n


def _row_tile(r, mult, width=1024):
    cap = max(mult, (512 * 1024 // width) // mult * mult)
    for cand in range(cap, mult - 1, -mult):
        if r % cand == 0:
            return cand
    return r


def _resident(shape):
    nd = len(shape)
    return pl.BlockSpec(tuple(shape), lambda *_: (0,) * nd, pipeline_mode=pl.Buffered(1))


def _dot(a, b):
    return jnp.dot(a, b, preferred_element_type=F32)


def _dot_nt(a, b):
    return lax.dot_general(a, b, (((1,), (1,)), ((), ())), preferred_element_type=F32)


def _dot_tn(a, b):
    return lax.dot_general(a, b, (((0,), (0,)), ((), ())), preferred_element_type=F32)


def _sigmoid(x):
    return 1.0 / (1.0 + jnp.exp(-x))


def _colsum(x):
    return jnp.sum(x, axis=0, keepdims=True)


def _rms_parts(x):
    rstd = lax.rsqrt(jnp.mean(x * x, axis=-1, keepdims=True) + EPS)
    return x * rstd, rstd


class _Gather8:
    def __init__(self, xs):
        self.n = len(xs)
        self.m = [x.shape[0] for x in xs]
        self.land = [_sds((8 * x.shape[0],) + tuple(x.shape[1:]), x.dtype) for x in xs]
        self.sems = [pltpu.SemaphoreType.DMA((7 * self.n,)), pltpu.SemaphoreType.DMA((7 * self.n,)),
                     pltpu.SemaphoreType.DMA((self.n,))]

    def _parts(self, a, x_refs, out_refs, send_sems, recv_sems, local_sems):
        x, y, c = lax.axis_index("x"), lax.axis_index("y"), lax.axis_index("c")
        me, sibling = (x, y, c), (x, y, 1 - c)
        chips = [(1 - x, y), (x, 1 - y), (1 - x, 1 - y)]
        m_per, x_ref, out_ref = self.m[a], x_refs[a], out_refs[a]

        def rows(px, py, pc):
            return out_ref.at[pl.ds((4 * px + 2 * py + pc) * m_per, m_per)]

        def copy(k, block, to, src=None):
            return pltpu.make_async_remote_copy(
                src_ref=rows(*block) if src is None else src, dst_ref=rows(*block),
                send_sem=send_sems.at[7 * a + k], recv_sem=recv_sems.at[7 * a + k], device_id=to, device_id_type=MESH)

        mine = pltpu.make_async_copy(x_ref, rows(*me), local_sems.at[a])
        first = [copy(0, me, sibling, src=x_ref)]
        first += [copy(1 + j, me, (*chip, c), src=x_ref) for j, chip in enumerate(chips)]
        passed = [copy(4 + j, (*chip, c), sibling) for j, chip in enumerate(chips)]
        return c, me, sibling, chips, copy, mine, first, passed

    def start(self, *refs):
        for a in range(self.n):
            _, _, _, _, _, mine, first, _ = self._parts(a, *refs)
            mine.start()
            for cp in first:
                cp.start()

    def finish(self, *refs):
        parts = [self._parts(a, *refs) for a in range(self.n)]
        for j in range(3):
            for c, me, sibling, chips, copy, mine, first, passed in parts:
                copy(1 + j, (*chips[j], c), me).wait_recv()
                passed[j].start()
        for c, me, sibling, chips, copy, mine, first, passed in parts:
            copy(0, sibling, me).wait_recv()
            for j, chip in enumerate(chips):
                copy(4 + j, (*chip, 1 - c), me).wait_recv()
            for cp in first + passed:
                cp.wait_send()
            mine.wait()


class _ScatterChips:
    def __init__(self, qs):
        self.n = len(qs)
        self.land = [_sds((3,) + tuple(q.shape[1:]), q.dtype) for q in qs]
        self.sems = [pltpu.SemaphoreType.DMA((3 * self.n,)), pltpu.SemaphoreType.DMA((3 * self.n,))]

    def _copies(self, q_refs, land_refs, send_sems, recv_sems):
        x, y, c = lax.axis_index("x"), lax.axis_index("y"), lax.axis_index("c")
        chips = [(1 - x, y), (x, 1 - y), (1 - x, 1 - y)]
        return [pltpu.make_async_remote_copy(
            src_ref=q_refs[a].at[2 * cx + cy], dst_ref=land_refs[a].at[k],
            send_sem=send_sems.at[3 * a + k], recv_sem=recv_sems.at[3 * a + k],
            device_id=(cx, cy, c), device_id_type=MESH) for a in range(self.n) for k, (cx, cy) in enumerate(chips)]

    def start(self, *refs):
        for cp in self._copies(*refs):
            cp.start()

    def finish(self, *refs):
        copies = self._copies(*refs)
        for cp in copies:
            cp.wait_recv()
        for cp in copies:
            cp.wait_send()


class _SwapHalves:
    def __init__(self, ps):
        self.n = len(ps)
        self.land = [_sds((p.shape[0],) + tuple(p.shape[2:]), p.dtype) for p in ps]
        self.nb = [p.shape[0] for p in ps]
        tot = sum(self.nb)
        self.sems = [pltpu.SemaphoreType.DMA((tot,)), pltpu.SemaphoreType.DMA((tot,))]

    def _copies(self, p_refs, land_refs, send_sems, recv_sems):
        x, y, c = lax.axis_index("x"), lax.axis_index("y"), lax.axis_index("c")
        out, k = [], 0
        for a in range(self.n):
            for j in range(self.nb[a]):
                out.append(pltpu.make_async_remote_copy(
                    src_ref=p_refs[a].at[j, 1 - c], dst_ref=land_refs[a].at[j], send_sem=send_sems.at[k],
                    recv_sem=recv_sems.at[k], device_id=(x, y, 1 - c), device_id_type=MESH))
                k += 1
        return out

    start = _ScatterChips.start
    finish = _ScatterChips.finish


def _hosted_call(body, comm, *, grid, in_specs, out_specs, out_shape, scratch_shapes, name, sem, args, vmem_mb=VMEM_MB):
    def first():
        return functools.reduce(jnp.logical_and, [pl.program_id(a) == 0 for a in range(len(grid))])

    def last():
        return functools.reduce(jnp.logical_and, [pl.program_id(a) == g - 1 for a, g in enumerate(grid)])

    out_specs = tuple(out_specs) if isinstance(out_specs, (tuple, list)) else (out_specs,)
    out_shape = tuple(out_shape) if isinstance(out_shape, (tuple, list)) else (out_shape,)
    if comm is None:
        return pl.pallas_call(body, grid=grid, in_specs=list(in_specs), out_specs=out_specs, out_shape=out_shape,
                              scratch_shapes=list(scratch_shapes), name=name, compiler_params=_cp(sem, vmem_mb))(*args)
    ex, srcs = comm
    n_in, n_out, n_scr, n_ex = len(in_specs), len(out_shape), len(scratch_shapes), ex.n

    def wrapped(*refs):
        ins, src_refs = refs[:n_in], refs[n_in:n_in + n_ex]
        o0 = n_in + n_ex
        outs, land_refs = refs[o0:o0 + n_out], refs[o0 + n_out:o0 + n_out + n_ex]
        s0 = o0 + n_out + n_ex
        scr, sems = refs[s0:s0 + n_scr], refs[s0 + n_scr:]

        @pl.when(first())
        def _():
            ex.start(src_refs, land_refs, *sems)

        body(*ins, *outs, *scr)

        @pl.when(last())
        def _():
            ex.finish(src_refs, land_refs, *sems)

    hbm = pl.BlockSpec(memory_space=pl.ANY)
    res = pl.pallas_call(
        wrapped, grid=grid, in_specs=[*in_specs, *[hbm] * n_ex], out_specs=(*out_specs, *[hbm] * n_ex),
        out_shape=(*out_shape, *ex.land), scratch_shapes=[*scratch_shapes, *ex.sems], name=name,
        compiler_params=_cp(tuple("arbitrary" for _ in grid), vmem_mb))(*args, *srcs)
    return (*res[:n_out], list(res[n_out:]))


def _exchange(ex, srcs, name, in_vmem=False):
    n = ex.n

    def body(*refs):
        src_refs, land_refs, sems = refs[:n], refs[n:2 * n], refs[2 * n:]
        ex.start(src_refs, land_refs, *sems)
        ex.finish(src_refs, land_refs, *sems)

    spec = pl.BlockSpec(memory_space=pltpu.VMEM if in_vmem else pl.ANY)
    return list(pl.pallas_call(
        body, out_shape=tuple(ex.land), in_specs=[spec] * n, out_specs=tuple([spec] * n),
        scratch_shapes=ex.sems, name=name)(*srcs))


def _allgather8(x_shard, name, in_vmem):
    return _exchange(_Gather8([x_shard]), [x_shard], name, in_vmem)[0]


def _share_halves(bufs):
    n = len(bufs)

    def body(*refs):
        b_refs, out_refs, send_sems, recv_sems = refs[:n], refs[n:2 * n], refs[2 * n], refs[2 * n + 1]
        x, y, c = lax.axis_index("x"), lax.axis_index("y"), lax.axis_index("c")
        copies = [pltpu.make_async_remote_copy(
            src_ref=b_refs[k].at[c], dst_ref=out_refs[k].at[c], send_sem=send_sems.at[k], recv_sem=recv_sems.at[k],
            device_id=(x, y, 1 - c), device_id_type=MESH) for k in range(n)]
        for cp in copies:
            cp.start()
        for cp in copies:
            cp.wait_recv()
        for cp in copies:
            cp.wait_send()

    hbm = pl.BlockSpec(memory_space=pl.ANY)
    return pl.pallas_call(
        body, out_shape=tuple(_sds(b.shape, b.dtype) for b in bufs), in_specs=[hbm] * n, out_specs=tuple([hbm] * n),
        scratch_shapes=[pltpu.SemaphoreType.DMA((n,)), pltpu.SemaphoreType.DMA((n,))],
        input_output_aliases={k: k for k in range(n)}, name="rs_share_halves")(*bufs)


def _add_halves(p, land, sel, name):
    nb, _, r, w = p.shape
    tr = _row_tile(r, 16, w)

    def body(sel_ref, p_ref, l_ref, q16_ref, own_ref):
        q = p_ref[0, 0] + l_ref[0]
        q16_ref[0] = q.astype(BF16)

        @pl.when(pl.program_id(1) == sel_ref[1])
        def _():
            own_ref[...] = q

    gs = pltpu.PrefetchScalarGridSpec(
        num_scalar_prefetch=1, grid=(r // tr, nb),
        in_specs=[pl.BlockSpec((1, 1, tr, w), lambda i, j, sl: (j, sl[0], i, 0)),
                  pl.BlockSpec((1, tr, w), lambda i, j, sl: (j, i, 0))],
        out_specs=(pl.BlockSpec((1, tr, w), lambda i, j, sl: (j, i, 0)), pl.BlockSpec((tr, w), lambda i, j, sl: (i, 0))))
    return pl.pallas_call(body, grid_spec=gs, out_shape=(_sds((nb, r, w), BF16), _sds((r, w), F32)), name=name,
                          compiler_params=_cp(("parallel", "arbitrary")))(sel, p, land)


def _add_chips(own, land, sel, name):
    r, w = own.shape
    tr = _row_tile(r, 16, w)

    def body(sel_ref, q_ref, l_ref, o_ref):
        o_ref[0] = ((q_ref[...] + l_ref[0].astype(F32)) + l_ref[1].astype(F32)) + l_ref[2].astype(F32)

    gs = pltpu.PrefetchScalarGridSpec(
        num_scalar_prefetch=1, grid=(r // tr,),
        in_specs=[pl.BlockSpec((tr, w), lambda i, sl: (i, 0)), pl.BlockSpec((3, tr, w), lambda i, sl: (0, i, 0))],
        out_specs=pl.BlockSpec((1, tr, w), lambda i, sl: (sl[0], i, 0)))
    return pl.pallas_call(body, grid_spec=gs, out_shape=_sds((2, r, w), F32), name=name,
                          compiler_params=_cp(("parallel",)))(sel, own, land)


def _sum8(g):
    _, m, n = g.shape

    def body(g_ref, o_ref):
        acc = g_ref[0]
        for k in range(1, 8):
            acc = acc + g_ref[k]
        o_ref[...] = acc

    return pl.pallas_call(body, out_shape=_sds((m, n), g.dtype), name="sum8")(g)


def _ada_fwd(c_all, w3, name):
    nl, d, n = w3.shape
    tn = 256

    def body(c_ref, w_ref, o_ref):
        cc = c_ref[...]
        ca = (cc * _sigmoid(cc)).astype(BF16)
        o_ref[0] = _dot(ca, w_ref[0].astype(BF16))

    return pl.pallas_call(
        body, grid=(nl, n // tn), out_shape=_sds((nl, 8, n), F32),
        in_specs=[pl.BlockSpec((8, d), lambda l, j: (0, 0)), pl.BlockSpec((1, d, tn), lambda l, j: (l, 0, j))],
        out_specs=pl.BlockSpec((1, 8, tn), lambda l, j: (l, 0, j)),
        name=name, compiler_params=_cp(("parallel", "parallel")))(c_all, w3)


def _in_pair(x, gain, shift, scale, wg, bias, conv, name, comm=None):
    s, d = x.shape
    n = wg.shape[2]
    ts = _tile(s, 512)

    def body(*refs):
        if conv:
            x_ref, g_ref, sh_ref, sc_ref, w_ref, b_ref, h_ref, o_ref, sa_ref, sb_ref = refs
        else:
            x_ref, g_ref, sh_ref, sc_ref, w_ref, h_ref, o_ref, sa_ref, sb_ref = refs
        xhat, _ = _rms_parts(x_ref[...])
        h = ((xhat * g_ref[...]) * (1.0 + sc_ref[...]) + sh_ref[...]).astype(BF16)
        h_ref[...] = h
        for q in range(2):
            a = _dot(h, w_ref[q])
            b = _dot(h, w_ref[q + 2])
            cs = pl.ds(q * n, n)
            if conv:
                a = a + b_ref[q]
                b = b + b_ref[q + 2]
                o_ref[:, cs] = a * _sigmoid(b)
            else:
                o_ref[:, cs] = (a * _sigmoid(a) * b).astype(BF16)
            sa_ref[:, cs] = a.astype(BF16)
            sb_ref[:, cs] = b.astype(BF16)

    vec = pl.BlockSpec((1, d), lambda i: (0, 0))
    in_specs = [pl.BlockSpec((ts, d), lambda i: (i, 0)), vec, vec, vec, _resident(wg.shape)]
    args = [x, gain, shift, scale, wg]
    if conv:
        in_specs.append(_resident(bias.shape))
        args.append(bias)
    tile = pl.BlockSpec((ts, 2 * n), lambda i: (i, 0))
    return _hosted_call(
        body, comm, grid=(s // ts,),
        out_shape=(_sds((s, d), BF16), _sds((s, 2 * n), F32 if conv else BF16), _sds((s, 2 * n), BF16), _sds((s, 2 * n), BF16)),
        in_specs=in_specs, out_specs=(pl.BlockSpec((ts, d), lambda i: (i, 0)), tile, tile, tile),
        scratch_shapes=[], name=name, sem=("parallel",), args=args)


def _shift_copies(buf, shf):
    n = shf.shape[1]
    for r in range(1, SUBLANES):
        shf[r - 1, :, :] = buf[pl.ds(r, n), :]


def _shifted(buf, shf, start, n, cs):
    a, r = divmod(start, SUBLANES)
    if r == 0:
        return buf[pl.ds(start, n), cs]
    return shf[r - 1, pl.ds(a * SUBLANES, n), cs]


def _dwconv_fwd(glu, wdw, bdw, lng, lnb, comm=None):
    s, d = glu.shape
    ts = _tile(s, 256)
    rb, cb = 32, 256

    def body(cur_ref, halo_ref, w_ref, b_ref, g_ref, be_ref, dwo_ref, sw_ref, buf, shf):
        i = pl.program_id(0)

        @pl.when(i == 0)
        def _():
            buf[pl.ds(0, HALO), :] = jnp.zeros((HALO, d), F32)

        @pl.when(i > 0)
        def _():
            buf[pl.ds(0, HALO), :] = halo_ref[...]

        buf[pl.ds(HALO, ts), :] = cur_ref[...]
        _shift_copies(buf, shf)
        for r in range(ts // rb):
            for cc in range(d // cb):
                cs = pl.ds(cc * cb, cb)
                acc = jnp.zeros((rb, cb), F32) + b_ref[:, cs]
                for k in range(CONV_K):
                    acc = acc + w_ref[pl.ds(k, 1), cs] * _shifted(buf, shf, HALO - (CONV_K - 1) + k + r * rb, rb, cs)
                dwo_ref[pl.ds(r * rb, rb), cs] = acc
            rows = pl.ds(r * rb, rb)
            yv = dwo_ref[rows, :]
            mu = jnp.mean(yv, axis=-1, keepdims=True)
            yc = yv - mu
            var = jnp.mean(yc * yc, axis=-1, keepdims=True)
            ln = yc * lax.rsqrt(var + EPS) * g_ref[...] + be_ref[...]
            sw_ref[rows, :] = (ln * _sigmoid(ln)).astype(BF16)

    vec = pl.BlockSpec((1, d), lambda i: (0, 0))
    return _hosted_call(
        body, comm, grid=(s // ts,), out_shape=(_sds((s, d), F32), _sds((s, d), BF16)),
        in_specs=[pl.BlockSpec((ts, d), lambda i: (i, 0)),
                  pl.BlockSpec((HALO, d), lambda i: (jnp.maximum(i * (ts // HALO) - 1, 0), 0)),
                  pl.BlockSpec((HALO, d), lambda i: (0, 0)), vec, vec, vec],
        out_specs=(pl.BlockSpec((ts, d), lambda i: (i, 0)), pl.BlockSpec((ts, d), lambda i: (i, 0))),
        scratch_shapes=[pltpu.VMEM((HALO + ts, d), F32), pltpu.VMEM((SUBLANES - 1, HALO + ts - SUBLANES, d), F32)],
        name="dwconv_fwd", sem=("parallel",),
        args=(glu, glu, wdw, bdw, lng, lnb))


def _mm_res(a, w, b, gate, x, name):
    s, k = a.shape
    d = w.shape[1]
    ts = _tile(s, 512)

    def body(a_ref, w_ref, b_ref, g_ref, x_ref, o_ref):
        yv = _dot(a_ref[...], w_ref[...]) + b_ref[...]
        o_ref[...] = x_ref[...] + g_ref[...] * yv

    vec = pl.BlockSpec((1, d), lambda i: (0, 0))
    return pl.pallas_call(
        body, grid=(s // ts,), out_shape=_sds((s, d), F32),
        in_specs=[pl.BlockSpec((ts, k), lambda i: (i, 0)), _resident((k, d)), vec, vec, pl.BlockSpec((ts, d), lambda i: (i, 0))],
        out_specs=pl.BlockSpec((ts, d), lambda i: (i, 0)),
        name=name, compiler_params=_cp(("parallel",)))(a, w, b, gate, x)


def _qkv(x, kvp, mxp, wk, wv, wf, wq):
    s, d = x.shape
    ts = _tile(s, 512)
    qscale = HEAD_DIM ** -0.5

    def body(x_ref, gk, shk, sck, gm, shm, scm, wk_ref, wv_ref, wf_ref, wq_ref, hk_ref, h1_ref, k_ref, v_ref, q_ref, f_ref):
        xhat, _ = _rms_parts(x_ref[...])
        hk = ((xhat * gk[...]) * (1.0 + sck[...]) + shk[...]).astype(BF16)
        h1 = ((xhat * gm[...]) * (1.0 + scm[...]) + shm[...]).astype(BF16)
        hk_ref[...] = hk
        h1_ref[...] = h1
        k_ref[...] = _dot(hk, wk_ref[...]).astype(BF16)
        v_ref[...] = _dot(hk, wv_ref[...]).astype(BF16)
        f_ref[...] = _dot(hk, wf_ref[...])
        q_ref[...] = (_dot(h1, wq_ref[...]) * qscale).astype(BF16)

    vec = pl.BlockSpec((1, d), lambda i: (0, 0))
    row = pl.BlockSpec((ts, d), lambda i: (i, 0))
    return pl.pallas_call(
        body, grid=(s // ts,),
        out_shape=tuple(_sds((s, d), BF16) for _ in range(5)) + (_sds((s, LANE), F32),),
        in_specs=[row, vec, vec, vec, vec, vec, vec, _resident((d, d)), _resident((d, d)), _resident((d, LANE)), _resident((d, d))],
        out_specs=(row, row, row, row, row, pl.BlockSpec((ts, LANE), lambda i: (i, 0))),
        name="qkv_proj", compiler_params=_cp(("parallel",)))(x, *kvp, *mxp, wk, wv, wf, wq)


def _log_sigmoid(z):
    return jnp.minimum(z, 0.0) - jnp.log(1.0 + jnp.exp(-jnp.abs(z)))


def _cumsum_fwd(flog, fb):
    s = flog.shape[0]
    ts = _tile(s, 256)

    def body(f_ref, b_ref, cum_ref, cumt_ref, carry):
        @pl.when(pl.program_id(0) == 0)
        def _():
            carry[...] = jnp.zeros_like(carry)

        ls = _log_sigmoid(f_ref[...] + b_ref[...])
        r = lax.broadcasted_iota(jnp.int32, (ts, ts), 0)
        cidx = lax.broadcasted_iota(jnp.int32, (ts, ts), 1)
        tri = (cidx <= r).astype(F32)
        cs = jnp.dot(tri, ls, preferred_element_type=F32, precision=lax.Precision.HIGHEST) + carry[...]
        cum_ref[...] = cs
        cumt_ref[...] = cs.T
        carry[...] = cs[ts - 1:ts, :]

    return pl.pallas_call(
        body, grid=(s // ts,), out_shape=(_sds((s, LANE), F32), _sds((LANE, s), F32)),
        in_specs=[pl.BlockSpec((ts, LANE), lambda i: (i, 0)), pl.BlockSpec((1, LANE), lambda i: (0, 0))],
        out_specs=(pl.BlockSpec((ts, LANE), lambda i: (i, 0)), pl.BlockSpec((LANE, ts), lambda i: (0, i))),
        scratch_shapes=[pltpu.VMEM((1, LANE), F32)],
        name="forget_cumsum", compiler_params=_cp(("arbitrary",)))(flog, fb)


def _pick_row(m, idx):
    r = lax.broadcasted_iota(jnp.int32, (m.shape[0], 1), 0)
    return jnp.sum(jnp.where(r == idx, m, 0.0), axis=0, keepdims=True)


def _pick_col(m, idx):
    cidx = lax.broadcasted_iota(jnp.int32, (1, m.shape[1]), 1)
    return jnp.sum(jnp.where(cidx == idx, m, 0.0), axis=1, keepdims=True)


def _split3(x):
    hi = x.astype(BF16)
    r1 = x - hi.astype(F32)
    mid = r1.astype(BF16)
    lo = (r1 - mid.astype(F32)).astype(BF16)
    return hi, mid, lo


def _head_mask(lane, hh):
    lo = lane < HEAD_DIM
    return lo if hh == 0 else jnp.logical_not(lo)


def _attn_prep(k, v, cum):
    s, d = k.shape
    npair = d // LANE
    tc = _tile(s, 1024)

    def body(k_ref, v_ref, c_ref, ka_ref, kt_ref, vt_ref):
        p = pl.program_id(0)
        lane = lax.broadcasted_iota(jnp.int32, (1, LANE), 1)
        kk = k_ref[...]
        vv = v_ref[...].astype(F32)
        ckt = c_ref[...]
        for hh in range(2):
            head = _head_mask(lane, hh)
            b = SPARE[hh]
            ck = _pick_col(ckt, 2 * p + hh)
            extra = jnp.where(lane == b + NPIECE, 1.0, 0.0).astype(BF16) + jnp.zeros((tc, LANE), BF16)
            for n_, pc in enumerate(_split3(ck)):
                extra = jnp.where(lane == b + n_, pc, extra)
            ka = jnp.where(head, kk, extra)
            ka_ref[0, hh] = ka
            kt_ref[0, hh] = ka.astype(F32).T.astype(BF16)
            vx = jnp.where(head, vv, jnp.where(lane == b, 1.0, 0.0))
            vt_ref[0, hh] = vx.T.astype(BF16)

    blk = pl.BlockSpec((tc, LANE), lambda p, c: (c, p))
    return pl.pallas_call(
        body, grid=(npair, s // tc),
        out_shape=(_sds((npair, 2, s, LANE), BF16), _sds((npair, 2, LANE, s), BF16), _sds((npair, 2, LANE, s), BF16)),
        in_specs=[blk, blk, pl.BlockSpec((tc, LANE), lambda p, c: (c, 0))],
        out_specs=(pl.BlockSpec((1, 2, tc, LANE), lambda p, c: (p, 0, c, 0)),
                   pl.BlockSpec((1, 2, LANE, tc), lambda p, c: (p, 0, 0, c)),
                   pl.BlockSpec((1, 2, LANE, tc), lambda p, c: (p, 0, 0, c))),
        name="fox_attn_prep", compiler_params=_cp(("parallel", "parallel")))(k, v, cum)


def _q_aug(qq, lane, hh):
    b = SPARE[hh]
    sel = jnp.logical_and(lane >= b, lane < b + NPIECE)
    neg = jnp.full((1, LANE), -1.0, BF16)
    zl = jnp.zeros((1, LANE), BF16)
    return jnp.where(_head_mask(lane, hh), qq, jnp.where(sel, neg, zl))


def _attn_fwd(q, kaug, vtr, cumt):
    s, d = q.shape
    tq = _tile(s, ATT_FWD_TQ)
    tk = _tile(s, ATT_TK)
    npair = d // LANE
    npart = max(1, tq // tk)

    def body(q_ref, ka_ref, vt_ref, cumt_ref, o_ref, lse_ref):
        p = pl.program_id(0)
        i = pl.program_id(1)
        lane = lax.broadcasted_iota(jnp.int32, (1, LANE), 1)
        qq = q_ref[...]
        qx = (_q_aug(qq, lane, 0), _q_aug(qq, lane, 1))
        cqt = cumt_ref[:, pl.ds(pl.multiple_of(i * tq, tq), tq)]
        cq = (_pick_row(cqt, 2 * p), _pick_row(cqt, 2 * p + 1))
        jd = (i * tq) // tk

        def kv_step(j, carry, diag, q_lo=0):
            ks = pl.multiple_of(j * tk, tk)
            nq_ = tq - q_lo
            if diag:
                krow = lax.broadcasted_iota(jnp.int32, (tk, nq_), 0) + j * tk
                qcol = lax.broadcasted_iota(jnp.int32, (tk, nq_), 1) + (i * tq + q_lo)
                causal = krow <= qcol
            out = []
            for hh in range(2):
                m_all, acc_all = carry[2 * hh], carry[2 * hh + 1]
                m, acc, cqh = m_all[:, q_lo:], acc_all[:, q_lo:], cq[hh][:, q_lo:]
                sc = _dot_nt(ka_ref[0, hh, pl.ds(ks, tk), :], qx[hh][q_lo:, :])
                if diag:
                    sc = jnp.where(causal, sc, -jnp.inf)
                mx = jnp.max(sc, axis=0, keepdims=True) + cqh
                mn = jnp.maximum(m, mx)
                alpha = jnp.exp(m - mn)
                pt = jnp.exp(sc + (cqh - mn)).astype(BF16)
                acc = alpha * acc + _dot(vt_ref[0, hh, :, pl.ds(ks, tk)], pt)
                if q_lo:
                    mn = jnp.concatenate([m_all[:, :q_lo], mn], axis=1)
                    acc = jnp.concatenate([acc_all[:, :q_lo], acc], axis=1)
                out += [mn, acc]
            return tuple(out)

        minit = jnp.full((1, tq), -jnp.inf, F32)
        ainit = jnp.zeros((LANE, tq), F32)
        carry = (minit, ainit, minit, ainit)
        for pj in range(npart):
            carry = kv_step(jd + pj, carry, True, q_lo=pj * tk)
        carry = lax.fori_loop(0, jd, lambda j, cr: kv_step(j, cr, False), carry)
        m0, a0, m1, a1 = carry
        l0 = a0[SPARE[0]:SPARE[0] + 1, :]
        l1 = a1[SPARE[1]:SPARE[1] + 1, :]
        row = lax.broadcasted_iota(jnp.int32, (LANE, 1), 0)
        ot = jnp.where(row < HEAD_DIM, a0 / l0, a1 / l1)
        o_ref[...] = ot.T.astype(BF16)
        r8 = lax.broadcasted_iota(jnp.int32, (8, 1), 0)
        lse_ref[0] = jnp.where(r8 == 0, m0 + jnp.log(l0), jnp.where(r8 == 1, m1 + jnp.log(l1), 0.0))

    return pl.pallas_call(
        body, grid=(npair, s // tq), out_shape=(_sds((s, d), BF16), _sds((npair, 8, s), F32)),
        in_specs=[pl.BlockSpec((tq, LANE), lambda p, i: (i, p)),
                  pl.BlockSpec((1, 2, s, LANE), lambda p, i: (p, 0, 0, 0)),
                  pl.BlockSpec((1, 2, LANE, s), lambda p, i: (p, 0, 0, 0)),
                  pl.BlockSpec((N_HEADS, s), lambda p, i: (0, 0))],
        out_specs=(pl.BlockSpec((tq, LANE), lambda p, i: (i, p)), pl.BlockSpec((1, 8, tq), lambda p, i: (p, 0, i))),
        name="fox_attn_fwd", compiler_params=_cp(("parallel", "parallel")))(q, kaug, vtr, cumt)


def _mm_res_final(a, w, gate, x, gain, target):
    s, k = a.shape
    d = w.shape[1]
    ts = _tile(s, 512)

    def body(a_ref, w_ref, gt_ref, x_ref, g_ref, t_ref, lsum_ref, dx_ref, dg_ref):
        @pl.when(pl.program_id(0) == 0)
        def _():
            lsum_ref[...] = jnp.zeros_like(lsum_ref)
            dg_ref[...] = jnp.zeros_like(dg_ref)

        xv = x_ref[...] + gt_ref[...] * _dot(a_ref[...], w_ref[...])
        xhat, rstd = _rms_parts(xv)
        e = xhat * g_ref[...] - t_ref[...]
        lsum_ref[...] += _colsum(e * e)
        dout = e * (1.0 / d)
        dg_ref[...] += _colsum(dout * xhat)
        dxhat = dout * g_ref[...]
        dx_ref[...] = rstd * (dxhat - xhat * jnp.mean(dxhat * xhat, axis=-1, keepdims=True))

    vec = pl.BlockSpec((1, d), lambda i: (0, 0))
    row = pl.BlockSpec((ts, d), lambda i: (i, 0))
    return pl.pallas_call(
        body, grid=(s // ts,), out_shape=(_sds((1, d), F32), _sds((s, d), F32), _sds((1, d), F32)),
        in_specs=[pl.BlockSpec((ts, k), lambda i: (i, 0)), _resident((k, d)), vec, row, vec, row], out_specs=(vec, row, vec),
        name="ffn1_out_final_loss", compiler_params=_cp(("arbitrary",)))(a, w, gate, x, gain, target)


def _ffn_bwd_act(dx, gate, w_out, ug, uu, name, comm=None):
    s, d = dx.shape
    f = w_out.shape[0]
    n = f // 2
    ts = _tile(s, 512)

    def body(dx_ref, g_ref, w_ref, ug_ref, uu_ref, dug_ref, duu_ref):
        dy = (dx_ref[...] * g_ref[...]).astype(BF16)
        for q in range(2):
            cs = pl.ds(q * n, n)
            dact = _dot_nt(dy, w_ref[cs, :])
            g = ug_ref[:, cs].astype(F32)
            u = uu_ref[:, cs].astype(F32)
            sg = _sigmoid(g)
            dug_ref[:, cs] = (dact * u * sg * (1.0 + g * (1.0 - sg))).astype(BF16)
            duu_ref[:, cs] = (dact * g * sg).astype(BF16)

    tile = pl.BlockSpec((ts, f), lambda i: (i, 0))
    return _hosted_call(
        body, comm, grid=(s // ts,), out_shape=(_sds((s, f), BF16), _sds((s, f), BF16)),
        in_specs=[pl.BlockSpec((ts, d), lambda i: (i, 0)), pl.BlockSpec((1, d), lambda i: (0, 0)),
                  _resident(w_out.shape), tile, tile],
        out_specs=(tile, tile), scratch_shapes=[], name=name, sem=("parallel",), args=(dx, gate, w_out, ug, uu))


def _dw_mm(a, b_list, tk, tn, name, gate=None, wfull=None, dgate_init=None):
    s, kdim = a.shape
    nb1 = b_list[0].shape[1] // tn
    nb = nb1 * len(b_list)
    ts = _tile(s, 1024)
    nk = kdim // tk
    ns = s // ts
    gated = gate is not None

    def body(*refs):
        a_ref = refs[0]
        b_refs = refs[1:1 + len(b_list)]
        rest = refs[1 + len(b_list):]
        if gated:
            g_ref, w_ref, di_ref, o_ref, dg_ref, acc = rest
        else:
            o_ref, acc = rest
        jn, ik, st = pl.program_id(0), pl.program_id(1), pl.program_id(2)

        @pl.when(st == 0)
        def _():
            acc[...] = jnp.zeros_like(acc)

        for mi, b_ref in enumerate(b_refs):
            @pl.when(jn // nb1 == mi)
            def _(b_ref=b_ref):
                acc[...] += _dot_tn(a_ref[...], b_ref[...].astype(BF16))

        if gated:
            @pl.when(jnp.logical_and(ik == 0, st == 0))
            def _():
                dg_ref[...] = di_ref[...]

        @pl.when(st == ns - 1)
        def _():
            if gated:
                o_ref[0] = acc[...] * g_ref[...]
                dg_ref[...] += _colsum(acc[...] * w_ref[...].astype(F32))
            else:
                o_ref[0] = acc[...]

    in_specs = [pl.BlockSpec((ts, tk), lambda jn, ik, st: (st, ik))]
    for mi in range(len(b_list)):
        in_specs.append(pl.BlockSpec(
            (ts, tn), lambda jn, ik, st, mi=mi: (st, jnp.clip(jn - mi * nb1, 0, nb1 - 1))))
    args = [a] + list(b_list)
    out_shape = [_sds((nb, kdim, tn), F32)]
    out_specs = [pl.BlockSpec((1, tk, tn), lambda jn, ik, st: (jn, ik, 0))]
    if gated:
        vec = pl.BlockSpec((1, tn), lambda jn, ik, st: (0, jn))
        in_specs += [vec, pl.BlockSpec((tk, tn), lambda jn, ik, st: (ik, jn)), vec]
        args += [gate, wfull, dgate_init]
        out_shape.append(_sds((1, nb * tn), F32))
        out_specs.append(vec)
    res = pl.pallas_call(
        body, grid=(nb, nk, ns), out_shape=tuple(out_shape), in_specs=in_specs, out_specs=tuple(out_specs),
        scratch_shapes=[pltpu.VMEM((tk, tn), F32)],
        name=name, compiler_params=_cp(("parallel", "arbitrary", "arbitrary")))(*args)
    return res if gated else res[0]


def _mm_normbwd(terms, x, dxres, gain, scale, name, ts_pref=256, comm=None, second=None):
    s, d = x.shape
    ts = _tile(s, ts_pref)
    sets = [(terms, gain, scale)] + ([second] if second is not None else [])
    arrs, warrs = [], []
    for tms, _, _ in sets:
        for a, _, w, _ in tms:
            if not any(a is z for z in arrs):
                arrs.append(a)
            if not any(w is z for z in warrs):
                warrs.append(w)
    na, nw, ns_ = len(arrs), len(warrs), len(sets)

    def body(*refs):
        a_refs, w_refs = refs[:na], refs[na:na + nw]
        x_ref, dr_ref = refs[na + nw], refs[na + nw + 1]
        par = refs[na + nw + 2:na + nw + 2 + 2 * ns_]
        dx_ref = refs[na + nw + 2 + 2 * ns_]
        sums = refs[na + nw + 3 + 2 * ns_:]

        @pl.when(pl.program_id(0) == 0)
        def _():
            for r in sums:
                r[...] = jnp.zeros_like(r)

        xhat, rstd = _rms_parts(x_ref[...])
        dxhat = None
        for k, (tms, _, _) in enumerate(sets):
            g_ref, sc_ref = par[2 * k], par[2 * k + 1]
            dsh_ref, dsc_ref, dg_ref = sums[3 * k:3 * k + 3]
            dh = None
            for a, c0, w, q in tms:
                ai = next(i for i, z in enumerate(arrs) if z is a)
                wi = next(i for i, z in enumerate(warrs) if z is w)
                part = _dot_nt(a_refs[ai][:, pl.ds(c0, w.shape[2])], w_refs[wi][q])
                dh = part if dh is None else dh + part
            dsh_ref[...] += _colsum(dh)
            dsc_ref[...] += _colsum(dh * (xhat * g_ref[...]))
            dn = dh * (1.0 + sc_ref[...])
            dg_ref[...] += _colsum(dn * xhat)
            dxh = dn * g_ref[...]
            dxhat = dxh if dxhat is None else dxhat + dxh
        dx_ref[...] = dr_ref[...] + rstd * (dxhat - xhat * jnp.mean(dxhat * xhat, axis=-1, keepdims=True))

    vec = pl.BlockSpec((1, d), lambda i: (0, 0))
    row = pl.BlockSpec((ts, d), lambda i: (i, 0))
    in_specs = [pl.BlockSpec((ts, a.shape[1]), lambda i: (i, 0)) for a in arrs]
    in_specs += [_resident(w.shape) for w in warrs]
    in_specs += [row, row] + [vec] * (2 * ns_)
    par_args = [p_ for _, g_, s_ in sets for p_ in (g_, s_)]
    return _hosted_call(
        body, comm, grid=(s // ts,), out_shape=(_sds((s, d), F32),) + tuple(_sds((1, d), F32) for _ in range(3 * ns_)),
        in_specs=in_specs, out_specs=(row,) + tuple(vec for _ in range(3 * ns_)), scratch_shapes=[],
        name=name, sem=("arbitrary",), args=(*arrs, *warrs, x, dxres, *par_args))


def _do_kernel(dx, gate, wo, o, comm=None):
    s, d = dx.shape
    ts = _tile(s, 512)

    def body(dx_ref, g_ref, w_ref, o_ref, do_ref, dl_ref):
        dy = (dx_ref[...] * g_ref[...]).astype(BF16)
        do = _dot_nt(dy, w_ref[...])
        do_ref[...] = do.astype(BF16)
        prod = do * o_ref[...].astype(F32)
        hrow = lax.broadcasted_iota(jnp.int32, (N_HEADS, d), 0)
        hcol = lax.broadcasted_iota(jnp.int32, (N_HEADS, d), 1) // HEAD_DIM
        sel = (hrow == hcol).astype(F32)
        dl_ref[...] = lax.dot_general(sel, prod, (((1,), (1,)), ((), ())), preferred_element_type=F32,
                                      precision=lax.Precision.HIGHEST)

    row = pl.BlockSpec((ts, d), lambda i: (i, 0))
    return _hosted_call(
        body, comm, grid=(s // ts,), out_shape=(_sds((s, d), BF16), _sds((N_HEADS, s), F32)),
        in_specs=[row, pl.BlockSpec((1, d), lambda i: (0, 0)), _resident(wo.shape), row],
        out_specs=(row, pl.BlockSpec((N_HEADS, ts), lambda i: (0, i))), scratch_shapes=[],
        name="attn_do", sem=("parallel",), args=(dx, gate, wo, o))


def _attn_bwd(q, do, kaug, kaugt, v, cumt, lse, deltat, comm=None):
    s, d = q.shape
    tq = _tile(s, ATT_TQ)
    tk = _tile(s, ATT_TK)
    assert tq in (tk, 2 * tk)
    npair = d // LANE
    nq = s // tq
    nkb = s // tk
    qscale = HEAD_DIM ** -0.5

    def body(q_ref, do_ref, ka_ref, kt_ref, v_ref, cumt_ref, lse_ref, dl_ref,
             dq_ref, dk_ref, dv_ref, dcq_ref, dck_ref, qaug, dom, rowv, dqt):
        p = pl.program_id(0)
        j = pl.program_id(1)
        lane = lax.broadcasted_iota(jnp.int32, (1, LANE), 1)
        lo = lane < HEAD_DIM
        r8 = lax.broadcasted_iota(jnp.int32, (8, 1), 0)

        @pl.when(j == 0)
        def _():
            dqt[...] = jnp.zeros_like(dqt)
            for c in range(nq):
                rows = pl.ds(c * tq, tq)
                qq = q_ref[rows, :]
                dd = do_ref[rows, :]
                cqt = cumt_ref[:, rows]
                dlt = dl_ref[:, rows]
                lst = lse_ref[0, :, rows]
                for hh in range(2):
                    qaug[hh, rows, :] = _q_aug(qq, lane, hh)
                    dom[hh, rows, :] = jnp.where(_head_mask(lane, hh), dd, jnp.zeros_like(dd))
                    rowv[hh, :, rows] = jnp.where(
                        r8 == 0, _pick_row(cqt, 2 * p + hh) - lst[hh:hh + 1, :],
                        jnp.where(r8 == 1, _pick_row(dlt, 2 * p + hh), 0.0))

        vv = v_ref[...]
        i0 = (j * tk) // tq

        def q_step(qs, nq_, carry, diag):
            dv_acc, dk0, dk1 = carry
            qs = pl.multiple_of(qs, tk)
            if diag:
                krow = lax.broadcasted_iota(jnp.int32, (tk, nq_), 0) + j * tk
                qcol = lax.broadcasted_iota(jnp.int32, (tk, nq_), 1) + qs
                causal = krow <= qcol
            dks = [dk0, dk1]
            for hh in range(2):
                rv = rowv[hh, :, pl.ds(qs, nq_)]
                qa = qaug[hh, pl.ds(qs, nq_), :]
                dh = dom[hh, pl.ds(qs, nq_), :]
                sc = _dot_nt(ka_ref[0, hh], qa)
                if diag:
                    sc = jnp.where(causal, sc, -jnp.inf)
                pt = jnp.exp(sc + rv[0:1, :])
                dpt = _dot_nt(vv, dh)
                dst = (pt * (dpt - rv[1:2, :])).astype(BF16)
                dv_acc = dv_acc + _dot(pt.astype(BF16), dh)
                dks[hh] = dks[hh] + _dot(dst, qa)
                dqt[hh, :, pl.ds(qs, nq_)] += _dot(kt_ref[0, hh], dst)
            return dv_acc, dks[0], dks[1]

        z = jnp.zeros((tk, LANE), F32)
        first = ((j * tk) % tq == 0).astype(jnp.int32)
        carry = lax.fori_loop(0, first, lambda _, cr: q_step(i0 * tq, tq, cr, True), (z, z, z))
        if tq > tk:
            carry = lax.fori_loop(0, 1 - first, lambda _, cr: q_step(j * tk, tq - tk, cr, True), carry)
        dv_acc, dk0, dk1 = lax.fori_loop(i0 + 1, nq, lambda i, cr: q_step(i * tq, tq, cr, False), carry)
        dv_ref[...] = dv_acc.astype(BF16)
        dk_ref[...] = jnp.where(lo, dk0, dk1).astype(BF16)
        dck_ref[0] = jnp.where(r8 == 0, dk0.T[SPARE[0]:SPARE[0] + 1, :],
                               jnp.where(r8 == 1, dk1.T[SPARE[1]:SPARE[1] + 1, :], 0.0))

        @pl.when(j == nkb - 1)
        def _():
            for c in range(nq):
                rows = pl.ds(c * tq, tq)
                a0 = dqt[0, :, rows].T
                a1 = dqt[1, :, rows].T
                dq_ref[rows, :] = (jnp.where(lo, a0, a1) * qscale).astype(BF16)
            r0, r1 = SPARE[0] + NPIECE, SPARE[1] + NPIECE
            dcq_ref[0] = jnp.where(r8 == 0, dqt[0, r0:r0 + 1, :], jnp.where(r8 == 1, dqt[1, r1:r1 + 1, :], 0.0))

    col = pl.BlockSpec((s, LANE), lambda p, j: (0, p), pipeline_mode=pl.Buffered(1))
    rows16 = pl.BlockSpec((N_HEADS, s), lambda p, j: (0, 0), pipeline_mode=pl.Buffered(1))
    blk = pl.BlockSpec((tk, LANE), lambda p, j: (j, p))
    return _hosted_call(
        body, comm, grid=(npair, nkb),
        out_shape=(_sds((s, d), BF16), _sds((s, d), BF16), _sds((s, d), BF16), _sds((npair, 8, s), F32), _sds((npair, 8, s), F32)),
        in_specs=[col, col, pl.BlockSpec((1, 2, tk, LANE), lambda p, j: (p, 0, j, 0)),
                  pl.BlockSpec((1, 2, LANE, tk), lambda p, j: (p, 0, 0, j)), blk, rows16,
                  pl.BlockSpec((1, 8, s), lambda p, j: (p, 0, 0), pipeline_mode=pl.Buffered(1)), rows16],
        out_specs=(pl.BlockSpec((s, LANE), lambda p, j: (0, p)), blk, blk,
                   pl.BlockSpec((1, 8, s), lambda p, j: (p, 0, 0)), pl.BlockSpec((1, 8, tk), lambda p, j: (p, 0, j))),
        scratch_shapes=[pltpu.VMEM((2, s, LANE), BF16), pltpu.VMEM((2, s, LANE), BF16), pltpu.VMEM((2, 8, s), F32),
                        pltpu.VMEM((2, LANE, s), F32)],
        name="fox_attn_bwd", sem=("arbitrary", "arbitrary"), vmem_mb=ATT_BWD_VMEM_MB,
        args=(q, do, kaug, kaugt, v, cumt, lse, deltat))


def _cumsum_bwd(dcq, dck, flog, fb):
    s = flog.shape[0]
    ts = _tile(s, 256)
    nt = s // ts

    def body(dq_ref, dk_ref, f_ref, b_ref, df_ref, db_ref, carry):
        @pl.when(pl.program_id(0) == 0)
        def _():
            carry[...] = jnp.zeros_like(carry)
            db_ref[...] = jnp.zeros_like(db_ref)

        r = lax.broadcasted_iota(jnp.int32, (ts, ts), 0)
        cidx = lax.broadcasted_iota(jnp.int32, (ts, ts), 1)
        tri = (r >= cidx).astype(F32)
        dct = dq_ref[...] + dk_ref[...]
        dlst = jnp.dot(dct, tri, preferred_element_type=F32, precision=lax.Precision.HIGHEST) + carry[...]
        carry[...] = dlst[:, 0:1]
        dls = jnp.concatenate([dlst, jnp.zeros((LANE - N_HEADS, ts), F32)], axis=0).T
        z = f_ref[...] + b_ref[...]
        df = dls * (1.0 / (1.0 + jnp.exp(z)))
        db_ref[...] += _colsum(df)
        df_ref[...] = df.astype(BF16)

    rev = pl.BlockSpec((ts, LANE), lambda i: (nt - 1 - i, 0))
    revt = pl.BlockSpec((N_HEADS, ts), lambda i: (0, nt - 1 - i))
    vec = pl.BlockSpec((1, LANE), lambda i: (0, 0))
    return pl.pallas_call(
        body, grid=(nt,), out_shape=(_sds((s, LANE), BF16), _sds((1, LANE), F32)),
        in_specs=[revt, revt, rev, vec], out_specs=(rev, vec), scratch_shapes=[pltpu.VMEM((N_HEADS, 1), F32)],
        name="forget_cumsum_bwd", compiler_params=_cp(("arbitrary",)))(dcq, dck, flog, fb)


def _conv_bwd1(dx, gate, w_out, b_out, dwo, lng, lnb, dgate_mm, comm=None):
    s, d = dx.shape
    ts = _tile(s, 512)
    ns = s // ts

    def body(dx_ref, g_ref, w_ref, bo_ref, y_ref, lg_ref, lb_ref, dgm_ref, dd_ref, dlg_ref, dlb_ref, dbd_ref, dbo_ref, dge_ref,
             cs):
        i = pl.program_id(0)

        @pl.when(i == 0)
        def _():
            for r in (dlg_ref, dlb_ref, dbd_ref, cs):
                r[...] = jnp.zeros_like(r)

        dxv = dx_ref[...]
        cs[...] += _colsum(dxv)
        dsw = _dot_nt((dxv * g_ref[...]).astype(BF16), w_ref[...])
        yv = y_ref[...]
        mu = jnp.mean(yv, axis=-1, keepdims=True)
        yc = yv - mu
        rstd = lax.rsqrt(jnp.mean(yc * yc, axis=-1, keepdims=True) + EPS)
        xhat = yc * rstd
        ln = xhat * lg_ref[...] + lb_ref[...]
        sg = _sigmoid(ln)
        dln = dsw * (sg * (1.0 + ln * (1.0 - sg)))
        dlg_ref[...] += _colsum(dln * xhat)
        dlb_ref[...] += _colsum(dln)
        dxh = dln * lg_ref[...]
        dd = rstd * (dxh - jnp.mean(dxh, axis=-1, keepdims=True) - xhat * jnp.mean(dxh * xhat, axis=-1, keepdims=True))
        dbd_ref[...] += _colsum(dd)
        dd_ref[...] = dd

        @pl.when(i == ns - 1)
        def _():
            dbo_ref[...] = g_ref[...] * cs[...]
            dge_ref[...] = dgm_ref[...] + bo_ref[...] * cs[...]

    vec = pl.BlockSpec((1, d), lambda i: (0, 0))
    row = pl.BlockSpec((ts, d), lambda i: (i, 0))
    return _hosted_call(
        body, comm, grid=(ns,), out_shape=(_sds((s, d), F32),) + tuple(_sds((1, d), F32) for _ in range(5)),
        in_specs=[row, vec, _resident(w_out.shape), vec, row, vec, vec, vec], out_specs=(row, vec, vec, vec, vec, vec),
        scratch_shapes=[pltpu.VMEM((1, d), F32)],
        name="conv_bwd_ln", sem=("arbitrary",), args=(dx, gate, w_out, b_out, dwo, lng, lnb, dgate_mm))


def _dwconv_bwd(ddwo, glu, a_s, g_s, wdw, comm=None):
    s, d = ddwo.shape
    ts = _tile(s, 256)
    ns = s // ts
    rb, cb = 32, 256
    nrb = ts // rb

    def body(dd_ref, ddn_ref, gl_ref, glh_ref, a_ref, g_ref, w_ref, da_ref, dg_ref, dw_ref, sa_ref, sg_ref, bufd, bufg, dws,
             shd, shg):
        i = pl.program_id(0)

        @pl.when(i == 0)
        def _():
            dws[...] = jnp.zeros_like(dws)
            sa_ref[...] = jnp.zeros_like(sa_ref)
            sg_ref[...] = jnp.zeros_like(sg_ref)
            bufg[pl.ds(0, HALO), :] = jnp.zeros((HALO, d), F32)

        @pl.when(i > 0)
        def _():
            bufg[pl.ds(0, HALO), :] = glh_ref[...]

        bufg[pl.ds(HALO, ts), :] = gl_ref[...]
        bufd[pl.ds(0, ts), :] = dd_ref[...]

        @pl.when(i == ns - 1)
        def _():
            bufd[pl.ds(ts, HALO), :] = jnp.zeros((HALO, d), F32)

        @pl.when(i < ns - 1)
        def _():
            bufd[pl.ds(ts, HALO), :] = ddn_ref[...]

        _shift_copies(bufd, shd)
        _shift_copies(bufg, shg)
        for cc in range(d // cb):
            cs = pl.ds(cc * cb, cb)
            for r in range(nrb):
                acc = jnp.zeros((rb, cb), F32)
                for k in range(CONV_K):
                    acc = acc + w_ref[pl.ds(k, 1), cs] * _shifted(bufd, shd, r * rb + (CONV_K - 1) - k, rb, cs)
                rows = pl.ds(r * rb, rb)
                av = a_ref[rows, cs].astype(F32)
                sg = _sigmoid(g_ref[rows, cs].astype(F32))
                dav = acc * sg
                dgv = acc * av * sg * (1.0 - sg)
                da_ref[rows, cs] = dav.astype(BF16)
                dg_ref[rows, cs] = dgv.astype(BF16)
                sa_ref[:, cs] += _colsum(dav)
                sg_ref[:, cs] += _colsum(dgv)
            for k in range(CONV_K):
                acc8 = jnp.zeros((8, cb), F32)
                for r in range(nrb):
                    prod = bufd[pl.ds(r * rb, rb), cs] * _shifted(bufg, shg, HALO - (CONV_K - 1) + k + r * rb, rb, cs)
                    acc8 = acc8 + (prod[0:8] + prod[8:16]) + (prod[16:24] + prod[24:32])
                dws[pl.ds(8 * k, 8), cs] += acc8

        @pl.when(i == ns - 1)
        def _():
            dw_ref[...] = jnp.zeros_like(dw_ref)
            for k in range(CONV_K):
                dw_ref[pl.ds(k, 1), :] = _colsum(dws[pl.ds(8 * k, 8), :])

    row = pl.BlockSpec((ts, d), lambda i: (i, 0))
    vec = pl.BlockSpec((1, d), lambda i: (0, 0))
    hb = ts // HALO
    return _hosted_call(
        body, comm, grid=(ns,),
        out_shape=(_sds((s, d), BF16), _sds((s, d), BF16), _sds((HALO, d), F32), _sds((1, d), F32), _sds((1, d), F32)),
        in_specs=[row, pl.BlockSpec((HALO, d), lambda i: (jnp.minimum((i + 1) * hb, ns * hb - 1), 0)),
                  row, pl.BlockSpec((HALO, d), lambda i: (jnp.maximum(i * hb - 1, 0), 0)),
                  row, row, pl.BlockSpec((HALO, d), lambda i: (0, 0))],
        out_specs=(row, row, pl.BlockSpec((HALO, d), lambda i: (0, 0)), vec, vec),
        scratch_shapes=[pltpu.VMEM((ts + HALO, d), F32), pltpu.VMEM((HALO + ts, d), F32), pltpu.VMEM((8 * HALO, d), F32),
                        pltpu.VMEM((SUBLANES - 1, HALO + ts - SUBLANES, d), F32),
                        pltpu.VMEM((SUBLANES - 1, HALO + ts - SUBLANES, d), F32)],
        name="dwconv_bwd", sem=("arbitrary",), args=(ddwo, ddwo, glu, glu, a_s, g_s, wdw))


def _ada_wgrad(cat, da, name):
    nl, _, n = da.shape
    d = cat.shape[0]
    tn = 256

    def body(c_ref, d_ref, o_ref):
        acc = c_ref[:, 0:1] * d_ref[0, 0:1, :]
        for r in range(1, 8):
            acc = acc + c_ref[:, r:r + 1] * d_ref[0, r:r + 1, :]
        o_ref[0] = acc

    return pl.pallas_call(
        body, grid=(nl, n // tn), out_shape=_sds((nl, d, n), F32),
        in_specs=[pl.BlockSpec((d, 8), lambda l, j: (0, 0)), pl.BlockSpec((1, 8, tn), lambda l, j: (l, 0, j))],
        out_specs=pl.BlockSpec((1, d, tn), lambda l, j: (l, 0, j)),
        name=name, compiler_params=_cp(("parallel", "parallel")))(cat, da)


def _silu_rows(c_all):
    def body(c_ref, o_ref):
        cc = c_ref[...]
        o_ref[...] = cc * _sigmoid(cc)

    return pl.pallas_call(body, out_shape=_sds(c_all.shape, F32), name="silu_c")(c_all)


def _adamw(w, g, m, v, name):
    r, c = w.shape
    tr = r
    for cand in (512, 256, 128, 64, 32, 16, 8):
        if r % cand == 0 and cand * c * 4 <= (1 << 20):
            tr = cand
            break
    bc1 = 1.0 - ADAM_B1 ** ADAM_STEP
    bc2 = 1.0 - ADAM_B2 ** ADAM_STEP

    def body(w_ref, g_ref, m_ref, v_ref, d_ref, nm_ref, nv_ref):
        gv = g_ref[...]
        mn = ADAM_B1 * m_ref[...] + (1.0 - ADAM_B1) * gv
        vn = ADAM_B2 * v_ref[...] + (1.0 - ADAM_B2) * (gv * gv)
        mh = mn / bc1
        vh = vn / bc2
        d_ref[...] = -ADAM_LR * (mh / (jnp.sqrt(vh) + ADAM_EPS) + ADAM_WD * w_ref[...])
        nm_ref[...] = mn
        nv_ref[...] = vn

    blk = pl.BlockSpec((tr, c), lambda i: (i, 0))
    return pl.pallas_call(
        body, grid=(r // tr,), out_shape=tuple(_sds((r, c), F32) for _ in range(3)),
        in_specs=[blk, blk, blk, blk], out_specs=(blk, blk, blk),
        name=name, compiler_params=_cp(("parallel",)))(w, g, m, v)


def _pad_rows(a, rows, axis):
    pad = [(0, 0)] * a.ndim
    pad[axis] = (0, rows - a.shape[axis])
    return jnp.pad(a, pad)


def _vec(a):
    return a.reshape(1, -1)


def kernel(x, c, mix_norm_g, mix_ada_w, mix_ada_b, ffn_norm_g, ffn_ada_w, ffn_ada_b, ffn_w_in, ffn_w_out, conv_w_in, conv_b_in, conv_w_dw, conv_b_dw, conv_ln_g, conv_ln_b, conv_w_out, conv_b_out, kv_norm_g, kv_ada_w, kv_ada_b, kv_w, forget_b, attn_w_q, attn_w_o, final_norm_g, loss_target, m_mix_norm_g, m_mix_ada_w, m_mix_ada_b, m_ffn_norm_g, m_ffn_ada_w, m_ffn_ada_b, m_ffn_w_in, m_ffn_w_out, m_conv_w_in, m_conv_b_in, m_conv_w_dw, m_conv_b_dw, m_conv_ln_g, m_conv_ln_b, m_conv_w_out, m_conv_b_out, m_kv_norm_g, m_kv_ada_w, m_kv_ada_b, m_kv_w, m_forget_b, m_attn_w_q, m_attn_w_o, m_final_norm_g, v_mix_norm_g, v_mix_ada_w, v_mix_ada_b, v_ffn_norm_g, v_ffn_ada_w, v_ffn_ada_b, v_ffn_w_in, v_ffn_w_out, v_conv_w_in, v_conv_b_in, v_conv_w_dw, v_conv_b_dw, v_conv_ln_g, v_conv_ln_b, v_conv_w_out, v_conv_b_out, v_kv_norm_g, v_kv_ada_w, v_kv_ada_b, v_kv_w, v_forget_b, v_attn_w_q, v_attn_w_o, v_final_norm_g):
    xi, yi, ci = lax.axis_index("x"), lax.axis_index("y"), lax.axis_index("c")
    chip = 2 * xi + yi
    dev = 4 * xi + 2 * yi + ci
    s, d = x.shape[1], x.shape[2]
    f = ffn_w_out.shape[1] * 4
    x0 = x[0]
    nkv = kv_w.shape[1]
    nkv_all = 4 * nkv

    def my_halves(ws):
        return [lax.dynamic_index_in_dim(w.astype(BF16).reshape(2, w.shape[0] // 2, w.shape[1]), ci, axis=0, keepdims=False)
                for w in ws]

    def whole(gath, ws):
        return [g.reshape(4, w.shape[0], w.shape[1]) for g, w in zip(gath, ws)]

    wdw_loc = _pad_rows(conv_w_dw[0], HALO, 0)
    small = jnp.concatenate([c.reshape(-1), conv_b_in.reshape(-1), wdw_loc.reshape(-1), conv_b_dw.reshape(-1),
                             conv_ln_g.reshape(-1), conv_ln_b.reshape(-1), conv_b_out.reshape(-1)])
    n_small = small.shape[0]
    w_small = -(-n_small // (8 * LANE)) * LANE
    small = jnp.pad(small, (0, 8 * w_small - n_small)).reshape(8, w_small)
    first = [small] + my_halves([conv_w_in[0]])
    small_all, cw_in = _exchange(_Gather8(first), first, "ag_small_params_w_conv")
    small_all = small_all.reshape(8, 8 * w_small)
    cw_in, = whole([cw_in], [conv_w_in[0]])
    c_all = small_all[:, :d]
    per_chip = small_all[0::2]
    dq_ = d // 4
    o1 = d
    b_in_full = per_chip[:, o1:o1 + 2 * dq_].reshape(4, 1, 2 * dq_)
    o1 += 2 * dq_
    wdw_full = per_chip[:, o1:o1 + HALO * dq_].reshape(4, HALO, dq_).transpose(1, 0, 2).reshape(HALO, d)
    o1 += HALO * dq_
    bdw_full = per_chip[:, o1:o1 + dq_].reshape(1, d)
    lng_full = per_chip[:, o1 + dq_:o1 + 2 * dq_].reshape(1, d)
    lnb_full = per_chip[:, o1 + 2 * dq_:o1 + 3 * dq_].reshape(1, d)
    bout_full = per_chip[:, o1 + 3 * dq_:o1 + 4 * dq_].reshape(1, d)

    a_mix = _ada_fwd(c_all, mix_ada_w, "ada_mix")
    a_ffn = _ada_fwd(c_all, ffn_ada_w, "ada_ffn")
    a_kv = _ada_fwd(c_all, kv_ada_w[None], "ada_kv")
    n3 = mix_ada_w.shape[2]
    n2 = kv_ada_w.shape[1]
    ada_loc = jnp.concatenate([a_mix[0], a_mix[1], a_ffn[0], a_ffn[1], a_kv[0]], axis=1)
    w_ada = ada_loc.shape[1]
    ada_all = _allgather8(ada_loc, "ag_ada", True).reshape(8, 8, w_ada)
    ada_me = lax.dynamic_index_in_dim(ada_all, dev, axis=1, keepdims=False)[0::2]

    def ada_vec(off, n, bias):
        return ada_me[:, off:off + n].reshape(1, 4 * n) + bias.reshape(1, -1)

    ada_m0 = ada_vec(0, n3, mix_ada_b[0])
    ada_m1 = ada_vec(n3, n3, mix_ada_b[1])
    ada_f0 = ada_vec(2 * n3, n3, ffn_ada_b[0])
    ada_f1 = ada_vec(3 * n3, n3, ffn_ada_b[1])
    ada_k = ada_vec(4 * n3, n2, kv_ada_b)

    def split3(a):
        return a[:, :d], a[:, d:2 * d], a[:, 2 * d:3 * d]

    sh_m0, sc_m0, gt_m0 = split3(ada_m0)
    sh_m1, sc_m1, gt_m1 = split3(ada_m1)
    sh_f0, sc_f0, gt_f0 = split3(ada_f0)
    sh_f1, sc_f1, gt_f1 = split3(ada_f1)
    sh_k, sc_k = ada_k[:, :d], ada_k[:, d:2 * d]

    grp_a = [ffn_w_in[0]]
    grp_b = [ffn_w_out[0], conv_w_out[0], ffn_w_in[1]]
    grp_c = [ffn_w_out[1], kv_w, attn_w_q[0], attn_w_o[0]]
    mine_a, mine_b, mine_c = my_halves(grp_a), my_halves(grp_b), my_halves(grp_c)

    zero_b = jnp.zeros((1, d), F32)
    g_m0, g_m1 = _vec(mix_norm_g[0]), _vec(mix_norm_g[1])
    g_f0, g_f1 = _vec(ffn_norm_g[0]), _vec(ffn_norm_g[1])
    g_k, g_fin = _vec(kv_norm_g), _vec(final_norm_g)
    fb = jnp.pad(forget_b, (0, LANE - N_HEADS)).reshape(1, LANE)

    h0, glu, a_s, g_s, gath_a = _in_pair(x0, g_m0, sh_m0, sc_m0, cw_in, b_in_full, True, "conv_in",
                                         comm=(_Gather8(mine_a), mine_a))
    dwo, sw, gath_b = _dwconv_fwd(glu, wdw_full, bdw_full, lng_full, lnb_full, comm=(_Gather8(mine_b), mine_b))
    w_in0, = whole(gath_a, grp_a)
    w_out0, cw_out, w_in1 = whole(gath_b, grp_b)
    cw_out = cw_out.reshape(d, d)
    w_in = [w_in0, w_in1]
    x1 = _mm_res(sw, cw_out, bout_full, gt_m0, x0, "conv_out")
    hf0, act0, ug0, uu0, gath_c = _in_pair(x1, g_f0, sh_f0, sc_f0, w_in[0], None, False, "ffn0_in",
                                           comm=(_Gather8(mine_c), mine_c))
    w_out1, kvw, wq, wo = whole(gath_c, grp_c)
    w_out = [w_out0.reshape(f, d), w_out1.reshape(f, d)]
    kvw = kvw.transpose(1, 0, 2).reshape(d, nkv_all)
    wk, wv = kvw[:, :d], kvw[:, d:2 * d]
    wf = jnp.pad(kvw[:, 2 * d:], ((0, 0), (0, LANE - N_HEADS)))
    wq, wo = wq.reshape(d, d), wo.reshape(d, d)
    x2 = _mm_res(act0, w_out[0], zero_b, gt_f0, x1, "ffn0_out")
    hk, h1, kk, vv, qq, flog = _qkv(x2, (g_k, sh_k, sc_k), (g_m1, sh_m1, sc_m1), wk, wv, wf, wq)
    cum, cumt = _cumsum_fwd(flog, fb)
    kaug, kaugt, vtr = _attn_prep(kk, vv, cum)
    o, lse = _attn_fwd(qq, kaug, vtr, cumt)
    x3 = _mm_res(o, wo, zero_b, gt_m1, x2, "attn_out")
    hf1, act1, ug1, uu1 = _in_pair(x3, g_f1, sh_f1, sc_f1, w_in[1], None, False, "ffn1_in")
    lsum, dx4, d_gfin = _mm_res_final(act1, w_out[1], gt_f1, x3, g_fin, loss_target[0])
    loss = lax.psum(0.5 / d * jnp.sum(lsum), ("x", "y", "c"))

    nf = f // 2

    sel = jnp.stack([ci, chip]).astype(jnp.int32)

    def halves_of(gs):
        return [g.reshape(4, 2, g.shape[1] // 2, g.shape[2]) for g in gs]

    def reduce_adds(ps, lands, tag):
        pairs = [_add_halves(p_, l_, sel, f"{tag}_add{k}") for k, (p_, l_) in enumerate(zip(ps, lands))]
        return [q for q, _ in pairs], [o_ for _, o_ in pairs]

    def reduce_begin(gs, tag):
        ps = halves_of(gs)
        return reduce_adds(ps, _exchange(_SwapHalves(ps), ps, tag + "_swap"), tag)

    def reduce_sum(owns, lands, tag):
        return [_add_chips(o_, l_, sel, f"{tag}_sum{k}") for k, (o_, l_) in enumerate(zip(owns, lands))]

    def ffn_bwd(dx_out, x_in, hf, act, ug, uu, gain, scale, gate, w_in_l, w_out_l, tag, comm=None):
        res = _ffn_bwd_act(dx_out, gate, w_out_l, ug, uu, tag + "_bwd_act", comm=comm)
        dug, duu = res[0], res[1]
        dw_out, dgate = _dw_mm(act, [dx_out], nf, d, tag + "_dw_out", gate=gate, wfull=w_out_l, dgate_init=zero_b)
        terms = [(dug, 0, w_in_l, 0), (dug, nf, w_in_l, 1), (duu, 0, w_in_l, 2), (duu, nf, w_in_l, 3)]
        dx_in, dsh, dsc, dgn = _mm_normbwd(terms, x_in, dx_out, gain, scale, tag + "_bwd_in")
        dw_in = _dw_mm(hf, [dug, duu], d, nf, tag + "_dw_in")
        return dx_in, dw_in, dw_out[0], dsh, dsc, dgate, dgn, (res[2] if comm is not None else None)

    dx3, dw_in1, dw_out1, dsh_f1, dsc_f1, dgt_f1, dgn_f1, _ = ffn_bwd(dx4, x3, hf1, act1, ug1, uu1, g_f1, sc_f1, gt_f1, w_in[1], w_out[1], "ffn1")
    ps_1 = halves_of([dw_in1, dw_out1.reshape(4, f // 4, d)])
    do, deltat, swapped_1 = _do_kernel(dx3, gt_m1, wo, o, comm=(_SwapHalves(ps_1), ps_1))
    q16_1, own_1 = reduce_adds(ps_1, swapped_1, "rs_ffn1")
    dwo_att, dgt_m1 = _dw_mm(o, [dx3], d, d, "attn_dw_o", gate=gt_m1, wfull=wo, dgate_init=zero_b)
    dq, dk, dv, dcq, dck, land_1 = _attn_bwd(qq, do, kaug, kaugt, vv, cumt, lse, deltat, comm=(_ScatterChips(q16_1), q16_1))
    dwq = _dw_mm(h1, [dq], d, d, "attn_dw_q")[0]

    df, dfb = _cumsum_bwd(dcq[:, :2].reshape(N_HEADS, s), dck[:, :2].reshape(N_HEADS, s), flog, fb)
    dwk, dwv = _dw_mm(hk, [dk, dv], d, d, "kv_dw_kv")
    dwf = _dw_mm(hk, [df], d, LANE, "kv_dw_f")[0]
    dkvw = jnp.concatenate([dwk, dwv, dwf[:, :N_HEADS]], axis=1)
    dkvw = dkvw.reshape(d, 4, nkv).transpose(1, 0, 2)
    ps_2 = halves_of([dkvw, dwq.reshape(4, d // 4, d), dwo_att[0].reshape(4, d // 4, d)])
    terms = [(dk, 0, wk.reshape(1, d, d), 0), (dv, 0, wv.reshape(1, d, d), 0), (df, 0, wf.reshape(1, d, LANE), 0)]
    dx2, dsh_m1, dsc_m1, dgn_m1, dsh_k, dsc_k, dgn_k, swapped_2 = _mm_normbwd(
        [(dq, 0, wq.reshape(1, d, d), 0)], x2, dx3, g_m1, sc_m1, "attn_kv_bwd", second=(terms, g_k, sc_k),
        comm=(_SwapHalves(ps_2), ps_2))
    q16_2, own_2 = reduce_adds(ps_2, swapped_2, "rs_attn")
    dx1, dw_in0, dw_out0, dsh_f0, dsc_f0, dgt_f0, dgn_f0, land_2 = ffn_bwd(
        dx2, x1, hf0, act0, ug0, uu0, g_f0, sc_f0, gt_f0, w_in[0], w_out[0], "ffn0", comm=(_ScatterChips(q16_2), q16_2))

    dcw_out, dgt_m0_mm = _dw_mm(sw, [dx1], d, d, "conv_dw_out", gate=gt_m0, wfull=cw_out, dgate_init=zero_b)
    ps_3 = halves_of([dw_in0, dw_out0.reshape(4, f // 4, d), dcw_out[0].reshape(4, d // 4, d)])
    ddwo, d_lng, d_lnb, d_bdw, d_bout, dgt_m0, swapped_3 = _conv_bwd1(
        dx1, gt_m0, cw_out, bout_full, dwo, lng_full, lnb_full, dgt_m0_mm, comm=(_SwapHalves(ps_3), ps_3))
    q16_3, own_3 = reduce_adds(ps_3, swapped_3, "rs_ffn0")
    da, dg, d_wdw, d_bin_a, d_bin_g, land_3 = _dwconv_bwd(ddwo, glu, a_s, g_s, wdw_full, comm=(_ScatterChips(q16_3), q16_3))
    nc = cw_in.shape[2]
    terms = [(da, 0, cw_in, 0), (da, nc, cw_in, 1), (dg, 0, cw_in, 2), (dg, nc, cw_in, 3)]
    dx0, dsh_m0, dsc_m0, dgn_m0 = _mm_normbwd(terms, x0, dx1, g_m0, sc_m0, "conv_bwd_in")
    dcw_in = _dw_mm(h0, [da, dg], d, nc, "conv_dw_in")
    q16_4, own_4 = reduce_begin([dcw_in], "rs_conv")
    land_4 = _exchange(_ScatterChips(q16_4), q16_4, "rs_conv_scatter")

    sums = (reduce_sum(own_1, land_1, "rs_ffn1") + reduce_sum(own_2, land_2, "rs_attn")
            + reduce_sum(own_3, land_3, "rs_ffn0") + reduce_sum(own_4, land_4, "rs_conv"))
    reduced = [b.reshape(2 * b.shape[1], b.shape[2]) for b in _share_halves(sums)]
    g_w_in1, g_w_out1, g_kvw, g_wq, g_wo, g_w_in0, g_w_out0, g_cw_out, g_cw_in = reduced

    d_ada = [jnp.concatenate([dsh_m0, dsc_m0, dgt_m0], axis=1), jnp.concatenate([dsh_m1, dsc_m1, dgt_m1], axis=1),
             jnp.concatenate([dsh_f0, dsc_f0, dgt_f0], axis=1), jnp.concatenate([dsh_f1, dsc_f1, dgt_f1], axis=1),
             jnp.concatenate([dsh_k, dsc_k], axis=1)]
    fields = d_ada + [dgn_m0, dgn_m1, dgn_f0, dgn_f1, dgn_k, d_gfin, d_bin_a, d_bin_g, d_bdw, d_lng, d_lnb, d_bout,
                      d_wdw.reshape(1, -1), dfb]
    foffs = [0]
    for fl in fields:
        foffs.append(foffs[-1] + fl.shape[1])
    n_row = foffs[-1]
    w_row = -(-n_row // (8 * LANE)) * LANE
    row = jnp.pad(jnp.concatenate(fields, axis=1), ((0, 0), (0, 8 * w_row - n_row))).reshape(8, w_row)
    rows_all = _allgather8(row, "ag_small_grads", True).reshape(8, 8, w_row)
    rsum_small = _sum8(rows_all).reshape(1, 8 * w_row)
    rows_flat = rows_all.reshape(8, 8 * w_row)

    def fsum(i):
        return rsum_small[:, foffs[i]:foffs[i + 1]]

    cat = _silu_rows(c_all).T

    def ada_cols(i, n):
        full = rows_flat[:, foffs[i]:foffs[i + 1]].reshape(8, 4, n)
        return lax.dynamic_index_in_dim(full, chip, axis=1, keepdims=False)

    g_mix_ada_w = _ada_wgrad(cat, jnp.stack([ada_cols(0, n3), ada_cols(1, n3)]), "ada_mix_wgrad")
    g_ffn_ada_w = _ada_wgrad(cat, jnp.stack([ada_cols(2, n3), ada_cols(3, n3)]), "ada_ffn_wgrad")
    g_kv_ada_w = _ada_wgrad(cat, ada_cols(4, n2)[None], "ada_kv_wgrad")[0]

    def my_cols(v, n):
        return lax.dynamic_index_in_dim(v.reshape(4, n), chip, axis=0, keepdims=False)

    grads = {
        "mix_norm_g": jnp.concatenate([fsum(5), fsum(6)], axis=0),
        "mix_ada_w": g_mix_ada_w,
        "mix_ada_b": jnp.concatenate([fsum(0), fsum(1)], axis=0),
        "ffn_norm_g": jnp.concatenate([fsum(7), fsum(8)], axis=0),
        "ffn_ada_w": g_ffn_ada_w,
        "ffn_ada_b": jnp.concatenate([fsum(2), fsum(3)], axis=0),
        "ffn_w_in": jnp.stack([g_w_in0, g_w_in1]),
        "ffn_w_out": jnp.stack([g_w_out0, g_w_out1]),
        "conv_w_in": g_cw_in[None],
        "conv_b_in": my_cols(jnp.concatenate([fsum(11), fsum(12)], axis=1), 2 * dq_)[None],
        "conv_w_dw": lax.dynamic_index_in_dim(fsum(17).reshape(HALO, 4, dq_), chip, axis=1, keepdims=False)[:CONV_K][None],
        "conv_b_dw": my_cols(fsum(13), dq_)[None],
        "conv_ln_g": my_cols(fsum(14), dq_)[None],
        "conv_ln_b": my_cols(fsum(15), dq_)[None],
        "conv_w_out": g_cw_out[None],
        "conv_b_out": my_cols(fsum(16), dq_)[None],
        "kv_norm_g": fsum(9).reshape(-1),
        "kv_ada_w": g_kv_ada_w,
        "kv_ada_b": fsum(4).reshape(-1),
        "kv_w": g_kvw,
        "forget_b": fsum(18).reshape(-1)[:N_HEADS],
        "attn_w_q": g_wq[None],
        "attn_w_o": g_wo[None],
        "final_norm_g": fsum(10).reshape(-1),
    }
    weights = dict(mix_norm_g=mix_norm_g, mix_ada_w=mix_ada_w, mix_ada_b=mix_ada_b, ffn_norm_g=ffn_norm_g, ffn_ada_w=ffn_ada_w, ffn_ada_b=ffn_ada_b, ffn_w_in=ffn_w_in, ffn_w_out=ffn_w_out, conv_w_in=conv_w_in, conv_b_in=conv_b_in, conv_w_dw=conv_w_dw, conv_b_dw=conv_b_dw, conv_ln_g=conv_ln_g, conv_ln_b=conv_ln_b, conv_w_out=conv_w_out, conv_b_out=conv_b_out, kv_norm_g=kv_norm_g, kv_ada_w=kv_ada_w, kv_ada_b=kv_ada_b, kv_w=kv_w, forget_b=forget_b, attn_w_q=attn_w_q, attn_w_o=attn_w_o, final_norm_g=final_norm_g)
    moms = dict(mix_norm_g=(m_mix_norm_g, v_mix_norm_g), mix_ada_w=(m_mix_ada_w, v_mix_ada_w), mix_ada_b=(m_mix_ada_b, v_mix_ada_b), ffn_norm_g=(m_ffn_norm_g, v_ffn_norm_g), ffn_ada_w=(m_ffn_ada_w, v_ffn_ada_w), ffn_ada_b=(m_ffn_ada_b, v_ffn_ada_b), ffn_w_in=(m_ffn_w_in, v_ffn_w_in), ffn_w_out=(m_ffn_w_out, v_ffn_w_out), conv_w_in=(m_conv_w_in, v_conv_w_in), conv_b_in=(m_conv_b_in, v_conv_b_in), conv_w_dw=(m_conv_w_dw, v_conv_w_dw), conv_b_dw=(m_conv_b_dw, v_conv_b_dw), conv_ln_g=(m_conv_ln_g, v_conv_ln_g), conv_ln_b=(m_conv_ln_b, v_conv_ln_b), conv_w_out=(m_conv_w_out, v_conv_w_out), conv_b_out=(m_conv_b_out, v_conv_b_out), kv_norm_g=(m_kv_norm_g, v_kv_norm_g), kv_ada_w=(m_kv_ada_w, v_kv_ada_w), kv_ada_b=(m_kv_ada_b, v_kv_ada_b), kv_w=(m_kv_w, v_kv_w), forget_b=(m_forget_b, v_forget_b), attn_w_q=(m_attn_w_q, v_attn_w_q), attn_w_o=(m_attn_w_o, v_attn_w_o), final_norm_g=(m_final_norm_g, v_final_norm_g))
    names = list(weights)

    deltas, new_m, new_v = {}, {}, {}
    small_names = [n for n in names if weights[n].size < (1 << 16)]
    for n in names:
        if n in small_names:
            continue
        w = weights[n]
        w2 = w.reshape(-1, w.shape[-1])
        dl, nm, nv = _adamw(w2, grads[n].reshape(w2.shape), moms[n][0].reshape(w2.shape), moms[n][1].reshape(w2.shape), "adamw_" + n)
        deltas[n], new_m[n], new_v[n] = dl.reshape(w.shape), nm.reshape(w.shape), nv.reshape(w.shape)

    def pack_small(get):
        flat = jnp.concatenate([get(n).reshape(-1) for n in small_names])
        rows_ = -(-flat.shape[0] // (8 * LANE)) * 8
        return jnp.pad(flat, (0, rows_ * LANE - flat.shape[0])).reshape(rows_, LANE)

    ws, gs = pack_small(lambda n: weights[n]), pack_small(lambda n: grads[n])
    ms_, vs_ = pack_small(lambda n: moms[n][0]), pack_small(lambda n: moms[n][1])
    vs_ = jnp.where(jnp.arange(vs_.size).reshape(vs_.shape) < sum(weights[n].size for n in small_names), vs_, 1.0)
    dl, nm, nv = _adamw(ws, gs, ms_, vs_, "adamw_small")
    off = 0
    for n in small_names:
        sz = weights[n].size
        shp = weights[n].shape
        deltas[n] = dl.reshape(-1)[off:off + sz].reshape(shp)
        new_m[n] = nm.reshape(-1)[off:off + sz].reshape(shp)
        new_v[n] = nv.reshape(-1)[off:off + sz].reshape(shp)
        off += sz

    grad_out = [grads[n].reshape(weights[n].shape) for n in names]
    return (loss, dx0[None], *grad_out, *[deltas[n] for n in names], *[new_m[n] for n in names], *[new_v[n] for n in names])
```

```python
import functools

import jax
import jax.numpy as jnp
from jax import lax
from jax.experimental import pallas as pl
from jax.experimental.pallas import tpu as pltpu

F32 = jnp.float32
BF16 = jnp.bfloat16
MESH = pl.DeviceIdType.MESH

EPS = 1e-6
N_HEADS = 16
HEAD_DIM = 64
CONV_K = 31
LANE = 128
SUBLANES = 8
HALO = 32
ATT_FWD_TQ = 2048
ATT_TQ = 1024
ATT_TK = 512
NPIECE = 3
SPARE = (HEAD_DIM, 0)
VMEM_MB = 48
ATT_BWD_VMEM_MB = 56

ADAM_LR = 0.001
ADAM_B1 = 0.9
ADAM_B2 = 0.999
ADAM_EPS = 1e-08
ADAM_WD = 0.01
ADAM_STEP = 10


def _sds(shape, dtype):
    return jax.ShapeDtypeStruct(tuple(shape), dtype)


def _cp(sem=None, vmem_mb=VMEM_MB):
    return pltpu.CompilerParams(dimension_semantics=sem, vmem_limit_bytes=vmem_mb << 20)


def _tile(n, pref):
    return pref if n % pref == 0 else n


def _row_tile(r, mult, width=1024):
    cap = max(mult, (512 * 1024 // width) // mult * mult)
    for cand in range(cap, mult - 1, -mult):
        if r % cand == 0:
            return cand
    return r


def _resident(shape):
    nd = len(shape)
    return pl.BlockSpec(tuple(shape), lambda *_: (0,) * nd, pipeline_mode=pl.Buffered(1))


def _dot(a, b):
    return jnp.dot(a, b, preferred_element_type=F32)


def _dot_nt(a, b):
    return lax.dot_general(a, b, (((1,), (1,)), ((), ())), preferred_element_type=F32)


def _dot_tn(a, b):
    return lax.dot_general(a, b, (((0,), (0,)), ((), ())), preferred_element_type=F32)


def _sigmoid(x):
    return 1.0 / (1.0 + jnp.exp(-x))


def _colsum(x):
    return jnp.sum(x, axis=0, keepdims=True)


def _rms_parts(x):
    rstd = lax.rsqrt(jnp.mean(x * x, axis=-1, keepdims=True) + EPS)
    return x * rstd, rstd


class _Gather8:
    def __init__(self, xs):
        self.n = len(xs)
        self.m = [x.shape[0] for x in xs]
        self.land = [_sds((8 * x.shape[0],) + tuple(x.shape[1:]), x.dtype) for x in xs]
        self.sems = [pltpu.SemaphoreType.DMA((7 * self.n,)), pltpu.SemaphoreType.DMA((7 * self.n,)),
                     pltpu.SemaphoreType.DMA((self.n,))]

    def _parts(self, a, x_refs, out_refs, send_sems, recv_sems, local_sems):
        x, y, c = lax.axis_index("x"), lax.axis_index("y"), lax.axis_index("c")
        me, sibling = (x, y, c), (x, y, 1 - c)
        chips = [(1 - x, y), (x, 1 - y), (1 - x, 1 - y)]
        m_per, x_ref, out_ref = self.m[a], x_refs[a], out_refs[a]

        def rows(px, py, pc):
            return out_ref.at[pl.ds((4 * px + 2 * py + pc) * m_per, m_per)]

        def copy(k, block, to, src=None):
            return pltpu.make_async_remote_copy(
                src_ref=rows(*block) if src is None else src, dst_ref=rows(*block),
                send_sem=send_sems.at[7 * a + k], recv_sem=recv_sems.at[7 * a + k], device_id=to, device_id_type=MESH)

        mine = pltpu.make_async_copy(x_ref, rows(*me), local_sems.at[a])
        first = [copy(0, me, sibling, src=x_ref)]
        first += [copy(1 + j, me, (*chip, c), src=x_ref) for j, chip in enumerate(chips)]
        passed = [copy(4 + j, (*chip, c), sibling) for j, chip in enumerate(chips)]
        return c, me, sibling, chips, copy, mine, first, passed

    def start(self, *refs):
        for a in range(self.n):
            _, _, _, _, _, mine, first, _ = self._parts(a, *refs)
            mine.start()
            for cp in first:
                cp.start()

    def finish(self, *refs):
        parts = [self._parts(a, *refs) for a in range(self.n)]
        for j in range(3):
            for c, me, sibling, chips, copy, mine, first, passed in parts:
                copy(1 + j, (*chips[j], c), me).wait_recv()
                passed[j].start()
        for c, me, sibling, chips, copy, mine, first, passed in parts:
            copy(0, sibling, me).wait_recv()
            for j, chip in enumerate(chips):
                copy(4 + j, (*chip, 1 - c), me).wait_recv()
            for cp in first + passed:
                cp.wait_send()
            mine.wait()


class _ScatterChips:
    def __init__(self, qs):
        self.n = len(qs)
        self.land = [_sds((3,) + tuple(q.shape[1:]), q.dtype) for q in qs]
        self.sems = [pltpu.SemaphoreType.DMA((3 * self.n,)), pltpu.SemaphoreType.DMA((3 * self.n,))]

    def _copies(self, q_refs, land_refs, send_sems, recv_sems):
        x, y, c = lax.axis_index("x"), lax.axis_index("y"), lax.axis_index("c")
        chips = [(1 - x, y), (x, 1 - y), (1 - x, 1 - y)]
        return [pltpu.make_async_remote_copy(
            src_ref=q_refs[a].at[2 * cx + cy], dst_ref=land_refs[a].at[k],
            send_sem=send_sems.at[3 * a + k], recv_sem=recv_sems.at[3 * a + k],
            device_id=(cx, cy, c), device_id_type=MESH) for a in range(self.n) for k, (cx, cy) in enumerate(chips)]

    def start(self, *refs):
        for cp in self._copies(*refs):
            cp.start()

    def finish(self, *refs):
        copies = self._copies(*refs)
        for cp in copies:
            cp.wait_recv()
        for cp in copies:
            cp.wait_send()


class _SwapHalves:
    def __init__(self, ps):
        self.n = len(ps)
        self.land = [_sds((p.shape[0],) + tuple(p.shape[2:]), p.dtype) for p in ps]
        self.nb = [p.shape[0] for p in ps]
        tot = sum(self.nb)
        self.sems = [pltpu.SemaphoreType.DMA((tot,)), pltpu.SemaphoreType.DMA((tot,))]

    def _copies(self, p_refs, land_refs, send_sems, recv_sems):
        x, y, c = lax.axis_index("x"), lax.axis_index("y"), lax.axis_index("c")
        out, k = [], 0
        for a in range(self.n):
            for j in range(self.nb[a]):
                out.append(pltpu.make_async_remote_copy(
                    src_ref=p_refs[a].at[j, 1 - c], dst_ref=land_refs[a].at[j], send_sem=send_sems.at[k],
                    recv_sem=recv_sems.at[k], device_id=(x, y, 1 - c), device_id_type=MESH))
                k += 1
        return out

    start = _ScatterChips.start
    finish = _ScatterChips.finish


def _hosted_call(body, comm, *, grid, in_specs, out_specs, out_shape, scratch_shapes, name, sem, args, vmem_mb=VMEM_MB):
    def first():
        return functools.reduce(jnp.logical_and, [pl.program_id(a) == 0 for a in range(len(grid))])

    def last():
        return functools.reduce(jnp.logical_and, [pl.program_id(a) == g - 1 for a, g in enumerate(grid)])

    out_specs = tuple(out_specs) if isinstance(out_specs, (tuple, list)) else (out_specs,)
    out_shape = tuple(out_shape) if isinstance(out_shape, (tuple, list)) else (out_shape,)
    if comm is None:
        return pl.pallas_call(body, grid=grid, in_specs=list(in_specs), out_specs=out_specs, out_shape=out_shape,
                              scratch_shapes=list(scratch_shapes), name=name, compiler_params=_cp(sem, vmem_mb))(*args)
    ex, srcs = comm
    n_in, n_out, n_scr, n_ex = len(in_specs), len(out_shape), len(scratch_shapes), ex.n

    def wrapped(*refs):
        ins, src_refs = refs[:n_in], refs[n_in:n_in + n_ex]
        o0 = n_in + n_ex
        outs, land_refs = refs[o0:o0 + n_out], refs[o0 + n_out:o0 + n_out + n_ex]
        s0 = o0 + n_out + n_ex
        scr, sems = refs[s0:s0 + n_scr], refs[s0 + n_scr:]

        @pl.when(first())
        def _():
            ex.start(src_refs, land_refs, *sems)

        body(*ins, *outs, *scr)

        @pl.when(last())
        def _():
            ex.finish(src_refs, land_refs, *sems)

    hbm = pl.BlockSpec(memory_space=pl.ANY)
    res = pl.pallas_call(
        wrapped, grid=grid, in_specs=[*in_specs, *[hbm] * n_ex], out_specs=(*out_specs, *[hbm] * n_ex),
        out_shape=(*out_shape, *ex.land), scratch_shapes=[*scratch_shapes, *ex.sems], name=name,
        compiler_params=_cp(tuple("arbitrary" for _ in grid), vmem_mb))(*args, *srcs)
    return (*res[:n_out], list(res[n_out:]))


def _exchange(ex, srcs, name, in_vmem=False):
    n = ex.n

    def body(*refs):
        src_refs, land_refs, sems = refs[:n], refs[n:2 * n], refs[2 * n:]
        ex.start(src_refs, land_refs, *sems)
        ex.finish(src_refs, land_refs, *sems)

    spec = pl.BlockSpec(memory_space=pltpu.VMEM if in_vmem else pl.ANY)
    return list(pl.pallas_call(
        body, out_shape=tuple(ex.land), in_specs=[spec] * n, out_specs=tuple([spec] * n),
        scratch_shapes=ex.sems, name=name)(*srcs))


def _allgather8(x_shard, name, in_vmem):
    return _exchange(_Gather8([x_shard]), [x_shard], name, in_vmem)[0]


def _share_halves(bufs):
    n = len(bufs)

    def body(*refs):
        b_refs, out_refs, send_sems, recv_sems = refs[:n], refs[n:2 * n], refs[2 * n], refs[2 * n + 1]
        x, y, c = lax.axis_index("x"), lax.axis_index("y"), lax.axis_index("c")
        copies = [pltpu.make_async_remote_copy(
            src_ref=b_refs[k].at[c], dst_ref=out_refs[k].at[c], send_sem=send_sems.at[k], recv_sem=recv_sems.at[k],
            device_id=(x, y, 1 - c), device_id_type=MESH) for k in range(n)]
        for cp in copies:
            cp.start()
        for cp in copies:
            cp.wait_recv()
        for cp in copies:
            cp.wait_send()

    hbm = pl.BlockSpec(memory_space=pl.ANY)
    return pl.pallas_call(
        body, out_shape=tuple(_sds(b.shape, b.dtype) for b in bufs), in_specs=[hbm] * n, out_specs=tuple([hbm] * n),
        scratch_shapes=[pltpu.SemaphoreType.DMA((n,)), pltpu.SemaphoreType.DMA((n,))],
        input_output_aliases={k: k for k in range(n)}, name="rs_share_halves")(*bufs)


def _add_halves(p, land, sel, name):
    nb, _, r, w = p.shape
    tr = _row_tile(r, 16, w)

    def body(sel_ref, p_ref, l_ref, q16_ref, own_ref):
        q = p_ref[0, 0] + l_ref[0]
        q16_ref[0] = q.astype(BF16)

        @pl.when(pl.program_id(1) == sel_ref[1])
        def _():
            own_ref[...] = q

    gs = pltpu.PrefetchScalarGridSpec(
        num_scalar_prefetch=1, grid=(r // tr, nb),
        in_specs=[pl.BlockSpec((1, 1, tr, w), lambda i, j, sl: (j, sl[0], i, 0)),
                  pl.BlockSpec((1, tr, w), lambda i, j, sl: (j, i, 0))],
        out_specs=(pl.BlockSpec((1, tr, w), lambda i, j, sl: (j, i, 0)), pl.BlockSpec((tr, w), lambda i, j, sl: (i, 0))))
    return pl.pallas_call(body, grid_spec=gs, out_shape=(_sds((nb, r, w), BF16), _sds((r, w), F32)), name=name,
                          compiler_params=_cp(("parallel", "arbitrary")))(sel, p, land)


def _add_chips(own, land, sel, name):
    r, w = own.shape
    tr = _row_tile(r, 16, w)

    def body(sel_ref, q_ref, l_ref, o_ref):
        o_ref[0] = ((q_ref[...] + l_ref[0].astype(F32)) + l_ref[1].astype(F32)) + l_ref[2].astype(F32)

    gs = pltpu.PrefetchScalarGridSpec(
        num_scalar_prefetch=1, grid=(r // tr,),
        in_specs=[pl.BlockSpec((tr, w), lambda i, sl: (i, 0)), pl.BlockSpec((3, tr, w), lambda i, sl: (0, i, 0))],
        out_specs=pl.BlockSpec((1, tr, w), lambda i, sl: (sl[0], i, 0)))
    return pl.pallas_call(body, grid_spec=gs, out_shape=_sds((2, r, w), F32), name=name,
                          compiler_params=_cp(("parallel",)))(sel, own, land)


def _sum8(g):
    _, m, n = g.shape

    def body(g_ref, o_ref):
        acc = g_ref[0]
        for k in range(1, 8):
            acc = acc + g_ref[k]
        o_ref[...] = acc

    return pl.pallas_call(body, out_shape=_sds((m, n), g.dtype), name="sum8")(g)


def _ada_fwd(c_all, w3, name):
    nl, d, n = w3.shape
    tn = 256

    def body(c_ref, w_ref, o_ref):
        cc = c_ref[...]
        ca = (cc * _sigmoid(cc)).astype(BF16)
        o_ref[0] = _dot(ca, w_ref[0].astype(BF16))

    return pl.pallas_call(
        body, grid=(nl, n // tn), out_shape=_sds((nl, 8, n), F32),
        in_specs=[pl.BlockSpec((8, d), lambda l, j: (0, 0)), pl.BlockSpec((1, d, tn), lambda l, j: (l, 0, j))],
        out_specs=pl.BlockSpec((1, 8, tn), lambda l, j: (l, 0, j)),
        name=name, compiler_params=_cp(("parallel", "parallel")))(c_all, w3)


def _in_pair(x, gain, shift, scale, wg, bias, conv, name, comm=None):
    s, d = x.shape
    n = wg.shape[2]
    ts = _tile(s, 512)

    def body(*refs):
        if conv:
            x_ref, g_ref, sh_ref, sc_ref, w_ref, b_ref, h_ref, o_ref, sa_ref, sb_ref = refs
        else:
            x_ref, g_ref, sh_ref, sc_ref, w_ref, h_ref, o_ref, sa_ref, sb_ref = refs
        xhat, _ = _rms_parts(x_ref[...])
        h = ((xhat * g_ref[...]) * (1.0 + sc_ref[...]) + sh_ref[...]).astype(BF16)
        h_ref[...] = h
        for q in range(2):
            a = _dot(h, w_ref[q])
            b = _dot(h, w_ref[q + 2])
            cs = pl.ds(q * n, n)
            if conv:
                a = a + b_ref[q]
                b = b + b_ref[q + 2]
                o_ref[:, cs] = a * _sigmoid(b)
            else:
                o_ref[:, cs] = (a * _sigmoid(a) * b).astype(BF16)
            sa_ref[:, cs] = a.astype(BF16)
            sb_ref[:, cs] = b.astype(BF16)

    vec = pl.BlockSpec((1, d), lambda i: (0, 0))
    in_specs = [pl.BlockSpec((ts, d), lambda i: (i, 0)), vec, vec, vec, _resident(wg.shape)]
    args = [x, gain, shift, scale, wg]
    if conv:
        in_specs.append(_resident(bias.shape))
        args.append(bias)
    tile = pl.BlockSpec((ts, 2 * n), lambda i: (i, 0))
    return _hosted_call(
        body, comm, grid=(s // ts,),
        out_shape=(_sds((s, d), BF16), _sds((s, 2 * n), F32 if conv else BF16), _sds((s, 2 * n), BF16), _sds((s, 2 * n), BF16)),
        in_specs=in_specs, out_specs=(pl.BlockSpec((ts, d), lambda i: (i, 0)), tile, tile, tile),
        scratch_shapes=[], name=name, sem=("parallel",), args=args)


def _shift_copies(buf, shf):
    n = shf.shape[1]
    for r in range(1, SUBLANES):
        shf[r - 1, :, :] = buf[pl.ds(r, n), :]


def _shifted(buf, shf, start, n, cs):
    a, r = divmod(start, SUBLANES)
    if r == 0:
        return buf[pl.ds(start, n), cs]
    return shf[r - 1, pl.ds(a * SUBLANES, n), cs]


def _dwconv_fwd(glu, wdw, bdw, lng, lnb, comm=None):
    s, d = glu.shape
    ts = _tile(s, 256)
    rb, cb = 32, 256

    def body(cur_ref, halo_ref, w_ref, b_ref, g_ref, be_ref, dwo_ref, sw_ref, buf, shf):
        i = pl.program_id(0)

        @pl.when(i == 0)
        def _():
            buf[pl.ds(0, HALO), :] = jnp.zeros((HALO, d), F32)

        @pl.when(i > 0)
        def _():
            buf[pl.ds(0, HALO), :] = halo_ref[...]

        buf[pl.ds(HALO, ts), :] = cur_ref[...]
        _shift_copies(buf, shf)
        for r in range(ts // rb):
            for cc in range(d // cb):
                cs = pl.ds(cc * cb, cb)
                acc = jnp.zeros((rb, cb), F32) + b_ref[:, cs]
                for k in range(CONV_K):
                    acc = acc + w_ref[pl.ds(k, 1), cs] * _shifted(buf, shf, HALO - (CONV_K - 1) + k + r * rb, rb, cs)
                dwo_ref[pl.ds(r * rb, rb), cs] = acc
            rows = pl.ds(r * rb, rb)
            yv = dwo_ref[rows, :]
            mu = jnp.mean(yv, axis=-1, keepdims=True)
            yc = yv - mu
            var = jnp.mean(yc * yc, axis=-1, keepdims=True)
            ln = yc * lax.rsqrt(var + EPS) * g_ref[...] + be_ref[...]
            sw_ref[rows, :] = (ln * _sigmoid(ln)).astype(BF16)

    vec = pl.BlockSpec((1, d), lambda i: (0, 0))
    return _hosted_call(
        body, comm, grid=(s // ts,), out_shape=(_sds((s, d), F32), _sds((s, d), BF16)),
        in_specs=[pl.BlockSpec((ts, d), lambda i: (i, 0)),
                  pl.BlockSpec((HALO, d), lambda i: (jnp.maximum(i * (ts // HALO) - 1, 0), 0)),
                  pl.BlockSpec((HALO, d), lambda i: (0, 0)), vec, vec, vec],
        out_specs=(pl.BlockSpec((ts, d), lambda i: (i, 0)), pl.BlockSpec((ts, d), lambda i: (i, 0))),
        scratch_shapes=[pltpu.VMEM((HALO + ts, d), F32), pltpu.VMEM((SUBLANES - 1, HALO + ts - SUBLANES, d), F32)],
        name="dwconv_fwd", sem=("parallel",),
        args=(glu, glu, wdw, bdw, lng, lnb))


def _mm_res(a, w, b, gate, x, name):
    s, k = a.shape
    d = w.shape[1]
    ts = _tile(s, 512)

    def body(a_ref, w_ref, b_ref, g_ref, x_ref, o_ref):
        yv = _dot(a_ref[...], w_ref[...]) + b_ref[...]
        o_ref[...] = x_ref[...] + g_ref[...] * yv

    vec = pl.BlockSpec((1, d), lambda i: (0, 0))
    return pl.pallas_call(
        body, grid=(s // ts,), out_shape=_sds((s, d), F32),
        in_specs=[pl.BlockSpec((ts, k), lambda i: (i, 0)), _resident((k, d)), vec, vec, pl.BlockSpec((ts, d), lambda i: (i, 0))],
        out_specs=pl.BlockSpec((ts, d), lambda i: (i, 0)),
        name=name, compiler_params=_cp(("parallel",)))(a, w, b, gate, x)


def _qkv(x, kvp, mxp, wk, wv, wf, wq):
    s, d = x.shape
    ts = _tile(s, 512)
    qscale = HEAD_DIM ** -0.5

    def body(x_ref, gk, shk, sck, gm, shm, scm, wk_ref, wv_ref, wf_ref, wq_ref, hk_ref, h1_ref, k_ref, v_ref, q_ref, f_ref):
        xhat, _ = _rms_parts(x_ref[...])
        hk = ((xhat * gk[...]) * (1.0 + sck[...]) + shk[...]).astype(BF16)
        h1 = ((xhat * gm[...]) * (1.0 + scm[...]) + shm[...]).astype(BF16)
        hk_ref[...] = hk
        h1_ref[...] = h1
        k_ref[...] = _dot(hk, wk_ref[...]).astype(BF16)
        v_ref[...] = _dot(hk, wv_ref[...]).astype(BF16)
        f_ref[...] = _dot(hk, wf_ref[...])
        q_ref[...] = (_dot(h1, wq_ref[...]) * qscale).astype(BF16)

    vec = pl.BlockSpec((1, d), lambda i: (0, 0))
    row = pl.BlockSpec((ts, d), lambda i: (i, 0))
    return pl.pallas_call(
        body, grid=(s // ts,),
        out_shape=tuple(_sds((s, d), BF16) for _ in range(5)) + (_sds((s, LANE), F32),),
        in_specs=[row, vec, vec, vec, vec, vec, vec, _resident((d, d)), _resident((d, d)), _resident((d, LANE)), _resident((d, d))],
        out_specs=(row, row, row, row, row, pl.BlockSpec((ts, LANE), lambda i: (i, 0))),
        name="qkv_proj", compiler_params=_cp(("parallel",)))(x, *kvp, *mxp, wk, wv, wf, wq)


def _log_sigmoid(z):
    return jnp.minimum(z, 0.0) - jnp.log(1.0 + jnp.exp(-jnp.abs(z)))


def _cumsum_fwd(flog, fb):
    s = flog.shape[0]
    ts = _tile(s, 256)

    def body(f_ref, b_ref, cum_ref, cumt_ref, carry):
        @pl.when(pl.program_id(0) == 0)
        def _():
            carry[...] = jnp.zeros_like(carry)

        ls = _log_sigmoid(f_ref[...] + b_ref[...])
        r = lax.broadcasted_iota(jnp.int32, (ts, ts), 0)
        cidx = lax.broadcasted_iota(jnp.int32, (ts, ts), 1)
        tri = (cidx <= r).astype(F32)
        cs = jnp.dot(tri, ls, preferred_element_type=F32, precision=lax.Precision.HIGHEST) + carry[...]
        cum_ref[...] = cs
        cumt_ref[...] = cs.T
        carry[...] = cs[ts - 1:ts, :]

    return pl.pallas_call(
        body, grid=(s // ts,), out_shape=(_sds((s, LANE), F32), _sds((LANE, s), F32)),
        in_specs=[pl.BlockSpec((ts, LANE), lambda i: (i, 0)), pl.BlockSpec((1, LANE), lambda i: (0, 0))],
        out_specs=(pl.BlockSpec((ts, LANE), lambda i: (i, 0)), pl.BlockSpec((LANE, ts), lambda i: (0, i))),
        scratch_shapes=[pltpu.VMEM((1, LANE), F32)],
        name="forget_cumsum", compiler_params=_cp(("arbitrary",)))(flog, fb)


def _pick_row(m, idx):
    r = lax.broadcasted_iota(jnp.int32, (m.shape[0], 1), 0)
    return jnp.sum(jnp.where(r == idx, m, 0.0), axis=0, keepdims=True)


def _pick_col(m, idx):
    cidx = lax.broadcasted_iota(jnp.int32, (1, m.shape[1]), 1)
    return jnp.sum(jnp.where(cidx == idx, m, 0.0), axis=1, keepdims=True)


def _split3(x):
    hi = x.astype(BF16)
    r1 = x - hi.astype(F32)
    mid = r1.astype(BF16)
    lo = (r1 - mid.astype(F32)).astype(BF16)
    return hi, mid, lo


def _head_mask(lane, hh):
    lo = lane < HEAD_DIM
    return lo if hh == 0 else jnp.logical_not(lo)


def _attn_prep(k, v, cum):
    s, d = k.shape
    npair = d // LANE
    tc = _tile(s, 1024)

    def body(k_ref, v_ref, c_ref, ka_ref, kt_ref, vt_ref):
        p = pl.program_id(0)
        lane = lax.broadcasted_iota(jnp.int32, (1, LANE), 1)
        kk = k_ref[...]
        vv = v_ref[...].astype(F32)
        ckt = c_ref[...]
        for hh in range(2):
            head = _head_mask(lane, hh)
            b = SPARE[hh]
            ck = _pick_col(ckt, 2 * p + hh)
            extra = jnp.where(lane == b + NPIECE, 1.0, 0.0).astype(BF16) + jnp.zeros((tc, LANE), BF16)
            for n_, pc in enumerate(_split3(ck)):
                extra = jnp.where(lane == b + n_, pc, extra)
            ka = jnp.where(head, kk, extra)
            ka_ref[0, hh] = ka
            kt_ref[0, hh] = ka.astype(F32).T.astype(BF16)
            vx = jnp.where(head, vv, jnp.where(lane == b, 1.0, 0.0))
            vt_ref[0, hh] = vx.T.astype(BF16)

    blk = pl.BlockSpec((tc, LANE), lambda p, c: (c, p))
    return pl.pallas_call(
        body, grid=(npair, s // tc),
        out_shape=(_sds((npair, 2, s, LANE), BF16), _sds((npair, 2, LANE, s), BF16), _sds((npair, 2, LANE, s), BF16)),
        in_specs=[blk, blk, pl.BlockSpec((tc, LANE), lambda p, c: (c, 0))],
        out_specs=(pl.BlockSpec((1, 2, tc, LANE), lambda p, c: (p, 0, c, 0)),
                   pl.BlockSpec((1, 2, LANE, tc), lambda p, c: (p, 0, 0, c)),
                   pl.BlockSpec((1, 2, LANE, tc), lambda p, c: (p, 0, 0, c))),
        name="fox_attn_prep", compiler_params=_cp(("parallel", "parallel")))(k, v, cum)


def _q_aug(qq, lane, hh):
    b = SPARE[hh]
    sel = jnp.logical_and(lane >= b, lane < b + NPIECE)
    neg = jnp.full((1, LANE), -1.0, BF16)
    zl = jnp.zeros((1, LANE), BF16)
    return jnp.where(_head_mask(lane, hh), qq, jnp.where(sel, neg, zl))


def _attn_fwd(q, kaug, vtr, cumt):
    s, d = q.shape
    tq = _tile(s, ATT_FWD_TQ)
    tk = _tile(s, ATT_TK)
    npair = d // LANE
    npart = max(1, tq // tk)

    def body(q_ref, ka_ref, vt_ref, cumt_ref, o_ref, lse_ref):
        p = pl.program_id(0)
        i = pl.program_id(1)
        lane = lax.broadcasted_iota(jnp.int32, (1, LANE), 1)
        qq = q_ref[...]
        qx = (_q_aug(qq, lane, 0), _q_aug(qq, lane, 1))
        cqt = cumt_ref[:, pl.ds(pl.multiple_of(i * tq, tq), tq)]
        cq = (_pick_row(cqt, 2 * p), _pick_row(cqt, 2 * p + 1))
        jd = (i * tq) // tk

        def kv_step(j, carry, diag, q_lo=0):
            ks = pl.multiple_of(j * tk, tk)
            nq_ = tq - q_lo
            if diag:
                krow = lax.broadcasted_iota(jnp.int32, (tk, nq_), 0) + j * tk
                qcol = lax.broadcasted_iota(jnp.int32, (tk, nq_), 1) + (i * tq + q_lo)
                causal = krow <= qcol
            out = []
            for hh in range(2):
                m_all, acc_all = carry[2 * hh], carry[2 * hh + 1]
                m, acc, cqh = m_all[:, q_lo:], acc_all[:, q_lo:], cq[hh][:, q_lo:]
                sc = _dot_nt(ka_ref[0, hh, pl.ds(ks, tk), :], qx[hh][q_lo:, :])
                if diag:
                    sc = jnp.where(causal, sc, -jnp.inf)
                mx = jnp.max(sc, axis=0, keepdims=True) + cqh
                mn = jnp.maximum(m, mx)
                alpha = jnp.exp(m - mn)
                pt = jnp.exp(sc + (cqh - mn)).astype(BF16)
                acc = alpha * acc + _dot(vt_ref[0, hh, :, pl.ds(ks, tk)], pt)
                if q_lo:
                    mn = jnp.concatenate([m_all[:, :q_lo], mn], axis=1)
                    acc = jnp.concatenate([acc_all[:, :q_lo], acc], axis=1)
                out += [mn, acc]
            return tuple(out)

        minit = jnp.full((1, tq), -jnp.inf, F32)
        ainit = jnp.zeros((LANE, tq), F32)
        carry = (minit, ainit, minit, ainit)
        for pj in range(npart):
            carry = kv_step(jd + pj, carry, True, q_lo=pj * tk)
        carry = lax.fori_loop(0, jd, lambda j, cr: kv_step(j, cr, False), carry)
        m0, a0, m1, a1 = carry
        l0 = a0[SPARE[0]:SPARE[0] + 1, :]
        l1 = a1[SPARE[1]:SPARE[1] + 1, :]
        row = lax.broadcasted_iota(jnp.int32, (LANE, 1), 0)
        ot = jnp.where(row < HEAD_DIM, a0 / l0, a1 / l1)
        o_ref[...] = ot.T.astype(BF16)
        r8 = lax.broadcasted_iota(jnp.int32, (8, 1), 0)
        lse_ref[0] = jnp.where(r8 == 0, m0 + jnp.log(l0), jnp.where(r8 == 1, m1 + jnp.log(l1), 0.0))

    return pl.pallas_call(
        body, grid=(npair, s // tq), out_shape=(_sds((s, d), BF16), _sds((npair, 8, s), F32)),
        in_specs=[pl.BlockSpec((tq, LANE), lambda p, i: (i, p)),
                  pl.BlockSpec((1, 2, s, LANE), lambda p, i: (p, 0, 0, 0)),
                  pl.BlockSpec((1, 2, LANE, s), lambda p, i: (p, 0, 0, 0)),
                  pl.BlockSpec((N_HEADS, s), lambda p, i: (0, 0))],
        out_specs=(pl.BlockSpec((tq, LANE), lambda p, i: (i, p)), pl.BlockSpec((1, 8, tq), lambda p, i: (p, 0, i))),
        name="fox_attn_fwd", compiler_params=_cp(("parallel", "parallel")))(q, kaug, vtr, cumt)


def _mm_res_final(a, w, gate, x, gain, target):
    s, k = a.shape
    d = w.shape[1]
    ts = _tile(s, 512)

    def body(a_ref, w_ref, gt_ref, x_ref, g_ref, t_ref, lsum_ref, dx_ref, dg_ref):
        @pl.when(pl.program_id(0) == 0)
        def _():
            lsum_ref[...] = jnp.zeros_like(lsum_ref)
            dg_ref[...] = jnp.zeros_like(dg_ref)

        xv = x_ref[...] + gt_ref[...] * _dot(a_ref[...], w_ref[...])
        xhat, rstd = _rms_parts(xv)
        e = xhat * g_ref[...] - t_ref[...]
        lsum_ref[...] += _colsum(e * e)
        dout = e * (1.0 / d)
        dg_ref[...] += _colsum(dout * xhat)
        dxhat = dout * g_ref[...]
        dx_ref[...] = rstd * (dxhat - xhat * jnp.mean(dxhat * xhat, axis=-1, keepdims=True))

    vec = pl.BlockSpec((1, d), lambda i: (0, 0))
    row = pl.BlockSpec((ts, d), lambda i: (i, 0))
    return pl.pallas_call(
        body, grid=(s // ts,), out_shape=(_sds((1, d), F32), _sds((s, d), F32), _sds((1, d), F32)),
        in_specs=[pl.BlockSpec((ts, k), lambda i: (i, 0)), _resident((k, d)), vec, row, vec, row], out_specs=(vec, row, vec),
        name="ffn1_out_final_loss", compiler_params=_cp(("arbitrary",)))(a, w, gate, x, gain, target)


def _ffn_bwd_act(dx, gate, w_out, ug, uu, name, comm=None):
    s, d = dx.shape
    f = w_out.shape[0]
    n = f // 2
    ts = _tile(s, 512)

    def body(dx_ref, g_ref, w_ref, ug_ref, uu_ref, dug_ref, duu_ref):
        dy = (dx_ref[...] * g_ref[...]).astype(BF16)
        for q in range(2):
            cs = pl.ds(q * n, n)
            dact = _dot_nt(dy, w_ref[cs, :])
            g = ug_ref[:, cs].astype(F32)
            u = uu_ref[:, cs].astype(F32)
            sg = _sigmoid(g)
            dug_ref[:, cs] = (dact * u * sg * (1.0 + g * (1.0 - sg))).astype(BF16)
            duu_ref[:, cs] = (dact * g * sg).astype(BF16)

    tile = pl.BlockSpec((ts, f), lambda i: (i, 0))
    return _hosted_call(
        body, comm, grid=(s // ts,), out_shape=(_sds((s, f), BF16), _sds((s, f), BF16)),
        in_specs=[pl.BlockSpec((ts, d), lambda i: (i, 0)), pl.BlockSpec((1, d), lambda i: (0, 0)),
                  _resident(w_out.shape), tile, tile],
        out_specs=(tile, tile), scratch_shapes=[], name=name, sem=("parallel",), args=(dx, gate, w_out, ug, uu))


def _dw_mm(a, b_list, tk, tn, name, gate=None, wfull=None, dgate_init=None):
    s, kdim = a.shape
    nb1 = b_list[0].shape[1] // tn
    nb = nb1 * len(b_list)
    ts = _tile(s, 1024)
    nk = kdim // tk
    ns = s // ts
    gated = gate is not None

    def body(*refs):
        a_ref = refs[0]
        b_refs = refs[1:1 + len(b_list)]
        rest = refs[1 + len(b_list):]
        if gated:
            g_ref, w_ref, di_ref, o_ref, dg_ref, acc = rest
        else:
            o_ref, acc = rest
        jn, ik, st = pl.program_id(0), pl.program_id(1), pl.program_id(2)

        @pl.when(st == 0)
        def _():
            acc[...] = jnp.zeros_like(acc)

        for mi, b_ref in enumerate(b_refs):
            @pl.when(jn // nb1 == mi)
            def _(b_ref=b_ref):
                acc[...] += _dot_tn(a_ref[...], b_ref[...].astype(BF16))

        if gated:
            @pl.when(jnp.logical_and(ik == 0, st == 0))
            def _():
                dg_ref[...] = di_ref[...]

        @pl.when(st == ns - 1)
        def _():
            if gated:
                o_ref[0] = acc[...] * g_ref[...]
                dg_ref[...] += _colsum(acc[...] * w_ref[...].astype(F32))
            else:
                o_ref[0] = acc[...]

    in_specs = [pl.BlockSpec((ts, tk), lambda jn, ik, st: (st, ik))]
    for mi in range(len(b_list)):
        in_specs.append(pl.BlockSpec(
            (ts, tn), lambda jn, ik, st, mi=mi: (st, jnp.clip(jn - mi * nb1, 0, nb1 - 1))))
    args = [a] + list(b_list)
    out_shape = [_sds((nb, kdim, tn), F32)]
    out_specs = [pl.BlockSpec((1, tk, tn), lambda jn, ik, st: (jn, ik, 0))]
    if gated:
        vec = pl.BlockSpec((1, tn), lambda jn, ik, st: (0, jn))
        in_specs += [vec, pl.BlockSpec((tk, tn), lambda jn, ik, st: (ik, jn)), vec]
        args += [gate, wfull, dgate_init]
        out_shape.append(_sds((1, nb * tn), F32))
        out_specs.append(vec)
    res = pl.pallas_call(
        body, grid=(nb, nk, ns), out_shape=tuple(out_shape), in_specs=in_specs, out_specs=tuple(out_specs),
        scratch_shapes=[pltpu.VMEM((tk, tn), F32)],
        name=name, compiler_params=_cp(("parallel", "arbitrary", "arbitrary")))(*args)
    return res if gated else res[0]


def _mm_normbwd(terms, x, dxres, gain, scale, name, ts_pref=256, comm=None, second=None):
    s, d = x.shape
    ts = _tile(s, ts_pref)
    sets = [(terms, gain, scale)] + ([second] if second is not None else [])
    arrs, warrs = [], []
    for tms, _, _ in sets:
        for a, _, w, _ in tms:
            if not any(a is z for z in arrs):
                arrs.append(a)
            if not any(w is z for z in warrs):
                warrs.append(w)
    na, nw, ns_ = len(arrs), len(warrs), len(sets)

    def body(*refs):
        a_refs, w_refs = refs[:na], refs[na:na + nw]
        x_ref, dr_ref = refs[na + nw], refs[na + nw + 1]
        par = refs[na + nw + 2:na + nw + 2 + 2 * ns_]
        dx_ref = refs[na + nw + 2 + 2 * ns_]
        sums = refs[na + nw + 3 + 2 * ns_:]

        @pl.when(pl.program_id(0) == 0)
        def _():
            for r in sums:
                r[...] = jnp.zeros_like(r)

        xhat, rstd = _rms_parts(x_ref[...])
        dxhat = None
        for k, (tms, _, _) in enumerate(sets):
            g_ref, sc_ref = par[2 * k], par[2 * k + 1]
            dsh_ref, dsc_ref, dg_ref = sums[3 * k:3 * k + 3]
            dh = None
            for a, c0, w, q in tms:
                ai = next(i for i, z in enumerate(arrs) if z is a)
                wi = next(i for i, z in enumerate(warrs) if z is w)
                part = _dot_nt(a_refs[ai][:, pl.ds(c0, w.shape[2])], w_refs[wi][q])
                dh = part if dh is None else dh + part
            dsh_ref[...] += _colsum(dh)
            dsc_ref[...] += _colsum(dh * (xhat * g_ref[...]))
            dn = dh * (1.0 + sc_ref[...])
            dg_ref[...] += _colsum(dn * xhat)
            dxh = dn * g_ref[...]
            dxhat = dxh if dxhat is None else dxhat + dxh
        dx_ref[...] = dr_ref[...] + rstd * (dxhat - xhat * jnp.mean(dxhat * xhat, axis=-1, keepdims=True))

    vec = pl.BlockSpec((1, d), lambda i: (0, 0))
    row = pl.BlockSpec((ts, d), lambda i: (i, 0))
    in_specs = [pl.BlockSpec((ts, a.shape[1]), lambda i: (i, 0)) for a in arrs]
    in_specs += [_resident(w.shape) for w in warrs]
    in_specs += [row, row] + [vec] * (2 * ns_)
    par_args = [p_ for _, g_, s_ in sets for p_ in (g_, s_)]
    return _hosted_call(
        body, comm, grid=(s // ts,), out_shape=(_sds((s, d), F32),) + tuple(_sds((1, d), F32) for _ in range(3 * ns_)),
        in_specs=in_specs, out_specs=(row,) + tuple(vec for _ in range(3 * ns_)), scratch_shapes=[],
        name=name, sem=("arbitrary",), args=(*arrs, *warrs, x, dxres, *par_args))


def _do_kernel(dx, gate, wo, o, comm=None):
    s, d = dx.shape
    ts = _tile(s, 512)

    def body(dx_ref, g_ref, w_ref, o_ref, do_ref, dl_ref):
        dy = (dx_ref[...] * g_ref[...]).astype(BF16)
        do = _dot_nt(dy, w_ref[...])
        do_ref[...] = do.astype(BF16)
        prod = do * o_ref[...].astype(F32)
        hrow = lax.broadcasted_iota(jnp.int32, (N_HEADS, d), 0)
        hcol = lax.broadcasted_iota(jnp.int32, (N_HEADS, d), 1) // HEAD_DIM
        sel = (hrow == hcol).astype(F32)
        dl_ref[...] = lax.dot_general(sel, prod, (((1,), (1,)), ((), ())), preferred_element_type=F32,
                                      precision=lax.Precision.HIGHEST)

    row = pl.BlockSpec((ts, d), lambda i: (i, 0))
    return _hosted_call(
        body, comm, grid=(s // ts,), out_shape=(_sds((s, d), BF16), _sds((N_HEADS, s), F32)),
        in_specs=[row, pl.BlockSpec((1, d), lambda i: (0, 0)), _resident(wo.shape), row],
        out_specs=(row, pl.BlockSpec((N_HEADS, ts), lambda i: (0, i))), scratch_shapes=[],
        name="attn_do", sem=("parallel",), args=(dx, gate, wo, o))


def _attn_bwd(q, do, kaug, kaugt, v, cumt, lse, deltat, comm=None):
    s, d = q.shape
    tq = _tile(s, ATT_TQ)
    tk = _tile(s, ATT_TK)
    assert tq in (tk, 2 * tk)
    npair = d // LANE
    nq = s // tq
    nkb = s // tk
    qscale = HEAD_DIM ** -0.5

    def body(q_ref, do_ref, ka_ref, kt_ref, v_ref, cumt_ref, lse_ref, dl_ref,
             dq_ref, dk_ref, dv_ref, dcq_ref, dck_ref, qaug, dom, rowv, dqt):
        p = pl.program_id(0)
        j = pl.program_id(1)
        lane = lax.broadcasted_iota(jnp.int32, (1, LANE), 1)
        lo = lane < HEAD_DIM
        r8 = lax.broadcasted_iota(jnp.int32, (8, 1), 0)

        @pl.when(j == 0)
        def _():
            dqt[...] = jnp.zeros_like(dqt)
            for c in range(nq):
                rows = pl.ds(c * tq, tq)
                qq = q_ref[rows, :]
                dd = do_ref[rows, :]
                cqt = cumt_ref[:, rows]
                dlt = dl_ref[:, rows]
                lst = lse_ref[0, :, rows]
                for hh in range(2):
                    qaug[hh, rows, :] = _q_aug(qq, lane, hh)
                    dom[hh, rows, :] = jnp.where(_head_mask(lane, hh), dd, jnp.zeros_like(dd))
                    rowv[hh, :, rows] = jnp.where(
                        r8 == 0, _pick_row(cqt, 2 * p + hh) - lst[hh:hh + 1, :],
                        jnp.where(r8 == 1, _pick_row(dlt, 2 * p + hh), 0.0))

        vv = v_ref[...]
        i0 = (j * tk) // tq

        def q_step(qs, nq_, carry, diag):
            dv_acc, dk0, dk1 = carry
            qs = pl.multiple_of(qs, tk)
            if diag:
                krow = lax.broadcasted_iota(jnp.int32, (tk, nq_), 0) + j * tk
                qcol = lax.broadcasted_iota(jnp.int32, (tk, nq_), 1) + qs
                causal = krow <= qcol
            dks = [dk0, dk1]
            for hh in range(2):
                rv = rowv[hh, :, pl.ds(qs, nq_)]
                qa = qaug[hh, pl.ds(qs, nq_), :]
                dh = dom[hh, pl.ds(qs, nq_), :]
                sc = _dot_nt(ka_ref[0, hh], qa)
                if diag:
                    sc = jnp.where(causal, sc, -jnp.inf)
                pt = jnp.exp(sc + rv[0:1, :])
                dpt = _dot_nt(vv, dh)
                dst = (pt * (dpt - rv[1:2, :])).astype(BF16)
                dv_acc = dv_acc + _dot(pt.astype(BF16), dh)
                dks[hh] = dks[hh] + _dot(dst, qa)
                dqt[hh, :, pl.ds(qs, nq_)] += _dot(kt_ref[0, hh], dst)
            return dv_acc, dks[0], dks[1]

        z = jnp.zeros((tk, LANE), F32)
        first = ((j * tk) % tq == 0).astype(jnp.int32)
        carry = lax.fori_loop(0, first, lambda _, cr: q_step(i0 * tq, tq, cr, True), (z, z, z))
        if tq > tk:
            carry = lax.fori_loop(0, 1 - first, lambda _, cr: q_step(j * tk, tq - tk, cr, True), carry)
        dv_acc, dk0, dk1 = lax.fori_loop(i0 + 1, nq, lambda i, cr: q_step(i * tq, tq, cr, False), carry)
        dv_ref[...] = dv_acc.astype(BF16)
        dk_ref[...] = jnp.where(lo, dk0, dk1).astype(BF16)
        dck_ref[0] = jnp.where(r8 == 0, dk0.T[SPARE[0]:SPARE[0] + 1, :],
                               jnp.where(r8 == 1, dk1.T[SPARE[1]:SPARE[1] + 1, :], 0.0))

        @pl.when(j == nkb - 1)
        def _():
            for c in range(nq):
                rows = pl.ds(c * tq, tq)
                a0 = dqt[0, :, rows].T
                a1 = dqt[1, :, rows].T
                dq_ref[rows, :] = (jnp.where(lo, a0, a1) * qscale).astype(BF16)
            r0, r1 = SPARE[0] + NPIECE, SPARE[1] + NPIECE
            dcq_ref[0] = jnp.where(r8 == 0, dqt[0, r0:r0 + 1, :], jnp.where(r8 == 1, dqt[1, r1:r1 + 1, :], 0.0))

    col = pl.BlockSpec((s, LANE), lambda p, j: (0, p), pipeline_mode=pl.Buffered(1))
    rows16 = pl.BlockSpec((N_HEADS, s), lambda p, j: (0, 0), pipeline_mode=pl.Buffered(1))
    blk = pl.BlockSpec((tk, LANE), lambda p, j: (j, p))
    return _hosted_call(
        body, comm, grid=(npair, nkb),
        out_shape=(_sds((s, d), BF16), _sds((s, d), BF16), _sds((s, d), BF16), _sds((npair, 8, s), F32), _sds((npair, 8, s), F32)),
        in_specs=[col, col, pl.BlockSpec((1, 2, tk, LANE), lambda p, j: (p, 0, j, 0)),
                  pl.BlockSpec((1, 2, LANE, tk), lambda p, j: (p, 0, 0, j)), blk, rows16,
                  pl.BlockSpec((1, 8, s), lambda p, j: (p, 0, 0), pipeline_mode=pl.Buffered(1)), rows16],
        out_specs=(pl.BlockSpec((s, LANE), lambda p, j: (0, p)), blk, blk,
                   pl.BlockSpec((1, 8, s), lambda p, j: (p, 0, 0)), pl.BlockSpec((1, 8, tk), lambda p, j: (p, 0, j))),
        scratch_shapes=[pltpu.VMEM((2, s, LANE), BF16), pltpu.VMEM((2, s, LANE), BF16), pltpu.VMEM((2, 8, s), F32),
                        pltpu.VMEM((2, LANE, s), F32)],
        name="fox_attn_bwd", sem=("arbitrary", "arbitrary"), vmem_mb=ATT_BWD_VMEM_MB,
        args=(q, do, kaug, kaugt, v, cumt, lse, deltat))


def _cumsum_bwd(dcq, dck, flog, fb):
    s = flog.shape[0]
    ts = _tile(s, 256)
    nt = s // ts

    def body(dq_ref, dk_ref, f_ref, b_ref, df_ref, db_ref, carry):
        @pl.when(pl.program_id(0) == 0)
        def _():
            carry[...] = jnp.zeros_like(carry)
            db_ref[...] = jnp.zeros_like(db_ref)

        r = lax.broadcasted_iota(jnp.int32, (ts, ts), 0)
        cidx = lax.broadcasted_iota(jnp.int32, (ts, ts), 1)
        tri = (r >= cidx).astype(F32)
        dct = dq_ref[...] + dk_ref[...]
        dlst = jnp.dot(dct, tri, preferred_element_type=F32, precision=lax.Precision.HIGHEST) + carry[...]
        carry[...] = dlst[:, 0:1]
        dls = jnp.concatenate([dlst, jnp.zeros((LANE - N_HEADS, ts), F32)], axis=0).T
        z = f_ref[...] + b_ref[...]
        df = dls * (1.0 / (1.0 + jnp.exp(z)))
        db_ref[...] += _colsum(df)
        df_ref[...] = df.astype(BF16)

    rev = pl.BlockSpec((ts, LANE), lambda i: (nt - 1 - i, 0))
    revt = pl.BlockSpec((N_HEADS, ts), lambda i: (0, nt - 1 - i))
    vec = pl.BlockSpec((1, LANE), lambda i: (0, 0))
    return pl.pallas_call(
        body, grid=(nt,), out_shape=(_sds((s, LANE), BF16), _sds((1, LANE), F32)),
        in_specs=[revt, revt, rev, vec], out_specs=(rev, vec), scratch_shapes=[pltpu.VMEM((N_HEADS, 1), F32)],
        name="forget_cumsum_bwd", compiler_params=_cp(("arbitrary",)))(dcq, dck, flog, fb)


def _conv_bwd1(dx, gate, w_out, b_out, dwo, lng, lnb, dgate_mm, comm=None):
    s, d = dx.shape
    ts = _tile(s, 512)
    ns = s // ts

    def body(dx_ref, g_ref, w_ref, bo_ref, y_ref, lg_ref, lb_ref, dgm_ref, dd_ref, dlg_ref, dlb_ref, dbd_ref, dbo_ref, dge_ref,
             cs):
        i = pl.program_id(0)

        @pl.when(i == 0)
        def _():
            for r in (dlg_ref, dlb_ref, dbd_ref, cs):
                r[...] = jnp.zeros_like(r)

        dxv = dx_ref[...]
        cs[...] += _colsum(dxv)
        dsw = _dot_nt((dxv * g_ref[...]).astype(BF16), w_ref[...])
        yv = y_ref[...]
        mu = jnp.mean(yv, axis=-1, keepdims=True)
        yc = yv - mu
        rstd = lax.rsqrt(jnp.mean(yc * yc, axis=-1, keepdims=True) + EPS)
        xhat = yc * rstd
        ln = xhat * lg_ref[...] + lb_ref[...]
        sg = _sigmoid(ln)
        dln = dsw * (sg * (1.0 + ln * (1.0 - sg)))
        dlg_ref[...] += _colsum(dln * xhat)
        dlb_ref[...] += _colsum(dln)
        dxh = dln * lg_ref[...]
        dd = rstd * (dxh - jnp.mean(dxh, axis=-1, keepdims=True) - xhat * jnp.mean(dxh * xhat, axis=-1, keepdims=True))
        dbd_ref[...] += _colsum(dd)
        dd_ref[...] = dd

        @pl.when(i == ns - 1)
        def _():
            dbo_ref[...] = g_ref[...] * cs[...]
            dge_ref[...] = dgm_ref[...] + bo_ref[...] * cs[...]

    vec = pl.BlockSpec((1, d), lambda i: (0, 0))
    row = pl.BlockSpec((ts, d), lambda i: (i, 0))
    return _hosted_call(
        body, comm, grid=(ns,), out_shape=(_sds((s, d), F32),) + tuple(_sds((1, d), F32) for _ in range(5)),
        in_specs=[row, vec, _resident(w_out.shape), vec, row, vec, vec, vec], out_specs=(row, vec, vec, vec, vec, vec),
        scratch_shapes=[pltpu.VMEM((1, d), F32)],
        name="conv_bwd_ln", sem=("arbitrary",), args=(dx, gate, w_out, b_out, dwo, lng, lnb, dgate_mm))


def _dwconv_bwd(ddwo, glu, a_s, g_s, wdw, comm=None):
    s, d = ddwo.shape
    ts = _tile(s, 256)
    ns = s // ts
    rb, cb = 32, 256
    nrb = ts // rb

    def body(dd_ref, ddn_ref, gl_ref, glh_ref, a_ref, g_ref, w_ref, da_ref, dg_ref, dw_ref, sa_ref, sg_ref, bufd, bufg, dws,
             shd, shg):
        i = pl.program_id(0)

        @pl.when(i == 0)
        def _():
            dws[...] = jnp.zeros_like(dws)
            sa_ref[...] = jnp.zeros_like(sa_ref)
            sg_ref[...] = jnp.zeros_like(sg_ref)
            bufg[pl.ds(0, HALO), :] = jnp.zeros((HALO, d), F32)

        @pl.when(i > 0)
        def _():
            bufg[pl.ds(0, HALO), :] = glh_ref[...]

        bufg[pl.ds(HALO, ts), :] = gl_ref[...]
        bufd[pl.ds(0, ts), :] = dd_ref[...]

        @pl.when(i == ns - 1)
        def _():
            bufd[pl.ds(ts, HALO), :] = jnp.zeros((HALO, d), F32)

        @pl.when(i < ns - 1)
        def _():
            bufd[pl.ds(ts, HALO), :] = ddn_ref[...]

        _shift_copies(bufd, shd)
        _shift_copies(bufg, shg)
        for cc in range(d // cb):
            cs = pl.ds(cc * cb, cb)
            for r in range(nrb):
                acc = jnp.zeros((rb, cb), F32)
                for k in range(CONV_K):
                    acc = acc + w_ref[pl.ds(k, 1), cs] * _shifted(bufd, shd, r * rb + (CONV_K - 1) - k, rb, cs)
                rows = pl.ds(r * rb, rb)
                av = a_ref[rows, cs].astype(F32)
                sg = _sigmoid(g_ref[rows, cs].astype(F32))
                dav = acc * sg
                dgv = acc * av * sg * (1.0 - sg)
                da_ref[rows, cs] = dav.astype(BF16)
                dg_ref[rows, cs] = dgv.astype(BF16)
                sa_ref[:, cs] += _colsum(dav)
                sg_ref[:, cs] += _colsum(dgv)
            for k in range(CONV_K):
                acc8 = jnp.zeros((8, cb), F32)
                for r in range(nrb):
                    prod = bufd[pl.ds(r * rb, rb), cs] * _shifted(bufg, shg, HALO - (CONV_K - 1) + k + r * rb, rb, cs)
                    acc8 = acc8 + (prod[0:8] + prod[8:16]) + (prod[16:24] + prod[24:32])
                dws[pl.ds(8 * k, 8), cs] += acc8

        @pl.when(i == ns - 1)
        def _():
            dw_ref[...] = jnp.zeros_like(dw_ref)
            for k in range(CONV_K):
                dw_ref[pl.ds(k, 1), :] = _colsum(dws[pl.ds(8 * k, 8), :])

    row = pl.BlockSpec((ts, d), lambda i: (i, 0))
    vec = pl.BlockSpec((1, d), lambda i: (0, 0))
    hb = ts // HALO
    return _hosted_call(
        body, comm, grid=(ns,),
        out_shape=(_sds((s, d), BF16), _sds((s, d), BF16), _sds((HALO, d), F32), _sds((1, d), F32), _sds((1, d), F32)),
        in_specs=[row, pl.BlockSpec((HALO, d), lambda i: (jnp.minimum((i + 1) * hb, ns * hb - 1), 0)),
                  row, pl.BlockSpec((HALO, d), lambda i: (jnp.maximum(i * hb - 1, 0), 0)),
                  row, row, pl.BlockSpec((HALO, d), lambda i: (0, 0))],
        out_specs=(row, row, pl.BlockSpec((HALO, d), lambda i: (0, 0)), vec, vec),
        scratch_shapes=[pltpu.VMEM((ts + HALO, d), F32), pltpu.VMEM((HALO + ts, d), F32), pltpu.VMEM((8 * HALO, d), F32),
                        pltpu.VMEM((SUBLANES - 1, HALO + ts - SUBLANES, d), F32),
                        pltpu.VMEM((SUBLANES - 1, HALO + ts - SUBLANES, d), F32)],
        name="dwconv_bwd", sem=("arbitrary",), args=(ddwo, ddwo, glu, glu, a_s, g_s, wdw))


def _ada_wgrad(cat, da, name):
    nl, _, n = da.shape
    d = cat.shape[0]
    tn = 256

    def body(c_ref, d_ref, o_ref):
        acc = c_ref[:, 0:1] * d_ref[0, 0:1, :]
        for r in range(1, 8):
            acc = acc + c_ref[:, r:r + 1] * d_ref[0, r:r + 1, :]
        o_ref[0] = acc

    return pl.pallas_call(
        body, grid=(nl, n // tn), out_shape=_sds((nl, d, n), F32),
        in_specs=[pl.BlockSpec((d, 8), lambda l, j: (0, 0)), pl.BlockSpec((1, 8, tn), lambda l, j: (l, 0, j))],
        out_specs=pl.BlockSpec((1, d, tn), lambda l, j: (l, 0, j)),
        name=name, compiler_params=_cp(("parallel", "parallel")))(cat, da)


def _silu_rows(c_all):
    def body(c_ref, o_ref):
        cc = c_ref[...]
        o_ref[...] = cc * _sigmoid(cc)

    return pl.pallas_call(body, out_shape=_sds(c_all.shape, F32), name="silu_c")(c_all)


def _adamw(w, g, m, v, name):
    r, c = w.shape
    tr = r
    for cand in (512, 256, 128, 64, 32, 16, 8):
        if r % cand == 0 and cand * c * 4 <= (1 << 20):
            tr = cand
            break
    bc1 = 1.0 - ADAM_B1 ** ADAM_STEP
    bc2 = 1.0 - ADAM_B2 ** ADAM_STEP

    def body(w_ref, g_ref, m_ref, v_ref, d_ref, nm_ref, nv_ref):
        gv = g_ref[...]
        mn = ADAM_B1 * m_ref[...] + (1.0 - ADAM_B1) * gv
        vn = ADAM_B2 * v_ref[...] + (1.0 - ADAM_B2) * (gv * gv)
        mh = mn / bc1
        vh = vn / bc2
        d_ref[...] = -ADAM_LR * (mh / (jnp.sqrt(vh) + ADAM_EPS) + ADAM_WD * w_ref[...])
        nm_ref[...] = mn
        nv_ref[...] = vn

    blk = pl.BlockSpec((tr, c), lambda i: (i, 0))
    return pl.pallas_call(
        body, grid=(r // tr,), out_shape=tuple(_sds((r, c), F32) for _ in range(3)),
        in_specs=[blk, blk, blk, blk], out_specs=(blk, blk, blk),
        name=name, compiler_params=_cp(("parallel",)))(w, g, m, v)


def _pad_rows(a, rows, axis):
    pad = [(0, 0)] * a.ndim
    pad[axis] = (0, rows - a.shape[axis])
    return jnp.pad(a, pad)


def _vec(a):
    return a.reshape(1, -1)


def kernel(x, c, mix_norm_g, mix_ada_w, mix_ada_b, ffn_norm_g, ffn_ada_w, ffn_ada_b, ffn_w_in, ffn_w_out, conv_w_in, conv_b_in, conv_w_dw, conv_b_dw, conv_ln_g, conv_ln_b, conv_w_out, conv_b_out, kv_norm_g, kv_ada_w, kv_ada_b, kv_w, forget_b, attn_w_q, attn_w_o, final_norm_g, loss_target, m_mix_norm_g, m_mix_ada_w, m_mix_ada_b, m_ffn_norm_g, m_ffn_ada_w, m_ffn_ada_b, m_ffn_w_in, m_ffn_w_out, m_conv_w_in, m_conv_b_in, m_conv_w_dw, m_conv_b_dw, m_conv_ln_g, m_conv_ln_b, m_conv_w_out, m_conv_b_out, m_kv_norm_g, m_kv_ada_w, m_kv_ada_b, m_kv_w, m_forget_b, m_attn_w_q, m_attn_w_o, m_final_norm_g, v_mix_norm_g, v_mix_ada_w, v_mix_ada_b, v_ffn_norm_g, v_ffn_ada_w, v_ffn_ada_b, v_ffn_w_in, v_ffn_w_out, v_conv_w_in, v_conv_b_in, v_conv_w_dw, v_conv_b_dw, v_conv_ln_g, v_conv_ln_b, v_conv_w_out, v_conv_b_out, v_kv_norm_g, v_kv_ada_w, v_kv_ada_b, v_kv_w, v_forget_b, v_attn_w_q, v_attn_w_o, v_final_norm_g):
    xi, yi, ci = lax.axis_index("x"), lax.axis_index("y"), lax.axis_index("c")
    chip = 2 * xi + yi
    dev = 4 * xi + 2 * yi + ci
    s, d = x.shape[1], x.shape[2]
    f = ffn_w_out.shape[1] * 4
    x0 = x[0]
    nkv = kv_w.shape[1]
    nkv_all = 4 * nkv

    def my_halves(ws):
        return [lax.dynamic_index_in_dim(w.astype(BF16).reshape(2, w.shape[0] // 2, w.shape[1]), ci, axis=0, keepdims=False)
                for w in ws]

    def whole(gath, ws):
        return [g.reshape(4, w.shape[0], w.shape[1]) for g, w in zip(gath, ws)]

    wdw_loc = _pad_rows(conv_w_dw[0], HALO, 0)
    small = jnp.concatenate([c.reshape(-1), conv_b_in.reshape(-1), wdw_loc.reshape(-1), conv_b_dw.reshape(-1),
                             conv_ln_g.reshape(-1), conv_ln_b.reshape(-1), conv_b_out.reshape(-1)])
    n_small = small.shape[0]
    w_small = -(-n_small // (8 * LANE)) * LANE
    small = jnp.pad(small, (0, 8 * w_small - n_small)).reshape(8, w_small)
    first = [small] + my_halves([conv_w_in[0]])
    small_all, cw_in = _exchange(_Gather8(first), first, "ag_small_params_w_conv")
    small_all = small_all.reshape(8, 8 * w_small)
    cw_in, = whole([cw_in], [conv_w_in[0]])
    c_all = small_all[:, :d]
    per_chip = small_all[0::2]
    dq_ = d // 4
    o1 = d
    b_in_full = per_chip[:, o1:o1 + 2 * dq_].reshape(4, 1, 2 * dq_)
    o1 += 2 * dq_
    wdw_full = per_chip[:, o1:o1 + HALO * dq_].reshape(4, HALO, dq_).transpose(1, 0, 2).reshape(HALO, d)
    o1 += HALO * dq_
    bdw_full = per_chip[:, o1:o1 + dq_].reshape(1, d)
    lng_full = per_chip[:, o1 + dq_:o1 + 2 * dq_].reshape(1, d)
    lnb_full = per_chip[:, o1 + 2 * dq_:o1 + 3 * dq_].reshape(1, d)
    bout_full = per_chip[:, o1 + 3 * dq_:o1 + 4 * dq_].reshape(1, d)

    a_mix = _ada_fwd(c_all, mix_ada_w, "ada_mix")
    a_ffn = _ada_fwd(c_all, ffn_ada_w, "ada_ffn")
    a_kv = _ada_fwd(c_all, kv_ada_w[None], "ada_kv")
    n3 = mix_ada_w.shape[2]
    n2 = kv_ada_w.shape[1]
    ada_loc = jnp.concatenate([a_mix[0], a_mix[1], a_ffn[0], a_ffn[1], a_kv[0]], axis=1)
    w_ada = ada_loc.shape[1]
    ada_all = _allgather8(ada_loc, "ag_ada", True).reshape(8, 8, w_ada)
    ada_me = lax.dynamic_index_in_dim(ada_all, dev, axis=1, keepdims=False)[0::2]

    def ada_vec(off, n, bias):
        return ada_me[:, off:off + n].reshape(1, 4 * n) + bias.reshape(1, -1)

    ada_m0 = ada_vec(0, n3, mix_ada_b[0])
    ada_m1 = ada_vec(n3, n3, mix_ada_b[1])
    ada_f0 = ada_vec(2 * n3, n3, ffn_ada_b[0])
    ada_f1 = ada_vec(3 * n3, n3, ffn_ada_b[1])
    ada_k = ada_vec(4 * n3, n2, kv_ada_b)

    def split3(a):
        return a[:, :d], a[:, d:2 * d], a[:, 2 * d:3 * d]

    sh_m0, sc_m0, gt_m0 = split3(ada_m0)
    sh_m1, sc_m1, gt_m1 = split3(ada_m1)
    sh_f0, sc_f0, gt_f0 = split3(ada_f0)
    sh_f1, sc_f1, gt_f1 = split3(ada_f1)
    sh_k, sc_k = ada_k[:, :d], ada_k[:, d:2 * d]

    grp_a = [ffn_w_in[0]]
    grp_b = [ffn_w_out[0], conv_w_out[0], ffn_w_in[1]]
    grp_c = [ffn_w_out[1], kv_w, attn_w_q[0], attn_w_o[0]]
    mine_a, mine_b, mine_c = my_halves(grp_a), my_halves(grp_b), my_halves(grp_c)

    zero_b = jnp.zeros((1, d), F32)
    g_m0, g_m1 = _vec(mix_norm_g[0]), _vec(mix_norm_g[1])
    g_f0, g_f1 = _vec(ffn_norm_g[0]), _vec(ffn_norm_g[1])
    g_k, g_fin = _vec(kv_norm_g), _vec(final_norm_g)
    fb = jnp.pad(forget_b, (0, LANE - N_HEADS)).reshape(1, LANE)

    h0, glu, a_s, g_s, gath_a = _in_pair(x0, g_m0, sh_m0, sc_m0, cw_in, b_in_full, True, "conv_in",
                                         comm=(_Gather8(mine_a), mine_a))
    dwo, sw, gath_b = _dwconv_fwd(glu, wdw_full, bdw_full, lng_full, lnb_full, comm=(_Gather8(mine_b), mine_b))
    w_in0, = whole(gath_a, grp_a)
    w_out0, cw_out, w_in1 = whole(gath_b, grp_b)
    cw_out = cw_out.reshape(d, d)
    w_in = [w_in0, w_in1]
    x1 = _mm_res(sw, cw_out, bout_full, gt_m0, x0, "conv_out")
    hf0, act0, ug0, uu0, gath_c = _in_pair(x1, g_f0, sh_f0, sc_f0, w_in[0], None, False, "ffn0_in",
                                           comm=(_Gather8(mine_c), mine_c))
    w_out1, kvw, wq, wo = whole(gath_c, grp_c)
    w_out = [w_out0.reshape(f, d), w_out1.reshape(f, d)]
    kvw = kvw.transpose(1, 0, 2).reshape(d, nkv_all)
    wk, wv = kvw[:, :d], kvw[:, d:2 * d]
    wf = jnp.pad(kvw[:, 2 * d:], ((0, 0), (0, LANE - N_HEADS)))
    wq, wo = wq.reshape(d, d), wo.reshape(d, d)
    x2 = _mm_res(act0, w_out[0], zero_b, gt_f0, x1, "ffn0_out")
    hk, h1, kk, vv, qq, flog = _qkv(x2, (g_k, sh_k, sc_k), (g_m1, sh_m1, sc_m1), wk, wv, wf, wq)
    cum, cumt = _cumsum_fwd(flog, fb)
    kaug, kaugt, vtr = _attn_prep(kk, vv, cum)
    o, lse = _attn_fwd(qq, kaug, vtr, cumt)
    x3 = _mm_res(o, wo, zero_b, gt_m1, x2, "attn_out")
    hf1, act1, ug1, uu1 = _in_pair(x3, g_f1, sh_f1, sc_f1, w_in[1], None, False, "ffn1_in")
    lsum, dx4, d_gfin = _mm_res_final(act1, w_out[1], gt_f1, x3, g_fin, loss_target[0])
    loss = lax.psum(0.5 / d * jnp.sum(lsum), ("x", "y", "c"))

    nf = f // 2

    sel = jnp.stack([ci, chip]).astype(jnp.int32)

    def halves_of(gs):
        return [g.reshape(4, 2, g.shape[1] // 2, g.shape[2]) for g in gs]

    def reduce_adds(ps, lands, tag):
        pairs = [_add_halves(p_, l_, sel, f"{tag}_add{k}") for k, (p_, l_) in enumerate(zip(ps, lands))]
        return [q for q, _ in pairs], [o_ for _, o_ in pairs]

    def reduce_begin(gs, tag):
        ps = halves_of(gs)
        return reduce_adds(ps, _exchange(_SwapHalves(ps), ps, tag + "_swap"), tag)

    def reduce_sum(owns, lands, tag):
        return [_add_chips(o_, l_, sel, f"{tag}_sum{k}") for k, (o_, l_) in enumerate(zip(owns, lands))]

    def ffn_bwd(dx_out, x_in, hf, act, ug, uu, gain, scale, gate, w_in_l, w_out_l, tag, comm=None):
        res = _ffn_bwd_act(dx_out, gate, w_out_l, ug, uu, tag + "_bwd_act", comm=comm)
        dug, duu = res[0], res[1]
        dw_out, dgate = _dw_mm(act, [dx_out], nf, d, tag + "_dw_out", gate=gate, wfull=w_out_l, dgate_init=zero_b)
        terms = [(dug, 0, w_in_l, 0), (dug, nf, w_in_l, 1), (duu, 0, w_in_l, 2), (duu, nf, w_in_l, 3)]
        dx_in, dsh, dsc, dgn = _mm_normbwd(terms, x_in, dx_out, gain, scale, tag + "_bwd_in")
        dw_in = _dw_mm(hf, [dug, duu], d, nf, tag + "_dw_in")
        return dx_in, dw_in, dw_out[0], dsh, dsc, dgate, dgn, (res[2] if comm is not None else None)

    dx3, dw_in1, dw_out1, dsh_f1, dsc_f1, dgt_f1, dgn_f1, _ = ffn_bwd(dx4, x3, hf1, act1, ug1, uu1, g_f1, sc_f1, gt_f1, w_in[1], w_out[1], "ffn1")
    ps_1 = halves_of([dw_in1, dw_out1.reshape(4, f // 4, d)])
    do, deltat, swapped_1 = _do_kernel(dx3, gt_m1, wo, o, comm=(_SwapHalves(ps_1), ps_1))
    q16_1, own_1 = reduce_adds(ps_1, swapped_1, "rs_ffn1")
    dwo_att, dgt_m1 = _dw_mm(o, [dx3], d, d, "attn_dw_o", gate=gt_m1, wfull=wo, dgate_init=zero_b)
    dq, dk, dv, dcq, dck, land_1 = _attn_bwd(qq, do, kaug, kaugt, vv, cumt, lse, deltat, comm=(_ScatterChips(q16_1), q16_1))
    dwq = _dw_mm(h1, [dq], d, d, "attn_dw_q")[0]

    df, dfb = _cumsum_bwd(dcq[:, :2].reshape(N_HEADS, s), dck[:, :2].reshape(N_HEADS, s), flog, fb)
    dwk, dwv = _dw_mm(hk, [dk, dv], d, d, "kv_dw_kv")
    dwf = _dw_mm(hk, [df], d, LANE, "kv_dw_f")[0]
    dkvw = jnp.concatenate([dwk, dwv, dwf[:, :N_HEADS]], axis=1)
    dkvw = dkvw.reshape(d, 4, nkv).transpose(1, 0, 2)
    ps_2 = halves_of([dkvw, dwq.reshape(4, d // 4, d), dwo_att[0].reshape(4, d // 4, d)])
    terms = [(dk, 0, wk.reshape(1, d, d), 0), (dv, 0, wv.reshape(1, d, d), 0), (df, 0, wf.reshape(1, d, LANE), 0)]
    dx2, dsh_m1, dsc_m1, dgn_m1, dsh_k, dsc_k, dgn_k, swapped_2 = _mm_normbwd(
        [(dq, 0, wq.reshape(1, d, d), 0)], x2, dx3, g_m1, sc_m1, "attn_kv_bwd", second=(terms, g_k, sc_k),
        comm=(_SwapHalves(ps_2), ps_2))
    q16_2, own_2 = reduce_adds(ps_2, swapped_2, "rs_attn")
    dx1, dw_in0, dw_out0, dsh_f0, dsc_f0, dgt_f0, dgn_f0, land_2 = ffn_bwd(
        dx2, x1, hf0, act0, ug0, uu0, g_f0, sc_f0, gt_f0, w_in[0], w_out[0], "ffn0", comm=(_ScatterChips(q16_2), q16_2))

    dcw_out, dgt_m0_mm = _dw_mm(sw, [dx1], d, d, "conv_dw_out", gate=gt_m0, wfull=cw_out, dgate_init=zero_b)
    ps_3 = halves_of([dw_in0, dw_out0.reshape(4, f // 4, d), dcw_out[0].reshape(4, d // 4, d)])
    ddwo, d_lng, d_lnb, d_bdw, d_bout, dgt_m0, swapped_3 = _conv_bwd1(
        dx1, gt_m0, cw_out, bout_full, dwo, lng_full, lnb_full, dgt_m0_mm, comm=(_SwapHalves(ps_3), ps_3))
    q16_3, own_3 = reduce_adds(ps_3, swapped_3, "rs_ffn0")
    da, dg, d_wdw, d_bin_a, d_bin_g, land_3 = _dwconv_bwd(ddwo, glu, a_s, g_s, wdw_full, comm=(_ScatterChips(q16_3), q16_3))
    nc = cw_in.shape[2]
    terms = [(da, 0, cw_in, 0), (da, nc, cw_in, 1), (dg, 0, cw_in, 2), (dg, nc, cw_in, 3)]
    dcw_in = _dw_mm(h0, [da, dg], d, nc, "conv_dw_in")
    ps_4 = halves_of([dcw_in])
    dx0, dsh_m0, dsc_m0, dgn_m0, swapped_4 = _mm_normbwd(terms, x0, dx1, g_m0, sc_m0, "conv_bwd_in",
                                                         comm=(_SwapHalves(ps_4), ps_4))
    q16_4, own_4 = reduce_adds(ps_4, swapped_4, "rs_conv")
    land_4 = _exchange(_ScatterChips(q16_4), q16_4, "rs_conv_scatter")

    sums = (reduce_sum(own_1, land_1, "rs_ffn1") + reduce_sum(own_2, land_2, "rs_attn")
            + reduce_sum(own_3, land_3, "rs_ffn0") + reduce_sum(own_4, land_4, "rs_conv"))
    reduced = [b.reshape(2 * b.shape[1], b.shape[2]) for b in _share_halves(sums)]
    g_w_in1, g_w_out1, g_kvw, g_wq, g_wo, g_w_in0, g_w_out0, g_cw_out, g_cw_in = reduced

    d_ada = [jnp.concatenate([dsh_m0, dsc_m0, dgt_m0], axis=1), jnp.concatenate([dsh_m1, dsc_m1, dgt_m1], axis=1),
             jnp.concatenate([dsh_f0, dsc_f0, dgt_f0], axis=1), jnp.concatenate([dsh_f1, dsc_f1, dgt_f1], axis=1),
             jnp.concatenate([dsh_k, dsc_k], axis=1)]
    fields = d_ada + [dgn_m0, dgn_m1, dgn_f0, dgn_f1, dgn_k, d_gfin, d_bin_a, d_bin_g, d_bdw, d_lng, d_lnb, d_bout,
                      d_wdw.reshape(1, -1), dfb]
    foffs = [0]
    for fl in fields:
        foffs.append(foffs[-1] + fl.shape[1])
    n_row = foffs[-1]
    w_row = -(-n_row // (8 * LANE)) * LANE
    row = jnp.pad(jnp.concatenate(fields, axis=1), ((0, 0), (0, 8 * w_row - n_row))).reshape(8, w_row)
    rows_all = _allgather8(row, "ag_small_grads", True).reshape(8, 8, w_row)
    rsum_small = _sum8(rows_all).reshape(1, 8 * w_row)
    rows_flat = rows_all.reshape(8, 8 * w_row)

    def fsum(i):
        return rsum_small[:, foffs[i]:foffs[i + 1]]

    cat = _silu_rows(c_all).T

    def ada_cols(i, n):
        full = rows_flat[:, foffs[i]:foffs[i + 1]].reshape(8, 4, n)
        return lax.dynamic_index_in_dim(full, chip, axis=1, keepdims=False)

    g_mix_ada_w = _ada_wgrad(cat, jnp.stack([ada_cols(0, n3), ada_cols(1, n3)]), "ada_mix_wgrad")
    g_ffn_ada_w = _ada_wgrad(cat, jnp.stack([ada_cols(2, n3), ada_cols(3, n3)]), "ada_ffn_wgrad")
    g_kv_ada_w = _ada_wgrad(cat, ada_cols(4, n2)[None], "ada_kv_wgrad")[0]

    def my_cols(v, n):
        return lax.dynamic_index_in_dim(v.reshape(4, n), chip, axis=0, keepdims=False)

    grads = {
        "mix_norm_g": jnp.concatenate([fsum(5), fsum(6)], axis=0),
        "mix_ada_w": g_mix_ada_w,
        "mix_ada_b": jnp.concatenate([fsum(0), fsum(1)], axis=0),
        "ffn_norm_g": jnp.concatenate([fsum(7), fsum(8)], axis=0),
        "ffn_ada_w": g_ffn_ada_w,
        "ffn_ada_b": jnp.concatenate([fsum(2), fsum(3)], axis=0),
        "ffn_w_in": jnp.stack([g_w_in0, g_w_in1]),
        "ffn_w_out": jnp.stack([g_w_out0, g_w_out1]),
        "conv_w_in": g_cw_in[None],
        "conv_b_in": my_cols(jnp.concatenate([fsum(11), fsum(12)], axis=1), 2 * dq_)[None],
        "conv_w_dw": lax.dynamic_index_in_dim(fsum(17).reshape(HALO, 4, dq_), chip, axis=1, keepdims=False)[:CONV_K][None],
        "conv_b_dw": my_cols(fsum(13), dq_)[None],
        "conv_ln_g": my_cols(fsum(14), dq_)[None],
        "conv_ln_b": my_cols(fsum(15), dq_)[None],
        "conv_w_out": g_cw_out[None],
        "conv_b_out": my_cols(fsum(16), dq_)[None],
        "kv_norm_g": fsum(9).reshape(-1),
        "kv_ada_w": g_kv_ada_w,
        "kv_ada_b": fsum(4).reshape(-1),
        "kv_w": g_kvw,
        "forget_b": fsum(18).reshape(-1)[:N_HEADS],
        "attn_w_q": g_wq[None],
        "attn_w_o": g_wo[None],
        "final_norm_g": fsum(10).reshape(-1),
    }
    weights = dict(mix_norm_g=mix_norm_g, mix_ada_w=mix_ada_w, mix_ada_b=mix_ada_b, ffn_norm_g=ffn_norm_g, ffn_ada_w=ffn_ada_w, ffn_ada_b=ffn_ada_b, ffn_w_in=ffn_w_in, ffn_w_out=ffn_w_out, conv_w_in=conv_w_in, conv_b_in=conv_b_in, conv_w_dw=conv_w_dw, conv_b_dw=conv_b_dw, conv_ln_g=conv_ln_g, conv_ln_b=conv_ln_b, conv_w_out=conv_w_out, conv_b_out=conv_b_out, kv_norm_g=kv_norm_g, kv_ada_w=kv_ada_w, kv_ada_b=kv_ada_b, kv_w=kv_w, forget_b=forget_b, attn_w_q=attn_w_q, attn_w_o=attn_w_o, final_norm_g=final_norm_g)
    moms = dict(mix_norm_g=(m_mix_norm_g, v_mix_norm_g), mix_ada_w=(m_mix_ada_w, v_mix_ada_w), mix_ada_b=(m_mix_ada_b, v_mix_ada_b), ffn_norm_g=(m_ffn_norm_g, v_ffn_norm_g), ffn_ada_w=(m_ffn_ada_w, v_ffn_ada_w), ffn_ada_b=(m_ffn_ada_b, v_ffn_ada_b), ffn_w_in=(m_ffn_w_in, v_ffn_w_in), ffn_w_out=(m_ffn_w_out, v_ffn_w_out), conv_w_in=(m_conv_w_in, v_conv_w_in), conv_b_in=(m_conv_b_in, v_conv_b_in), conv_w_dw=(m_conv_w_dw, v_conv_w_dw), conv_b_dw=(m_conv_b_dw, v_conv_b_dw), conv_ln_g=(m_conv_ln_g, v_conv_ln_g), conv_ln_b=(m_conv_ln_b, v_conv_ln_b), conv_w_out=(m_conv_w_out, v_conv_w_out), conv_b_out=(m_conv_b_out, v_conv_b_out), kv_norm_g=(m_kv_norm_g, v_kv_norm_g), kv_ada_w=(m_kv_ada_w, v_kv_ada_w), kv_ada_b=(m_kv_ada_b, v_kv_ada_b), kv_w=(m_kv_w, v_kv_w), forget_b=(m_forget_b, v_forget_b), attn_w_q=(m_attn_w_q, v_attn_w_q), attn_w_o=(m_attn_w_o, v_attn_w_o), final_norm_g=(m_final_norm_g, v_final_norm_g))
    names = list(weights)

    deltas, new_m, new_v = {}, {}, {}
    small_names = [n for n in names if weights[n].size < (1 << 16)]
    for n in names:
        if n in small_names:
            continue
        w = weights[n]
        w2 = w.reshape(-1, w.shape[-1])
        dl, nm, nv = _adamw(w2, grads[n].reshape(w2.shape), moms[n][0].reshape(w2.shape), moms[n][1].reshape(w2.shape), "adamw_" + n)
        deltas[n], new_m[n], new_v[n] = dl.reshape(w.shape), nm.reshape(w.shape), nv.reshape(w.shape)

    def pack_small(get):
        flat = jnp.concatenate([get(n).reshape(-1) for n in small_names])
        rows_ = -(-flat.shape[0] // (8 * LANE)) * 8
        return jnp.pad(flat, (0, rows_ * LANE - flat.shape[0])).reshape(rows_, LANE)

    ws, gs = pack_small(lambda n: weights[n]), pack_small(lambda n: grads[n])
    ms_, vs_ = pack_small(lambda n: moms[n][0]), pack_small(lambda n: moms[n][1])
    vs_ = jnp.where(jnp.arange(vs_.size).reshape(vs_.shape) < sum(weights[n].size for n in small_names), vs_, 1.0)
    dl, nm, nv = _adamw(ws, gs, ms_, vs_, "adamw_small")
    off = 0
    for n in small_names:
        sz = weights[n].size
        shp = weights[n].shape
        deltas[n] = dl.reshape(-1)[off:off + sz].reshape(shp)
        new_m[n] = nm.reshape(-1)[off:off + sz].reshape(shp)
        new_v[n] = nv.reshape(-1)[off:off + sz].reshape(shp)
        off += sz

    grad_out = [grads[n].reshape(weights[n].shape) for n in names]
    return (loss, dx0[None], *grad_out, *[deltas[n] for n in names], *[new_m[n] for n in names], *[new_v[n] for n in names])
```
